```python
import math
import jax
import jax.numpy as jnp
from jax import lax
import numpy as np

D_MODEL = 2048
BATCH = 2
SEQ = 4096
DEPTH = 1

CTX_LEN = 256
GRID_W = 64

D_S5 = 1024
S5_GROUP = 16
S5_GROUPS = D_S5 // S5_GROUP
S5_STATE = 64
S5_DIRS = 2
LAMBDA_RE_MAX = -1e-4

D_HY = 1024
HY_ORDER = 2
HY_DIRS = 2
HY_SHORT = 3
HY_EMB = 33
HY_BANDS = (HY_EMB - 1) // 2
HY_FF = 64
HY_TARGET = 1e-2
HY_FAST_PCT = 0.3
HY_SLOW_PCT = 1.5

N_BRANCH = 2
I_HY = D_S5
I_GA = D_S5 + (HY_ORDER + 1) * D_HY
I_GB = I_GA + D_MODEL
D_IN = I_GB + D_MODEL
D_FF = 5632
N_SUB = 3
N_MOD = 3
HALF_STEP = 0.5
RMS_EPS = 1e-6

kernel_name = "hybrid_s5_hyena_macaron_dit_layer"


def _rmsnorm(x, g):
    xf = x.astype(jnp.float32)
    y = xf * lax.rsqrt(jnp.mean(xf * xf, axis=-1, keepdims=True) + RMS_EPS)
    return (y * g.astype(jnp.float32)).astype(x.dtype)


def _modulate(h, shift, scale):
    return h * (1.0 + scale) + shift


def _swiglu(h, w_gate, w_up, w_down):
    return (jax.nn.silu(h @ w_gate) * (h @ w_up)) @ w_down


def _ffn_sublayer(x, m, gain, w_gate, w_up, w_down):
    h = _modulate(_rmsnorm(x, gain), m[:, 0], m[:, 1])
    return x + HALF_STEP * m[:, 2] * _swiglu(h, w_gate, w_up, w_down)


def _s5_discretise(lam_re, lam_im, log_dt, b_re, b_im):
    f32 = jnp.float32
    dt = jnp.exp(log_dt.astype(f32))[:, None]
    lr = jnp.minimum(lam_re.astype(f32), LAMBDA_RE_MAX)
    li = lam_im.astype(f32)
    mag = jnp.exp(lr * dt)
    ab_re = mag * jnp.cos(li * dt)
    ab_im = mag * jnp.sin(li * dt)
    nr, ni = ab_re - 1.0, ab_im
    den = lr * lr + li * li
    f_re = (nr * lr + ni * li) / den
    f_im = (ni * lr - nr * li) / den
    br, bi = b_re.astype(f32), b_im.astype(f32)
    bb_re = f_re[..., None] * br - f_im[..., None] * bi
    bb_im = f_re[..., None] * bi + f_im[..., None] * br
    return ab_re, ab_im, bb_re, bb_im


def _affine_combine(e1, e2):
    a1r, a1i, b1r, b1i = e1
    a2r, a2i, b2r, b2i = e2
    return (a2r * a1r - a2i * a1i,
            a2r * a1i + a2i * a1r,
            a2r * b1r - a2i * b1i + b2r,
            a2r * b1i + a2i * b1r + b2i)


def _s5_states(u, disc, h0, reverse):
    ab_re, ab_im, bb_re, bb_im = disc
    bu_re = jnp.einsum('blgh,gph->blgp', u, bb_re)
    bu_im = jnp.einsum('blgh,gph->blgp', u, bb_im)
    if h0 is not None:
        first = -1 if reverse else 0
        h_re, h_im = h0
        bu_re = bu_re.at[:, first].add(ab_re * h_re - ab_im * h_im)
        bu_im = bu_im.at[:, first].add(ab_re * h_im + ab_im * h_re)
    a_re = jnp.broadcast_to(ab_re, bu_re.shape)
    a_im = jnp.broadcast_to(ab_im, bu_im.shape)
    _, _, s_re, s_im = lax.associative_scan(_affine_combine, (a_re, a_im, bu_re, bu_im),
                                            reverse=reverse, axis=1)
    return s_re, s_im


def _s5_readout(s_re, s_im, c_re, c_im):
    return jnp.einsum('blgp,ghp->blgh', s_re, c_re) - jnp.einsum('blgp,ghp->blgh', s_im, c_im)


def _s5_bidirectional(u, u_ctx, lam_re, lam_im, log_dt, b_re, b_im, c_re, c_im, d_skip, ctx_out):
    f32 = jnp.float32
    bsz, length, _ = u.shape
    ctx_len = u_ctx.shape[1]
    ug = u.astype(f32).reshape(bsz, length, S5_GROUPS, S5_GROUP)
    ucg = u_ctx.astype(f32).reshape(bsz, ctx_len, S5_GROUPS, S5_GROUP)
    dg = d_skip.astype(f32).reshape(S5_GROUPS, S5_GROUP)
    y = ug * dg
    y_ctx = ucg * dg if ctx_out else None
    for direction, rev in enumerate((False, True)):
        disc = _s5_discretise(lam_re[direction], lam_im[direction], log_dt[direction],
                              b_re[direction], b_im[direction])
        cr = c_re[direction].astype(f32)
        ci = c_im[direction].astype(f32)
        sc_re, sc_im = _s5_states(ucg, disc, None, rev)
        last = 0 if rev else -1
        h_ctx = (sc_re[:, last], sc_im[:, last])
        sl_re, sl_im = _s5_states(ug, disc, h_ctx, rev)
        y = y + _s5_readout(sl_re, sl_im, cr, ci)
        if ctx_out:
            y_ctx = y_ctx + _s5_readout(sc_re, sc_im, cr, ci)
    y = y.reshape(bsz, length, D_S5)
    if ctx_out:
        y_ctx = y_ctx.reshape(bsz, ctx_len, D_S5)
    return y, y_ctx


def _short_conv_rows(u, w, b, n_rows, row_len):
    bsz, length, ch = u.shape
    ug = u.reshape(bsz, n_rows, row_len, ch)
    pad = HY_SHORT // 2
    up = jnp.pad(ug, ((0, 0), (0, 0), (pad, pad), (0, 0)))
    y = b + up[:, :, 0:row_len] * w[0]
    for j in range(1, HY_SHORT):
        y = y + up[:, :, j:j + row_len] * w[j]
    return y.reshape(bsz, length, ch)


def _hyena_filter_spectrum(length, w1, b1, w2, b2, w3, b3, freq, w_out):
    f32 = jnp.float32
    t = jnp.linspace(0.0, 1.0, length, dtype=f32)[:, None]
    w = (2.0 * math.pi / length) * jnp.arange(length, dtype=f32)[:, None]
    bands = jnp.linspace(1e-4, HY_BANDS - 1, HY_BANDS, dtype=f32)[None, :]
    feats = jnp.concatenate([t, jnp.cos(bands * w), -jnp.sin(bands * w)], axis=-1)
    fr = freq.astype(f32)
    hdn = jnp.sin(fr * (feats @ w1.astype(f32) + b1.astype(f32)))
    hdn = jnp.sin(fr * (hdn @ w2.astype(f32) + b2.astype(f32)))
    hdn = jnp.sin(fr * (hdn @ w3.astype(f32) + b3.astype(f32)))
    k = hdn @ w_out.astype(f32)
    n_ch = k.shape[-1]
    deltas = jnp.abs(jnp.linspace(math.log(HY_TARGET) / HY_SLOW_PCT,
                                  math.log(HY_TARGET) / HY_FAST_PCT, n_ch, dtype=f32))
    k = (k * jnp.exp(-t * deltas)).reshape(length, HY_ORDER, HY_DIRS, D_HY)
    k_two = jnp.concatenate([k[:, :, 0],
                             jnp.zeros((1, HY_ORDER, D_HY), f32),
                             k[:0:-1, :, 1]], axis=0)
    k_two = k_two / jnp.sum(jnp.abs(k_two), axis=0, keepdims=True)
    return jnp.fft.rfft(k_two, axis=0)


def _hyena(u, n_rows, row_len, short_w, short_b, w1, b1, w2, b2, w3, b3, freq, w_out, bias):
    f32 = jnp.float32
    length = u.shape[1]
    us = _short_conv_rows(u, short_w, short_b, n_rows, row_len).astype(f32)
    parts = jnp.split(us, HY_ORDER + 1, axis=-1)
    z, gates = parts[0], parts[1:]
    k_f = _hyena_filter_spectrum(length, w1, b1, w2, b2, w3, b3, freq, w_out)
    bias = bias.astype(f32)
    n_fft = 2 * length
    for o in range(HY_ORDER):
        conv = jnp.fft.irfft(jnp.fft.rfft(z, n=n_fft, axis=1) * k_f[None, :, o], n=n_fft, axis=1)[:, :length]
        z = gates[o] * (conv + bias[o] * z)
    return z


def _merge_branches(y_s5, y_hy, g_a, g_b, w_pa, w_pb, w_out):
    s = jax.nn.gelu(y_s5)
    pa = s @ w_pa
    y_a = pa[..., :D_MODEL] * jax.nn.sigmoid(pa[..., D_MODEL:])
    y_b = y_hy @ w_pb
    return (jax.nn.sigmoid(g_a) * y_a + jax.nn.sigmoid(g_b) * y_b) @ w_out


def setup_inputs(seed: int = 0) -> dict:
    key = jax.random.key(seed)
    keys = iter(jax.random.split(key, 48))
    f32 = jnp.float32

    def normal(shape, scale):
        return scale * jax.random.normal(next(keys), shape, f32)

    x = normal((BATCH, SEQ, D_MODEL), 1.0)
    c = normal((BATCH, D_MODEL), 1.0)
    ctx = normal((BATCH, CTX_LEN, D_MODEL), 1.0)
    c_ctx = normal((D_MODEL,), 1.0)
    w_ada = normal((DEPTH, D_MODEL, N_SUB * N_MOD * D_MODEL), D_MODEL ** -0.5)
    b_ada = normal((DEPTH, N_SUB * N_MOD * D_MODEL), 0.02)
    norm_g = 1.0 + normal((DEPTH, N_SUB, D_MODEL), 0.01)
    ffn_w_gate = normal((DEPTH, 2, D_MODEL, D_FF), D_MODEL ** -0.5)
    ffn_w_up = normal((DEPTH, 2, D_MODEL, D_FF), D_MODEL ** -0.5)
    ffn_w_down = normal((DEPTH, 2, D_FF, D_MODEL), D_FF ** -0.5)
    w_in = normal((DEPTH, D_MODEL, D_IN), D_MODEL ** -0.5)
    s5_shape = (DEPTH, S5_DIRS, S5_GROUPS, S5_STATE)
    s5_lam_re = -0.5 + normal(s5_shape, 0.02)
    s5_lam_im = math.pi * jnp.arange(S5_STATE, dtype=f32) + normal(s5_shape, 0.02)
    s5_log_dt = jax.random.uniform(next(keys), (DEPTH, S5_DIRS, S5_GROUPS), f32,
                                   math.log(1e-3), math.log(1e-1))
    b_shape = (DEPTH, S5_DIRS, S5_GROUPS, S5_STATE, S5_GROUP)
    s5_b_re = normal(b_shape, (2.0 * S5_GROUP) ** -0.5)
    s5_b_im = normal(b_shape, (2.0 * S5_GROUP) ** -0.5)
    c_shape = (DEPTH, S5_DIRS, S5_GROUPS, S5_GROUP, S5_STATE)
    s5_c_re = normal(c_shape, (2.0 * S5_STATE) ** -0.5)
    s5_c_im = normal(c_shape, (2.0 * S5_STATE) ** -0.5)
    s5_d = normal((DEPTH, D_S5), 1.0)
    hy_short_w = normal((DEPTH, HY_SHORT, (HY_ORDER + 1) * D_HY), HY_SHORT ** -0.5)
    hy_short_b = normal((DEPTH, (HY_ORDER + 1) * D_HY), 0.02)
    hy_w1 = normal((DEPTH, HY_EMB, HY_FF), HY_EMB ** -0.5)
    hy_b1 = normal((DEPTH, HY_FF), 0.1)
    hy_w2 = normal((DEPTH, HY_FF, HY_FF), HY_FF ** -0.5)
    hy_b2 = normal((DEPTH, HY_FF), 0.1)
    hy_w3 = normal((DEPTH, HY_FF, HY_FF), HY_FF ** -0.5)
    hy_b3 = normal((DEPTH, HY_FF), 0.1)
    hy_freq = 1.0 + normal((DEPTH, HY_FF), 0.1)
    hy_w_out = normal((DEPTH, HY_FF, HY_ORDER * HY_DIRS * D_HY), HY_FF ** -0.5)
    hy_bias = normal((DEPTH, HY_ORDER, D_HY), 0.5)
    w_pa = normal((DEPTH, D_S5, 2 * D_MODEL), D_S5 ** -0.5)
    w_pb = normal((DEPTH, D_HY, D_MODEL), D_HY ** -0.5)
    w_out = normal((DEPTH, D_MODEL, D_MODEL), D_MODEL ** -0.5)
    final_g = 1.0 + normal((D_MODEL,), 0.01)
    return {"x": x, "c": c, "ctx": ctx, "c_ctx": c_ctx,
            "w_ada": w_ada, "b_ada": b_ada, "norm_g": norm_g,
            "ffn_w_gate": ffn_w_gate, "ffn_w_up": ffn_w_up, "ffn_w_down": ffn_w_down,
            "w_in": w_in,
            "s5_lam_re": s5_lam_re, "s5_lam_im": s5_lam_im, "s5_log_dt": s5_log_dt,
            "s5_b_re": s5_b_re, "s5_b_im": s5_b_im, "s5_c_re": s5_c_re, "s5_c_im": s5_c_im,
            "s5_d": s5_d,
            "hy_short_w": hy_short_w, "hy_short_b": hy_short_b,
            "hy_w1": hy_w1, "hy_b1": hy_b1, "hy_w2": hy_w2, "hy_b2": hy_b2,
            "hy_w3": hy_w3, "hy_b3": hy_b3, "hy_freq": hy_freq, "hy_w_out": hy_w_out,
            "hy_bias": hy_bias,
            "w_pa": w_pa, "w_pb": w_pb, "w_out": w_out, "final_g": final_g}


def reference(x, c, ctx, c_ctx, w_ada, b_ada, norm_g, ffn_w_gate, ffn_w_up, ffn_w_down, w_in,
              s5_lam_re, s5_lam_im, s5_log_dt, s5_b_re, s5_b_im, s5_c_re, s5_c_im, s5_d,
              hy_short_w, hy_short_b, hy_w1, hy_b1, hy_w2, hy_b2, hy_w3, hy_b3, hy_freq, hy_w_out,
              hy_bias, w_pa, w_pb, w_out, final_g):
    bsz = x.shape[0]
    n_rows = x.shape[1] // GRID_W
    ctx_len = ctx.shape[1]
    for l in range(DEPTH):
        update_ctx = l < DEPTH - 1
        mod = (jax.nn.silu(c) @ w_ada[l] + b_ada[l]).reshape(bsz, N_SUB, N_MOD, 1, D_MODEL)
        mod_c = (jax.nn.silu(c_ctx) @ w_ada[l] + b_ada[l]).reshape(1, N_SUB, N_MOD, 1, D_MODEL)

        x = _ffn_sublayer(x, mod[:, 0], norm_g[l, 0], ffn_w_gate[l, 0], ffn_w_up[l, 0], ffn_w_down[l, 0])
        ctx = _ffn_sublayer(ctx, mod_c[:, 0], norm_g[l, 0], ffn_w_gate[l, 0], ffn_w_up[l, 0], ffn_w_down[l, 0])

        h = _modulate(_rmsnorm(x, norm_g[l, 1]), mod[:, 1, 0], mod[:, 1, 1])
        hc = _modulate(_rmsnorm(ctx, norm_g[l, 1]), mod_c[:, 1, 0], mod_c[:, 1, 1])
        proj = h @ w_in[l]
        proj_c = hc @ (w_in[l] if update_ctx else w_in[l][:, :D_S5])
        y_s5, y_s5_c = _s5_bidirectional(proj[..., :I_HY], proj_c[..., :D_S5],
                                         s5_lam_re[l], s5_lam_im[l], s5_log_dt[l],
                                         s5_b_re[l], s5_b_im[l], s5_c_re[l], s5_c_im[l], s5_d[l],
                                         update_ctx)
        y_hy = _hyena(proj[..., I_HY:I_GA], n_rows, GRID_W, hy_short_w[l], hy_short_b[l],
                      hy_w1[l], hy_b1[l], hy_w2[l], hy_b2[l], hy_w3[l], hy_b3[l], hy_freq[l],
                      hy_w_out[l], hy_bias[l])
        mixed = _merge_branches(y_s5.astype(x.dtype), y_hy.astype(x.dtype),
                                proj[..., I_GA:I_GB], proj[..., I_GB:], w_pa[l], w_pb[l], w_out[l])
        x = x + mod[:, 1, 2] * mixed
        if update_ctx:
            y_hy_c = _hyena(proj_c[..., I_HY:I_GA], 1, ctx_len, hy_short_w[l], hy_short_b[l],
                            hy_w1[l], hy_b1[l], hy_w2[l], hy_b2[l], hy_w3[l], hy_b3[l], hy_freq[l],
                            hy_w_out[l], hy_bias[l])
            mixed_c = _merge_branches(y_s5_c.astype(ctx.dtype), y_hy_c.astype(ctx.dtype),
                                      proj_c[..., I_GA:I_GB], proj_c[..., I_GB:],
                                      w_pa[l], w_pb[l], w_out[l])
            ctx = ctx + mod_c[:, 1, 2] * mixed_c
            ctx = _ffn_sublayer(ctx, mod_c[:, 2], norm_g[l, 2], ffn_w_gate[l, 1], ffn_w_up[l, 1],
                                ffn_w_down[l, 1])

        x = _ffn_sublayer(x, mod[:, 2], norm_g[l, 2], ffn_w_gate[l, 1], ffn_w_up[l, 1], ffn_w_down[l, 1])
    return _rmsnorm(x, final_g)
```

```python
import functools
import math

import jax
import jax.numpy as jnp
from jax import lax
from jax.experimental import pallas as pl
from jax.experimental.pallas import tpu as pltpu

F32 = jnp.float32
BF16 = jnp.bfloat16

D_MODEL = 2048
GRID_W = 64
D_S5 = 1024
S5_GROUP = 16
S5_GROUPS = D_S5 // S5_GROUP
S5_STATE = 64
LAMBDA_RE_MAX = -1e-4
D_HY = 1024
HY_ORDER = 2
HY_DIRS = 2
HY_SHORT = 3
HY_EMB = 33
HY_BANDS = (HY_EMB - 1) // 2
HY_TARGET = 1e-2
HY_FAST_PCT = 0.3
HY_SLOW_PCT = 1.5
I_HY = D_S5
I_GA = D_S5 + (HY_ORDER + 1) * D_HY
I_GB = I_GA + D_MODEL
D_IN = I_GB + D_MODEL
D_FF = 5632
N_SUB = 3
N_MOD = 3
HALF_STEP = 0.5
RMS_EPS = 1e-6

VMEM_LIMIT_BYTES = 56 * 1024 * 1024


def _rms_mod(x, gain, shift, scale):
    ms = jnp.mean(x * x, axis=-1, keepdims=True)
    y = x * lax.rsqrt(ms + RMS_EPS) * gain
    return y * (1.0 + scale) + shift


def _ada_kernel(c_ref, w_ref, b_ref, o_ref):
    c = c_ref[...]
    a = c * jax.nn.sigmoid(c)
    o_ref[...] = jnp.dot(a, w_ref[...], preferred_element_type=F32,
                         precision=lax.Precision.HIGHEST) + b_ref[...]


def _ada_mod(c_rows, w, b, tn=1024):
    rows, d = c_rows.shape
    n = w.shape[1]
    return pl.pallas_call(
        _ada_kernel,
        grid=(n // tn,),
        in_specs=[pl.BlockSpec((rows, d), lambda j: (0, 0)),
                  pl.BlockSpec((d, tn), lambda j: (0, j)),
                  pl.BlockSpec((1, tn), lambda j: (0, j))],
        out_specs=pl.BlockSpec((rows, tn), lambda j: (0, j)),
        out_shape=jax.ShapeDtypeStruct((rows, n), F32),
        compiler_params=pltpu.CompilerParams(
            dimension_semantics=("arbitrary",), vmem_limit_bytes=VMEM_LIMIT_BYTES),
        name="ada_mod",
    )(c_rows, w, b.reshape(1, n))


def _ffn_kernel(x_ref, shift_ref, scale_ref, gate_ref, gain_ref, wg_ref, wu_ref, wd_ref,
                fg_ref, o_ref, h_ref, acc_ref, *, final_norm):
    j = pl.program_id(1)

    @pl.when(j == 0)
    def _():
        h_ref[...] = _rms_mod(x_ref[...], gain_ref[...], shift_ref[...], scale_ref[...]).astype(BF16)
        acc_ref[...] = jnp.zeros_like(acc_ref)

    h = h_ref[...]
    g = jnp.dot(h, wg_ref[...], preferred_element_type=F32)
    u = jnp.dot(h, wu_ref[...], preferred_element_type=F32)
    a = (g * jax.nn.sigmoid(g) * u).astype(BF16)
    acc_ref[...] += jnp.dot(a, wd_ref[...], preferred_element_type=F32)

    @pl.when(j == pl.num_programs(1) - 1)
    def _():
        y = x_ref[...] + (HALF_STEP * gate_ref[...]) * acc_ref[...]
        if final_norm:
            ms = jnp.mean(y * y, axis=-1, keepdims=True)
            y = y * lax.rsqrt(ms + RMS_EPS) * fg_ref[...]
        o_ref[...] = y


def _ffn_sublayer(x, mods, gain, wg, wu, wd, final_gain=None, tm=512, tf=512):
    t, d = x.shape
    bm = mods[0].shape[0]
    blocks_per_batch = (t // bm) // tm
    dff = wg.shape[1]
    final_norm = final_gain is not None
    fg = final_gain if final_norm else gain
    mod_spec = pl.BlockSpec((None, 1, d), lambda i, j: (i // blocks_per_batch, 0, 0))
    vec_spec = pl.BlockSpec((1, d), lambda i, j: (0, 0))
    return pl.pallas_call(
        functools.partial(_ffn_kernel, final_norm=final_norm),
        grid=(t // tm, dff // tf),
        in_specs=[pl.BlockSpec((tm, d), lambda i, j: (i, 0)),
                  mod_spec, mod_spec, mod_spec, vec_spec,
                  pl.BlockSpec((d, tf), lambda i, j: (0, j)),
                  pl.BlockSpec((d, tf), lambda i, j: (0, j)),
                  pl.BlockSpec((tf, d), lambda i, j: (j, 0)),
                  vec_spec],
        out_specs=pl.BlockSpec((tm, d), lambda i, j: (i, 0)),
        out_shape=jax.ShapeDtypeStruct((t, d), F32),
        scratch_shapes=[pltpu.VMEM((tm, d), BF16), pltpu.VMEM((tm, d), F32)],
        compiler_params=pltpu.CompilerParams(
            dimension_semantics=("parallel", "arbitrary"), vmem_limit_bytes=VMEM_LIMIT_BYTES),
        name="ffn_final" if final_norm else "ffn",
    )(x, *mods, gain.reshape(1, d), wg, wu, wd, fg.reshape(1, d))


def _proj_kernel(x_ref, shift_ref, scale_ref, gain_ref, w_ref, *rest, n32, n16):
    o_refs, h_ref = rest[:-1], rest[-1]
    j = pl.program_id(1)

    @pl.when(j == 0)
    def _():
        h_ref[...] = _rms_mod(x_ref[...], gain_ref[...], shift_ref[...], scale_ref[...]).astype(BF16)

    p = jnp.dot(h_ref[...], w_ref[...], preferred_element_type=F32)
    if n16 == 0:
        o_refs[0][...] = p
    else:
        @pl.when(j < n32)
        def _():
            o_refs[0][...] = p

        @pl.when(j >= n32)
        def _():
            o_refs[1][...] = jax.nn.sigmoid(p).astype(BF16)


def _in_proj(x, shift, scale, gain, w, n_plain, tm=512, tn=512):
    t, d = x.shape
    bm = shift.shape[0]
    blocks_per_batch = (t // bm) // tm
    n = w.shape[1]
    n32 = n_plain // tn
    n16 = (n - n_plain) // tn
    mod_spec = pl.BlockSpec((None, 1, d), lambda i, j: (i // blocks_per_batch, 0, 0))
    out_shape = [jax.ShapeDtypeStruct((t, n_plain), F32)]
    out_specs = [pl.BlockSpec((tm, tn), lambda i, j: (i, jnp.minimum(j, n32 - 1)))]
    if n16:
        out_shape.append(jax.ShapeDtypeStruct((t, n - n_plain), BF16))
        out_specs.append(pl.BlockSpec((tm, tn), lambda i, j: (i, jnp.maximum(j - n32, 0))))
    return pl.pallas_call(
        functools.partial(_proj_kernel, n32=n32, n16=n16),
        grid=(t // tm, n // tn),
        in_specs=[pl.BlockSpec((tm, d), lambda i, j: (i, 0)),
                  mod_spec, mod_spec,
                  pl.BlockSpec((1, d), lambda i, j: (0, 0)),
                  pl.BlockSpec((d, tn), lambda i, j: (0, j))],
        out_specs=out_specs,
        out_shape=out_shape,
        scratch_shapes=[pltpu.VMEM((tm, d), BF16)],
        compiler_params=pltpu.CompilerParams(
            dimension_semantics=("parallel", "arbitrary"), vmem_limit_bytes=VMEM_LIMIT_BYTES),
        name="in_proj",
    )(x, shift, scale, gain.reshape(1, d), w)


def _gelu_tanh(x):
    return 0.5 * x * (1.0 + jnp.tanh(math.sqrt(2.0 / math.pi) * (x + 0.044715 * (x * x * x))))


def _merge_kernel(x_ref, gate_ref, ys_ref, yh_ref, ga_ref, gb_ref, wpa_lo_ref, wpa_hi_ref, wpb_ref,
                  wout_ref, o_ref, s_ref, acc_ref):
    j = pl.program_id(1)

    @pl.when(j == 0)
    def _():
        s_ref[...] = _gelu_tanh(ys_ref[...]).astype(BF16)
        acc_ref[...] = jnp.zeros_like(acc_ref)

    s = s_ref[...]
    pa_lo = jnp.dot(s, wpa_lo_ref[...], preferred_element_type=F32)
    pa_hi = jnp.dot(s, wpa_hi_ref[...], preferred_element_type=F32)
    y_a = pa_lo * jax.nn.sigmoid(pa_hi)
    y_b = jnp.dot(yh_ref[...].astype(BF16), wpb_ref[...], preferred_element_type=F32)
    m = ga_ref[...].astype(F32) * y_a + gb_ref[...].astype(F32) * y_b
    acc_ref[...] += jnp.dot(m.astype(BF16), wout_ref[...], preferred_element_type=F32)

    @pl.when(j == pl.num_programs(1) - 1)
    def _():
        o_ref[...] = x_ref[...] + gate_ref[...] * acc_ref[...]


def _merge(x, gate, y_s5, y_hy, sig_gates, w_pa, w_pb, w_out, tm=512, tn=512):
    t, d = x.shape
    bm = gate.shape[0]
    blocks_per_batch = (t // bm) // tm
    nj = d // tn
    ds5 = y_s5.shape[1]
    dhy = y_hy.shape[1]
    return pl.pallas_call(
        _merge_kernel,
        grid=(t // tm, nj),
        in_specs=[pl.BlockSpec((tm, d), lambda i, j: (i, 0)),
                  pl.BlockSpec((None, 1, d), lambda i, j: (i // blocks_per_batch, 0, 0)),
                  pl.BlockSpec((tm, ds5), lambda i, j: (i, 0)),
                  pl.BlockSpec((tm, dhy), lambda i, j: (i, 0)),
                  pl.BlockSpec((tm, tn), lambda i, j: (i, j)),
                  pl.BlockSpec((tm, tn), lambda i, j: (i, nj + j)),
                  pl.BlockSpec((ds5, tn), lambda i, j: (0, j)),
                  pl.BlockSpec((ds5, tn), lambda i, j: (0, nj + j)),
                  pl.BlockSpec((dhy, tn), lambda i, j: (0, j)),
                  pl.BlockSpec((tn, d), lambda i, j: (j, 0))],
        out_specs=pl.BlockSpec((tm, d), lambda i, j: (i, 0)),
        out_shape=jax.ShapeDtypeStruct((t, d), F32),
        scratch_shapes=[pltpu.VMEM((tm, ds5), BF16), pltpu.VMEM((tm, d), F32)],
        compiler_params=pltpu.CompilerParams(
            dimension_semantics=("parallel", "arbitrary"), vmem_limit_bytes=VMEM_LIMIT_BYTES),
        name="merge",
    )(x, gate, y_s5, y_hy, sig_gates, sig_gates, w_pa, w_pa, w_pb, w_out)


def _s5_discretise(lam_re, lam_im, log_dt, b_re, b_im):
    dt = jnp.exp(log_dt)[:, None]
    lr = jnp.minimum(lam_re, LAMBDA_RE_MAX)
    li = lam_im
    mag = jnp.exp(lr * dt)
    ab_re = mag * jnp.cos(li * dt)
    ab_im = mag * jnp.sin(li * dt)
    nr, ni = ab_re - 1.0, ab_im
    den = lr * lr + li * li
    f_re = (nr * lr + ni * li) / den
    f_im = (ni * lr - nr * li) / den
    bb_re = f_re[..., None] * b_re - f_im[..., None] * b_im
    bb_im = f_re[..., None] * b_im + f_im[..., None] * b_re
    return ab_re, ab_im, bb_re, bb_im


def _affine_combine(e1, e2):
    a1r, a1i, b1r, b1i = e1
    a2r, a2i, b2r, b2i = e2
    return (a2r * a1r - a2i * a1i,
            a2r * a1i + a2i * a1r,
            a2r * b1r - a2i * b1i + b2r,
            a2r * b1i + a2i * b1r + b2i)


def _s5_states(u, disc, h0, reverse):
    ab_re, ab_im, bb_re, bb_im = disc
    hp = lax.Precision.HIGHEST
    bu_re = jnp.einsum('blgh,gph->blgp', u, bb_re, precision=hp)
    bu_im = jnp.einsum('blgh,gph->blgp', u, bb_im, precision=hp)
    if h0 is not None:
        first = -1 if reverse else 0
        h_re, h_im = h0
        bu_re = bu_re.at[:, first].add(ab_re * h_re - ab_im * h_im)
        bu_im = bu_im.at[:, first].add(ab_re * h_im + ab_im * h_re)
    a_re = jnp.broadcast_to(ab_re, bu_re.shape)
    a_im = jnp.broadcast_to(ab_im, bu_im.shape)
    _, _, s_re, s_im = lax.associative_scan(_affine_combine, (a_re, a_im, bu_re, bu_im),
                                            reverse=reverse, axis=1)
    return s_re, s_im


def _s5_bidirectional(u, u_ctx, lam_re, lam_im, log_dt, b_re, b_im, c_re, c_im, d_skip):
    bsz, length, _ = u.shape
    ctx_len = u_ctx.shape[1]
    hp = lax.Precision.HIGHEST
    ug = u.reshape(bsz, length, S5_GROUPS, S5_GROUP)
    ucg = u_ctx.reshape(bsz, ctx_len, S5_GROUPS, S5_GROUP)
    y = ug * d_skip.reshape(S5_GROUPS, S5_GROUP)
    for direction, rev in enumerate((False, True)):
        disc = _s5_discretise(lam_re[direction], lam_im[direction], log_dt[direction],
                              b_re[direction], b_im[direction])
        sc_re, sc_im = _s5_states(ucg, disc, None, rev)
        last = 0 if rev else -1
        sl_re, sl_im = _s5_states(ug, disc, (sc_re[:, last], sc_im[:, last]), rev)
        y = y + (jnp.einsum('blgp,ghp->blgh', sl_re, c_re[direction], precision=hp)
                 - jnp.einsum('blgp,ghp->blgh', sl_im, c_im[direction], precision=hp))
    return y.reshape(bsz, length, D_S5)


def _short_conv_rows(u, w, b, n_rows, row_len):
    bsz, length, ch = u.shape
    ug = u.reshape(bsz, n_rows, row_len, ch)
    pad = HY_SHORT // 2
    up = jnp.pad(ug, ((0, 0), (0, 0), (pad, pad), (0, 0)))
    y = b + up[:, :, 0:row_len] * w[0]
    for j in range(1, HY_SHORT):
        y = y + up[:, :, j:j + row_len] * w[j]
    return y.reshape(bsz, length, ch)


def _hyena_filter_spectrum(length, w1, b1, w2, b2, w3, b3, freq, w_out):
    hp = lax.Precision.HIGHEST
    t = jnp.linspace(0.0, 1.0, length, dtype=F32)[:, None]
    w = (2.0 * math.pi / length) * jnp.arange(length, dtype=F32)[:, None]
    bands = jnp.linspace(1e-4, HY_BANDS - 1, HY_BANDS, dtype=F32)[None, :]
    feats = jnp.concatenate([t, jnp.cos(bands * w), -jnp.sin(bands * w)], axis=-1)
    hdn = jnp.sin(freq * (jnp.dot(feats, w1, precision=hp) + b1))
    hdn = jnp.sin(freq * (jnp.dot(hdn, w2, precision=hp) + b2))
    hdn = jnp.sin(freq * (jnp.dot(hdn, w3, precision=hp) + b3))
    k = jnp.dot(hdn, w_out, precision=hp)
    n_ch = k.shape[-1]
    deltas = jnp.abs(jnp.linspace(math.log(HY_TARGET) / HY_SLOW_PCT,
                                  math.log(HY_TARGET) / HY_FAST_PCT, n_ch, dtype=F32))
    k = (k * jnp.exp(-t * deltas)).reshape(length, HY_ORDER, HY_DIRS, D_HY)
    k_two = jnp.concatenate([k[:, :, 0], jnp.zeros((1, HY_ORDER, D_HY), F32), k[:0:-1, :, 1]], axis=0)
    k_two = k_two / jnp.sum(jnp.abs(k_two), axis=0, keepdims=True)
    return jnp.fft.rfft(k_two, axis=0)


def _hyena(u, n_rows, row_len, short_w, short_b, w1, b1, w2, b2, w3, b3, freq, w_out, bias):
    length = u.shape[1]
    us = _short_conv_rows(u, short_w, short_b, n_rows, row_len)
    parts = jnp.split(us, HY_ORDER + 1, axis=-1)
    z, gates = parts[0], parts[1:]
    k_f = _hyena_filter_spectrum(length, w1, b1, w2, b2, w3, b3, freq, w_out)
    n_fft = 2 * length
    for o in range(HY_ORDER):
        conv = jnp.fft.irfft(jnp.fft.rfft(z, n=n_fft, axis=1) * k_f[None, :, o], n=n_fft, axis=1)[:, :length]
        z = gates[o] * (conv + bias[o] * z)
    return z


def kernel(x, c, ctx, c_ctx, w_ada, b_ada, norm_g, ffn_w_gate, ffn_w_up, ffn_w_down, w_in,
           s5_lam_re, s5_lam_im, s5_log_dt, s5_b_re, s5_b_im, s5_c_re, s5_c_im, s5_d,
           hy_short_w, hy_short_b, hy_w1, hy_b1, hy_w2, hy_b2, hy_w3, hy_b3, hy_freq, hy_w_out,
           hy_bias, w_pa, w_pb, w_out, final_g):
    bsz, seq, d = x.shape
    ctx_len = ctx.shape[1]
    n_rows = seq // GRID_W
    depth = w_ada.shape[0]
    assert depth == 1, "context-token outputs are only dropped by the last layer"
    l = 0

    c_rows = jnp.concatenate([c, c_ctx[None, :], jnp.zeros((8 - bsz - 1, d), F32)], axis=0)
    mod_all = _ada_mod(c_rows, w_ada[l], b_ada[l])
    mod = mod_all[:bsz].reshape(bsz, N_SUB, N_MOD, 1, d)
    mod_c = mod_all[bsz:bsz + 1].reshape(1, N_SUB, N_MOD, 1, d)

    def mods(m, sub):
        return tuple(m[:, sub, k] for k in range(N_MOD))

    wg = ffn_w_gate[l].astype(BF16)
    wu = ffn_w_up[l].astype(BF16)
    wd = ffn_w_down[l].astype(BF16)
    w_in_b = w_in[l].astype(BF16)

    xt = x.reshape(bsz * seq, d)
    ct = ctx.reshape(bsz * ctx_len, d)

    xt = _ffn_sublayer(xt, mods(mod, 0), norm_g[l, 0], wg[0], wu[0], wd[0])
    ct = _ffn_sublayer(ct, mods(mod_c, 0), norm_g[l, 0], wg[0], wu[0], wd[0])

    uhy, sig_gates = _in_proj(xt, mod[:, 1, 0], mod[:, 1, 1], norm_g[l, 1], w_in_b, I_GA)
    (u_ctx,) = _in_proj(ct, mod_c[:, 1, 0], mod_c[:, 1, 1], norm_g[l, 1], w_in_b[:, :D_S5], D_S5)

    proj = uhy.reshape(bsz, seq, I_GA)
    y_s5 = _s5_bidirectional(proj[..., :I_HY], u_ctx.reshape(bsz, ctx_len, D_S5),
                             s5_lam_re[l], s5_lam_im[l], s5_log_dt[l],
                             s5_b_re[l], s5_b_im[l], s5_c_re[l], s5_c_im[l], s5_d[l])
    y_hy = _hyena(proj[..., I_HY:I_GA], n_rows, GRID_W, hy_short_w[l], hy_short_b[l],
                  hy_w1[l], hy_b1[l], hy_w2[l], hy_b2[l], hy_w3[l], hy_b3[l], hy_freq[l],
                  hy_w_out[l], hy_bias[l])

    xt = _merge(xt, mod[:, 1, 2], y_s5.reshape(bsz * seq, D_S5), y_hy.reshape(bsz * seq, D_HY),
                sig_gates, w_pa[l].astype(BF16), w_pb[l].astype(BF16), w_out[l].astype(BF16))

    xt = _ffn_sublayer(xt, mods(mod, 2), norm_g[l, 2], wg[1], wu[1], wd[1], final_gain=final_g)
    return xt.reshape(bsz, seq, d)
```

```python
import functools
import math

import jax
import jax.numpy as jnp
from jax import lax
from jax.experimental import pallas as pl
from jax.experimental.pallas import tpu as pltpu

F32 = jnp.float32
BF16 = jnp.bfloat16

D_MODEL = 2048
GRID_W = 64
D_S5 = 1024
S5_GROUP = 16
S5_GROUPS = D_S5 // S5_GROUP
S5_STATE = 64
LAMBDA_RE_MAX = -1e-4
S5_CHUNK = 16
D_HY = 1024
HY_ORDER = 2
HY_DIRS = 2
HY_SHORT = 3
HY_EMB = 33
HY_BANDS = (HY_EMB - 1) // 2
HY_TARGET = 1e-2
HY_FAST_PCT = 0.3
HY_SLOW_PCT = 1.5
I_HY = D_S5
I_GA = D_S5 + (HY_ORDER + 1) * D_HY
I_GB = I_GA + D_MODEL
D_IN = I_GB + D_MODEL
D_FF = 5632
N_SUB = 3
N_MOD = 3
HALF_STEP = 0.5
RMS_EPS = 1e-6

VMEM_LIMIT_BYTES = 56 * 1024 * 1024


def _rms_mod(x, gain, shift, scale):
    ms = jnp.mean(x * x, axis=-1, keepdims=True)
    y = x * lax.rsqrt(ms + RMS_EPS) * gain
    return y * (1.0 + scale) + shift


def _ada_kernel(c_ref, w_ref, b_ref, o_ref):
    c = c_ref[...]
    a = c * jax.nn.sigmoid(c)
    o_ref[...] = jnp.dot(a, w_ref[...], preferred_element_type=F32,
                         precision=lax.Precision.HIGHEST) + b_ref[...]


def _ada_mod(c_rows, w, b, tn=1024):
    rows, d = c_rows.shape
    n = w.shape[1]
    return pl.pallas_call(
        _ada_kernel,
        grid=(n // tn,),
        in_specs=[pl.BlockSpec((rows, d), lambda j: (0, 0)),
                  pl.BlockSpec((d, tn), lambda j: (0, j)),
                  pl.BlockSpec((1, tn), lambda j: (0, j))],
        out_specs=pl.BlockSpec((rows, tn), lambda j: (0, j)),
        out_shape=jax.ShapeDtypeStruct((rows, n), F32),
        compiler_params=pltpu.CompilerParams(
            dimension_semantics=("arbitrary",), vmem_limit_bytes=VMEM_LIMIT_BYTES),
        name="ada_mod",
    )(c_rows, w, b.reshape(1, n))


def _ffn_kernel(x_ref, shift_ref, scale_ref, gate_ref, gain_ref, wg_ref, wu_ref, wd_ref,
                fg_ref, o_ref, h_ref, acc_ref, *, final_norm):
    j = pl.program_id(1)

    @pl.when(j == 0)
    def _():
        h_ref[...] = _rms_mod(x_ref[...], gain_ref[...], shift_ref[...], scale_ref[...]).astype(BF16)
        acc_ref[...] = jnp.zeros_like(acc_ref)

    h = h_ref[...]
    g = jnp.dot(h, wg_ref[...], preferred_element_type=F32)
    u = jnp.dot(h, wu_ref[...], preferred_element_type=F32)
    a = (g * jax.nn.sigmoid(g) * u).astype(BF16)
    acc_ref[...] += jnp.dot(a, wd_ref[...], preferred_element_type=F32)

    @pl.when(j == pl.num_programs(1) - 1)
    def _():
        y = x_ref[...] + (HALF_STEP * gate_ref[...]) * acc_ref[...]
        if final_norm:
            ms = jnp.mean(y * y, axis=-1, keepdims=True)
            y = y * lax.rsqrt(ms + RMS_EPS) * fg_ref[...]
        o_ref[...] = y


def _ffn_sublayer(x, mods, gain, wg, wu, wd, final_gain=None, tm=512, tf=512):
    t, d = x.shape
    bm = mods[0].shape[0]
    blocks_per_batch = (t // bm) // tm
    dff = wg.shape[1]
    final_norm = final_gain is not None
    fg = final_gain if final_norm else gain
    mod_spec = pl.BlockSpec((None, 1, d), lambda i, j: (i // blocks_per_batch, 0, 0))
    vec_spec = pl.BlockSpec((1, d), lambda i, j: (0, 0))
    return pl.pallas_call(
        functools.partial(_ffn_kernel, final_norm=final_norm),
        grid=(t // tm, dff // tf),
        in_specs=[pl.BlockSpec((tm, d), lambda i, j: (i, 0)),
                  mod_spec, mod_spec, mod_spec, vec_spec,
                  pl.BlockSpec((d, tf), lambda i, j: (0, j)),
                  pl.BlockSpec((d, tf), lambda i, j: (0, j)),
                  pl.BlockSpec((tf, d), lambda i, j: (j, 0)),
                  vec_spec],
        out_specs=pl.BlockSpec((tm, d), lambda i, j: (i, 0)),
        out_shape=jax.ShapeDtypeStruct((t, d), F32),
        scratch_shapes=[pltpu.VMEM((tm, d), BF16), pltpu.VMEM((tm, d), F32)],
        compiler_params=pltpu.CompilerParams(
            dimension_semantics=("parallel", "arbitrary"), vmem_limit_bytes=VMEM_LIMIT_BYTES),
        name="ffn_final" if final_norm else "ffn",
    )(x, *mods, gain.reshape(1, d), wg, wu, wd, fg.reshape(1, d))


def _proj_kernel(x_ref, shift_ref, scale_ref, gain_ref, w_ref, *rest, n32, n16):
    o_refs, h_ref = rest[:-1], rest[-1]
    j = pl.program_id(1)

    @pl.when(j == 0)
    def _():
        h_ref[...] = _rms_mod(x_ref[...], gain_ref[...], shift_ref[...], scale_ref[...]).astype(BF16)

    p = jnp.dot(h_ref[...], w_ref[...], preferred_element_type=F32)
    if n16 == 0:
        o_refs[0][...] = p
    else:
        @pl.when(j < n32)
        def _():
            o_refs[0][...] = p

        @pl.when(j >= n32)
        def _():
            o_refs[1][...] = jax.nn.sigmoid(p).astype(BF16)


def _in_proj(x, shift, scale, gain, w, n_plain, tm=512, tn=512):
    t, d = x.shape
    bm = shift.shape[0]
    blocks_per_batch = (t // bm) // tm
    n = w.shape[1]
    n32 = n_plain // tn
    n16 = (n - n_plain) // tn
    mod_spec = pl.BlockSpec((None, 1, d), lambda i, j: (i // blocks_per_batch, 0, 0))
    out_shape = [jax.ShapeDtypeStruct((t, n_plain), F32)]
    out_specs = [pl.BlockSpec((tm, tn), lambda i, j: (i, jnp.minimum(j, n32 - 1)))]
    if n16:
        out_shape.append(jax.ShapeDtypeStruct((t, n - n_plain), BF16))
        out_specs.append(pl.BlockSpec((tm, tn), lambda i, j: (i, jnp.maximum(j - n32, 0))))
    return pl.pallas_call(
        functools.partial(_proj_kernel, n32=n32, n16=n16),
        grid=(t // tm, n // tn),
        in_specs=[pl.BlockSpec((tm, d), lambda i, j: (i, 0)),
                  mod_spec, mod_spec,
                  pl.BlockSpec((1, d), lambda i, j: (0, 0)),
                  pl.BlockSpec((d, tn), lambda i, j: (0, j))],
        out_specs=out_specs,
        out_shape=out_shape,
        scratch_shapes=[pltpu.VMEM((tm, d), BF16)],
        compiler_params=pltpu.CompilerParams(
            dimension_semantics=("parallel", "arbitrary"), vmem_limit_bytes=VMEM_LIMIT_BYTES),
        name="in_proj",
    )(x, shift, scale, gain.reshape(1, d), w)


def _gelu_tanh(x):
    return 0.5 * x * (1.0 + jnp.tanh(math.sqrt(2.0 / math.pi) * (x + 0.044715 * (x * x * x))))


def _merge_kernel(x_ref, gate_ref, ys_ref, yh_ref, ga_ref, gb_ref, wpa_lo_ref, wpa_hi_ref, wpb_ref,
                  wout_ref, o_ref, s_ref, acc_ref):
    j = pl.program_id(1)

    @pl.when(j == 0)
    def _():
        s_ref[...] = _gelu_tanh(ys_ref[...]).astype(BF16)
        acc_ref[...] = jnp.zeros_like(acc_ref)

    s = s_ref[...]
    pa_lo = jnp.dot(s, wpa_lo_ref[...], preferred_element_type=F32)
    pa_hi = jnp.dot(s, wpa_hi_ref[...], preferred_element_type=F32)
    y_a = pa_lo * jax.nn.sigmoid(pa_hi)
    y_b = jnp.dot(yh_ref[...].astype(BF16), wpb_ref[...], preferred_element_type=F32)
    m = ga_ref[...].astype(F32) * y_a + gb_ref[...].astype(F32) * y_b
    acc_ref[...] += jnp.dot(m.astype(BF16), wout_ref[...], preferred_element_type=F32)

    @pl.when(j == pl.num_programs(1) - 1)
    def _():
        o_ref[...] = x_ref[...] + gate_ref[...] * acc_ref[...]


def _merge(x, gate, y_s5, y_hy, sig_gates, w_pa, w_pb, w_out, tm=512, tn=512):
    t, d = x.shape
    bm = gate.shape[0]
    blocks_per_batch = (t // bm) // tm
    nj = d // tn
    ds5 = y_s5.shape[1]
    dhy = y_hy.shape[1]
    return pl.pallas_call(
        _merge_kernel,
        grid=(t // tm, nj),
        in_specs=[pl.BlockSpec((tm, d), lambda i, j: (i, 0)),
                  pl.BlockSpec((None, 1, d), lambda i, j: (i // blocks_per_batch, 0, 0)),
                  pl.BlockSpec((tm, ds5), lambda i, j: (i, 0)),
                  pl.BlockSpec((tm, dhy), lambda i, j: (i, 0)),
                  pl.BlockSpec((tm, tn), lambda i, j: (i, j)),
                  pl.BlockSpec((tm, tn), lambda i, j: (i, nj + j)),
                  pl.BlockSpec((ds5, tn), lambda i, j: (0, j)),
                  pl.BlockSpec((ds5, tn), lambda i, j: (0, nj + j)),
                  pl.BlockSpec((dhy, tn), lambda i, j: (0, j)),
                  pl.BlockSpec((tn, d), lambda i, j: (j, 0))],
        out_specs=pl.BlockSpec((tm, d), lambda i, j: (i, 0)),
        out_shape=jax.ShapeDtypeStruct((t, d), F32),
        scratch_shapes=[pltpu.VMEM((tm, ds5), BF16), pltpu.VMEM((tm, d), F32)],
        compiler_params=pltpu.CompilerParams(
            dimension_semantics=("parallel", "arbitrary"), vmem_limit_bytes=VMEM_LIMIT_BYTES),
        name="merge",
    )(x, gate, y_s5, y_hy, sig_gates, sig_gates, w_pa, w_pa, w_pb, w_out)


def _s5_weights(lam_re, lam_im, log_dt, b_re, b_im, c_re, c_im, d_skip, n_steps):
    hp = lax.Precision.HIGHEST
    t = S5_CHUNK
    g, p, h = S5_GROUPS, S5_STATE, S5_GROUP
    dt = jnp.exp(log_dt)[..., None]
    lr = jnp.minimum(lam_re, LAMBDA_RE_MAX)
    li = lam_im
    zr, zi = lr * dt, li * dt

    def apow(j):
        jj = j.astype(F32)[:, None, None, None]
        mag = jnp.exp(jj * zr)
        return mag * jnp.cos(jj * zi), mag * jnp.sin(jj * zi)

    ab_re, ab_im = apow(jnp.arange(1, 2))
    ab_re, ab_im = ab_re[0], ab_im[0]
    nr, ni = ab_re - 1.0, ab_im
    den = lr * lr + li * li
    f_re = (nr * lr + ni * li) / den
    f_im = (ni * lr - nr * li) / den
    bb_re = f_re[..., None] * b_re - f_im[..., None] * b_im
    bb_im = f_re[..., None] * b_im + f_im[..., None] * b_re

    pr, pi = apow(jnp.arange(t + 1))
    ca_re = c_re[None] * pr[:, :, :, None, :] - c_im[None] * pi[:, :, :, None, :]
    ca_im = c_re[None] * pi[:, :, :, None, :] + c_im[None] * pr[:, :, :, None, :]
    kk = (jnp.einsum('jdghp,dgpk->jdghk', ca_re, bb_re, precision=hp)
          - jnp.einsum('jdghp,dgpk->jdghk', ca_im, bb_im, precision=hp))

    ii = jnp.arange(t)
    lag = ii[None, :] - ii[:, None]
    kf = kk[jnp.clip(lag, 0, t), 0] * (lag >= 0)[:, :, None, None, None].astype(F32)
    kb = kk[jnp.clip(-lag, 0, t), 1] * (lag <= 0)[:, :, None, None, None].astype(F32)
    skip = d_skip.reshape(g, h)[:, :, None] * jnp.eye(h, dtype=F32)[None]
    diag = jnp.eye(t, dtype=F32)[:, :, None, None, None] * skip[None, None]
    toep = (kf + kb + diag).transpose(2, 0, 4, 1, 3).reshape(g, t * h, t * h)

    def bpow(pw_re, pw_im, d):
        re = pw_re[:, d, :, :, None] * bb_re[d][None] - pw_im[:, d, :, :, None] * bb_im[d][None]
        im = pw_re[:, d, :, :, None] * bb_im[d][None] + pw_im[:, d, :, :, None] * bb_re[d][None]
        fold = lambda a: a.transpose(1, 0, 3, 2).reshape(g, t * h, p)
        return fold(re), fold(im)

    bpw = jnp.concatenate(bpow(pr[t - 1 - ii], pi[t - 1 - ii], 0) + bpow(pr[ii], pi[ii], 1), axis=-1)

    def cpow(e, d):
        re = ca_re[e, d].transpose(1, 3, 0, 2).reshape(g, p, t * h)
        im = -ca_im[e, d].transpose(1, 3, 0, 2).reshape(g, p, t * h)
        return jnp.concatenate([re, im], axis=1)

    cpw = jnp.concatenate([cpow(ii + 1, 0), cpow(t - ii, 1)], axis=1)

    sr, si = apow(t * (2 ** jnp.arange(n_steps)))
    a1 = jnp.concatenate([sr, sr], axis=-1)
    a2 = jnp.concatenate([-si, si], axis=-1)
    tab = jnp.stack([a1, a2], axis=2).transpose(3, 1, 0, 2, 4).reshape(g, 4 * n_steps, 2 * p)
    return bpw, toep, cpw, tab


def _s5_kernel(ul_ref, uc_ref, bpow_ref, toep_ref, cpow_ref, tab_ref, y_ref, *, bsz, n_steps):
    n_lat = ul_ref.shape[0] // bsz
    n_ctx = uc_ref.shape[0] // bsz
    n_ch = n_lat + n_ctx
    rows = bsz * n_ch
    half = 2 * S5_STATE
    ul = ul_ref[...].astype(BF16)
    uc = uc_ref[...].astype(BF16)
    bpow = bpow_ref[...]
    zl = jnp.dot(ul, bpow, preferred_element_type=F32)
    zc = jnp.dot(uc, bpow, preferred_element_type=F32)
    fparts, bparts = [], []
    for b in range(bsz):
        lat = slice(b * n_lat, (b + 1) * n_lat)
        ctx = slice(b * n_ctx, (b + 1) * n_ctx)
        fparts += [zc[ctx, :half], zl[lat, :half]]
        bparts += [zl[lat, half:], zc[ctx, half:]]
    fw = jnp.concatenate(fparts, axis=0)
    bw = jnp.concatenate(bparts, axis=0)
    rib = lax.broadcasted_iota(jnp.int32, (rows, half), 0) % n_ch
    tab = tab_ref[...]

    def cmul_add(acc, sh, a1, a2):
        return acc + a1 * sh + a2 * pltpu.roll(sh, S5_STATE, 1)

    for s in range(n_steps):
        d = 1 << s
        sh = jnp.where(rib >= d, pltpu.roll(fw, d, 0), 0.0)
        fw = cmul_add(fw, sh, tab[2 * s:2 * s + 1], tab[2 * s + 1:2 * s + 2])
        o = 2 * n_steps
        sh = jnp.where(rib < n_ch - d, pltpu.roll(bw, rows - d, 0), 0.0)
        bw = cmul_add(bw, sh, tab[o + 2 * s:o + 2 * s + 1], tab[o + 2 * s + 1:o + 2 * s + 2])
    fe = jnp.where(rib >= 1, pltpu.roll(fw, 1, 0), 0.0)
    be = jnp.where(rib < n_ch - 1, pltpu.roll(bw, rows - 1, 0), 0.0)
    fl = jnp.concatenate([fe[b * n_ch + n_ctx:(b + 1) * n_ch] for b in range(bsz)], axis=0)
    bl = jnp.concatenate([be[b * n_ch:b * n_ch + n_lat] for b in range(bsz)], axis=0)
    st = jnp.concatenate([fl, bl], axis=1).astype(BF16)
    y_ref[...] = (jnp.dot(ul, toep_ref[...], preferred_element_type=F32)
                  + jnp.dot(st, cpow_ref[...], preferred_element_type=F32))


def _s5_mix(u2, u2c, bpw, toep, cpw, tab, bsz):
    g, rl, w = u2.shape
    rc = u2c.shape[1]
    n_steps = tab.shape[1] // 4
    wspec = pl.BlockSpec((None, w, w), lambda i: (i, 0, 0))
    return pl.pallas_call(
        functools.partial(_s5_kernel, bsz=bsz, n_steps=n_steps),
        grid=(g,),
        in_specs=[pl.BlockSpec((None, rl, w), lambda i: (i, 0, 0)),
                  pl.BlockSpec((None, rc, w), lambda i: (i, 0, 0)),
                  wspec, wspec, wspec,
                  pl.BlockSpec((None, 4 * n_steps, tab.shape[2]), lambda i: (i, 0, 0))],
        out_specs=pl.BlockSpec((None, rl, w), lambda i: (i, 0, 0)),
        out_shape=jax.ShapeDtypeStruct((g, rl, w), F32),
        compiler_params=pltpu.CompilerParams(
            dimension_semantics=("parallel",), vmem_limit_bytes=VMEM_LIMIT_BYTES),
        name="s5_mix",
    )(u2, u2c, bpw.astype(BF16), toep.astype(BF16), cpw.astype(BF16), tab)


def _s5_bidirectional(u, u_ctx, lam_re, lam_im, log_dt, b_re, b_im, c_re, c_im, d_skip):
    bsz, length, dm = u.shape
    ctx_len = u_ctx.shape[1]
    t, g, h = S5_CHUNK, S5_GROUPS, S5_GROUP
    n_lat, n_ctx = length // t, ctx_len // t
    n_steps = max(1, math.ceil(math.log2(n_lat + n_ctx)))

    def fold(a, n):
        return a.reshape(bsz, n, t, g, h).transpose(3, 0, 1, 2, 4).reshape(g, bsz * n, t * h)

    bpw, toep, cpw, tab = _s5_weights(lam_re, lam_im, log_dt, b_re, b_im, c_re, c_im, d_skip, n_steps)
    y2 = _s5_mix(fold(u, n_lat), fold(u_ctx, n_ctx), bpw, toep, cpw, tab, bsz)
    return y2.reshape(g, bsz, n_lat, t, h).transpose(1, 2, 3, 0, 4).reshape(bsz, length, dm)


def _short_conv_rows(u, w, b, n_rows, row_len):
    bsz, length, ch = u.shape
    ug = u.reshape(bsz, n_rows, row_len, ch)
    pad = HY_SHORT // 2
    up = jnp.pad(ug, ((0, 0), (0, 0), (pad, pad), (0, 0)))
    y = b + up[:, :, 0:row_len] * w[0]
    for j in range(1, HY_SHORT):
        y = y + up[:, :, j:j + row_len] * w[j]
    return y.reshape(bsz, length, ch)


def _hyena_filter_spectrum(length, w1, b1, w2, b2, w3, b3, freq, w_out):
    hp = lax.Precision.HIGHEST
    t = jnp.linspace(0.0, 1.0, length, dtype=F32)[:, None]
    w = (2.0 * math.pi / length) * jnp.arange(length, dtype=F32)[:, None]
    bands = jnp.linspace(1e-4, HY_BANDS - 1, HY_BANDS, dtype=F32)[None, :]
    feats = jnp.concatenate([t, jnp.cos(bands * w), -jnp.sin(bands * w)], axis=-1)
    hdn = jnp.sin(freq * (jnp.dot(feats, w1, precision=hp) + b1))
    hdn = jnp.sin(freq * (jnp.dot(hdn, w2, precision=hp) + b2))
    hdn = jnp.sin(freq * (jnp.dot(hdn, w3, precision=hp) + b3))
    k = jnp.dot(hdn, w_out, precision=hp)
    n_ch = k.shape[-1]
    deltas = jnp.abs(jnp.linspace(math.log(HY_TARGET) / HY_SLOW_PCT,
                                  math.log(HY_TARGET) / HY_FAST_PCT, n_ch, dtype=F32))
    k = (k * jnp.exp(-t * deltas)).reshape(length, HY_ORDER, HY_DIRS, D_HY)
    k_two = jnp.concatenate([k[:, :, 0], jnp.zeros((1, HY_ORDER, D_HY), F32), k[:0:-1, :, 1]], axis=0)
    k_two = k_two / jnp.sum(jnp.abs(k_two), axis=0, keepdims=True)
    return jnp.fft.rfft(k_two, axis=0)


def _hyena(u, n_rows, row_len, short_w, short_b, w1, b1, w2, b2, w3, b3, freq, w_out, bias):
    length = u.shape[1]
    us = _short_conv_rows(u, short_w, short_b, n_rows, row_len)
    parts = jnp.split(us, HY_ORDER + 1, axis=-1)
    z, gates = parts[0], parts[1:]
    k_f = _hyena_filter_spectrum(length, w1, b1, w2, b2, w3, b3, freq, w_out)
    n_fft = 2 * length
    for o in range(HY_ORDER):
        conv = jnp.fft.irfft(jnp.fft.rfft(z, n=n_fft, axis=1) * k_f[None, :, o], n=n_fft, axis=1)[:, :length]
        z = gates[o] * (conv + bias[o] * z)
    return z


def kernel(x, c, ctx, c_ctx, w_ada, b_ada, norm_g, ffn_w_gate, ffn_w_up, ffn_w_down, w_in,
           s5_lam_re, s5_lam_im, s5_log_dt, s5_b_re, s5_b_im, s5_c_re, s5_c_im, s5_d,
           hy_short_w, hy_short_b, hy_w1, hy_b1, hy_w2, hy_b2, hy_w3, hy_b3, hy_freq, hy_w_out,
           hy_bias, w_pa, w_pb, w_out, final_g):
    bsz, seq, d = x.shape
    ctx_len = ctx.shape[1]
    n_rows = seq // GRID_W
    depth = w_ada.shape[0]
    assert depth == 1, "context-token outputs are only dropped by the last layer"
    l = 0

    c_rows = jnp.concatenate([c, c_ctx[None, :], jnp.zeros((8 - bsz - 1, d), F32)], axis=0)
    mod_all = _ada_mod(c_rows, w_ada[l], b_ada[l])
    mod = mod_all[:bsz].reshape(bsz, N_SUB, N_MOD, 1, d)
    mod_c = mod_all[bsz:bsz + 1].reshape(1, N_SUB, N_MOD, 1, d)

    def mods(m, sub):
        return tuple(m[:, sub, k] for k in range(N_MOD))

    wg = ffn_w_gate[l].astype(BF16)
    wu = ffn_w_up[l].astype(BF16)
    wd = ffn_w_down[l].astype(BF16)
    w_in_b = w_in[l].astype(BF16)

    xt = x.reshape(bsz * seq, d)
    ct = ctx.reshape(bsz * ctx_len, d)

    xt = _ffn_sublayer(xt, mods(mod, 0), norm_g[l, 0], wg[0], wu[0], wd[0])
    ct = _ffn_sublayer(ct, mods(mod_c, 0), norm_g[l, 0], wg[0], wu[0], wd[0])

    uhy, sig_gates = _in_proj(xt, mod[:, 1, 0], mod[:, 1, 1], norm_g[l, 1], w_in_b, I_GA)
    (u_ctx,) = _in_proj(ct, mod_c[:, 1, 0], mod_c[:, 1, 1], norm_g[l, 1], w_in_b[:, :D_S5], D_S5)

    proj = uhy.reshape(bsz, seq, I_GA)
    y_s5 = _s5_bidirectional(proj[..., :I_HY], u_ctx.reshape(bsz, ctx_len, D_S5),
                             s5_lam_re[l], s5_lam_im[l], s5_log_dt[l],
                             s5_b_re[l], s5_b_im[l], s5_c_re[l], s5_c_im[l], s5_d[l])
    y_hy = _hyena(proj[..., I_HY:I_GA], n_rows, GRID_W, hy_short_w[l], hy_short_b[l],
                  hy_w1[l], hy_b1[l], hy_w2[l], hy_b2[l], hy_w3[l], hy_b3[l], hy_freq[l],
                  hy_w_out[l], hy_bias[l])

    xt = _merge(xt, mod[:, 1, 2], y_s5.reshape(bsz * seq, D_S5), y_hy.reshape(bsz * seq, D_HY),
                sig_gates, w_pa[l].astype(BF16), w_pb[l].astype(BF16), w_out[l].astype(BF16))

    xt = _ffn_sublayer(xt, mods(mod, 2), norm_g[l, 2], wg[1], wu[1], wd[1], final_gain=final_g)
    return xt.reshape(bsz, seq, d)
```

```python
import functools
import math

import jax
import jax.numpy as jnp
import numpy as np
from jax import lax
from jax.experimental import pallas as pl
from jax.experimental.pallas import tpu as pltpu

F32 = jnp.float32
BF16 = jnp.bfloat16

D_MODEL = 2048
GRID_W = 64
D_S5 = 1024
S5_GROUP = 16
S5_GROUPS = D_S5 // S5_GROUP
S5_STATE = 64
LAMBDA_RE_MAX = -1e-4
S5_CHUNK = 16
D_HY = 1024
HY_ORDER = 2
HY_DIRS = 2
HY_SHORT = 3
HY_EMB = 33
HY_BANDS = (HY_EMB - 1) // 2
HY_TARGET = 1e-2
HY_FAST_PCT = 0.3
HY_SLOW_PCT = 1.5
FFT_N1 = 64
FFT_N2 = 128
FFT_N = FFT_N1 * FFT_N2
I_HY = D_S5
I_GA = D_S5 + (HY_ORDER + 1) * D_HY
I_GB = I_GA + D_MODEL
D_IN = I_GB + D_MODEL
D_FF = 5632
N_SUB = 3
N_MOD = 3
HALF_STEP = 0.5
RMS_EPS = 1e-6

VMEM_LIMIT_BYTES = 56 * 1024 * 1024


def _rms_mod(x, gain, shift, scale):
    ms = jnp.mean(x * x, axis=-1, keepdims=True)
    y = x * lax.rsqrt(ms + RMS_EPS) * gain
    return y * (1.0 + scale) + shift


def _ada_kernel(c_ref, w_ref, b_ref, o_ref):
    c = c_ref[...]
    a = c * jax.nn.sigmoid(c)
    o_ref[...] = jnp.dot(a, w_ref[...], preferred_element_type=F32,
                         precision=lax.Precision.HIGHEST) + b_ref[...]


def _ada_mod(c_rows, w, b, tn=1024):
    rows, d = c_rows.shape
    n = w.shape[1]
    return pl.pallas_call(
        _ada_kernel,
        grid=(n // tn,),
        in_specs=[pl.BlockSpec((rows, d), lambda j: (0, 0)),
                  pl.BlockSpec((d, tn), lambda j: (0, j)),
                  pl.BlockSpec((1, tn), lambda j: (0, j))],
        out_specs=pl.BlockSpec((rows, tn), lambda j: (0, j)),
        out_shape=jax.ShapeDtypeStruct((rows, n), F32),
        compiler_params=pltpu.CompilerParams(
            dimension_semantics=("arbitrary",), vmem_limit_bytes=VMEM_LIMIT_BYTES),
        name="ada_mod",
    )(c_rows, w, b.reshape(1, n))


def _ffn_kernel(x_ref, shift_ref, scale_ref, gate_ref, gain_ref, wg_ref, wu_ref, wd_ref,
                fg_ref, o_ref, h_ref, acc_ref, *, final_norm):
    j = pl.program_id(1)

    @pl.when(j == 0)
    def _():
        h_ref[...] = _rms_mod(x_ref[...], gain_ref[...], shift_ref[...], scale_ref[...]).astype(BF16)
        acc_ref[...] = jnp.zeros_like(acc_ref)

    h = h_ref[...]
    g = jnp.dot(h, wg_ref[...], preferred_element_type=F32)
    u = jnp.dot(h, wu_ref[...], preferred_element_type=F32)
    a = (g * jax.nn.sigmoid(g) * u).astype(BF16)
    acc_ref[...] += jnp.dot(a, wd_ref[...], preferred_element_type=F32)

    @pl.when(j == pl.num_programs(1) - 1)
    def _():
        y = x_ref[...] + (HALF_STEP * gate_ref[...]) * acc_ref[...]
        if final_norm:
            ms = jnp.mean(y * y, axis=-1, keepdims=True)
            y = y * lax.rsqrt(ms + RMS_EPS) * fg_ref[...]
        o_ref[...] = y


def _ffn_sublayer(x, mods, gain, wg, wu, wd, final_gain=None, tm=512, tf=512):
    t, d = x.shape
    bm = mods[0].shape[0]
    blocks_per_batch = (t // bm) // tm
    dff = wg.shape[1]
    final_norm = final_gain is not None
    fg = final_gain if final_norm else gain
    mod_spec = pl.BlockSpec((None, 1, d), lambda i, j: (i // blocks_per_batch, 0, 0))
    vec_spec = pl.BlockSpec((1, d), lambda i, j: (0, 0))
    return pl.pallas_call(
        functools.partial(_ffn_kernel, final_norm=final_norm),
        grid=(t // tm, dff // tf),
        in_specs=[pl.BlockSpec((tm, d), lambda i, j: (i, 0)),
                  mod_spec, mod_spec, mod_spec, vec_spec,
                  pl.BlockSpec((d, tf), lambda i, j: (0, j)),
                  pl.BlockSpec((d, tf), lambda i, j: (0, j)),
                  pl.BlockSpec((tf, d), lambda i, j: (j, 0)),
                  vec_spec],
        out_specs=pl.BlockSpec((tm, d), lambda i, j: (i, 0)),
        out_shape=jax.ShapeDtypeStruct((t, d), F32),
        scratch_shapes=[pltpu.VMEM((tm, d), BF16), pltpu.VMEM((tm, d), F32)],
        compiler_params=pltpu.CompilerParams(
            dimension_semantics=("parallel", "arbitrary"), vmem_limit_bytes=VMEM_LIMIT_BYTES),
        name="ffn_final" if final_norm else "ffn",
    )(x, *mods, gain.reshape(1, d), wg, wu, wd, fg.reshape(1, d))


def _proj_kernel(x_ref, shift_ref, scale_ref, gain_ref, w_ref, *rest, n32, n16):
    o_refs, h_ref = rest[:-1], rest[-1]
    j = pl.program_id(1)

    @pl.when(j == 0)
    def _():
        h_ref[...] = _rms_mod(x_ref[...], gain_ref[...], shift_ref[...], scale_ref[...]).astype(BF16)

    p = jnp.dot(h_ref[...], w_ref[...], preferred_element_type=F32)
    if n16 == 0:
        o_refs[0][...] = p
    else:
        @pl.when(j < n32)
        def _():
            o_refs[0][...] = p

        @pl.when(j >= n32)
        def _():
            o_refs[1][...] = jax.nn.sigmoid(p).astype(BF16)


def _in_proj(x, shift, scale, gain, w, n_plain, tm=512, tn=512):
    t, d = x.shape
    bm = shift.shape[0]
    blocks_per_batch = (t // bm) // tm
    n = w.shape[1]
    n32 = n_plain // tn
    n16 = (n - n_plain) // tn
    mod_spec = pl.BlockSpec((None, 1, d), lambda i, j: (i // blocks_per_batch, 0, 0))
    out_shape = [jax.ShapeDtypeStruct((t, n_plain), F32)]
    out_specs = [pl.BlockSpec((tm, tn), lambda i, j: (i, jnp.minimum(j, n32 - 1)))]
    if n16:
        out_shape.append(jax.ShapeDtypeStruct((t, n - n_plain), BF16))
        out_specs.append(pl.BlockSpec((tm, tn), lambda i, j: (i, jnp.maximum(j - n32, 0))))
    return pl.pallas_call(
        functools.partial(_proj_kernel, n32=n32, n16=n16),
        grid=(t // tm, n // tn),
        in_specs=[pl.BlockSpec((tm, d), lambda i, j: (i, 0)),
                  mod_spec, mod_spec,
                  pl.BlockSpec((1, d), lambda i, j: (0, 0)),
                  pl.BlockSpec((d, tn), lambda i, j: (0, j))],
        out_specs=out_specs,
        out_shape=out_shape,
        scratch_shapes=[pltpu.VMEM((tm, d), BF16)],
        compiler_params=pltpu.CompilerParams(
            dimension_semantics=("parallel", "arbitrary"), vmem_limit_bytes=VMEM_LIMIT_BYTES),
        name="in_proj",
    )(x, shift, scale, gain.reshape(1, d), w)


def _gelu_tanh(x):
    return 0.5 * x * (1.0 + jnp.tanh(math.sqrt(2.0 / math.pi) * (x + 0.044715 * (x * x * x))))


def _merge_kernel(x_ref, gate_ref, ys_ref, yh_ref, ga_ref, gb_ref, wpa_lo_ref, wpa_hi_ref, wpb_ref,
                  wout_ref, o_ref, s_ref, acc_ref):
    j = pl.program_id(1)

    @pl.when(j == 0)
    def _():
        s_ref[...] = _gelu_tanh(ys_ref[...]).astype(BF16)
        acc_ref[...] = jnp.zeros_like(acc_ref)

    s = s_ref[...]
    pa_lo = jnp.dot(s, wpa_lo_ref[...], preferred_element_type=F32)
    pa_hi = jnp.dot(s, wpa_hi_ref[...], preferred_element_type=F32)
    y_a = pa_lo * jax.nn.sigmoid(pa_hi)
    y_b = jnp.dot(yh_ref[...].astype(BF16), wpb_ref[...], preferred_element_type=F32)
    m = ga_ref[...].astype(F32) * y_a + gb_ref[...].astype(F32) * y_b
    acc_ref[...] += jnp.dot(m.astype(BF16), wout_ref[...], preferred_element_type=F32)

    @pl.when(j == pl.num_programs(1) - 1)
    def _():
        o_ref[...] = x_ref[...] + gate_ref[...] * acc_ref[...]


def _merge(x, gate, y_s5, y_hy, sig_gates, w_pa, w_pb, w_out, tm=512, tn=512):
    t, d = x.shape
    bm = gate.shape[0]
    blocks_per_batch = (t // bm) // tm
    nj = d // tn
    ds5 = y_s5.shape[1]
    dhy = y_hy.shape[1]
    return pl.pallas_call(
        _merge_kernel,
        grid=(t // tm, nj),
        in_specs=[pl.BlockSpec((tm, d), lambda i, j: (i, 0)),
                  pl.BlockSpec((None, 1, d), lambda i, j: (i // blocks_per_batch, 0, 0)),
                  pl.BlockSpec((tm, ds5), lambda i, j: (i, 0)),
                  pl.BlockSpec((tm, dhy), lambda i, j: (i, 0)),
                  pl.BlockSpec((tm, tn), lambda i, j: (i, j)),
                  pl.BlockSpec((tm, tn), lambda i, j: (i, nj + j)),
                  pl.BlockSpec((ds5, tn), lambda i, j: (0, j)),
                  pl.BlockSpec((ds5, tn), lambda i, j: (0, nj + j)),
                  pl.BlockSpec((dhy, tn), lambda i, j: (0, j)),
                  pl.BlockSpec((tn, d), lambda i, j: (j, 0))],
        out_specs=pl.BlockSpec((tm, d), lambda i, j: (i, 0)),
        out_shape=jax.ShapeDtypeStruct((t, d), F32),
        scratch_shapes=[pltpu.VMEM((tm, ds5), BF16), pltpu.VMEM((tm, d), F32)],
        compiler_params=pltpu.CompilerParams(
            dimension_semantics=("parallel", "arbitrary"), vmem_limit_bytes=VMEM_LIMIT_BYTES),
        name="merge",
    )(x, gate, y_s5, y_hy, sig_gates, sig_gates, w_pa, w_pa, w_pb, w_out)


def _s5_weights(lam_re, lam_im, log_dt, b_re, b_im, c_re, c_im, d_skip, n_steps):
    hp = lax.Precision.HIGHEST
    t = S5_CHUNK
    g, p, h = S5_GROUPS, S5_STATE, S5_GROUP
    dt = jnp.exp(log_dt)[..., None]
    lr = jnp.minimum(lam_re, LAMBDA_RE_MAX)
    li = lam_im
    zr, zi = lr * dt, li * dt

    def apow(j):
        jj = j.astype(F32)[:, None, None, None]
        mag = jnp.exp(jj * zr)
        return mag * jnp.cos(jj * zi), mag * jnp.sin(jj * zi)

    ab_re, ab_im = apow(jnp.arange(1, 2))
    ab_re, ab_im = ab_re[0], ab_im[0]
    nr, ni = ab_re - 1.0, ab_im
    den = lr * lr + li * li
    f_re = (nr * lr + ni * li) / den
    f_im = (ni * lr - nr * li) / den
    bb_re = f_re[..., None] * b_re - f_im[..., None] * b_im
    bb_im = f_re[..., None] * b_im + f_im[..., None] * b_re

    pr, pi = apow(jnp.arange(t + 1))
    ca_re = c_re[None] * pr[:, :, :, None, :] - c_im[None] * pi[:, :, :, None, :]
    ca_im = c_re[None] * pi[:, :, :, None, :] + c_im[None] * pr[:, :, :, None, :]
    kk = (jnp.einsum('jdghp,dgpk->jdghk', ca_re, bb_re, precision=hp)
          - jnp.einsum('jdghp,dgpk->jdghk', ca_im, bb_im, precision=hp))

    ii = jnp.arange(t)
    lag = ii[None, :] - ii[:, None]
    kf = kk[jnp.clip(lag, 0, t), 0] * (lag >= 0)[:, :, None, None, None].astype(F32)
    kb = kk[jnp.clip(-lag, 0, t), 1] * (lag <= 0)[:, :, None, None, None].astype(F32)
    skip = d_skip.reshape(g, h)[:, :, None] * jnp.eye(h, dtype=F32)[None]
    diag = jnp.eye(t, dtype=F32)[:, :, None, None, None] * skip[None, None]
    toep = (kf + kb + diag).transpose(2, 0, 4, 1, 3).reshape(g, t * h, t * h)

    def bpow(pw_re, pw_im, d):
        re = pw_re[:, d, :, :, None] * bb_re[d][None] - pw_im[:, d, :, :, None] * bb_im[d][None]
        im = pw_re[:, d, :, :, None] * bb_im[d][None] + pw_im[:, d, :, :, None] * bb_re[d][None]
        fold = lambda a: a.transpose(1, 0, 3, 2).reshape(g, t * h, p)
        return fold(re), fold(im)

    bpw = jnp.concatenate(bpow(pr[t - 1 - ii], pi[t - 1 - ii], 0) + bpow(pr[ii], pi[ii], 1), axis=-1)

    def cpow(e, d):
        re = ca_re[e, d].transpose(1, 3, 0, 2).reshape(g, p, t * h)
        im = -ca_im[e, d].transpose(1, 3, 0, 2).reshape(g, p, t * h)
        return jnp.concatenate([re, im], axis=1)

    cpw = jnp.concatenate([cpow(ii + 1, 0), cpow(t - ii, 1)], axis=1)

    sr, si = apow(t * (2 ** jnp.arange(n_steps)))
    a1 = jnp.concatenate([sr, sr], axis=-1)
    a2 = jnp.concatenate([-si, si], axis=-1)
    tab = jnp.stack([a1, a2], axis=2).transpose(3, 1, 0, 2, 4).reshape(g, 4 * n_steps, 2 * p)
    return bpw, toep, cpw, tab


def _s5_kernel(ul_ref, uc_ref, bpow_ref, toep_ref, cpow_ref, tab_ref, y_ref, *, bsz, n_steps):
    n_lat = ul_ref.shape[0] // bsz
    n_ctx = uc_ref.shape[0] // bsz
    n_ch = n_lat + n_ctx
    rows = bsz * n_ch
    half = 2 * S5_STATE
    ul = ul_ref[...].astype(BF16)
    uc = uc_ref[...].astype(BF16)
    bpow = bpow_ref[...]
    zl = jnp.dot(ul, bpow, preferred_element_type=F32)
    zc = jnp.dot(uc, bpow, preferred_element_type=F32)
    fparts, bparts = [], []
    for b in range(bsz):
        lat = slice(b * n_lat, (b + 1) * n_lat)
        ctx = slice(b * n_ctx, (b + 1) * n_ctx)
        fparts += [zc[ctx, :half], zl[lat, :half]]
        bparts += [zl[lat, half:], zc[ctx, half:]]
    fw = jnp.concatenate(fparts, axis=0)
    bw = jnp.concatenate(bparts, axis=0)
    rib = lax.broadcasted_iota(jnp.int32, (rows, half), 0) % n_ch
    tab = tab_ref[...]

    def cmul_add(acc, sh, a1, a2):
        return acc + a1 * sh + a2 * pltpu.roll(sh, S5_STATE, 1)

    for s in range(n_steps):
        d = 1 << s
        sh = jnp.where(rib >= d, pltpu.roll(fw, d, 0), 0.0)
        fw = cmul_add(fw, sh, tab[2 * s:2 * s + 1], tab[2 * s + 1:2 * s + 2])
        o = 2 * n_steps
        sh = jnp.where(rib < n_ch - d, pltpu.roll(bw, rows - d, 0), 0.0)
        bw = cmul_add(bw, sh, tab[o + 2 * s:o + 2 * s + 1], tab[o + 2 * s + 1:o + 2 * s + 2])
    fe = jnp.where(rib >= 1, pltpu.roll(fw, 1, 0), 0.0)
    be = jnp.where(rib < n_ch - 1, pltpu.roll(bw, rows - 1, 0), 0.0)
    fl = jnp.concatenate([fe[b * n_ch + n_ctx:(b + 1) * n_ch] for b in range(bsz)], axis=0)
    bl = jnp.concatenate([be[b * n_ch:b * n_ch + n_lat] for b in range(bsz)], axis=0)
    st = jnp.concatenate([fl, bl], axis=1).astype(BF16)
    y_ref[...] = (jnp.dot(ul, toep_ref[...], preferred_element_type=F32)
                  + jnp.dot(st, cpow_ref[...], preferred_element_type=F32))


def _s5_mix(u2, u2c, bpw, toep, cpw, tab, bsz):
    g, rl, w = u2.shape
    rc = u2c.shape[1]
    n_steps = tab.shape[1] // 4
    wspec = pl.BlockSpec((None, w, w), lambda i: (i, 0, 0))
    return pl.pallas_call(
        functools.partial(_s5_kernel, bsz=bsz, n_steps=n_steps),
        grid=(g,),
        in_specs=[pl.BlockSpec((None, rl, w), lambda i: (i, 0, 0)),
                  pl.BlockSpec((None, rc, w), lambda i: (i, 0, 0)),
                  wspec, wspec, wspec,
                  pl.BlockSpec((None, 4 * n_steps, tab.shape[2]), lambda i: (i, 0, 0))],
        out_specs=pl.BlockSpec((None, rl, w), lambda i: (i, 0, 0)),
        out_shape=jax.ShapeDtypeStruct((g, rl, w), F32),
        compiler_params=pltpu.CompilerParams(
            dimension_semantics=("parallel",), vmem_limit_bytes=VMEM_LIMIT_BYTES),
        name="s5_mix",
    )(u2, u2c, bpw.astype(BF16), toep.astype(BF16), cpw.astype(BF16), tab)


def _s5_bidirectional(u, u_ctx, lam_re, lam_im, log_dt, b_re, b_im, c_re, c_im, d_skip):
    bsz, length, dm = u.shape
    ctx_len = u_ctx.shape[1]
    t, g, h = S5_CHUNK, S5_GROUPS, S5_GROUP
    n_lat, n_ctx = length // t, ctx_len // t
    n_steps = max(1, math.ceil(math.log2(n_lat + n_ctx)))

    def fold(a, n):
        return a.reshape(bsz, n, t, g, h).transpose(3, 0, 1, 2, 4).reshape(g, bsz * n, t * h)

    bpw, toep, cpw, tab = _s5_weights(lam_re, lam_im, log_dt, b_re, b_im, c_re, c_im, d_skip, n_steps)
    y2 = _s5_mix(fold(u, n_lat), fold(u_ctx, n_ctx), bpw, toep, cpw, tab, bsz)
    return y2.reshape(g, bsz, n_lat, t, h).transpose(1, 2, 3, 0, 4).reshape(bsz, length, dm)


def _dft_constants():
    n1 = np.arange(FFT_N1)
    n2 = np.arange(FFT_N2)
    half = FFT_N1 // 2
    th = 2 * np.pi * np.outer(n1, n1) / FFT_N1
    c1, s1 = np.cos(th), np.sin(th)
    w1 = np.concatenate([np.concatenate([c1[:half], -s1[:half]], axis=1),
                         np.concatenate([s1[:half], c1[:half]], axis=1)], axis=0)
    z = np.zeros_like(w1)
    w1p = np.block([[w1, z], [z, w1]])
    ph = 2 * np.pi * np.outer(n2, n1) / FFT_N
    t1 = np.concatenate([np.cos(ph), np.cos(ph)], axis=1)
    t2 = np.concatenate([np.sin(ph), -np.sin(ph)], axis=1)
    ps = 2 * np.pi * np.outer(n2, n2) / FFT_N2
    f2 = np.concatenate([np.cos(ps), -np.sin(ps)], axis=1)
    g2 = np.concatenate([np.cos(ps), np.sin(ps)], axis=1)
    c2, s2 = np.cos(ph).T, np.sin(ph).T
    wi = np.concatenate([np.concatenate([c1[:, :half], s1[:, :half]], axis=1),
                         np.concatenate([-s1[:, :half], c1[:, :half]], axis=1)], axis=0) / FFT_N
    zi = np.zeros_like(wi)
    wi2 = np.stack([np.concatenate([wi, zi], axis=1), np.concatenate([zi, wi], axis=1)])
    as_b = lambda a: jnp.asarray(a, F32).astype(BF16)
    as_f = lambda a: jnp.asarray(a, F32)
    return [as_b(w1p), as_f(t1), as_f(t2), as_b(f2), as_b(g2), as_f(c2), as_f(s2), as_b(wi2)]


def _fwd_spectrum(xp, w1p, t1, t2, f2):
    cp = xp.shape[0]
    hn = FFT_N1
    a = jnp.dot(xp.reshape(cp * FFT_N2, 128).astype(BF16), w1p, preferred_element_type=F32)
    out = []
    for par in range(2):
        ap = a[:, par * 128:(par + 1) * 128]
        ap = ap.reshape(cp, FFT_N2, 128) * t1 + pltpu.roll(ap, hn, 1).reshape(cp, FFT_N2, 128) * t2
        at = jnp.swapaxes(ap, 1, 2)
        p = jnp.dot(at.reshape(cp * 128, FFT_N2).astype(BF16), f2, preferred_element_type=F32)
        p = p.reshape(cp, 128, 2 * FFT_N2)
        out.append((p[:, :hn, :FFT_N2] - p[:, hn:, FFT_N2:], p[:, :hn, FFT_N2:] + p[:, hn:, :FFT_N2]))
    return out


def _inv_time(yre, yim, g2, c2, s2, wi_par):
    cp = yre.shape[0]
    hn = FFT_N1
    y = jnp.concatenate([yre, yim], axis=1).reshape(cp * 128, FFT_N2).astype(BF16)
    q = jnp.dot(y, g2, preferred_element_type=F32).reshape(cp, 128, 2 * FFT_N2)
    bre = q[:, :hn, :FFT_N2] - q[:, hn:, FFT_N2:]
    bim = q[:, :hn, FFT_N2:] + q[:, hn:, :FFT_N2]
    b2 = jnp.concatenate([bre * c2 - bim * s2, bre * s2 + bim * c2], axis=1)
    bt = jnp.swapaxes(b2, 1, 2)
    return jnp.dot(bt.reshape(cp * FFT_N2, 128).astype(BF16), wi_par, preferred_element_type=F32)


def _hyena_kernel(z_ref, g1_ref, g2_ref, kf_ref, bias_ref, w1p_ref, t1_ref, t2_ref, f2_ref, gi_ref, c2_ref,
                  s2_ref, wi_ref, o_ref):
    cp = z_ref.shape[0]
    z = z_ref[...]
    gates = (g1_ref, g2_ref)
    for o in range(HY_ORDER):
        spec = _fwd_spectrum(z, w1p_ref[...], t1_ref[...], t2_ref[...], f2_ref[...])
        conv = None
        for par in range(2):
            xre, xim = spec[par]
            kre = kf_ref[o, par, :, :FFT_N1, :]
            kim = kf_ref[o, par, :, FFT_N1:, :]
            part = _inv_time(xre * kre - xim * kim, xre * kim + xim * kre,
                             gi_ref[...], c2_ref[...], s2_ref[...], wi_ref[par])
            conv = part if conv is None else conv + part
        z = gates[o][...] * (conv.reshape(cp, FFT_N2, 128) + bias_ref[o] * z)
    o_ref[...] = z


def _to_packed(v, bsz):
    _, length, ch = v.shape
    nn1 = length // FFT_N2
    return v.reshape(bsz, nn1, FFT_N2, ch // 2, 2).transpose(3, 2, 4, 0, 1).reshape(ch // 2, FFT_N2, 2 * bsz * nn1)


def _from_packed(vp, bsz):
    chp, _, lanes = vp.shape
    nn1 = lanes // (2 * bsz)
    return vp.reshape(chp, FFT_N2, 2, bsz, nn1).transpose(3, 4, 1, 0, 2).reshape(bsz, nn1 * FFT_N2, chp * 2)


def _hyena_conv(zp, g1p, g2p, kf, biasp, cp=8):
    chp = zp.shape[0]
    consts = _dft_constants()
    dspec = pl.BlockSpec((cp, FFT_N2, 128), lambda i: (i, 0, 0))
    full = lambda a: pl.BlockSpec(a.shape, lambda i: (0,) * a.ndim)
    return pl.pallas_call(
        _hyena_kernel,
        grid=(chp // cp,),
        in_specs=[dspec, dspec, dspec,
                  pl.BlockSpec((HY_ORDER, 2, cp, 128, FFT_N2), lambda i: (0, 0, i, 0, 0)),
                  pl.BlockSpec((HY_ORDER, cp, 1, 128), lambda i: (0, i, 0, 0))] + [full(a) for a in consts],
        out_specs=dspec,
        out_shape=jax.ShapeDtypeStruct(zp.shape, F32),
        compiler_params=pltpu.CompilerParams(
            dimension_semantics=("parallel",), vmem_limit_bytes=VMEM_LIMIT_BYTES),
        name="hyena_conv",
    )(zp, g1p, g2p, kf, biasp, *consts)


def _short_conv_rows(u, w, b, n_rows, row_len):
    bsz, length, ch = u.shape
    ug = u.reshape(bsz, n_rows, row_len, ch)
    pad = HY_SHORT // 2
    up = jnp.pad(ug, ((0, 0), (0, 0), (pad, pad), (0, 0)))
    y = b + up[:, :, 0:row_len] * w[0]
    for j in range(1, HY_SHORT):
        y = y + up[:, :, j:j + row_len] * w[j]
    return y.reshape(bsz, length, ch)


def _hyena_filter_spectrum(length, w1, b1, w2, b2, w3, b3, freq, w_out):
    hp = lax.Precision.HIGHEST
    t = jnp.linspace(0.0, 1.0, length, dtype=F32)[:, None]
    w = (2.0 * math.pi / length) * jnp.arange(length, dtype=F32)[:, None]
    bands = jnp.linspace(1e-4, HY_BANDS - 1, HY_BANDS, dtype=F32)[None, :]
    feats = jnp.concatenate([t, jnp.cos(bands * w), -jnp.sin(bands * w)], axis=-1)
    hdn = jnp.sin(freq * (jnp.dot(feats, w1, precision=hp) + b1))
    hdn = jnp.sin(freq * (jnp.dot(hdn, w2, precision=hp) + b2))
    hdn = jnp.sin(freq * (jnp.dot(hdn, w3, precision=hp) + b3))
    k = jnp.dot(hdn, w_out, precision=hp)
    n_ch = k.shape[-1]
    deltas = jnp.abs(jnp.linspace(math.log(HY_TARGET) / HY_SLOW_PCT,
                                  math.log(HY_TARGET) / HY_FAST_PCT, n_ch, dtype=F32))
    k = (k * jnp.exp(-t * deltas)).reshape(length, HY_ORDER, HY_DIRS, D_HY)
    k_two = jnp.concatenate([k[:, :, 0], jnp.zeros((1, HY_ORDER, D_HY), F32), k[:0:-1, :, 1]], axis=0)
    k_two = k_two / jnp.sum(jnp.abs(k_two), axis=0, keepdims=True)
    kh = jnp.fft.fft(k_two, axis=0)
    kh = jnp.stack([jnp.real(kh), jnp.imag(kh)])
    kf = kh.reshape(2, FFT_N2, FFT_N1, HY_ORDER, D_HY // 2, 2).transpose(3, 5, 4, 0, 2, 1)
    return kf.reshape(HY_ORDER, 2, D_HY // 2, 2 * FFT_N1, FFT_N2)


def _hyena(u, n_rows, row_len, short_w, short_b, w1, b1, w2, b2, w3, b3, freq, w_out, bias):
    bsz, length, _ = u.shape
    assert 2 * length == FFT_N and bsz == 2, "one complex transform carries exactly two batch rows"
    us = _short_conv_rows(u, short_w, short_b, n_rows, row_len)
    zp, g1p, g2p = (_to_packed(us[..., i * D_HY:(i + 1) * D_HY], bsz) for i in range(HY_ORDER + 1))
    kf = _hyena_filter_spectrum(length, w1, b1, w2, b2, w3, b3, freq, w_out)
    biasp = jnp.repeat(bias.reshape(HY_ORDER, D_HY // 2, 1, 2), FFT_N1, axis=-1)
    return _from_packed(_hyena_conv(zp, g1p, g2p, kf, biasp), bsz)


def kernel(x, c, ctx, c_ctx, w_ada, b_ada, norm_g, ffn_w_gate, ffn_w_up, ffn_w_down, w_in,
           s5_lam_re, s5_lam_im, s5_log_dt, s5_b_re, s5_b_im, s5_c_re, s5_c_im, s5_d,
           hy_short_w, hy_short_b, hy_w1, hy_b1, hy_w2, hy_b2, hy_w3, hy_b3, hy_freq, hy_w_out,
           hy_bias, w_pa, w_pb, w_out, final_g):
    bsz, seq, d = x.shape
    ctx_len = ctx.shape[1]
    n_rows = seq // GRID_W
    depth = w_ada.shape[0]
    assert depth == 1, "context-token outputs are only dropped by the last layer"
    l = 0

    c_rows = jnp.concatenate([c, c_ctx[None, :], jnp.zeros((8 - bsz - 1, d), F32)], axis=0)
    mod_all = _ada_mod(c_rows, w_ada[l], b_ada[l])
    mod = mod_all[:bsz].reshape(bsz, N_SUB, N_MOD, 1, d)
    mod_c = mod_all[bsz:bsz + 1].reshape(1, N_SUB, N_MOD, 1, d)

    def mods(m, sub):
        return tuple(m[:, sub, k] for k in range(N_MOD))

    wg = ffn_w_gate[l].astype(BF16)
    wu = ffn_w_up[l].astype(BF16)
    wd = ffn_w_down[l].astype(BF16)
    w_in_b = w_in[l].astype(BF16)

    xt = x.reshape(bsz * seq, d)
    ct = ctx.reshape(bsz * ctx_len, d)

    xt = _ffn_sublayer(xt, mods(mod, 0), norm_g[l, 0], wg[0], wu[0], wd[0])
    ct = _ffn_sublayer(ct, mods(mod_c, 0), norm_g[l, 0], wg[0], wu[0], wd[0])

    uhy, sig_gates = _in_proj(xt, mod[:, 1, 0], mod[:, 1, 1], norm_g[l, 1], w_in_b, I_GA)
    (u_ctx,) = _in_proj(ct, mod_c[:, 1, 0], mod_c[:, 1, 1], norm_g[l, 1], w_in_b[:, :D_S5], D_S5)

    proj = uhy.reshape(bsz, seq, I_GA)
    y_s5 = _s5_bidirectional(proj[..., :I_HY], u_ctx.reshape(bsz, ctx_len, D_S5),
                             s5_lam_re[l], s5_lam_im[l], s5_log_dt[l],
                             s5_b_re[l], s5_b_im[l], s5_c_re[l], s5_c_im[l], s5_d[l])
    y_hy = _hyena(proj[..., I_HY:I_GA], n_rows, GRID_W, hy_short_w[l], hy_short_b[l],
                  hy_w1[l], hy_b1[l], hy_w2[l], hy_b2[l], hy_w3[l], hy_b3[l], hy_freq[l],
                  hy_w_out[l], hy_bias[l])

    xt = _merge(xt, mod[:, 1, 2], y_s5.reshape(bsz * seq, D_S5), y_hy.reshape(bsz * seq, D_HY),
                sig_gates, w_pa[l].astype(BF16), w_pb[l].astype(BF16), w_out[l].astype(BF16))

    xt = _ffn_sublayer(xt, mods(mod, 2), norm_g[l, 2], wg[1], wu[1], wd[1], final_gain=final_g)
    return xt.reshape(bsz, seq, d)
```

```python
import functools
import math

import jax
import jax.numpy as jnp
import numpy as np
from jax import lax
from jax.experimental import pallas as pl
from jax.experimental.pallas import tpu as pltpu

F32 = jnp.float32
BF16 = jnp.bfloat16

D_MODEL = 2048
GRID_W = 64
D_S5 = 1024
S5_GROUP = 16
S5_GROUPS = D_S5 // S5_GROUP
S5_STATE = 64
LAMBDA_RE_MAX = -1e-4
S5_CHUNK = 16
D_HY = 1024
HY_ORDER = 2
HY_DIRS = 2
HY_SHORT = 3
HY_EMB = 33
HY_BANDS = (HY_EMB - 1) // 2
HY_TARGET = 1e-2
HY_FAST_PCT = 0.3
HY_SLOW_PCT = 1.5
FFT_N1 = 64
FFT_N2 = 128
FFT_N = FFT_N1 * FFT_N2
I_HY = D_S5
I_GA = D_S5 + (HY_ORDER + 1) * D_HY
I_GB = I_GA + D_MODEL
D_IN = I_GB + D_MODEL
D_FF = 5632
N_SUB = 3
N_MOD = 3
HALF_STEP = 0.5
RMS_EPS = 1e-6

VMEM_LIMIT_BYTES = 56 * 1024 * 1024


def _rms_mod(x, gain, shift, scale):
    ms = jnp.mean(x * x, axis=-1, keepdims=True)
    y = x * lax.rsqrt(ms + RMS_EPS) * gain
    return y * (1.0 + scale) + shift


def _ada_kernel(c_ref, w_ref, b_ref, o_ref):
    c = c_ref[...]
    a = c * jax.nn.sigmoid(c)
    o_ref[...] = jnp.dot(a, w_ref[...], preferred_element_type=F32,
                         precision=lax.Precision.HIGHEST) + b_ref[...]


def _ada_mod(c_rows, w, b, tn=1024):
    rows, d = c_rows.shape
    n = w.shape[1]
    return pl.pallas_call(
        _ada_kernel,
        grid=(n // tn,),
        in_specs=[pl.BlockSpec((rows, d), lambda j: (0, 0)),
                  pl.BlockSpec((d, tn), lambda j: (0, j)),
                  pl.BlockSpec((1, tn), lambda j: (0, j))],
        out_specs=pl.BlockSpec((rows, tn), lambda j: (0, j)),
        out_shape=jax.ShapeDtypeStruct((rows, n), F32),
        compiler_params=pltpu.CompilerParams(
            dimension_semantics=("arbitrary",), vmem_limit_bytes=VMEM_LIMIT_BYTES),
        name="ada_mod",
    )(c_rows, w, b.reshape(1, n))


def _ffn_kernel(x_ref, shift_ref, scale_ref, gate_ref, gain_ref, wg_ref, wu_ref, wd_ref,
                fg_ref, o_ref, h_ref, acc_ref, *, final_norm):
    j = pl.program_id(1)

    @pl.when(j == 0)
    def _():
        h_ref[...] = _rms_mod(x_ref[...], gain_ref[...], shift_ref[...], scale_ref[...]).astype(BF16)
        acc_ref[...] = jnp.zeros_like(acc_ref)

    h = h_ref[...]
    g = jnp.dot(h, wg_ref[...], preferred_element_type=F32)
    u = jnp.dot(h, wu_ref[...], preferred_element_type=F32)
    a = (g * jax.nn.sigmoid(g) * u).astype(BF16)
    acc_ref[...] += jnp.dot(a, wd_ref[...], preferred_element_type=F32)

    @pl.when(j == pl.num_programs(1) - 1)
    def _():
        y = x_ref[...] + (HALF_STEP * gate_ref[...]) * acc_ref[...]
        if final_norm:
            ms = jnp.mean(y * y, axis=-1, keepdims=True)
            y = y * lax.rsqrt(ms + RMS_EPS) * fg_ref[...]
        o_ref[...] = y


def _ffn_sublayer(x, mods, gain, wg, wu, wd, final_gain=None, tm=512, tf=512):
    t, d = x.shape
    bm = mods[0].shape[0]
    blocks_per_batch = (t // bm) // tm
    dff = wg.shape[1]
    final_norm = final_gain is not None
    fg = final_gain if final_norm else gain
    mod_spec = pl.BlockSpec((None, 1, d), lambda i, j: (i // blocks_per_batch, 0, 0))
    vec_spec = pl.BlockSpec((1, d), lambda i, j: (0, 0))
    return pl.pallas_call(
        functools.partial(_ffn_kernel, final_norm=final_norm),
        grid=(t // tm, dff // tf),
        in_specs=[pl.BlockSpec((tm, d), lambda i, j: (i, 0)),
                  mod_spec, mod_spec, mod_spec, vec_spec,
                  pl.BlockSpec((d, tf), lambda i, j: (0, j)),
                  pl.BlockSpec((d, tf), lambda i, j: (0, j)),
                  pl.BlockSpec((tf, d), lambda i, j: (j, 0)),
                  vec_spec],
        out_specs=pl.BlockSpec((tm, d), lambda i, j: (i, 0)),
        out_shape=jax.ShapeDtypeStruct((t, d), F32),
        scratch_shapes=[pltpu.VMEM((tm, d), BF16), pltpu.VMEM((tm, d), F32)],
        compiler_params=pltpu.CompilerParams(
            dimension_semantics=("parallel", "arbitrary"), vmem_limit_bytes=VMEM_LIMIT_BYTES),
        name="ffn_final" if final_norm else "ffn",
    )(x, *mods, gain.reshape(1, d), wg, wu, wd, fg.reshape(1, d))


def _proj_kernel(x_ref, shift_ref, scale_ref, gain_ref, w_ref, sw_ref, sb_ref, *rest, n_u, n_hy, row_len):
    o_refs, h_ref = rest[:-1], rest[-1]
    j = pl.program_id(1)

    @pl.when(j == 0)
    def _():
        h_ref[...] = _rms_mod(x_ref[...], gain_ref[...], shift_ref[...], scale_ref[...]).astype(BF16)

    p = jnp.dot(h_ref[...], w_ref[...], preferred_element_type=F32)
    if n_hy == 0:
        o_refs[0][...] = p
        return

    @pl.when(j < n_u)
    def _():
        o_refs[0][...] = p

    @pl.when((j >= n_u) & (j < n_u + n_hy))
    def _():
        tm = p.shape[0]
        col = lax.broadcasted_iota(jnp.int32, p.shape, 0) % row_len
        prev = jnp.where(col == 0, 0.0, pltpu.roll(p, 1, 0))
        nxt = jnp.where(col == row_len - 1, 0.0, pltpu.roll(p, tm - 1, 0))
        sw = sw_ref[...]
        o_refs[1][...] = sb_ref[...] + prev * sw[0:1] + p * sw[1:2] + nxt * sw[2:3]

    @pl.when(j >= n_u + n_hy)
    def _():
        o_refs[2][...] = jax.nn.sigmoid(p).astype(BF16)


def _in_proj(x, shift, scale, gain, w, short_w=None, short_b=None, n_u=D_S5, n_hy=0, row_len=GRID_W,
             tm=512, tn=512):
    t, d = x.shape
    bm = shift.shape[0]
    blocks_per_batch = (t // bm) // tm
    n = w.shape[1]
    assert tm % row_len == 0 and (t // bm) % tm == 0
    bu, bh = n_u // tn, n_hy // tn
    bg = n // tn - bu - bh
    mod_spec = pl.BlockSpec((None, 1, d), lambda i, j: (i // blocks_per_batch, 0, 0))
    out_shape = [jax.ShapeDtypeStruct((t, n_u), F32)]
    out_specs = [pl.BlockSpec((tm, tn), lambda i, j: (i, jnp.minimum(j, bu - 1)))]
    if bh:
        out_shape += [jax.ShapeDtypeStruct((t, n_hy), F32), jax.ShapeDtypeStruct((t, bg * tn), BF16)]
        out_specs += [pl.BlockSpec((tm, tn), lambda i, j: (i, jnp.clip(j - bu, 0, bh - 1))),
                      pl.BlockSpec((tm, tn), lambda i, j: (i, jnp.maximum(j - bu - bh, 0)))]
        sw, sb = short_w, short_b.reshape(1, n_hy)
        hy_blk = lambda i, j: (0, jnp.clip(j - bu, 0, bh - 1))
    else:
        sw, sb = jnp.zeros((HY_SHORT, tn), F32), jnp.zeros((1, tn), F32)
        hy_blk = lambda i, j: (0, 0)
    return pl.pallas_call(
        functools.partial(_proj_kernel, n_u=bu, n_hy=bh, row_len=row_len),
        grid=(t // tm, n // tn),
        in_specs=[pl.BlockSpec((tm, d), lambda i, j: (i, 0)),
                  mod_spec, mod_spec,
                  pl.BlockSpec((1, d), lambda i, j: (0, 0)),
                  pl.BlockSpec((d, tn), lambda i, j: (0, j)),
                  pl.BlockSpec((HY_SHORT, tn), hy_blk),
                  pl.BlockSpec((1, tn), hy_blk)],
        out_specs=out_specs,
        out_shape=out_shape,
        scratch_shapes=[pltpu.VMEM((tm, d), BF16)],
        compiler_params=pltpu.CompilerParams(
            dimension_semantics=("parallel", "arbitrary"), vmem_limit_bytes=VMEM_LIMIT_BYTES),
        name="in_proj",
    )(x, shift, scale, gain.reshape(1, d), w, sw, sb)


def _gelu_tanh(x):
    return 0.5 * x * (1.0 + jnp.tanh(math.sqrt(2.0 / math.pi) * (x + 0.044715 * (x * x * x))))


def _merge_kernel(x_ref, gate_ref, ys_ref, yh_ref, ga_ref, gb_ref, wpa_lo_ref, wpa_hi_ref, wpb_ref,
                  wout_ref, o_ref, s_ref, acc_ref):
    j = pl.program_id(1)

    @pl.when(j == 0)
    def _():
        s_ref[...] = _gelu_tanh(ys_ref[...]).astype(BF16)
        acc_ref[...] = jnp.zeros_like(acc_ref)

    s = s_ref[...]
    pa_lo = jnp.dot(s, wpa_lo_ref[...], preferred_element_type=F32)
    pa_hi = jnp.dot(s, wpa_hi_ref[...], preferred_element_type=F32)
    y_a = pa_lo * jax.nn.sigmoid(pa_hi)
    y_b = jnp.dot(yh_ref[...].astype(BF16), wpb_ref[...], preferred_element_type=F32)
    m = ga_ref[...].astype(F32) * y_a + gb_ref[...].astype(F32) * y_b
    acc_ref[...] += jnp.dot(m.astype(BF16), wout_ref[...], preferred_element_type=F32)

    @pl.when(j == pl.num_programs(1) - 1)
    def _():
        o_ref[...] = x_ref[...] + gate_ref[...] * acc_ref[...]


def _merge(x, gate, y_s5, y_hy, sig_gates, w_pa, w_pb, w_out, tm=512, tn=512):
    t, d = x.shape
    bm = gate.shape[0]
    blocks_per_batch = (t // bm) // tm
    nj = d // tn
    ds5 = y_s5.shape[1]
    dhy = y_hy.shape[1]
    return pl.pallas_call(
        _merge_kernel,
        grid=(t // tm, nj),
        in_specs=[pl.BlockSpec((tm, d), lambda i, j: (i, 0)),
                  pl.BlockSpec((None, 1, d), lambda i, j: (i // blocks_per_batch, 0, 0)),
                  pl.BlockSpec((tm, ds5), lambda i, j: (i, 0)),
                  pl.BlockSpec((tm, dhy), lambda i, j: (i, 0)),
                  pl.BlockSpec((tm, tn), lambda i, j: (i, j)),
                  pl.BlockSpec((tm, tn), lambda i, j: (i, nj + j)),
                  pl.BlockSpec((ds5, tn), lambda i, j: (0, j)),
                  pl.BlockSpec((ds5, tn), lambda i, j: (0, nj + j)),
                  pl.BlockSpec((dhy, tn), lambda i, j: (0, j)),
                  pl.BlockSpec((tn, d), lambda i, j: (j, 0))],
        out_specs=pl.BlockSpec((tm, d), lambda i, j: (i, 0)),
        out_shape=jax.ShapeDtypeStruct((t, d), F32),
        scratch_shapes=[pltpu.VMEM((tm, ds5), BF16), pltpu.VMEM((tm, d), F32)],
        compiler_params=pltpu.CompilerParams(
            dimension_semantics=("parallel", "arbitrary"), vmem_limit_bytes=VMEM_LIMIT_BYTES),
        name="merge",
    )(x, gate, y_s5, y_hy, sig_gates, sig_gates, w_pa, w_pa, w_pb, w_out)


def _s5_weights(lam_re, lam_im, log_dt, b_re, b_im, c_re, c_im, d_skip, n_steps):
    hp = lax.Precision.HIGHEST
    t = S5_CHUNK
    g, p, h = S5_GROUPS, S5_STATE, S5_GROUP
    dt = jnp.exp(log_dt)[..., None]
    lr = jnp.minimum(lam_re, LAMBDA_RE_MAX)
    li = lam_im
    zr, zi = lr * dt, li * dt

    def apow(j):
        jj = j.astype(F32)[:, None, None, None]
        mag = jnp.exp(jj * zr)
        return mag * jnp.cos(jj * zi), mag * jnp.sin(jj * zi)

    ab_re, ab_im = apow(jnp.arange(1, 2))
    ab_re, ab_im = ab_re[0], ab_im[0]
    nr, ni = ab_re - 1.0, ab_im
    den = lr * lr + li * li
    f_re = (nr * lr + ni * li) / den
    f_im = (ni * lr - nr * li) / den
    bb_re = f_re[..., None] * b_re - f_im[..., None] * b_im
    bb_im = f_re[..., None] * b_im + f_im[..., None] * b_re

    pr, pi = apow(jnp.arange(t + 1))
    ca_re = c_re[None] * pr[:, :, :, None, :] - c_im[None] * pi[:, :, :, None, :]
    ca_im = c_re[None] * pi[:, :, :, None, :] + c_im[None] * pr[:, :, :, None, :]
    kk = (jnp.einsum('jdghp,dgpk->jdghk', ca_re, bb_re, precision=hp)
          - jnp.einsum('jdghp,dgpk->jdghk', ca_im, bb_im, precision=hp))

    ii = jnp.arange(t)
    lag = ii[None, :] - ii[:, None]
    kf = kk[jnp.clip(lag, 0, t), 0] * (lag >= 0)[:, :, None, None, None].astype(F32)
    kb = kk[jnp.clip(-lag, 0, t), 1] * (lag <= 0)[:, :, None, None, None].astype(F32)
    skip = d_skip.reshape(g, h)[:, :, None] * jnp.eye(h, dtype=F32)[None]
    diag = jnp.eye(t, dtype=F32)[:, :, None, None, None] * skip[None, None]
    toep = (kf + kb + diag).transpose(2, 0, 4, 1, 3).reshape(g, t * h, t * h)

    def bpow(pw_re, pw_im, d):
        re = pw_re[:, d, :, :, None] * bb_re[d][None] - pw_im[:, d, :, :, None] * bb_im[d][None]
        im = pw_re[:, d, :, :, None] * bb_im[d][None] + pw_im[:, d, :, :, None] * bb_re[d][None]
        fold = lambda a: a.transpose(1, 0, 3, 2).reshape(g, t * h, p)
        return fold(re), fold(im)

    bpw = jnp.concatenate(bpow(pr[t - 1 - ii], pi[t - 1 - ii], 0) + bpow(pr[ii], pi[ii], 1), axis=-1)

    def cpow(e, d):
        re = ca_re[e, d].transpose(1, 3, 0, 2).reshape(g, p, t * h)
        im = -ca_im[e, d].transpose(1, 3, 0, 2).reshape(g, p, t * h)
        return jnp.concatenate([re, im], axis=1)

    cpw = jnp.concatenate([cpow(ii + 1, 0), cpow(t - ii, 1)], axis=1)

    sr, si = apow(t * (2 ** jnp.arange(n_steps)))
    a1 = jnp.concatenate([sr, sr], axis=-1)
    a2 = jnp.concatenate([-si, si], axis=-1)
    tab = jnp.stack([a1, a2], axis=2).transpose(3, 1, 0, 2, 4).reshape(g, 4 * n_steps, 2 * p)
    return bpw, toep, cpw, tab


def _s5_kernel(ul_ref, uc_ref, bpow_ref, toep_ref, cpow_ref, tab_ref, y_ref, *, bsz, n_steps):
    n_lat = ul_ref.shape[0] // bsz
    n_ctx = uc_ref.shape[0] // bsz
    n_ch = n_lat + n_ctx
    rows = bsz * n_ch
    half = 2 * S5_STATE
    ul = ul_ref[...].astype(BF16)
    uc = uc_ref[...].astype(BF16)
    bpow = bpow_ref[...]
    zl = jnp.dot(ul, bpow, preferred_element_type=F32)
    zc = jnp.dot(uc, bpow, preferred_element_type=F32)
    fparts, bparts = [], []
    for b in range(bsz):
        lat = slice(b * n_lat, (b + 1) * n_lat)
        ctx = slice(b * n_ctx, (b + 1) * n_ctx)
        fparts += [zc[ctx, :half], zl[lat, :half]]
        bparts += [zl[lat, half:], zc[ctx, half:]]
    fw = jnp.concatenate(fparts, axis=0)
    bw = jnp.concatenate(bparts, axis=0)
    rib = lax.broadcasted_iota(jnp.int32, (rows, half), 0) % n_ch
    tab = tab_ref[...]

    def cmul_add(acc, sh, a1, a2):
        return acc + a1 * sh + a2 * pltpu.roll(sh, S5_STATE, 1)

    for s in range(n_steps):
        d = 1 << s
        sh = jnp.where(rib >= d, pltpu.roll(fw, d, 0), 0.0)
        fw = cmul_add(fw, sh, tab[2 * s:2 * s + 1], tab[2 * s + 1:2 * s + 2])
        o = 2 * n_steps
        sh = jnp.where(rib < n_ch - d, pltpu.roll(bw, rows - d, 0), 0.0)
        bw = cmul_add(bw, sh, tab[o + 2 * s:o + 2 * s + 1], tab[o + 2 * s + 1:o + 2 * s + 2])
    fe = jnp.where(rib >= 1, pltpu.roll(fw, 1, 0), 0.0)
    be = jnp.where(rib < n_ch - 1, pltpu.roll(bw, rows - 1, 0), 0.0)
    fl = jnp.concatenate([fe[b * n_ch + n_ctx:(b + 1) * n_ch] for b in range(bsz)], axis=0)
    bl = jnp.concatenate([be[b * n_ch:b * n_ch + n_lat] for b in range(bsz)], axis=0)
    st = jnp.concatenate([fl, bl], axis=1).astype(BF16)
    y_ref[...] = (jnp.dot(ul, toep_ref[...], preferred_element_type=F32)
                  + jnp.dot(st, cpow_ref[...], preferred_element_type=F32))


def _s5_mix(u2, u2c, bpw, toep, cpw, tab, bsz):
    g, rl, w = u2.shape
    rc = u2c.shape[1]
    n_steps = tab.shape[1] // 4
    wspec = pl.BlockSpec((None, w, w), lambda i: (i, 0, 0))
    return pl.pallas_call(
        functools.partial(_s5_kernel, bsz=bsz, n_steps=n_steps),
        grid=(g,),
        in_specs=[pl.BlockSpec((None, rl, w), lambda i: (i, 0, 0)),
                  pl.BlockSpec((None, rc, w), lambda i: (i, 0, 0)),
                  wspec, wspec, wspec,
                  pl.BlockSpec((None, 4 * n_steps, tab.shape[2]), lambda i: (i, 0, 0))],
        out_specs=pl.BlockSpec((None, rl, w), lambda i: (i, 0, 0)),
        out_shape=jax.ShapeDtypeStruct((g, rl, w), F32),
        compiler_params=pltpu.CompilerParams(
            dimension_semantics=("parallel",), vmem_limit_bytes=VMEM_LIMIT_BYTES),
        name="s5_mix",
    )(u2, u2c, bpw.astype(BF16), toep.astype(BF16), cpw.astype(BF16), tab)


def _s5_bidirectional(u, u_ctx, lam_re, lam_im, log_dt, b_re, b_im, c_re, c_im, d_skip):
    bsz, length, dm = u.shape
    ctx_len = u_ctx.shape[1]
    t, g, h = S5_CHUNK, S5_GROUPS, S5_GROUP
    n_lat, n_ctx = length // t, ctx_len // t
    n_steps = max(1, math.ceil(math.log2(n_lat + n_ctx)))

    def fold(a, n):
        return a.reshape(bsz, n, t, g, h).transpose(3, 0, 1, 2, 4).reshape(g, bsz * n, t * h)

    bpw, toep, cpw, tab = _s5_weights(lam_re, lam_im, log_dt, b_re, b_im, c_re, c_im, d_skip, n_steps)
    y2 = _s5_mix(fold(u, n_lat), fold(u_ctx, n_ctx), bpw, toep, cpw, tab, bsz)
    return y2.reshape(g, bsz, n_lat, t, h).transpose(1, 2, 3, 0, 4).reshape(bsz, length, dm)


def _dft_constants(real_input=False):
    n1 = np.arange(FFT_N1)
    n2 = np.arange(FFT_N2)
    half = FFT_N1 // 2
    th = 2 * np.pi * np.outer(n1, n1) / FFT_N1
    c1, s1 = np.cos(th), np.sin(th)
    if real_input:
        w1 = np.concatenate([c1, -s1], axis=1)
    else:
        w1 = np.concatenate([np.concatenate([c1[:half], -s1[:half]], axis=1),
                             np.concatenate([s1[:half], c1[:half]], axis=1)], axis=0)
    z = np.zeros_like(w1)
    w1p = np.block([[w1, z], [z, w1]])
    ph = 2 * np.pi * np.outer(n2, n1) / FFT_N
    t1 = np.concatenate([np.cos(ph), np.cos(ph)], axis=1)
    t2 = np.concatenate([np.sin(ph), -np.sin(ph)], axis=1)
    ps = 2 * np.pi * np.outer(n2, n2) / FFT_N2
    f2 = np.concatenate([np.cos(ps), -np.sin(ps)], axis=1)
    g2 = np.concatenate([np.cos(ps), np.sin(ps)], axis=1)
    c2, s2 = np.cos(ph).T, np.sin(ph).T
    wi = np.concatenate([np.concatenate([c1[:, :half], s1[:, :half]], axis=1),
                         np.concatenate([-s1[:, :half], c1[:, :half]], axis=1)], axis=0) / FFT_N
    zi = np.zeros_like(wi)
    wi2 = np.stack([np.concatenate([wi, zi], axis=1), np.concatenate([zi, wi], axis=1)])
    as_b = lambda a: jnp.asarray(a, F32).astype(BF16)
    as_f = lambda a: jnp.asarray(a, F32)
    return [as_b(w1p), as_f(t1), as_f(t2), as_b(f2), as_b(g2), as_f(c2), as_f(s2), as_b(wi2)]


def _fwd_spectrum(xp, w1p, t1, t2, f2):
    cp = xp.shape[0]
    hn = FFT_N1
    a = jnp.dot(xp.reshape(cp * FFT_N2, 128).astype(BF16), w1p, preferred_element_type=F32)
    out = []
    for par in range(2):
        ap = a[:, par * 128:(par + 1) * 128]
        ap = ap.reshape(cp, FFT_N2, 128) * t1 + pltpu.roll(ap, hn, 1).reshape(cp, FFT_N2, 128) * t2
        at = jnp.swapaxes(ap, 1, 2)
        p = jnp.dot(at.reshape(cp * 128, FFT_N2).astype(BF16), f2, preferred_element_type=F32)
        p = p.reshape(cp, 128, 2 * FFT_N2)
        out.append((p[:, :hn, :FFT_N2] - p[:, hn:, FFT_N2:], p[:, :hn, FFT_N2:] + p[:, hn:, :FFT_N2]))
    return out


def _inv_time(yre, yim, g2, c2, s2, wi_par):
    cp = yre.shape[0]
    hn = FFT_N1
    y = jnp.concatenate([yre, yim], axis=1).reshape(cp * 128, FFT_N2).astype(BF16)
    q = jnp.dot(y, g2, preferred_element_type=F32).reshape(cp, 128, 2 * FFT_N2)
    bre = q[:, :hn, :FFT_N2] - q[:, hn:, FFT_N2:]
    bim = q[:, :hn, FFT_N2:] + q[:, hn:, :FFT_N2]
    b2 = jnp.concatenate([bre * c2 - bim * s2, bre * s2 + bim * c2], axis=1)
    bt = jnp.swapaxes(b2, 1, 2)
    return jnp.dot(bt.reshape(cp * FFT_N2, 128).astype(BF16), wi_par, preferred_element_type=F32)


def _hyena_kernel(z_ref, g1_ref, g2_ref, kf_ref, bias_ref, w1p_ref, t1_ref, t2_ref, f2_ref, gi_ref, c2_ref,
                  s2_ref, wi_ref, o_ref):
    cp = z_ref.shape[0]
    z = z_ref[...]
    gates = (g1_ref, g2_ref)
    for o in range(HY_ORDER):
        spec = _fwd_spectrum(z, w1p_ref[...], t1_ref[...], t2_ref[...], f2_ref[...])
        conv = None
        for par in range(2):
            xre, xim = spec[par]
            kre = kf_ref[o, par, :, :FFT_N1, :]
            kim = kf_ref[o, par, :, FFT_N1:, :]
            part = _inv_time(xre * kre - xim * kim, xre * kim + xim * kre,
                             gi_ref[...], c2_ref[...], s2_ref[...], wi_ref[par])
            conv = part if conv is None else conv + part
        z = gates[o][...] * (conv.reshape(cp, FFT_N2, 128) + bias_ref[o] * z)
    o_ref[...] = z


def _to_packed(v, bsz):
    _, length, ch = v.shape
    nn1 = length // FFT_N2
    return v.reshape(bsz, nn1, FFT_N2, ch // 2, 2).transpose(3, 2, 4, 0, 1).reshape(ch // 2, FFT_N2, 2 * bsz * nn1)


def _from_packed(vp, bsz):
    chp, _, lanes = vp.shape
    nn1 = lanes // (2 * bsz)
    return vp.reshape(chp, FFT_N2, 2, bsz, nn1).transpose(3, 4, 1, 0, 2).reshape(bsz, nn1 * FFT_N2, chp * 2)


def _hyena_conv(zp, g1p, g2p, kf, biasp, cp=8):
    chp = zp.shape[0]
    consts = _dft_constants()
    dspec = pl.BlockSpec((cp, FFT_N2, 128), lambda i: (i, 0, 0))
    full = lambda a: pl.BlockSpec(a.shape, lambda i: (0,) * a.ndim)
    return pl.pallas_call(
        _hyena_kernel,
        grid=(chp // cp,),
        in_specs=[dspec, dspec, dspec,
                  pl.BlockSpec((HY_ORDER, 2, cp, 128, FFT_N2), lambda i: (0, 0, i, 0, 0)),
                  pl.BlockSpec((HY_ORDER, cp, 1, 128), lambda i: (0, i, 0, 0))] + [full(a) for a in consts],
        out_specs=dspec,
        out_shape=jax.ShapeDtypeStruct(zp.shape, F32),
        compiler_params=pltpu.CompilerParams(
            dimension_semantics=("parallel",), vmem_limit_bytes=VMEM_LIMIT_BYTES),
        name="hyena_conv",
    )(zp, g1p, g2p, kf, biasp, *consts)


def _filter_time_kernel(w1t_ref, w1c_ref, w1s_ref, b1_ref, w2_ref, b2_ref, w3_ref, b3_ref, fr_ref,
                        wf_ref, wb_ref, df_ref, db_ref, o_ref, h_ref, k_ref, *, length):
    n_fft = 2 * length
    hp = lax.Precision.HIGHEST

    @pl.when(pl.program_id(0) == 0)
    def _():
        pos = lax.broadcasted_iota(jnp.int32, (1, n_fft), 1)
        lag = jnp.where(pos < length, pos, n_fft - pos).astype(F32)
        t = lag / float(length - 1)
        w = (2.0 * math.pi / length) * lag
        band_step = (HY_BANDS - 1 - 1e-4) / (HY_BANDS - 1)
        bands = 1e-4 + band_step * lax.broadcasted_iota(jnp.int32, (HY_BANDS, 1), 0).astype(F32)
        ang = bands * w
        fr = fr_ref[...]
        h = (w1t_ref[...] * t + jnp.dot(w1c_ref[...], jnp.cos(ang), preferred_element_type=F32, precision=hp)
             - jnp.dot(w1s_ref[...], jnp.sin(ang), preferred_element_type=F32, precision=hp))
        h = jnp.sin(fr * (h + b1_ref[...]))
        h = jnp.sin(fr * (jnp.dot(w2_ref[...], h, preferred_element_type=F32, precision=hp) + b2_ref[...]))
        h = jnp.sin(fr * (jnp.dot(w3_ref[...], h, preferred_element_type=F32, precision=hp) + b3_ref[...]))
        h_ref[...] = h

    pos = lax.broadcasted_iota(jnp.int32, (1, length), 1)
    tf = pos.astype(F32) / float(length - 1)
    tb = (length - pos).astype(F32) / float(length - 1)
    kf = (jnp.dot(wf_ref[...], h_ref[:, :length], preferred_element_type=F32, precision=hp)
          * jnp.exp(-tf * df_ref[...]))
    kb = (jnp.dot(wb_ref[...], h_ref[:, length:], preferred_element_type=F32, precision=hp)
          * jnp.exp(-tb * db_ref[...]))
    kb = jnp.where(pos == 0, 0.0, kb)
    inv = 1.0 / (jnp.sum(jnp.abs(kf), axis=1, keepdims=True) + jnp.sum(jnp.abs(kb), axis=1, keepdims=True))
    k_ref[:, :length] = kf * inv
    k_ref[:, length:] = kb * inv
    for n1 in range(FFT_N1):
        o_ref[:, n1, :] = k_ref[:, n1 * FFT_N2:(n1 + 1) * FFT_N2]


def _filter_time(length, w1, b1, w2, b2, w3, b3, freq, w_out, cb=128):
    col = lambda v: v.reshape(-1, 1)
    w1t = w1.T
    n_ch = w_out.shape[1]
    deltas = jnp.abs(jnp.linspace(math.log(HY_TARGET) / HY_SLOW_PCT, math.log(HY_TARGET) / HY_FAST_PCT,
                                  n_ch, dtype=F32)).reshape(n_ch, 1)
    wot = w_out.T
    nb = D_HY // cb
    small = lambda a: pl.BlockSpec(a.shape, lambda i: (0,) * a.ndim)
    fwd = lambda i: ((i // nb) * HY_DIRS * nb + i % nb, 0)
    bwd = lambda i: ((i // nb) * HY_DIRS * nb + nb + i % nb, 0)
    ins = [w1t[:, 0:1], w1t[:, 1:1 + HY_BANDS], w1t[:, 1 + HY_BANDS:], col(b1), w2.T, col(b2), w3.T, col(b3),
           col(freq)]
    hy_ff = w2.shape[0]
    return pl.pallas_call(
        functools.partial(_filter_time_kernel, length=length),
        grid=(HY_ORDER * nb,),
        in_specs=[small(a) for a in ins] + [pl.BlockSpec((cb, hy_ff), fwd), pl.BlockSpec((cb, hy_ff), bwd),
                                            pl.BlockSpec((cb, 1), fwd), pl.BlockSpec((cb, 1), bwd)],
        out_specs=pl.BlockSpec((cb, FFT_N1, FFT_N2), lambda i: (i, 0, 0)),
        out_shape=jax.ShapeDtypeStruct((HY_ORDER * D_HY, FFT_N1, FFT_N2), F32),
        scratch_shapes=[pltpu.VMEM((hy_ff, 2 * length), F32), pltpu.VMEM((cb, 2 * length), F32)],
        compiler_params=pltpu.CompilerParams(
            dimension_semantics=("arbitrary",), vmem_limit_bytes=VMEM_LIMIT_BYTES),
        name="hyena_filter_time",
    )(*ins, wot, wot, deltas, deltas)


def _filter_spec_kernel(k_ref, w1p_ref, t1_ref, t2_ref, f2_ref, o_ref):
    cb = k_ref.shape[0]
    kt = jnp.swapaxes(k_ref[...], 1, 2).reshape(cb // 2, 2, FFT_N2, FFT_N1)
    xp = jnp.concatenate([kt[:, 0], kt[:, 1]], axis=-1)
    spec = _fwd_spectrum(xp, w1p_ref[...], t1_ref[...], t2_ref[...], f2_ref[...])
    for par in range(2):
        o_ref[par, :, :FFT_N1, :] = spec[par][0]
        o_ref[par, :, FFT_N1:, :] = spec[par][1]


def _filter_spectrum(kt, cb=32):
    consts = _dft_constants(real_input=True)[:4]
    nb = D_HY // cb
    full = lambda a: pl.BlockSpec(a.shape, lambda i: (0,) * a.ndim)
    return pl.pallas_call(
        _filter_spec_kernel,
        grid=(HY_ORDER * nb,),
        in_specs=[pl.BlockSpec((cb, FFT_N1, FFT_N2), lambda i: (i, 0, 0))] + [full(a) for a in consts],
        out_specs=pl.BlockSpec((None, 2, cb // 2, 2 * FFT_N1, FFT_N2), lambda i: (i // nb, 0, i % nb, 0, 0)),
        out_shape=jax.ShapeDtypeStruct((HY_ORDER, 2, D_HY // 2, 2 * FFT_N1, FFT_N2), F32),
        compiler_params=pltpu.CompilerParams(
            dimension_semantics=("parallel",), vmem_limit_bytes=VMEM_LIMIT_BYTES),
        name="hyena_filter_spectrum",
    )(kt, *consts)


def _hyena(us, w1, b1, w2, b2, w3, b3, freq, w_out, bias):
    bsz, length, _ = us.shape
    assert 2 * length == FFT_N and bsz == 2, "one complex transform carries exactly two batch rows"
    zp, g1p, g2p = (_to_packed(us[..., i * D_HY:(i + 1) * D_HY], bsz) for i in range(HY_ORDER + 1))
    kf = _filter_spectrum(_filter_time(length, w1, b1, w2, b2, w3, b3, freq, w_out))
    biasp = jnp.repeat(bias.reshape(HY_ORDER, D_HY // 2, 1, 2), FFT_N1, axis=-1)
    return _from_packed(_hyena_conv(zp, g1p, g2p, kf, biasp), bsz)


def kernel(x, c, ctx, c_ctx, w_ada, b_ada, norm_g, ffn_w_gate, ffn_w_up, ffn_w_down, w_in,
           s5_lam_re, s5_lam_im, s5_log_dt, s5_b_re, s5_b_im, s5_c_re, s5_c_im, s5_d,
           hy_short_w, hy_short_b, hy_w1, hy_b1, hy_w2, hy_b2, hy_w3, hy_b3, hy_freq, hy_w_out,
           hy_bias, w_pa, w_pb, w_out, final_g):
    bsz, seq, d = x.shape
    ctx_len = ctx.shape[1]
    n_rows = seq // GRID_W
    depth = w_ada.shape[0]
    assert depth == 1, "context-token outputs are only dropped by the last layer"
    l = 0

    c_rows = jnp.concatenate([c, c_ctx[None, :], jnp.zeros((8 - bsz - 1, d), F32)], axis=0)
    mod_all = _ada_mod(c_rows, w_ada[l], b_ada[l])
    mod = mod_all[:bsz].reshape(bsz, N_SUB, N_MOD, 1, d)
    mod_c = mod_all[bsz:bsz + 1].reshape(1, N_SUB, N_MOD, 1, d)

    def mods(m, sub):
        return tuple(m[:, sub, k] for k in range(N_MOD))

    wg = ffn_w_gate[l].astype(BF16)
    wu = ffn_w_up[l].astype(BF16)
    wd = ffn_w_down[l].astype(BF16)
    w_in_b = w_in[l].astype(BF16)

    xt = x.reshape(bsz * seq, d)
    ct = ctx.reshape(bsz * ctx_len, d)

    xt = _ffn_sublayer(xt, mods(mod, 0), norm_g[l, 0], wg[0], wu[0], wd[0])
    ct = _ffn_sublayer(ct, mods(mod_c, 0), norm_g[l, 0], wg[0], wu[0], wd[0])

    assert GRID_W * n_rows == seq
    u_s5, us_hy, sig_gates = _in_proj(xt, mod[:, 1, 0], mod[:, 1, 1], norm_g[l, 1], w_in_b,
                                      hy_short_w[l], hy_short_b[l], n_u=I_HY, n_hy=I_GA - I_HY)
    (u_ctx,) = _in_proj(ct, mod_c[:, 1, 0], mod_c[:, 1, 1], norm_g[l, 1], w_in_b[:, :D_S5])

    y_s5 = _s5_bidirectional(u_s5.reshape(bsz, seq, D_S5), u_ctx.reshape(bsz, ctx_len, D_S5),
                             s5_lam_re[l], s5_lam_im[l], s5_log_dt[l],
                             s5_b_re[l], s5_b_im[l], s5_c_re[l], s5_c_im[l], s5_d[l])
    y_hy = _hyena(us_hy.reshape(bsz, seq, I_GA - I_HY),
                  hy_w1[l], hy_b1[l], hy_w2[l], hy_b2[l], hy_w3[l], hy_b3[l], hy_freq[l],
                  hy_w_out[l], hy_bias[l])

    xt = _merge(xt, mod[:, 1, 2], y_s5.reshape(bsz * seq, D_S5), y_hy.reshape(bsz * seq, D_HY),
                sig_gates, w_pa[l].astype(BF16), w_pb[l].astype(BF16), w_out[l].astype(BF16))

    xt = _ffn_sublayer(xt, mods(mod, 2), norm_g[l, 2], wg[1], wu[1], wd[1], final_gain=final_g)
    return xt.reshape(bsz, seq, d)
```

```python
import functools
import math

import jax
import jax.numpy as jnp
import numpy as np
from jax import lax
from jax.experimental import pallas as pl
from jax.experimental.pallas import tpu as pltpu

F32 = jnp.float32
BF16 = jnp.bfloat16

D_MODEL = 2048
GRID_W = 64
D_S5 = 1024
S5_GROUP = 16
S5_GROUPS = D_S5 // S5_GROUP
S5_STATE = 64
S5_DIRS = 2
LAMBDA_RE_MAX = -1e-4
S5_CHUNK = 16
D_HY = 1024
HY_ORDER = 2
HY_DIRS = 2
HY_SHORT = 3
HY_EMB = 33
HY_BANDS = (HY_EMB - 1) // 2
HY_TARGET = 1e-2
HY_FAST_PCT = 0.3
HY_SLOW_PCT = 1.5
FFT_N1 = 64
FFT_N2 = 128
FFT_N = FFT_N1 * FFT_N2
I_HY = D_S5
I_GA = D_S5 + (HY_ORDER + 1) * D_HY
I_GB = I_GA + D_MODEL
D_IN = I_GB + D_MODEL
D_FF = 5632
N_SUB = 3
N_MOD = 3
HALF_STEP = 0.5
RMS_EPS = 1e-6

VMEM_LIMIT_BYTES = 56 * 1024 * 1024


def _rms_mod(x, gain, shift, scale):
    ms = jnp.mean(x * x, axis=-1, keepdims=True)
    y = x * lax.rsqrt(ms + RMS_EPS) * gain
    return y * (1.0 + scale) + shift


def _ada_kernel(c_ref, w_ref, b_ref, o_ref):
    c = c_ref[...]
    a = c * jax.nn.sigmoid(c)
    o_ref[...] = jnp.dot(a, w_ref[...], preferred_element_type=F32,
                         precision=lax.Precision.HIGHEST) + b_ref[...]


def _ada_mod(c_rows, w, b, tn=1024):
    rows, d = c_rows.shape
    n = w.shape[1]
    return pl.pallas_call(
        _ada_kernel,
        grid=(n // tn,),
        in_specs=[pl.BlockSpec((rows, d), lambda j: (0, 0)),
                  pl.BlockSpec((d, tn), lambda j: (0, j)),
                  pl.BlockSpec((1, tn), lambda j: (0, j))],
        out_specs=pl.BlockSpec((rows, tn), lambda j: (0, j)),
        out_shape=jax.ShapeDtypeStruct((rows, n), F32),
        compiler_params=pltpu.CompilerParams(
            dimension_semantics=("arbitrary",), vmem_limit_bytes=VMEM_LIMIT_BYTES),
        name="ada_mod",
    )(c_rows, w, b.reshape(1, n))


def _ffn_kernel(x_ref, shift_ref, scale_ref, gate_ref, gain_ref, wg_ref, wu_ref, wd_ref,
                fg_ref, o_ref, h_ref, acc_ref, *, final_norm):
    j = pl.program_id(1)

    @pl.when(j == 0)
    def _():
        h_ref[...] = _rms_mod(x_ref[...], gain_ref[...], shift_ref[...], scale_ref[...]).astype(BF16)
        acc_ref[...] = jnp.zeros_like(acc_ref)

    h = h_ref[...]
    g = jnp.dot(h, wg_ref[...], preferred_element_type=F32)
    u = jnp.dot(h, wu_ref[...], preferred_element_type=F32)
    a = (g * jax.nn.sigmoid(g) * u).astype(BF16)
    acc_ref[...] += jnp.dot(a, wd_ref[...], preferred_element_type=F32)

    @pl.when(j == pl.num_programs(1) - 1)
    def _():
        y = x_ref[...] + (HALF_STEP * gate_ref[...]) * acc_ref[...]
        if final_norm:
            ms = jnp.mean(y * y, axis=-1, keepdims=True)
            y = y * lax.rsqrt(ms + RMS_EPS) * fg_ref[...]
        o_ref[...] = y


def _ffn_sublayer(x, mods, gain, wg, wu, wd, final_gain=None, tm=512, tf=512):
    t, d = x.shape
    bm = mods[0].shape[0]
    blocks_per_batch = (t // bm) // tm
    dff = wg.shape[1]
    final_norm = final_gain is not None
    fg = final_gain if final_norm else gain
    mod_spec = pl.BlockSpec((None, 1, d), lambda i, j: (i // blocks_per_batch, 0, 0))
    vec_spec = pl.BlockSpec((1, d), lambda i, j: (0, 0))
    return pl.pallas_call(
        functools.partial(_ffn_kernel, final_norm=final_norm),
        grid=(t // tm, dff // tf),
        in_specs=[pl.BlockSpec((tm, d), lambda i, j: (i, 0)),
                  mod_spec, mod_spec, mod_spec, vec_spec,
                  pl.BlockSpec((d, tf), lambda i, j: (0, j)),
                  pl.BlockSpec((d, tf), lambda i, j: (0, j)),
                  pl.BlockSpec((tf, d), lambda i, j: (j, 0)),
                  vec_spec],
        out_specs=pl.BlockSpec((tm, d), lambda i, j: (i, 0)),
        out_shape=jax.ShapeDtypeStruct((t, d), F32),
        scratch_shapes=[pltpu.VMEM((tm, d), BF16), pltpu.VMEM((tm, d), F32)],
        compiler_params=pltpu.CompilerParams(
            dimension_semantics=("parallel", "arbitrary"), vmem_limit_bytes=VMEM_LIMIT_BYTES),
        name="ffn_final" if final_norm else "ffn",
    )(x, *mods, gain.reshape(1, d), wg, wu, wd, fg.reshape(1, d))


def _proj_kernel(x_ref, shift_ref, scale_ref, gain_ref, w_ref, sw_ref, sb_ref, *rest, n_u, n_hy, row_len):
    o_refs, h_ref = rest[:-1], rest[-1]
    j = pl.program_id(1)

    @pl.when(j == 0)
    def _():
        h_ref[...] = _rms_mod(x_ref[...], gain_ref[...], shift_ref[...], scale_ref[...]).astype(BF16)

    p = jnp.dot(h_ref[...], w_ref[...], preferred_element_type=F32)
    if n_hy == 0:
        o_refs[0][...] = p
        return

    @pl.when(j < n_u)
    def _():
        o_refs[0][...] = p

    @pl.when((j >= n_u) & (j < n_u + n_hy))
    def _():
        tm = p.shape[0]
        col = lax.broadcasted_iota(jnp.int32, p.shape, 0) % row_len
        prev = jnp.where(col == 0, 0.0, pltpu.roll(p, 1, 0))
        nxt = jnp.where(col == row_len - 1, 0.0, pltpu.roll(p, tm - 1, 0))
        sw = sw_ref[...]
        o_refs[1][...] = sb_ref[...] + prev * sw[0:1] + p * sw[1:2] + nxt * sw[2:3]

    @pl.when(j >= n_u + n_hy)
    def _():
        o_refs[2][...] = jax.nn.sigmoid(p).astype(BF16)


def _in_proj(x, shift, scale, gain, w, short_w=None, short_b=None, n_u=D_S5, n_hy=0, row_len=GRID_W,
             tm=512, tn=512):
    t, d = x.shape
    bm = shift.shape[0]
    blocks_per_batch = (t // bm) // tm
    n = w.shape[1]
    assert tm % row_len == 0 and (t // bm) % tm == 0
    bu, bh = n_u // tn, n_hy // tn
    bg = n // tn - bu - bh
    mod_spec = pl.BlockSpec((None, 1, d), lambda i, j: (i // blocks_per_batch, 0, 0))
    out_shape = [jax.ShapeDtypeStruct((t, n_u), F32)]
    out_specs = [pl.BlockSpec((tm, tn), lambda i, j: (i, jnp.minimum(j, bu - 1)))]
    if bh:
        out_shape += [jax.ShapeDtypeStruct((t, n_hy), F32), jax.ShapeDtypeStruct((t, bg * tn), BF16)]
        out_specs += [pl.BlockSpec((tm, tn), lambda i, j: (i, jnp.clip(j - bu, 0, bh - 1))),
                      pl.BlockSpec((tm, tn), lambda i, j: (i, jnp.maximum(j - bu - bh, 0)))]
        sw, sb = short_w, short_b.reshape(1, n_hy)
        hy_blk = lambda i, j: (0, jnp.clip(j - bu, 0, bh - 1))
    else:
        sw, sb = jnp.zeros((HY_SHORT, tn), F32), jnp.zeros((1, tn), F32)
        hy_blk = lambda i, j: (0, 0)
    return pl.pallas_call(
        functools.partial(_proj_kernel, n_u=bu, n_hy=bh, row_len=row_len),
        grid=(t // tm, n // tn),
        in_specs=[pl.BlockSpec((tm, d), lambda i, j: (i, 0)),
                  mod_spec, mod_spec,
                  pl.BlockSpec((1, d), lambda i, j: (0, 0)),
                  pl.BlockSpec((d, tn), lambda i, j: (0, j)),
                  pl.BlockSpec((HY_SHORT, tn), hy_blk),
                  pl.BlockSpec((1, tn), hy_blk)],
        out_specs=out_specs,
        out_shape=out_shape,
        scratch_shapes=[pltpu.VMEM((tm, d), BF16)],
        compiler_params=pltpu.CompilerParams(
            dimension_semantics=("parallel", "arbitrary"), vmem_limit_bytes=VMEM_LIMIT_BYTES),
        name="in_proj",
    )(x, shift, scale, gain.reshape(1, d), w, sw, sb)


def _gelu_tanh(x):
    return 0.5 * x * (1.0 + jnp.tanh(math.sqrt(2.0 / math.pi) * (x + 0.044715 * (x * x * x))))


def _merge_kernel(x_ref, gate_ref, ys_ref, yh_ref, ga_ref, gb_ref, wpa_lo_ref, wpa_hi_ref, wpb_ref,
                  wout_ref, o_ref, s_ref, acc_ref):
    j = pl.program_id(1)

    @pl.when(j == 0)
    def _():
        s_ref[...] = _gelu_tanh(ys_ref[...]).astype(BF16)
        acc_ref[...] = jnp.zeros_like(acc_ref)

    s = s_ref[...]
    pa_lo = jnp.dot(s, wpa_lo_ref[...], preferred_element_type=F32)
    pa_hi = jnp.dot(s, wpa_hi_ref[...], preferred_element_type=F32)
    y_a = pa_lo * jax.nn.sigmoid(pa_hi)
    y_b = jnp.dot(yh_ref[...].astype(BF16), wpb_ref[...], preferred_element_type=F32)
    m = ga_ref[...].astype(F32) * y_a + gb_ref[...].astype(F32) * y_b
    acc_ref[...] += jnp.dot(m.astype(BF16), wout_ref[...], preferred_element_type=F32)

    @pl.when(j == pl.num_programs(1) - 1)
    def _():
        o_ref[...] = x_ref[...] + gate_ref[...] * acc_ref[...]


def _merge(x, gate, y_s5, y_hy, sig_gates, w_pa, w_pb, w_out, tm=512, tn=512):
    t, d = x.shape
    bm = gate.shape[0]
    blocks_per_batch = (t // bm) // tm
    nj = d // tn
    ds5 = y_s5.shape[1]
    dhy = y_hy.shape[1]
    return pl.pallas_call(
        _merge_kernel,
        grid=(t // tm, nj),
        in_specs=[pl.BlockSpec((tm, d), lambda i, j: (i, 0)),
                  pl.BlockSpec((None, 1, d), lambda i, j: (i // blocks_per_batch, 0, 0)),
                  pl.BlockSpec((tm, ds5), lambda i, j: (i, 0)),
                  pl.BlockSpec((tm, dhy), lambda i, j: (i, 0)),
                  pl.BlockSpec((tm, tn), lambda i, j: (i, j)),
                  pl.BlockSpec((tm, tn), lambda i, j: (i, nj + j)),
                  pl.BlockSpec((ds5, tn), lambda i, j: (0, j)),
                  pl.BlockSpec((ds5, tn), lambda i, j: (0, nj + j)),
                  pl.BlockSpec((dhy, tn), lambda i, j: (0, j)),
                  pl.BlockSpec((tn, d), lambda i, j: (j, 0))],
        out_specs=pl.BlockSpec((tm, d), lambda i, j: (i, 0)),
        out_shape=jax.ShapeDtypeStruct((t, d), F32),
        scratch_shapes=[pltpu.VMEM((tm, ds5), BF16), pltpu.VMEM((tm, d), F32)],
        compiler_params=pltpu.CompilerParams(
            dimension_semantics=("parallel", "arbitrary"), vmem_limit_bytes=VMEM_LIMIT_BYTES),
        name="merge",
    )(x, gate, y_s5, y_hy, sig_gates, sig_gates, w_pa, w_pa, w_pb, w_out)


def _s5_weights_kernel(par_ref, bt_ref, c_ref, d_ref, bpow_ref, toep_ref, cpow_ref, tab_ref, ca_ref, *, n_steps):
    t, h, p = S5_CHUNK, S5_GROUP, S5_STATE
    lanes = 2 * p
    hp = lax.Precision.HIGHEST
    sgn = jnp.where(lax.broadcasted_iota(jnp.int32, (1, lanes), 1) < p, -1.0, 1.0)
    par = par_ref[...]
    gsum = None
    for d in range(S5_DIRS):
        lr = jnp.minimum(par[3 * d:3 * d + 1], LAMBDA_RE_MAX)
        li = par[3 * d + 1:3 * d + 2]
        dt = jnp.exp(par[3 * d + 2:3 * d + 3])
        zr, zi = lr * dt, li * dt

        def apow(j):
            mag = jnp.exp(j * zr)
            return mag * jnp.cos(j * zi), sgn * (mag * jnp.sin(j * zi))

        def cmul(x, a1, a2):
            return x * a1 + pltpu.roll(x, p, 1) * a2

        a1, a2 = apow(lax.broadcasted_iota(jnp.int32, (t + 1, 1), 0).astype(F32))
        nr, ni = a1[1:2] - 1.0, sgn * a2[1:2]
        den = lr * lr + li * li
        f_re = (nr * lr + ni * li) / den
        f_im = (ni * lr - nr * li) / den
        bbar = cmul(bt_ref[d], f_re, sgn * f_im)
        cc = c_ref[d]
        ca = [cmul(cc, a1[j:j + 1], a2[j:j + 1]) * (-sgn) for j in range(t + 1)]
        ca_ref[...] = jnp.zeros_like(ca_ref)
        for k in range(t):
            e_b, e_c = (t - 1 - k, k + 1) if d == 0 else (k, t - k)
            bpow_ref[k * h:(k + 1) * h, d * lanes:(d + 1) * lanes] = (
                cmul(bbar, a1[e_b:e_b + 1], a2[e_b:e_b + 1]).astype(BF16))
            cpow_ref[k * h:(k + 1) * h, d * lanes:(d + 1) * lanes] = ca[e_c].astype(BF16)
            l = t - 1 + k if d == 0 else t - 1 - k
            ca_ref[l * h:(l + 1) * h, :] = ca[k]
        g = lax.dot_general(bbar, ca_ref[...], (((1,), (1,)), ((), ())), preferred_element_type=F32, precision=hp)
        gsum = g if gsum is None else gsum + g
        for s in range(n_steps):
            s1, s2 = apow(float(t * 2 ** s))
            r = d * 2 * n_steps + 2 * s
            tab_ref[r:r + 1, :] = s1
            tab_ref[r + 1:r + 2, :] = s2
    wide = gsum.shape[1]
    col = lax.broadcasted_iota(jnp.int32, (h, wide), 1)
    row = lax.broadcasted_iota(jnp.int32, (h, wide), 0)
    gsum = gsum + jnp.where(col - (t - 1) * h == row, d_ref[...], 0.0)
    for k in range(t):
        off = (t - 1 - k) * h
        shifted = gsum if off == 0 else pltpu.roll(gsum, wide - off, 1)
        toep_ref[k * h:(k + 1) * h, :] = shifted[:, :t * h].astype(BF16)


def _s5_weights(lam_re, lam_im, log_dt, b_re, b_im, c_re, c_im, d_skip, n_steps):
    g, p, h, t = S5_GROUPS, S5_STATE, S5_GROUP, S5_CHUNK
    cat2 = lambda a: jnp.concatenate([a, a], axis=-1)
    par = jnp.stack([cat2(lam_re), cat2(lam_im), jnp.broadcast_to(log_dt[..., None], (S5_DIRS, g, 2 * p))], axis=1)
    par = par.transpose(2, 0, 1, 3).reshape(g, 3 * S5_DIRS, 2 * p)
    btc = jnp.concatenate([b_re, b_im], axis=2).transpose(1, 0, 3, 2)
    ccat = jnp.concatenate([c_re, c_im], axis=3).transpose(1, 0, 2, 3)
    wide = 2 * t * h
    drow = jnp.zeros((g, 1, wide), F32).at[:, 0, (t - 1) * h:t * h].set(d_skip.reshape(g, h))
    w = t * h
    sq = pl.BlockSpec((None, w, w), lambda i: (i, 0, 0))
    return pl.pallas_call(
        functools.partial(_s5_weights_kernel, n_steps=n_steps),
        grid=(g,),
        in_specs=[pl.BlockSpec((None, 3 * S5_DIRS, 2 * p), lambda i: (i, 0, 0)),
                  pl.BlockSpec((None, S5_DIRS, h, 2 * p), lambda i: (i, 0, 0, 0)),
                  pl.BlockSpec((None, S5_DIRS, h, 2 * p), lambda i: (i, 0, 0, 0)),
                  pl.BlockSpec((None, 1, wide), lambda i: (i, 0, 0))],
        out_specs=[sq, sq, sq, pl.BlockSpec((None, 4 * n_steps, 2 * p), lambda i: (i, 0, 0))],
        out_shape=[jax.ShapeDtypeStruct((g, w, w), BF16)] * 3 + [jax.ShapeDtypeStruct((g, 4 * n_steps, 2 * p), F32)],
        scratch_shapes=[pltpu.VMEM((wide, 2 * p), F32)],
        compiler_params=pltpu.CompilerParams(
            dimension_semantics=("parallel",), vmem_limit_bytes=VMEM_LIMIT_BYTES),
        name="s5_weights",
    )(par, btc, ccat, drow)


def _s5_kernel(ul_ref, uc_ref, bpow_ref, toep_ref, cpow_ref, tab_ref, y_ref, *, bsz, n_steps):
    n_lat = ul_ref.shape[0] // bsz
    n_ctx = uc_ref.shape[0] // bsz
    n_ch = n_lat + n_ctx
    rows = bsz * n_ch
    half = 2 * S5_STATE
    ul = ul_ref[...].astype(BF16)
    uc = uc_ref[...].astype(BF16)
    bpow = bpow_ref[...]
    zl = jnp.dot(ul, bpow, preferred_element_type=F32)
    zc = jnp.dot(uc, bpow, preferred_element_type=F32)
    fparts, bparts = [], []
    for b in range(bsz):
        lat = slice(b * n_lat, (b + 1) * n_lat)
        ctx = slice(b * n_ctx, (b + 1) * n_ctx)
        fparts += [zc[ctx, :half], zl[lat, :half]]
        bparts += [zl[lat, half:], zc[ctx, half:]]
    fw = jnp.concatenate(fparts, axis=0)
    bw = jnp.concatenate(bparts, axis=0)
    rib = lax.broadcasted_iota(jnp.int32, (rows, half), 0) % n_ch
    tab = tab_ref[...]

    def cmul_add(acc, sh, a1, a2):
        return acc + a1 * sh + a2 * pltpu.roll(sh, S5_STATE, 1)

    for s in range(n_steps):
        d = 1 << s
        sh = jnp.where(rib >= d, pltpu.roll(fw, d, 0), 0.0)
        fw = cmul_add(fw, sh, tab[2 * s:2 * s + 1], tab[2 * s + 1:2 * s + 2])
        o = 2 * n_steps
        sh = jnp.where(rib < n_ch - d, pltpu.roll(bw, rows - d, 0), 0.0)
        bw = cmul_add(bw, sh, tab[o + 2 * s:o + 2 * s + 1], tab[o + 2 * s + 1:o + 2 * s + 2])
    fe = jnp.where(rib >= 1, pltpu.roll(fw, 1, 0), 0.0)
    be = jnp.where(rib < n_ch - 1, pltpu.roll(bw, rows - 1, 0), 0.0)
    fl = jnp.concatenate([fe[b * n_ch + n_ctx:(b + 1) * n_ch] for b in range(bsz)], axis=0)
    bl = jnp.concatenate([be[b * n_ch:b * n_ch + n_lat] for b in range(bsz)], axis=0)
    st = jnp.concatenate([fl, bl], axis=1).astype(BF16)
    y_ref[...] = (jnp.dot(ul, toep_ref[...], preferred_element_type=F32)
                  + lax.dot_general(st, cpow_ref[...], (((1,), (1,)), ((), ())), preferred_element_type=F32))


def _s5_mix(u2, u2c, bpw, toep, cpw, tab, bsz):
    g, rl, w = u2.shape
    rc = u2c.shape[1]
    n_steps = tab.shape[1] // 4
    wspec = pl.BlockSpec((None, w, w), lambda i: (i, 0, 0))
    return pl.pallas_call(
        functools.partial(_s5_kernel, bsz=bsz, n_steps=n_steps),
        grid=(g,),
        in_specs=[pl.BlockSpec((None, rl, w), lambda i: (i, 0, 0)),
                  pl.BlockSpec((None, rc, w), lambda i: (i, 0, 0)),
                  wspec, wspec, wspec,
                  pl.BlockSpec((None, 4 * n_steps, tab.shape[2]), lambda i: (i, 0, 0))],
        out_specs=pl.BlockSpec((None, rl, w), lambda i: (i, 0, 0)),
        out_shape=jax.ShapeDtypeStruct((g, rl, w), F32),
        compiler_params=pltpu.CompilerParams(
            dimension_semantics=("parallel",), vmem_limit_bytes=VMEM_LIMIT_BYTES),
        name="s5_mix",
    )(u2, u2c, bpw, toep, cpw, tab)


def _s5_bidirectional(u, u_ctx, lam_re, lam_im, log_dt, b_re, b_im, c_re, c_im, d_skip):
    bsz, length, dm = u.shape
    ctx_len = u_ctx.shape[1]
    t, g, h = S5_CHUNK, S5_GROUPS, S5_GROUP
    n_lat, n_ctx = length // t, ctx_len // t
    n_steps = max(1, math.ceil(math.log2(n_lat + n_ctx)))

    def fold(a, n):
        return a.reshape(bsz, n, t, g, h).transpose(3, 0, 1, 2, 4).reshape(g, bsz * n, t * h)

    bpw, toep, cpw, tab = _s5_weights(lam_re, lam_im, log_dt, b_re, b_im, c_re, c_im, d_skip, n_steps)
    y2 = _s5_mix(fold(u, n_lat), fold(u_ctx, n_ctx), bpw, toep, cpw, tab, bsz)
    return y2.reshape(g, bsz, n_lat, t, h).transpose(1, 2, 3, 0, 4).reshape(bsz, length, dm)


def _dft_constants(real_input=False):
    n1 = np.arange(FFT_N1)
    n2 = np.arange(FFT_N2)
    half = FFT_N1 // 2
    th = 2 * np.pi * np.outer(n1, n1) / FFT_N1
    c1, s1 = np.cos(th), np.sin(th)
    if real_input:
        w1 = np.concatenate([c1, -s1], axis=1)
    else:
        w1 = np.concatenate([np.concatenate([c1[:half], -s1[:half]], axis=1),
                             np.concatenate([s1[:half], c1[:half]], axis=1)], axis=0)
    z = np.zeros_like(w1)
    w1p = np.block([[w1, z], [z, w1]])
    ph = 2 * np.pi * np.outer(n2, n1) / FFT_N
    t1 = np.concatenate([np.cos(ph), np.cos(ph)], axis=1)
    t2 = np.concatenate([np.sin(ph), -np.sin(ph)], axis=1)
    ps = 2 * np.pi * np.outer(n2, n2) / FFT_N2
    f2 = np.concatenate([np.cos(ps), -np.sin(ps)], axis=1)
    g2 = np.concatenate([np.cos(ps), np.sin(ps)], axis=1)
    c2, s2 = np.cos(ph).T, np.sin(ph).T
    wi = np.concatenate([np.concatenate([c1[:, :half], s1[:, :half]], axis=1),
                         np.concatenate([-s1[:, :half], c1[:, :half]], axis=1)], axis=0) / FFT_N
    zi = np.zeros_like(wi)
    wi2 = np.stack([np.concatenate([wi, zi], axis=1), np.concatenate([zi, wi], axis=1)])
    as_b = lambda a: jnp.asarray(a, F32).astype(BF16)
    as_f = lambda a: jnp.asarray(a, F32)
    return [as_b(w1p), as_f(t1), as_f(t2), as_b(f2), as_b(g2), as_f(c2), as_f(s2), as_b(wi2)]


def _fwd_spectrum(xp, w1p, t1, t2, f2):
    cp = xp.shape[0]
    hn = FFT_N1
    a = jnp.dot(xp.reshape(cp * FFT_N2, 128).astype(BF16), w1p, preferred_element_type=F32)
    out = []
    for par in range(2):
        ap = a[:, par * 128:(par + 1) * 128]
        ap = ap.reshape(cp, FFT_N2, 128) * t1 + pltpu.roll(ap, hn, 1).reshape(cp, FFT_N2, 128) * t2
        at = jnp.swapaxes(ap, 1, 2)
        p = jnp.dot(at.reshape(cp * 128, FFT_N2).astype(BF16), f2, preferred_element_type=F32)
        p = p.reshape(cp, 128, 2 * FFT_N2)
        out.append((p[:, :hn, :FFT_N2] - p[:, hn:, FFT_N2:], p[:, :hn, FFT_N2:] + p[:, hn:, :FFT_N2]))
    return out


def _inv_time(yre, yim, g2, c2, s2, wi_par):
    cp = yre.shape[0]
    hn = FFT_N1
    y = jnp.concatenate([yre, yim], axis=1).reshape(cp * 128, FFT_N2).astype(BF16)
    q = jnp.dot(y, g2, preferred_element_type=F32).reshape(cp, 128, 2 * FFT_N2)
    bre = q[:, :hn, :FFT_N2] - q[:, hn:, FFT_N2:]
    bim = q[:, :hn, FFT_N2:] + q[:, hn:, :FFT_N2]
    b2 = jnp.concatenate([bre * c2 - bim * s2, bre * s2 + bim * c2], axis=1)
    bt = jnp.swapaxes(b2, 1, 2)
    return jnp.dot(bt.reshape(cp * FFT_N2, 128).astype(BF16), wi_par, preferred_element_type=F32)


def _hyena_kernel(z_ref, g1_ref, g2_ref, kf_ref, bias_ref, w1p_ref, t1_ref, t2_ref, f2_ref, gi_ref, c2_ref,
                  s2_ref, wi_ref, o_ref):
    cp = z_ref.shape[0]
    z = z_ref[...]
    gates = (g1_ref, g2_ref)
    for o in range(HY_ORDER):
        spec = _fwd_spectrum(z, w1p_ref[...], t1_ref[...], t2_ref[...], f2_ref[...])
        conv = None
        for par in range(2):
            xre, xim = spec[par]
            kre = kf_ref[o, par, :, :FFT_N1, :]
            kim = kf_ref[o, par, :, FFT_N1:, :]
            part = _inv_time(xre * kre - xim * kim, xre * kim + xim * kre,
                             gi_ref[...], c2_ref[...], s2_ref[...], wi_ref[par])
            conv = part if conv is None else conv + part
        z = gates[o][...] * (conv.reshape(cp, FFT_N2, 128) + bias_ref[o] * z)
    o_ref[...] = z


def _to_packed(v, bsz):
    _, length, ch = v.shape
    nn1 = length // FFT_N2
    return v.reshape(bsz, nn1, FFT_N2, ch // 2, 2).transpose(3, 2, 4, 0, 1).reshape(ch // 2, FFT_N2, 2 * bsz * nn1)


def _from_packed(vp, bsz):
    chp, _, lanes = vp.shape
    nn1 = lanes // (2 * bsz)
    return vp.reshape(chp, FFT_N2, 2, bsz, nn1).transpose(3, 4, 1, 0, 2).reshape(bsz, nn1 * FFT_N2, chp * 2)


def _hyena_conv(zp, g1p, g2p, kf, biasp, cp=8):
    chp = zp.shape[0]
    consts = _dft_constants()
    dspec = pl.BlockSpec((cp, FFT_N2, 128), lambda i: (i, 0, 0))
    full = lambda a: pl.BlockSpec(a.shape, lambda i: (0,) * a.ndim)
    return pl.pallas_call(
        _hyena_kernel,
        grid=(chp // cp,),
        in_specs=[dspec, dspec, dspec,
                  pl.BlockSpec((HY_ORDER, 2, cp, 128, FFT_N2), lambda i: (0, 0, i, 0, 0)),
                  pl.BlockSpec((HY_ORDER, cp, 1, 128), lambda i: (0, i, 0, 0))] + [full(a) for a in consts],
        out_specs=dspec,
        out_shape=jax.ShapeDtypeStruct(zp.shape, F32),
        compiler_params=pltpu.CompilerParams(
            dimension_semantics=("parallel",), vmem_limit_bytes=VMEM_LIMIT_BYTES),
        name="hyena_conv",
    )(zp, g1p, g2p, kf, biasp, *consts)


def _filter_time_kernel(w1t_ref, w1c_ref, w1s_ref, b1_ref, w2_ref, b2_ref, w3_ref, b3_ref, fr_ref,
                        wf_ref, wb_ref, df_ref, db_ref, o_ref, h_ref, k_ref, *, length):
    n_fft = 2 * length
    hp = lax.Precision.HIGHEST

    @pl.when(pl.program_id(0) == 0)
    def _():
        pos = lax.broadcasted_iota(jnp.int32, (1, n_fft), 1)
        lag = jnp.where(pos < length, pos, n_fft - pos).astype(F32)
        t = lag / float(length - 1)
        w = (2.0 * math.pi / length) * lag
        band_step = (HY_BANDS - 1 - 1e-4) / (HY_BANDS - 1)
        bands = 1e-4 + band_step * lax.broadcasted_iota(jnp.int32, (HY_BANDS, 1), 0).astype(F32)
        ang = bands * w
        fr = fr_ref[...]
        h = (w1t_ref[...] * t + jnp.dot(w1c_ref[...], jnp.cos(ang), preferred_element_type=F32, precision=hp)
             - jnp.dot(w1s_ref[...], jnp.sin(ang), preferred_element_type=F32, precision=hp))
        h = jnp.sin(fr * (h + b1_ref[...]))
        h = jnp.sin(fr * (jnp.dot(w2_ref[...], h, preferred_element_type=F32, precision=hp) + b2_ref[...]))
        h = jnp.sin(fr * (jnp.dot(w3_ref[...], h, preferred_element_type=F32, precision=hp) + b3_ref[...]))
        h_ref[...] = h

    pos = lax.broadcasted_iota(jnp.int32, (1, length), 1)
    tf = pos.astype(F32) / float(length - 1)
    tb = (length - pos).astype(F32) / float(length - 1)
    kf = (jnp.dot(wf_ref[...], h_ref[:, :length], preferred_element_type=F32, precision=hp)
          * jnp.exp(-tf * df_ref[...]))
    kb = (jnp.dot(wb_ref[...], h_ref[:, length:], preferred_element_type=F32, precision=hp)
          * jnp.exp(-tb * db_ref[...]))
    kb = jnp.where(pos == 0, 0.0, kb)
    inv = 1.0 / (jnp.sum(jnp.abs(kf), axis=1, keepdims=True) + jnp.sum(jnp.abs(kb), axis=1, keepdims=True))
    k_ref[:, :length] = kf * inv
    k_ref[:, length:] = kb * inv
    for n1 in range(FFT_N1):
        o_ref[:, n1, :] = k_ref[:, n1 * FFT_N2:(n1 + 1) * FFT_N2]


def _filter_time(length, w1, b1, w2, b2, w3, b3, freq, w_out, cb=128):
    col = lambda v: v.reshape(-1, 1)
    w1t = w1.T
    n_ch = w_out.shape[1]
    deltas = jnp.abs(jnp.linspace(math.log(HY_TARGET) / HY_SLOW_PCT, math.log(HY_TARGET) / HY_FAST_PCT,
                                  n_ch, dtype=F32)).reshape(n_ch, 1)
    wot = w_out.T
    nb = D_HY // cb
    small = lambda a: pl.BlockSpec(a.shape, lambda i: (0,) * a.ndim)
    fwd = lambda i: ((i // nb) * HY_DIRS * nb + i % nb, 0)
    bwd = lambda i: ((i // nb) * HY_DIRS * nb + nb + i % nb, 0)
    ins = [w1t[:, 0:1], w1t[:, 1:1 + HY_BANDS], w1t[:, 1 + HY_BANDS:], col(b1), w2.T, col(b2), w3.T, col(b3),
           col(freq)]
    hy_ff = w2.shape[0]
    return pl.pallas_call(
        functools.partial(_filter_time_kernel, length=length),
        grid=(HY_ORDER * nb,),
        in_specs=[small(a) for a in ins] + [pl.BlockSpec((cb, hy_ff), fwd), pl.BlockSpec((cb, hy_ff), bwd),
                                            pl.BlockSpec((cb, 1), fwd), pl.BlockSpec((cb, 1), bwd)],
        out_specs=pl.BlockSpec((cb, FFT_N1, FFT_N2), lambda i: (i, 0, 0)),
        out_shape=jax.ShapeDtypeStruct((HY_ORDER * D_HY, FFT_N1, FFT_N2), F32),
        scratch_shapes=[pltpu.VMEM((hy_ff, 2 * length), F32), pltpu.VMEM((cb, 2 * length), F32)],
        compiler_params=pltpu.CompilerParams(
            dimension_semantics=("arbitrary",), vmem_limit_bytes=VMEM_LIMIT_BYTES),
        name="hyena_filter_time",
    )(*ins, wot, wot, deltas, deltas)


def _filter_spec_kernel(k_ref, w1p_ref, t1_ref, t2_ref, f2_ref, o_ref):
    cb = k_ref.shape[0]
    kt = jnp.swapaxes(k_ref[...], 1, 2).reshape(cb // 2, 2, FFT_N2, FFT_N1)
    xp = jnp.concatenate([kt[:, 0], kt[:, 1]], axis=-1)
    spec = _fwd_spectrum(xp, w1p_ref[...], t1_ref[...], t2_ref[...], f2_ref[...])
    for par in range(2):
        o_ref[par, :, :FFT_N1, :] = spec[par][0]
        o_ref[par, :, FFT_N1:, :] = spec[par][1]


def _filter_spectrum(kt, cb=32):
    consts = _dft_constants(real_input=True)[:4]
    nb = D_HY // cb
    full = lambda a: pl.BlockSpec(a.shape, lambda i: (0,) * a.ndim)
    return pl.pallas_call(
        _filter_spec_kernel,
        grid=(HY_ORDER * nb,),
        in_specs=[pl.BlockSpec((cb, FFT_N1, FFT_N2), lambda i: (i, 0, 0))] + [full(a) for a in consts],
        out_specs=pl.BlockSpec((None, 2, cb // 2, 2 * FFT_N1, FFT_N2), lambda i: (i // nb, 0, i % nb, 0, 0)),
        out_shape=jax.ShapeDtypeStruct((HY_ORDER, 2, D_HY // 2, 2 * FFT_N1, FFT_N2), F32),
        compiler_params=pltpu.CompilerParams(
            dimension_semantics=("parallel",), vmem_limit_bytes=VMEM_LIMIT_BYTES),
        name="hyena_filter_spectrum",
    )(kt, *consts)


def _hyena(us, w1, b1, w2, b2, w3, b3, freq, w_out, bias):
    bsz, length, _ = us.shape
    assert 2 * length == FFT_N and bsz == 2, "one complex transform carries exactly two batch rows"
    zp, g1p, g2p = (_to_packed(us[..., i * D_HY:(i + 1) * D_HY], bsz) for i in range(HY_ORDER + 1))
    kf = _filter_spectrum(_filter_time(length, w1, b1, w2, b2, w3, b3, freq, w_out))
    biasp = jnp.repeat(bias.reshape(HY_ORDER, D_HY // 2, 1, 2), FFT_N1, axis=-1)
    return _from_packed(_hyena_conv(zp, g1p, g2p, kf, biasp), bsz)


def kernel(x, c, ctx, c_ctx, w_ada, b_ada, norm_g, ffn_w_gate, ffn_w_up, ffn_w_down, w_in,
           s5_lam_re, s5_lam_im, s5_log_dt, s5_b_re, s5_b_im, s5_c_re, s5_c_im, s5_d,
           hy_short_w, hy_short_b, hy_w1, hy_b1, hy_w2, hy_b2, hy_w3, hy_b3, hy_freq, hy_w_out,
           hy_bias, w_pa, w_pb, w_out, final_g):
    bsz, seq, d = x.shape
    ctx_len = ctx.shape[1]
    n_rows = seq // GRID_W
    depth = w_ada.shape[0]
    assert depth == 1, "context-token outputs are only dropped by the last layer"
    l = 0

    c_rows = jnp.concatenate([c, c_ctx[None, :], jnp.zeros((8 - bsz - 1, d), F32)], axis=0)
    mod_all = _ada_mod(c_rows, w_ada[l], b_ada[l])
    mod = mod_all[:bsz].reshape(bsz, N_SUB, N_MOD, 1, d)
    mod_c = mod_all[bsz:bsz + 1].reshape(1, N_SUB, N_MOD, 1, d)

    def mods(m, sub):
        return tuple(m[:, sub, k] for k in range(N_MOD))

    wg = ffn_w_gate[l].astype(BF16)
    wu = ffn_w_up[l].astype(BF16)
    wd = ffn_w_down[l].astype(BF16)
    w_in_b = w_in[l].astype(BF16)

    xt = x.reshape(bsz * seq, d)
    ct = ctx.reshape(bsz * ctx_len, d)

    xt = _ffn_sublayer(xt, mods(mod, 0), norm_g[l, 0], wg[0], wu[0], wd[0])
    ct = _ffn_sublayer(ct, mods(mod_c, 0), norm_g[l, 0], wg[0], wu[0], wd[0])

    assert GRID_W * n_rows == seq
    u_s5, us_hy, sig_gates = _in_proj(xt, mod[:, 1, 0], mod[:, 1, 1], norm_g[l, 1], w_in_b,
                                      hy_short_w[l], hy_short_b[l], n_u=I_HY, n_hy=I_GA - I_HY)
    (u_ctx,) = _in_proj(ct, mod_c[:, 1, 0], mod_c[:, 1, 1], norm_g[l, 1], w_in_b[:, :D_S5])

    y_s5 = _s5_bidirectional(u_s5.reshape(bsz, seq, D_S5), u_ctx.reshape(bsz, ctx_len, D_S5),
                             s5_lam_re[l], s5_lam_im[l], s5_log_dt[l],
                             s5_b_re[l], s5_b_im[l], s5_c_re[l], s5_c_im[l], s5_d[l])
    y_hy = _hyena(us_hy.reshape(bsz, seq, I_GA - I_HY),
                  hy_w1[l], hy_b1[l], hy_w2[l], hy_b2[l], hy_w3[l], hy_b3[l], hy_freq[l],
                  hy_w_out[l], hy_bias[l])

    xt = _merge(xt, mod[:, 1, 2], y_s5.reshape(bsz * seq, D_S5), y_hy.reshape(bsz * seq, D_HY),
                sig_gates, w_pa[l].astype(BF16), w_pb[l].astype(BF16), w_out[l].astype(BF16))

    xt = _ffn_sublayer(xt, mods(mod, 2), norm_g[l, 2], wg[1], wu[1], wd[1], final_gain=final_g)
    return xt.reshape(bsz, seq, d)
```

```python
import functools
import math

import jax
import jax.numpy as jnp
import numpy as np
from jax import lax
from jax.experimental import pallas as pl
from jax.experimental.pallas import tpu as pltpu

F32 = jnp.float32
BF16 = jnp.bfloat16

D_MODEL = 2048
GRID_W = 64
D_S5 = 1024
S5_GROUP = 16
S5_GROUPS = D_S5 // S5_GROUP
S5_STATE = 64
S5_DIRS = 2
LAMBDA_RE_MAX = -1e-4
S5_CHUNK = 16
D_HY = 1024
HY_ORDER = 2
HY_DIRS = 2
HY_SHORT = 3
HY_EMB = 33
HY_BANDS = (HY_EMB - 1) // 2
HY_TARGET = 1e-2
HY_FAST_PCT = 0.3
HY_SLOW_PCT = 1.5
FFT_N1 = 64
FFT_N2 = 128
FFT_N = FFT_N1 * FFT_N2
I_HY = D_S5
I_GA = D_S5 + (HY_ORDER + 1) * D_HY
I_GB = I_GA + D_MODEL
D_IN = I_GB + D_MODEL
D_FF = 5632
N_SUB = 3
N_MOD = 3
HALF_STEP = 0.5
RMS_EPS = 1e-6

VMEM_LIMIT_BYTES = 56 * 1024 * 1024


def _rms_mod(x, gain, shift, scale):
    ms = jnp.mean(x * x, axis=-1, keepdims=True)
    y = x * lax.rsqrt(ms + RMS_EPS) * gain
    return y * (1.0 + scale) + shift


def _ada_kernel(c_ref, w_ref, b_ref, o_ref):
    c = c_ref[...]
    a = c * jax.nn.sigmoid(c)
    o_ref[...] = jnp.dot(a, w_ref[...], preferred_element_type=F32,
                         precision=lax.Precision.HIGHEST) + b_ref[...]


def _ada_mod(c_rows, w, b, tn=1024):
    rows, d = c_rows.shape
    n = w.shape[1]
    return pl.pallas_call(
        _ada_kernel,
        grid=(n // tn,),
        in_specs=[pl.BlockSpec((rows, d), lambda j: (0, 0)),
                  pl.BlockSpec((d, tn), lambda j: (0, j)),
                  pl.BlockSpec((1, tn), lambda j: (0, j))],
        out_specs=pl.BlockSpec((rows, tn), lambda j: (0, j)),
        out_shape=jax.ShapeDtypeStruct((rows, n), F32),
        compiler_params=pltpu.CompilerParams(
            dimension_semantics=("arbitrary",), vmem_limit_bytes=VMEM_LIMIT_BYTES),
        name="ada_mod",
    )(c_rows, w, b.reshape(1, n))


def _ffn_kernel(x_ref, shift_ref, scale_ref, gate_ref, gain_ref, wg_ref, wu_ref, wd_ref,
                fg_ref, o_ref, h_ref, acc_ref, *, final_norm):
    j = pl.program_id(1)

    @pl.when(j == 0)
    def _():
        h_ref[...] = _rms_mod(x_ref[...], gain_ref[...], shift_ref[...], scale_ref[...]).astype(BF16)
        acc_ref[...] = jnp.zeros_like(acc_ref)

    h = h_ref[...]
    g = jnp.dot(h, wg_ref[...], preferred_element_type=F32)
    u = jnp.dot(h, wu_ref[...], preferred_element_type=F32)
    a = (g * jax.nn.sigmoid(g) * u).astype(BF16)
    acc_ref[...] += jnp.dot(a, wd_ref[...], preferred_element_type=F32)

    @pl.when(j == pl.num_programs(1) - 1)
    def _():
        y = x_ref[...] + (HALF_STEP * gate_ref[...]) * acc_ref[...]
        if final_norm:
            ms = jnp.mean(y * y, axis=-1, keepdims=True)
            y = y * lax.rsqrt(ms + RMS_EPS) * fg_ref[...]
        o_ref[...] = y


def _ffn_sublayer(x, mods, gain, wg, wu, wd, final_gain=None, tm=512, tf=512):
    t, d = x.shape
    bm = mods[0].shape[0]
    blocks_per_batch = (t // bm) // tm
    dff = wg.shape[1]
    final_norm = final_gain is not None
    fg = final_gain if final_norm else gain
    mod_spec = pl.BlockSpec((None, 1, d), lambda i, j: (i // blocks_per_batch, 0, 0))
    vec_spec = pl.BlockSpec((1, d), lambda i, j: (0, 0))
    return pl.pallas_call(
        functools.partial(_ffn_kernel, final_norm=final_norm),
        grid=(t // tm, dff // tf),
        in_specs=[pl.BlockSpec((tm, d), lambda i, j: (i, 0)),
                  mod_spec, mod_spec, mod_spec, vec_spec,
                  pl.BlockSpec((d, tf), lambda i, j: (0, j)),
                  pl.BlockSpec((d, tf), lambda i, j: (0, j)),
                  pl.BlockSpec((tf, d), lambda i, j: (j, 0)),
                  vec_spec],
        out_specs=pl.BlockSpec((tm, d), lambda i, j: (i, 0)),
        out_shape=jax.ShapeDtypeStruct((t, d), F32),
        scratch_shapes=[pltpu.VMEM((tm, d), BF16), pltpu.VMEM((tm, d), F32)],
        compiler_params=pltpu.CompilerParams(
            dimension_semantics=("parallel", "arbitrary"), vmem_limit_bytes=VMEM_LIMIT_BYTES),
        name="ffn_final" if final_norm else "ffn",
    )(x, *mods, gain.reshape(1, d), wg, wu, wd, fg.reshape(1, d))


def _proj_kernel(x_ref, shift_ref, scale_ref, gain_ref, w_ref, sw_ref, sb_ref, *rest, n_u, n_hy, row_len):
    o_refs, h_ref = rest[:-1], rest[-1]
    j = pl.program_id(1)

    @pl.when(j == 0)
    def _():
        h_ref[...] = _rms_mod(x_ref[...], gain_ref[...], shift_ref[...], scale_ref[...]).astype(BF16)

    p = jnp.dot(h_ref[...], w_ref[...], preferred_element_type=F32)
    if n_hy == 0:
        o_refs[0][...] = p.astype(o_refs[0].dtype)
        return

    @pl.when(j < n_u)
    def _():
        o_refs[0][...] = p.astype(o_refs[0].dtype)

    @pl.when((j >= n_u) & (j < n_u + n_hy))
    def _():
        tm = p.shape[0]
        col = lax.broadcasted_iota(jnp.int32, p.shape, 0) % row_len
        prev = jnp.where(col == 0, 0.0, pltpu.roll(p, 1, 0))
        nxt = jnp.where(col == row_len - 1, 0.0, pltpu.roll(p, tm - 1, 0))
        sw = sw_ref[...]
        us = sb_ref[...] + prev * sw[0:1] + p * sw[1:2] + nxt * sw[2:3]
        o_refs[1][...] = us.astype(o_refs[1].dtype)

    @pl.when(j >= n_u + n_hy)
    def _():
        o_refs[2][...] = jax.nn.sigmoid(p).astype(BF16)


def _in_proj(x, shift, scale, gain, w, short_w=None, short_b=None, n_u=D_S5, n_hy=0, row_len=GRID_W,
             tm=512, tn=512):
    t, d = x.shape
    bm = shift.shape[0]
    blocks_per_batch = (t // bm) // tm
    n = w.shape[1]
    assert tm % row_len == 0 and (t // bm) % tm == 0
    bu, bh = n_u // tn, n_hy // tn
    bg = n // tn - bu - bh
    mod_spec = pl.BlockSpec((None, 1, d), lambda i, j: (i // blocks_per_batch, 0, 0))
    out_shape = [jax.ShapeDtypeStruct((t, n_u), BF16)]
    out_specs = [pl.BlockSpec((tm, tn), lambda i, j: (i, jnp.minimum(j, bu - 1)))]
    if bh:
        out_shape += [jax.ShapeDtypeStruct((t, n_hy), BF16), jax.ShapeDtypeStruct((t, bg * tn), BF16)]
        out_specs += [pl.BlockSpec((tm, tn), lambda i, j: (i, jnp.clip(j - bu, 0, bh - 1))),
                      pl.BlockSpec((tm, tn), lambda i, j: (i, jnp.maximum(j - bu - bh, 0)))]
        sw, sb = short_w, short_b.reshape(1, n_hy)
        hy_blk = lambda i, j: (0, jnp.clip(j - bu, 0, bh - 1))
    else:
        sw, sb = jnp.zeros((HY_SHORT, tn), F32), jnp.zeros((1, tn), F32)
        hy_blk = lambda i, j: (0, 0)
    return pl.pallas_call(
        functools.partial(_proj_kernel, n_u=bu, n_hy=bh, row_len=row_len),
        grid=(t // tm, n // tn),
        in_specs=[pl.BlockSpec((tm, d), lambda i, j: (i, 0)),
                  mod_spec, mod_spec,
                  pl.BlockSpec((1, d), lambda i, j: (0, 0)),
                  pl.BlockSpec((d, tn), lambda i, j: (0, j)),
                  pl.BlockSpec((HY_SHORT, tn), hy_blk),
                  pl.BlockSpec((1, tn), hy_blk)],
        out_specs=out_specs,
        out_shape=out_shape,
        scratch_shapes=[pltpu.VMEM((tm, d), BF16)],
        compiler_params=pltpu.CompilerParams(
            dimension_semantics=("parallel", "arbitrary"), vmem_limit_bytes=VMEM_LIMIT_BYTES),
        name="in_proj",
    )(x, shift, scale, gain.reshape(1, d), w, sw, sb)


def _gelu_tanh(x):
    return 0.5 * x * (1.0 + jnp.tanh(math.sqrt(2.0 / math.pi) * (x + 0.044715 * (x * x * x))))


def _merge_kernel(x_ref, gate_ref, ys_ref, yh_ref, ga_ref, gb_ref, wpa_lo_ref, wpa_hi_ref, wpb_ref,
                  wout_ref, o_ref, s_ref, acc_ref):
    j = pl.program_id(1)

    @pl.when(j == 0)
    def _():
        s_ref[...] = _gelu_tanh(ys_ref[...].astype(F32)).astype(BF16)
        acc_ref[...] = jnp.zeros_like(acc_ref)

    s = s_ref[...]
    pa_lo = jnp.dot(s, wpa_lo_ref[...], preferred_element_type=F32)
    pa_hi = jnp.dot(s, wpa_hi_ref[...], preferred_element_type=F32)
    y_a = pa_lo * jax.nn.sigmoid(pa_hi)
    y_b = jnp.dot(yh_ref[...], wpb_ref[...], preferred_element_type=F32)
    m = ga_ref[...].astype(F32) * y_a + gb_ref[...].astype(F32) * y_b
    acc_ref[...] += jnp.dot(m.astype(BF16), wout_ref[...], preferred_element_type=F32)

    @pl.when(j == pl.num_programs(1) - 1)
    def _():
        o_ref[...] = x_ref[...] + gate_ref[...] * acc_ref[...]


def _merge(x, gate, y_s5, y_hy, sig_gates, w_pa, w_pb, w_out, tm=512, tn=512):
    t, d = x.shape
    bm = gate.shape[0]
    blocks_per_batch = (t // bm) // tm
    nj = d // tn
    ds5 = y_s5.shape[1]
    dhy = y_hy.shape[1]
    return pl.pallas_call(
        _merge_kernel,
        grid=(t // tm, nj),
        in_specs=[pl.BlockSpec((tm, d), lambda i, j: (i, 0)),
                  pl.BlockSpec((None, 1, d), lambda i, j: (i // blocks_per_batch, 0, 0)),
                  pl.BlockSpec((tm, ds5), lambda i, j: (i, 0)),
                  pl.BlockSpec((tm, dhy), lambda i, j: (i, 0)),
                  pl.BlockSpec((tm, tn), lambda i, j: (i, j)),
                  pl.BlockSpec((tm, tn), lambda i, j: (i, nj + j)),
                  pl.BlockSpec((ds5, tn), lambda i, j: (0, j)),
                  pl.BlockSpec((ds5, tn), lambda i, j: (0, nj + j)),
                  pl.BlockSpec((dhy, tn), lambda i, j: (0, j)),
                  pl.BlockSpec((tn, d), lambda i, j: (j, 0))],
        out_specs=pl.BlockSpec((tm, d), lambda i, j: (i, 0)),
        out_shape=jax.ShapeDtypeStruct((t, d), F32),
        scratch_shapes=[pltpu.VMEM((tm, ds5), BF16), pltpu.VMEM((tm, d), F32)],
        compiler_params=pltpu.CompilerParams(
            dimension_semantics=("parallel", "arbitrary"), vmem_limit_bytes=VMEM_LIMIT_BYTES),
        name="merge",
    )(x, gate, y_s5, y_hy, sig_gates, sig_gates, w_pa, w_pa, w_pb, w_out)


def _s5_weights_kernel(par_ref, bt_ref, c_ref, d_ref, bpow_ref, toep_ref, cpow_ref, tab_ref, ca_ref, *, n_steps):
    t, h, p = S5_CHUNK, S5_GROUP, S5_STATE
    lanes = 2 * p
    hp = lax.Precision.HIGHEST
    sgn = jnp.where(lax.broadcasted_iota(jnp.int32, (1, lanes), 1) < p, -1.0, 1.0)
    par = par_ref[...]
    gsum = None
    for d in range(S5_DIRS):
        lr = jnp.minimum(par[3 * d:3 * d + 1], LAMBDA_RE_MAX)
        li = par[3 * d + 1:3 * d + 2]
        dt = jnp.exp(par[3 * d + 2:3 * d + 3])
        zr, zi = lr * dt, li * dt

        def apow(j):
            mag = jnp.exp(j * zr)
            return mag * jnp.cos(j * zi), sgn * (mag * jnp.sin(j * zi))

        def cmul(x, a1, a2):
            return x * a1 + pltpu.roll(x, p, 1) * a2

        a1, a2 = apow(lax.broadcasted_iota(jnp.int32, (t + 1, 1), 0).astype(F32))
        nr, ni = a1[1:2] - 1.0, sgn * a2[1:2]
        den = lr * lr + li * li
        f_re = (nr * lr + ni * li) / den
        f_im = (ni * lr - nr * li) / den
        bbar = cmul(bt_ref[d], f_re, sgn * f_im)
        cc = c_ref[d]
        ca = [cmul(cc, a1[j:j + 1], a2[j:j + 1]) * (-sgn) for j in range(t + 1)]
        ca_ref[...] = jnp.zeros_like(ca_ref)
        for k in range(t):
            e_b, e_c = (t - 1 - k, k + 1) if d == 0 else (k, t - k)
            bpow_ref[k * h:(k + 1) * h, d * lanes:(d + 1) * lanes] = (
                cmul(bbar, a1[e_b:e_b + 1], a2[e_b:e_b + 1]).astype(BF16))
            cpow_ref[k * h:(k + 1) * h, d * lanes:(d + 1) * lanes] = ca[e_c].astype(BF16)
            l = t - 1 + k if d == 0 else t - 1 - k
            ca_ref[l * h:(l + 1) * h, :] = ca[k]
        g = lax.dot_general(bbar, ca_ref[...], (((1,), (1,)), ((), ())), preferred_element_type=F32, precision=hp)
        gsum = g if gsum is None else gsum + g
        for s in range(n_steps):
            s1, s2 = apow(float(t * 2 ** s))
            r = d * 2 * n_steps + 2 * s
            tab_ref[r:r + 1, :] = s1
            tab_ref[r + 1:r + 2, :] = s2
    wide = gsum.shape[1]
    col = lax.broadcasted_iota(jnp.int32, (h, wide), 1)
    row = lax.broadcasted_iota(jnp.int32, (h, wide), 0)
    gsum = gsum + jnp.where(col - (t - 1) * h == row, d_ref[...], 0.0)
    for k in range(t):
        off = (t - 1 - k) * h
        shifted = gsum if off == 0 else pltpu.roll(gsum, wide - off, 1)
        toep_ref[k * h:(k + 1) * h, :] = shifted[:, :t * h].astype(BF16)


def _s5_weights(lam_re, lam_im, log_dt, b_re, b_im, c_re, c_im, d_skip, n_steps):
    g, p, h, t = S5_GROUPS, S5_STATE, S5_GROUP, S5_CHUNK
    cat2 = lambda a: jnp.concatenate([a, a], axis=-1)
    par = jnp.stack([cat2(lam_re), cat2(lam_im), jnp.broadcast_to(log_dt[..., None], (S5_DIRS, g, 2 * p))], axis=1)
    par = par.transpose(2, 0, 1, 3).reshape(g, 3 * S5_DIRS, 2 * p)
    btc = jnp.concatenate([b_re, b_im], axis=2).transpose(1, 0, 3, 2)
    ccat = jnp.concatenate([c_re, c_im], axis=3).transpose(1, 0, 2, 3)
    wide = 2 * t * h
    drow = jnp.zeros((g, 1, wide), F32).at[:, 0, (t - 1) * h:t * h].set(d_skip.reshape(g, h))
    w = t * h
    sq = pl.BlockSpec((None, w, w), lambda i: (i, 0, 0))
    return pl.pallas_call(
        functools.partial(_s5_weights_kernel, n_steps=n_steps),
        grid=(g,),
        in_specs=[pl.BlockSpec((None, 3 * S5_DIRS, 2 * p), lambda i: (i, 0, 0)),
                  pl.BlockSpec((None, S5_DIRS, h, 2 * p), lambda i: (i, 0, 0, 0)),
                  pl.BlockSpec((None, S5_DIRS, h, 2 * p), lambda i: (i, 0, 0, 0)),
                  pl.BlockSpec((None, 1, wide), lambda i: (i, 0, 0))],
        out_specs=[sq, sq, sq, pl.BlockSpec((None, 4 * n_steps, 2 * p), lambda i: (i, 0, 0))],
        out_shape=[jax.ShapeDtypeStruct((g, w, w), BF16)] * 3 + [jax.ShapeDtypeStruct((g, 4 * n_steps, 2 * p), F32)],
        scratch_shapes=[pltpu.VMEM((wide, 2 * p), F32)],
        compiler_params=pltpu.CompilerParams(
            dimension_semantics=("parallel",), vmem_limit_bytes=VMEM_LIMIT_BYTES),
        name="s5_weights",
    )(par, btc, ccat, drow)


def _s5_kernel(ul_ref, uc_ref, bpow_ref, toep_ref, cpow_ref, tab_ref, y_ref, *, bsz, n_steps):
    n_lat = ul_ref.shape[0] // bsz
    n_ctx = uc_ref.shape[0] // bsz
    n_ch = n_lat + n_ctx
    rows = bsz * n_ch
    half = 2 * S5_STATE
    ul = ul_ref[...]
    uc = uc_ref[...]
    bpow = bpow_ref[...]
    zl = jnp.dot(ul, bpow, preferred_element_type=F32)
    zc = jnp.dot(uc, bpow, preferred_element_type=F32)
    fparts, bparts = [], []
    for b in range(bsz):
        lat = slice(b * n_lat, (b + 1) * n_lat)
        ctx = slice(b * n_ctx, (b + 1) * n_ctx)
        fparts += [zc[ctx, :half], zl[lat, :half]]
        bparts += [zl[lat, half:], zc[ctx, half:]]
    fw = jnp.concatenate(fparts, axis=0)
    bw = jnp.concatenate(bparts, axis=0)
    rib = lax.broadcasted_iota(jnp.int32, (rows, half), 0) % n_ch
    tab = tab_ref[...]

    def cmul_add(acc, sh, a1, a2):
        return acc + a1 * sh + a2 * pltpu.roll(sh, S5_STATE, 1)

    for s in range(n_steps):
        d = 1 << s
        sh = jnp.where(rib >= d, pltpu.roll(fw, d, 0), 0.0)
        fw = cmul_add(fw, sh, tab[2 * s:2 * s + 1], tab[2 * s + 1:2 * s + 2])
        o = 2 * n_steps
        sh = jnp.where(rib < n_ch - d, pltpu.roll(bw, rows - d, 0), 0.0)
        bw = cmul_add(bw, sh, tab[o + 2 * s:o + 2 * s + 1], tab[o + 2 * s + 1:o + 2 * s + 2])
    fe = jnp.where(rib >= 1, pltpu.roll(fw, 1, 0), 0.0)
    be = jnp.where(rib < n_ch - 1, pltpu.roll(bw, rows - 1, 0), 0.0)
    fl = jnp.concatenate([fe[b * n_ch + n_ctx:(b + 1) * n_ch] for b in range(bsz)], axis=0)
    bl = jnp.concatenate([be[b * n_ch:b * n_ch + n_lat] for b in range(bsz)], axis=0)
    st = jnp.concatenate([fl, bl], axis=1).astype(BF16)
    y = (jnp.dot(ul, toep_ref[...], preferred_element_type=F32)
         + lax.dot_general(st, cpow_ref[...], (((1,), (1,)), ((), ())), preferred_element_type=F32))
    y_ref[...] = y.astype(y_ref.dtype)


def _s5_mix(u2, u2c, bpw, toep, cpw, tab, bsz):
    g, rl, w = u2.shape
    rc = u2c.shape[1]
    n_steps = tab.shape[1] // 4
    wspec = pl.BlockSpec((None, w, w), lambda i: (i, 0, 0))
    return pl.pallas_call(
        functools.partial(_s5_kernel, bsz=bsz, n_steps=n_steps),
        grid=(g,),
        in_specs=[pl.BlockSpec((None, rl, w), lambda i: (i, 0, 0)),
                  pl.BlockSpec((None, rc, w), lambda i: (i, 0, 0)),
                  wspec, wspec, wspec,
                  pl.BlockSpec((None, 4 * n_steps, tab.shape[2]), lambda i: (i, 0, 0))],
        out_specs=pl.BlockSpec((None, rl, w), lambda i: (i, 0, 0)),
        out_shape=jax.ShapeDtypeStruct((g, rl, w), BF16),
        compiler_params=pltpu.CompilerParams(
            dimension_semantics=("parallel",), vmem_limit_bytes=VMEM_LIMIT_BYTES),
        name="s5_mix",
    )(u2, u2c, bpw, toep, cpw, tab)


def _s5_bidirectional(u, u_ctx, lam_re, lam_im, log_dt, b_re, b_im, c_re, c_im, d_skip):
    bsz, length, dm = u.shape
    ctx_len = u_ctx.shape[1]
    t, g, h = S5_CHUNK, S5_GROUPS, S5_GROUP
    n_lat, n_ctx = length // t, ctx_len // t
    n_steps = max(1, math.ceil(math.log2(n_lat + n_ctx)))

    def fold(a, n):
        return a.reshape(bsz, n, t, g, h).transpose(3, 0, 1, 2, 4).reshape(g, bsz * n, t * h)

    bpw, toep, cpw, tab = _s5_weights(lam_re, lam_im, log_dt, b_re, b_im, c_re, c_im, d_skip, n_steps)
    y2 = _s5_mix(fold(u, n_lat), fold(u_ctx, n_ctx), bpw, toep, cpw, tab, bsz)
    return y2.reshape(g, bsz, n_lat, t, h).transpose(1, 2, 3, 0, 4).reshape(bsz, length, dm)


def _dft_constants(real_input=False):
    n1 = np.arange(FFT_N1)
    n2 = np.arange(FFT_N2)
    half = FFT_N1 // 2
    th = 2 * np.pi * np.outer(n1, n1) / FFT_N1
    c1, s1 = np.cos(th), np.sin(th)
    if real_input:
        w1 = np.concatenate([c1, -s1], axis=1)
    else:
        w1 = np.concatenate([np.concatenate([c1[:half], -s1[:half]], axis=1),
                             np.concatenate([s1[:half], c1[:half]], axis=1)], axis=0)
    z = np.zeros_like(w1)
    w1p = np.block([[w1, z], [z, w1]])
    ph = 2 * np.pi * np.outer(n2, n1) / FFT_N
    t1 = np.concatenate([np.cos(ph), np.cos(ph)], axis=1)
    t2 = np.concatenate([np.sin(ph), -np.sin(ph)], axis=1)
    ps = 2 * np.pi * np.outer(n2, n2) / FFT_N2
    f2 = np.concatenate([np.cos(ps), -np.sin(ps)], axis=1)
    g2 = np.concatenate([np.cos(ps), np.sin(ps)], axis=1)
    c2, s2 = np.cos(ph).T, np.sin(ph).T
    wi = np.concatenate([np.concatenate([c1[:, :half], s1[:, :half]], axis=1),
                         np.concatenate([-s1[:, :half], c1[:, :half]], axis=1)], axis=0) / FFT_N
    zi = np.zeros_like(wi)
    wi2 = np.stack([np.concatenate([wi, zi], axis=1), np.concatenate([zi, wi], axis=1)])
    as_b = lambda a: jnp.asarray(a, F32).astype(BF16)
    as_f = lambda a: jnp.asarray(a, F32)
    return [as_b(w1p), as_f(t1), as_f(t2), as_b(f2), as_b(g2), as_f(c2), as_f(s2), as_b(wi2)]


def _fwd_spectrum(xp, w1p, t1, t2, f2):
    cp = xp.shape[0]
    hn = FFT_N1
    a = jnp.dot(xp.reshape(cp * FFT_N2, 128).astype(BF16), w1p, preferred_element_type=F32)
    out = []
    for par in range(2):
        ap = a[:, par * 128:(par + 1) * 128]
        ap = ap.reshape(cp, FFT_N2, 128) * t1 + pltpu.roll(ap, hn, 1).reshape(cp, FFT_N2, 128) * t2
        at = jnp.swapaxes(ap, 1, 2)
        p = jnp.dot(at.reshape(cp * 128, FFT_N2).astype(BF16), f2, preferred_element_type=F32)
        p = p.reshape(cp, 128, 2 * FFT_N2)
        out.append((p[:, :hn, :FFT_N2] - p[:, hn:, FFT_N2:], p[:, :hn, FFT_N2:] + p[:, hn:, :FFT_N2]))
    return out


def _inv_time(yre, yim, g2, c2, s2, wi_par):
    cp = yre.shape[0]
    hn = FFT_N1
    y = jnp.concatenate([yre, yim], axis=1).reshape(cp * 128, FFT_N2).astype(BF16)
    q = jnp.dot(y, g2, preferred_element_type=F32).reshape(cp, 128, 2 * FFT_N2)
    bre = q[:, :hn, :FFT_N2] - q[:, hn:, FFT_N2:]
    bim = q[:, :hn, FFT_N2:] + q[:, hn:, :FFT_N2]
    b2 = jnp.concatenate([bre * c2 - bim * s2, bre * s2 + bim * c2], axis=1)
    bt = jnp.swapaxes(b2, 1, 2)
    return jnp.dot(bt.reshape(cp * FFT_N2, 128).astype(BF16), wi_par, preferred_element_type=F32)


def _hyena_kernel(z_ref, g1_ref, g2_ref, kf_ref, bias_ref, w1p_ref, t1_ref, t2_ref, f2_ref, gi_ref, c2_ref,
                  s2_ref, wi_ref, o_ref):
    cp = z_ref.shape[0]
    z = z_ref[...].astype(F32)
    gates = (g1_ref, g2_ref)
    for o in range(HY_ORDER):
        spec = _fwd_spectrum(z, w1p_ref[...], t1_ref[...], t2_ref[...], f2_ref[...])
        conv = None
        for par in range(2):
            xre, xim = spec[par]
            kre = kf_ref[o, par, :, :FFT_N1, :]
            kim = kf_ref[o, par, :, FFT_N1:, :]
            part = _inv_time(xre * kre - xim * kim, xre * kim + xim * kre,
                             gi_ref[...], c2_ref[...], s2_ref[...], wi_ref[par])
            conv = part if conv is None else conv + part
        z = gates[o][...].astype(F32) * (conv.reshape(cp, FFT_N2, 128) + bias_ref[o] * z)
    o_ref[...] = z.astype(o_ref.dtype)


def _to_packed(v, bsz):
    _, length, ch = v.shape
    nn1 = length // FFT_N2
    return v.reshape(bsz, nn1, FFT_N2, ch // 2, 2).transpose(3, 2, 4, 0, 1).reshape(ch // 2, FFT_N2, 2 * bsz * nn1)


def _from_packed(vp, bsz):
    chp, _, lanes = vp.shape
    nn1 = lanes // (2 * bsz)
    return vp.reshape(chp, FFT_N2, 2, bsz, nn1).transpose(3, 4, 1, 0, 2).reshape(bsz, nn1 * FFT_N2, chp * 2)


def _hyena_conv(zp, g1p, g2p, kf, biasp, cp=8):
    chp = zp.shape[0]
    consts = _dft_constants()
    dspec = pl.BlockSpec((cp, FFT_N2, 128), lambda i: (i, 0, 0))
    full = lambda a: pl.BlockSpec(a.shape, lambda i: (0,) * a.ndim)
    return pl.pallas_call(
        _hyena_kernel,
        grid=(chp // cp,),
        in_specs=[dspec, dspec, dspec,
                  pl.BlockSpec((HY_ORDER, 2, cp, 128, FFT_N2), lambda i: (0, 0, i, 0, 0)),
                  pl.BlockSpec((HY_ORDER, cp, 1, 128), lambda i: (0, i, 0, 0))] + [full(a) for a in consts],
        out_specs=dspec,
        out_shape=jax.ShapeDtypeStruct(zp.shape, zp.dtype),
        compiler_params=pltpu.CompilerParams(
            dimension_semantics=("parallel",), vmem_limit_bytes=VMEM_LIMIT_BYTES),
        name="hyena_conv",
    )(zp, g1p, g2p, kf, biasp, *consts)


def _filter_time_kernel(w1t_ref, w1c_ref, w1s_ref, b1_ref, w2_ref, b2_ref, w3_ref, b3_ref, fr_ref,
                        wf_ref, wb_ref, df_ref, db_ref, o_ref, h_ref, k_ref, *, length):
    n_fft = 2 * length
    hp = lax.Precision.HIGHEST

    @pl.when(pl.program_id(0) == 0)
    def _():
        pos = lax.broadcasted_iota(jnp.int32, (1, n_fft), 1)
        lag = jnp.where(pos < length, pos, n_fft - pos).astype(F32)
        t = lag / float(length - 1)
        w = (2.0 * math.pi / length) * lag
        band_step = (HY_BANDS - 1 - 1e-4) / (HY_BANDS - 1)
        bands = 1e-4 + band_step * lax.broadcasted_iota(jnp.int32, (HY_BANDS, 1), 0).astype(F32)
        ang = bands * w
        fr = fr_ref[...]
        h = (w1t_ref[...] * t + jnp.dot(w1c_ref[...], jnp.cos(ang), preferred_element_type=F32, precision=hp)
             - jnp.dot(w1s_ref[...], jnp.sin(ang), preferred_element_type=F32, precision=hp))
        h = jnp.sin(fr * (h + b1_ref[...]))
        h = jnp.sin(fr * (jnp.dot(w2_ref[...], h, preferred_element_type=F32, precision=hp) + b2_ref[...]))
        h = jnp.sin(fr * (jnp.dot(w3_ref[...], h, preferred_element_type=F32, precision=hp) + b3_ref[...]))
        h_ref[...] = h

    pos = lax.broadcasted_iota(jnp.int32, (1, length), 1)
    tf = pos.astype(F32) / float(length - 1)
    tb = (length - pos).astype(F32) / float(length - 1)
    kf = (jnp.dot(wf_ref[...], h_ref[:, :length], preferred_element_type=F32, precision=hp)
          * jnp.exp(-tf * df_ref[...]))
    kb = (jnp.dot(wb_ref[...], h_ref[:, length:], preferred_element_type=F32, precision=hp)
          * jnp.exp(-tb * db_ref[...]))
    kb = jnp.where(pos == 0, 0.0, kb)
    inv = 1.0 / (jnp.sum(jnp.abs(kf), axis=1, keepdims=True) + jnp.sum(jnp.abs(kb), axis=1, keepdims=True))
    k_ref[:, :length] = kf * inv
    k_ref[:, length:] = kb * inv
    for n1 in range(FFT_N1):
        o_ref[:, n1, :] = k_ref[:, n1 * FFT_N2:(n1 + 1) * FFT_N2]


def _filter_time(length, w1, b1, w2, b2, w3, b3, freq, w_out, cb=128):
    col = lambda v: v.reshape(-1, 1)
    w1t = w1.T
    n_ch = w_out.shape[1]
    deltas = jnp.abs(jnp.linspace(math.log(HY_TARGET) / HY_SLOW_PCT, math.log(HY_TARGET) / HY_FAST_PCT,
                                  n_ch, dtype=F32)).reshape(n_ch, 1)
    wot = w_out.T
    nb = D_HY // cb
    small = lambda a: pl.BlockSpec(a.shape, lambda i: (0,) * a.ndim)
    fwd = lambda i: ((i // nb) * HY_DIRS * nb + i % nb, 0)
    bwd = lambda i: ((i // nb) * HY_DIRS * nb + nb + i % nb, 0)
    ins = [w1t[:, 0:1], w1t[:, 1:1 + HY_BANDS], w1t[:, 1 + HY_BANDS:], col(b1), w2.T, col(b2), w3.T, col(b3),
           col(freq)]
    hy_ff = w2.shape[0]
    return pl.pallas_call(
        functools.partial(_filter_time_kernel, length=length),
        grid=(HY_ORDER * nb,),
        in_specs=[small(a) for a in ins] + [pl.BlockSpec((cb, hy_ff), fwd), pl.BlockSpec((cb, hy_ff), bwd),
                                            pl.BlockSpec((cb, 1), fwd), pl.BlockSpec((cb, 1), bwd)],
        out_specs=pl.BlockSpec((cb, FFT_N1, FFT_N2), lambda i: (i, 0, 0)),
        out_shape=jax.ShapeDtypeStruct((HY_ORDER * D_HY, FFT_N1, FFT_N2), F32),
        scratch_shapes=[pltpu.VMEM((hy_ff, 2 * length), F32), pltpu.VMEM((cb, 2 * length), F32)],
        compiler_params=pltpu.CompilerParams(
            dimension_semantics=("arbitrary",), vmem_limit_bytes=VMEM_LIMIT_BYTES),
        name="hyena_filter_time",
    )(*ins, wot, wot, deltas, deltas)


def _filter_spec_kernel(k_ref, w1p_ref, t1_ref, t2_ref, f2_ref, o_ref):
    cb = k_ref.shape[0]
    kt = jnp.swapaxes(k_ref[...], 1, 2).reshape(cb // 2, 2, FFT_N2, FFT_N1)
    xp = jnp.concatenate([kt[:, 0], kt[:, 1]], axis=-1)
    spec = _fwd_spectrum(xp, w1p_ref[...], t1_ref[...], t2_ref[...], f2_ref[...])
    for par in range(2):
        o_ref[par, :, :FFT_N1, :] = spec[par][0]
        o_ref[par, :, FFT_N1:, :] = spec[par][1]


def _filter_spectrum(kt, cb=32):
    consts = _dft_constants(real_input=True)[:4]
    nb = D_HY // cb
    full = lambda a: pl.BlockSpec(a.shape, lambda i: (0,) * a.ndim)
    return pl.pallas_call(
        _filter_spec_kernel,
        grid=(HY_ORDER * nb,),
        in_specs=[pl.BlockSpec((cb, FFT_N1, FFT_N2), lambda i: (i, 0, 0))] + [full(a) for a in consts],
        out_specs=pl.BlockSpec((None, 2, cb // 2, 2 * FFT_N1, FFT_N2), lambda i: (i // nb, 0, i % nb, 0, 0)),
        out_shape=jax.ShapeDtypeStruct((HY_ORDER, 2, D_HY // 2, 2 * FFT_N1, FFT_N2), F32),
        compiler_params=pltpu.CompilerParams(
            dimension_semantics=("parallel",), vmem_limit_bytes=VMEM_LIMIT_BYTES),
        name="hyena_filter_spectrum",
    )(kt, *consts)


def _hyena(us, w1, b1, w2, b2, w3, b3, freq, w_out, bias):
    bsz, length, _ = us.shape
    assert 2 * length == FFT_N and bsz == 2, "one complex transform carries exactly two batch rows"
    zp, g1p, g2p = (_to_packed(us[..., i * D_HY:(i + 1) * D_HY], bsz) for i in range(HY_ORDER + 1))
    kf = _filter_spectrum(_filter_time(length, w1, b1, w2, b2, w3, b3, freq, w_out))
    biasp = jnp.repeat(bias.reshape(HY_ORDER, D_HY // 2, 1, 2), FFT_N1, axis=-1)
    return _from_packed(_hyena_conv(zp, g1p, g2p, kf, biasp), bsz)


def kernel(x, c, ctx, c_ctx, w_ada, b_ada, norm_g, ffn_w_gate, ffn_w_up, ffn_w_down, w_in,
           s5_lam_re, s5_lam_im, s5_log_dt, s5_b_re, s5_b_im, s5_c_re, s5_c_im, s5_d,
           hy_short_w, hy_short_b, hy_w1, hy_b1, hy_w2, hy_b2, hy_w3, hy_b3, hy_freq, hy_w_out,
           hy_bias, w_pa, w_pb, w_out, final_g):
    bsz, seq, d = x.shape
    ctx_len = ctx.shape[1]
    n_rows = seq // GRID_W
    depth = w_ada.shape[0]
    assert depth == 1, "context-token outputs are only dropped by the last layer"
    l = 0

    c_rows = jnp.concatenate([c, c_ctx[None, :], jnp.zeros((8 - bsz - 1, d), F32)], axis=0)
    mod_all = _ada_mod(c_rows, w_ada[l], b_ada[l])
    mod = mod_all[:bsz].reshape(bsz, N_SUB, N_MOD, 1, d)
    mod_c = mod_all[bsz:bsz + 1].reshape(1, N_SUB, N_MOD, 1, d)

    def mods(m, sub):
        return tuple(m[:, sub, k] for k in range(N_MOD))

    wg = ffn_w_gate[l].astype(BF16)
    wu = ffn_w_up[l].astype(BF16)
    wd = ffn_w_down[l].astype(BF16)
    w_in_b = w_in[l].astype(BF16)

    xt = x.reshape(bsz * seq, d)
    ct = ctx.reshape(bsz * ctx_len, d)

    xt = _ffn_sublayer(xt, mods(mod, 0), norm_g[l, 0], wg[0], wu[0], wd[0])
    ct = _ffn_sublayer(ct, mods(mod_c, 0), norm_g[l, 0], wg[0], wu[0], wd[0])

    assert GRID_W * n_rows == seq
    u_s5, us_hy, sig_gates = _in_proj(xt, mod[:, 1, 0], mod[:, 1, 1], norm_g[l, 1], w_in_b,
                                      hy_short_w[l], hy_short_b[l], n_u=I_HY, n_hy=I_GA - I_HY)
    (u_ctx,) = _in_proj(ct, mod_c[:, 1, 0], mod_c[:, 1, 1], norm_g[l, 1], w_in_b[:, :D_S5])

    y_s5 = _s5_bidirectional(u_s5.reshape(bsz, seq, D_S5), u_ctx.reshape(bsz, ctx_len, D_S5),
                             s5_lam_re[l], s5_lam_im[l], s5_log_dt[l],
                             s5_b_re[l], s5_b_im[l], s5_c_re[l], s5_c_im[l], s5_d[l])
    y_hy = _hyena(us_hy.reshape(bsz, seq, I_GA - I_HY),
                  hy_w1[l], hy_b1[l], hy_w2[l], hy_b2[l], hy_w3[l], hy_b3[l], hy_freq[l],
                  hy_w_out[l], hy_bias[l])

    xt = _merge(xt, mod[:, 1, 2], y_s5.reshape(bsz * seq, D_S5), y_hy.reshape(bsz * seq, D_HY),
                sig_gates, w_pa[l].astype(BF16), w_pb[l].astype(BF16), w_out[l].astype(BF16))

    xt = _ffn_sublayer(xt, mods(mod, 2), norm_g[l, 2], wg[1], wu[1], wd[1], final_gain=final_g)
    return xt.reshape(bsz, seq, d)
```

```python
import functools
import math

import jax
import jax.numpy as jnp
import numpy as np
from jax import lax
from jax.experimental import pallas as pl
from jax.experimental.pallas import tpu as pltpu

F32 = jnp.float32
BF16 = jnp.bfloat16

D_MODEL = 2048
GRID_W = 64
D_S5 = 1024
S5_GROUP = 16
S5_GROUPS = D_S5 // S5_GROUP
S5_STATE = 64
S5_DIRS = 2
LAMBDA_RE_MAX = -1e-4
S5_CHUNK = 16
D_HY = 1024
HY_ORDER = 2
HY_DIRS = 2
HY_SHORT = 3
HY_EMB = 33
HY_BANDS = (HY_EMB - 1) // 2
HY_TARGET = 1e-2
HY_FAST_PCT = 0.3
HY_SLOW_PCT = 1.5
FFT_N1 = 64
FFT_N2 = 128
FFT_N = FFT_N1 * FFT_N2
I_HY = D_S5
I_GA = D_S5 + (HY_ORDER + 1) * D_HY
I_GB = I_GA + D_MODEL
D_IN = I_GB + D_MODEL
D_FF = 5632
N_SUB = 3
N_MOD = 3
HALF_STEP = 0.5
RMS_EPS = 1e-6

VMEM_LIMIT_BYTES = 58 * 1024 * 1024


def _rms_mod(x, gain, shift, scale):
    ms = jnp.mean(x * x, axis=-1, keepdims=True)
    y = x * lax.rsqrt(ms + RMS_EPS) * gain
    return y * (1.0 + scale) + shift


def _ada_kernel(c_ref, w_ref, b_ref, o_ref):
    c = c_ref[...]
    a = c * jax.nn.sigmoid(c)
    o_ref[...] = jnp.dot(a, w_ref[...], preferred_element_type=F32,
                         precision=lax.Precision.HIGHEST) + b_ref[...]


def _ada_mod(c_rows, w, b, tn=1024):
    rows, d = c_rows.shape
    n = w.shape[1]
    return pl.pallas_call(
        _ada_kernel,
        grid=(n // tn,),
        in_specs=[pl.BlockSpec((rows, d), lambda j: (0, 0)),
                  pl.BlockSpec((d, tn), lambda j: (0, j)),
                  pl.BlockSpec((1, tn), lambda j: (0, j))],
        out_specs=pl.BlockSpec((rows, tn), lambda j: (0, j)),
        out_shape=jax.ShapeDtypeStruct((rows, n), F32),
        compiler_params=pltpu.CompilerParams(
            dimension_semantics=("arbitrary",), vmem_limit_bytes=VMEM_LIMIT_BYTES),
        name="ada_mod",
    )(c_rows, w, b.reshape(1, n))


def _ffn_kernel(x_ref, shift_ref, scale_ref, gate_ref, gain_ref, wg_ref, wu_ref, wd_ref,
                fg_ref, o_ref, h_ref, *, final_norm):
    j = pl.program_id(1)

    @pl.when(j == 0)
    def _():
        h_ref[...] = _rms_mod(x_ref[...], gain_ref[...], shift_ref[...], scale_ref[...]).astype(BF16)
        o_ref[...] = jnp.zeros_like(o_ref)

    h = h_ref[...]
    g = jnp.dot(h, wg_ref[...].astype(BF16), preferred_element_type=F32)
    u = jnp.dot(h, wu_ref[...].astype(BF16), preferred_element_type=F32)
    a = (g * jax.nn.sigmoid(g) * u).astype(BF16)
    o_ref[...] += jnp.dot(a, wd_ref[...].astype(BF16), preferred_element_type=F32)

    @pl.when(j == pl.num_programs(1) - 1)
    def _():
        y = x_ref[...] + (HALF_STEP * gate_ref[...]) * o_ref[...]
        if final_norm:
            ms = jnp.mean(y * y, axis=-1, keepdims=True)
            y = y * lax.rsqrt(ms + RMS_EPS) * fg_ref[...]
        o_ref[...] = y


def _ffn_sublayer(x, mods, gain, wg, wu, wd, final_gain=None, tm=1024, tf=256):
    t, d = x.shape
    bm = mods[0].shape[0]
    tm = min(tm, t // bm)
    blocks_per_batch = (t // bm) // tm
    dff = wg.shape[1]
    final_norm = final_gain is not None
    fg = final_gain if final_norm else gain
    mod_spec = pl.BlockSpec((None, 1, d), lambda i, j: (i // blocks_per_batch, 0, 0))
    vec_spec = pl.BlockSpec((1, d), lambda i, j: (0, 0))
    return pl.pallas_call(
        functools.partial(_ffn_kernel, final_norm=final_norm),
        grid=(t // tm, dff // tf),
        in_specs=[pl.BlockSpec((tm, d), lambda i, j: (i, 0)),
                  mod_spec, mod_spec, mod_spec, vec_spec,
                  pl.BlockSpec((d, tf), lambda i, j: (0, j)),
                  pl.BlockSpec((d, tf), lambda i, j: (0, j)),
                  pl.BlockSpec((tf, d), lambda i, j: (j, 0)),
                  vec_spec],
        out_specs=pl.BlockSpec((tm, d), lambda i, j: (i, 0)),
        out_shape=jax.ShapeDtypeStruct((t, d), F32),
        scratch_shapes=[pltpu.VMEM((tm, d), BF16)],
        compiler_params=pltpu.CompilerParams(
            dimension_semantics=("parallel", "arbitrary"), vmem_limit_bytes=VMEM_LIMIT_BYTES),
        name="ffn_final" if final_norm else "ffn",
    )(x, *mods, gain.reshape(1, d), wg, wu, wd, fg.reshape(1, d))


def _proj_kernel(x_ref, shift_ref, scale_ref, gain_ref, w_ref, sw_ref, sb_ref, *rest, n_u, n_hy, row_len):
    o_refs, h_ref = rest[:-1], rest[-1]
    j = pl.program_id(1)

    @pl.when(j == 0)
    def _():
        h_ref[...] = _rms_mod(x_ref[...], gain_ref[...], shift_ref[...], scale_ref[...]).astype(BF16)

    p = jnp.dot(h_ref[...], w_ref[...].astype(BF16), preferred_element_type=F32)
    if n_hy == 0:
        o_refs[0][...] = p.astype(o_refs[0].dtype)
        return

    @pl.when(j < n_u)
    def _():
        o_refs[0][...] = p.astype(o_refs[0].dtype)

    @pl.when((j >= n_u) & (j < n_u + n_hy))
    def _():
        tm = p.shape[0]
        col = lax.broadcasted_iota(jnp.int32, p.shape, 0) % row_len
        prev = jnp.where(col == 0, 0.0, pltpu.roll(p, 1, 0))
        nxt = jnp.where(col == row_len - 1, 0.0, pltpu.roll(p, tm - 1, 0))
        sw = sw_ref[...]
        us = sb_ref[...] + prev * sw[0:1] + p * sw[1:2] + nxt * sw[2:3]
        o_refs[1][...] = us.astype(o_refs[1].dtype)

    @pl.when(j >= n_u + n_hy)
    def _():
        o_refs[2][...] = jax.nn.sigmoid(p).astype(BF16)


def _in_proj(x, shift, scale, gain, w, short_w=None, short_b=None, n_u=D_S5, n_hy=0, row_len=GRID_W,
             tm=1024, tn=512):
    t, d = x.shape
    bm = shift.shape[0]
    tm = min(tm, t // bm)
    blocks_per_batch = (t // bm) // tm
    n = w.shape[1]
    assert tm % row_len == 0 and (t // bm) % tm == 0
    bu, bh = n_u // tn, n_hy // tn
    bg = n // tn - bu - bh
    mod_spec = pl.BlockSpec((None, 1, d), lambda i, j: (i // blocks_per_batch, 0, 0))
    out_shape = [jax.ShapeDtypeStruct((t, n_u), BF16)]
    out_specs = [pl.BlockSpec((tm, tn), lambda i, j: (i, jnp.minimum(j, bu - 1)))]
    if bh:
        out_shape += [jax.ShapeDtypeStruct((t, n_hy), BF16), jax.ShapeDtypeStruct((t, bg * tn), BF16)]
        out_specs += [pl.BlockSpec((tm, tn), lambda i, j: (i, jnp.clip(j - bu, 0, bh - 1))),
                      pl.BlockSpec((tm, tn), lambda i, j: (i, jnp.maximum(j - bu - bh, 0)))]
        sw, sb = short_w, short_b.reshape(1, n_hy)
        hy_blk = lambda i, j: (0, jnp.clip(j - bu, 0, bh - 1))
    else:
        sw, sb = jnp.zeros((HY_SHORT, tn), F32), jnp.zeros((1, tn), F32)
        hy_blk = lambda i, j: (0, 0)
    return pl.pallas_call(
        functools.partial(_proj_kernel, n_u=bu, n_hy=bh, row_len=row_len),
        grid=(t // tm, n // tn),
        in_specs=[pl.BlockSpec((tm, d), lambda i, j: (i, 0)),
                  mod_spec, mod_spec,
                  pl.BlockSpec((1, d), lambda i, j: (0, 0)),
                  pl.BlockSpec((d, tn), lambda i, j: (0, j)),
                  pl.BlockSpec((HY_SHORT, tn), hy_blk),
                  pl.BlockSpec((1, tn), hy_blk)],
        out_specs=out_specs,
        out_shape=out_shape,
        scratch_shapes=[pltpu.VMEM((tm, d), BF16)],
        compiler_params=pltpu.CompilerParams(
            dimension_semantics=("parallel", "arbitrary"), vmem_limit_bytes=VMEM_LIMIT_BYTES),
        name="in_proj",
    )(x, shift, scale, gain.reshape(1, d), w, sw, sb)


def _gelu_tanh(x):
    return 0.5 * x * (1.0 + jnp.tanh(math.sqrt(2.0 / math.pi) * (x + 0.044715 * (x * x * x))))


def _merge_kernel(x_ref, gate_ref, ys_ref, yh_ref, ga_ref, gb_ref, wpa_lo_ref, wpa_hi_ref, wpb_ref,
                  wout_ref, o_ref, s_ref, acc_ref):
    j = pl.program_id(1)

    @pl.when(j == 0)
    def _():
        s_ref[...] = _gelu_tanh(ys_ref[...].astype(F32)).astype(BF16)
        acc_ref[...] = jnp.zeros_like(acc_ref)

    s = s_ref[...]
    pa_lo = jnp.dot(s, wpa_lo_ref[...], preferred_element_type=F32)
    pa_hi = jnp.dot(s, wpa_hi_ref[...], preferred_element_type=F32)
    y_a = pa_lo * jax.nn.sigmoid(pa_hi)
    y_b = jnp.dot(yh_ref[...], wpb_ref[...], preferred_element_type=F32)
    m = ga_ref[...].astype(F32) * y_a + gb_ref[...].astype(F32) * y_b
    acc_ref[...] += jnp.dot(m.astype(BF16), wout_ref[...], preferred_element_type=F32)

    @pl.when(j == pl.num_programs(1) - 1)
    def _():
        o_ref[...] = x_ref[...] + gate_ref[...] * acc_ref[...]


def _merge(x, gate, y_s5, y_hy, sig_gates, w_pa, w_pb, w_out, tm=512, tn=512):
    t, d = x.shape
    bm = gate.shape[0]
    blocks_per_batch = (t // bm) // tm
    nj = d // tn
    ds5 = y_s5.shape[1]
    dhy = y_hy.shape[1]
    return pl.pallas_call(
        _merge_kernel,
        grid=(t // tm, nj),
        in_specs=[pl.BlockSpec((tm, d), lambda i, j: (i, 0)),
                  pl.BlockSpec((None, 1, d), lambda i, j: (i // blocks_per_batch, 0, 0)),
                  pl.BlockSpec((tm, ds5), lambda i, j: (i, 0)),
                  pl.BlockSpec((tm, dhy), lambda i, j: (i, 0)),
                  pl.BlockSpec((tm, tn), lambda i, j: (i, j)),
                  pl.BlockSpec((tm, tn), lambda i, j: (i, nj + j)),
                  pl.BlockSpec((ds5, tn), lambda i, j: (0, j)),
                  pl.BlockSpec((ds5, tn), lambda i, j: (0, nj + j)),
                  pl.BlockSpec((dhy, tn), lambda i, j: (0, j)),
                  pl.BlockSpec((tn, d), lambda i, j: (j, 0))],
        out_specs=pl.BlockSpec((tm, d), lambda i, j: (i, 0)),
        out_shape=jax.ShapeDtypeStruct((t, d), F32),
        scratch_shapes=[pltpu.VMEM((tm, ds5), BF16), pltpu.VMEM((tm, d), F32)],
        compiler_params=pltpu.CompilerParams(
            dimension_semantics=("parallel", "arbitrary"), vmem_limit_bytes=VMEM_LIMIT_BYTES),
        name="merge",
    )(x, gate, y_s5, y_hy, sig_gates, sig_gates, w_pa, w_pa, w_pb, w_out)


def _s5_weights_kernel(par_ref, bt_ref, c_ref, d_ref, bpow_ref, toep_ref, cpow_ref, tab_ref, ca_ref, *, n_steps):
    t, h, p = S5_CHUNK, S5_GROUP, S5_STATE
    lanes = 2 * p
    hp = lax.Precision.HIGHEST
    sgn = jnp.where(lax.broadcasted_iota(jnp.int32, (1, lanes), 1) < p, -1.0, 1.0)
    par = par_ref[...]
    gsum = None
    for d in range(S5_DIRS):
        lr = jnp.minimum(par[3 * d:3 * d + 1], LAMBDA_RE_MAX)
        li = par[3 * d + 1:3 * d + 2]
        dt = jnp.exp(par[3 * d + 2:3 * d + 3])
        zr, zi = lr * dt, li * dt

        def apow(j):
            mag = jnp.exp(j * zr)
            return mag * jnp.cos(j * zi), sgn * (mag * jnp.sin(j * zi))

        def cmul(x, a1, a2):
            return x * a1 + pltpu.roll(x, p, 1) * a2

        a1, a2 = apow(lax.broadcasted_iota(jnp.int32, (t + 1, 1), 0).astype(F32))
        nr, ni = a1[1:2] - 1.0, sgn * a2[1:2]
        den = lr * lr + li * li
        f_re = (nr * lr + ni * li) / den
        f_im = (ni * lr - nr * li) / den
        bbar = cmul(bt_ref[d], f_re, sgn * f_im)
        cc = c_ref[d]
        ca = [cmul(cc, a1[j:j + 1], a2[j:j + 1]) * (-sgn) for j in range(t + 1)]
        ca_ref[...] = jnp.zeros_like(ca_ref)
        for k in range(t):
            e_b, e_c = (t - 1 - k, k + 1) if d == 0 else (k, t - k)
            bpow_ref[k * h:(k + 1) * h, d * lanes:(d + 1) * lanes] = (
                cmul(bbar, a1[e_b:e_b + 1], a2[e_b:e_b + 1]).astype(BF16))
            cpow_ref[k * h:(k + 1) * h, d * lanes:(d + 1) * lanes] = ca[e_c].astype(BF16)
            l = t - 1 + k if d == 0 else t - 1 - k
            ca_ref[l * h:(l + 1) * h, :] = ca[k]
        g = lax.dot_general(bbar, ca_ref[...], (((1,), (1,)), ((), ())), preferred_element_type=F32, precision=hp)
        gsum = g if gsum is None else gsum + g
        for s in range(n_steps):
            s1, s2 = apow(float(t * 2 ** s))
            r = d * 2 * n_steps + 2 * s
            tab_ref[r:r + 1, :] = s1
            tab_ref[r + 1:r + 2, :] = s2
    wide = gsum.shape[1]
    col = lax.broadcasted_iota(jnp.int32, (h, wide), 1)
    row = lax.broadcasted_iota(jnp.int32, (h, wide), 0)
    gsum = gsum + jnp.where(col - (t - 1) * h == row, d_ref[...], 0.0)
    for k in range(t):
        off = (t - 1 - k) * h
        shifted = gsum if off == 0 else pltpu.roll(gsum, wide - off, 1)
        toep_ref[k * h:(k + 1) * h, :] = shifted[:, :t * h].astype(BF16)


def _s5_weights(lam_re, lam_im, log_dt, b_re, b_im, c_re, c_im, d_skip, n_steps):
    g, p, h, t = S5_GROUPS, S5_STATE, S5_GROUP, S5_CHUNK
    cat2 = lambda a: jnp.concatenate([a, a], axis=-1)
    par = jnp.stack([cat2(lam_re), cat2(lam_im), jnp.broadcast_to(log_dt[..., None], (S5_DIRS, g, 2 * p))], axis=1)
    par = par.transpose(2, 0, 1, 3).reshape(g, 3 * S5_DIRS, 2 * p)
    btc = jnp.concatenate([b_re, b_im], axis=2).transpose(1, 0, 3, 2)
    ccat = jnp.concatenate([c_re, c_im], axis=3).transpose(1, 0, 2, 3)
    wide = 2 * t * h
    drow = jnp.zeros((g, 1, wide), F32).at[:, 0, (t - 1) * h:t * h].set(d_skip.reshape(g, h))
    w = t * h
    sq = pl.BlockSpec((None, w, w), lambda i: (i, 0, 0))
    return pl.pallas_call(
        functools.partial(_s5_weights_kernel, n_steps=n_steps),
        grid=(g,),
        in_specs=[pl.BlockSpec((None, 3 * S5_DIRS, 2 * p), lambda i: (i, 0, 0)),
                  pl.BlockSpec((None, S5_DIRS, h, 2 * p), lambda i: (i, 0, 0, 0)),
                  pl.BlockSpec((None, S5_DIRS, h, 2 * p), lambda i: (i, 0, 0, 0)),
                  pl.BlockSpec((None, 1, wide), lambda i: (i, 0, 0))],
        out_specs=[sq, sq, sq, pl.BlockSpec((None, 4 * n_steps, 2 * p), lambda i: (i, 0, 0))],
        out_shape=[jax.ShapeDtypeStruct((g, w, w), BF16)] * 3 + [jax.ShapeDtypeStruct((g, 4 * n_steps, 2 * p), F32)],
        scratch_shapes=[pltpu.VMEM((wide, 2 * p), F32)],
        compiler_params=pltpu.CompilerParams(
            dimension_semantics=("parallel",), vmem_limit_bytes=VMEM_LIMIT_BYTES),
        name="s5_weights",
    )(par, btc, ccat, drow)


def _s5_kernel(ul_ref, uc_ref, bpow_ref, toep_ref, cpow_ref, tab_ref, y_ref, *, bsz, n_steps):
    n_lat = ul_ref.shape[0] // bsz
    n_ctx = uc_ref.shape[0] // bsz
    n_ch = n_lat + n_ctx
    rows = bsz * n_ch
    half = 2 * S5_STATE
    ul = ul_ref[...]
    uc = uc_ref[...]
    bpow = bpow_ref[...]
    zl = jnp.dot(ul, bpow, preferred_element_type=F32)
    zc = jnp.dot(uc, bpow, preferred_element_type=F32)
    fparts, bparts = [], []
    for b in range(bsz):
        lat = slice(b * n_lat, (b + 1) * n_lat)
        ctx = slice(b * n_ctx, (b + 1) * n_ctx)
        fparts += [zc[ctx, :half], zl[lat, :half]]
        bparts += [zl[lat, half:], zc[ctx, half:]]
    fw = jnp.concatenate(fparts, axis=0)
    bw = jnp.concatenate(bparts, axis=0)
    rib = lax.broadcasted_iota(jnp.int32, (rows, half), 0) % n_ch
    tab = tab_ref[...]

    def cmul_add(acc, sh, a1, a2):
        return acc + a1 * sh + a2 * pltpu.roll(sh, S5_STATE, 1)

    for s in range(n_steps):
        d = 1 << s
        sh = jnp.where(rib >= d, pltpu.roll(fw, d, 0), 0.0)
        fw = cmul_add(fw, sh, tab[2 * s:2 * s + 1], tab[2 * s + 1:2 * s + 2])
        o = 2 * n_steps
        sh = jnp.where(rib < n_ch - d, pltpu.roll(bw, rows - d, 0), 0.0)
        bw = cmul_add(bw, sh, tab[o + 2 * s:o + 2 * s + 1], tab[o + 2 * s + 1:o + 2 * s + 2])
    fe = jnp.where(rib >= 1, pltpu.roll(fw, 1, 0), 0.0)
    be = jnp.where(rib < n_ch - 1, pltpu.roll(bw, rows - 1, 0), 0.0)
    fl = jnp.concatenate([fe[b * n_ch + n_ctx:(b + 1) * n_ch] for b in range(bsz)], axis=0)
    bl = jnp.concatenate([be[b * n_ch:b * n_ch + n_lat] for b in range(bsz)], axis=0)
    st = jnp.concatenate([fl, bl], axis=1).astype(BF16)
    y = (jnp.dot(ul, toep_ref[...], preferred_element_type=F32)
         + lax.dot_general(st, cpow_ref[...], (((1,), (1,)), ((), ())), preferred_element_type=F32))
    y_ref[...] = y.astype(y_ref.dtype)


def _s5_mix(u2, u2c, bpw, toep, cpw, tab, bsz):
    g, rl, w = u2.shape
    rc = u2c.shape[1]
    n_steps = tab.shape[1] // 4
    wspec = pl.BlockSpec((None, w, w), lambda i: (i, 0, 0))
    return pl.pallas_call(
        functools.partial(_s5_kernel, bsz=bsz, n_steps=n_steps),
        grid=(g,),
        in_specs=[pl.BlockSpec((None, rl, w), lambda i: (i, 0, 0)),
                  pl.BlockSpec((None, rc, w), lambda i: (i, 0, 0)),
                  wspec, wspec, wspec,
                  pl.BlockSpec((None, 4 * n_steps, tab.shape[2]), lambda i: (i, 0, 0))],
        out_specs=pl.BlockSpec((None, rl, w), lambda i: (i, 0, 0)),
        out_shape=jax.ShapeDtypeStruct((g, rl, w), BF16),
        compiler_params=pltpu.CompilerParams(
            dimension_semantics=("parallel",), vmem_limit_bytes=VMEM_LIMIT_BYTES),
        name="s5_mix",
    )(u2, u2c, bpw, toep, cpw, tab)


def _s5_bidirectional(u, u_ctx, lam_re, lam_im, log_dt, b_re, b_im, c_re, c_im, d_skip):
    bsz, length, dm = u.shape
    ctx_len = u_ctx.shape[1]
    t, g, h = S5_CHUNK, S5_GROUPS, S5_GROUP
    n_lat, n_ctx = length // t, ctx_len // t
    n_steps = max(1, math.ceil(math.log2(n_lat + n_ctx)))

    def fold(a, n):
        return a.reshape(bsz, n, t, g, h).transpose(3, 0, 1, 2, 4).reshape(g, bsz * n, t * h)

    bpw, toep, cpw, tab = _s5_weights(lam_re, lam_im, log_dt, b_re, b_im, c_re, c_im, d_skip, n_steps)
    y2 = _s5_mix(fold(u, n_lat), fold(u_ctx, n_ctx), bpw, toep, cpw, tab, bsz)
    return y2.reshape(g, bsz, n_lat, t, h).transpose(1, 2, 3, 0, 4).reshape(bsz, length, dm)


def _dft_constants(real_input=False):
    n1 = np.arange(FFT_N1)
    n2 = np.arange(FFT_N2)
    half = FFT_N1 // 2
    th = 2 * np.pi * np.outer(n1, n1) / FFT_N1
    c1, s1 = np.cos(th), np.sin(th)
    if real_input:
        w1 = np.concatenate([c1, -s1], axis=1)
    else:
        w1 = np.concatenate([np.concatenate([c1[:half], -s1[:half]], axis=1),
                             np.concatenate([s1[:half], c1[:half]], axis=1)], axis=0)
    z = np.zeros_like(w1)
    w1p = np.block([[w1, z], [z, w1]])
    ph = 2 * np.pi * np.outer(n2, n1) / FFT_N
    t1 = np.concatenate([np.cos(ph), np.cos(ph)], axis=1)
    t2 = np.concatenate([np.sin(ph), -np.sin(ph)], axis=1)
    ps = 2 * np.pi * np.outer(n2, n2) / FFT_N2
    f2 = np.concatenate([np.cos(ps), -np.sin(ps)], axis=1)
    g2 = np.concatenate([np.cos(ps), np.sin(ps)], axis=1)
    c2, s2 = np.cos(ph).T, np.sin(ph).T
    wi = np.concatenate([np.concatenate([c1[:, :half], s1[:, :half]], axis=1),
                         np.concatenate([-s1[:, :half], c1[:, :half]], axis=1)], axis=0) / FFT_N
    zi = np.zeros_like(wi)
    wi2 = np.stack([np.concatenate([wi, zi], axis=1), np.concatenate([zi, wi], axis=1)])
    as_b = lambda a: jnp.asarray(a, F32).astype(BF16)
    as_f = lambda a: jnp.asarray(a, F32)
    return [as_b(w1p), as_f(t1), as_f(t2), as_b(f2), as_b(g2), as_f(c2), as_f(s2), as_b(wi2)]


def _fwd_spectrum(xp, w1p, t1, t2, f2):
    cp = xp.shape[0]
    hn = FFT_N1
    a = jnp.dot(xp.reshape(cp * FFT_N2, 128).astype(BF16), w1p, preferred_element_type=F32)
    out = []
    for par in range(2):
        ap = a[:, par * 128:(par + 1) * 128]
        ap = ap.reshape(cp, FFT_N2, 128) * t1 + pltpu.roll(ap, hn, 1).reshape(cp, FFT_N2, 128) * t2
        at = jnp.swapaxes(ap, 1, 2)
        p = jnp.dot(at.reshape(cp * 128, FFT_N2).astype(BF16), f2, preferred_element_type=F32)
        p = p.reshape(cp, 128, 2 * FFT_N2)
        out.append((p[:, :hn, :FFT_N2] - p[:, hn:, FFT_N2:], p[:, :hn, FFT_N2:] + p[:, hn:, :FFT_N2]))
    return out


def _inv_time(yre, yim, g2, c2, s2, wi_par):
    cp = yre.shape[0]
    hn = FFT_N1
    y = jnp.concatenate([yre, yim], axis=1).reshape(cp * 128, FFT_N2).astype(BF16)
    q = jnp.dot(y, g2, preferred_element_type=F32).reshape(cp, 128, 2 * FFT_N2)
    bre = q[:, :hn, :FFT_N2] - q[:, hn:, FFT_N2:]
    bim = q[:, :hn, FFT_N2:] + q[:, hn:, :FFT_N2]
    b2 = jnp.concatenate([bre * c2 - bim * s2, bre * s2 + bim * c2], axis=1)
    bt = jnp.swapaxes(b2, 1, 2)
    return jnp.dot(bt.reshape(cp * FFT_N2, 128).astype(BF16), wi_par, preferred_element_type=F32)


def _hyena_kernel(z_ref, g1_ref, g2_ref, kf_ref, bias_ref, w1p_ref, t1_ref, t2_ref, f2_ref, gi_ref, c2_ref,
                  s2_ref, wi_ref, o_ref):
    cp = z_ref.shape[0]
    z = z_ref[...].astype(F32)
    gates = (g1_ref, g2_ref)
    for o in range(HY_ORDER):
        spec = _fwd_spectrum(z, w1p_ref[...], t1_ref[...], t2_ref[...], f2_ref[...])
        conv = None
        for par in range(2):
            xre, xim = spec[par]
            kre = kf_ref[o, par, :, :FFT_N1, :]
            kim = kf_ref[o, par, :, FFT_N1:, :]
            part = _inv_time(xre * kre - xim * kim, xre * kim + xim * kre,
                             gi_ref[...], c2_ref[...], s2_ref[...], wi_ref[par])
            conv = part if conv is None else conv + part
        z = gates[o][...].astype(F32) * (conv.reshape(cp, FFT_N2, 128) + bias_ref[o] * z)
    o_ref[...] = z.astype(o_ref.dtype)


def _to_packed(v, bsz):
    _, length, ch = v.shape
    nn1 = length // FFT_N2
    return v.reshape(bsz, nn1, FFT_N2, ch // 2, 2).transpose(3, 2, 4, 0, 1).reshape(ch // 2, FFT_N2, 2 * bsz * nn1)


def _from_packed(vp, bsz):
    chp, _, lanes = vp.shape
    nn1 = lanes // (2 * bsz)
    return vp.reshape(chp, FFT_N2, 2, bsz, nn1).transpose(3, 4, 1, 0, 2).reshape(bsz, nn1 * FFT_N2, chp * 2)


def _hyena_conv(zp, g1p, g2p, kf, biasp, cp=8):
    chp = zp.shape[0]
    consts = _dft_constants()
    dspec = pl.BlockSpec((cp, FFT_N2, 128), lambda i: (i, 0, 0))
    full = lambda a: pl.BlockSpec(a.shape, lambda i: (0,) * a.ndim)
    return pl.pallas_call(
        _hyena_kernel,
        grid=(chp // cp,),
        in_specs=[dspec, dspec, dspec,
                  pl.BlockSpec((HY_ORDER, 2, cp, 128, FFT_N2), lambda i: (0, 0, i, 0, 0)),
                  pl.BlockSpec((HY_ORDER, cp, 1, 128), lambda i: (0, i, 0, 0))] + [full(a) for a in consts],
        out_specs=dspec,
        out_shape=jax.ShapeDtypeStruct(zp.shape, zp.dtype),
        compiler_params=pltpu.CompilerParams(
            dimension_semantics=("parallel",), vmem_limit_bytes=VMEM_LIMIT_BYTES),
        name="hyena_conv",
    )(zp, g1p, g2p, kf, biasp, *consts)


def _filter_time_kernel(w1t_ref, w1c_ref, w1s_ref, b1_ref, w2_ref, b2_ref, w3_ref, b3_ref, fr_ref,
                        wf_ref, wb_ref, df_ref, db_ref, o_ref, h_ref, k_ref, *, length):
    n_fft = 2 * length
    hp = lax.Precision.HIGHEST

    @pl.when(pl.program_id(0) == 0)
    def _():
        pos = lax.broadcasted_iota(jnp.int32, (1, n_fft), 1)
        lag = jnp.where(pos < length, pos, n_fft - pos).astype(F32)
        t = lag / float(length - 1)
        w = (2.0 * math.pi / length) * lag
        band_step = (HY_BANDS - 1 - 1e-4) / (HY_BANDS - 1)
        bands = 1e-4 + band_step * lax.broadcasted_iota(jnp.int32, (HY_BANDS, 1), 0).astype(F32)
        ang = bands * w
        fr = fr_ref[...]
        h = (w1t_ref[...] * t + jnp.dot(w1c_ref[...], jnp.cos(ang), preferred_element_type=F32, precision=hp)
             - jnp.dot(w1s_ref[...], jnp.sin(ang), preferred_element_type=F32, precision=hp))
        h = jnp.sin(fr * (h + b1_ref[...]))
        h = jnp.sin(fr * (jnp.dot(w2_ref[...], h, preferred_element_type=F32, precision=hp) + b2_ref[...]))
        h = jnp.sin(fr * (jnp.dot(w3_ref[...], h, preferred_element_type=F32, precision=hp) + b3_ref[...]))
        h_ref[...] = h

    pos = lax.broadcasted_iota(jnp.int32, (1, length), 1)
    tf = pos.astype(F32) / float(length - 1)
    tb = (length - pos).astype(F32) / float(length - 1)
    kf = (jnp.dot(wf_ref[...], h_ref[:, :length], preferred_element_type=F32, precision=hp)
          * jnp.exp(-tf * df_ref[...]))
    kb = (jnp.dot(wb_ref[...], h_ref[:, length:], preferred_element_type=F32, precision=hp)
          * jnp.exp(-tb * db_ref[...]))
    kb = jnp.where(pos == 0, 0.0, kb)
    inv = 1.0 / (jnp.sum(jnp.abs(kf), axis=1, keepdims=True) + jnp.sum(jnp.abs(kb), axis=1, keepdims=True))
    k_ref[:, :length] = kf * inv
    k_ref[:, length:] = kb * inv
    for n1 in range(FFT_N1):
        o_ref[:, n1, :] = k_ref[:, n1 * FFT_N2:(n1 + 1) * FFT_N2]


def _filter_time(length, w1, b1, w2, b2, w3, b3, freq, w_out, cb=128):
    col = lambda v: v.reshape(-1, 1)
    w1t = w1.T
    n_ch = w_out.shape[1]
    deltas = jnp.abs(jnp.linspace(math.log(HY_TARGET) / HY_SLOW_PCT, math.log(HY_TARGET) / HY_FAST_PCT,
                                  n_ch, dtype=F32)).reshape(n_ch, 1)
    wot = w_out.T
    nb = D_HY // cb
    small = lambda a: pl.BlockSpec(a.shape, lambda i: (0,) * a.ndim)
    fwd = lambda i: ((i // nb) * HY_DIRS * nb + i % nb, 0)
    bwd = lambda i: ((i // nb) * HY_DIRS * nb + nb + i % nb, 0)
    ins = [w1t[:, 0:1], w1t[:, 1:1 + HY_BANDS], w1t[:, 1 + HY_BANDS:], col(b1), w2.T, col(b2), w3.T, col(b3),
           col(freq)]
    hy_ff = w2.shape[0]
    return pl.pallas_call(
        functools.partial(_filter_time_kernel, length=length),
        grid=(HY_ORDER * nb,),
        in_specs=[small(a) for a in ins] + [pl.BlockSpec((cb, hy_ff), fwd), pl.BlockSpec((cb, hy_ff), bwd),
                                            pl.BlockSpec((cb, 1), fwd), pl.BlockSpec((cb, 1), bwd)],
        out_specs=pl.BlockSpec((cb, FFT_N1, FFT_N2), lambda i: (i, 0, 0)),
        out_shape=jax.ShapeDtypeStruct((HY_ORDER * D_HY, FFT_N1, FFT_N2), F32),
        scratch_shapes=[pltpu.VMEM((hy_ff, 2 * length), F32), pltpu.VMEM((cb, 2 * length), F32)],
        compiler_params=pltpu.CompilerParams(
            dimension_semantics=("arbitrary",), vmem_limit_bytes=VMEM_LIMIT_BYTES),
        name="hyena_filter_time",
    )(*ins, wot, wot, deltas, deltas)


def _filter_spec_kernel(k_ref, w1p_ref, t1_ref, t2_ref, f2_ref, o_ref):
    cb = k_ref.shape[0]
    kt = jnp.swapaxes(k_ref[...], 1, 2).reshape(cb // 2, 2, FFT_N2, FFT_N1)
    xp = jnp.concatenate([kt[:, 0], kt[:, 1]], axis=-1)
    spec = _fwd_spectrum(xp, w1p_ref[...], t1_ref[...], t2_ref[...], f2_ref[...])
    for par in range(2):
        o_ref[par, :, :FFT_N1, :] = spec[par][0]
        o_ref[par, :, FFT_N1:, :] = spec[par][1]


def _filter_spectrum(kt, cb=32):
    consts = _dft_constants(real_input=True)[:4]
    nb = D_HY // cb
    full = lambda a: pl.BlockSpec(a.shape, lambda i: (0,) * a.ndim)
    return pl.pallas_call(
        _filter_spec_kernel,
        grid=(HY_ORDER * nb,),
        in_specs=[pl.BlockSpec((cb, FFT_N1, FFT_N2), lambda i: (i, 0, 0))] + [full(a) for a in consts],
        out_specs=pl.BlockSpec((None, 2, cb // 2, 2 * FFT_N1, FFT_N2), lambda i: (i // nb, 0, i % nb, 0, 0)),
        out_shape=jax.ShapeDtypeStruct((HY_ORDER, 2, D_HY // 2, 2 * FFT_N1, FFT_N2), F32),
        compiler_params=pltpu.CompilerParams(
            dimension_semantics=("parallel",), vmem_limit_bytes=VMEM_LIMIT_BYTES),
        name="hyena_filter_spectrum",
    )(kt, *consts)


def _hyena(us, w1, b1, w2, b2, w3, b3, freq, w_out, bias):
    bsz, length, _ = us.shape
    assert 2 * length == FFT_N and bsz == 2, "one complex transform carries exactly two batch rows"
    zp, g1p, g2p = (_to_packed(us[..., i * D_HY:(i + 1) * D_HY], bsz) for i in range(HY_ORDER + 1))
    kf = _filter_spectrum(_filter_time(length, w1, b1, w2, b2, w3, b3, freq, w_out))
    biasp = jnp.repeat(bias.reshape(HY_ORDER, D_HY // 2, 1, 2), FFT_N1, axis=-1)
    return _from_packed(_hyena_conv(zp, g1p, g2p, kf, biasp), bsz)


def kernel(x, c, ctx, c_ctx, w_ada, b_ada, norm_g, ffn_w_gate, ffn_w_up, ffn_w_down, w_in,
           s5_lam_re, s5_lam_im, s5_log_dt, s5_b_re, s5_b_im, s5_c_re, s5_c_im, s5_d,
           hy_short_w, hy_short_b, hy_w1, hy_b1, hy_w2, hy_b2, hy_w3, hy_b3, hy_freq, hy_w_out,
           hy_bias, w_pa, w_pb, w_out, final_g):
    bsz, seq, d = x.shape
    ctx_len = ctx.shape[1]
    n_rows = seq // GRID_W
    depth = w_ada.shape[0]
    assert depth == 1, "context-token outputs are only dropped by the last layer"
    l = 0

    c_rows = jnp.concatenate([c, c_ctx[None, :], jnp.zeros((8 - bsz - 1, d), F32)], axis=0)
    mod_all = _ada_mod(c_rows, w_ada[l], b_ada[l])
    mod = mod_all[:bsz].reshape(bsz, N_SUB, N_MOD, 1, d)
    mod_c = mod_all[bsz:bsz + 1].reshape(1, N_SUB, N_MOD, 1, d)

    def mods(m, sub):
        return tuple(m[:, sub, k] for k in range(N_MOD))

    wg, wu, wd = ffn_w_gate[l], ffn_w_up[l], ffn_w_down[l]
    w_in_b = w_in[l]

    xt = x.reshape(bsz * seq, d)
    ct = ctx.reshape(bsz * ctx_len, d)

    xt = _ffn_sublayer(xt, mods(mod, 0), norm_g[l, 0], wg[0], wu[0], wd[0])
    ct = _ffn_sublayer(ct, mods(mod_c, 0), norm_g[l, 0], wg[0], wu[0], wd[0])

    assert GRID_W * n_rows == seq
    u_s5, us_hy, sig_gates = _in_proj(xt, mod[:, 1, 0], mod[:, 1, 1], norm_g[l, 1], w_in_b,
                                      hy_short_w[l], hy_short_b[l], n_u=I_HY, n_hy=I_GA - I_HY)
    (u_ctx,) = _in_proj(ct, mod_c[:, 1, 0], mod_c[:, 1, 1], norm_g[l, 1], w_in_b[:, :D_S5])

    y_s5 = _s5_bidirectional(u_s5.reshape(bsz, seq, D_S5), u_ctx.reshape(bsz, ctx_len, D_S5),
                             s5_lam_re[l], s5_lam_im[l], s5_log_dt[l],
                             s5_b_re[l], s5_b_im[l], s5_c_re[l], s5_c_im[l], s5_d[l])
    y_hy = _hyena(us_hy.reshape(bsz, seq, I_GA - I_HY),
                  hy_w1[l], hy_b1[l], hy_w2[l], hy_b2[l], hy_w3[l], hy_b3[l], hy_freq[l],
                  hy_w_out[l], hy_bias[l])

    xt = _merge(xt, mod[:, 1, 2], y_s5.reshape(bsz * seq, D_S5), y_hy.reshape(bsz * seq, D_HY),
                sig_gates, w_pa[l].astype(BF16), w_pb[l].astype(BF16), w_out[l].astype(BF16))

    xt = _ffn_sublayer(xt, mods(mod, 2), norm_g[l, 2], wg[1], wu[1], wd[1], final_gain=final_g)
    return xt.reshape(bsz, seq, d)
```

```python
import functools
import math

import jax
import jax.numpy as jnp
import numpy as np
from jax import lax
from jax.experimental import pallas as pl
from jax.experimental.pallas import tpu as pltpu

F32 = jnp.float32
BF16 = jnp.bfloat16

D_MODEL = 2048
GRID_W = 64
D_S5 = 1024
S5_GROUP = 16
S5_GROUPS = D_S5 // S5_GROUP
S5_STATE = 64
S5_DIRS = 2
LAMBDA_RE_MAX = -1e-4
S5_CHUNK = 16
D_HY = 1024
HY_ORDER = 2
HY_DIRS = 2
HY_SHORT = 3
HY_EMB = 33
HY_BANDS = (HY_EMB - 1) // 2
HY_TARGET = 1e-2
HY_FAST_PCT = 0.3
HY_SLOW_PCT = 1.5
FFT_N1 = 64
FFT_N2 = 128
FFT_N = FFT_N1 * FFT_N2
FILTER_TILE_ROWS = FFT_N1 + 8
I_HY = D_S5
I_GA = D_S5 + (HY_ORDER + 1) * D_HY
I_GB = I_GA + D_MODEL
D_IN = I_GB + D_MODEL
D_FF = 5632
N_SUB = 3
N_MOD = 3
HALF_STEP = 0.5
RMS_EPS = 1e-6

VMEM_LIMIT_BYTES = 58 * 1024 * 1024


def _rms_mod(x, gain, shift, scale):
    ms = jnp.mean(x * x, axis=-1, keepdims=True)
    y = x * lax.rsqrt(ms + RMS_EPS) * gain
    return y * (1.0 + scale) + shift


def _ada_kernel(c_ref, w_ref, b_ref, o_ref):
    c = c_ref[...]
    a = c * jax.nn.sigmoid(c)
    o_ref[...] = jnp.dot(a, w_ref[...], preferred_element_type=F32,
                         precision=lax.Precision.HIGHEST) + b_ref[...]


def _ada_mod(c_rows, w, b, tn=1024):
    rows, d = c_rows.shape
    n = w.shape[1]
    return pl.pallas_call(
        _ada_kernel,
        grid=(n // tn,),
        in_specs=[pl.BlockSpec((rows, d), lambda j: (0, 0)),
                  pl.BlockSpec((d, tn), lambda j: (0, j)),
                  pl.BlockSpec((1, tn), lambda j: (0, j))],
        out_specs=pl.BlockSpec((rows, tn), lambda j: (0, j)),
        out_shape=jax.ShapeDtypeStruct((rows, n), F32),
        compiler_params=pltpu.CompilerParams(
            dimension_semantics=("arbitrary",), vmem_limit_bytes=VMEM_LIMIT_BYTES),
        name="ada_mod",
    )(c_rows, w, b.reshape(1, n))


def _ffn_kernel(x_ref, shift_ref, scale_ref, gate_ref, gain_ref, wg_ref, wu_ref, wd_ref,
                fg_ref, o_ref, h_ref, *, final_norm):
    j = pl.program_id(1)

    @pl.when(j == 0)
    def _():
        h_ref[...] = _rms_mod(x_ref[...], gain_ref[...], shift_ref[...], scale_ref[...]).astype(BF16)
        o_ref[...] = jnp.zeros_like(o_ref)

    h = h_ref[...]
    g = jnp.dot(h, wg_ref[...].astype(BF16), preferred_element_type=F32)
    u = jnp.dot(h, wu_ref[...].astype(BF16), preferred_element_type=F32)
    a = (g * jax.nn.sigmoid(g) * u).astype(BF16)
    o_ref[...] += jnp.dot(a, wd_ref[...].astype(BF16), preferred_element_type=F32)

    @pl.when(j == pl.num_programs(1) - 1)
    def _():
        y = x_ref[...] + (HALF_STEP * gate_ref[...]) * o_ref[...]
        if final_norm:
            ms = jnp.mean(y * y, axis=-1, keepdims=True)
            y = y * lax.rsqrt(ms + RMS_EPS) * fg_ref[...]
        o_ref[...] = y


def _ffn_sublayer(x, mods, gain, wg, wu, wd, which, final_gain=None, tm=1024, tf=256):
    t, d = x.shape
    bm = mods[0].shape[0]
    tm = min(tm, t // bm)
    blocks_per_batch = (t // bm) // tm
    dff = wg.shape[2]
    final_norm = final_gain is not None
    fg = final_gain if final_norm else gain
    mod_spec = pl.BlockSpec((None, 1, d), lambda i, j: (i // blocks_per_batch, 0, 0))
    vec_spec = pl.BlockSpec((1, d), lambda i, j: (0, 0))
    return pl.pallas_call(
        functools.partial(_ffn_kernel, final_norm=final_norm),
        grid=(t // tm, dff // tf),
        in_specs=[pl.BlockSpec((tm, d), lambda i, j: (i, 0)),
                  mod_spec, mod_spec, mod_spec, vec_spec,
                  pl.BlockSpec((None, d, tf), lambda i, j: (which, 0, j)),
                  pl.BlockSpec((None, d, tf), lambda i, j: (which, 0, j)),
                  pl.BlockSpec((None, tf, d), lambda i, j: (which, j, 0)),
                  vec_spec],
        out_specs=pl.BlockSpec((tm, d), lambda i, j: (i, 0)),
        out_shape=jax.ShapeDtypeStruct((t, d), F32),
        scratch_shapes=[pltpu.VMEM((tm, d), BF16)],
        compiler_params=pltpu.CompilerParams(
            dimension_semantics=("parallel", "arbitrary"), vmem_limit_bytes=VMEM_LIMIT_BYTES),
        name="ffn_final" if final_norm else "ffn",
    )(x, *mods, gain.reshape(1, d), wg, wu, wd, fg.reshape(1, d))


def _proj_kernel(x_ref, shift_ref, scale_ref, gain_ref, w_ref, sw_ref, sb_ref, *rest, n_u, n_hy, row_len):
    o_refs, h_ref = rest[:-1], rest[-1]
    j = pl.program_id(1)

    @pl.when(j == 0)
    def _():
        h_ref[...] = _rms_mod(x_ref[...], gain_ref[...], shift_ref[...], scale_ref[...]).astype(BF16)

    p = jnp.dot(h_ref[...], w_ref[...].astype(BF16), preferred_element_type=F32)
    if n_hy == 0:
        o_refs[0][...] = p.astype(o_refs[0].dtype)
        return

    @pl.when(j < n_u)
    def _():
        o_refs[0][...] = p.astype(o_refs[0].dtype)

    @pl.when((j >= n_u) & (j < n_u + n_hy))
    def _():
        tm = p.shape[0]
        col = lax.broadcasted_iota(jnp.int32, p.shape, 0) % row_len
        prev = jnp.where(col == 0, 0.0, pltpu.roll(p, 1, 0))
        nxt = jnp.where(col == row_len - 1, 0.0, pltpu.roll(p, tm - 1, 0))
        sw = sw_ref[...]
        us = sb_ref[...] + prev * sw[0:1] + p * sw[1:2] + nxt * sw[2:3]
        o_refs[1][...] = us.astype(o_refs[1].dtype)

    @pl.when(j >= n_u + n_hy)
    def _():
        o_refs[2][...] = jax.nn.sigmoid(p).astype(BF16)


def _in_proj(x, shift, scale, gain, w, short_w=None, short_b=None, n_u=D_S5, n_hy=0, row_len=GRID_W,
             tm=1024, tn=512):
    t, d = x.shape
    bm = shift.shape[0]
    tm = min(tm, t // bm)
    blocks_per_batch = (t // bm) // tm
    n = w.shape[1]
    assert tm % row_len == 0 and (t // bm) % tm == 0
    bu, bh = n_u // tn, n_hy // tn
    bg = n // tn - bu - bh
    mod_spec = pl.BlockSpec((None, 1, d), lambda i, j: (i // blocks_per_batch, 0, 0))
    out_shape = [jax.ShapeDtypeStruct((t, n_u), BF16)]
    out_specs = [pl.BlockSpec((tm, tn), lambda i, j: (i, jnp.minimum(j, bu - 1)))]
    if bh:
        out_shape += [jax.ShapeDtypeStruct((t, n_hy), BF16), jax.ShapeDtypeStruct((t, bg * tn), BF16)]
        out_specs += [pl.BlockSpec((tm, tn), lambda i, j: (i, jnp.clip(j - bu, 0, bh - 1))),
                      pl.BlockSpec((tm, tn), lambda i, j: (i, jnp.maximum(j - bu - bh, 0)))]
        sw, sb = short_w, short_b.reshape(1, n_hy)
        hy_blk = lambda i, j: (0, jnp.clip(j - bu, 0, bh - 1))
    else:
        sw, sb = jnp.zeros((HY_SHORT, tn), F32), jnp.zeros((1, tn), F32)
        hy_blk = lambda i, j: (0, 0)
    return pl.pallas_call(
        functools.partial(_proj_kernel, n_u=bu, n_hy=bh, row_len=row_len),
        grid=(t // tm, n // tn),
        in_specs=[pl.BlockSpec((tm, d), lambda i, j: (i, 0)),
                  mod_spec, mod_spec,
                  pl.BlockSpec((1, d), lambda i, j: (0, 0)),
                  pl.BlockSpec((d, tn), lambda i, j: (0, j)),
                  pl.BlockSpec((HY_SHORT, tn), hy_blk),
                  pl.BlockSpec((1, tn), hy_blk)],
        out_specs=out_specs,
        out_shape=out_shape,
        scratch_shapes=[pltpu.VMEM((tm, d), BF16)],
        compiler_params=pltpu.CompilerParams(
            dimension_semantics=("parallel", "arbitrary"), vmem_limit_bytes=VMEM_LIMIT_BYTES),
        name="in_proj",
    )(x, shift, scale, gain.reshape(1, d), w, sw, sb)


def _gelu_tanh(x):
    return 0.5 * x * (1.0 + jnp.tanh(math.sqrt(2.0 / math.pi) * (x + 0.044715 * (x * x * x))))


def _merge_kernel(x_ref, gate_ref, ys_ref, yh_ref, ga_ref, gb_ref, wpa_lo_ref, wpa_hi_ref, wpb_ref,
                  wout_ref, o_ref, s_ref, acc_ref):
    j = pl.program_id(1)

    @pl.when(j == 0)
    def _():
        s_ref[...] = _gelu_tanh(ys_ref[...].astype(F32)).astype(BF16)
        acc_ref[...] = jnp.zeros_like(acc_ref)

    s = s_ref[...]
    pa_lo = jnp.dot(s, wpa_lo_ref[...], preferred_element_type=F32)
    pa_hi = jnp.dot(s, wpa_hi_ref[...], preferred_element_type=F32)
    y_a = pa_lo * jax.nn.sigmoid(pa_hi)
    y_b = jnp.dot(yh_ref[...], wpb_ref[...], preferred_element_type=F32)
    m = ga_ref[...].astype(F32) * y_a + gb_ref[...].astype(F32) * y_b
    acc_ref[...] += jnp.dot(m.astype(BF16), wout_ref[...], preferred_element_type=F32)

    @pl.when(j == pl.num_programs(1) - 1)
    def _():
        o_ref[...] = x_ref[...] + gate_ref[...] * acc_ref[...]


def _merge(x, gate, y_s5, y_hy, sig_gates, w_pa, w_pb, w_out, tm=512, tn=512):
    t, d = x.shape
    bm = gate.shape[0]
    blocks_per_batch = (t // bm) // tm
    nj = d // tn
    ds5 = y_s5.shape[1]
    dhy = y_hy.shape[1]
    return pl.pallas_call(
        _merge_kernel,
        grid=(t // tm, nj),
        in_specs=[pl.BlockSpec((tm, d), lambda i, j: (i, 0)),
                  pl.BlockSpec((None, 1, d), lambda i, j: (i // blocks_per_batch, 0, 0)),
                  pl.BlockSpec((tm, ds5), lambda i, j: (i, 0)),
                  pl.BlockSpec((tm, dhy), lambda i, j: (i, 0)),
                  pl.BlockSpec((tm, tn), lambda i, j: (i, j)),
                  pl.BlockSpec((tm, tn), lambda i, j: (i, nj + j)),
                  pl.BlockSpec((ds5, tn), lambda i, j: (0, j)),
                  pl.BlockSpec((ds5, tn), lambda i, j: (0, nj + j)),
                  pl.BlockSpec((dhy, tn), lambda i, j: (0, j)),
                  pl.BlockSpec((tn, d), lambda i, j: (j, 0))],
        out_specs=pl.BlockSpec((tm, d), lambda i, j: (i, 0)),
        out_shape=jax.ShapeDtypeStruct((t, d), F32),
        scratch_shapes=[pltpu.VMEM((tm, ds5), BF16), pltpu.VMEM((tm, d), F32)],
        compiler_params=pltpu.CompilerParams(
            dimension_semantics=("parallel", "arbitrary"), vmem_limit_bytes=VMEM_LIMIT_BYTES),
        name="merge",
    )(x, gate, y_s5, y_hy, sig_gates, sig_gates, w_pa, w_pa, w_pb, w_out)


def _s5_weights_kernel(par_ref, bt_ref, c_ref, d_ref, bpow_ref, toep_ref, cpow_ref, tab_ref, ca_ref, *, n_steps):
    t, h, p = S5_CHUNK, S5_GROUP, S5_STATE
    lanes = 2 * p
    hp = lax.Precision.HIGHEST
    sgn = jnp.where(lax.broadcasted_iota(jnp.int32, (1, lanes), 1) < p, -1.0, 1.0)
    par = par_ref[...]
    gsum = None
    for d in range(S5_DIRS):
        lr = jnp.minimum(par[3 * d:3 * d + 1], LAMBDA_RE_MAX)
        li = par[3 * d + 1:3 * d + 2]
        dt = jnp.exp(par[3 * d + 2:3 * d + 3])
        zr, zi = lr * dt, li * dt

        def apow(j):
            mag = jnp.exp(j * zr)
            return mag * jnp.cos(j * zi), sgn * (mag * jnp.sin(j * zi))

        def cmul(x, a1, a2):
            return x * a1 + pltpu.roll(x, p, 1) * a2

        a1, a2 = apow(lax.broadcasted_iota(jnp.int32, (t + 1, 1), 0).astype(F32))
        nr, ni = a1[1:2] - 1.0, sgn * a2[1:2]
        den = lr * lr + li * li
        f_re = (nr * lr + ni * li) / den
        f_im = (ni * lr - nr * li) / den
        bbar = cmul(bt_ref[d], f_re, sgn * f_im)
        cc = c_ref[d]
        ca = [cmul(cc, a1[j:j + 1], a2[j:j + 1]) * (-sgn) for j in range(t + 1)]
        ca_ref[...] = jnp.zeros_like(ca_ref)
        for k in range(t):
            e_b, e_c = (t - 1 - k, k + 1) if d == 0 else (k, t - k)
            bpow_ref[k * h:(k + 1) * h, d * lanes:(d + 1) * lanes] = (
                cmul(bbar, a1[e_b:e_b + 1], a2[e_b:e_b + 1]).astype(BF16))
            cpow_ref[k * h:(k + 1) * h, d * lanes:(d + 1) * lanes] = ca[e_c].astype(BF16)
            l = t - 1 + k if d == 0 else t - 1 - k
            ca_ref[l * h:(l + 1) * h, :] = ca[k]
        g = lax.dot_general(bbar, ca_ref[...], (((1,), (1,)), ((), ())), preferred_element_type=F32, precision=hp)
        gsum = g if gsum is None else gsum + g
        for s in range(n_steps):
            s1, s2 = apow(float(t * 2 ** s))
            r = d * 2 * n_steps + 2 * s
            tab_ref[r:r + 1, :] = s1
            tab_ref[r + 1:r + 2, :] = s2
    wide = gsum.shape[1]
    col = lax.broadcasted_iota(jnp.int32, (h, wide), 1)
    row = lax.broadcasted_iota(jnp.int32, (h, wide), 0)
    gsum = gsum + jnp.where(col - (t - 1) * h == row, d_ref[...], 0.0)
    for k in range(t):
        off = (t - 1 - k) * h
        shifted = gsum if off == 0 else pltpu.roll(gsum, wide - off, 1)
        toep_ref[k * h:(k + 1) * h, :] = shifted[:, :t * h].astype(BF16)


def _s5_weights(lam_re, lam_im, log_dt, b_re, b_im, c_re, c_im, d_skip, n_steps):
    g, p, h, t = S5_GROUPS, S5_STATE, S5_GROUP, S5_CHUNK
    cat2 = lambda a: jnp.concatenate([a, a], axis=-1)
    par = jnp.stack([cat2(lam_re), cat2(lam_im), jnp.broadcast_to(log_dt[..., None], (S5_DIRS, g, 2 * p))], axis=1)
    par = par.transpose(2, 0, 1, 3).reshape(g, 3 * S5_DIRS, 2 * p)
    btc = jnp.concatenate([b_re, b_im], axis=2).transpose(1, 0, 3, 2)
    ccat = jnp.concatenate([c_re, c_im], axis=3).transpose(1, 0, 2, 3)
    wide = 2 * t * h
    drow = jnp.zeros((g, 1, wide), F32).at[:, 0, (t - 1) * h:t * h].set(d_skip.reshape(g, h))
    w = t * h
    sq = pl.BlockSpec((None, w, w), lambda i: (i, 0, 0))
    return pl.pallas_call(
        functools.partial(_s5_weights_kernel, n_steps=n_steps),
        grid=(g,),
        in_specs=[pl.BlockSpec((None, 3 * S5_DIRS, 2 * p), lambda i: (i, 0, 0)),
                  pl.BlockSpec((None, S5_DIRS, h, 2 * p), lambda i: (i, 0, 0, 0)),
                  pl.BlockSpec((None, S5_DIRS, h, 2 * p), lambda i: (i, 0, 0, 0)),
                  pl.BlockSpec((None, 1, wide), lambda i: (i, 0, 0))],
        out_specs=[sq, sq, sq, pl.BlockSpec((None, 4 * n_steps, 2 * p), lambda i: (i, 0, 0))],
        out_shape=[jax.ShapeDtypeStruct((g, w, w), BF16)] * 3 + [jax.ShapeDtypeStruct((g, 4 * n_steps, 2 * p), F32)],
        scratch_shapes=[pltpu.VMEM((wide, 2 * p), F32)],
        compiler_params=pltpu.CompilerParams(
            dimension_semantics=("parallel",), vmem_limit_bytes=VMEM_LIMIT_BYTES),
        name="s5_weights",
    )(par, btc, ccat, drow)


def _s5_kernel(ul_ref, uc_ref, bpow_ref, toep_ref, cpow_ref, tab_ref, y_ref, *, bsz, n_steps):
    n_lat = ul_ref.shape[0] // bsz
    n_ctx = uc_ref.shape[0] // bsz
    n_ch = n_lat + n_ctx
    rows = bsz * n_ch
    half = 2 * S5_STATE
    ul = ul_ref[...]
    uc = uc_ref[...]
    bpow = bpow_ref[...]
    zl = jnp.dot(ul, bpow, preferred_element_type=F32)
    zc = jnp.dot(uc, bpow, preferred_element_type=F32)
    fparts, bparts = [], []
    for b in range(bsz):
        lat = slice(b * n_lat, (b + 1) * n_lat)
        ctx = slice(b * n_ctx, (b + 1) * n_ctx)
        fparts += [zc[ctx, :half], zl[lat, :half]]
        bparts += [zl[lat, half:], zc[ctx, half:]]
    fw = jnp.concatenate(fparts, axis=0)
    bw = jnp.concatenate(bparts, axis=0)
    rib = lax.broadcasted_iota(jnp.int32, (rows, half), 0) % n_ch
    tab = tab_ref[...]

    def cmul_add(acc, sh, a1, a2):
        return acc + a1 * sh + a2 * pltpu.roll(sh, S5_STATE, 1)

    for s in range(n_steps):
        d = 1 << s
        sh = jnp.where(rib >= d, pltpu.roll(fw, d, 0), 0.0)
        fw = cmul_add(fw, sh, tab[2 * s:2 * s + 1], tab[2 * s + 1:2 * s + 2])
        o = 2 * n_steps
        sh = jnp.where(rib < n_ch - d, pltpu.roll(bw, rows - d, 0), 0.0)
        bw = cmul_add(bw, sh, tab[o + 2 * s:o + 2 * s + 1], tab[o + 2 * s + 1:o + 2 * s + 2])
    fe = jnp.where(rib >= 1, pltpu.roll(fw, 1, 0), 0.0)
    be = jnp.where(rib < n_ch - 1, pltpu.roll(bw, rows - 1, 0), 0.0)
    fl = jnp.concatenate([fe[b * n_ch + n_ctx:(b + 1) * n_ch] for b in range(bsz)], axis=0)
    bl = jnp.concatenate([be[b * n_ch:b * n_ch + n_lat] for b in range(bsz)], axis=0)
    st = jnp.concatenate([fl, bl], axis=1).astype(BF16)
    y = (jnp.dot(ul, toep_ref[...], preferred_element_type=F32)
         + lax.dot_general(st, cpow_ref[...], (((1,), (1,)), ((), ())), preferred_element_type=F32))
    y_ref[...] = y.astype(y_ref.dtype)


def _s5_mix(u2, u2c, bpw, toep, cpw, tab, bsz):
    g, rl, w = u2.shape
    rc = u2c.shape[1]
    n_steps = tab.shape[1] // 4
    wspec = pl.BlockSpec((None, w, w), lambda i: (i, 0, 0))
    return pl.pallas_call(
        functools.partial(_s5_kernel, bsz=bsz, n_steps=n_steps),
        grid=(g,),
        in_specs=[pl.BlockSpec((None, rl, w), lambda i: (i, 0, 0)),
                  pl.BlockSpec((None, rc, w), lambda i: (i, 0, 0)),
                  wspec, wspec, wspec,
                  pl.BlockSpec((None, 4 * n_steps, tab.shape[2]), lambda i: (i, 0, 0))],
        out_specs=pl.BlockSpec((None, rl, w), lambda i: (i, 0, 0)),
        out_shape=jax.ShapeDtypeStruct((g, rl, w), BF16),
        compiler_params=pltpu.CompilerParams(
            dimension_semantics=("parallel",), vmem_limit_bytes=VMEM_LIMIT_BYTES),
        name="s5_mix",
    )(u2, u2c, bpw, toep, cpw, tab)


def _s5_bidirectional(u, u_ctx, lam_re, lam_im, log_dt, b_re, b_im, c_re, c_im, d_skip):
    bsz, length, dm = u.shape
    ctx_len = u_ctx.shape[1]
    t, g, h = S5_CHUNK, S5_GROUPS, S5_GROUP
    n_lat, n_ctx = length // t, ctx_len // t
    n_steps = max(1, math.ceil(math.log2(n_lat + n_ctx)))

    def fold(a, n):
        return a.reshape(bsz, n, t, g, h).transpose(3, 0, 1, 2, 4).reshape(g, bsz * n, t * h)

    bpw, toep, cpw, tab = _s5_weights(lam_re, lam_im, log_dt, b_re, b_im, c_re, c_im, d_skip, n_steps)
    y2 = _s5_mix(fold(u, n_lat), fold(u_ctx, n_ctx), bpw, toep, cpw, tab, bsz)
    return y2.reshape(g, bsz, n_lat, t, h).transpose(1, 2, 3, 0, 4).reshape(bsz, length, dm)


def _dft_constants(real_input=False):
    n1 = np.arange(FFT_N1)
    n2 = np.arange(FFT_N2)
    half = FFT_N1 // 2
    th = 2 * np.pi * np.outer(n1, n1) / FFT_N1
    c1, s1 = np.cos(th), np.sin(th)
    if real_input:
        w1 = np.concatenate([c1, -s1], axis=1)
    else:
        w1 = np.concatenate([np.concatenate([c1[:half], -s1[:half]], axis=1),
                             np.concatenate([s1[:half], c1[:half]], axis=1)], axis=0)
    z = np.zeros_like(w1)
    w1p = np.block([[w1, z], [z, w1]])
    ph = 2 * np.pi * np.outer(n2, n1) / FFT_N
    t1 = np.concatenate([np.cos(ph), np.cos(ph)], axis=1)
    t2 = np.concatenate([np.sin(ph), -np.sin(ph)], axis=1)
    ps = 2 * np.pi * np.outer(n2, n2) / FFT_N2
    f2 = np.concatenate([np.cos(ps), -np.sin(ps)], axis=1)
    g2 = np.concatenate([np.cos(ps), np.sin(ps)], axis=1)
    c2, s2 = np.cos(ph).T, np.sin(ph).T
    wi = np.concatenate([np.concatenate([c1[:, :half], s1[:, :half]], axis=1),
                         np.concatenate([-s1[:, :half], c1[:, :half]], axis=1)], axis=0) / FFT_N
    zi = np.zeros_like(wi)
    wi2 = np.stack([np.concatenate([wi, zi], axis=1), np.concatenate([zi, wi], axis=1)])
    as_b = lambda a: jnp.asarray(a, F32).astype(BF16)
    as_f = lambda a: jnp.asarray(a, F32)
    return [as_b(w1p), as_f(t1), as_f(t2), as_b(f2), as_b(g2), as_f(c2), as_f(s2), as_b(wi2)]


def _fwd_spectrum(xp, w1p, t1, t2, f2):
    cp = xp.shape[0]
    hn = FFT_N1
    a = jnp.dot(xp.reshape(cp * FFT_N2, 128).astype(BF16), w1p, preferred_element_type=F32)
    out = []
    for par in range(2):
        ap = a[:, par * 128:(par + 1) * 128]
        ap = ap.reshape(cp, FFT_N2, 128) * t1 + pltpu.roll(ap, hn, 1).reshape(cp, FFT_N2, 128) * t2
        at = jnp.swapaxes(ap, 1, 2)
        p = jnp.dot(at.reshape(cp * 128, FFT_N2).astype(BF16), f2, preferred_element_type=F32)
        p = p.reshape(cp, 128, 2 * FFT_N2)
        out.append((p[:, :hn, :FFT_N2] - p[:, hn:, FFT_N2:], p[:, :hn, FFT_N2:] + p[:, hn:, :FFT_N2]))
    return out


def _inv_time(yre, yim, g2, c2, s2, wi_par):
    cp = yre.shape[0]
    hn = FFT_N1
    y = jnp.concatenate([yre, yim], axis=1).reshape(cp * 128, FFT_N2).astype(BF16)
    q = jnp.dot(y, g2, preferred_element_type=F32).reshape(cp, 128, 2 * FFT_N2)
    bre = q[:, :hn, :FFT_N2] - q[:, hn:, FFT_N2:]
    bim = q[:, :hn, FFT_N2:] + q[:, hn:, :FFT_N2]
    b2 = jnp.concatenate([bre * c2 - bim * s2, bre * s2 + bim * c2], axis=1)
    bt = jnp.swapaxes(b2, 1, 2)
    return jnp.dot(bt.reshape(cp * FFT_N2, 128).astype(BF16), wi_par, preferred_element_type=F32)


def _hyena_kernel(z_ref, g1_ref, g2_ref, kf_ref, bias_ref, w1p_ref, t1_ref, t2_ref, f2_ref, gi_ref, c2_ref,
                  s2_ref, wi_ref, o_ref):
    cp = z_ref.shape[0]
    z = z_ref[...].astype(F32)
    gates = (g1_ref, g2_ref)
    for o in range(HY_ORDER):
        spec = _fwd_spectrum(z, w1p_ref[...], t1_ref[...], t2_ref[...], f2_ref[...])
        conv = None
        for par in range(2):
            xre, xim = spec[par]
            kre = kf_ref[o, par, :, :FFT_N1, :]
            kim = kf_ref[o, par, :, FFT_N1:, :]
            part = _inv_time(xre * kre - xim * kim, xre * kim + xim * kre,
                             gi_ref[...], c2_ref[...], s2_ref[...], wi_ref[par])
            conv = part if conv is None else conv + part
        z = gates[o][...].astype(F32) * (conv.reshape(cp, FFT_N2, 128) + bias_ref[o] * z)
    o_ref[...] = z.astype(o_ref.dtype)


def _to_packed(v, bsz):
    _, length, ch = v.shape
    nn1 = length // FFT_N2
    return v.reshape(bsz, nn1, FFT_N2, ch // 2, 2).transpose(3, 2, 4, 0, 1).reshape(ch // 2, FFT_N2, 2 * bsz * nn1)


def _from_packed(vp, bsz):
    chp, _, lanes = vp.shape
    nn1 = lanes // (2 * bsz)
    return vp.reshape(chp, FFT_N2, 2, bsz, nn1).transpose(3, 4, 1, 0, 2).reshape(bsz, nn1 * FFT_N2, chp * 2)


def _hyena_conv(zp, g1p, g2p, kf, biasp, cp=8):
    chp = zp.shape[0]
    consts = _dft_constants()
    dspec = pl.BlockSpec((cp, FFT_N2, 128), lambda i: (i, 0, 0))
    full = lambda a: pl.BlockSpec(a.shape, lambda i: (0,) * a.ndim)
    return pl.pallas_call(
        _hyena_kernel,
        grid=(chp // cp,),
        in_specs=[dspec, dspec, dspec,
                  pl.BlockSpec((HY_ORDER, 2, cp, 128, FFT_N2), lambda i: (0, 0, i, 0, 0)),
                  pl.BlockSpec((HY_ORDER, cp, 1, 128), lambda i: (0, i, 0, 0))] + [full(a) for a in consts],
        out_specs=dspec,
        out_shape=jax.ShapeDtypeStruct(zp.shape, zp.dtype),
        compiler_params=pltpu.CompilerParams(
            dimension_semantics=("parallel",), vmem_limit_bytes=VMEM_LIMIT_BYTES),
        name="hyena_conv",
    )(zp, g1p, g2p, kf, biasp, *consts)


def _filter_time_kernel(w1t_ref, w1c_ref, w1s_ref, b1_ref, w2_ref, b2_ref, w3_ref, b3_ref, fr_ref,
                        wf_ref, wb_ref, df_ref, db_ref, o_ref, h_ref, k_ref, *, length):
    n_fft = 2 * length
    hp = lax.Precision.HIGHEST

    @pl.when(pl.program_id(0) == 0)
    def _():
        pos = lax.broadcasted_iota(jnp.int32, (1, n_fft), 1)
        lag = jnp.where(pos < length, pos, n_fft - pos).astype(F32)
        t = lag / float(length - 1)
        w = (2.0 * math.pi / length) * lag
        band_step = (HY_BANDS - 1 - 1e-4) / (HY_BANDS - 1)
        bands = 1e-4 + band_step * lax.broadcasted_iota(jnp.int32, (HY_BANDS, 1), 0).astype(F32)
        ang = bands * w
        fr = fr_ref[...]
        h = (w1t_ref[...] * t + jnp.dot(w1c_ref[...], jnp.cos(ang), preferred_element_type=F32, precision=hp)
             - jnp.dot(w1s_ref[...], jnp.sin(ang), preferred_element_type=F32, precision=hp))
        h = jnp.sin(fr * (h + b1_ref[...]))
        h = jnp.sin(fr * (jnp.dot(w2_ref[...], h, preferred_element_type=F32, precision=hp) + b2_ref[...]))
        h = jnp.sin(fr * (jnp.dot(w3_ref[...], h, preferred_element_type=F32, precision=hp) + b3_ref[...]))
        hi = h.astype(BF16)
        h_ref[0] = hi
        h_ref[1] = (h - hi.astype(F32)).astype(BF16)

    def dot3(w, lo, hi_):
        w_hi = w.astype(BF16)
        w_lo = (w - w_hi.astype(F32)).astype(BF16)
        h_hi, h_lo = h_ref[0, :, lo:hi_], h_ref[1, :, lo:hi_]
        return (jnp.dot(w_hi, h_hi, preferred_element_type=F32) + jnp.dot(w_hi, h_lo, preferred_element_type=F32)
                + jnp.dot(w_lo, h_hi, preferred_element_type=F32))

    pos = lax.broadcasted_iota(jnp.int32, (1, length), 1)
    tf = pos.astype(F32) / float(length - 1)
    tb = (length - pos).astype(F32) / float(length - 1)
    kf = dot3(wf_ref[...], 0, length) * jnp.exp(-tf * df_ref[...])
    kb = dot3(wb_ref[...], length, n_fft) * jnp.exp(-tb * db_ref[...])
    kb = jnp.where(pos == 0, 0.0, kb)
    inv = 1.0 / (jnp.sum(jnp.abs(kf), axis=1, keepdims=True) + jnp.sum(jnp.abs(kb), axis=1, keepdims=True))
    k_ref[:, :length] = kf * inv
    k_ref[:, length:] = kb * inv
    for n1 in range(FFT_N1):
        o_ref[:, n1, :] = k_ref[:, n1 * FFT_N2:(n1 + 1) * FFT_N2]
    o_ref[:, FFT_N1:, :] = jnp.zeros((o_ref.shape[0], FILTER_TILE_ROWS - FFT_N1, FFT_N2), F32)


def _filter_time(length, w1, b1, w2, b2, w3, b3, freq, w_out, cb=128):
    col = lambda v: v.reshape(-1, 1)
    w1t = w1.T
    n_ch = w_out.shape[1]
    deltas = jnp.abs(jnp.linspace(math.log(HY_TARGET) / HY_SLOW_PCT, math.log(HY_TARGET) / HY_FAST_PCT,
                                  n_ch, dtype=F32)).reshape(n_ch, 1)
    wot = w_out.T
    nb = D_HY // cb
    small = lambda a: pl.BlockSpec(a.shape, lambda i: (0,) * a.ndim)
    fwd = lambda i: ((i // nb) * HY_DIRS * nb + i % nb, 0)
    bwd = lambda i: ((i // nb) * HY_DIRS * nb + nb + i % nb, 0)
    ins = [w1t[:, 0:1], w1t[:, 1:1 + HY_BANDS], w1t[:, 1 + HY_BANDS:], col(b1), w2.T, col(b2), w3.T, col(b3),
           col(freq)]
    hy_ff = w2.shape[0]
    return pl.pallas_call(
        functools.partial(_filter_time_kernel, length=length),
        grid=(HY_ORDER * nb,),
        in_specs=[small(a) for a in ins] + [pl.BlockSpec((cb, hy_ff), fwd), pl.BlockSpec((cb, hy_ff), bwd),
                                            pl.BlockSpec((cb, 1), fwd), pl.BlockSpec((cb, 1), bwd)],
        out_specs=pl.BlockSpec((cb, FILTER_TILE_ROWS, FFT_N2), lambda i: (i, 0, 0)),
        out_shape=jax.ShapeDtypeStruct((HY_ORDER * D_HY, FILTER_TILE_ROWS, FFT_N2), F32),
        scratch_shapes=[pltpu.VMEM((2, hy_ff, 2 * length), BF16), pltpu.VMEM((cb, 2 * length), F32)],
        compiler_params=pltpu.CompilerParams(
            dimension_semantics=("arbitrary",), vmem_limit_bytes=VMEM_LIMIT_BYTES),
        name="hyena_filter_time",
    )(*ins, wot, wot, deltas, deltas)


def _filter_spec_kernel(k_ref, w1p_ref, t1_ref, t2_ref, f2_ref, o_ref):
    cb = k_ref.shape[0]
    kt = jnp.swapaxes(k_ref[:, :FFT_N1, :], 1, 2).reshape(cb // 2, 2, FFT_N2, FFT_N1)
    xp = jnp.concatenate([kt[:, 0], kt[:, 1]], axis=-1)
    spec = _fwd_spectrum(xp, w1p_ref[...], t1_ref[...], t2_ref[...], f2_ref[...])
    for par in range(2):
        o_ref[par, :, :FFT_N1, :] = spec[par][0]
        o_ref[par, :, FFT_N1:, :] = spec[par][1]


def _filter_spectrum(kt, cb=32):
    consts = _dft_constants(real_input=True)[:4]
    nb = D_HY // cb
    full = lambda a: pl.BlockSpec(a.shape, lambda i: (0,) * a.ndim)
    return pl.pallas_call(
        _filter_spec_kernel,
        grid=(HY_ORDER * nb,),
        in_specs=[pl.BlockSpec((cb, FILTER_TILE_ROWS, FFT_N2), lambda i: (i, 0, 0))] + [full(a) for a in consts],
        out_specs=pl.BlockSpec((None, 2, cb // 2, 2 * FFT_N1, FFT_N2), lambda i: (i // nb, 0, i % nb, 0, 0)),
        out_shape=jax.ShapeDtypeStruct((HY_ORDER, 2, D_HY // 2, 2 * FFT_N1, FFT_N2), F32),
        compiler_params=pltpu.CompilerParams(
            dimension_semantics=("parallel",), vmem_limit_bytes=VMEM_LIMIT_BYTES),
        name="hyena_filter_spectrum",
    )(kt, *consts)


def _hyena(us, w1, b1, w2, b2, w3, b3, freq, w_out, bias):
    bsz, length, _ = us.shape
    assert 2 * length == FFT_N and bsz == 2, "one complex transform carries exactly two batch rows"
    zp, g1p, g2p = (_to_packed(us[..., i * D_HY:(i + 1) * D_HY], bsz) for i in range(HY_ORDER + 1))
    kf = _filter_spectrum(_filter_time(length, w1, b1, w2, b2, w3, b3, freq, w_out))
    biasp = jnp.repeat(bias.reshape(HY_ORDER, D_HY // 2, 1, 2), FFT_N1, axis=-1)
    return _from_packed(_hyena_conv(zp, g1p, g2p, kf, biasp), bsz)


def kernel(x, c, ctx, c_ctx, w_ada, b_ada, norm_g, ffn_w_gate, ffn_w_up, ffn_w_down, w_in,
           s5_lam_re, s5_lam_im, s5_log_dt, s5_b_re, s5_b_im, s5_c_re, s5_c_im, s5_d,
           hy_short_w, hy_short_b, hy_w1, hy_b1, hy_w2, hy_b2, hy_w3, hy_b3, hy_freq, hy_w_out,
           hy_bias, w_pa, w_pb, w_out, final_g):
    bsz, seq, d = x.shape
    ctx_len = ctx.shape[1]
    n_rows = seq // GRID_W
    depth = w_ada.shape[0]
    assert depth == 1, "context-token outputs are only dropped by the last layer"
    l = 0

    c_rows = jnp.concatenate([c, c_ctx[None, :], jnp.zeros((8 - bsz - 1, d), F32)], axis=0)
    mod_all = _ada_mod(c_rows, w_ada[l], b_ada[l])
    mod = mod_all[:bsz].reshape(bsz, N_SUB, N_MOD, 1, d)
    mod_c = mod_all[bsz:bsz + 1].reshape(1, N_SUB, N_MOD, 1, d)

    def mods(m, sub):
        return tuple(m[:, sub, k] for k in range(N_MOD))

    wg, wu, wd = ffn_w_gate[l], ffn_w_up[l], ffn_w_down[l]
    w_in_b = w_in[l]

    xt = x.reshape(bsz * seq, d)
    ct = ctx.reshape(bsz * ctx_len, d)

    xt = _ffn_sublayer(xt, mods(mod, 0), norm_g[l, 0], wg, wu, wd, 0)
    ct = _ffn_sublayer(ct, mods(mod_c, 0), norm_g[l, 0], wg, wu, wd, 0)

    assert GRID_W * n_rows == seq
    u_s5, us_hy, sig_gates = _in_proj(xt, mod[:, 1, 0], mod[:, 1, 1], norm_g[l, 1], w_in_b,
                                      hy_short_w[l], hy_short_b[l], n_u=I_HY, n_hy=I_GA - I_HY)
    (u_ctx,) = _in_proj(ct, mod_c[:, 1, 0], mod_c[:, 1, 1], norm_g[l, 1], w_in_b[:, :D_S5])

    y_s5 = _s5_bidirectional(u_s5.reshape(bsz, seq, D_S5), u_ctx.reshape(bsz, ctx_len, D_S5),
                             s5_lam_re[l], s5_lam_im[l], s5_log_dt[l],
                             s5_b_re[l], s5_b_im[l], s5_c_re[l], s5_c_im[l], s5_d[l])
    y_hy = _hyena(us_hy.reshape(bsz, seq, I_GA - I_HY),
                  hy_w1[l], hy_b1[l], hy_w2[l], hy_b2[l], hy_w3[l], hy_b3[l], hy_freq[l],
                  hy_w_out[l], hy_bias[l])

    xt = _merge(xt, mod[:, 1, 2], y_s5.reshape(bsz * seq, D_S5), y_hy.reshape(bsz * seq, D_HY),
                sig_gates, w_pa[l].astype(BF16), w_pb[l].astype(BF16), w_out[l].astype(BF16))

    xt = _ffn_sublayer(xt, mods(mod, 2), norm_g[l, 2], wg, wu, wd, 1, final_gain=final_g)
    return xt.reshape(bsz, seq, d)
```

```python
import functools
import math

import jax
import jax.numpy as jnp
import numpy as np
from jax import lax
from jax.experimental import pallas as pl
from jax.experimental.pallas import tpu as pltpu

F32 = jnp.float32
BF16 = jnp.bfloat16

D_MODEL = 2048
GRID_W = 64
D_S5 = 1024
S5_GROUP = 16
S5_GROUPS = D_S5 // S5_GROUP
S5_STATE = 64
S5_DIRS = 2
LAMBDA_RE_MAX = -1e-4
S5_CHUNK = 16
LANE = 128
S5_GROUPS_PER_COL = LANE // S5_GROUP
D_HY = 1024
HY_ORDER = 2
HY_DIRS = 2
HY_SHORT = 3
HY_EMB = 33
HY_BANDS = (HY_EMB - 1) // 2
HY_TARGET = 1e-2
HY_FAST_PCT = 0.3
HY_SLOW_PCT = 1.5
FFT_N1 = 64
FFT_N2 = 128
FFT_N = FFT_N1 * FFT_N2
FILTER_TILE_ROWS = FFT_N1 + 8
I_HY = D_S5
I_GA = D_S5 + (HY_ORDER + 1) * D_HY
I_GB = I_GA + D_MODEL
D_IN = I_GB + D_MODEL
D_FF = 5632
N_SUB = 3
N_MOD = 3
HALF_STEP = 0.5
RMS_EPS = 1e-6

VMEM_LIMIT_BYTES = 58 * 1024 * 1024


def _rms_mod(x, gain, shift, scale):
    ms = jnp.mean(x * x, axis=-1, keepdims=True)
    y = x * lax.rsqrt(ms + RMS_EPS) * gain
    return y * (1.0 + scale) + shift


def _ada_kernel(c_ref, w_ref, b_ref, o_ref):
    c = c_ref[...]
    a = c * jax.nn.sigmoid(c)
    o_ref[...] = jnp.dot(a, w_ref[...], preferred_element_type=F32,
                         precision=lax.Precision.HIGHEST) + b_ref[...]


def _ada_mod(c_rows, w, b, tn=1024):
    rows, d = c_rows.shape
    n = w.shape[1]
    return pl.pallas_call(
        _ada_kernel,
        grid=(n // tn,),
        in_specs=[pl.BlockSpec((rows, d), lambda j: (0, 0)),
                  pl.BlockSpec((d, tn), lambda j: (0, j)),
                  pl.BlockSpec((1, tn), lambda j: (0, j))],
        out_specs=pl.BlockSpec((rows, tn), lambda j: (0, j)),
        out_shape=jax.ShapeDtypeStruct((rows, n), F32),
        compiler_params=pltpu.CompilerParams(
            dimension_semantics=("arbitrary",), vmem_limit_bytes=VMEM_LIMIT_BYTES),
        name="ada_mod",
    )(c_rows, w, b.reshape(1, n))


def _ffn_kernel(x_ref, shift_ref, scale_ref, gate_ref, gain_ref, wg_ref, wu_ref, wd_ref,
                fg_ref, o_ref, h_ref, *, final_norm):
    j = pl.program_id(1)

    @pl.when(j == 0)
    def _():
        h_ref[...] = _rms_mod(x_ref[...], gain_ref[...], shift_ref[...], scale_ref[...]).astype(BF16)
        o_ref[...] = jnp.zeros_like(o_ref)

    h = h_ref[...]
    g = jnp.dot(h, wg_ref[...].astype(BF16), preferred_element_type=F32)
    u = jnp.dot(h, wu_ref[...].astype(BF16), preferred_element_type=F32)
    a = (g * jax.nn.sigmoid(g) * u).astype(BF16)
    o_ref[...] += jnp.dot(a, wd_ref[...].astype(BF16), preferred_element_type=F32)

    @pl.when(j == pl.num_programs(1) - 1)
    def _():
        y = x_ref[...] + (HALF_STEP * gate_ref[...]) * o_ref[...]
        if final_norm:
            ms = jnp.mean(y * y, axis=-1, keepdims=True)
            y = y * lax.rsqrt(ms + RMS_EPS) * fg_ref[...]
        o_ref[...] = y


def _ffn_sublayer(x, mods, gain, wg, wu, wd, which, final_gain=None, tm=1024, tf=256):
    t, d = x.shape
    bm = mods[0].shape[0]
    tm = min(tm, t // bm)
    blocks_per_batch = (t // bm) // tm
    dff = wg.shape[2]
    final_norm = final_gain is not None
    fg = final_gain if final_norm else gain
    mod_spec = pl.BlockSpec((None, 1, d), lambda i, j: (i // blocks_per_batch, 0, 0))
    vec_spec = pl.BlockSpec((1, d), lambda i, j: (0, 0))
    return pl.pallas_call(
        functools.partial(_ffn_kernel, final_norm=final_norm),
        grid=(t // tm, dff // tf),
        in_specs=[pl.BlockSpec((tm, d), lambda i, j: (i, 0)),
                  mod_spec, mod_spec, mod_spec, vec_spec,
                  pl.BlockSpec((None, d, tf), lambda i, j: (which, 0, j)),
                  pl.BlockSpec((None, d, tf), lambda i, j: (which, 0, j)),
                  pl.BlockSpec((None, tf, d), lambda i, j: (which, j, 0)),
                  vec_spec],
        out_specs=pl.BlockSpec((tm, d), lambda i, j: (i, 0)),
        out_shape=jax.ShapeDtypeStruct((t, d), F32),
        scratch_shapes=[pltpu.VMEM((tm, d), BF16)],
        compiler_params=pltpu.CompilerParams(
            dimension_semantics=("parallel", "arbitrary"), vmem_limit_bytes=VMEM_LIMIT_BYTES),
        name="ffn_final" if final_norm else "ffn",
    )(x, *mods, gain.reshape(1, d), wg, wu, wd, fg.reshape(1, d))


def _proj_kernel(x_ref, shift_ref, scale_ref, gain_ref, w_ref, sw_ref, sb_ref, *rest, n_u, n_hy, row_len):
    o_refs, h_ref = rest[:-1], rest[-1]
    j = pl.program_id(1)

    @pl.when(j == 0)
    def _():
        h_ref[...] = _rms_mod(x_ref[...], gain_ref[...], shift_ref[...], scale_ref[...]).astype(BF16)

    p = jnp.dot(h_ref[...], w_ref[...].astype(BF16), preferred_element_type=F32)
    if n_hy == 0:
        o_refs[0][...] = p.astype(o_refs[0].dtype)
        return

    @pl.when(j < n_u)
    def _():
        o_refs[0][...] = p.astype(o_refs[0].dtype)

    @pl.when((j >= n_u) & (j < n_u + n_hy))
    def _():
        tm = p.shape[0]
        col = lax.broadcasted_iota(jnp.int32, p.shape, 0) % row_len
        prev = jnp.where(col == 0, 0.0, pltpu.roll(p, 1, 0))
        nxt = jnp.where(col == row_len - 1, 0.0, pltpu.roll(p, tm - 1, 0))
        sw = sw_ref[...]
        us = sb_ref[...] + prev * sw[0:1] + p * sw[1:2] + nxt * sw[2:3]
        o_refs[1][...] = us.astype(o_refs[1].dtype)

    @pl.when(j >= n_u + n_hy)
    def _():
        o_refs[2][...] = jax.nn.sigmoid(p).astype(BF16)


def _in_proj(x, shift, scale, gain, w, short_w=None, short_b=None, n_u=D_S5, n_hy=0, row_len=GRID_W,
             tm=1024, tn=512):
    t, d = x.shape
    bm = shift.shape[0]
    tm = min(tm, t // bm)
    blocks_per_batch = (t // bm) // tm
    n = w.shape[1]
    assert tm % row_len == 0 and (t // bm) % tm == 0
    bu, bh = n_u // tn, n_hy // tn
    bg = n // tn - bu - bh
    mod_spec = pl.BlockSpec((None, 1, d), lambda i, j: (i // blocks_per_batch, 0, 0))
    out_shape = [jax.ShapeDtypeStruct((t, n_u), F32)]
    out_specs = [pl.BlockSpec((tm, tn), lambda i, j: (i, jnp.minimum(j, bu - 1)))]
    if bh:
        out_shape += [jax.ShapeDtypeStruct((t, n_hy), BF16), jax.ShapeDtypeStruct((t, bg * tn), BF16)]
        out_specs += [pl.BlockSpec((tm, tn), lambda i, j: (i, jnp.clip(j - bu, 0, bh - 1))),
                      pl.BlockSpec((tm, tn), lambda i, j: (i, jnp.maximum(j - bu - bh, 0)))]
        sw, sb = short_w, short_b.reshape(1, n_hy)
        hy_blk = lambda i, j: (0, jnp.clip(j - bu, 0, bh - 1))
    else:
        sw, sb = jnp.zeros((HY_SHORT, tn), F32), jnp.zeros((1, tn), F32)
        hy_blk = lambda i, j: (0, 0)
    return pl.pallas_call(
        functools.partial(_proj_kernel, n_u=bu, n_hy=bh, row_len=row_len),
        grid=(t // tm, n // tn),
        in_specs=[pl.BlockSpec((tm, d), lambda i, j: (i, 0)),
                  mod_spec, mod_spec,
                  pl.BlockSpec((1, d), lambda i, j: (0, 0)),
                  pl.BlockSpec((d, tn), lambda i, j: (0, j)),
                  pl.BlockSpec((HY_SHORT, tn), hy_blk),
                  pl.BlockSpec((1, tn), hy_blk)],
        out_specs=out_specs,
        out_shape=out_shape,
        scratch_shapes=[pltpu.VMEM((tm, d), BF16)],
        compiler_params=pltpu.CompilerParams(
            dimension_semantics=("parallel", "arbitrary"), vmem_limit_bytes=VMEM_LIMIT_BYTES),
        name="in_proj",
    )(x, shift, scale, gain.reshape(1, d), w, sw, sb)


def _gelu_tanh(x):
    return 0.5 * x * (1.0 + jnp.tanh(math.sqrt(2.0 / math.pi) * (x + 0.044715 * (x * x * x))))


def _merge_kernel(x_ref, gate_ref, ys_ref, yh_ref, ga_ref, gb_ref, wpa_lo_ref, wpa_hi_ref, wpb_ref,
                  wout_ref, o_ref, s_ref, acc_ref):
    j = pl.program_id(1)

    @pl.when(j == 0)
    def _():
        s_ref[...] = _gelu_tanh(ys_ref[...].astype(F32)).astype(BF16)
        acc_ref[...] = jnp.zeros_like(acc_ref)

    s = s_ref[...]
    pa_lo = jnp.dot(s, wpa_lo_ref[...], preferred_element_type=F32)
    pa_hi = jnp.dot(s, wpa_hi_ref[...], preferred_element_type=F32)
    y_a = pa_lo * jax.nn.sigmoid(pa_hi)
    y_b = jnp.dot(yh_ref[...], wpb_ref[...], preferred_element_type=F32)
    m = ga_ref[...].astype(F32) * y_a + gb_ref[...].astype(F32) * y_b
    acc_ref[...] += jnp.dot(m.astype(BF16), wout_ref[...], preferred_element_type=F32)

    @pl.when(j == pl.num_programs(1) - 1)
    def _():
        o_ref[...] = x_ref[...] + gate_ref[...] * acc_ref[...]


def _merge(x, gate, y_s5, y_hy, sig_gates, w_pa, w_pb, w_out, tm=512, tn=512):
    t, d = x.shape
    bm = gate.shape[0]
    blocks_per_batch = (t // bm) // tm
    nj = d // tn
    ds5 = y_s5.shape[1]
    dhy = y_hy.shape[1]
    return pl.pallas_call(
        _merge_kernel,
        grid=(t // tm, nj),
        in_specs=[pl.BlockSpec((tm, d), lambda i, j: (i, 0)),
                  pl.BlockSpec((None, 1, d), lambda i, j: (i // blocks_per_batch, 0, 0)),
                  pl.BlockSpec((tm, ds5), lambda i, j: (i, 0)),
                  pl.BlockSpec((tm, dhy), lambda i, j: (i, 0)),
                  pl.BlockSpec((tm, tn), lambda i, j: (i, j)),
                  pl.BlockSpec((tm, tn), lambda i, j: (i, nj + j)),
                  pl.BlockSpec((ds5, tn), lambda i, j: (0, j)),
                  pl.BlockSpec((ds5, tn), lambda i, j: (0, nj + j)),
                  pl.BlockSpec((dhy, tn), lambda i, j: (0, j)),
                  pl.BlockSpec((tn, d), lambda i, j: (j, 0))],
        out_specs=pl.BlockSpec((tm, d), lambda i, j: (i, 0)),
        out_shape=jax.ShapeDtypeStruct((t, d), F32),
        scratch_shapes=[pltpu.VMEM((tm, ds5), BF16), pltpu.VMEM((tm, d), F32)],
        compiler_params=pltpu.CompilerParams(
            dimension_semantics=("parallel", "arbitrary"), vmem_limit_bytes=VMEM_LIMIT_BYTES),
        name="merge",
    )(x, gate, y_s5, y_hy, sig_gates, sig_gates, w_pa, w_pa, w_pb, w_out)


def _s5_weights_kernel(par_ref, bt_ref, c_ref, d_ref, bpow_ref, toep_ref, cpow_ref, tab_ref, ca_ref, *, n_steps):
    t, h, p = S5_CHUNK, S5_GROUP, S5_STATE
    lanes = 2 * p
    hp = lax.Precision.HIGHEST
    sgn = jnp.where(lax.broadcasted_iota(jnp.int32, (1, lanes), 1) < p, -1.0, 1.0)
    par = par_ref[...]
    gsum = None
    for d in range(S5_DIRS):
        lr = jnp.minimum(par[3 * d:3 * d + 1], LAMBDA_RE_MAX)
        li = par[3 * d + 1:3 * d + 2]
        dt = jnp.exp(par[3 * d + 2:3 * d + 3])
        zr, zi = lr * dt, li * dt

        def apow(j):
            mag = jnp.exp(j * zr)
            return mag * jnp.cos(j * zi), sgn * (mag * jnp.sin(j * zi))

        def cmul(x, a1, a2):
            return x * a1 + pltpu.roll(x, p, 1) * a2

        a1, a2 = apow(lax.broadcasted_iota(jnp.int32, (t + 1, 1), 0).astype(F32))
        nr, ni = a1[1:2] - 1.0, sgn * a2[1:2]
        den = lr * lr + li * li
        f_re = (nr * lr + ni * li) / den
        f_im = (ni * lr - nr * li) / den
        bbar = cmul(bt_ref[d], f_re, sgn * f_im)
        cc = c_ref[d]
        ca = [cmul(cc, a1[j:j + 1], a2[j:j + 1]) * (-sgn) for j in range(t + 1)]
        ca_ref[...] = jnp.zeros_like(ca_ref)
        for k in range(t):
            e_b, e_c = (t - 1 - k, k + 1) if d == 0 else (k, t - k)
            bpow_ref[k * h:(k + 1) * h, d * lanes:(d + 1) * lanes] = (
                cmul(bbar, a1[e_b:e_b + 1], a2[e_b:e_b + 1]).astype(BF16))
            cpow_ref[k * h:(k + 1) * h, d * lanes:(d + 1) * lanes] = ca[e_c].astype(BF16)
            l = t - 1 + k if d == 0 else t - 1 - k
            ca_ref[l * h:(l + 1) * h, :] = ca[k]
        g = lax.dot_general(bbar, ca_ref[...], (((1,), (1,)), ((), ())), preferred_element_type=F32, precision=hp)
        gsum = g if gsum is None else gsum + g
        for s in range(n_steps):
            s1, s2 = apow(float(t * 2 ** s))
            r = d * 2 * n_steps + 2 * s
            tab_ref[r:r + 1, :] = s1
            tab_ref[r + 1:r + 2, :] = s2
    wide = gsum.shape[1]
    col = lax.broadcasted_iota(jnp.int32, (h, wide), 1)
    row = lax.broadcasted_iota(jnp.int32, (h, wide), 0)
    gsum = gsum + jnp.where(col - (t - 1) * h == row, d_ref[...], 0.0)
    for k in range(t):
        off = (t - 1 - k) * h
        shifted = gsum if off == 0 else pltpu.roll(gsum, wide - off, 1)
        toep_ref[k * h:(k + 1) * h, :] = shifted[:, :t * h].astype(BF16)


def _s5_weights(lam_re, lam_im, log_dt, b_re, b_im, c_re, c_im, d_skip, n_steps):
    g, p, h, t = S5_GROUPS, S5_STATE, S5_GROUP, S5_CHUNK
    cat2 = lambda a: jnp.concatenate([a, a], axis=-1)
    par = jnp.stack([cat2(lam_re), cat2(lam_im), jnp.broadcast_to(log_dt[..., None], (S5_DIRS, g, 2 * p))], axis=1)
    par = par.transpose(2, 0, 1, 3).reshape(g, 3 * S5_DIRS, 2 * p)
    btc = jnp.concatenate([b_re, b_im], axis=2).transpose(1, 0, 3, 2)
    ccat = jnp.concatenate([c_re, c_im], axis=3).transpose(1, 0, 2, 3)
    wide = 2 * t * h
    drow = jnp.zeros((g, 1, wide), F32).at[:, 0, (t - 1) * h:t * h].set(d_skip.reshape(g, h))
    w = t * h
    sq = pl.BlockSpec((None, w, w), lambda i: (i, 0, 0))
    return pl.pallas_call(
        functools.partial(_s5_weights_kernel, n_steps=n_steps),
        grid=(g,),
        in_specs=[pl.BlockSpec((None, 3 * S5_DIRS, 2 * p), lambda i: (i, 0, 0)),
                  pl.BlockSpec((None, S5_DIRS, h, 2 * p), lambda i: (i, 0, 0, 0)),
                  pl.BlockSpec((None, S5_DIRS, h, 2 * p), lambda i: (i, 0, 0, 0)),
                  pl.BlockSpec((None, 1, wide), lambda i: (i, 0, 0))],
        out_specs=[sq, sq, sq, pl.BlockSpec((None, 4 * n_steps, 2 * p), lambda i: (i, 0, 0))],
        out_shape=[jax.ShapeDtypeStruct((g, w, w), BF16)] * 3 + [jax.ShapeDtypeStruct((g, 4 * n_steps, 2 * p), F32)],
        scratch_shapes=[pltpu.VMEM((wide, 2 * p), F32)],
        compiler_params=pltpu.CompilerParams(
            dimension_semantics=("parallel",), vmem_limit_bytes=VMEM_LIMIT_BYTES),
        name="s5_weights",
    )(par, btc, ccat, drow)


def _s5_select():
    t, h = S5_CHUNK, S5_GROUP
    sel = np.zeros((S5_GROUPS_PER_COL, t, LANE, t, h), np.float32)
    for gl in range(S5_GROUPS_PER_COL):
        for k in range(t):
            sel[gl, k, gl * h:(gl + 1) * h, k, :] = np.eye(h)
    return jnp.asarray(sel.reshape(S5_GROUPS_PER_COL, t * LANE, t * h)).astype(BF16)


def _s5_kernel(u_ref, uc_ref, bpow_ref, toep_ref, cpow_ref, tab_ref, sel_ref, y_ref, ucat_ref, ucc_ref, yacc_ref,
               *, n_steps):
    bsz, n_lat, t, _ = u_ref.shape
    n_ctx = uc_ref.shape[1]
    n_ch = n_lat + n_ctx
    rows = bsz * n_ch
    half = 2 * S5_STATE
    for k in range(t):
        ucat_ref[:, k * LANE:(k + 1) * LANE] = u_ref[:, :, k, :].reshape(bsz * n_lat, LANE).astype(BF16)
        ucc_ref[:, k * LANE:(k + 1) * LANE] = uc_ref[:, :, k, :].reshape(bsz * n_ctx, LANE).astype(BF16)
    yacc_ref[...] = jnp.zeros_like(yacc_ref)
    rib = lax.broadcasted_iota(jnp.int32, (rows, half), 0) % n_ch

    def cmul_add(acc, sh, a1, a2):
        return acc + a1 * sh + a2 * pltpu.roll(sh, S5_STATE, 1)

    def group(gl, carry):
        sel = sel_ref[gl]
        ul = jnp.dot(ucat_ref[...], sel, preferred_element_type=F32).astype(BF16)
        uc = jnp.dot(ucc_ref[...], sel, preferred_element_type=F32).astype(BF16)
        bpow = bpow_ref[gl]
        zl = jnp.dot(ul, bpow, preferred_element_type=F32)
        zc = jnp.dot(uc, bpow, preferred_element_type=F32)
        fparts, bparts = [], []
        for b in range(bsz):
            lat = slice(b * n_lat, (b + 1) * n_lat)
            ctx = slice(b * n_ctx, (b + 1) * n_ctx)
            fparts += [zc[ctx, :half], zl[lat, :half]]
            bparts += [zl[lat, half:], zc[ctx, half:]]
        fw = jnp.concatenate(fparts, axis=0)
        bw = jnp.concatenate(bparts, axis=0)
        tab = tab_ref[gl]
        for s in range(n_steps):
            d = 1 << s
            sh = jnp.where(rib >= d, pltpu.roll(fw, d, 0), 0.0)
            fw = cmul_add(fw, sh, tab[2 * s:2 * s + 1], tab[2 * s + 1:2 * s + 2])
            o = 2 * n_steps
            sh = jnp.where(rib < n_ch - d, pltpu.roll(bw, rows - d, 0), 0.0)
            bw = cmul_add(bw, sh, tab[o + 2 * s:o + 2 * s + 1], tab[o + 2 * s + 1:o + 2 * s + 2])
        fe = jnp.where(rib >= 1, pltpu.roll(fw, 1, 0), 0.0)
        be = jnp.where(rib < n_ch - 1, pltpu.roll(bw, rows - 1, 0), 0.0)
        fl = jnp.concatenate([fe[b * n_ch + n_ctx:(b + 1) * n_ch] for b in range(bsz)], axis=0)
        bl = jnp.concatenate([be[b * n_ch:b * n_ch + n_lat] for b in range(bsz)], axis=0)
        st = jnp.concatenate([fl, bl], axis=1).astype(BF16)
        y = (jnp.dot(ul, toep_ref[gl], preferred_element_type=F32)
             + lax.dot_general(st, cpow_ref[gl], (((1,), (1,)), ((), ())), preferred_element_type=F32))
        yacc_ref[...] += lax.dot_general(y.astype(BF16), sel, (((1,), (1,)), ((), ())),
                                         preferred_element_type=F32)
        return carry

    lax.fori_loop(0, S5_GROUPS_PER_COL, group, 0)
    for i in range(t):
        y_ref[:, :, i, :] = yacc_ref[:, i * LANE:(i + 1) * LANE].reshape(bsz, n_lat, LANE)


def _s5_mix(u4, u4c, bpw, toep, cpw, tab):
    bsz, n_lat, t, dm = u4.shape
    n_ctx = u4c.shape[1]
    w = t * S5_GROUP
    n_steps = tab.shape[1] // 4
    sel = _s5_select()
    gpc = S5_GROUPS_PER_COL
    wspec = pl.BlockSpec((gpc, w, w), lambda i: (i, 0, 0))
    return pl.pallas_call(
        functools.partial(_s5_kernel, n_steps=n_steps),
        grid=(dm // LANE,),
        in_specs=[pl.BlockSpec((bsz, n_lat, t, LANE), lambda i: (0, 0, 0, i)),
                  pl.BlockSpec((bsz, n_ctx, t, LANE), lambda i: (0, 0, 0, i)),
                  wspec, wspec, wspec,
                  pl.BlockSpec((gpc, 4 * n_steps, tab.shape[2]), lambda i: (i, 0, 0)),
                  pl.BlockSpec(sel.shape, lambda i: (0, 0, 0))],
        out_specs=pl.BlockSpec((bsz, n_lat, t, LANE), lambda i: (0, 0, 0, i)),
        out_shape=jax.ShapeDtypeStruct(u4.shape, F32),
        scratch_shapes=[pltpu.VMEM((bsz * n_lat, t * LANE), BF16), pltpu.VMEM((bsz * n_ctx, t * LANE), BF16),
                        pltpu.VMEM((bsz * n_lat, t * LANE), F32)],
        compiler_params=pltpu.CompilerParams(
            dimension_semantics=("parallel",), vmem_limit_bytes=VMEM_LIMIT_BYTES),
        name="s5_mix",
    )(u4, u4c, bpw, toep, cpw, tab, sel)


def _s5_bidirectional(u, u_ctx, lam_re, lam_im, log_dt, b_re, b_im, c_re, c_im, d_skip):
    bsz, length, dm = u.shape
    ctx_len = u_ctx.shape[1]
    t = S5_CHUNK
    n_lat, n_ctx = length // t, ctx_len // t
    n_steps = max(1, math.ceil(math.log2(n_lat + n_ctx)))
    bpw, toep, cpw, tab = _s5_weights(lam_re, lam_im, log_dt, b_re, b_im, c_re, c_im, d_skip, n_steps)
    y4 = _s5_mix(u.reshape(bsz, n_lat, t, dm), u_ctx.reshape(bsz, n_ctx, t, dm), bpw, toep, cpw, tab)
    return y4.reshape(bsz, length, dm)


def _dft_constants(real_input=False):
    n1 = np.arange(FFT_N1)
    n2 = np.arange(FFT_N2)
    half = FFT_N1 // 2
    th = 2 * np.pi * np.outer(n1, n1) / FFT_N1
    c1, s1 = np.cos(th), np.sin(th)
    if real_input:
        w1 = np.concatenate([c1, -s1], axis=1)
    else:
        w1 = np.concatenate([np.concatenate([c1[:half], -s1[:half]], axis=1),
                             np.concatenate([s1[:half], c1[:half]], axis=1)], axis=0)
    z = np.zeros_like(w1)
    w1p = np.block([[w1, z], [z, w1]])
    ph = 2 * np.pi * np.outer(n2, n1) / FFT_N
    t1 = np.concatenate([np.cos(ph), np.cos(ph)], axis=1)
    t2 = np.concatenate([np.sin(ph), -np.sin(ph)], axis=1)
    ps = 2 * np.pi * np.outer(n2, n2) / FFT_N2
    f2 = np.concatenate([np.cos(ps), -np.sin(ps)], axis=1)
    g2 = np.concatenate([np.cos(ps), np.sin(ps)], axis=1)
    c2, s2 = np.cos(ph).T, np.sin(ph).T
    wi = np.concatenate([np.concatenate([c1[:, :half], s1[:, :half]], axis=1),
                         np.concatenate([-s1[:, :half], c1[:, :half]], axis=1)], axis=0) / FFT_N
    zi = np.zeros_like(wi)
    wi2 = np.stack([np.concatenate([wi, zi], axis=1), np.concatenate([zi, wi], axis=1)])
    as_b = lambda a: jnp.asarray(a, F32).astype(BF16)
    as_f = lambda a: jnp.asarray(a, F32)
    return [as_b(w1p), as_f(t1), as_f(t2), as_b(f2), as_b(g2), as_f(c2), as_f(s2), as_b(wi2)]


def _fwd_spectrum(xp, w1p, t1, t2, f2):
    cp = xp.shape[0]
    hn = FFT_N1
    a = jnp.dot(xp.reshape(cp * FFT_N2, 128).astype(BF16), w1p, preferred_element_type=F32)
    out = []
    for par in range(2):
        ap = a[:, par * 128:(par + 1) * 128]
        ap = ap.reshape(cp, FFT_N2, 128) * t1 + pltpu.roll(ap, hn, 1).reshape(cp, FFT_N2, 128) * t2
        at = jnp.swapaxes(ap, 1, 2)
        p = jnp.dot(at.reshape(cp * 128, FFT_N2).astype(BF16), f2, preferred_element_type=F32)
        p = p.reshape(cp, 128, 2 * FFT_N2)
        out.append((p[:, :hn, :FFT_N2] - p[:, hn:, FFT_N2:], p[:, :hn, FFT_N2:] + p[:, hn:, :FFT_N2]))
    return out


def _inv_time(yre, yim, g2, c2, s2, wi_par):
    cp = yre.shape[0]
    hn = FFT_N1
    y = jnp.concatenate([yre, yim], axis=1).reshape(cp * 128, FFT_N2).astype(BF16)
    q = jnp.dot(y, g2, preferred_element_type=F32).reshape(cp, 128, 2 * FFT_N2)
    bre = q[:, :hn, :FFT_N2] - q[:, hn:, FFT_N2:]
    bim = q[:, :hn, FFT_N2:] + q[:, hn:, :FFT_N2]
    b2 = jnp.concatenate([bre * c2 - bim * s2, bre * s2 + bim * c2], axis=1)
    bt = jnp.swapaxes(b2, 1, 2)
    return jnp.dot(bt.reshape(cp * FFT_N2, 128).astype(BF16), wi_par, preferred_element_type=F32)


def _hyena_kernel(z_ref, g1_ref, g2_ref, kf_ref, bias_ref, w1p_ref, t1_ref, t2_ref, f2_ref, gi_ref, c2_ref,
                  s2_ref, wi_ref, o_ref):
    cp = z_ref.shape[0]
    z = z_ref[...].astype(F32)
    gates = (g1_ref, g2_ref)
    for o in range(HY_ORDER):
        spec = _fwd_spectrum(z, w1p_ref[...], t1_ref[...], t2_ref[...], f2_ref[...])
        conv = None
        for par in range(2):
            xre, xim = spec[par]
            kre = kf_ref[o, par, :, :FFT_N1, :]
            kim = kf_ref[o, par, :, FFT_N1:, :]
            part = _inv_time(xre * kre - xim * kim, xre * kim + xim * kre,
                             gi_ref[...], c2_ref[...], s2_ref[...], wi_ref[par])
            conv = part if conv is None else conv + part
        z = gates[o][...].astype(F32) * (conv.reshape(cp, FFT_N2, 128) + bias_ref[o] * z)
    o_ref[...] = z.astype(o_ref.dtype)


def _to_packed(v, bsz):
    _, length, ch = v.shape
    nn1 = length // FFT_N2
    return v.reshape(bsz, nn1, FFT_N2, ch // 2, 2).transpose(3, 2, 4, 0, 1).reshape(ch // 2, FFT_N2, 2 * bsz * nn1)


def _from_packed(vp, bsz):
    chp, _, lanes = vp.shape
    nn1 = lanes // (2 * bsz)
    return vp.reshape(chp, FFT_N2, 2, bsz, nn1).transpose(3, 4, 1, 0, 2).reshape(bsz, nn1 * FFT_N2, chp * 2)


def _hyena_conv(zp, g1p, g2p, kf, biasp, cp=8):
    chp = zp.shape[0]
    consts = _dft_constants()
    dspec = pl.BlockSpec((cp, FFT_N2, 128), lambda i: (i, 0, 0))
    full = lambda a: pl.BlockSpec(a.shape, lambda i: (0,) * a.ndim)
    return pl.pallas_call(
        _hyena_kernel,
        grid=(chp // cp,),
        in_specs=[dspec, dspec, dspec,
                  pl.BlockSpec((HY_ORDER, 2, cp, 128, FFT_N2), lambda i: (0, 0, i, 0, 0)),
                  pl.BlockSpec((HY_ORDER, cp, 1, 128), lambda i: (0, i, 0, 0))] + [full(a) for a in consts],
        out_specs=dspec,
        out_shape=jax.ShapeDtypeStruct(zp.shape, zp.dtype),
        compiler_params=pltpu.CompilerParams(
            dimension_semantics=("parallel",), vmem_limit_bytes=VMEM_LIMIT_BYTES),
        name="hyena_conv",
    )(zp, g1p, g2p, kf, biasp, *consts)


def _filter_time_kernel(w1t_ref, w1c_ref, w1s_ref, b1_ref, w2_ref, b2_ref, w3_ref, b3_ref, fr_ref,
                        wf_ref, wb_ref, df_ref, db_ref, o_ref, h_ref, k_ref, *, length):
    n_fft = 2 * length
    hp = lax.Precision.HIGHEST

    @pl.when(pl.program_id(0) == 0)
    def _():
        pos = lax.broadcasted_iota(jnp.int32, (1, n_fft), 1)
        lag = jnp.where(pos < length, pos, n_fft - pos).astype(F32)
        t = lag / float(length - 1)
        w = (2.0 * math.pi / length) * lag
        band_step = (HY_BANDS - 1 - 1e-4) / (HY_BANDS - 1)
        bands = 1e-4 + band_step * lax.broadcasted_iota(jnp.int32, (HY_BANDS, 1), 0).astype(F32)
        ang = bands * w
        fr = fr_ref[...]
        h = (w1t_ref[...] * t + jnp.dot(w1c_ref[...], jnp.cos(ang), preferred_element_type=F32, precision=hp)
             - jnp.dot(w1s_ref[...], jnp.sin(ang), preferred_element_type=F32, precision=hp))
        h = jnp.sin(fr * (h + b1_ref[...]))
        h = jnp.sin(fr * (jnp.dot(w2_ref[...], h, preferred_element_type=F32, precision=hp) + b2_ref[...]))
        h = jnp.sin(fr * (jnp.dot(w3_ref[...], h, preferred_element_type=F32, precision=hp) + b3_ref[...]))
        hi = h.astype(BF16)
        h_ref[0] = hi
        h_ref[1] = (h - hi.astype(F32)).astype(BF16)

    def dot3(w, lo, hi_):
        w_hi = w.astype(BF16)
        w_lo = (w - w_hi.astype(F32)).astype(BF16)
        h_hi, h_lo = h_ref[0, :, lo:hi_], h_ref[1, :, lo:hi_]
        return (jnp.dot(w_hi, h_hi, preferred_element_type=F32) + jnp.dot(w_hi, h_lo, preferred_element_type=F32)
                + jnp.dot(w_lo, h_hi, preferred_element_type=F32))

    pos = lax.broadcasted_iota(jnp.int32, (1, length), 1)
    tf = pos.astype(F32) / float(length - 1)
    tb = (length - pos).astype(F32) / float(length - 1)
    kf = dot3(wf_ref[...], 0, length) * jnp.exp(-tf * df_ref[...])
    kb = dot3(wb_ref[...], length, n_fft) * jnp.exp(-tb * db_ref[...])
    kb = jnp.where(pos == 0, 0.0, kb)
    inv = 1.0 / (jnp.sum(jnp.abs(kf), axis=1, keepdims=True) + jnp.sum(jnp.abs(kb), axis=1, keepdims=True))
    k_ref[:, :length] = kf * inv
    k_ref[:, length:] = kb * inv
    for n1 in range(FFT_N1):
        o_ref[:, n1, :] = k_ref[:, n1 * FFT_N2:(n1 + 1) * FFT_N2]
    o_ref[:, FFT_N1:, :] = jnp.zeros((o_ref.shape[0], FILTER_TILE_ROWS - FFT_N1, FFT_N2), F32)


def _filter_time(length, w1, b1, w2, b2, w3, b3, freq, w_out, cb=128):
    col = lambda v: v.reshape(-1, 1)
    w1t = w1.T
    n_ch = w_out.shape[1]
    deltas = jnp.abs(jnp.linspace(math.log(HY_TARGET) / HY_SLOW_PCT, math.log(HY_TARGET) / HY_FAST_PCT,
                                  n_ch, dtype=F32)).reshape(n_ch, 1)
    wot = w_out.T
    nb = D_HY // cb
    small = lambda a: pl.BlockSpec(a.shape, lambda i: (0,) * a.ndim)
    fwd = lambda i: ((i // nb) * HY_DIRS * nb + i % nb, 0)
    bwd = lambda i: ((i // nb) * HY_DIRS * nb + nb + i % nb, 0)
    ins = [w1t[:, 0:1], w1t[:, 1:1 + HY_BANDS], w1t[:, 1 + HY_BANDS:], col(b1), w2.T, col(b2), w3.T, col(b3),
           col(freq)]
    hy_ff = w2.shape[0]
    return pl.pallas_call(
        functools.partial(_filter_time_kernel, length=length),
        grid=(HY_ORDER * nb,),
        in_specs=[small(a) for a in ins] + [pl.BlockSpec((cb, hy_ff), fwd), pl.BlockSpec((cb, hy_ff), bwd),
                                            pl.BlockSpec((cb, 1), fwd), pl.BlockSpec((cb, 1), bwd)],
        out_specs=pl.BlockSpec((cb, FILTER_TILE_ROWS, FFT_N2), lambda i: (i, 0, 0)),
        out_shape=jax.ShapeDtypeStruct((HY_ORDER * D_HY, FILTER_TILE_ROWS, FFT_N2), F32),
        scratch_shapes=[pltpu.VMEM((2, hy_ff, 2 * length), BF16), pltpu.VMEM((cb, 2 * length), F32)],
        compiler_params=pltpu.CompilerParams(
            dimension_semantics=("arbitrary",), vmem_limit_bytes=VMEM_LIMIT_BYTES),
        name="hyena_filter_time",
    )(*ins, wot, wot, deltas, deltas)


def _filter_spec_kernel(k_ref, w1p_ref, t1_ref, t2_ref, f2_ref, o_ref):
    cb = k_ref.shape[0]
    kt = jnp.swapaxes(k_ref[:, :FFT_N1, :], 1, 2).reshape(cb // 2, 2, FFT_N2, FFT_N1)
    xp = jnp.concatenate([kt[:, 0], kt[:, 1]], axis=-1)
    spec = _fwd_spectrum(xp, w1p_ref[...], t1_ref[...], t2_ref[...], f2_ref[...])
    for par in range(2):
        o_ref[par, :, :FFT_N1, :] = spec[par][0]
        o_ref[par, :, FFT_N1:, :] = spec[par][1]


def _filter_spectrum(kt, cb=32):
    consts = _dft_constants(real_input=True)[:4]
    nb = D_HY // cb
    full = lambda a: pl.BlockSpec(a.shape, lambda i: (0,) * a.ndim)
    return pl.pallas_call(
        _filter_spec_kernel,
        grid=(HY_ORDER * nb,),
        in_specs=[pl.BlockSpec((cb, FILTER_TILE_ROWS, FFT_N2), lambda i: (i, 0, 0))] + [full(a) for a in consts],
        out_specs=pl.BlockSpec((None, 2, cb // 2, 2 * FFT_N1, FFT_N2), lambda i: (i // nb, 0, i % nb, 0, 0)),
        out_shape=jax.ShapeDtypeStruct((HY_ORDER, 2, D_HY // 2, 2 * FFT_N1, FFT_N2), F32),
        compiler_params=pltpu.CompilerParams(
            dimension_semantics=("parallel",), vmem_limit_bytes=VMEM_LIMIT_BYTES),
        name="hyena_filter_spectrum",
    )(kt, *consts)


def _hyena(us, w1, b1, w2, b2, w3, b3, freq, w_out, bias):
    bsz, length, _ = us.shape
    assert 2 * length == FFT_N and bsz == 2, "one complex transform carries exactly two batch rows"
    zp, g1p, g2p = (_to_packed(us[..., i * D_HY:(i + 1) * D_HY], bsz) for i in range(HY_ORDER + 1))
    kf = _filter_spectrum(_filter_time(length, w1, b1, w2, b2, w3, b3, freq, w_out))
    biasp = jnp.repeat(bias.reshape(HY_ORDER, D_HY // 2, 1, 2), FFT_N1, axis=-1)
    return _from_packed(_hyena_conv(zp, g1p, g2p, kf, biasp), bsz)


def kernel(x, c, ctx, c_ctx, w_ada, b_ada, norm_g, ffn_w_gate, ffn_w_up, ffn_w_down, w_in,
           s5_lam_re, s5_lam_im, s5_log_dt, s5_b_re, s5_b_im, s5_c_re, s5_c_im, s5_d,
           hy_short_w, hy_short_b, hy_w1, hy_b1, hy_w2, hy_b2, hy_w3, hy_b3, hy_freq, hy_w_out,
           hy_bias, w_pa, w_pb, w_out, final_g):
    bsz, seq, d = x.shape
    ctx_len = ctx.shape[1]
    n_rows = seq // GRID_W
    depth = w_ada.shape[0]
    assert depth == 1, "context-token outputs are only dropped by the last layer"
    l = 0

    c_rows = jnp.concatenate([c, c_ctx[None, :], jnp.zeros((8 - bsz - 1, d), F32)], axis=0)
    mod_all = _ada_mod(c_rows, w_ada[l], b_ada[l])
    mod = mod_all[:bsz].reshape(bsz, N_SUB, N_MOD, 1, d)
    mod_c = mod_all[bsz:bsz + 1].reshape(1, N_SUB, N_MOD, 1, d)

    def mods(m, sub):
        return tuple(m[:, sub, k] for k in range(N_MOD))

    wg, wu, wd = ffn_w_gate[l], ffn_w_up[l], ffn_w_down[l]
    w_in_b = w_in[l]

    xt = x.reshape(bsz * seq, d)
    ct = ctx.reshape(bsz * ctx_len, d)

    xt = _ffn_sublayer(xt, mods(mod, 0), norm_g[l, 0], wg, wu, wd, 0)
    ct = _ffn_sublayer(ct, mods(mod_c, 0), norm_g[l, 0], wg, wu, wd, 0)

    assert GRID_W * n_rows == seq
    u_s5, us_hy, sig_gates = _in_proj(xt, mod[:, 1, 0], mod[:, 1, 1], norm_g[l, 1], w_in_b,
                                      hy_short_w[l], hy_short_b[l], n_u=I_HY, n_hy=I_GA - I_HY)
    (u_ctx,) = _in_proj(ct, mod_c[:, 1, 0], mod_c[:, 1, 1], norm_g[l, 1], w_in_b[:, :D_S5])

    y_s5 = _s5_bidirectional(u_s5.reshape(bsz, seq, D_S5), u_ctx.reshape(bsz, ctx_len, D_S5),
                             s5_lam_re[l], s5_lam_im[l], s5_log_dt[l],
                             s5_b_re[l], s5_b_im[l], s5_c_re[l], s5_c_im[l], s5_d[l])
    y_hy = _hyena(us_hy.reshape(bsz, seq, I_GA - I_HY),
                  hy_w1[l], hy_b1[l], hy_w2[l], hy_b2[l], hy_w3[l], hy_b3[l], hy_freq[l],
                  hy_w_out[l], hy_bias[l])

    xt = _merge(xt, mod[:, 1, 2], y_s5.reshape(bsz * seq, D_S5), y_hy.reshape(bsz * seq, D_HY),
                sig_gates, w_pa[l].astype(BF16), w_pb[l].astype(BF16), w_out[l].astype(BF16))

    xt = _ffn_sublayer(xt, mods(mod, 2), norm_g[l, 2], wg, wu, wd, 1, final_gain=final_g)
    return xt.reshape(bsz, seq, d)
```

```python
import functools
import math

import jax
import jax.numpy as jnp
import numpy as np
from jax import lax
from jax.experimental import pallas as pl
from jax.experimental.pallas import tpu as pltpu

F32 = jnp.float32
BF16 = jnp.bfloat16

D_MODEL = 2048
GRID_W = 64
D_S5 = 1024
S5_GROUP = 16
S5_GROUPS = D_S5 // S5_GROUP
S5_STATE = 64
S5_DIRS = 2
LAMBDA_RE_MAX = -1e-4
S5_CHUNK = 16
LANE = 128
S5_GROUPS_PER_COL = LANE // S5_GROUP
D_HY = 1024
HY_ORDER = 2
HY_DIRS = 2
HY_SHORT = 3
HY_EMB = 33
HY_BANDS = (HY_EMB - 1) // 2
HY_TARGET = 1e-2
HY_FAST_PCT = 0.3
HY_SLOW_PCT = 1.5
FFT_N1 = 64
FFT_N2 = 128
FFT_N = FFT_N1 * FFT_N2
FILTER_TILE_ROWS = FFT_N1 + 8
HY_TILE_PITCH = FILTER_TILE_ROWS
HY_PAIRS_PER_COL = LANE // 2
HY_PAIRS_PER_STEP = 8
I_HY = D_S5
I_GA = D_S5 + (HY_ORDER + 1) * D_HY
I_GB = I_GA + D_MODEL
D_IN = I_GB + D_MODEL
D_FF = 5632
N_SUB = 3
N_MOD = 3
HALF_STEP = 0.5
RMS_EPS = 1e-6

VMEM_LIMIT_BYTES = 58 * 1024 * 1024


def _rms_mod(x, gain, shift, scale):
    ms = jnp.mean(x * x, axis=-1, keepdims=True)
    y = x * lax.rsqrt(ms + RMS_EPS) * gain
    return y * (1.0 + scale) + shift


def _ada_kernel(c_ref, w_ref, b_ref, o_ref):
    c = c_ref[...]
    a = c * jax.nn.sigmoid(c)
    o_ref[...] = jnp.dot(a, w_ref[...], preferred_element_type=F32,
                         precision=lax.Precision.HIGHEST) + b_ref[...]


def _ada_mod(c_rows, w, b, tn=1024):
    rows, d = c_rows.shape
    n = w.shape[1]
    return pl.pallas_call(
        _ada_kernel,
        grid=(n // tn,),
        in_specs=[pl.BlockSpec((rows, d), lambda j: (0, 0)),
                  pl.BlockSpec((d, tn), lambda j: (0, j)),
                  pl.BlockSpec((1, tn), lambda j: (0, j))],
        out_specs=pl.BlockSpec((rows, tn), lambda j: (0, j)),
        out_shape=jax.ShapeDtypeStruct((rows, n), F32),
        compiler_params=pltpu.CompilerParams(
            dimension_semantics=("arbitrary",), vmem_limit_bytes=VMEM_LIMIT_BYTES),
        name="ada_mod",
    )(c_rows, w, b.reshape(1, n))


def _ffn_kernel(x_ref, shift_ref, scale_ref, gate_ref, gain_ref, wg_ref, wu_ref, wd_ref,
                fg_ref, o_ref, h_ref, *, final_norm):
    j = pl.program_id(1)

    @pl.when(j == 0)
    def _():
        h_ref[...] = _rms_mod(x_ref[...], gain_ref[...], shift_ref[...], scale_ref[...]).astype(BF16)
        o_ref[...] = jnp.zeros_like(o_ref)

    h = h_ref[...]
    g = jnp.dot(h, wg_ref[...].astype(BF16), preferred_element_type=F32)
    u = jnp.dot(h, wu_ref[...].astype(BF16), preferred_element_type=F32)
    a = (g * jax.nn.sigmoid(g) * u).astype(BF16)
    o_ref[...] += jnp.dot(a, wd_ref[...].astype(BF16), preferred_element_type=F32)

    @pl.when(j == pl.num_programs(1) - 1)
    def _():
        y = x_ref[...] + (HALF_STEP * gate_ref[...]) * o_ref[...]
        if final_norm:
            ms = jnp.mean(y * y, axis=-1, keepdims=True)
            y = y * lax.rsqrt(ms + RMS_EPS) * fg_ref[...]
        o_ref[...] = y


def _ffn_sublayer(x, mods, gain, wg, wu, wd, which, final_gain=None, tm=1024, tf=256):
    t, d = x.shape
    bm = mods[0].shape[0]
    tm = min(tm, t // bm)
    blocks_per_batch = (t // bm) // tm
    dff = wg.shape[2]
    final_norm = final_gain is not None
    fg = final_gain if final_norm else gain
    mod_spec = pl.BlockSpec((None, 1, d), lambda i, j: (i // blocks_per_batch, 0, 0))
    vec_spec = pl.BlockSpec((1, d), lambda i, j: (0, 0))
    return pl.pallas_call(
        functools.partial(_ffn_kernel, final_norm=final_norm),
        grid=(t // tm, dff // tf),
        in_specs=[pl.BlockSpec((tm, d), lambda i, j: (i, 0)),
                  mod_spec, mod_spec, mod_spec, vec_spec,
                  pl.BlockSpec((None, d, tf), lambda i, j: (which, 0, j)),
                  pl.BlockSpec((None, d, tf), lambda i, j: (which, 0, j)),
                  pl.BlockSpec((None, tf, d), lambda i, j: (which, j, 0)),
                  vec_spec],
        out_specs=pl.BlockSpec((tm, d), lambda i, j: (i, 0)),
        out_shape=jax.ShapeDtypeStruct((t, d), F32),
        scratch_shapes=[pltpu.VMEM((tm, d), BF16)],
        compiler_params=pltpu.CompilerParams(
            dimension_semantics=("parallel", "arbitrary"), vmem_limit_bytes=VMEM_LIMIT_BYTES),
        name="ffn_final" if final_norm else "ffn",
    )(x, *mods, gain.reshape(1, d), wg, wu, wd, fg.reshape(1, d))


def _proj_kernel(x_ref, shift_ref, scale_ref, gain_ref, w_ref, sw_ref, sb_ref, *rest, n_u, n_hy, row_len):
    o_refs, h_ref = rest[:-1], rest[-1]
    j = pl.program_id(1)

    @pl.when(j == 0)
    def _():
        h_ref[...] = _rms_mod(x_ref[...], gain_ref[...], shift_ref[...], scale_ref[...]).astype(BF16)

    p = jnp.dot(h_ref[...], w_ref[...].astype(BF16), preferred_element_type=F32)
    if n_hy == 0:
        o_refs[0][...] = p.astype(o_refs[0].dtype)
        return

    @pl.when(j < n_u)
    def _():
        o_refs[0][...] = p.astype(o_refs[0].dtype)

    @pl.when((j >= n_u) & (j < n_u + n_hy))
    def _():
        tm = p.shape[0]
        col = lax.broadcasted_iota(jnp.int32, p.shape, 0) % row_len
        prev = jnp.where(col == 0, 0.0, pltpu.roll(p, 1, 0))
        nxt = jnp.where(col == row_len - 1, 0.0, pltpu.roll(p, tm - 1, 0))
        sw = sw_ref[...]
        us = sb_ref[...] + prev * sw[0:1] + p * sw[1:2] + nxt * sw[2:3]
        o_refs[1][...] = us.astype(o_refs[1].dtype)

    @pl.when(j >= n_u + n_hy)
    def _():
        o_refs[2][...] = jax.nn.sigmoid(p).astype(BF16)


def _in_proj(x, shift, scale, gain, w, short_w=None, short_b=None, n_u=D_S5, n_hy=0, row_len=GRID_W,
             tm=1024, tn=512):
    t, d = x.shape
    bm = shift.shape[0]
    tm = min(tm, t // bm)
    blocks_per_batch = (t // bm) // tm
    n = w.shape[1]
    assert tm % row_len == 0 and (t // bm) % tm == 0
    bu, bh = n_u // tn, n_hy // tn
    bg = n // tn - bu - bh
    mod_spec = pl.BlockSpec((None, 1, d), lambda i, j: (i // blocks_per_batch, 0, 0))
    out_shape = [jax.ShapeDtypeStruct((t, n_u), F32)]
    out_specs = [pl.BlockSpec((tm, tn), lambda i, j: (i, jnp.minimum(j, bu - 1)))]
    if bh:
        out_shape += [jax.ShapeDtypeStruct((t, n_hy), BF16), jax.ShapeDtypeStruct((t, bg * tn), BF16)]
        out_specs += [pl.BlockSpec((tm, tn), lambda i, j: (i, jnp.clip(j - bu, 0, bh - 1))),
                      pl.BlockSpec((tm, tn), lambda i, j: (i, jnp.maximum(j - bu - bh, 0)))]
        sw, sb = short_w, short_b.reshape(1, n_hy)
        hy_blk = lambda i, j: (0, jnp.clip(j - bu, 0, bh - 1))
    else:
        sw, sb = jnp.zeros((HY_SHORT, tn), F32), jnp.zeros((1, tn), F32)
        hy_blk = lambda i, j: (0, 0)
    return pl.pallas_call(
        functools.partial(_proj_kernel, n_u=bu, n_hy=bh, row_len=row_len),
        grid=(t // tm, n // tn),
        in_specs=[pl.BlockSpec((tm, d), lambda i, j: (i, 0)),
                  mod_spec, mod_spec,
                  pl.BlockSpec((1, d), lambda i, j: (0, 0)),
                  pl.BlockSpec((d, tn), lambda i, j: (0, j)),
                  pl.BlockSpec((HY_SHORT, tn), hy_blk),
                  pl.BlockSpec((1, tn), hy_blk)],
        out_specs=out_specs,
        out_shape=out_shape,
        scratch_shapes=[pltpu.VMEM((tm, d), BF16)],
        compiler_params=pltpu.CompilerParams(
            dimension_semantics=("parallel", "arbitrary"), vmem_limit_bytes=VMEM_LIMIT_BYTES),
        name="in_proj",
    )(x, shift, scale, gain.reshape(1, d), w, sw, sb)


def _gelu_tanh(x):
    return 0.5 * x * (1.0 + jnp.tanh(math.sqrt(2.0 / math.pi) * (x + 0.044715 * (x * x * x))))


def _merge_kernel(x_ref, gate_ref, ys_ref, yh_ref, ga_ref, gb_ref, wpa_lo_ref, wpa_hi_ref, wpb_ref,
                  wout_ref, o_ref, s_ref, acc_ref):
    j = pl.program_id(1)

    @pl.when(j == 0)
    def _():
        s_ref[...] = _gelu_tanh(ys_ref[...].astype(F32)).astype(BF16)
        acc_ref[...] = jnp.zeros_like(acc_ref)

    s = s_ref[...]
    pa_lo = jnp.dot(s, wpa_lo_ref[...], preferred_element_type=F32)
    pa_hi = jnp.dot(s, wpa_hi_ref[...], preferred_element_type=F32)
    y_a = pa_lo * jax.nn.sigmoid(pa_hi)
    y_b = jnp.dot(yh_ref[...], wpb_ref[...], preferred_element_type=F32)
    m = ga_ref[...].astype(F32) * y_a + gb_ref[...].astype(F32) * y_b
    acc_ref[...] += jnp.dot(m.astype(BF16), wout_ref[...], preferred_element_type=F32)

    @pl.when(j == pl.num_programs(1) - 1)
    def _():
        o_ref[...] = x_ref[...] + gate_ref[...] * acc_ref[...]


def _merge(x, gate, y_s5, y_hy, sig_gates, w_pa, w_pb, w_out, tm=512, tn=512):
    t, d = x.shape
    bm = gate.shape[0]
    blocks_per_batch = (t // bm) // tm
    nj = d // tn
    ds5 = y_s5.shape[1]
    dhy = y_hy.shape[1]
    return pl.pallas_call(
        _merge_kernel,
        grid=(t // tm, nj),
        in_specs=[pl.BlockSpec((tm, d), lambda i, j: (i, 0)),
                  pl.BlockSpec((None, 1, d), lambda i, j: (i // blocks_per_batch, 0, 0)),
                  pl.BlockSpec((tm, ds5), lambda i, j: (i, 0)),
                  pl.BlockSpec((tm, dhy), lambda i, j: (i, 0)),
                  pl.BlockSpec((tm, tn), lambda i, j: (i, j)),
                  pl.BlockSpec((tm, tn), lambda i, j: (i, nj + j)),
                  pl.BlockSpec((ds5, tn), lambda i, j: (0, j)),
                  pl.BlockSpec((ds5, tn), lambda i, j: (0, nj + j)),
                  pl.BlockSpec((dhy, tn), lambda i, j: (0, j)),
                  pl.BlockSpec((tn, d), lambda i, j: (j, 0))],
        out_specs=pl.BlockSpec((tm, d), lambda i, j: (i, 0)),
        out_shape=jax.ShapeDtypeStruct((t, d), F32),
        scratch_shapes=[pltpu.VMEM((tm, ds5), BF16), pltpu.VMEM((tm, d), F32)],
        compiler_params=pltpu.CompilerParams(
            dimension_semantics=("parallel", "arbitrary"), vmem_limit_bytes=VMEM_LIMIT_BYTES),
        name="merge",
    )(x, gate, y_s5, y_hy, sig_gates, sig_gates, w_pa, w_pa, w_pb, w_out)


def _s5_weights_kernel(par_ref, bt_ref, c_ref, d_ref, bpow_ref, toep_ref, cpow_ref, tab_ref, ca_ref, *, n_steps):
    t, h, p = S5_CHUNK, S5_GROUP, S5_STATE
    lanes = 2 * p
    hp = lax.Precision.HIGHEST
    sgn = jnp.where(lax.broadcasted_iota(jnp.int32, (1, lanes), 1) < p, -1.0, 1.0)
    par = par_ref[...]
    gsum = None
    for d in range(S5_DIRS):
        lr = jnp.minimum(par[3 * d:3 * d + 1], LAMBDA_RE_MAX)
        li = par[3 * d + 1:3 * d + 2]
        dt = jnp.exp(par[3 * d + 2:3 * d + 3])
        zr, zi = lr * dt, li * dt

        def apow(j):
            mag = jnp.exp(j * zr)
            return mag * jnp.cos(j * zi), sgn * (mag * jnp.sin(j * zi))

        def cmul(x, a1, a2):
            return x * a1 + pltpu.roll(x, p, 1) * a2

        a1, a2 = apow(lax.broadcasted_iota(jnp.int32, (t + 1, 1), 0).astype(F32))
        nr, ni = a1[1:2] - 1.0, sgn * a2[1:2]
        den = lr * lr + li * li
        f_re = (nr * lr + ni * li) / den
        f_im = (ni * lr - nr * li) / den
        bbar = cmul(bt_ref[d], f_re, sgn * f_im)
        cc = c_ref[d]
        ca = [cmul(cc, a1[j:j + 1], a2[j:j + 1]) * (-sgn) for j in range(t + 1)]
        ca_ref[...] = jnp.zeros_like(ca_ref)
        for k in range(t):
            e_b, e_c = (t - 1 - k, k + 1) if d == 0 else (k, t - k)
            bpow_ref[k * h:(k + 1) * h, d * lanes:(d + 1) * lanes] = (
                cmul(bbar, a1[e_b:e_b + 1], a2[e_b:e_b + 1]).astype(BF16))
            cpow_ref[k * h:(k + 1) * h, d * lanes:(d + 1) * lanes] = ca[e_c].astype(BF16)
            l = t - 1 + k if d == 0 else t - 1 - k
            ca_ref[l * h:(l + 1) * h, :] = ca[k]
        g = lax.dot_general(bbar, ca_ref[...], (((1,), (1,)), ((), ())), preferred_element_type=F32, precision=hp)
        gsum = g if gsum is None else gsum + g
        for s in range(n_steps):
            s1, s2 = apow(float(t * 2 ** s))
            r = d * 2 * n_steps + 2 * s
            tab_ref[r:r + 1, :] = s1
            tab_ref[r + 1:r + 2, :] = s2
    wide = gsum.shape[1]
    col = lax.broadcasted_iota(jnp.int32, (h, wide), 1)
    row = lax.broadcasted_iota(jnp.int32, (h, wide), 0)
    gsum = gsum + jnp.where(col - (t - 1) * h == row, d_ref[...], 0.0)
    for k in range(t):
        off = (t - 1 - k) * h
        shifted = gsum if off == 0 else pltpu.roll(gsum, wide - off, 1)
        toep_ref[k * h:(k + 1) * h, :] = shifted[:, :t * h].astype(BF16)


def _s5_weights(lam_re, lam_im, log_dt, b_re, b_im, c_re, c_im, d_skip, n_steps):
    g, p, h, t = S5_GROUPS, S5_STATE, S5_GROUP, S5_CHUNK
    cat2 = lambda a: jnp.concatenate([a, a], axis=-1)
    par = jnp.stack([cat2(lam_re), cat2(lam_im), jnp.broadcast_to(log_dt[..., None], (S5_DIRS, g, 2 * p))], axis=1)
    par = par.transpose(2, 0, 1, 3).reshape(g, 3 * S5_DIRS, 2 * p)
    btc = jnp.concatenate([b_re, b_im], axis=2).transpose(1, 0, 3, 2)
    ccat = jnp.concatenate([c_re, c_im], axis=3).transpose(1, 0, 2, 3)
    wide = 2 * t * h
    drow = jnp.zeros((g, 1, wide), F32).at[:, 0, (t - 1) * h:t * h].set(d_skip.reshape(g, h))
    w = t * h
    sq = pl.BlockSpec((None, w, w), lambda i: (i, 0, 0))
    return pl.pallas_call(
        functools.partial(_s5_weights_kernel, n_steps=n_steps),
        grid=(g,),
        in_specs=[pl.BlockSpec((None, 3 * S5_DIRS, 2 * p), lambda i: (i, 0, 0)),
                  pl.BlockSpec((None, S5_DIRS, h, 2 * p), lambda i: (i, 0, 0, 0)),
                  pl.BlockSpec((None, S5_DIRS, h, 2 * p), lambda i: (i, 0, 0, 0)),
                  pl.BlockSpec((None, 1, wide), lambda i: (i, 0, 0))],
        out_specs=[sq, sq, sq, pl.BlockSpec((None, 4 * n_steps, 2 * p), lambda i: (i, 0, 0))],
        out_shape=[jax.ShapeDtypeStruct((g, w, w), BF16)] * 3 + [jax.ShapeDtypeStruct((g, 4 * n_steps, 2 * p), F32)],
        scratch_shapes=[pltpu.VMEM((wide, 2 * p), F32)],
        compiler_params=pltpu.CompilerParams(
            dimension_semantics=("parallel",), vmem_limit_bytes=VMEM_LIMIT_BYTES),
        name="s5_weights",
    )(par, btc, ccat, drow)


def _s5_select():
    t, h = S5_CHUNK, S5_GROUP
    sel = np.zeros((S5_GROUPS_PER_COL, t, LANE, t, h), np.float32)
    for gl in range(S5_GROUPS_PER_COL):
        for k in range(t):
            sel[gl, k, gl * h:(gl + 1) * h, k, :] = np.eye(h)
    return jnp.asarray(sel.reshape(S5_GROUPS_PER_COL, t * LANE, t * h)).astype(BF16)


def _s5_kernel(u_ref, uc_ref, bpow_ref, toep_ref, cpow_ref, tab_ref, sel_ref, y_ref, ucat_ref, ucc_ref, yacc_ref,
               *, n_steps):
    bsz, n_lat, t, _ = u_ref.shape
    n_ctx = uc_ref.shape[1]
    n_ch = n_lat + n_ctx
    rows = bsz * n_ch
    half = 2 * S5_STATE
    for k in range(t):
        yacc_ref[:, k * LANE:(k + 1) * LANE] = u_ref[:, :, k, :].reshape(bsz * n_lat, LANE)
        ucc_ref[:, k * LANE:(k + 1) * LANE] = uc_ref[:, :, k, :].reshape(bsz * n_ctx, LANE).astype(BF16)
    ucat_ref[...] = yacc_ref[...].astype(BF16)
    yacc_ref[...] = jnp.zeros_like(yacc_ref)
    rib = lax.broadcasted_iota(jnp.int32, (rows, half), 0) % n_ch

    def cmul_add(acc, sh, a1, a2):
        return acc + a1 * sh + a2 * pltpu.roll(sh, S5_STATE, 1)

    def group(gl, carry):
        sel = sel_ref[gl]
        ul = jnp.dot(ucat_ref[...], sel, preferred_element_type=F32).astype(BF16)
        uc = jnp.dot(ucc_ref[...], sel, preferred_element_type=F32).astype(BF16)
        bpow = bpow_ref[gl]
        zl = jnp.dot(ul, bpow, preferred_element_type=F32)
        zc = jnp.dot(uc, bpow, preferred_element_type=F32)
        fparts, bparts = [], []
        for b in range(bsz):
            lat = slice(b * n_lat, (b + 1) * n_lat)
            ctx = slice(b * n_ctx, (b + 1) * n_ctx)
            fparts += [zc[ctx, :half], zl[lat, :half]]
            bparts += [zl[lat, half:], zc[ctx, half:]]
        fw = jnp.concatenate(fparts, axis=0)
        bw = jnp.concatenate(bparts, axis=0)
        tab = tab_ref[gl]
        for s in range(n_steps):
            d = 1 << s
            sh = jnp.where(rib >= d, pltpu.roll(fw, d, 0), 0.0)
            fw = cmul_add(fw, sh, tab[2 * s:2 * s + 1], tab[2 * s + 1:2 * s + 2])
            o = 2 * n_steps
            sh = jnp.where(rib < n_ch - d, pltpu.roll(bw, rows - d, 0), 0.0)
            bw = cmul_add(bw, sh, tab[o + 2 * s:o + 2 * s + 1], tab[o + 2 * s + 1:o + 2 * s + 2])
        fe = jnp.where(rib >= 1, pltpu.roll(fw, 1, 0), 0.0)
        be = jnp.where(rib < n_ch - 1, pltpu.roll(bw, rows - 1, 0), 0.0)
        fl = jnp.concatenate([fe[b * n_ch + n_ctx:(b + 1) * n_ch] for b in range(bsz)], axis=0)
        bl = jnp.concatenate([be[b * n_ch:b * n_ch + n_lat] for b in range(bsz)], axis=0)
        st = jnp.concatenate([fl, bl], axis=1).astype(BF16)
        y = (jnp.dot(ul, toep_ref[gl], preferred_element_type=F32)
             + lax.dot_general(st, cpow_ref[gl], (((1,), (1,)), ((), ())), preferred_element_type=F32))
        yacc_ref[...] += lax.dot_general(y.astype(BF16), sel, (((1,), (1,)), ((), ())),
                                         preferred_element_type=F32)
        return carry

    lax.fori_loop(0, S5_GROUPS_PER_COL, group, 0)
    for i in range(t):
        y_ref[:, :, i, :] = yacc_ref[:, i * LANE:(i + 1) * LANE].reshape(bsz, n_lat, LANE)


def _s5_mix(u4, u4c, bpw, toep, cpw, tab):
    bsz, n_lat, t, dm = u4.shape
    n_ctx = u4c.shape[1]
    w = t * S5_GROUP
    n_steps = tab.shape[1] // 4
    sel = _s5_select()
    gpc = S5_GROUPS_PER_COL
    wspec = pl.BlockSpec((gpc, w, w), lambda i: (i, 0, 0))
    return pl.pallas_call(
        functools.partial(_s5_kernel, n_steps=n_steps),
        grid=(dm // LANE,),
        in_specs=[pl.BlockSpec((bsz, n_lat, t, LANE), lambda i: (0, 0, 0, i)),
                  pl.BlockSpec((bsz, n_ctx, t, LANE), lambda i: (0, 0, 0, i)),
                  wspec, wspec, wspec,
                  pl.BlockSpec((gpc, 4 * n_steps, tab.shape[2]), lambda i: (i, 0, 0)),
                  pl.BlockSpec(sel.shape, lambda i: (0, 0, 0))],
        out_specs=pl.BlockSpec((bsz, n_lat, t, LANE), lambda i: (0, 0, 0, i)),
        out_shape=jax.ShapeDtypeStruct(u4.shape, F32),
        scratch_shapes=[pltpu.VMEM((bsz * n_lat, t * LANE), BF16), pltpu.VMEM((bsz * n_ctx, t * LANE), BF16),
                        pltpu.VMEM((bsz * n_lat, t * LANE), F32)],
        compiler_params=pltpu.CompilerParams(
            dimension_semantics=("parallel",), vmem_limit_bytes=VMEM_LIMIT_BYTES),
        name="s5_mix",
    )(u4, u4c, bpw, toep, cpw, tab, sel)


def _s5_bidirectional(u, u_ctx, lam_re, lam_im, log_dt, b_re, b_im, c_re, c_im, d_skip):
    bsz, length, dm = u.shape
    ctx_len = u_ctx.shape[1]
    t = S5_CHUNK
    n_lat, n_ctx = length // t, ctx_len // t
    n_steps = max(1, math.ceil(math.log2(n_lat + n_ctx)))
    bpw, toep, cpw, tab = _s5_weights(lam_re, lam_im, log_dt, b_re, b_im, c_re, c_im, d_skip, n_steps)
    y4 = _s5_mix(u.reshape(bsz, n_lat, t, dm), u_ctx.reshape(bsz, n_ctx, t, dm), bpw, toep, cpw, tab)
    return y4.reshape(bsz, length, dm)


def _dft_constants(real_input=False):
    n1 = np.arange(FFT_N1)
    n2 = np.arange(FFT_N2)
    half = FFT_N1 // 2
    th = 2 * np.pi * np.outer(n1, n1) / FFT_N1
    c1, s1 = np.cos(th), np.sin(th)
    if real_input:
        w1 = np.concatenate([c1, -s1], axis=1)
    else:
        w1 = np.concatenate([np.concatenate([c1[:half], -s1[:half]], axis=1),
                             np.concatenate([s1[:half], c1[:half]], axis=1)], axis=0)
    z = np.zeros_like(w1)
    w1p = np.block([[w1, z], [z, w1]])
    ph = 2 * np.pi * np.outer(n2, n1) / FFT_N
    t1 = np.concatenate([np.cos(ph), np.cos(ph)], axis=1)
    t2 = np.concatenate([np.sin(ph), -np.sin(ph)], axis=1)
    ps = 2 * np.pi * np.outer(n2, n2) / FFT_N2
    f2 = np.concatenate([np.cos(ps), -np.sin(ps)], axis=1)
    g2 = np.concatenate([np.cos(ps), np.sin(ps)], axis=1)
    c2, s2 = np.cos(ph).T, np.sin(ph).T
    wi = np.concatenate([np.concatenate([c1[:, :half], s1[:, :half]], axis=1),
                         np.concatenate([-s1[:, :half], c1[:, :half]], axis=1)], axis=0) / FFT_N
    zi = np.zeros_like(wi)
    wi2 = np.stack([np.concatenate([wi, zi], axis=1), np.concatenate([zi, wi], axis=1)])
    as_b = lambda a: jnp.asarray(a, F32).astype(BF16)
    as_f = lambda a: jnp.asarray(a, F32)
    return [as_b(w1p), as_f(t1), as_f(t2), as_b(f2), as_b(g2), as_f(c2), as_f(s2), as_b(wi2)]


def _fwd_spectrum(xp, w1p, t1, t2, f2):
    cp = xp.shape[0]
    hn = FFT_N1
    a = jnp.dot(xp.reshape(cp * FFT_N2, 128).astype(BF16), w1p, preferred_element_type=F32)
    out = []
    for par in range(2):
        ap = a[:, par * 128:(par + 1) * 128]
        ap = ap.reshape(cp, FFT_N2, 128) * t1 + pltpu.roll(ap, hn, 1).reshape(cp, FFT_N2, 128) * t2
        at = jnp.swapaxes(ap, 1, 2)
        p = jnp.dot(at.reshape(cp * 128, FFT_N2).astype(BF16), f2, preferred_element_type=F32)
        p = p.reshape(cp, 128, 2 * FFT_N2)
        out.append((p[:, :hn, :FFT_N2] - p[:, hn:, FFT_N2:], p[:, :hn, FFT_N2:] + p[:, hn:, :FFT_N2]))
    return out


def _inv_time(yre, yim, g2, c2, s2, wi_par):
    cp = yre.shape[0]
    hn = FFT_N1
    y = jnp.concatenate([yre, yim], axis=1).reshape(cp * 128, FFT_N2).astype(BF16)
    q = jnp.dot(y, g2, preferred_element_type=F32).reshape(cp, 128, 2 * FFT_N2)
    bre = q[:, :hn, :FFT_N2] - q[:, hn:, FFT_N2:]
    bim = q[:, :hn, FFT_N2:] + q[:, hn:, :FFT_N2]
    b2 = jnp.concatenate([bre * c2 - bim * s2, bre * s2 + bim * c2], axis=1)
    bt = jnp.swapaxes(b2, 1, 2)
    return jnp.dot(bt.reshape(cp * FFT_N2, 128).astype(BF16), wi_par, preferred_element_type=F32)


def _hyena_kernel(z_ref, g1_ref, g2_ref, kf_ref, bias_ref, w1p_ref, t1_ref, t2_ref, f2_ref, gi_ref, c2_ref,
                  s2_ref, wi_ref, o_ref, zt_ref, g1t_ref, g2t_ref, ot_ref, stage_ref):
    k = pl.program_id(1)
    n_s = z_ref.shape[0] // FFT_N2
    cp = HY_PAIRS_PER_STEP

    def to_tiles(x_ref, t_ref):
        for s in range(n_s):
            xs = x_ref[s * FFT_N2:(s + 1) * FFT_N2, :].astype(F32)
            stage_ref[:, s, :] = xs.T
        st = stage_ref[...].reshape(HY_PAIRS_PER_COL, 2 * HY_TILE_PITCH, FFT_N2)
        both = jnp.concatenate([st[:, :n_s], st[:, HY_TILE_PITCH:HY_TILE_PITCH + n_s]], axis=1)
        t_ref[...] = jnp.swapaxes(both, 1, 2)

    @pl.when(k == 0)
    def _():
        to_tiles(z_ref, zt_ref)
        to_tiles(g1_ref, g1t_ref)
        to_tiles(g2_ref, g2t_ref)

    sl = pl.ds(pl.multiple_of(k * cp, cp), cp)
    z = zt_ref[sl]
    gates = (g1t_ref, g2t_ref)
    for o in range(HY_ORDER):
        spec = _fwd_spectrum(z, w1p_ref[...], t1_ref[...], t2_ref[...], f2_ref[...])
        conv = None
        for par in range(2):
            xre, xim = spec[par]
            kre = kf_ref[o, par, :, :FFT_N1, :]
            kim = kf_ref[o, par, :, FFT_N1:, :]
            part = _inv_time(xre * kre - xim * kim, xre * kim + xim * kre,
                             gi_ref[...], c2_ref[...], s2_ref[...], wi_ref[par])
            conv = part if conv is None else conv + part
        z = gates[o][sl] * (conv.reshape(cp, FFT_N2, LANE) + bias_ref[o] * z)
    ot_ref[sl] = z

    @pl.when(k == pl.num_programs(1) - 1)
    def _():
        back = jnp.swapaxes(ot_ref[...], 1, 2)
        for par in range(2):
            stage_ref[par::2, :n_s, :] = back[:, par * n_s:(par + 1) * n_s, :]
        for s in range(n_s):
            o_ref[s * FFT_N2:(s + 1) * FFT_N2, :] = stage_ref[:, s, :].T.astype(o_ref.dtype)


def _hyena_conv(us, kf, biasp):
    t, _ = us.shape
    consts = _dft_constants()
    ncol = D_HY // LANE
    nsub = HY_PAIRS_PER_COL // HY_PAIRS_PER_STEP
    full = lambda a: pl.BlockSpec(a.shape, lambda j, k: (0,) * a.ndim)
    nat = lambda off: pl.BlockSpec((t, LANE), lambda j, k: (0, off + j))
    tiles = pltpu.VMEM((HY_PAIRS_PER_COL, FFT_N2, LANE), F32)
    return pl.pallas_call(
        _hyena_kernel,
        grid=(ncol, nsub),
        in_specs=[nat(0), nat(ncol), nat(2 * ncol),
                  pl.BlockSpec((HY_ORDER, 2, HY_PAIRS_PER_STEP, 2 * FFT_N1, FFT_N2),
                               lambda j, k: (0, 0, j * nsub + k, 0, 0)),
                  pl.BlockSpec((HY_ORDER, HY_PAIRS_PER_STEP, 1, LANE), lambda j, k: (0, j * nsub + k, 0, 0))]
                 + [full(a) for a in consts],
        out_specs=pl.BlockSpec((t, LANE), lambda j, k: (0, j)),
        out_shape=jax.ShapeDtypeStruct((t, D_HY), us.dtype),
        scratch_shapes=[tiles, tiles, tiles, tiles, pltpu.VMEM((LANE, HY_TILE_PITCH, FFT_N2), F32)],
        compiler_params=pltpu.CompilerParams(
            dimension_semantics=("parallel", "arbitrary"), vmem_limit_bytes=VMEM_LIMIT_BYTES),
        name="hyena_conv",
    )(us, us, us, kf, biasp, *consts)


def _filter_time_kernel(w1t_ref, w1c_ref, w1s_ref, b1_ref, w2_ref, b2_ref, w3_ref, b3_ref, fr_ref,
                        wf_ref, wb_ref, df_ref, db_ref, o_ref, h_ref, k_ref, *, length):
    n_fft = 2 * length
    hp = lax.Precision.HIGHEST

    @pl.when(pl.program_id(0) == 0)
    def _():
        pos = lax.broadcasted_iota(jnp.int32, (1, n_fft), 1)
        lag = jnp.where(pos < length, pos, n_fft - pos).astype(F32)
        t = lag / float(length - 1)
        w = (2.0 * math.pi / length) * lag
        band_step = (HY_BANDS - 1 - 1e-4) / (HY_BANDS - 1)
        bands = 1e-4 + band_step * lax.broadcasted_iota(jnp.int32, (HY_BANDS, 1), 0).astype(F32)
        ang = bands * w
        fr = fr_ref[...]
        h = (w1t_ref[...] * t + jnp.dot(w1c_ref[...], jnp.cos(ang), preferred_element_type=F32, precision=hp)
             - jnp.dot(w1s_ref[...], jnp.sin(ang), preferred_element_type=F32, precision=hp))
        h = jnp.sin(fr * (h + b1_ref[...]))
        h = jnp.sin(fr * (jnp.dot(w2_ref[...], h, preferred_element_type=F32, precision=hp) + b2_ref[...]))
        h = jnp.sin(fr * (jnp.dot(w3_ref[...], h, preferred_element_type=F32, precision=hp) + b3_ref[...]))
        hi = h.astype(BF16)
        h_ref[0] = hi
        h_ref[1] = (h - hi.astype(F32)).astype(BF16)

    def dot3(w, lo, hi_):
        w_hi = w.astype(BF16)
        w_lo = (w - w_hi.astype(F32)).astype(BF16)
        h_hi, h_lo = h_ref[0, :, lo:hi_], h_ref[1, :, lo:hi_]
        return (jnp.dot(w_hi, h_hi, preferred_element_type=F32) + jnp.dot(w_hi, h_lo, preferred_element_type=F32)
                + jnp.dot(w_lo, h_hi, preferred_element_type=F32))

    pos = lax.broadcasted_iota(jnp.int32, (1, length), 1)
    tf = pos.astype(F32) / float(length - 1)
    tb = (length - pos).astype(F32) / float(length - 1)
    kf = dot3(wf_ref[...], 0, length) * jnp.exp(-tf * df_ref[...])
    kb = dot3(wb_ref[...], length, n_fft) * jnp.exp(-tb * db_ref[...])
    kb = jnp.where(pos == 0, 0.0, kb)
    inv = 1.0 / (jnp.sum(jnp.abs(kf), axis=1, keepdims=True) + jnp.sum(jnp.abs(kb), axis=1, keepdims=True))
    k_ref[:, :length] = kf * inv
    k_ref[:, length:] = kb * inv
    for n1 in range(FFT_N1):
        o_ref[:, n1, :] = k_ref[:, n1 * FFT_N2:(n1 + 1) * FFT_N2]
    o_ref[:, FFT_N1:, :] = jnp.zeros((o_ref.shape[0], FILTER_TILE_ROWS - FFT_N1, FFT_N2), F32)


def _filter_time(length, w1, b1, w2, b2, w3, b3, freq, w_out, cb=128):
    col = lambda v: v.reshape(-1, 1)
    w1t = w1.T
    n_ch = w_out.shape[1]
    deltas = jnp.abs(jnp.linspace(math.log(HY_TARGET) / HY_SLOW_PCT, math.log(HY_TARGET) / HY_FAST_PCT,
                                  n_ch, dtype=F32)).reshape(n_ch, 1)
    wot = w_out.T
    nb = D_HY // cb
    small = lambda a: pl.BlockSpec(a.shape, lambda i: (0,) * a.ndim)
    fwd = lambda i: ((i // nb) * HY_DIRS * nb + i % nb, 0)
    bwd = lambda i: ((i // nb) * HY_DIRS * nb + nb + i % nb, 0)
    ins = [w1t[:, 0:1], w1t[:, 1:1 + HY_BANDS], w1t[:, 1 + HY_BANDS:], col(b1), w2.T, col(b2), w3.T, col(b3),
           col(freq)]
    hy_ff = w2.shape[0]
    return pl.pallas_call(
        functools.partial(_filter_time_kernel, length=length),
        grid=(HY_ORDER * nb,),
        in_specs=[small(a) for a in ins] + [pl.BlockSpec((cb, hy_ff), fwd), pl.BlockSpec((cb, hy_ff), bwd),
                                            pl.BlockSpec((cb, 1), fwd), pl.BlockSpec((cb, 1), bwd)],
        out_specs=pl.BlockSpec((cb, FILTER_TILE_ROWS, FFT_N2), lambda i: (i, 0, 0)),
        out_shape=jax.ShapeDtypeStruct((HY_ORDER * D_HY, FILTER_TILE_ROWS, FFT_N2), F32),
        scratch_shapes=[pltpu.VMEM((2, hy_ff, 2 * length), BF16), pltpu.VMEM((cb, 2 * length), F32)],
        compiler_params=pltpu.CompilerParams(
            dimension_semantics=("arbitrary",), vmem_limit_bytes=VMEM_LIMIT_BYTES),
        name="hyena_filter_time",
    )(*ins, wot, wot, deltas, deltas)


def _filter_spec_kernel(k_ref, w1p_ref, t1_ref, t2_ref, f2_ref, o_ref):
    cb = k_ref.shape[0]
    xp = jnp.swapaxes(k_ref[:, :FFT_N1, :].reshape(cb // 2, 2 * FFT_N1, FFT_N2), 1, 2)
    spec = _fwd_spectrum(xp, w1p_ref[...], t1_ref[...], t2_ref[...], f2_ref[...])
    for par in range(2):
        o_ref[par, :, :FFT_N1, :] = spec[par][0]
        o_ref[par, :, FFT_N1:, :] = spec[par][1]


def _filter_spectrum(kt, cb=32):
    consts = _dft_constants(real_input=True)[:4]
    nb = D_HY // cb
    full = lambda a: pl.BlockSpec(a.shape, lambda i: (0,) * a.ndim)
    return pl.pallas_call(
        _filter_spec_kernel,
        grid=(HY_ORDER * nb,),
        in_specs=[pl.BlockSpec((cb, FILTER_TILE_ROWS, FFT_N2), lambda i: (i, 0, 0))] + [full(a) for a in consts],
        out_specs=pl.BlockSpec((None, 2, cb // 2, 2 * FFT_N1, FFT_N2), lambda i: (i // nb, 0, i % nb, 0, 0)),
        out_shape=jax.ShapeDtypeStruct((HY_ORDER, 2, D_HY // 2, 2 * FFT_N1, FFT_N2), F32),
        compiler_params=pltpu.CompilerParams(
            dimension_semantics=("parallel",), vmem_limit_bytes=VMEM_LIMIT_BYTES),
        name="hyena_filter_spectrum",
    )(kt, *consts)


def _hyena(us, w1, b1, w2, b2, w3, b3, freq, w_out, bias):
    bsz, length, _ = us.shape
    assert 2 * length == FFT_N and bsz == 2, "one complex transform carries exactly two batch rows"
    kf = _filter_spectrum(_filter_time(length, w1, b1, w2, b2, w3, b3, freq, w_out))
    biasp = jnp.repeat(bias.reshape(HY_ORDER, D_HY // 2, 1, 2), FFT_N1, axis=-1)
    return _hyena_conv(us.reshape(bsz * length, -1), kf, biasp).reshape(bsz, length, D_HY)


def kernel(x, c, ctx, c_ctx, w_ada, b_ada, norm_g, ffn_w_gate, ffn_w_up, ffn_w_down, w_in,
           s5_lam_re, s5_lam_im, s5_log_dt, s5_b_re, s5_b_im, s5_c_re, s5_c_im, s5_d,
           hy_short_w, hy_short_b, hy_w1, hy_b1, hy_w2, hy_b2, hy_w3, hy_b3, hy_freq, hy_w_out,
           hy_bias, w_pa, w_pb, w_out, final_g):
    bsz, seq, d = x.shape
    ctx_len = ctx.shape[1]
    n_rows = seq // GRID_W
    depth = w_ada.shape[0]
    assert depth == 1, "context-token outputs are only dropped by the last layer"
    l = 0

    c_rows = jnp.concatenate([c, c_ctx[None, :], jnp.zeros((8 - bsz - 1, d), F32)], axis=0)
    mod_all = _ada_mod(c_rows, w_ada[l], b_ada[l])
    mod = mod_all[:bsz].reshape(bsz, N_SUB, N_MOD, 1, d)
    mod_c = mod_all[bsz:bsz + 1].reshape(1, N_SUB, N_MOD, 1, d)

    def mods(m, sub):
        return tuple(m[:, sub, k] for k in range(N_MOD))

    wg, wu, wd = ffn_w_gate[l], ffn_w_up[l], ffn_w_down[l]
    w_in_b = w_in[l]

    xt = x.reshape(bsz * seq, d)
    ct = ctx.reshape(bsz * ctx_len, d)

    xt = _ffn_sublayer(xt, mods(mod, 0), norm_g[l, 0], wg, wu, wd, 0)
    ct = _ffn_sublayer(ct, mods(mod_c, 0), norm_g[l, 0], wg, wu, wd, 0)

    assert GRID_W * n_rows == seq
    u_s5, us_hy, sig_gates = _in_proj(xt, mod[:, 1, 0], mod[:, 1, 1], norm_g[l, 1], w_in_b,
                                      hy_short_w[l], hy_short_b[l], n_u=I_HY, n_hy=I_GA - I_HY)
    (u_ctx,) = _in_proj(ct, mod_c[:, 1, 0], mod_c[:, 1, 1], norm_g[l, 1], w_in_b[:, :D_S5])

    y_s5 = _s5_bidirectional(u_s5.reshape(bsz, seq, D_S5), u_ctx.reshape(bsz, ctx_len, D_S5),
                             s5_lam_re[l], s5_lam_im[l], s5_log_dt[l],
                             s5_b_re[l], s5_b_im[l], s5_c_re[l], s5_c_im[l], s5_d[l])
    y_hy = _hyena(us_hy.reshape(bsz, seq, I_GA - I_HY),
                  hy_w1[l], hy_b1[l], hy_w2[l], hy_b2[l], hy_w3[l], hy_b3[l], hy_freq[l],
                  hy_w_out[l], hy_bias[l])

    xt = _merge(xt, mod[:, 1, 2], y_s5.reshape(bsz * seq, D_S5), y_hy.reshape(bsz * seq, D_HY),
                sig_gates, w_pa[l].astype(BF16), w_pb[l].astype(BF16), w_out[l].astype(BF16))

    xt = _ffn_sublayer(xt, mods(mod, 2), norm_g[l, 2], wg, wu, wd, 1, final_gain=final_g)
    return xt.reshape(bsz, seq, d)
```

```python
import functools
import math

import jax
import jax.numpy as jnp
import numpy as np
from jax import lax
from jax.experimental import pallas as pl
from jax.experimental.pallas import tpu as pltpu

F32 = jnp.float32
BF16 = jnp.bfloat16

D_MODEL = 2048
GRID_W = 64
D_S5 = 1024
S5_GROUP = 16
S5_GROUPS = D_S5 // S5_GROUP
S5_STATE = 64
S5_DIRS = 2
LAMBDA_RE_MAX = -1e-4
S5_CHUNK = 16
LANE = 128
S5_GROUPS_PER_COL = LANE // S5_GROUP
D_HY = 1024
HY_ORDER = 2
HY_DIRS = 2
HY_SHORT = 3
HY_EMB = 33
HY_BANDS = (HY_EMB - 1) // 2
HY_TARGET = 1e-2
HY_FAST_PCT = 0.3
HY_SLOW_PCT = 1.5
FFT_N1 = 64
FFT_N2 = 128
FFT_N = FFT_N1 * FFT_N2
FILTER_TILE_ROWS = FFT_N1 + 8
HY_TILE_PITCH = FILTER_TILE_ROWS
HY_PAIRS_PER_COL = LANE // 2
HY_PAIRS_PER_STEP = 8
I_HY = D_S5
I_GA = D_S5 + (HY_ORDER + 1) * D_HY
I_GB = I_GA + D_MODEL
D_IN = I_GB + D_MODEL
D_FF = 5632
N_SUB = 3
N_MOD = 3
HALF_STEP = 0.5
RMS_EPS = 1e-6

VMEM_LIMIT_BYTES = 58 * 1024 * 1024


def _rms_mod(x, gain, shift, scale):
    ms = jnp.mean(x * x, axis=-1, keepdims=True)
    y = x * lax.rsqrt(ms + RMS_EPS) * gain
    return y * (1.0 + scale) + shift


def _ada_kernel(c_ref, w_ref, b_ref, o_ref):
    c = c_ref[...]
    a = c * jax.nn.sigmoid(c)
    o_ref[...] = jnp.dot(a, w_ref[...], preferred_element_type=F32,
                         precision=lax.Precision.HIGHEST) + b_ref[...]


def _ada_mod(c_rows, w, b, tn=1024):
    rows, d = c_rows.shape
    n = w.shape[1]
    return pl.pallas_call(
        _ada_kernel,
        grid=(n // tn,),
        in_specs=[pl.BlockSpec((rows, d), lambda j: (0, 0)),
                  pl.BlockSpec((d, tn), lambda j: (0, j)),
                  pl.BlockSpec((1, tn), lambda j: (0, j))],
        out_specs=pl.BlockSpec((rows, tn), lambda j: (0, j)),
        out_shape=jax.ShapeDtypeStruct((rows, n), F32),
        compiler_params=pltpu.CompilerParams(
            dimension_semantics=("arbitrary",), vmem_limit_bytes=VMEM_LIMIT_BYTES),
        name="ada_mod",
    )(c_rows, w, b.reshape(1, n))


def _ffn_kernel(x_ref, shift_ref, scale_ref, gate_ref, gain_ref, wg_ref, wu_ref, wd_ref,
                fg_ref, o_ref, h_ref, *, final_norm):
    j = pl.program_id(1)

    @pl.when(j == 0)
    def _():
        h_ref[...] = _rms_mod(x_ref[...], gain_ref[...], shift_ref[...], scale_ref[...]).astype(BF16)
        o_ref[...] = jnp.zeros_like(o_ref)

    h = h_ref[...]
    g = jnp.dot(h, wg_ref[...].astype(BF16), preferred_element_type=F32)
    u = jnp.dot(h, wu_ref[...].astype(BF16), preferred_element_type=F32)
    a = (g * jax.nn.sigmoid(g) * u).astype(BF16)
    o_ref[...] += jnp.dot(a, wd_ref[...].astype(BF16), preferred_element_type=F32)

    @pl.when(j == pl.num_programs(1) - 1)
    def _():
        y = x_ref[...] + (HALF_STEP * gate_ref[...]) * o_ref[...]
        if final_norm:
            ms = jnp.mean(y * y, axis=-1, keepdims=True)
            y = y * lax.rsqrt(ms + RMS_EPS) * fg_ref[...]
        o_ref[...] = y


def _ffn_sublayer(x, mods, gain, wg, wu, wd, which, final_gain=None, tm=1024, tf=256):
    t, d = x.shape
    bm = mods[0].shape[0]
    tm = min(tm, t // bm)
    blocks_per_batch = (t // bm) // tm
    dff = wg.shape[2]
    final_norm = final_gain is not None
    fg = final_gain if final_norm else gain
    mod_spec = pl.BlockSpec((None, 1, d), lambda i, j: (i // blocks_per_batch, 0, 0))
    vec_spec = pl.BlockSpec((1, d), lambda i, j: (0, 0))
    return pl.pallas_call(
        functools.partial(_ffn_kernel, final_norm=final_norm),
        grid=(t // tm, dff // tf),
        in_specs=[pl.BlockSpec((tm, d), lambda i, j: (i, 0)),
                  mod_spec, mod_spec, mod_spec, vec_spec,
                  pl.BlockSpec((None, d, tf), lambda i, j: (which, 0, j)),
                  pl.BlockSpec((None, d, tf), lambda i, j: (which, 0, j)),
                  pl.BlockSpec((None, tf, d), lambda i, j: (which, j, 0)),
                  vec_spec],
        out_specs=pl.BlockSpec((tm, d), lambda i, j: (i, 0)),
        out_shape=jax.ShapeDtypeStruct((t, d), F32),
        scratch_shapes=[pltpu.VMEM((tm, d), BF16)],
        compiler_params=pltpu.CompilerParams(
            dimension_semantics=("parallel", "arbitrary"), vmem_limit_bytes=VMEM_LIMIT_BYTES),
        name="ffn_final" if final_norm else "ffn",
    )(x, *mods, gain.reshape(1, d), wg, wu, wd, fg.reshape(1, d))


def _proj_kernel(x_ref, shift_ref, scale_ref, gain_ref, w_ref, sw_ref, sb_ref, *rest, n_u, n_hy, row_len, part):
    o_refs, h_ref = rest[:-1], rest[-1]
    j = pl.program_id(1)

    @pl.when(j == 0)
    def _():
        h_ref[...] = _rms_mod(x_ref[...], gain_ref[...], shift_ref[...], scale_ref[...]).astype(BF16)

    tm, tn = h_ref.shape[0], w_ref.shape[1]

    def in_parts(o_ref, epilogue):
        for c in range(0, tn, part):
            p = jnp.dot(h_ref[...], w_ref[:, c:c + part].astype(BF16), preferred_element_type=F32)
            o_ref[:, c:c + part] = epilogue(p, c).astype(o_ref.dtype)

    def short_conv(p, c):
        col = lax.broadcasted_iota(jnp.int32, p.shape, 0) % row_len
        prev = jnp.where(col == 0, 0.0, pltpu.roll(p, 1, 0))
        nxt = jnp.where(col == row_len - 1, 0.0, pltpu.roll(p, tm - 1, 0))
        sw = sw_ref[:, c:c + part]
        return sb_ref[:, c:c + part] + prev * sw[0:1] + p * sw[1:2] + nxt * sw[2:3]

    if n_hy == 0:
        in_parts(o_refs[0], lambda p, c: p)
        return

    @pl.when(j < n_u)
    def _():
        in_parts(o_refs[0], lambda p, c: p)

    @pl.when((j >= n_u) & (j < n_u + n_hy))
    def _():
        in_parts(o_refs[1], short_conv)

    @pl.when(j >= n_u + n_hy)
    def _():
        in_parts(o_refs[2], lambda p, c: jax.nn.sigmoid(p))


def _in_proj(x, shift, scale, gain, w, short_w=None, short_b=None, n_u=D_S5, n_hy=0, row_len=GRID_W,
             tm=1024, tn=512, part=256):
    t, d = x.shape
    bm = shift.shape[0]
    tm = min(tm, t // bm)
    blocks_per_batch = (t // bm) // tm
    n = w.shape[1]
    assert tm % row_len == 0 and (t // bm) % tm == 0
    bu, bh = n_u // tn, n_hy // tn
    bg = n // tn - bu - bh
    mod_spec = pl.BlockSpec((None, 1, d), lambda i, j: (i // blocks_per_batch, 0, 0))
    out_shape = [jax.ShapeDtypeStruct((t, n_u), F32)]
    out_specs = [pl.BlockSpec((tm, tn), lambda i, j: (i, jnp.minimum(j, bu - 1)))]
    if bh:
        out_shape += [jax.ShapeDtypeStruct((t, n_hy), BF16), jax.ShapeDtypeStruct((t, bg * tn), BF16)]
        out_specs += [pl.BlockSpec((tm, tn), lambda i, j: (i, jnp.clip(j - bu, 0, bh - 1))),
                      pl.BlockSpec((tm, tn), lambda i, j: (i, jnp.maximum(j - bu - bh, 0)))]
        sw, sb = short_w, short_b.reshape(1, n_hy)
        hy_blk = lambda i, j: (0, jnp.clip(j - bu, 0, bh - 1))
    else:
        sw, sb = jnp.zeros((HY_SHORT, tn), F32), jnp.zeros((1, tn), F32)
        hy_blk = lambda i, j: (0, 0)
    return pl.pallas_call(
        functools.partial(_proj_kernel, n_u=bu, n_hy=bh, row_len=row_len, part=part),
        grid=(t // tm, n // tn),
        in_specs=[pl.BlockSpec((tm, d), lambda i, j: (i, 0)),
                  mod_spec, mod_spec,
                  pl.BlockSpec((1, d), lambda i, j: (0, 0)),
                  pl.BlockSpec((d, tn), lambda i, j: (0, j)),
                  pl.BlockSpec((HY_SHORT, tn), hy_blk),
                  pl.BlockSpec((1, tn), hy_blk)],
        out_specs=out_specs,
        out_shape=out_shape,
        scratch_shapes=[pltpu.VMEM((tm, d), BF16)],
        compiler_params=pltpu.CompilerParams(
            dimension_semantics=("parallel", "arbitrary"), vmem_limit_bytes=VMEM_LIMIT_BYTES),
        name="in_proj",
    )(x, shift, scale, gain.reshape(1, d), w, sw, sb)


def _gelu_tanh(x):
    return 0.5 * x * (1.0 + jnp.tanh(math.sqrt(2.0 / math.pi) * (x + 0.044715 * (x * x * x))))


def _merge_kernel(x_ref, gate_ref, ys_ref, yh_ref, ga_ref, gb_ref, wpa_lo_ref, wpa_hi_ref, wpb_ref,
                  wout_ref, o_ref, s_ref):
    j = pl.program_id(1)

    @pl.when(j == 0)
    def _():
        s_ref[...] = _gelu_tanh(ys_ref[...].astype(F32)).astype(BF16)
        o_ref[...] = jnp.zeros_like(o_ref)

    s = s_ref[...]
    pa_lo = jnp.dot(s, wpa_lo_ref[...], preferred_element_type=F32)
    pa_hi = jnp.dot(s, wpa_hi_ref[...], preferred_element_type=F32)
    y_a = pa_lo * jax.nn.sigmoid(pa_hi)
    y_b = jnp.dot(yh_ref[...], wpb_ref[...], preferred_element_type=F32)
    m = ga_ref[...].astype(F32) * y_a + gb_ref[...].astype(F32) * y_b
    o_ref[...] += jnp.dot(m.astype(BF16), wout_ref[...], preferred_element_type=F32)

    @pl.when(j == pl.num_programs(1) - 1)
    def _():
        o_ref[...] = x_ref[...] + gate_ref[...] * o_ref[...]


def _merge(x, gate, y_s5, y_hy, sig_gates, w_pa, w_pb, w_out, tm=1024, tn=256):
    t, d = x.shape
    bm = gate.shape[0]
    blocks_per_batch = (t // bm) // tm
    nj = d // tn
    ds5 = y_s5.shape[1]
    dhy = y_hy.shape[1]
    return pl.pallas_call(
        _merge_kernel,
        grid=(t // tm, nj),
        in_specs=[pl.BlockSpec((tm, d), lambda i, j: (i, 0)),
                  pl.BlockSpec((None, 1, d), lambda i, j: (i // blocks_per_batch, 0, 0)),
                  pl.BlockSpec((tm, ds5), lambda i, j: (i, 0)),
                  pl.BlockSpec((tm, dhy), lambda i, j: (i, 0)),
                  pl.BlockSpec((tm, tn), lambda i, j: (i, j)),
                  pl.BlockSpec((tm, tn), lambda i, j: (i, nj + j)),
                  pl.BlockSpec((ds5, tn), lambda i, j: (0, j)),
                  pl.BlockSpec((ds5, tn), lambda i, j: (0, nj + j)),
                  pl.BlockSpec((dhy, tn), lambda i, j: (0, j)),
                  pl.BlockSpec((tn, d), lambda i, j: (j, 0))],
        out_specs=pl.BlockSpec((tm, d), lambda i, j: (i, 0)),
        out_shape=jax.ShapeDtypeStruct((t, d), F32),
        scratch_shapes=[pltpu.VMEM((tm, ds5), BF16)],
        compiler_params=pltpu.CompilerParams(
            dimension_semantics=("parallel", "arbitrary"), vmem_limit_bytes=VMEM_LIMIT_BYTES),
        name="merge",
    )(x, gate, y_s5, y_hy, sig_gates, sig_gates, w_pa, w_pa, w_pb, w_out)


def _s5_weights_kernel(par_ref, bt_ref, c_ref, d_ref, bpow_ref, toep_ref, cpow_ref, tab_ref, ca_ref, *, n_steps):
    t, h, p = S5_CHUNK, S5_GROUP, S5_STATE
    lanes = 2 * p
    hp = lax.Precision.HIGHEST
    sgn = jnp.where(lax.broadcasted_iota(jnp.int32, (1, lanes), 1) < p, -1.0, 1.0)
    par = par_ref[...]
    gsum = None
    for d in range(S5_DIRS):
        lr = jnp.minimum(par[3 * d:3 * d + 1], LAMBDA_RE_MAX)
        li = par[3 * d + 1:3 * d + 2]
        dt = jnp.exp(par[3 * d + 2:3 * d + 3])
        zr, zi = lr * dt, li * dt

        def apow(j):
            mag = jnp.exp(j * zr)
            return mag * jnp.cos(j * zi), sgn * (mag * jnp.sin(j * zi))

        def cmul(x, a1, a2):
            return x * a1 + pltpu.roll(x, p, 1) * a2

        a1, a2 = apow(lax.broadcasted_iota(jnp.int32, (t + 1, 1), 0).astype(F32))
        nr, ni = a1[1:2] - 1.0, sgn * a2[1:2]
        den = lr * lr + li * li
        f_re = (nr * lr + ni * li) / den
        f_im = (ni * lr - nr * li) / den
        bbar = cmul(bt_ref[d], f_re, sgn * f_im)
        cc = c_ref[d]
        ca = [cmul(cc, a1[j:j + 1], a2[j:j + 1]) * (-sgn) for j in range(t + 1)]
        ca_ref[...] = jnp.zeros_like(ca_ref)
        for k in range(t):
            e_b, e_c = (t - 1 - k, k + 1) if d == 0 else (k, t - k)
            bpow_ref[k * h:(k + 1) * h, d * lanes:(d + 1) * lanes] = (
                cmul(bbar, a1[e_b:e_b + 1], a2[e_b:e_b + 1]).astype(BF16))
            cpow_ref[k * h:(k + 1) * h, d * lanes:(d + 1) * lanes] = ca[e_c].astype(BF16)
            l = t - 1 + k if d == 0 else t - 1 - k
            ca_ref[l * h:(l + 1) * h, :] = ca[k]
        g = lax.dot_general(bbar, ca_ref[...], (((1,), (1,)), ((), ())), preferred_element_type=F32, precision=hp)
        gsum = g if gsum is None else gsum + g
        for s in range(n_steps):
            s1, s2 = apow(float(t * 2 ** s))
            r = d * 2 * n_steps + 2 * s
            tab_ref[r:r + 1, :] = s1
            tab_ref[r + 1:r + 2, :] = s2
    wide = gsum.shape[1]
    col = lax.broadcasted_iota(jnp.int32, (h, wide), 1)
    row = lax.broadcasted_iota(jnp.int32, (h, wide), 0)
    gsum = gsum + jnp.where(col - (t - 1) * h == row, d_ref[...], 0.0)
    for k in range(t):
        off = (t - 1 - k) * h
        shifted = gsum if off == 0 else pltpu.roll(gsum, wide - off, 1)
        toep_ref[k * h:(k + 1) * h, :] = shifted[:, :t * h].astype(BF16)


def _s5_weights(lam_re, lam_im, log_dt, b_re, b_im, c_re, c_im, d_skip, n_steps):
    g, p, h, t = S5_GROUPS, S5_STATE, S5_GROUP, S5_CHUNK
    cat2 = lambda a: jnp.concatenate([a, a], axis=-1)
    par = jnp.stack([cat2(lam_re), cat2(lam_im), jnp.broadcast_to(log_dt[..., None], (S5_DIRS, g, 2 * p))], axis=1)
    par = par.transpose(2, 0, 1, 3).reshape(g, 3 * S5_DIRS, 2 * p)
    btc = jnp.concatenate([b_re, b_im], axis=2).transpose(1, 0, 3, 2)
    ccat = jnp.concatenate([c_re, c_im], axis=3).transpose(1, 0, 2, 3)
    wide = 2 * t * h
    drow = jnp.zeros((g, 1, wide), F32).at[:, 0, (t - 1) * h:t * h].set(d_skip.reshape(g, h))
    w = t * h
    sq = pl.BlockSpec((None, w, w), lambda i: (i, 0, 0))
    return pl.pallas_call(
        functools.partial(_s5_weights_kernel, n_steps=n_steps),
        grid=(g,),
        in_specs=[pl.BlockSpec((None, 3 * S5_DIRS, 2 * p), lambda i: (i, 0, 0)),
                  pl.BlockSpec((None, S5_DIRS, h, 2 * p), lambda i: (i, 0, 0, 0)),
                  pl.BlockSpec((None, S5_DIRS, h, 2 * p), lambda i: (i, 0, 0, 0)),
                  pl.BlockSpec((None, 1, wide), lambda i: (i, 0, 0))],
        out_specs=[sq, sq, sq, pl.BlockSpec((None, 4 * n_steps, 2 * p), lambda i: (i, 0, 0))],
        out_shape=[jax.ShapeDtypeStruct((g, w, w), BF16)] * 3 + [jax.ShapeDtypeStruct((g, 4 * n_steps, 2 * p), F32)],
        scratch_shapes=[pltpu.VMEM((wide, 2 * p), F32)],
        compiler_params=pltpu.CompilerParams(
            dimension_semantics=("parallel",), vmem_limit_bytes=VMEM_LIMIT_BYTES),
        name="s5_weights",
    )(par, btc, ccat, drow)


def _s5_select():
    t, h = S5_CHUNK, S5_GROUP
    sel = np.zeros((S5_GROUPS_PER_COL, t, LANE, t, h), np.float32)
    for gl in range(S5_GROUPS_PER_COL):
        for k in range(t):
            sel[gl, k, gl * h:(gl + 1) * h, k, :] = np.eye(h)
    return jnp.asarray(sel.reshape(S5_GROUPS_PER_COL, t * LANE, t * h)).astype(BF16)


def _s5_kernel(u_ref, uc_ref, bpow_ref, toep_ref, cpow_ref, tab_ref, sel_ref, y_ref, ucat_ref, ucc_ref, yacc_ref,
               *, n_steps):
    bsz, n_lat, t, _ = u_ref.shape
    n_ctx = uc_ref.shape[1]
    n_ch = n_lat + n_ctx
    rows = bsz * n_ch
    half = 2 * S5_STATE
    for k in range(t):
        yacc_ref[:, k * LANE:(k + 1) * LANE] = u_ref[:, :, k, :].reshape(bsz * n_lat, LANE)
        ucc_ref[:, k * LANE:(k + 1) * LANE] = uc_ref[:, :, k, :].reshape(bsz * n_ctx, LANE).astype(BF16)
    ucat_ref[...] = yacc_ref[...].astype(BF16)
    yacc_ref[...] = jnp.zeros_like(yacc_ref)
    rib = lax.broadcasted_iota(jnp.int32, (rows, half), 0) % n_ch

    def cmul_add(acc, sh, a1, a2):
        return acc + a1 * sh + a2 * pltpu.roll(sh, S5_STATE, 1)

    def group(gl, carry):
        sel = sel_ref[gl]
        ul = jnp.dot(ucat_ref[...], sel, preferred_element_type=F32).astype(BF16)
        uc = jnp.dot(ucc_ref[...], sel, preferred_element_type=F32).astype(BF16)
        bpow = bpow_ref[gl]
        zl = jnp.dot(ul, bpow, preferred_element_type=F32)
        zc = jnp.dot(uc, bpow, preferred_element_type=F32)
        fparts, bparts = [], []
        for b in range(bsz):
            lat = slice(b * n_lat, (b + 1) * n_lat)
            ctx = slice(b * n_ctx, (b + 1) * n_ctx)
            fparts += [zc[ctx, :half], zl[lat, :half]]
            bparts += [zl[lat, half:], zc[ctx, half:]]
        fw = jnp.concatenate(fparts, axis=0)
        bw = jnp.concatenate(bparts, axis=0)
        tab = tab_ref[gl]
        for s in range(n_steps):
            d = 1 << s
            sh = jnp.where(rib >= d, pltpu.roll(fw, d, 0), 0.0)
            fw = cmul_add(fw, sh, tab[2 * s:2 * s + 1], tab[2 * s + 1:2 * s + 2])
            o = 2 * n_steps
            sh = jnp.where(rib < n_ch - d, pltpu.roll(bw, rows - d, 0), 0.0)
            bw = cmul_add(bw, sh, tab[o + 2 * s:o + 2 * s + 1], tab[o + 2 * s + 1:o + 2 * s + 2])
        fe = jnp.where(rib >= 1, pltpu.roll(fw, 1, 0), 0.0)
        be = jnp.where(rib < n_ch - 1, pltpu.roll(bw, rows - 1, 0), 0.0)
        fl = jnp.concatenate([fe[b * n_ch + n_ctx:(b + 1) * n_ch] for b in range(bsz)], axis=0)
        bl = jnp.concatenate([be[b * n_ch:b * n_ch + n_lat] for b in range(bsz)], axis=0)
        st = jnp.concatenate([fl, bl], axis=1).astype(BF16)
        y = (jnp.dot(ul, toep_ref[gl], preferred_element_type=F32)
             + lax.dot_general(st, cpow_ref[gl], (((1,), (1,)), ((), ())), preferred_element_type=F32))
        yacc_ref[...] += lax.dot_general(y.astype(BF16), sel, (((1,), (1,)), ((), ())),
                                         preferred_element_type=F32)
        return carry

    lax.fori_loop(0, S5_GROUPS_PER_COL, group, 0)
    for i in range(t):
        y_ref[:, :, i, :] = yacc_ref[:, i * LANE:(i + 1) * LANE].reshape(bsz, n_lat, LANE)


def _s5_mix(u4, u4c, bpw, toep, cpw, tab):
    bsz, n_lat, t, dm = u4.shape
    n_ctx = u4c.shape[1]
    w = t * S5_GROUP
    n_steps = tab.shape[1] // 4
    sel = _s5_select()
    gpc = S5_GROUPS_PER_COL
    wspec = pl.BlockSpec((gpc, w, w), lambda i: (i, 0, 0))
    return pl.pallas_call(
        functools.partial(_s5_kernel, n_steps=n_steps),
        grid=(dm // LANE,),
        in_specs=[pl.BlockSpec((bsz, n_lat, t, LANE), lambda i: (0, 0, 0, i)),
                  pl.BlockSpec((bsz, n_ctx, t, LANE), lambda i: (0, 0, 0, i)),
                  wspec, wspec, wspec,
                  pl.BlockSpec((gpc, 4 * n_steps, tab.shape[2]), lambda i: (i, 0, 0)),
                  pl.BlockSpec(sel.shape, lambda i: (0, 0, 0))],
        out_specs=pl.BlockSpec((bsz, n_lat, t, LANE), lambda i: (0, 0, 0, i)),
        out_shape=jax.ShapeDtypeStruct(u4.shape, F32),
        scratch_shapes=[pltpu.VMEM((bsz * n_lat, t * LANE), BF16), pltpu.VMEM((bsz * n_ctx, t * LANE), BF16),
                        pltpu.VMEM((bsz * n_lat, t * LANE), F32)],
        compiler_params=pltpu.CompilerParams(
            dimension_semantics=("parallel",), vmem_limit_bytes=VMEM_LIMIT_BYTES),
        name="s5_mix",
    )(u4, u4c, bpw, toep, cpw, tab, sel)


def _s5_bidirectional(u, u_ctx, lam_re, lam_im, log_dt, b_re, b_im, c_re, c_im, d_skip):
    bsz, length, dm = u.shape
    ctx_len = u_ctx.shape[1]
    t = S5_CHUNK
    n_lat, n_ctx = length // t, ctx_len // t
    n_steps = max(1, math.ceil(math.log2(n_lat + n_ctx)))
    bpw, toep, cpw, tab = _s5_weights(lam_re, lam_im, log_dt, b_re, b_im, c_re, c_im, d_skip, n_steps)
    y4 = _s5_mix(u.reshape(bsz, n_lat, t, dm), u_ctx.reshape(bsz, n_ctx, t, dm), bpw, toep, cpw, tab)
    return y4.reshape(bsz, length, dm)


def _dft_constants(real_input=False):
    n1 = np.arange(FFT_N1)
    n2 = np.arange(FFT_N2)
    half = FFT_N1 // 2
    th = 2 * np.pi * np.outer(n1, n1) / FFT_N1
    c1, s1 = np.cos(th), np.sin(th)
    if real_input:
        w1 = np.concatenate([c1, -s1], axis=1)
    else:
        w1 = np.concatenate([np.concatenate([c1[:half], -s1[:half]], axis=1),
                             np.concatenate([s1[:half], c1[:half]], axis=1)], axis=0)
    z = np.zeros_like(w1)
    w1p = np.block([[w1, z], [z, w1]])
    ph = 2 * np.pi * np.outer(n2, n1) / FFT_N
    t1 = np.concatenate([np.cos(ph), np.cos(ph)], axis=1)
    t2 = np.concatenate([np.sin(ph), -np.sin(ph)], axis=1)
    ps = 2 * np.pi * np.outer(n2, n2) / FFT_N2
    f2 = np.concatenate([np.cos(ps), -np.sin(ps)], axis=1)
    g2 = np.concatenate([np.cos(ps), np.sin(ps)], axis=1)
    c2, s2 = np.cos(ph).T, np.sin(ph).T
    wi = np.concatenate([np.concatenate([c1[:, :half], s1[:, :half]], axis=1),
                         np.concatenate([-s1[:, :half], c1[:, :half]], axis=1)], axis=0) / FFT_N
    zi = np.zeros_like(wi)
    wi2 = np.stack([np.concatenate([wi, zi], axis=1), np.concatenate([zi, wi], axis=1)])
    as_b = lambda a: jnp.asarray(a, F32).astype(BF16)
    as_f = lambda a: jnp.asarray(a, F32)
    return [as_b(w1p), as_f(t1), as_f(t2), as_b(f2), as_b(g2), as_f(c2), as_f(s2), as_b(wi2)]


def _fwd_spectrum(xp, w1p, t1, t2, f2):
    cp = xp.shape[0]
    hn = FFT_N1
    a = jnp.dot(xp.reshape(cp * FFT_N2, 128).astype(BF16), w1p, preferred_element_type=F32)
    out = []
    for par in range(2):
        ap = a[:, par * 128:(par + 1) * 128]
        ap = ap.reshape(cp, FFT_N2, 128) * t1 + pltpu.roll(ap, hn, 1).reshape(cp, FFT_N2, 128) * t2
        at = jnp.swapaxes(ap, 1, 2)
        p = jnp.dot(at.reshape(cp * 128, FFT_N2).astype(BF16), f2, preferred_element_type=F32)
        p = p.reshape(cp, 128, 2 * FFT_N2)
        out.append((p[:, :hn, :FFT_N2] - p[:, hn:, FFT_N2:], p[:, :hn, FFT_N2:] + p[:, hn:, :FFT_N2]))
    return out


def _inv_time(yre, yim, g2, c2, s2, wi_par):
    cp = yre.shape[0]
    hn = FFT_N1
    y = jnp.concatenate([yre, yim], axis=1).reshape(cp * 128, FFT_N2).astype(BF16)
    q = jnp.dot(y, g2, preferred_element_type=F32).reshape(cp, 128, 2 * FFT_N2)
    bre = q[:, :hn, :FFT_N2] - q[:, hn:, FFT_N2:]
    bim = q[:, :hn, FFT_N2:] + q[:, hn:, :FFT_N2]
    b2 = jnp.concatenate([bre * c2 - bim * s2, bre * s2 + bim * c2], axis=1)
    bt = jnp.swapaxes(b2, 1, 2)
    return jnp.dot(bt.reshape(cp * FFT_N2, 128).astype(BF16), wi_par, preferred_element_type=F32)


def _hyena_kernel(z_ref, g1_ref, g2_ref, kf_ref, bias_ref, w1p_ref, t1_ref, t2_ref, f2_ref, gi_ref, c2_ref,
                  s2_ref, wi_ref, o_ref, zt_ref, g1t_ref, g2t_ref, ot_ref, stage_ref):
    k = pl.program_id(1)
    n_s = z_ref.shape[0] // FFT_N2
    cp = HY_PAIRS_PER_STEP

    def to_tiles(x_ref, t_ref):
        for s in range(n_s):
            xs = x_ref[s * FFT_N2:(s + 1) * FFT_N2, :].astype(F32)
            stage_ref[:, s, :] = xs.T
        st = stage_ref[...].reshape(HY_PAIRS_PER_COL, 2 * HY_TILE_PITCH, FFT_N2)
        both = jnp.concatenate([st[:, :n_s], st[:, HY_TILE_PITCH:HY_TILE_PITCH + n_s]], axis=1)
        t_ref[...] = jnp.swapaxes(both, 1, 2)

    @pl.when(k == 0)
    def _():
        to_tiles(z_ref, zt_ref)
        to_tiles(g1_ref, g1t_ref)
        to_tiles(g2_ref, g2t_ref)

    sl = pl.ds(pl.multiple_of(k * cp, cp), cp)
    z = zt_ref[sl]
    gates = (g1t_ref, g2t_ref)
    for o in range(HY_ORDER):
        spec = _fwd_spectrum(z, w1p_ref[...], t1_ref[...], t2_ref[...], f2_ref[...])
        conv = None
        for par in range(2):
            xre, xim = spec[par]
            kre = kf_ref[o, par, :, :FFT_N1, :]
            kim = kf_ref[o, par, :, FFT_N1:, :]
            part = _inv_time(xre * kre - xim * kim, xre * kim + xim * kre,
                             gi_ref[...], c2_ref[...], s2_ref[...], wi_ref[par])
            conv = part if conv is None else conv + part
        z = gates[o][sl] * (conv.reshape(cp, FFT_N2, LANE) + bias_ref[o] * z)
    ot_ref[sl] = z

    @pl.when(k == pl.num_programs(1) - 1)
    def _():
        back = jnp.swapaxes(ot_ref[...], 1, 2)
        for par in range(2):
            stage_ref[par::2, :n_s, :] = back[:, par * n_s:(par + 1) * n_s, :]
        for s in range(n_s):
            o_ref[s * FFT_N2:(s + 1) * FFT_N2, :] = stage_ref[:, s, :].T.astype(o_ref.dtype)


def _hyena_conv(us, kf, biasp):
    t, _ = us.shape
    consts = _dft_constants()
    ncol = D_HY // LANE
    nsub = HY_PAIRS_PER_COL // HY_PAIRS_PER_STEP
    full = lambda a: pl.BlockSpec(a.shape, lambda j, k: (0,) * a.ndim)
    nat = lambda off: pl.BlockSpec((t, LANE), lambda j, k: (0, off + j))
    tiles = pltpu.VMEM((HY_PAIRS_PER_COL, FFT_N2, LANE), F32)
    return pl.pallas_call(
        _hyena_kernel,
        grid=(ncol, nsub),
        in_specs=[nat(0), nat(ncol), nat(2 * ncol),
                  pl.BlockSpec((HY_ORDER, 2, HY_PAIRS_PER_STEP, 2 * FFT_N1, FFT_N2),
                               lambda j, k: (0, 0, j * nsub + k, 0, 0)),
                  pl.BlockSpec((HY_ORDER, HY_PAIRS_PER_STEP, 1, LANE), lambda j, k: (0, j * nsub + k, 0, 0))]
                 + [full(a) for a in consts],
        out_specs=pl.BlockSpec((t, LANE), lambda j, k: (0, j)),
        out_shape=jax.ShapeDtypeStruct((t, D_HY), us.dtype),
        scratch_shapes=[tiles, tiles, tiles, tiles, pltpu.VMEM((LANE, HY_TILE_PITCH, FFT_N2), F32)],
        compiler_params=pltpu.CompilerParams(
            dimension_semantics=("parallel", "arbitrary"), vmem_limit_bytes=VMEM_LIMIT_BYTES),
        name="hyena_conv",
    )(us, us, us, kf, biasp, *consts)


def _filter_time_kernel(w1t_ref, w1c_ref, w1s_ref, b1_ref, w2_ref, b2_ref, w3_ref, b3_ref, fr_ref,
                        wf_ref, wb_ref, df_ref, db_ref, o_ref, h_ref, k_ref, *, length):
    n_fft = 2 * length
    hp = lax.Precision.HIGHEST

    @pl.when(pl.program_id(0) == 0)
    def _():
        pos = lax.broadcasted_iota(jnp.int32, (1, n_fft), 1)
        lag = jnp.where(pos < length, pos, n_fft - pos).astype(F32)
        t = lag / float(length - 1)
        w = (2.0 * math.pi / length) * lag
        band_step = (HY_BANDS - 1 - 1e-4) / (HY_BANDS - 1)
        bands = 1e-4 + band_step * lax.broadcasted_iota(jnp.int32, (HY_BANDS, 1), 0).astype(F32)
        ang = bands * w
        fr = fr_ref[...]
        h = (w1t_ref[...] * t + jnp.dot(w1c_ref[...], jnp.cos(ang), preferred_element_type=F32, precision=hp)
             - jnp.dot(w1s_ref[...], jnp.sin(ang), preferred_element_type=F32, precision=hp))
        h = jnp.sin(fr * (h + b1_ref[...]))
        h = jnp.sin(fr * (jnp.dot(w2_ref[...], h, preferred_element_type=F32, precision=hp) + b2_ref[...]))
        h = jnp.sin(fr * (jnp.dot(w3_ref[...], h, preferred_element_type=F32, precision=hp) + b3_ref[...]))
        hi = h.astype(BF16)
        h_ref[0] = hi
        h_ref[1] = (h - hi.astype(F32)).astype(BF16)

    def dot3(w, lo, hi_):
        w_hi = w.astype(BF16)
        w_lo = (w - w_hi.astype(F32)).astype(BF16)
        h_hi, h_lo = h_ref[0, :, lo:hi_], h_ref[1, :, lo:hi_]
        return (jnp.dot(w_hi, h_hi, preferred_element_type=F32) + jnp.dot(w_hi, h_lo, preferred_element_type=F32)
                + jnp.dot(w_lo, h_hi, preferred_element_type=F32))

    pos = lax.broadcasted_iota(jnp.int32, (1, length), 1)
    tf = pos.astype(F32) / float(length - 1)
    tb = (length - pos).astype(F32) / float(length - 1)
    kf = dot3(wf_ref[...], 0, length) * jnp.exp(-tf * df_ref[...])
    kb = dot3(wb_ref[...], length, n_fft) * jnp.exp(-tb * db_ref[...])
    kb = jnp.where(pos == 0, 0.0, kb)
    inv = 1.0 / (jnp.sum(jnp.abs(kf), axis=1, keepdims=True) + jnp.sum(jnp.abs(kb), axis=1, keepdims=True))
    k_ref[:, :length] = kf * inv
    k_ref[:, length:] = kb * inv
    for n1 in range(FFT_N1):
        o_ref[:, n1, :] = k_ref[:, n1 * FFT_N2:(n1 + 1) * FFT_N2]
    o_ref[:, FFT_N1:, :] = jnp.zeros((o_ref.shape[0], FILTER_TILE_ROWS - FFT_N1, FFT_N2), F32)


def _filter_time(length, w1, b1, w2, b2, w3, b3, freq, w_out, cb=128):
    col = lambda v: v.reshape(-1, 1)
    w1t = w1.T
    n_ch = w_out.shape[1]
    deltas = jnp.abs(jnp.linspace(math.log(HY_TARGET) / HY_SLOW_PCT, math.log(HY_TARGET) / HY_FAST_PCT,
                                  n_ch, dtype=F32)).reshape(n_ch, 1)
    wot = w_out.T
    nb = D_HY // cb
    small = lambda a: pl.BlockSpec(a.shape, lambda i: (0,) * a.ndim)
    fwd = lambda i: ((i // nb) * HY_DIRS * nb + i % nb, 0)
    bwd = lambda i: ((i // nb) * HY_DIRS * nb + nb + i % nb, 0)
    ins = [w1t[:, 0:1], w1t[:, 1:1 + HY_BANDS], w1t[:, 1 + HY_BANDS:], col(b1), w2.T, col(b2), w3.T, col(b3),
           col(freq)]
    hy_ff = w2.shape[0]
    return pl.pallas_call(
        functools.partial(_filter_time_kernel, length=length),
        grid=(HY_ORDER * nb,),
        in_specs=[small(a) for a in ins] + [pl.BlockSpec((cb, hy_ff), fwd), pl.BlockSpec((cb, hy_ff), bwd),
                                            pl.BlockSpec((cb, 1), fwd), pl.BlockSpec((cb, 1), bwd)],
        out_specs=pl.BlockSpec((cb, FILTER_TILE_ROWS, FFT_N2), lambda i: (i, 0, 0)),
        out_shape=jax.ShapeDtypeStruct((HY_ORDER * D_HY, FILTER_TILE_ROWS, FFT_N2), F32),
        scratch_shapes=[pltpu.VMEM((2, hy_ff, 2 * length), BF16), pltpu.VMEM((cb, 2 * length), F32)],
        compiler_params=pltpu.CompilerParams(
            dimension_semantics=("arbitrary",), vmem_limit_bytes=VMEM_LIMIT_BYTES),
        name="hyena_filter_time",
    )(*ins, wot, wot, deltas, deltas)


def _filter_spec_kernel(k_ref, w1p_ref, t1_ref, t2_ref, f2_ref, o_ref):
    cb = k_ref.shape[0]
    xp = jnp.swapaxes(k_ref[:, :FFT_N1, :].reshape(cb // 2, 2 * FFT_N1, FFT_N2), 1, 2)
    spec = _fwd_spectrum(xp, w1p_ref[...], t1_ref[...], t2_ref[...], f2_ref[...])
    for par in range(2):
        o_ref[par, :, :FFT_N1, :] = spec[par][0]
        o_ref[par, :, FFT_N1:, :] = spec[par][1]


def _filter_spectrum(kt, cb=32):
    consts = _dft_constants(real_input=True)[:4]
    nb = D_HY // cb
    full = lambda a: pl.BlockSpec(a.shape, lambda i: (0,) * a.ndim)
    return pl.pallas_call(
        _filter_spec_kernel,
        grid=(HY_ORDER * nb,),
        in_specs=[pl.BlockSpec((cb, FILTER_TILE_ROWS, FFT_N2), lambda i: (i, 0, 0))] + [full(a) for a in consts],
        out_specs=pl.BlockSpec((None, 2, cb // 2, 2 * FFT_N1, FFT_N2), lambda i: (i // nb, 0, i % nb, 0, 0)),
        out_shape=jax.ShapeDtypeStruct((HY_ORDER, 2, D_HY // 2, 2 * FFT_N1, FFT_N2), F32),
        compiler_params=pltpu.CompilerParams(
            dimension_semantics=("parallel",), vmem_limit_bytes=VMEM_LIMIT_BYTES),
        name="hyena_filter_spectrum",
    )(kt, *consts)


def _hyena(us, w1, b1, w2, b2, w3, b3, freq, w_out, bias):
    bsz, length, _ = us.shape
    assert 2 * length == FFT_N and bsz == 2, "one complex transform carries exactly two batch rows"
    kf = _filter_spectrum(_filter_time(length, w1, b1, w2, b2, w3, b3, freq, w_out))
    biasp = jnp.repeat(bias.reshape(HY_ORDER, D_HY // 2, 1, 2), FFT_N1, axis=-1)
    return _hyena_conv(us.reshape(bsz * length, -1), kf, biasp).reshape(bsz, length, D_HY)


def kernel(x, c, ctx, c_ctx, w_ada, b_ada, norm_g, ffn_w_gate, ffn_w_up, ffn_w_down, w_in,
           s5_lam_re, s5_lam_im, s5_log_dt, s5_b_re, s5_b_im, s5_c_re, s5_c_im, s5_d,
           hy_short_w, hy_short_b, hy_w1, hy_b1, hy_w2, hy_b2, hy_w3, hy_b3, hy_freq, hy_w_out,
           hy_bias, w_pa, w_pb, w_out, final_g):
    bsz, seq, d = x.shape
    ctx_len = ctx.shape[1]
    n_rows = seq // GRID_W
    depth = w_ada.shape[0]
    assert depth == 1, "context-token outputs are only dropped by the last layer"
    l = 0

    c_rows = jnp.concatenate([c, c_ctx[None, :], jnp.zeros((8 - bsz - 1, d), F32)], axis=0)
    mod_all = _ada_mod(c_rows, w_ada[l], b_ada[l])
    mod = mod_all[:bsz].reshape(bsz, N_SUB, N_MOD, 1, d)
    mod_c = mod_all[bsz:bsz + 1].reshape(1, N_SUB, N_MOD, 1, d)

    def mods(m, sub):
        return tuple(m[:, sub, k] for k in range(N_MOD))

    wg, wu, wd = ffn_w_gate[l], ffn_w_up[l], ffn_w_down[l]
    w_in_b = w_in[l]

    xt = x.reshape(bsz * seq, d)
    ct = ctx.reshape(bsz * ctx_len, d)

    xt = _ffn_sublayer(xt, mods(mod, 0), norm_g[l, 0], wg, wu, wd, 0)
    ct = _ffn_sublayer(ct, mods(mod_c, 0), norm_g[l, 0], wg, wu, wd, 0)

    assert GRID_W * n_rows == seq
    u_s5, us_hy, sig_gates = _in_proj(xt, mod[:, 1, 0], mod[:, 1, 1], norm_g[l, 1], w_in_b,
                                      hy_short_w[l], hy_short_b[l], n_u=I_HY, n_hy=I_GA - I_HY)
    (u_ctx,) = _in_proj(ct, mod_c[:, 1, 0], mod_c[:, 1, 1], norm_g[l, 1], w_in_b[:, :D_S5])

    y_s5 = _s5_bidirectional(u_s5.reshape(bsz, seq, D_S5), u_ctx.reshape(bsz, ctx_len, D_S5),
                             s5_lam_re[l], s5_lam_im[l], s5_log_dt[l],
                             s5_b_re[l], s5_b_im[l], s5_c_re[l], s5_c_im[l], s5_d[l])
    y_hy = _hyena(us_hy.reshape(bsz, seq, I_GA - I_HY),
                  hy_w1[l], hy_b1[l], hy_w2[l], hy_b2[l], hy_w3[l], hy_b3[l], hy_freq[l],
                  hy_w_out[l], hy_bias[l])

    xt = _merge(xt, mod[:, 1, 2], y_s5.reshape(bsz * seq, D_S5), y_hy.reshape(bsz * seq, D_HY),
                sig_gates, w_pa[l].astype(BF16), w_pb[l].astype(BF16), w_out[l].astype(BF16))

    xt = _ffn_sublayer(xt, mods(mod, 2), norm_g[l, 2], wg, wu, wd, 1, final_gain=final_g)
    return xt.reshape(bsz, seq, d)
```

```python
import functools
import math

import jax
import jax.numpy as jnp
import numpy as np
from jax import lax
from jax.experimental import pallas as pl
from jax.experimental.pallas import tpu as pltpu

F32 = jnp.float32
BF16 = jnp.bfloat16

D_MODEL = 2048
GRID_W = 64
D_S5 = 1024
S5_GROUP = 16
S5_GROUPS = D_S5 // S5_GROUP
S5_STATE = 64
S5_DIRS = 2
LAMBDA_RE_MAX = -1e-4
S5_CHUNK = 16
LANE = 128
S5_GROUPS_PER_COL = LANE // S5_GROUP
D_HY = 1024
HY_ORDER = 2
HY_DIRS = 2
HY_SHORT = 3
HY_EMB = 33
HY_BANDS = (HY_EMB - 1) // 2
HY_TARGET = 1e-2
HY_FAST_PCT = 0.3
HY_SLOW_PCT = 1.5
FFT_N1 = 64
FFT_N2 = 128
FFT_N = FFT_N1 * FFT_N2
FILTER_TILE_ROWS = FFT_N1 + 8
HY_TILE_PITCH = FILTER_TILE_ROWS
HY_PAIRS_PER_COL = LANE // 2
HY_PAIRS_PER_STEP = 8
I_HY = D_S5
I_GA = D_S5 + (HY_ORDER + 1) * D_HY
I_GB = I_GA + D_MODEL
D_IN = I_GB + D_MODEL
D_FF = 5632
N_SUB = 3
N_MOD = 3
HALF_STEP = 0.5
RMS_EPS = 1e-6

VMEM_LIMIT_BYTES = 58 * 1024 * 1024


def _rms_mod(x, gain, shift, scale):
    ms = jnp.mean(x * x, axis=-1, keepdims=True)
    y = x * lax.rsqrt(ms + RMS_EPS) * gain
    return y * (1.0 + scale) + shift


def _ada_kernel(c_ref, w_ref, b_ref, o_ref):
    c = c_ref[...]
    a = c * jax.nn.sigmoid(c)
    o_ref[...] = jnp.dot(a, w_ref[...], preferred_element_type=F32,
                         precision=lax.Precision.HIGHEST) + b_ref[...]


def _ada_mod(c_rows, w, b, tn=1024):
    rows, d = c_rows.shape
    n = w.shape[1]
    return pl.pallas_call(
        _ada_kernel,
        grid=(n // tn,),
        in_specs=[pl.BlockSpec((rows, d), lambda j: (0, 0)),
                  pl.BlockSpec((d, tn), lambda j: (0, j)),
                  pl.BlockSpec((1, tn), lambda j: (0, j))],
        out_specs=pl.BlockSpec((rows, tn), lambda j: (0, j)),
        out_shape=jax.ShapeDtypeStruct((rows, n), F32),
        compiler_params=pltpu.CompilerParams(
            dimension_semantics=("arbitrary",), vmem_limit_bytes=VMEM_LIMIT_BYTES),
        name="ada_mod",
    )(c_rows, w, b.reshape(1, n))


def _ffn_kernel(x_ref, shift_ref, scale_ref, gate_ref, gain_ref, wg_ref, wu_ref, wd_ref,
                fg_ref, o_ref, h_ref, *, final_norm):
    j = pl.program_id(1)

    @pl.when(j == 0)
    def _():
        h_ref[...] = _rms_mod(x_ref[...], gain_ref[...], shift_ref[...], scale_ref[...]).astype(BF16)
        o_ref[...] = jnp.zeros_like(o_ref)

    h = h_ref[...]
    g = jnp.dot(h, wg_ref[...].astype(BF16), preferred_element_type=F32)
    u = jnp.dot(h, wu_ref[...].astype(BF16), preferred_element_type=F32)
    a = (g * jax.nn.sigmoid(g) * u).astype(BF16)
    o_ref[...] += jnp.dot(a, wd_ref[...].astype(BF16), preferred_element_type=F32)

    @pl.when(j == pl.num_programs(1) - 1)
    def _():
        y = x_ref[...] + (HALF_STEP * gate_ref[...]) * o_ref[...]
        if final_norm:
            ms = jnp.mean(y * y, axis=-1, keepdims=True)
            y = y * lax.rsqrt(ms + RMS_EPS) * fg_ref[...]
        o_ref[...] = y


def _ffn_sublayer(x, mods, gain, wg, wu, wd, which, final_gain=None, tm=1024, tf=256):
    t, d = x.shape
    bm = mods[0].shape[0]
    tm = min(tm, t // bm)
    blocks_per_batch = (t // bm) // tm
    dff = wg.shape[2]
    final_norm = final_gain is not None
    fg = final_gain if final_norm else gain
    mod_spec = pl.BlockSpec((None, 1, d), lambda i, j: (i // blocks_per_batch, 0, 0))
    vec_spec = pl.BlockSpec((1, d), lambda i, j: (0, 0))
    return pl.pallas_call(
        functools.partial(_ffn_kernel, final_norm=final_norm),
        grid=(t // tm, dff // tf),
        in_specs=[pl.BlockSpec((tm, d), lambda i, j: (i, 0)),
                  mod_spec, mod_spec, mod_spec, vec_spec,
                  pl.BlockSpec((None, d, tf), lambda i, j: (which, 0, j)),
                  pl.BlockSpec((None, d, tf), lambda i, j: (which, 0, j)),
                  pl.BlockSpec((None, tf, d), lambda i, j: (which, j, 0)),
                  vec_spec],
        out_specs=pl.BlockSpec((tm, d), lambda i, j: (i, 0)),
        out_shape=jax.ShapeDtypeStruct((t, d), F32),
        scratch_shapes=[pltpu.VMEM((tm, d), BF16)],
        compiler_params=pltpu.CompilerParams(
            dimension_semantics=("parallel", "arbitrary"), vmem_limit_bytes=VMEM_LIMIT_BYTES),
        name="ffn_final" if final_norm else "ffn",
    )(x, *mods, gain.reshape(1, d), wg, wu, wd, fg.reshape(1, d))


def _proj_kernel(x_ref, shift_ref, scale_ref, gain_ref, w_ref, sw_ref, sb_ref, *rest, n_u, n_hy, row_len, part):
    o_refs, h_ref = rest[:-1], rest[-1]
    j = pl.program_id(1)

    @pl.when(j == 0)
    def _():
        h_ref[...] = _rms_mod(x_ref[...], gain_ref[...], shift_ref[...], scale_ref[...]).astype(BF16)

    tm, tn = h_ref.shape[0], w_ref.shape[1]

    def in_parts(o_ref, epilogue):
        for c in range(0, tn, part):
            p = jnp.dot(h_ref[...], w_ref[:, c:c + part].astype(BF16), preferred_element_type=F32)
            o_ref[:, c:c + part] = epilogue(p, c).astype(o_ref.dtype)

    def short_conv(p, c):
        col = lax.broadcasted_iota(jnp.int32, p.shape, 0) % row_len
        prev = jnp.where(col == 0, 0.0, pltpu.roll(p, 1, 0))
        nxt = jnp.where(col == row_len - 1, 0.0, pltpu.roll(p, tm - 1, 0))
        sw = sw_ref[:, c:c + part]
        return sb_ref[:, c:c + part] + prev * sw[0:1] + p * sw[1:2] + nxt * sw[2:3]

    if n_hy == 0:
        in_parts(o_refs[0], lambda p, c: p)
        return

    @pl.when(j < n_u)
    def _():
        in_parts(o_refs[0], lambda p, c: p)

    @pl.when((j >= n_u) & (j < n_u + n_hy))
    def _():
        in_parts(o_refs[1], short_conv)

    @pl.when(j >= n_u + n_hy)
    def _():
        in_parts(o_refs[2], lambda p, c: jax.nn.sigmoid(p))


def _in_proj(x, shift, scale, gain, w, short_w=None, short_b=None, n_u=D_S5, n_hy=0, row_len=GRID_W,
             tm=1024, tn=512, part=256):
    t, d = x.shape
    bm = shift.shape[0]
    tm = min(tm, t // bm)
    blocks_per_batch = (t // bm) // tm
    n = w.shape[1]
    assert tm % row_len == 0 and (t // bm) % tm == 0
    bu, bh = n_u // tn, n_hy // tn
    bg = n // tn - bu - bh
    mod_spec = pl.BlockSpec((None, 1, d), lambda i, j: (i // blocks_per_batch, 0, 0))
    out_shape = [jax.ShapeDtypeStruct((t, n_u), F32)]
    out_specs = [pl.BlockSpec((tm, tn), lambda i, j: (i, jnp.minimum(j, bu - 1)))]
    if bh:
        out_shape += [jax.ShapeDtypeStruct((t, n_hy), BF16), jax.ShapeDtypeStruct((t, bg * tn), BF16)]
        out_specs += [pl.BlockSpec((tm, tn), lambda i, j: (i, jnp.clip(j - bu, 0, bh - 1))),
                      pl.BlockSpec((tm, tn), lambda i, j: (i, jnp.maximum(j - bu - bh, 0)))]
        sw, sb = short_w, short_b.reshape(1, n_hy)
        hy_blk = lambda i, j: (0, jnp.clip(j - bu, 0, bh - 1))
    else:
        sw, sb = jnp.zeros((HY_SHORT, tn), F32), jnp.zeros((1, tn), F32)
        hy_blk = lambda i, j: (0, 0)
    return pl.pallas_call(
        functools.partial(_proj_kernel, n_u=bu, n_hy=bh, row_len=row_len, part=part),
        grid=(t // tm, n // tn),
        in_specs=[pl.BlockSpec((tm, d), lambda i, j: (i, 0)),
                  mod_spec, mod_spec,
                  pl.BlockSpec((1, d), lambda i, j: (0, 0)),
                  pl.BlockSpec((d, tn), lambda i, j: (0, j)),
                  pl.BlockSpec((HY_SHORT, tn), hy_blk),
                  pl.BlockSpec((1, tn), hy_blk)],
        out_specs=out_specs,
        out_shape=out_shape,
        scratch_shapes=[pltpu.VMEM((tm, d), BF16)],
        compiler_params=pltpu.CompilerParams(
            dimension_semantics=("parallel", "arbitrary"), vmem_limit_bytes=VMEM_LIMIT_BYTES),
        name="in_proj",
    )(x, shift, scale, gain.reshape(1, d), w, sw, sb)


def _gelu_tanh(x):
    return 0.5 * x * (1.0 + jnp.tanh(math.sqrt(2.0 / math.pi) * (x + 0.044715 * (x * x * x))))


def _merge_kernel(x_ref, gate_ref, ys_ref, yh_ref, ga_ref, gb_ref, wpa_lo_ref, wpa_hi_ref, wpb_ref,
                  wout_ref, o_ref, s_ref, acc_ref):
    j = pl.program_id(1)

    @pl.when(j == 0)
    def _():
        s_ref[...] = _gelu_tanh(ys_ref[...].astype(F32)).astype(BF16)
        acc_ref[...] = jnp.zeros_like(acc_ref)

    s = s_ref[...]
    pa_lo = jnp.dot(s, wpa_lo_ref[...], preferred_element_type=F32)
    pa_hi = jnp.dot(s, wpa_hi_ref[...], preferred_element_type=F32)
    y_a = pa_lo * jax.nn.sigmoid(pa_hi)
    y_b = jnp.dot(yh_ref[...], wpb_ref[...], preferred_element_type=F32)
    m = ga_ref[...].astype(F32) * y_a + gb_ref[...].astype(F32) * y_b
    acc_ref[...] += jnp.dot(m.astype(BF16), wout_ref[...], preferred_element_type=F32)

    @pl.when(j == pl.num_programs(1) - 1)
    def _():
        o_ref[...] = x_ref[...] + gate_ref[...] * acc_ref[...]


def _merge(x, gate, y_s5, y_hy, sig_gates, w_pa, w_pb, w_out, tm=512, tn=512):
    t, d = x.shape
    bm = gate.shape[0]
    blocks_per_batch = (t // bm) // tm
    nj = d // tn
    ds5 = y_s5.shape[1]
    dhy = y_hy.shape[1]
    return pl.pallas_call(
        _merge_kernel,
        grid=(t // tm, nj),
        in_specs=[pl.BlockSpec((tm, d), lambda i, j: (i, 0)),
                  pl.BlockSpec((None, 1, d), lambda i, j: (i // blocks_per_batch, 0, 0)),
                  pl.BlockSpec((tm, ds5), lambda i, j: (i, 0)),
                  pl.BlockSpec((tm, dhy), lambda i, j: (i, 0)),
                  pl.BlockSpec((tm, tn), lambda i, j: (i, j)),
                  pl.BlockSpec((tm, tn), lambda i, j: (i, nj + j)),
                  pl.BlockSpec((ds5, tn), lambda i, j: (0, j)),
                  pl.BlockSpec((ds5, tn), lambda i, j: (0, nj + j)),
                  pl.BlockSpec((dhy, tn), lambda i, j: (0, j)),
                  pl.BlockSpec((tn, d), lambda i, j: (j, 0))],
        out_specs=pl.BlockSpec((tm, d), lambda i, j: (i, 0)),
        out_shape=jax.ShapeDtypeStruct((t, d), F32),
        scratch_shapes=[pltpu.VMEM((tm, ds5), BF16), pltpu.VMEM((tm, d), F32)],
        compiler_params=pltpu.CompilerParams(
            dimension_semantics=("parallel", "arbitrary"), vmem_limit_bytes=VMEM_LIMIT_BYTES),
        name="merge",
    )(x, gate, y_s5, y_hy, sig_gates, sig_gates, w_pa, w_pa, w_pb, w_out)


def _s5_weights_kernel(par_ref, bt_ref, c_ref, d_ref, bpow_ref, toep_ref, cpow_ref, tab_ref, ca_ref, *, n_steps):
    t, h, p = S5_CHUNK, S5_GROUP, S5_STATE
    lanes = 2 * p
    hp = lax.Precision.HIGHEST
    sgn = jnp.where(lax.broadcasted_iota(jnp.int32, (1, lanes), 1) < p, -1.0, 1.0)
    par = par_ref[...]
    gsum = None
    for d in range(S5_DIRS):
        lr = jnp.minimum(par[3 * d:3 * d + 1], LAMBDA_RE_MAX)
        li = par[3 * d + 1:3 * d + 2]
        dt = jnp.exp(par[3 * d + 2:3 * d + 3])
        zr, zi = lr * dt, li * dt

        def apow(j):
            mag = jnp.exp(j * zr)
            return mag * jnp.cos(j * zi), sgn * (mag * jnp.sin(j * zi))

        def cmul(x, a1, a2):
            return x * a1 + pltpu.roll(x, p, 1) * a2

        a1, a2 = apow(lax.broadcasted_iota(jnp.int32, (t + 1, 1), 0).astype(F32))
        nr, ni = a1[1:2] - 1.0, sgn * a2[1:2]
        den = lr * lr + li * li
        f_re = (nr * lr + ni * li) / den
        f_im = (ni * lr - nr * li) / den
        bbar = cmul(bt_ref[d], f_re, sgn * f_im)
        cc = c_ref[d]
        ca = [cmul(cc, a1[j:j + 1], a2[j:j + 1]) * (-sgn) for j in range(t + 1)]
        ca_ref[...] = jnp.zeros_like(ca_ref)
        for k in range(t):
            e_b, e_c = (t - 1 - k, k + 1) if d == 0 else (k, t - k)
            bpow_ref[k * h:(k + 1) * h, d * lanes:(d + 1) * lanes] = (
                cmul(bbar, a1[e_b:e_b + 1], a2[e_b:e_b + 1]).astype(BF16))
            cpow_ref[k * h:(k + 1) * h, d * lanes:(d + 1) * lanes] = ca[e_c].astype(BF16)
            l = t - 1 + k if d == 0 else t - 1 - k
            ca_ref[l * h:(l + 1) * h, :] = ca[k]
        g = lax.dot_general(bbar, ca_ref[...], (((1,), (1,)), ((), ())), preferred_element_type=F32, precision=hp)
        gsum = g if gsum is None else gsum + g
        for s in range(n_steps):
            s1, s2 = apow(float(t * 2 ** s))
            r = d * 2 * n_steps + 2 * s
            tab_ref[r:r + 1, :] = s1
            tab_ref[r + 1:r + 2, :] = s2
    wide = gsum.shape[1]
    col = lax.broadcasted_iota(jnp.int32, (h, wide), 1)
    row = lax.broadcasted_iota(jnp.int32, (h, wide), 0)
    gsum = gsum + jnp.where(col - (t - 1) * h == row, d_ref[...], 0.0)
    for k in range(t):
        off = (t - 1 - k) * h
        shifted = gsum if off == 0 else pltpu.roll(gsum, wide - off, 1)
        toep_ref[k * h:(k + 1) * h, :] = shifted[:, :t * h].astype(BF16)


def _s5_weights(lam_re, lam_im, log_dt, b_re, b_im, c_re, c_im, d_skip, n_steps):
    g, p, h, t = S5_GROUPS, S5_STATE, S5_GROUP, S5_CHUNK
    cat2 = lambda a: jnp.concatenate([a, a], axis=-1)
    par = jnp.stack([cat2(lam_re), cat2(lam_im), jnp.broadcast_to(log_dt[..., None], (S5_DIRS, g, 2 * p))], axis=1)
    par = par.transpose(2, 0, 1, 3).reshape(g, 3 * S5_DIRS, 2 * p)
    btc = jnp.concatenate([b_re, b_im], axis=2).transpose(1, 0, 3, 2)
    ccat = jnp.concatenate([c_re, c_im], axis=3).transpose(1, 0, 2, 3)
    wide = 2 * t * h
    drow = jnp.zeros((g, 1, wide), F32).at[:, 0, (t - 1) * h:t * h].set(d_skip.reshape(g, h))
    w = t * h
    sq = pl.BlockSpec((None, w, w), lambda i: (i, 0, 0))
    return pl.pallas_call(
        functools.partial(_s5_weights_kernel, n_steps=n_steps),
        grid=(g,),
        in_specs=[pl.BlockSpec((None, 3 * S5_DIRS, 2 * p), lambda i: (i, 0, 0)),
                  pl.BlockSpec((None, S5_DIRS, h, 2 * p), lambda i: (i, 0, 0, 0)),
                  pl.BlockSpec((None, S5_DIRS, h, 2 * p), lambda i: (i, 0, 0, 0)),
                  pl.BlockSpec((None, 1, wide), lambda i: (i, 0, 0))],
        out_specs=[sq, sq, sq, pl.BlockSpec((None, 4 * n_steps, 2 * p), lambda i: (i, 0, 0))],
        out_shape=[jax.ShapeDtypeStruct((g, w, w), BF16)] * 3 + [jax.ShapeDtypeStruct((g, 4 * n_steps, 2 * p), F32)],
        scratch_shapes=[pltpu.VMEM((wide, 2 * p), F32)],
        compiler_params=pltpu.CompilerParams(
            dimension_semantics=("parallel",), vmem_limit_bytes=VMEM_LIMIT_BYTES),
        name="s5_weights",
    )(par, btc, ccat, drow)


def _s5_select():
    t, h = S5_CHUNK, S5_GROUP
    sel = np.zeros((S5_GROUPS_PER_COL, t, LANE, t, h), np.float32)
    for gl in range(S5_GROUPS_PER_COL):
        for k in range(t):
            sel[gl, k, gl * h:(gl + 1) * h, k, :] = np.eye(h)
    return jnp.asarray(sel.reshape(S5_GROUPS_PER_COL, t * LANE, t * h)).astype(BF16)


def _s5_kernel(u_ref, uc_ref, bpow_ref, toep_ref, cpow_ref, tab_ref, sel_ref, y_ref, ucat_ref, ucc_ref, yacc_ref,
               *, bsz, n_steps):
    t = S5_CHUNK
    n_lat = u_ref.shape[0] // (bsz * t)
    n_ctx = uc_ref.shape[0] // (bsz * t)
    n_ch = n_lat + n_ctx
    rows = bsz * n_ch
    half = 2 * S5_STATE
    for k in range(t):
        yacc_ref[:, k * LANE:(k + 1) * LANE] = u_ref[pl.ds(k, bsz * n_lat, stride=t), :]
        ucc_ref[:, k * LANE:(k + 1) * LANE] = uc_ref[pl.ds(k, bsz * n_ctx, stride=t), :].astype(BF16)
    ucat_ref[...] = yacc_ref[...].astype(BF16)
    yacc_ref[...] = jnp.zeros_like(yacc_ref)
    rib = lax.broadcasted_iota(jnp.int32, (rows, half), 0) % n_ch

    def cmul_add(acc, sh, a1, a2):
        return acc + a1 * sh + a2 * pltpu.roll(sh, S5_STATE, 1)

    def group(gl, carry):
        sel = sel_ref[gl]
        ul = jnp.dot(ucat_ref[...], sel, preferred_element_type=F32).astype(BF16)
        uc = jnp.dot(ucc_ref[...], sel, preferred_element_type=F32).astype(BF16)
        bpow = bpow_ref[gl]
        zl = jnp.dot(ul, bpow, preferred_element_type=F32)
        zc = jnp.dot(uc, bpow, preferred_element_type=F32)
        fparts, bparts = [], []
        for b in range(bsz):
            lat = slice(b * n_lat, (b + 1) * n_lat)
            ctx = slice(b * n_ctx, (b + 1) * n_ctx)
            fparts += [zc[ctx, :half], zl[lat, :half]]
            bparts += [zl[lat, half:], zc[ctx, half:]]
        fw = jnp.concatenate(fparts, axis=0)
        bw = jnp.concatenate(bparts, axis=0)
        tab = tab_ref[gl]
        for s in range(n_steps):
            d = 1 << s
            sh = jnp.where(rib >= d, pltpu.roll(fw, d, 0), 0.0)
            fw = cmul_add(fw, sh, tab[2 * s:2 * s + 1], tab[2 * s + 1:2 * s + 2])
            o = 2 * n_steps
            sh = jnp.where(rib < n_ch - d, pltpu.roll(bw, rows - d, 0), 0.0)
            bw = cmul_add(bw, sh, tab[o + 2 * s:o + 2 * s + 1], tab[o + 2 * s + 1:o + 2 * s + 2])
        fe = jnp.where(rib >= 1, pltpu.roll(fw, 1, 0), 0.0)
        be = jnp.where(rib < n_ch - 1, pltpu.roll(bw, rows - 1, 0), 0.0)
        fl = jnp.concatenate([fe[b * n_ch + n_ctx:(b + 1) * n_ch] for b in range(bsz)], axis=0)
        bl = jnp.concatenate([be[b * n_ch:b * n_ch + n_lat] for b in range(bsz)], axis=0)
        st = jnp.concatenate([fl, bl], axis=1).astype(BF16)
        y = (jnp.dot(ul, toep_ref[gl], preferred_element_type=F32)
             + lax.dot_general(st, cpow_ref[gl], (((1,), (1,)), ((), ())), preferred_element_type=F32))
        yacc_ref[...] += lax.dot_general(y.astype(BF16), sel, (((1,), (1,)), ((), ())),
                                         preferred_element_type=F32)
        return carry

    lax.fori_loop(0, S5_GROUPS_PER_COL, group, 0)
    for i in range(t):
        y_ref[pl.ds(i, bsz * n_lat, stride=t), :] = yacc_ref[:, i * LANE:(i + 1) * LANE]


def _s5_mix(u, uc, bpw, toep, cpw, tab, bsz):
    rl, dm = u.shape
    rc = uc.shape[0]
    t = S5_CHUNK
    w = t * S5_GROUP
    n_steps = tab.shape[1] // 4
    sel = _s5_select()
    gpc = S5_GROUPS_PER_COL
    wspec = pl.BlockSpec((gpc, w, w), lambda i: (i, 0, 0))
    return pl.pallas_call(
        functools.partial(_s5_kernel, bsz=bsz, n_steps=n_steps),
        grid=(dm // LANE,),
        in_specs=[pl.BlockSpec((rl, LANE), lambda i: (0, i)),
                  pl.BlockSpec((rc, LANE), lambda i: (0, i)),
                  wspec, wspec, wspec,
                  pl.BlockSpec((gpc, 4 * n_steps, tab.shape[2]), lambda i: (i, 0, 0)),
                  pl.BlockSpec(sel.shape, lambda i: (0, 0, 0))],
        out_specs=pl.BlockSpec((rl, LANE), lambda i: (0, i)),
        out_shape=jax.ShapeDtypeStruct(u.shape, F32),
        scratch_shapes=[pltpu.VMEM((rl // t, t * LANE), BF16), pltpu.VMEM((rc // t, t * LANE), BF16),
                        pltpu.VMEM((rl // t, t * LANE), F32)],
        compiler_params=pltpu.CompilerParams(
            dimension_semantics=("parallel",), vmem_limit_bytes=VMEM_LIMIT_BYTES),
        name="s5_mix",
    )(u, uc, bpw, toep, cpw, tab, sel)


def _s5_bidirectional(u, u_ctx, lam_re, lam_im, log_dt, b_re, b_im, c_re, c_im, d_skip):
    bsz, length, dm = u.shape
    ctx_len = u_ctx.shape[1]
    t = S5_CHUNK
    n_lat, n_ctx = length // t, ctx_len // t
    n_steps = max(1, math.ceil(math.log2(n_lat + n_ctx)))
    bpw, toep, cpw, tab = _s5_weights(lam_re, lam_im, log_dt, b_re, b_im, c_re, c_im, d_skip, n_steps)
    y = _s5_mix(u.reshape(bsz * length, dm), u_ctx.reshape(bsz * ctx_len, dm), bpw, toep, cpw, tab, bsz)
    return y.reshape(bsz, length, dm)


def _dft_constants(real_input=False):
    n1 = np.arange(FFT_N1)
    n2 = np.arange(FFT_N2)
    half = FFT_N1 // 2
    th = 2 * np.pi * np.outer(n1, n1) / FFT_N1
    c1, s1 = np.cos(th), np.sin(th)
    if real_input:
        w1 = np.concatenate([c1, -s1], axis=1)
    else:
        w1 = np.concatenate([np.concatenate([c1[:half], -s1[:half]], axis=1),
                             np.concatenate([s1[:half], c1[:half]], axis=1)], axis=0)
    z = np.zeros_like(w1)
    w1p = np.block([[w1, z], [z, w1]])
    ph = 2 * np.pi * np.outer(n2, n1) / FFT_N
    t1 = np.concatenate([np.cos(ph), np.cos(ph)], axis=1)
    t2 = np.concatenate([np.sin(ph), -np.sin(ph)], axis=1)
    ps = 2 * np.pi * np.outer(n2, n2) / FFT_N2
    f2 = np.concatenate([np.cos(ps), -np.sin(ps)], axis=1)
    g2 = np.concatenate([np.cos(ps), np.sin(ps)], axis=1)
    c2, s2 = np.cos(ph).T, np.sin(ph).T
    wi = np.concatenate([np.concatenate([c1[:, :half], s1[:, :half]], axis=1),
                         np.concatenate([-s1[:, :half], c1[:, :half]], axis=1)], axis=0) / FFT_N
    zi = np.zeros_like(wi)
    wi2 = np.stack([np.concatenate([wi, zi], axis=1), np.concatenate([zi, wi], axis=1)])
    as_b = lambda a: jnp.asarray(a, F32).astype(BF16)
    as_f = lambda a: jnp.asarray(a, F32)
    return [as_b(w1p), as_f(t1), as_f(t2), as_b(f2), as_b(g2), as_f(c2), as_f(s2), as_b(wi2)]


def _fwd_spectrum(xp, w1p, t1, t2, f2):
    cp = xp.shape[0]
    hn = FFT_N1
    a = jnp.dot(xp.reshape(cp * FFT_N2, 128).astype(BF16), w1p, preferred_element_type=F32)
    out = []
    for par in range(2):
        ap = a[:, par * 128:(par + 1) * 128]
        ap = ap.reshape(cp, FFT_N2, 128) * t1 + pltpu.roll(ap, hn, 1).reshape(cp, FFT_N2, 128) * t2
        at = jnp.swapaxes(ap, 1, 2)
        p = jnp.dot(at.reshape(cp * 128, FFT_N2).astype(BF16), f2, preferred_element_type=F32)
        p = p.reshape(cp, 128, 2 * FFT_N2)
        out.append((p[:, :hn, :FFT_N2] - p[:, hn:, FFT_N2:], p[:, :hn, FFT_N2:] + p[:, hn:, :FFT_N2]))
    return out


def _inv_time(yre, yim, g2, c2, s2, wi_par):
    cp = yre.shape[0]
    hn = FFT_N1
    y = jnp.concatenate([yre, yim], axis=1).reshape(cp * 128, FFT_N2).astype(BF16)
    q = jnp.dot(y, g2, preferred_element_type=F32).reshape(cp, 128, 2 * FFT_N2)
    bre = q[:, :hn, :FFT_N2] - q[:, hn:, FFT_N2:]
    bim = q[:, :hn, FFT_N2:] + q[:, hn:, :FFT_N2]
    b2 = jnp.concatenate([bre * c2 - bim * s2, bre * s2 + bim * c2], axis=1)
    bt = jnp.swapaxes(b2, 1, 2)
    return jnp.dot(bt.reshape(cp * FFT_N2, 128).astype(BF16), wi_par, preferred_element_type=F32)


def _hyena_kernel(z_ref, g1_ref, g2_ref, kf_ref, bias_ref, w1p_ref, t1_ref, t2_ref, f2_ref, gi_ref, c2_ref,
                  s2_ref, wi_ref, o_ref, zt_ref, g1t_ref, g2t_ref, ot_ref, stage_ref):
    k = pl.program_id(1)
    n_s = z_ref.shape[0] // FFT_N2
    cp = HY_PAIRS_PER_STEP

    def to_tiles(x_ref, t_ref):
        for s in range(n_s):
            xs = x_ref[s * FFT_N2:(s + 1) * FFT_N2, :].astype(F32)
            stage_ref[pl.ds(s, LANE, stride=HY_TILE_PITCH), :] = xs.T
        st = stage_ref[...].reshape(HY_PAIRS_PER_COL, 2 * HY_TILE_PITCH, FFT_N2)
        both = jnp.concatenate([st[:, :n_s], st[:, HY_TILE_PITCH:HY_TILE_PITCH + n_s]], axis=1)
        t_ref[...] = jnp.swapaxes(both, 1, 2)

    @pl.when(k == 0)
    def _():
        to_tiles(z_ref, zt_ref)
        to_tiles(g1_ref, g1t_ref)
        to_tiles(g2_ref, g2t_ref)

    sl = pl.ds(pl.multiple_of(k * cp, cp), cp)
    z = zt_ref[sl]
    gates = (g1t_ref, g2t_ref)
    for o in range(HY_ORDER):
        spec = _fwd_spectrum(z, w1p_ref[...], t1_ref[...], t2_ref[...], f2_ref[...])
        conv = None
        for par in range(2):
            xre, xim = spec[par]
            kre = kf_ref[o, par, :, :FFT_N1, :]
            kim = kf_ref[o, par, :, FFT_N1:, :]
            part = _inv_time(xre * kre - xim * kim, xre * kim + xim * kre,
                             gi_ref[...], c2_ref[...], s2_ref[...], wi_ref[par])
            conv = part if conv is None else conv + part
        z = gates[o][sl] * (conv.reshape(cp, FFT_N2, LANE) + bias_ref[o] * z)
    ot_ref[sl] = z

    @pl.when(k == pl.num_programs(1) - 1)
    def _():
        back = jnp.swapaxes(ot_ref[...], 1, 2)
        for c in range(LANE):
            stage_ref[c * HY_TILE_PITCH:c * HY_TILE_PITCH + n_s, :] = back[c // 2, (c % 2) * n_s:(c % 2 + 1) * n_s, :]
        for s in range(n_s):
            rows = stage_ref[pl.ds(s, LANE, stride=HY_TILE_PITCH), :]
            o_ref[s * FFT_N2:(s + 1) * FFT_N2, :] = rows.T.astype(o_ref.dtype)


def _hyena_conv(us, kf, biasp):
    t, _ = us.shape
    consts = _dft_constants()
    ncol = D_HY // LANE
    nsub = HY_PAIRS_PER_COL // HY_PAIRS_PER_STEP
    full = lambda a: pl.BlockSpec(a.shape, lambda j, k: (0,) * a.ndim)
    nat = lambda off: pl.BlockSpec((t, LANE), lambda j, k: (0, off + j))
    tiles = pltpu.VMEM((HY_PAIRS_PER_COL, FFT_N2, LANE), F32)
    return pl.pallas_call(
        _hyena_kernel,
        grid=(ncol, nsub),
        in_specs=[nat(0), nat(ncol), nat(2 * ncol),
                  pl.BlockSpec((HY_ORDER, 2, HY_PAIRS_PER_STEP, 2 * FFT_N1, FFT_N2),
                               lambda j, k: (0, 0, j * nsub + k, 0, 0)),
                  pl.BlockSpec((HY_ORDER, HY_PAIRS_PER_STEP, 1, LANE), lambda j, k: (0, j * nsub + k, 0, 0))]
                 + [full(a) for a in consts],
        out_specs=pl.BlockSpec((t, LANE), lambda j, k: (0, j)),
        out_shape=jax.ShapeDtypeStruct((t, D_HY), us.dtype),
        scratch_shapes=[tiles, tiles, tiles, tiles, pltpu.VMEM((LANE * HY_TILE_PITCH, FFT_N2), F32)],
        compiler_params=pltpu.CompilerParams(
            dimension_semantics=("parallel", "arbitrary"), vmem_limit_bytes=VMEM_LIMIT_BYTES),
        name="hyena_conv",
    )(us, us, us, kf, biasp, *consts)


def _filter_time_kernel(w1t_ref, w1c_ref, w1s_ref, b1_ref, w2_ref, b2_ref, w3_ref, b3_ref, fr_ref,
                        wf_ref, wb_ref, df_ref, db_ref, o_ref, h_ref, k_ref, *, length):
    n_fft = 2 * length
    hp = lax.Precision.HIGHEST

    @pl.when(pl.program_id(0) == 0)
    def _():
        pos = lax.broadcasted_iota(jnp.int32, (1, n_fft), 1)
        lag = jnp.where(pos < length, pos, n_fft - pos).astype(F32)
        t = lag / float(length - 1)
        w = (2.0 * math.pi / length) * lag
        band_step = (HY_BANDS - 1 - 1e-4) / (HY_BANDS - 1)
        bands = 1e-4 + band_step * lax.broadcasted_iota(jnp.int32, (HY_BANDS, 1), 0).astype(F32)
        ang = bands * w
        fr = fr_ref[...]
        h = (w1t_ref[...] * t + jnp.dot(w1c_ref[...], jnp.cos(ang), preferred_element_type=F32, precision=hp)
             - jnp.dot(w1s_ref[...], jnp.sin(ang), preferred_element_type=F32, precision=hp))
        h = jnp.sin(fr * (h + b1_ref[...]))
        h = jnp.sin(fr * (jnp.dot(w2_ref[...], h, preferred_element_type=F32, precision=hp) + b2_ref[...]))
        h = jnp.sin(fr * (jnp.dot(w3_ref[...], h, preferred_element_type=F32, precision=hp) + b3_ref[...]))
        hi = h.astype(BF16)
        h_ref[0] = hi
        h_ref[1] = (h - hi.astype(F32)).astype(BF16)

    def dot3(w, lo, hi_):
        w_hi = w.astype(BF16)
        w_lo = (w - w_hi.astype(F32)).astype(BF16)
        h_hi, h_lo = h_ref[0, :, lo:hi_], h_ref[1, :, lo:hi_]
        return (jnp.dot(w_hi, h_hi, preferred_element_type=F32) + jnp.dot(w_hi, h_lo, preferred_element_type=F32)
                + jnp.dot(w_lo, h_hi, preferred_element_type=F32))

    pos = lax.broadcasted_iota(jnp.int32, (1, length), 1)
    tf = pos.astype(F32) / float(length - 1)
    tb = (length - pos).astype(F32) / float(length - 1)
    kf = dot3(wf_ref[...], 0, length) * jnp.exp(-tf * df_ref[...])
    kb = dot3(wb_ref[...], length, n_fft) * jnp.exp(-tb * db_ref[...])
    kb = jnp.where(pos == 0, 0.0, kb)
    inv = 1.0 / (jnp.sum(jnp.abs(kf), axis=1, keepdims=True) + jnp.sum(jnp.abs(kb), axis=1, keepdims=True))
    k_ref[:, :length] = kf * inv
    k_ref[:, length:] = kb * inv
    cb = wf_ref.shape[0]
    for n1 in range(FILTER_TILE_ROWS):
        row = k_ref[:, n1 * FFT_N2:(n1 + 1) * FFT_N2] if n1 < FFT_N1 else jnp.zeros((cb, FFT_N2), F32)
        o_ref[pl.ds(n1, cb, stride=FILTER_TILE_ROWS), :] = row


def _filter_time(length, w1, b1, w2, b2, w3, b3, freq, w_out, cb=128):
    col = lambda v: v.reshape(-1, 1)
    w1t = w1.T
    n_ch = w_out.shape[1]
    deltas = jnp.abs(jnp.linspace(math.log(HY_TARGET) / HY_SLOW_PCT, math.log(HY_TARGET) / HY_FAST_PCT,
                                  n_ch, dtype=F32)).reshape(n_ch, 1)
    wot = w_out.T
    nb = D_HY // cb
    small = lambda a: pl.BlockSpec(a.shape, lambda i: (0,) * a.ndim)
    fwd = lambda i: ((i // nb) * HY_DIRS * nb + i % nb, 0)
    bwd = lambda i: ((i // nb) * HY_DIRS * nb + nb + i % nb, 0)
    ins = [w1t[:, 0:1], w1t[:, 1:1 + HY_BANDS], w1t[:, 1 + HY_BANDS:], col(b1), w2.T, col(b2), w3.T, col(b3),
           col(freq)]
    hy_ff = w2.shape[0]
    return pl.pallas_call(
        functools.partial(_filter_time_kernel, length=length),
        grid=(HY_ORDER * nb,),
        in_specs=[small(a) for a in ins] + [pl.BlockSpec((cb, hy_ff), fwd), pl.BlockSpec((cb, hy_ff), bwd),
                                            pl.BlockSpec((cb, 1), fwd), pl.BlockSpec((cb, 1), bwd)],
        out_specs=pl.BlockSpec((cb * FILTER_TILE_ROWS, FFT_N2), lambda i: (i, 0)),
        out_shape=jax.ShapeDtypeStruct((HY_ORDER * D_HY * FILTER_TILE_ROWS, FFT_N2), F32),
        scratch_shapes=[pltpu.VMEM((2, hy_ff, 2 * length), BF16), pltpu.VMEM((cb, 2 * length), F32)],
        compiler_params=pltpu.CompilerParams(
            dimension_semantics=("arbitrary",), vmem_limit_bytes=VMEM_LIMIT_BYTES),
        name="hyena_filter_time",
    )(*ins, wot, wot, deltas, deltas)


def _filter_spec_kernel(k_ref, w1p_ref, t1_ref, t2_ref, f2_ref, o_ref):
    cb = k_ref.shape[0] // FILTER_TILE_ROWS
    kt = k_ref[...].reshape(cb, FILTER_TILE_ROWS, FFT_N2)[:, :FFT_N1, :]
    xp = jnp.swapaxes(kt.reshape(cb // 2, 2 * FFT_N1, FFT_N2), 1, 2)
    spec = _fwd_spectrum(xp, w1p_ref[...], t1_ref[...], t2_ref[...], f2_ref[...])
    for par in range(2):
        o_ref[par, :, :FFT_N1, :] = spec[par][0]
        o_ref[par, :, FFT_N1:, :] = spec[par][1]


def _filter_spectrum(kt, cb=32):
    consts = _dft_constants(real_input=True)[:4]
    nb = D_HY // cb
    full = lambda a: pl.BlockSpec(a.shape, lambda i: (0,) * a.ndim)
    return pl.pallas_call(
        _filter_spec_kernel,
        grid=(HY_ORDER * nb,),
        in_specs=[pl.BlockSpec((cb * FILTER_TILE_ROWS, FFT_N2), lambda i: (i, 0))] + [full(a) for a in consts],
        out_specs=pl.BlockSpec((None, 2, cb // 2, 2 * FFT_N1, FFT_N2), lambda i: (i // nb, 0, i % nb, 0, 0)),
        out_shape=jax.ShapeDtypeStruct((HY_ORDER, 2, D_HY // 2, 2 * FFT_N1, FFT_N2), F32),
        compiler_params=pltpu.CompilerParams(
            dimension_semantics=("parallel",), vmem_limit_bytes=VMEM_LIMIT_BYTES),
        name="hyena_filter_spectrum",
    )(kt, *consts)


def _hyena(us, w1, b1, w2, b2, w3, b3, freq, w_out, bias):
    bsz, length, _ = us.shape
    assert 2 * length == FFT_N and bsz == 2, "one complex transform carries exactly two batch rows"
    kf = _filter_spectrum(_filter_time(length, w1, b1, w2, b2, w3, b3, freq, w_out))
    biasp = jnp.repeat(bias.reshape(HY_ORDER, D_HY // 2, 1, 2), FFT_N1, axis=-1)
    return _hyena_conv(us.reshape(bsz * length, -1), kf, biasp).reshape(bsz, length, D_HY)


def kernel(x, c, ctx, c_ctx, w_ada, b_ada, norm_g, ffn_w_gate, ffn_w_up, ffn_w_down, w_in,
           s5_lam_re, s5_lam_im, s5_log_dt, s5_b_re, s5_b_im, s5_c_re, s5_c_im, s5_d,
           hy_short_w, hy_short_b, hy_w1, hy_b1, hy_w2, hy_b2, hy_w3, hy_b3, hy_freq, hy_w_out,
           hy_bias, w_pa, w_pb, w_out, final_g):
    bsz, seq, d = x.shape
    ctx_len = ctx.shape[1]
    n_rows = seq // GRID_W
    depth = w_ada.shape[0]
    assert depth == 1, "context-token outputs are only dropped by the last layer"
    l = 0

    c_rows = jnp.concatenate([c, c_ctx[None, :], jnp.zeros((8 - bsz - 1, d), F32)], axis=0)
    mod_all = _ada_mod(c_rows, w_ada[l], b_ada[l])
    mod = mod_all[:bsz].reshape(bsz, N_SUB, N_MOD, 1, d)
    mod_c = mod_all[bsz:bsz + 1].reshape(1, N_SUB, N_MOD, 1, d)

    def mods(m, sub):
        return tuple(m[:, sub, k] for k in range(N_MOD))

    wg, wu, wd = ffn_w_gate[l], ffn_w_up[l], ffn_w_down[l]
    w_in_b = w_in[l]

    xt = x.reshape(bsz * seq, d)
    ct = ctx.reshape(bsz * ctx_len, d)

    xt = _ffn_sublayer(xt, mods(mod, 0), norm_g[l, 0], wg, wu, wd, 0)
    ct = _ffn_sublayer(ct, mods(mod_c, 0), norm_g[l, 0], wg, wu, wd, 0)

    assert GRID_W * n_rows == seq
    u_s5, us_hy, sig_gates = _in_proj(xt, mod[:, 1, 0], mod[:, 1, 1], norm_g[l, 1], w_in_b,
                                      hy_short_w[l], hy_short_b[l], n_u=I_HY, n_hy=I_GA - I_HY)
    (u_ctx,) = _in_proj(ct, mod_c[:, 1, 0], mod_c[:, 1, 1], norm_g[l, 1], w_in_b[:, :D_S5])

    y_s5 = _s5_bidirectional(u_s5.reshape(bsz, seq, D_S5), u_ctx.reshape(bsz, ctx_len, D_S5),
                             s5_lam_re[l], s5_lam_im[l], s5_log_dt[l],
                             s5_b_re[l], s5_b_im[l], s5_c_re[l], s5_c_im[l], s5_d[l])
    y_hy = _hyena(us_hy.reshape(bsz, seq, I_GA - I_HY),
                  hy_w1[l], hy_b1[l], hy_w2[l], hy_b2[l], hy_w3[l], hy_b3[l], hy_freq[l],
                  hy_w_out[l], hy_bias[l])

    xt = _merge(xt, mod[:, 1, 2], y_s5.reshape(bsz * seq, D_S5), y_hy.reshape(bsz * seq, D_HY),
                sig_gates, w_pa[l].astype(BF16), w_pb[l].astype(BF16), w_out[l].astype(BF16))

    xt = _ffn_sublayer(xt, mods(mod, 2), norm_g[l, 2], wg, wu, wd, 1, final_gain=final_g)
    return xt.reshape(bsz, seq, d)
```

```python
import functools
import math

import jax
import jax.numpy as jnp
import numpy as np
from jax import lax
from jax.experimental import pallas as pl
from jax.experimental.pallas import tpu as pltpu

F32 = jnp.float32
BF16 = jnp.bfloat16

D_MODEL = 2048
GRID_W = 64
D_S5 = 1024
S5_GROUP = 16
S5_GROUPS = D_S5 // S5_GROUP
S5_STATE = 64
S5_DIRS = 2
LAMBDA_RE_MAX = -1e-4
S5_CHUNK = 16
LANE = 128
S5_GROUPS_PER_COL = LANE // S5_GROUP
D_HY = 1024
HY_ORDER = 2
HY_DIRS = 2
HY_SHORT = 3
HY_EMB = 33
HY_BANDS = (HY_EMB - 1) // 2
HY_TARGET = 1e-2
HY_FAST_PCT = 0.3
HY_SLOW_PCT = 1.5
FFT_N1 = 64
FFT_N2 = 128
FFT_N = FFT_N1 * FFT_N2
FILTER_TILE_ROWS = FFT_N1 + 8
HY_TILE_PITCH = FILTER_TILE_ROWS
HY_PAIRS_PER_COL = LANE // 2
HY_PAIRS_PER_STEP = 8
I_HY = D_S5
I_GA = D_S5 + (HY_ORDER + 1) * D_HY
I_GB = I_GA + D_MODEL
D_IN = I_GB + D_MODEL
D_FF = 5632
N_SUB = 3
N_MOD = 3
HALF_STEP = 0.5
RMS_EPS = 1e-6

VMEM_LIMIT_BYTES = 58 * 1024 * 1024


def _rms_mod(x, gain, shift, scale):
    ms = jnp.mean(x * x, axis=-1, keepdims=True)
    y = x * lax.rsqrt(ms + RMS_EPS) * gain
    return y * (1.0 + scale) + shift


def _ada_kernel(c_ref, w_ref, b_ref, o_ref):
    c = c_ref[...]
    a = c * jax.nn.sigmoid(c)
    o_ref[...] = jnp.dot(a, w_ref[...], preferred_element_type=F32,
                         precision=lax.Precision.HIGHEST) + b_ref[...]


def _ada_mod(c_rows, w, b, tn=1024):
    rows, d = c_rows.shape
    n = w.shape[1]
    return pl.pallas_call(
        _ada_kernel,
        grid=(n // tn,),
        in_specs=[pl.BlockSpec((rows, d), lambda j: (0, 0)),
                  pl.BlockSpec((d, tn), lambda j: (0, j)),
                  pl.BlockSpec((1, tn), lambda j: (0, j))],
        out_specs=pl.BlockSpec((rows, tn), lambda j: (0, j)),
        out_shape=jax.ShapeDtypeStruct((rows, n), F32),
        compiler_params=pltpu.CompilerParams(
            dimension_semantics=("arbitrary",), vmem_limit_bytes=VMEM_LIMIT_BYTES),
        name="ada_mod",
    )(c_rows, w, b.reshape(1, n))


def _ffn_kernel(x_ref, shift_ref, scale_ref, gate_ref, gain_ref, wg_ref, wu_ref, wd_ref,
                fg_ref, o_ref, h_ref, *, final_norm):
    j = pl.program_id(1)

    @pl.when(j == 0)
    def _():
        h_ref[...] = _rms_mod(x_ref[...], gain_ref[...], shift_ref[...], scale_ref[...]).astype(BF16)
        o_ref[...] = jnp.zeros_like(o_ref)

    h = h_ref[...]
    g = jnp.dot(h, wg_ref[...].astype(BF16), preferred_element_type=F32)
    u = jnp.dot(h, wu_ref[...].astype(BF16), preferred_element_type=F32)
    a = (g * jax.nn.sigmoid(g) * u).astype(BF16)
    o_ref[...] += jnp.dot(a, wd_ref[...].astype(BF16), preferred_element_type=F32)

    @pl.when(j == pl.num_programs(1) - 1)
    def _():
        y = x_ref[...] + (HALF_STEP * gate_ref[...]) * o_ref[...]
        if final_norm:
            ms = jnp.mean(y * y, axis=-1, keepdims=True)
            y = y * lax.rsqrt(ms + RMS_EPS) * fg_ref[...]
        o_ref[...] = y


def _ffn_sublayer(x, mods, gain, wg, wu, wd, which, final_gain=None, tm=1024, tf=256):
    t, d = x.shape
    bm = mods[0].shape[0]
    tm = min(tm, t // bm)
    blocks_per_batch = (t // bm) // tm
    dff = wg.shape[2]
    final_norm = final_gain is not None
    fg = final_gain if final_norm else gain
    mod_spec = pl.BlockSpec((None, 1, d), lambda i, j: (i // blocks_per_batch, 0, 0))
    vec_spec = pl.BlockSpec((1, d), lambda i, j: (0, 0))
    return pl.pallas_call(
        functools.partial(_ffn_kernel, final_norm=final_norm),
        grid=(t // tm, dff // tf),
        in_specs=[pl.BlockSpec((tm, d), lambda i, j: (i, 0)),
                  mod_spec, mod_spec, mod_spec, vec_spec,
                  pl.BlockSpec((None, d, tf), lambda i, j: (which, 0, j)),
                  pl.BlockSpec((None, d, tf), lambda i, j: (which, 0, j)),
                  pl.BlockSpec((None, tf, d), lambda i, j: (which, j, 0)),
                  vec_spec],
        out_specs=pl.BlockSpec((tm, d), lambda i, j: (i, 0)),
        out_shape=jax.ShapeDtypeStruct((t, d), F32),
        scratch_shapes=[pltpu.VMEM((tm, d), BF16)],
        compiler_params=pltpu.CompilerParams(
            dimension_semantics=("parallel", "arbitrary"), vmem_limit_bytes=VMEM_LIMIT_BYTES),
        name="ffn_final" if final_norm else "ffn",
    )(x, *mods, gain.reshape(1, d), wg, wu, wd, fg.reshape(1, d))


def _proj_kernel(x_ref, shift_ref, scale_ref, gain_ref, w_ref, sw_ref, sb_ref, *rest, n_u, n_hy, row_len, part):
    o_refs, h_ref = rest[:-1], rest[-1]
    j = pl.program_id(1)

    @pl.when(j == 0)
    def _():
        h_ref[...] = _rms_mod(x_ref[...], gain_ref[...], shift_ref[...], scale_ref[...]).astype(BF16)

    tm, tn = h_ref.shape[0], w_ref.shape[1]

    def in_parts(o_ref, epilogue):
        for c in range(0, tn, part):
            p = jnp.dot(h_ref[...], w_ref[:, c:c + part].astype(BF16), preferred_element_type=F32)
            o_ref[:, c:c + part] = epilogue(p, c).astype(o_ref.dtype)

    def short_conv(p, c):
        col = lax.broadcasted_iota(jnp.int32, p.shape, 0) % row_len
        prev = jnp.where(col == 0, 0.0, pltpu.roll(p, 1, 0))
        nxt = jnp.where(col == row_len - 1, 0.0, pltpu.roll(p, tm - 1, 0))
        sw = sw_ref[:, c:c + part]
        return sb_ref[:, c:c + part] + prev * sw[0:1] + p * sw[1:2] + nxt * sw[2:3]

    if n_hy == 0:
        in_parts(o_refs[0], lambda p, c: p)
        return

    @pl.when(j < n_u)
    def _():
        in_parts(o_refs[0], lambda p, c: p)

    @pl.when((j >= n_u) & (j < n_u + n_hy))
    def _():
        in_parts(o_refs[1], short_conv)

    @pl.when(j >= n_u + n_hy)
    def _():
        in_parts(o_refs[2], lambda p, c: jax.nn.sigmoid(p))


def _in_proj(x, shift, scale, gain, w, short_w=None, short_b=None, n_u=D_S5, n_hy=0, row_len=GRID_W,
             tm=1024, tn=512, part=256):
    t, d = x.shape
    bm = shift.shape[0]
    tm = min(tm, t // bm)
    blocks_per_batch = (t // bm) // tm
    n = w.shape[1]
    assert tm % row_len == 0 and (t // bm) % tm == 0
    bu, bh = n_u // tn, n_hy // tn
    bg = n // tn - bu - bh
    mod_spec = pl.BlockSpec((None, 1, d), lambda i, j: (i // blocks_per_batch, 0, 0))
    out_shape = [jax.ShapeDtypeStruct((t, n_u), F32)]
    out_specs = [pl.BlockSpec((tm, tn), lambda i, j: (i, jnp.minimum(j, bu - 1)))]
    if bh:
        out_shape += [jax.ShapeDtypeStruct((t, n_hy), BF16), jax.ShapeDtypeStruct((t, bg * tn), BF16)]
        out_specs += [pl.BlockSpec((tm, tn), lambda i, j: (i, jnp.clip(j - bu, 0, bh - 1))),
                      pl.BlockSpec((tm, tn), lambda i, j: (i, jnp.maximum(j - bu - bh, 0)))]
        sw, sb = short_w, short_b.reshape(1, n_hy)
        hy_blk = lambda i, j: (0, jnp.clip(j - bu, 0, bh - 1))
    else:
        sw, sb = jnp.zeros((HY_SHORT, tn), F32), jnp.zeros((1, tn), F32)
        hy_blk = lambda i, j: (0, 0)
    return pl.pallas_call(
        functools.partial(_proj_kernel, n_u=bu, n_hy=bh, row_len=row_len, part=part),
        grid=(t // tm, n // tn),
        in_specs=[pl.BlockSpec((tm, d), lambda i, j: (i, 0)),
                  mod_spec, mod_spec,
                  pl.BlockSpec((1, d), lambda i, j: (0, 0)),
                  pl.BlockSpec((d, tn), lambda i, j: (0, j)),
                  pl.BlockSpec((HY_SHORT, tn), hy_blk),
                  pl.BlockSpec((1, tn), hy_blk)],
        out_specs=out_specs,
        out_shape=out_shape,
        scratch_shapes=[pltpu.VMEM((tm, d), BF16)],
        compiler_params=pltpu.CompilerParams(
            dimension_semantics=("parallel", "arbitrary"), vmem_limit_bytes=VMEM_LIMIT_BYTES),
        name="in_proj",
    )(x, shift, scale, gain.reshape(1, d), w, sw, sb)


def _gelu_tanh(x):
    return 0.5 * x * (1.0 + jnp.tanh(math.sqrt(2.0 / math.pi) * (x + 0.044715 * (x * x * x))))


def _merge_kernel(x_ref, gate_ref, ys_ref, yh_ref, ga_ref, gb_ref, wpa_lo_ref, wpa_hi_ref, wpb_ref,
                  wout_ref, o_ref, s_ref, acc_ref):
    j = pl.program_id(1)

    @pl.when(j == 0)
    def _():
        s_ref[...] = _gelu_tanh(ys_ref[...].astype(F32)).astype(BF16)
        acc_ref[...] = jnp.zeros_like(acc_ref)

    s = s_ref[...]
    pa_lo = jnp.dot(s, wpa_lo_ref[...], preferred_element_type=F32)
    pa_hi = jnp.dot(s, wpa_hi_ref[...], preferred_element_type=F32)
    y_a = pa_lo * jax.nn.sigmoid(pa_hi)
    y_b = jnp.dot(yh_ref[...], wpb_ref[...], preferred_element_type=F32)
    m = ga_ref[...].astype(F32) * y_a + gb_ref[...].astype(F32) * y_b
    acc_ref[...] += jnp.dot(m.astype(BF16), wout_ref[...], preferred_element_type=F32)

    @pl.when(j == pl.num_programs(1) - 1)
    def _():
        o_ref[...] = x_ref[...] + gate_ref[...] * acc_ref[...]


def _merge(x, gate, y_s5, y_hy, sig_gates, w_pa, w_pb, w_out, tm=512, tn=512):
    t, d = x.shape
    bm = gate.shape[0]
    blocks_per_batch = (t // bm) // tm
    nj = d // tn
    ds5 = y_s5.shape[1]
    dhy = y_hy.shape[1]
    return pl.pallas_call(
        _merge_kernel,
        grid=(t // tm, nj),
        in_specs=[pl.BlockSpec((tm, d), lambda i, j: (i, 0)),
                  pl.BlockSpec((None, 1, d), lambda i, j: (i // blocks_per_batch, 0, 0)),
                  pl.BlockSpec((tm, ds5), lambda i, j: (i, 0)),
                  pl.BlockSpec((tm, dhy), lambda i, j: (i, 0)),
                  pl.BlockSpec((tm, tn), lambda i, j: (i, j)),
                  pl.BlockSpec((tm, tn), lambda i, j: (i, nj + j)),
                  pl.BlockSpec((ds5, tn), lambda i, j: (0, j)),
                  pl.BlockSpec((ds5, tn), lambda i, j: (0, nj + j)),
                  pl.BlockSpec((dhy, tn), lambda i, j: (0, j)),
                  pl.BlockSpec((tn, d), lambda i, j: (j, 0))],
        out_specs=pl.BlockSpec((tm, d), lambda i, j: (i, 0)),
        out_shape=jax.ShapeDtypeStruct((t, d), F32),
        scratch_shapes=[pltpu.VMEM((tm, ds5), BF16), pltpu.VMEM((tm, d), F32)],
        compiler_params=pltpu.CompilerParams(
            dimension_semantics=("parallel", "arbitrary"), vmem_limit_bytes=VMEM_LIMIT_BYTES),
        name="merge",
    )(x, gate, y_s5, y_hy, sig_gates, sig_gates, w_pa, w_pa, w_pb, w_out)


def _s5_weights_kernel(par_ref, bt_ref, c_ref, d_ref, bpow_ref, toep_ref, cpow_ref, tab_ref, ca_ref, *, n_steps):
    t, h, p = S5_CHUNK, S5_GROUP, S5_STATE
    lanes = 2 * p
    hp = lax.Precision.HIGHEST
    sgn = jnp.where(lax.broadcasted_iota(jnp.int32, (1, lanes), 1) < p, -1.0, 1.0)
    par = par_ref[...]
    gsum = None
    for d in range(S5_DIRS):
        lr = jnp.minimum(par[3 * d:3 * d + 1], LAMBDA_RE_MAX)
        li = par[3 * d + 1:3 * d + 2]
        dt = jnp.exp(par[3 * d + 2:3 * d + 3])
        zr, zi = lr * dt, li * dt

        def apow(j):
            mag = jnp.exp(j * zr)
            return mag * jnp.cos(j * zi), sgn * (mag * jnp.sin(j * zi))

        def cmul(x, a1, a2):
            return x * a1 + pltpu.roll(x, p, 1) * a2

        a1, a2 = apow(lax.broadcasted_iota(jnp.int32, (t + 1, 1), 0).astype(F32))
        nr, ni = a1[1:2] - 1.0, sgn * a2[1:2]
        den = lr * lr + li * li
        f_re = (nr * lr + ni * li) / den
        f_im = (ni * lr - nr * li) / den
        bbar = cmul(bt_ref[d], f_re, sgn * f_im)
        cc = c_ref[d]
        ca = [cmul(cc, a1[j:j + 1], a2[j:j + 1]) * (-sgn) for j in range(t + 1)]
        ca_ref[...] = jnp.zeros_like(ca_ref)
        for k in range(t):
            e_b, e_c = (t - 1 - k, k + 1) if d == 0 else (k, t - k)
            bpow_ref[k * h:(k + 1) * h, d * lanes:(d + 1) * lanes] = (
                cmul(bbar, a1[e_b:e_b + 1], a2[e_b:e_b + 1]).astype(BF16))
            cpow_ref[k * h:(k + 1) * h, d * lanes:(d + 1) * lanes] = ca[e_c].astype(BF16)
            l = t - 1 + k if d == 0 else t - 1 - k
            ca_ref[l * h:(l + 1) * h, :] = ca[k]
        g = lax.dot_general(bbar, ca_ref[...], (((1,), (1,)), ((), ())), preferred_element_type=F32, precision=hp)
        gsum = g if gsum is None else gsum + g
        for s in range(n_steps):
            s1, s2 = apow(float(t * 2 ** s))
            r = d * 2 * n_steps + 2 * s
            tab_ref[r:r + 1, :] = s1
            tab_ref[r + 1:r + 2, :] = s2
    wide = gsum.shape[1]
    col = lax.broadcasted_iota(jnp.int32, (h, wide), 1)
    row = lax.broadcasted_iota(jnp.int32, (h, wide), 0)
    gsum = gsum + jnp.where(col - (t - 1) * h == row, d_ref[...], 0.0)
    for k in range(t):
        off = (t - 1 - k) * h
        shifted = gsum if off == 0 else pltpu.roll(gsum, wide - off, 1)
        toep_ref[k * h:(k + 1) * h, :] = shifted[:, :t * h].astype(BF16)


def _s5_weights(lam_re, lam_im, log_dt, b_re, b_im, c_re, c_im, d_skip, n_steps):
    g, p, h, t = S5_GROUPS, S5_STATE, S5_GROUP, S5_CHUNK
    cat2 = lambda a: jnp.concatenate([a, a], axis=-1)
    par = jnp.stack([cat2(lam_re), cat2(lam_im), jnp.broadcast_to(log_dt[..., None], (S5_DIRS, g, 2 * p))], axis=1)
    par = par.transpose(2, 0, 1, 3).reshape(g, 3 * S5_DIRS, 2 * p)
    btc = jnp.concatenate([b_re, b_im], axis=2).transpose(1, 0, 3, 2)
    ccat = jnp.concatenate([c_re, c_im], axis=3).transpose(1, 0, 2, 3)
    wide = 2 * t * h
    drow = jnp.zeros((g, 1, wide), F32).at[:, 0, (t - 1) * h:t * h].set(d_skip.reshape(g, h))
    w = t * h
    sq = pl.BlockSpec((None, w, w), lambda i: (i, 0, 0))
    return pl.pallas_call(
        functools.partial(_s5_weights_kernel, n_steps=n_steps),
        grid=(g,),
        in_specs=[pl.BlockSpec((None, 3 * S5_DIRS, 2 * p), lambda i: (i, 0, 0)),
                  pl.BlockSpec((None, S5_DIRS, h, 2 * p), lambda i: (i, 0, 0, 0)),
                  pl.BlockSpec((None, S5_DIRS, h, 2 * p), lambda i: (i, 0, 0, 0)),
                  pl.BlockSpec((None, 1, wide), lambda i: (i, 0, 0))],
        out_specs=[sq, sq, sq, pl.BlockSpec((None, 4 * n_steps, 2 * p), lambda i: (i, 0, 0))],
        out_shape=[jax.ShapeDtypeStruct((g, w, w), BF16)] * 3 + [jax.ShapeDtypeStruct((g, 4 * n_steps, 2 * p), F32)],
        scratch_shapes=[pltpu.VMEM((wide, 2 * p), F32)],
        compiler_params=pltpu.CompilerParams(
            dimension_semantics=("parallel",), vmem_limit_bytes=VMEM_LIMIT_BYTES),
        name="s5_weights",
    )(par, btc, ccat, drow)


def _s5_kernel(u_ref, uc_ref, bpow_ref, toep_ref, cpow_ref, tab_ref, y_ref, ucat_ref, ucc_ref, yacc_ref,
               *, bsz, n_steps):
    t, gw = S5_CHUNK, S5_GROUP
    per = LANE // gw
    n_lat = u_ref.shape[0] // (bsz * t)
    n_ctx = uc_ref.shape[0] // (bsz * t)
    n_ch = n_lat + n_ctx
    rows = bsz * n_ch
    half = 2 * S5_STATE
    for k in range(t):
        ucat_ref[:, k * LANE:(k + 1) * LANE] = u_ref[pl.ds(k, bsz * n_lat, stride=t), :]
        ucc_ref[:, k * LANE:(k + 1) * LANE] = uc_ref[pl.ds(k, bsz * n_ctx, stride=t), :]
    yacc_ref[...] = jnp.zeros_like(yacc_ref)
    rib = lax.broadcasted_iota(jnp.int32, (rows, half), 0) % n_ch
    lane = lax.broadcasted_iota(jnp.int32, (1, LANE), 1)

    def cmul_add(acc, sh, a1, a2):
        return acc + a1 * sh + a2 * pltpu.roll(sh, S5_STATE, 1)

    def gather(src_ref, gl):
        cols = []
        for j in range(t // per):
            acc = None
            for tt in range(per):
                k = j * per + tt
                r = pltpu.roll(src_ref[:, k * LANE:(k + 1) * LANE], (gw * tt - gw * gl) % LANE, 1)
                acc = r if acc is None else jnp.where((lane >= gw * tt) & (lane < gw * (tt + 1)), r, acc)
            cols.append(acc)
        return jnp.concatenate(cols, axis=1).astype(BF16)

    def group(gl, carry):
        ul = gather(ucat_ref, gl)
        uc = gather(ucc_ref, gl)
        bpow = bpow_ref[gl]
        zl = jnp.dot(ul, bpow, preferred_element_type=F32)
        zc = jnp.dot(uc, bpow, preferred_element_type=F32)
        fparts, bparts = [], []
        for b in range(bsz):
            lat = slice(b * n_lat, (b + 1) * n_lat)
            ctx = slice(b * n_ctx, (b + 1) * n_ctx)
            fparts += [zc[ctx, :half], zl[lat, :half]]
            bparts += [zl[lat, half:], zc[ctx, half:]]
        fw = jnp.concatenate(fparts, axis=0)
        bw = jnp.concatenate(bparts, axis=0)
        tab = tab_ref[gl]
        for s in range(n_steps):
            d = 1 << s
            sh = jnp.where(rib >= d, pltpu.roll(fw, d, 0), 0.0)
            fw = cmul_add(fw, sh, tab[2 * s:2 * s + 1], tab[2 * s + 1:2 * s + 2])
            o = 2 * n_steps
            sh = jnp.where(rib < n_ch - d, pltpu.roll(bw, rows - d, 0), 0.0)
            bw = cmul_add(bw, sh, tab[o + 2 * s:o + 2 * s + 1], tab[o + 2 * s + 1:o + 2 * s + 2])
        fe = jnp.where(rib >= 1, pltpu.roll(fw, 1, 0), 0.0)
        be = jnp.where(rib < n_ch - 1, pltpu.roll(bw, rows - 1, 0), 0.0)
        fl = jnp.concatenate([fe[b * n_ch + n_ctx:(b + 1) * n_ch] for b in range(bsz)], axis=0)
        bl = jnp.concatenate([be[b * n_ch:b * n_ch + n_lat] for b in range(bsz)], axis=0)
        st = jnp.concatenate([fl, bl], axis=1).astype(BF16)
        y = (jnp.dot(ul, toep_ref[gl], preferred_element_type=F32)
             + lax.dot_general(st, cpow_ref[gl], (((1,), (1,)), ((), ())), preferred_element_type=F32))
        mine = (lane >= gw * gl) & (lane < gw * (gl + 1))
        for i in range(t):
            src = y[:, (i // per) * LANE:(i // per + 1) * LANE]
            r = pltpu.roll(src, (gw * gl - gw * (i % per)) % LANE, 1)
            blk = slice(i * LANE, (i + 1) * LANE)
            yacc_ref[:, blk] = jnp.where(mine, r, yacc_ref[:, blk])
        return carry

    lax.fori_loop(0, S5_GROUPS_PER_COL, group, 0)
    for i in range(t):
        y_ref[pl.ds(i, bsz * n_lat, stride=t), :] = yacc_ref[:, i * LANE:(i + 1) * LANE]


def _s5_mix(u, uc, bpw, toep, cpw, tab, bsz):
    rl, dm = u.shape
    rc = uc.shape[0]
    t = S5_CHUNK
    w = t * S5_GROUP
    n_steps = tab.shape[1] // 4
    gpc = S5_GROUPS_PER_COL
    wspec = pl.BlockSpec((gpc, w, w), lambda i: (i, 0, 0))
    return pl.pallas_call(
        functools.partial(_s5_kernel, bsz=bsz, n_steps=n_steps),
        grid=(dm // LANE,),
        in_specs=[pl.BlockSpec((rl, LANE), lambda i: (0, i)),
                  pl.BlockSpec((rc, LANE), lambda i: (0, i)),
                  wspec, wspec, wspec,
                  pl.BlockSpec((gpc, 4 * n_steps, tab.shape[2]), lambda i: (i, 0, 0))],
        out_specs=pl.BlockSpec((rl, LANE), lambda i: (0, i)),
        out_shape=jax.ShapeDtypeStruct(u.shape, F32),
        scratch_shapes=[pltpu.VMEM((rl // t, t * LANE), F32), pltpu.VMEM((rc // t, t * LANE), F32),
                        pltpu.VMEM((rl // t, t * LANE), F32)],
        compiler_params=pltpu.CompilerParams(
            dimension_semantics=("parallel",), vmem_limit_bytes=VMEM_LIMIT_BYTES),
        name="s5_mix",
    )(u, uc, bpw, toep, cpw, tab)


def _s5_bidirectional(u, u_ctx, lam_re, lam_im, log_dt, b_re, b_im, c_re, c_im, d_skip):
    bsz, length, dm = u.shape
    ctx_len = u_ctx.shape[1]
    t = S5_CHUNK
    n_lat, n_ctx = length // t, ctx_len // t
    n_steps = max(1, math.ceil(math.log2(n_lat + n_ctx)))
    bpw, toep, cpw, tab = _s5_weights(lam_re, lam_im, log_dt, b_re, b_im, c_re, c_im, d_skip, n_steps)
    y = _s5_mix(u.reshape(bsz * length, dm), u_ctx.reshape(bsz * ctx_len, dm), bpw, toep, cpw, tab, bsz)
    return y.reshape(bsz, length, dm)


def _dft_constants(real_input=False):
    n1 = np.arange(FFT_N1)
    n2 = np.arange(FFT_N2)
    half = FFT_N1 // 2
    th = 2 * np.pi * np.outer(n1, n1) / FFT_N1
    c1, s1 = np.cos(th), np.sin(th)
    if real_input:
        w1 = np.concatenate([c1, -s1], axis=1)
    else:
        w1 = np.concatenate([np.concatenate([c1[:half], -s1[:half]], axis=1),
                             np.concatenate([s1[:half], c1[:half]], axis=1)], axis=0)
    z = np.zeros_like(w1)
    w1p = np.block([[w1, z], [z, w1]])
    ph = 2 * np.pi * np.outer(n2, n1) / FFT_N
    t1 = np.concatenate([np.cos(ph), np.cos(ph)], axis=1)
    t2 = np.concatenate([np.sin(ph), -np.sin(ph)], axis=1)
    ps = 2 * np.pi * np.outer(n2, n2) / FFT_N2
    f2 = np.concatenate([np.cos(ps), -np.sin(ps)], axis=1)
    g2 = np.concatenate([np.cos(ps), np.sin(ps)], axis=1)
    c2, s2 = np.cos(ph).T, np.sin(ph).T
    wi = np.concatenate([np.concatenate([c1[:, :half], s1[:, :half]], axis=1),
                         np.concatenate([-s1[:, :half], c1[:, :half]], axis=1)], axis=0) / FFT_N
    zi = np.zeros_like(wi)
    wi2 = np.stack([np.concatenate([wi, zi], axis=1), np.concatenate([zi, wi], axis=1)])
    as_b = lambda a: jnp.asarray(a, F32).astype(BF16)
    as_f = lambda a: jnp.asarray(a, F32)
    return [as_b(w1p), as_f(t1), as_f(t2), as_b(f2), as_b(g2), as_f(c2), as_f(s2), as_b(wi2)]


def _fwd_spectrum(xp, w1p, t1, t2, f2):
    cp = xp.shape[0]
    hn = FFT_N1
    a = jnp.dot(xp.reshape(cp * FFT_N2, 128).astype(BF16), w1p, preferred_element_type=F32)
    out = []
    for par in range(2):
        ap = a[:, par * 128:(par + 1) * 128]
        ap = ap.reshape(cp, FFT_N2, 128) * t1 + pltpu.roll(ap, hn, 1).reshape(cp, FFT_N2, 128) * t2
        at = jnp.swapaxes(ap, 1, 2)
        p = jnp.dot(at.reshape(cp * 128, FFT_N2).astype(BF16), f2, preferred_element_type=F32)
        p = p.reshape(cp, 128, 2 * FFT_N2)
        out.append((p[:, :hn, :FFT_N2] - p[:, hn:, FFT_N2:], p[:, :hn, FFT_N2:] + p[:, hn:, :FFT_N2]))
    return out


def _inv_time(yre, yim, g2, c2, s2, wi_par):
    cp = yre.shape[0]
    hn = FFT_N1
    y = jnp.concatenate([yre, yim], axis=1).reshape(cp * 128, FFT_N2).astype(BF16)
    q = jnp.dot(y, g2, preferred_element_type=F32).reshape(cp, 128, 2 * FFT_N2)
    bre = q[:, :hn, :FFT_N2] - q[:, hn:, FFT_N2:]
    bim = q[:, :hn, FFT_N2:] + q[:, hn:, :FFT_N2]
    b2 = jnp.concatenate([bre * c2 - bim * s2, bre * s2 + bim * c2], axis=1)
    bt = jnp.swapaxes(b2, 1, 2)
    return jnp.dot(bt.reshape(cp * FFT_N2, 128).astype(BF16), wi_par, preferred_element_type=F32)


def _hyena_kernel(z_ref, g1_ref, g2_ref, kf_ref, bias_ref, w1p_ref, t1_ref, t2_ref, f2_ref, gi_ref, c2_ref,
                  s2_ref, wi_ref, o_ref, zt_ref, g1t_ref, g2t_ref, ot_ref, stage_ref):
    k = pl.program_id(1)
    n_s = z_ref.shape[0] // FFT_N2
    cp = HY_PAIRS_PER_STEP

    def to_tiles(x_ref, t_ref):
        for s in range(n_s):
            xs = x_ref[s * FFT_N2:(s + 1) * FFT_N2, :].astype(F32)
            stage_ref[pl.ds(s, LANE, stride=HY_TILE_PITCH), :] = xs.T
        st = stage_ref[...].reshape(HY_PAIRS_PER_COL, 2 * HY_TILE_PITCH, FFT_N2)
        both = jnp.concatenate([st[:, :n_s], st[:, HY_TILE_PITCH:HY_TILE_PITCH + n_s]], axis=1)
        t_ref[...] = jnp.swapaxes(both, 1, 2)

    @pl.when(k == 0)
    def _():
        to_tiles(z_ref, zt_ref)
        to_tiles(g1_ref, g1t_ref)
        to_tiles(g2_ref, g2t_ref)

    sl = pl.ds(pl.multiple_of(k * cp, cp), cp)
    z = zt_ref[sl]
    gates = (g1t_ref, g2t_ref)
    for o in range(HY_ORDER):
        spec = _fwd_spectrum(z, w1p_ref[...], t1_ref[...], t2_ref[...], f2_ref[...])
        conv = None
        for par in range(2):
            xre, xim = spec[par]
            kre = kf_ref[o, par, :, :FFT_N1, :]
            kim = kf_ref[o, par, :, FFT_N1:, :]
            part = _inv_time(xre * kre - xim * kim, xre * kim + xim * kre,
                             gi_ref[...], c2_ref[...], s2_ref[...], wi_ref[par])
            conv = part if conv is None else conv + part
        z = gates[o][sl] * (conv.reshape(cp, FFT_N2, LANE) + bias_ref[o] * z)
    ot_ref[sl] = z

    @pl.when(k == pl.num_programs(1) - 1)
    def _():
        back = jnp.swapaxes(ot_ref[...], 1, 2)
        for c in range(LANE):
            stage_ref[c * HY_TILE_PITCH:c * HY_TILE_PITCH + n_s, :] = back[c // 2, (c % 2) * n_s:(c % 2 + 1) * n_s, :]
        for s in range(n_s):
            rows = stage_ref[pl.ds(s, LANE, stride=HY_TILE_PITCH), :]
            o_ref[s * FFT_N2:(s + 1) * FFT_N2, :] = rows.T.astype(o_ref.dtype)


def _hyena_conv(us, kf, biasp):
    t, _ = us.shape
    consts = _dft_constants()
    ncol = D_HY // LANE
    nsub = HY_PAIRS_PER_COL // HY_PAIRS_PER_STEP
    full = lambda a: pl.BlockSpec(a.shape, lambda j, k: (0,) * a.ndim)
    nat = lambda off: pl.BlockSpec((t, LANE), lambda j, k: (0, off + j))
    tiles = pltpu.VMEM((HY_PAIRS_PER_COL, FFT_N2, LANE), F32)
    return pl.pallas_call(
        _hyena_kernel,
        grid=(ncol, nsub),
        in_specs=[nat(0), nat(ncol), nat(2 * ncol),
                  pl.BlockSpec((HY_ORDER, 2, HY_PAIRS_PER_STEP, 2 * FFT_N1, FFT_N2),
                               lambda j, k: (0, 0, j * nsub + k, 0, 0)),
                  pl.BlockSpec((HY_ORDER, HY_PAIRS_PER_STEP, 1, LANE), lambda j, k: (0, j * nsub + k, 0, 0))]
                 + [full(a) for a in consts],
        out_specs=pl.BlockSpec((t, LANE), lambda j, k: (0, j)),
        out_shape=jax.ShapeDtypeStruct((t, D_HY), us.dtype),
        scratch_shapes=[tiles, tiles, tiles, tiles, pltpu.VMEM((LANE * HY_TILE_PITCH, FFT_N2), F32)],
        compiler_params=pltpu.CompilerParams(
            dimension_semantics=("parallel", "arbitrary"), vmem_limit_bytes=VMEM_LIMIT_BYTES),
        name="hyena_conv",
    )(us, us, us, kf, biasp, *consts)


def _filter_time_kernel(w1t_ref, w1c_ref, w1s_ref, b1_ref, w2_ref, b2_ref, w3_ref, b3_ref, fr_ref,
                        wf_ref, wb_ref, df_ref, db_ref, o_ref, h_ref, k_ref, *, length):
    n_fft = 2 * length
    hp = lax.Precision.HIGHEST

    @pl.when(pl.program_id(0) == 0)
    def _():
        pos = lax.broadcasted_iota(jnp.int32, (1, n_fft), 1)
        lag = jnp.where(pos < length, pos, n_fft - pos).astype(F32)
        t = lag / float(length - 1)
        w = (2.0 * math.pi / length) * lag
        band_step = (HY_BANDS - 1 - 1e-4) / (HY_BANDS - 1)
        bands = 1e-4 + band_step * lax.broadcasted_iota(jnp.int32, (HY_BANDS, 1), 0).astype(F32)
        ang = bands * w
        fr = fr_ref[...]
        h = (w1t_ref[...] * t + jnp.dot(w1c_ref[...], jnp.cos(ang), preferred_element_type=F32, precision=hp)
             - jnp.dot(w1s_ref[...], jnp.sin(ang), preferred_element_type=F32, precision=hp))
        h = jnp.sin(fr * (h + b1_ref[...]))
        h = jnp.sin(fr * (jnp.dot(w2_ref[...], h, preferred_element_type=F32, precision=hp) + b2_ref[...]))
        h = jnp.sin(fr * (jnp.dot(w3_ref[...], h, preferred_element_type=F32, precision=hp) + b3_ref[...]))
        hi = h.astype(BF16)
        h_ref[0] = hi
        h_ref[1] = (h - hi.astype(F32)).astype(BF16)

    def dot3(w, lo, hi_):
        w_hi = w.astype(BF16)
        w_lo = (w - w_hi.astype(F32)).astype(BF16)
        h_hi, h_lo = h_ref[0, :, lo:hi_], h_ref[1, :, lo:hi_]
        return (jnp.dot(w_hi, h_hi, preferred_element_type=F32) + jnp.dot(w_hi, h_lo, preferred_element_type=F32)
                + jnp.dot(w_lo, h_hi, preferred_element_type=F32))

    pos = lax.broadcasted_iota(jnp.int32, (1, length), 1)
    tf = pos.astype(F32) / float(length - 1)
    tb = (length - pos).astype(F32) / float(length - 1)
    kf = dot3(wf_ref[...], 0, length) * jnp.exp(-tf * df_ref[...])
    kb = dot3(wb_ref[...], length, n_fft) * jnp.exp(-tb * db_ref[...])
    kb = jnp.where(pos == 0, 0.0, kb)
    inv = 1.0 / (jnp.sum(jnp.abs(kf), axis=1, keepdims=True) + jnp.sum(jnp.abs(kb), axis=1, keepdims=True))
    k_ref[:, :length] = kf * inv
    k_ref[:, length:] = kb * inv
    cb = wf_ref.shape[0]
    for n1 in range(FILTER_TILE_ROWS):
        row = k_ref[:, n1 * FFT_N2:(n1 + 1) * FFT_N2] if n1 < FFT_N1 else jnp.zeros((cb, FFT_N2), F32)
        o_ref[pl.ds(n1, cb, stride=FILTER_TILE_ROWS), :] = row


def _filter_time(length, w1, b1, w2, b2, w3, b3, freq, w_out, cb=128):
    col = lambda v: v.reshape(-1, 1)
    w1t = w1.T
    n_ch = w_out.shape[1]
    deltas = jnp.abs(jnp.linspace(math.log(HY_TARGET) / HY_SLOW_PCT, math.log(HY_TARGET) / HY_FAST_PCT,
                                  n_ch, dtype=F32)).reshape(n_ch, 1)
    wot = w_out.T
    nb = D_HY // cb
    small = lambda a: pl.BlockSpec(a.shape, lambda i: (0,) * a.ndim)
    fwd = lambda i: ((i // nb) * HY_DIRS * nb + i % nb, 0)
    bwd = lambda i: ((i // nb) * HY_DIRS * nb + nb + i % nb, 0)
    ins = [w1t[:, 0:1], w1t[:, 1:1 + HY_BANDS], w1t[:, 1 + HY_BANDS:], col(b1), w2.T, col(b2), w3.T, col(b3),
           col(freq)]
    hy_ff = w2.shape[0]
    return pl.pallas_call(
        functools.partial(_filter_time_kernel, length=length),
        grid=(HY_ORDER * nb,),
        in_specs=[small(a) for a in ins] + [pl.BlockSpec((cb, hy_ff), fwd), pl.BlockSpec((cb, hy_ff), bwd),
                                            pl.BlockSpec((cb, 1), fwd), pl.BlockSpec((cb, 1), bwd)],
        out_specs=pl.BlockSpec((cb * FILTER_TILE_ROWS, FFT_N2), lambda i: (i, 0)),
        out_shape=jax.ShapeDtypeStruct((HY_ORDER * D_HY * FILTER_TILE_ROWS, FFT_N2), F32),
        scratch_shapes=[pltpu.VMEM((2, hy_ff, 2 * length), BF16), pltpu.VMEM((cb, 2 * length), F32)],
        compiler_params=pltpu.CompilerParams(
            dimension_semantics=("arbitrary",), vmem_limit_bytes=VMEM_LIMIT_BYTES),
        name="hyena_filter_time",
    )(*ins, wot, wot, deltas, deltas)


def _filter_spec_kernel(k_ref, w1p_ref, t1_ref, t2_ref, f2_ref, o_ref):
    cb = k_ref.shape[0] // FILTER_TILE_ROWS
    kt = k_ref[...].reshape(cb, FILTER_TILE_ROWS, FFT_N2)[:, :FFT_N1, :]
    xp = jnp.swapaxes(kt.reshape(cb // 2, 2 * FFT_N1, FFT_N2), 1, 2)
    spec = _fwd_spectrum(xp, w1p_ref[...], t1_ref[...], t2_ref[...], f2_ref[...])
    for par in range(2):
        o_ref[par, :, :FFT_N1, :] = spec[par][0]
        o_ref[par, :, FFT_N1:, :] = spec[par][1]


def _filter_spectrum(kt, cb=32):
    consts = _dft_constants(real_input=True)[:4]
    nb = D_HY // cb
    full = lambda a: pl.BlockSpec(a.shape, lambda i: (0,) * a.ndim)
    return pl.pallas_call(
        _filter_spec_kernel,
        grid=(HY_ORDER * nb,),
        in_specs=[pl.BlockSpec((cb * FILTER_TILE_ROWS, FFT_N2), lambda i: (i, 0))] + [full(a) for a in consts],
        out_specs=pl.BlockSpec((None, 2, cb // 2, 2 * FFT_N1, FFT_N2), lambda i: (i // nb, 0, i % nb, 0, 0)),
        out_shape=jax.ShapeDtypeStruct((HY_ORDER, 2, D_HY // 2, 2 * FFT_N1, FFT_N2), F32),
        compiler_params=pltpu.CompilerParams(
            dimension_semantics=("parallel",), vmem_limit_bytes=VMEM_LIMIT_BYTES),
        name="hyena_filter_spectrum",
    )(kt, *consts)


def _hyena(us, w1, b1, w2, b2, w3, b3, freq, w_out, bias):
    bsz, length, _ = us.shape
    assert 2 * length == FFT_N and bsz == 2, "one complex transform carries exactly two batch rows"
    kf = _filter_spectrum(_filter_time(length, w1, b1, w2, b2, w3, b3, freq, w_out))
    biasp = jnp.repeat(bias.reshape(HY_ORDER, D_HY // 2, 1, 2), FFT_N1, axis=-1)
    return _hyena_conv(us.reshape(bsz * length, -1), kf, biasp).reshape(bsz, length, D_HY)


def kernel(x, c, ctx, c_ctx, w_ada, b_ada, norm_g, ffn_w_gate, ffn_w_up, ffn_w_down, w_in,
           s5_lam_re, s5_lam_im, s5_log_dt, s5_b_re, s5_b_im, s5_c_re, s5_c_im, s5_d,
           hy_short_w, hy_short_b, hy_w1, hy_b1, hy_w2, hy_b2, hy_w3, hy_b3, hy_freq, hy_w_out,
           hy_bias, w_pa, w_pb, w_out, final_g):
    bsz, seq, d = x.shape
    ctx_len = ctx.shape[1]
    n_rows = seq // GRID_W
    depth = w_ada.shape[0]
    assert depth == 1, "context-token outputs are only dropped by the last layer"
    l = 0

    c_rows = jnp.concatenate([c, c_ctx[None, :], jnp.zeros((8 - bsz - 1, d), F32)], axis=0)
    mod_all = _ada_mod(c_rows, w_ada[l], b_ada[l])
    mod = mod_all[:bsz].reshape(bsz, N_SUB, N_MOD, 1, d)
    mod_c = mod_all[bsz:bsz + 1].reshape(1, N_SUB, N_MOD, 1, d)

    def mods(m, sub):
        return tuple(m[:, sub, k] for k in range(N_MOD))

    wg, wu, wd = ffn_w_gate[l], ffn_w_up[l], ffn_w_down[l]
    w_in_b = w_in[l]

    xt = x.reshape(bsz * seq, d)
    ct = ctx.reshape(bsz * ctx_len, d)

    xt = _ffn_sublayer(xt, mods(mod, 0), norm_g[l, 0], wg, wu, wd, 0)
    ct = _ffn_sublayer(ct, mods(mod_c, 0), norm_g[l, 0], wg, wu, wd, 0)

    assert GRID_W * n_rows == seq
    u_s5, us_hy, sig_gates = _in_proj(xt, mod[:, 1, 0], mod[:, 1, 1], norm_g[l, 1], w_in_b,
                                      hy_short_w[l], hy_short_b[l], n_u=I_HY, n_hy=I_GA - I_HY)
    (u_ctx,) = _in_proj(ct, mod_c[:, 1, 0], mod_c[:, 1, 1], norm_g[l, 1], w_in_b[:, :D_S5])

    y_s5 = _s5_bidirectional(u_s5.reshape(bsz, seq, D_S5), u_ctx.reshape(bsz, ctx_len, D_S5),
                             s5_lam_re[l], s5_lam_im[l], s5_log_dt[l],
                             s5_b_re[l], s5_b_im[l], s5_c_re[l], s5_c_im[l], s5_d[l])
    y_hy = _hyena(us_hy.reshape(bsz, seq, I_GA - I_HY),
                  hy_w1[l], hy_b1[l], hy_w2[l], hy_b2[l], hy_w3[l], hy_b3[l], hy_freq[l],
                  hy_w_out[l], hy_bias[l])

    xt = _merge(xt, mod[:, 1, 2], y_s5.reshape(bsz * seq, D_S5), y_hy.reshape(bsz * seq, D_HY),
                sig_gates, w_pa[l].astype(BF16), w_pb[l].astype(BF16), w_out[l].astype(BF16))

    xt = _ffn_sublayer(xt, mods(mod, 2), norm_g[l, 2], wg, wu, wd, 1, final_gain=final_g)
    return xt.reshape(bsz, seq, d)
```

```python
import functools
import math

import jax
import jax.numpy as jnp
import numpy as np
from jax import lax
from jax.experimental import pallas as pl
from jax.experimental.pallas import tpu as pltpu

F32 = jnp.float32
BF16 = jnp.bfloat16

D_MODEL = 2048
GRID_W = 64
D_S5 = 1024
S5_GROUP = 16
S5_GROUPS = D_S5 // S5_GROUP
S5_STATE = 64
S5_DIRS = 2
LAMBDA_RE_MAX = -1e-4
S5_CHUNK = 16
LANE = 128
S5_GROUPS_PER_COL = LANE // S5_GROUP
D_HY = 1024
HY_ORDER = 2
HY_DIRS = 2
HY_SHORT = 3
HY_EMB = 33
HY_BANDS = (HY_EMB - 1) // 2
HY_TARGET = 1e-2
HY_FAST_PCT = 0.3
HY_SLOW_PCT = 1.5
FFT_N1 = 64
FFT_N2 = 128
FFT_N = FFT_N1 * FFT_N2
FILTER_TILE_ROWS = FFT_N1 + 8
HY_TILE_PITCH = FILTER_TILE_ROWS
HY_PAIRS_PER_COL = LANE // 2
HY_PAIRS_PER_STEP = 16
I_HY = D_S5
I_GA = D_S5 + (HY_ORDER + 1) * D_HY
I_GB = I_GA + D_MODEL
D_IN = I_GB + D_MODEL
D_FF = 5632
N_SUB = 3
N_MOD = 3
HALF_STEP = 0.5
RMS_EPS = 1e-6

VMEM_LIMIT_BYTES = 58 * 1024 * 1024


def _rms_mod(x, gain, shift, scale):
    ms = jnp.mean(x * x, axis=-1, keepdims=True)
    y = x * lax.rsqrt(ms + RMS_EPS) * gain
    return y * (1.0 + scale) + shift


def _ada_kernel(c_ref, w_ref, b_ref, o_ref):
    c = c_ref[...]
    a = c * jax.nn.sigmoid(c)
    o_ref[...] = jnp.dot(a, w_ref[...], preferred_element_type=F32,
                         precision=lax.Precision.HIGHEST) + b_ref[...]


def _ada_mod(c_rows, w, b, tn=2048):
    rows, d = c_rows.shape
    n = w.shape[1]
    return pl.pallas_call(
        _ada_kernel,
        grid=(n // tn,),
        in_specs=[pl.BlockSpec((rows, d), lambda j: (0, 0)),
                  pl.BlockSpec((d, tn), lambda j: (0, j)),
                  pl.BlockSpec((1, tn), lambda j: (0, j))],
        out_specs=pl.BlockSpec((rows, tn), lambda j: (0, j)),
        out_shape=jax.ShapeDtypeStruct((rows, n), F32),
        compiler_params=pltpu.CompilerParams(
            dimension_semantics=("arbitrary",), vmem_limit_bytes=VMEM_LIMIT_BYTES),
        name="ada_mod",
    )(c_rows, w, b.reshape(1, n))


def _ffn_kernel(x_ref, shift_ref, scale_ref, gate_ref, gain_ref, wg_ref, wu_ref, wd_ref,
                fg_ref, o_ref, h_ref, *, final_norm):
    j = pl.program_id(1)

    @pl.when(j == 0)
    def _():
        h_ref[...] = _rms_mod(x_ref[...], gain_ref[...], shift_ref[...], scale_ref[...]).astype(BF16)
        o_ref[...] = jnp.zeros_like(o_ref)

    h = h_ref[...]
    g = jnp.dot(h, wg_ref[...].astype(BF16), preferred_element_type=F32)
    u = jnp.dot(h, wu_ref[...].astype(BF16), preferred_element_type=F32)
    a = (g * jax.nn.sigmoid(g) * u).astype(BF16)
    o_ref[...] += jnp.dot(a, wd_ref[...].astype(BF16), preferred_element_type=F32)

    @pl.when(j == pl.num_programs(1) - 1)
    def _():
        y = x_ref[...] + (HALF_STEP * gate_ref[...]) * o_ref[...]
        if final_norm:
            ms = jnp.mean(y * y, axis=-1, keepdims=True)
            y = y * lax.rsqrt(ms + RMS_EPS) * fg_ref[...]
        o_ref[...] = y


def _ffn_sublayer(x, mods, gain, wg, wu, wd, which, final_gain=None, tm=1024, tf=256):
    t, d = x.shape
    bm = mods[0].shape[0]
    tm = min(tm, t // bm)
    blocks_per_batch = (t // bm) // tm
    dff = wg.shape[2]
    final_norm = final_gain is not None
    fg = final_gain if final_norm else gain
    mod_spec = pl.BlockSpec((None, 1, d), lambda i, j: (i // blocks_per_batch, 0, 0))
    vec_spec = pl.BlockSpec((1, d), lambda i, j: (0, 0))
    return pl.pallas_call(
        functools.partial(_ffn_kernel, final_norm=final_norm),
        grid=(t // tm, dff // tf),
        in_specs=[pl.BlockSpec((tm, d), lambda i, j: (i, 0)),
                  mod_spec, mod_spec, mod_spec, vec_spec,
                  pl.BlockSpec((None, d, tf), lambda i, j: (which, 0, j)),
                  pl.BlockSpec((None, d, tf), lambda i, j: (which, 0, j)),
                  pl.BlockSpec((None, tf, d), lambda i, j: (which, j, 0)),
                  vec_spec],
        out_specs=pl.BlockSpec((tm, d), lambda i, j: (i, 0)),
        out_shape=jax.ShapeDtypeStruct((t, d), F32),
        scratch_shapes=[pltpu.VMEM((tm, d), BF16)],
        compiler_params=pltpu.CompilerParams(
            dimension_semantics=("parallel", "arbitrary"), vmem_limit_bytes=VMEM_LIMIT_BYTES),
        name="ffn_final" if final_norm else "ffn",
    )(x, *mods, gain.reshape(1, d), wg, wu, wd, fg.reshape(1, d))


def _proj_kernel(x_ref, shift_ref, scale_ref, gain_ref, w_ref, sw_ref, sb_ref, *rest, n_u, n_hy, row_len, part):
    o_refs, h_ref = rest[:-1], rest[-1]
    j = pl.program_id(1)

    @pl.when(j == 0)
    def _():
        h_ref[...] = _rms_mod(x_ref[...], gain_ref[...], shift_ref[...], scale_ref[...]).astype(BF16)

    tm, tn = h_ref.shape[0], w_ref.shape[1]

    def in_parts(o_ref, epilogue):
        for c in range(0, tn, part):
            p = jnp.dot(h_ref[...], w_ref[:, c:c + part].astype(BF16), preferred_element_type=F32)
            o_ref[:, c:c + part] = epilogue(p, c).astype(o_ref.dtype)

    def short_conv(p, c):
        col = lax.broadcasted_iota(jnp.int32, p.shape, 0) % row_len
        prev = jnp.where(col == 0, 0.0, pltpu.roll(p, 1, 0))
        nxt = jnp.where(col == row_len - 1, 0.0, pltpu.roll(p, tm - 1, 0))
        sw = sw_ref[:, c:c + part]
        return sb_ref[:, c:c + part] + prev * sw[0:1] + p * sw[1:2] + nxt * sw[2:3]

    if n_hy == 0:
        in_parts(o_refs[0], lambda p, c: p)
        return

    @pl.when(j < n_u)
    def _():
        in_parts(o_refs[0], lambda p, c: p)

    @pl.when((j >= n_u) & (j < n_u + n_hy))
    def _():
        in_parts(o_refs[1], short_conv)

    @pl.when(j >= n_u + n_hy)
    def _():
        in_parts(o_refs[2], lambda p, c: jax.nn.sigmoid(p))


def _in_proj(x, shift, scale, gain, w, short_w=None, short_b=None, n_u=D_S5, n_hy=0, row_len=GRID_W,
             tm=1024, tn=512, part=256):
    t, d = x.shape
    bm = shift.shape[0]
    tm = min(tm, t // bm)
    blocks_per_batch = (t // bm) // tm
    n = w.shape[1]
    assert tm % row_len == 0 and (t // bm) % tm == 0
    bu, bh = n_u // tn, n_hy // tn
    bg = n // tn - bu - bh
    mod_spec = pl.BlockSpec((None, 1, d), lambda i, j: (i // blocks_per_batch, 0, 0))
    out_shape = [jax.ShapeDtypeStruct((t, n_u), F32)]
    out_specs = [pl.BlockSpec((tm, tn), lambda i, j: (i, jnp.minimum(j, bu - 1)))]
    if bh:
        out_shape += [jax.ShapeDtypeStruct((t, n_hy), BF16), jax.ShapeDtypeStruct((t, bg * tn), BF16)]
        out_specs += [pl.BlockSpec((tm, tn), lambda i, j: (i, jnp.clip(j - bu, 0, bh - 1))),
                      pl.BlockSpec((tm, tn), lambda i, j: (i, jnp.maximum(j - bu - bh, 0)))]
        sw, sb = short_w, short_b.reshape(1, n_hy)
        hy_blk = lambda i, j: (0, jnp.clip(j - bu, 0, bh - 1))
    else:
        sw, sb = jnp.zeros((HY_SHORT, tn), F32), jnp.zeros((1, tn), F32)
        hy_blk = lambda i, j: (0, 0)
    return pl.pallas_call(
        functools.partial(_proj_kernel, n_u=bu, n_hy=bh, row_len=row_len, part=part),
        grid=(t // tm, n // tn),
        in_specs=[pl.BlockSpec((tm, d), lambda i, j: (i, 0)),
                  mod_spec, mod_spec,
                  pl.BlockSpec((1, d), lambda i, j: (0, 0)),
                  pl.BlockSpec((d, tn), lambda i, j: (0, j)),
                  pl.BlockSpec((HY_SHORT, tn), hy_blk),
                  pl.BlockSpec((1, tn), hy_blk)],
        out_specs=out_specs,
        out_shape=out_shape,
        scratch_shapes=[pltpu.VMEM((tm, d), BF16)],
        compiler_params=pltpu.CompilerParams(
            dimension_semantics=("parallel", "arbitrary"), vmem_limit_bytes=VMEM_LIMIT_BYTES),
        name="in_proj",
    )(x, shift, scale, gain.reshape(1, d), w, sw, sb)


def _gelu_tanh(x):
    return 0.5 * x * (1.0 + jnp.tanh(math.sqrt(2.0 / math.pi) * (x + 0.044715 * (x * x * x))))


def _merge_kernel(x_ref, gate_ref, ys_ref, yh_ref, ga_ref, gb_ref, wpa_lo_ref, wpa_hi_ref, wpb_ref,
                  wout_ref, o_ref, s_ref, acc_ref):
    j = pl.program_id(1)

    @pl.when(j == 0)
    def _():
        s_ref[...] = _gelu_tanh(ys_ref[...].astype(F32)).astype(BF16)
        acc_ref[...] = jnp.zeros_like(acc_ref)

    s = s_ref[...]
    pa_lo = jnp.dot(s, wpa_lo_ref[...], preferred_element_type=F32)
    pa_hi = jnp.dot(s, wpa_hi_ref[...], preferred_element_type=F32)
    y_a = pa_lo * jax.nn.sigmoid(pa_hi)
    y_b = jnp.dot(yh_ref[...], wpb_ref[...], preferred_element_type=F32)
    m = ga_ref[...].astype(F32) * y_a + gb_ref[...].astype(F32) * y_b
    acc_ref[...] += jnp.dot(m.astype(BF16), wout_ref[...], preferred_element_type=F32)

    @pl.when(j == pl.num_programs(1) - 1)
    def _():
        o_ref[...] = x_ref[...] + gate_ref[...] * acc_ref[...]


def _merge(x, gate, y_s5, y_hy, sig_gates, w_pa, w_pb, w_out, tm=512, tn=512):
    t, d = x.shape
    bm = gate.shape[0]
    blocks_per_batch = (t // bm) // tm
    nj = d // tn
    ds5 = y_s5.shape[1]
    dhy = y_hy.shape[1]
    return pl.pallas_call(
        _merge_kernel,
        grid=(t // tm, nj),
        in_specs=[pl.BlockSpec((tm, d), lambda i, j: (i, 0)),
                  pl.BlockSpec((None, 1, d), lambda i, j: (i // blocks_per_batch, 0, 0)),
                  pl.BlockSpec((tm, ds5), lambda i, j: (i, 0)),
                  pl.BlockSpec((tm, dhy), lambda i, j: (i, 0)),
                  pl.BlockSpec((tm, tn), lambda i, j: (i, j)),
                  pl.BlockSpec((tm, tn), lambda i, j: (i, nj + j)),
                  pl.BlockSpec((ds5, tn), lambda i, j: (0, j)),
                  pl.BlockSpec((ds5, tn), lambda i, j: (0, nj + j)),
                  pl.BlockSpec((dhy, tn), lambda i, j: (0, j)),
                  pl.BlockSpec((tn, d), lambda i, j: (j, 0))],
        out_specs=pl.BlockSpec((tm, d), lambda i, j: (i, 0)),
        out_shape=jax.ShapeDtypeStruct((t, d), F32),
        scratch_shapes=[pltpu.VMEM((tm, ds5), BF16), pltpu.VMEM((tm, d), F32)],
        compiler_params=pltpu.CompilerParams(
            dimension_semantics=("parallel", "arbitrary"), vmem_limit_bytes=VMEM_LIMIT_BYTES),
        name="merge",
    )(x, gate, y_s5, y_hy, sig_gates, sig_gates, w_pa, w_pa, w_pb, w_out)


def _s5_weights_kernel(par_ref, bt_ref, c_ref, d_ref, bpow_ref, toep_ref, cpow_ref, tab_ref, ca_ref, *, n_steps):
    t, h, p = S5_CHUNK, S5_GROUP, S5_STATE
    lanes = 2 * p
    hp = lax.Precision.HIGHEST
    sgn = jnp.where(lax.broadcasted_iota(jnp.int32, (1, lanes), 1) < p, -1.0, 1.0)
    par = par_ref[...]
    gsum = None
    for d in range(S5_DIRS):
        lr = jnp.minimum(par[3 * d:3 * d + 1], LAMBDA_RE_MAX)
        li = par[3 * d + 1:3 * d + 2]
        dt = jnp.exp(par[3 * d + 2:3 * d + 3])
        zr, zi = lr * dt, li * dt

        def apow(j):
            mag = jnp.exp(j * zr)
            return mag * jnp.cos(j * zi), sgn * (mag * jnp.sin(j * zi))

        def cmul(x, a1, a2):
            return x * a1 + pltpu.roll(x, p, 1) * a2

        a1, a2 = apow(lax.broadcasted_iota(jnp.int32, (t + 1, 1), 0).astype(F32))
        nr, ni = a1[1:2] - 1.0, sgn * a2[1:2]
        den = lr * lr + li * li
        f_re = (nr * lr + ni * li) / den
        f_im = (ni * lr - nr * li) / den
        bbar = cmul(bt_ref[d], f_re, sgn * f_im)
        cc = c_ref[d]
        ca = [cmul(cc, a1[j:j + 1], a2[j:j + 1]) * (-sgn) for j in range(t + 1)]
        ca_ref[...] = jnp.zeros_like(ca_ref)
        for k in range(t):
            e_b, e_c = (t - 1 - k, k + 1) if d == 0 else (k, t - k)
            bpow_ref[k * h:(k + 1) * h, d * lanes:(d + 1) * lanes] = (
                cmul(bbar, a1[e_b:e_b + 1], a2[e_b:e_b + 1]).astype(BF16))
            cpow_ref[k * h:(k + 1) * h, d * lanes:(d + 1) * lanes] = ca[e_c].astype(BF16)
            l = t - 1 + k if d == 0 else t - 1 - k
            ca_ref[l * h:(l + 1) * h, :] = ca[k]
        g = lax.dot_general(bbar, ca_ref[...], (((1,), (1,)), ((), ())), preferred_element_type=F32, precision=hp)
        gsum = g if gsum is None else gsum + g
        for s in range(n_steps):
            s1, s2 = apow(float(t * 2 ** s))
            r = d * 2 * n_steps + 2 * s
            tab_ref[r:r + 1, :] = s1
            tab_ref[r + 1:r + 2, :] = s2
    wide = gsum.shape[1]
    col = lax.broadcasted_iota(jnp.int32, (h, wide), 1)
    row = lax.broadcasted_iota(jnp.int32, (h, wide), 0)
    gsum = gsum + jnp.where(col - (t - 1) * h == row, d_ref[...], 0.0)
    for k in range(t):
        off = (t - 1 - k) * h
        shifted = gsum if off == 0 else pltpu.roll(gsum, wide - off, 1)
        toep_ref[k * h:(k + 1) * h, :] = shifted[:, :t * h].astype(BF16)


def _s5_weights(lam_re, lam_im, log_dt, b_re, b_im, c_re, c_im, d_skip, n_steps):
    g, p, h, t = S5_GROUPS, S5_STATE, S5_GROUP, S5_CHUNK
    cat2 = lambda a: jnp.concatenate([a, a], axis=-1)
    par = jnp.stack([cat2(lam_re), cat2(lam_im), jnp.broadcast_to(log_dt[..., None], (S5_DIRS, g, 2 * p))], axis=1)
    par = par.transpose(2, 0, 1, 3).reshape(g, 3 * S5_DIRS, 2 * p)
    btc = jnp.concatenate([b_re, b_im], axis=2).transpose(1, 0, 3, 2)
    ccat = jnp.concatenate([c_re, c_im], axis=3).transpose(1, 0, 2, 3)
    wide = 2 * t * h
    drow = jnp.zeros((g, 1, wide), F32).at[:, 0, (t - 1) * h:t * h].set(d_skip.reshape(g, h))
    w = t * h
    sq = pl.BlockSpec((None, w, w), lambda i: (i, 0, 0))
    return pl.pallas_call(
        functools.partial(_s5_weights_kernel, n_steps=n_steps),
        grid=(g,),
        in_specs=[pl.BlockSpec((None, 3 * S5_DIRS, 2 * p), lambda i: (i, 0, 0)),
                  pl.BlockSpec((None, S5_DIRS, h, 2 * p), lambda i: (i, 0, 0, 0)),
                  pl.BlockSpec((None, S5_DIRS, h, 2 * p), lambda i: (i, 0, 0, 0)),
                  pl.BlockSpec((None, 1, wide), lambda i: (i, 0, 0))],
        out_specs=[sq, sq, sq, pl.BlockSpec((None, 4 * n_steps, 2 * p), lambda i: (i, 0, 0))],
        out_shape=[jax.ShapeDtypeStruct((g, w, w), BF16)] * 3 + [jax.ShapeDtypeStruct((g, 4 * n_steps, 2 * p), F32)],
        scratch_shapes=[pltpu.VMEM((wide, 2 * p), F32)],
        compiler_params=pltpu.CompilerParams(
            dimension_semantics=("parallel",), vmem_limit_bytes=VMEM_LIMIT_BYTES),
        name="s5_weights",
    )(par, btc, ccat, drow)


def _s5_kernel(u_ref, uc_ref, bpow_ref, toep_ref, cpow_ref, tab_ref, y_ref, ucat_ref, ucc_ref, yacc_ref,
               *, bsz, n_steps):
    t, gw = S5_CHUNK, S5_GROUP
    per = LANE // gw
    n_lat = u_ref.shape[0] // (bsz * t)
    n_ctx = uc_ref.shape[0] // (bsz * t)
    n_ch = n_lat + n_ctx
    rows = bsz * n_ch
    half = 2 * S5_STATE
    for k in range(t):
        ucat_ref[:, k * LANE:(k + 1) * LANE] = u_ref[pl.ds(k, bsz * n_lat, stride=t), :]
        ucc_ref[:, k * LANE:(k + 1) * LANE] = uc_ref[pl.ds(k, bsz * n_ctx, stride=t), :]
    yacc_ref[...] = jnp.zeros_like(yacc_ref)
    rib = lax.broadcasted_iota(jnp.int32, (rows, half), 0) % n_ch
    lane = lax.broadcasted_iota(jnp.int32, (1, LANE), 1)

    def cmul_add(acc, sh, a1, a2):
        return acc + a1 * sh + a2 * pltpu.roll(sh, S5_STATE, 1)

    def gather(src_ref, gl):
        cols = []
        for j in range(t // per):
            acc = None
            for tt in range(per):
                k = j * per + tt
                r = pltpu.roll(src_ref[:, k * LANE:(k + 1) * LANE], (gw * tt - gw * gl) % LANE, 1)
                acc = r if acc is None else jnp.where((lane >= gw * tt) & (lane < gw * (tt + 1)), r, acc)
            cols.append(acc)
        return jnp.concatenate(cols, axis=1).astype(BF16)

    def group(gl, carry):
        ul = gather(ucat_ref, gl)
        uc = gather(ucc_ref, gl)
        bpow = bpow_ref[gl]
        zl = jnp.dot(ul, bpow, preferred_element_type=F32)
        zc = jnp.dot(uc, bpow, preferred_element_type=F32)
        fparts, bparts = [], []
        for b in range(bsz):
            lat = slice(b * n_lat, (b + 1) * n_lat)
            ctx = slice(b * n_ctx, (b + 1) * n_ctx)
            fparts += [zc[ctx, :half], zl[lat, :half]]
            bparts += [zl[lat, half:], zc[ctx, half:]]
        fw = jnp.concatenate(fparts, axis=0)
        bw = jnp.concatenate(bparts, axis=0)
        tab = tab_ref[gl]
        for s in range(n_steps):
            d = 1 << s
            sh = jnp.where(rib >= d, pltpu.roll(fw, d, 0), 0.0)
            fw = cmul_add(fw, sh, tab[2 * s:2 * s + 1], tab[2 * s + 1:2 * s + 2])
            o = 2 * n_steps
            sh = jnp.where(rib < n_ch - d, pltpu.roll(bw, rows - d, 0), 0.0)
            bw = cmul_add(bw, sh, tab[o + 2 * s:o + 2 * s + 1], tab[o + 2 * s + 1:o + 2 * s + 2])
        fe = jnp.where(rib >= 1, pltpu.roll(fw, 1, 0), 0.0)
        be = jnp.where(rib < n_ch - 1, pltpu.roll(bw, rows - 1, 0), 0.0)
        fl = jnp.concatenate([fe[b * n_ch + n_ctx:(b + 1) * n_ch] for b in range(bsz)], axis=0)
        bl = jnp.concatenate([be[b * n_ch:b * n_ch + n_lat] for b in range(bsz)], axis=0)
        st = jnp.concatenate([fl, bl], axis=1).astype(BF16)
        y = (jnp.dot(ul, toep_ref[gl], preferred_element_type=F32)
             + lax.dot_general(st, cpow_ref[gl], (((1,), (1,)), ((), ())), preferred_element_type=F32))
        mine = (lane >= gw * gl) & (lane < gw * (gl + 1))
        for i in range(t):
            src = y[:, (i // per) * LANE:(i // per + 1) * LANE]
            r = pltpu.roll(src, (gw * gl - gw * (i % per)) % LANE, 1)
            blk = slice(i * LANE, (i + 1) * LANE)
            yacc_ref[:, blk] = jnp.where(mine, r, yacc_ref[:, blk])
        return carry

    lax.fori_loop(0, S5_GROUPS_PER_COL, group, 0)
    for i in range(t):
        y_ref[pl.ds(i, bsz * n_lat, stride=t), :] = yacc_ref[:, i * LANE:(i + 1) * LANE]


def _s5_mix(u, uc, bpw, toep, cpw, tab, bsz):
    rl, dm = u.shape
    rc = uc.shape[0]
    t = S5_CHUNK
    w = t * S5_GROUP
    n_steps = tab.shape[1] // 4
    gpc = S5_GROUPS_PER_COL
    wspec = pl.BlockSpec((gpc, w, w), lambda i: (i, 0, 0))
    return pl.pallas_call(
        functools.partial(_s5_kernel, bsz=bsz, n_steps=n_steps),
        grid=(dm // LANE,),
        in_specs=[pl.BlockSpec((rl, LANE), lambda i: (0, i)),
                  pl.BlockSpec((rc, LANE), lambda i: (0, i)),
                  wspec, wspec, wspec,
                  pl.BlockSpec((gpc, 4 * n_steps, tab.shape[2]), lambda i: (i, 0, 0))],
        out_specs=pl.BlockSpec((rl, LANE), lambda i: (0, i)),
        out_shape=jax.ShapeDtypeStruct(u.shape, F32),
        scratch_shapes=[pltpu.VMEM((rl // t, t * LANE), F32), pltpu.VMEM((rc // t, t * LANE), F32),
                        pltpu.VMEM((rl // t, t * LANE), F32)],
        compiler_params=pltpu.CompilerParams(
            dimension_semantics=("parallel",), vmem_limit_bytes=VMEM_LIMIT_BYTES),
        name="s5_mix",
    )(u, uc, bpw, toep, cpw, tab)


def _s5_bidirectional(u, u_ctx, lam_re, lam_im, log_dt, b_re, b_im, c_re, c_im, d_skip):
    bsz, length, dm = u.shape
    ctx_len = u_ctx.shape[1]
    t = S5_CHUNK
    n_lat, n_ctx = length // t, ctx_len // t
    n_steps = max(1, math.ceil(math.log2(n_lat + n_ctx)))
    bpw, toep, cpw, tab = _s5_weights(lam_re, lam_im, log_dt, b_re, b_im, c_re, c_im, d_skip, n_steps)
    y = _s5_mix(u.reshape(bsz * length, dm), u_ctx.reshape(bsz * ctx_len, dm), bpw, toep, cpw, tab, bsz)
    return y.reshape(bsz, length, dm)


def _dft_constants(real_input=False):
    n1 = np.arange(FFT_N1)
    n2 = np.arange(FFT_N2)
    half = FFT_N1 // 2
    th = 2 * np.pi * np.outer(n1, n1) / FFT_N1
    c1, s1 = np.cos(th), np.sin(th)
    if real_input:
        w1 = np.concatenate([c1, -s1], axis=1)
    else:
        w1 = np.concatenate([np.concatenate([c1[:half], -s1[:half]], axis=1),
                             np.concatenate([s1[:half], c1[:half]], axis=1)], axis=0)
    z = np.zeros_like(w1)
    w1p = np.block([[w1, z], [z, w1]])
    ph = 2 * np.pi * np.outer(n2, n1) / FFT_N
    t1 = np.concatenate([np.cos(ph), np.cos(ph)], axis=1)
    t2 = np.concatenate([np.sin(ph), -np.sin(ph)], axis=1)
    ps = 2 * np.pi * np.outer(n2, n2) / FFT_N2
    f2 = np.concatenate([np.cos(ps), -np.sin(ps)], axis=1)
    g2 = np.concatenate([np.cos(ps), np.sin(ps)], axis=1)
    c2, s2 = np.cos(ph).T, np.sin(ph).T
    wi = np.concatenate([np.concatenate([c1[:, :half], s1[:, :half]], axis=1),
                         np.concatenate([-s1[:, :half], c1[:, :half]], axis=1)], axis=0) / FFT_N
    zi = np.zeros_like(wi)
    wi2 = np.stack([np.concatenate([wi, zi], axis=1), np.concatenate([zi, wi], axis=1)])
    as_b = lambda a: jnp.asarray(a, F32).astype(BF16)
    as_f = lambda a: jnp.asarray(a, F32)
    return [as_b(w1p), as_f(t1), as_f(t2), as_b(f2), as_b(g2), as_f(c2), as_f(s2), as_b(wi2)]


def _fwd_spectrum(xp, w1p, t1, t2, f2):
    cp = xp.shape[0]
    hn = FFT_N1
    a = jnp.dot(xp.reshape(cp * FFT_N2, 128).astype(BF16), w1p, preferred_element_type=F32)
    out = []
    for par in range(2):
        ap = a[:, par * 128:(par + 1) * 128]
        ap = ap.reshape(cp, FFT_N2, 128) * t1 + pltpu.roll(ap, hn, 1).reshape(cp, FFT_N2, 128) * t2
        at = jnp.swapaxes(ap, 1, 2)
        p = jnp.dot(at.reshape(cp * 128, FFT_N2).astype(BF16), f2, preferred_element_type=F32)
        p = p.reshape(cp, 128, 2 * FFT_N2)
        out.append((p[:, :hn, :FFT_N2] - p[:, hn:, FFT_N2:], p[:, :hn, FFT_N2:] + p[:, hn:, :FFT_N2]))
    return out


def _inv_time(yre, yim, g2, c2, s2, wi_par):
    cp = yre.shape[0]
    hn = FFT_N1
    y = jnp.concatenate([yre, yim], axis=1).reshape(cp * 128, FFT_N2).astype(BF16)
    q = jnp.dot(y, g2, preferred_element_type=F32).reshape(cp, 128, 2 * FFT_N2)
    bre = q[:, :hn, :FFT_N2] - q[:, hn:, FFT_N2:]
    bim = q[:, :hn, FFT_N2:] + q[:, hn:, :FFT_N2]
    b2 = jnp.concatenate([bre * c2 - bim * s2, bre * s2 + bim * c2], axis=1)
    bt = jnp.swapaxes(b2, 1, 2)
    return jnp.dot(bt.reshape(cp * FFT_N2, 128).astype(BF16), wi_par, preferred_element_type=F32)


def _hyena_kernel(z_ref, g1_ref, g2_ref, kf_ref, bias_ref, w1p_ref, t1_ref, t2_ref, f2_ref, gi_ref, c2_ref,
                  s2_ref, wi_ref, o_ref, zt_ref, g1t_ref, g2t_ref, ot_ref, stage_ref):
    k = pl.program_id(1)
    n_s = z_ref.shape[0] // FFT_N2
    cp = HY_PAIRS_PER_STEP

    def to_tiles(x_ref, t_ref):
        for s in range(n_s):
            xs = x_ref[s * FFT_N2:(s + 1) * FFT_N2, :].astype(F32)
            stage_ref[pl.ds(s, LANE, stride=HY_TILE_PITCH), :] = xs.T
        st = stage_ref[...].reshape(HY_PAIRS_PER_COL, 2 * HY_TILE_PITCH, FFT_N2)
        both = jnp.concatenate([st[:, :n_s], st[:, HY_TILE_PITCH:HY_TILE_PITCH + n_s]], axis=1)
        t_ref[...] = jnp.swapaxes(both, 1, 2)

    @pl.when(k == 0)
    def _():
        to_tiles(z_ref, zt_ref)
        to_tiles(g1_ref, g1t_ref)
        to_tiles(g2_ref, g2t_ref)

    sl = pl.ds(pl.multiple_of(k * cp, cp), cp)
    z = zt_ref[sl]
    gates = (g1t_ref, g2t_ref)
    for o in range(HY_ORDER):
        spec = _fwd_spectrum(z, w1p_ref[...], t1_ref[...], t2_ref[...], f2_ref[...])
        conv = None
        for par in range(2):
            xre, xim = spec[par]
            kre = kf_ref[o, par, :, :FFT_N1, :]
            kim = kf_ref[o, par, :, FFT_N1:, :]
            part = _inv_time(xre * kre - xim * kim, xre * kim + xim * kre,
                             gi_ref[...], c2_ref[...], s2_ref[...], wi_ref[par])
            conv = part if conv is None else conv + part
        z = gates[o][sl] * (conv.reshape(cp, FFT_N2, LANE) + bias_ref[o] * z)
    ot_ref[sl] = z

    @pl.when(k == pl.num_programs(1) - 1)
    def _():
        back = jnp.swapaxes(ot_ref[...], 1, 2)
        for c in range(LANE):
            stage_ref[c * HY_TILE_PITCH:c * HY_TILE_PITCH + n_s, :] = back[c // 2, (c % 2) * n_s:(c % 2 + 1) * n_s, :]
        for s in range(n_s):
            rows = stage_ref[pl.ds(s, LANE, stride=HY_TILE_PITCH), :]
            o_ref[s * FFT_N2:(s + 1) * FFT_N2, :] = rows.T.astype(o_ref.dtype)


def _hyena_conv(us, kf, biasp):
    t, _ = us.shape
    consts = _dft_constants()
    ncol = D_HY // LANE
    nsub = HY_PAIRS_PER_COL // HY_PAIRS_PER_STEP
    full = lambda a: pl.BlockSpec(a.shape, lambda j, k: (0,) * a.ndim)
    nat = lambda off: pl.BlockSpec((t, LANE), lambda j, k: (0, off + j))
    tiles = pltpu.VMEM((HY_PAIRS_PER_COL, FFT_N2, LANE), F32)
    return pl.pallas_call(
        _hyena_kernel,
        grid=(ncol, nsub),
        in_specs=[nat(0), nat(ncol), nat(2 * ncol),
                  pl.BlockSpec((HY_ORDER, 2, HY_PAIRS_PER_STEP, 2 * FFT_N1, FFT_N2),
                               lambda j, k: (0, 0, j * nsub + k, 0, 0)),
                  pl.BlockSpec((HY_ORDER, HY_PAIRS_PER_STEP, 1, LANE), lambda j, k: (0, j * nsub + k, 0, 0))]
                 + [full(a) for a in consts],
        out_specs=pl.BlockSpec((t, LANE), lambda j, k: (0, j)),
        out_shape=jax.ShapeDtypeStruct((t, D_HY), us.dtype),
        scratch_shapes=[tiles, tiles, tiles, tiles, pltpu.VMEM((LANE * HY_TILE_PITCH, FFT_N2), F32)],
        compiler_params=pltpu.CompilerParams(
            dimension_semantics=("parallel", "arbitrary"), vmem_limit_bytes=VMEM_LIMIT_BYTES),
        name="hyena_conv",
    )(us, us, us, kf, biasp, *consts)


def _filter_time_kernel(w1t_ref, w1c_ref, w1s_ref, b1_ref, w2_ref, b2_ref, w3_ref, b3_ref, fr_ref,
                        wf_ref, wb_ref, df_ref, db_ref, o_ref, h_ref, k_ref, *, length):
    n_fft = 2 * length
    hp = lax.Precision.HIGHEST

    @pl.when(pl.program_id(0) == 0)
    def _():
        pos = lax.broadcasted_iota(jnp.int32, (1, n_fft), 1)
        lag = jnp.where(pos < length, pos, n_fft - pos).astype(F32)
        t = lag / float(length - 1)
        w = (2.0 * math.pi / length) * lag
        band_step = (HY_BANDS - 1 - 1e-4) / (HY_BANDS - 1)
        bands = 1e-4 + band_step * lax.broadcasted_iota(jnp.int32, (HY_BANDS, 1), 0).astype(F32)
        ang = bands * w
        fr = fr_ref[...]
        h = (w1t_ref[...] * t + jnp.dot(w1c_ref[...], jnp.cos(ang), preferred_element_type=F32, precision=hp)
             - jnp.dot(w1s_ref[...], jnp.sin(ang), preferred_element_type=F32, precision=hp))
        h = jnp.sin(fr * (h + b1_ref[...]))
        h = jnp.sin(fr * (jnp.dot(w2_ref[...], h, preferred_element_type=F32, precision=hp) + b2_ref[...]))
        h = jnp.sin(fr * (jnp.dot(w3_ref[...], h, preferred_element_type=F32, precision=hp) + b3_ref[...]))
        hi = h.astype(BF16)
        h_ref[0] = hi
        h_ref[1] = (h - hi.astype(F32)).astype(BF16)

    def dot3(w, lo, hi_):
        w_hi = w.astype(BF16)
        w_lo = (w - w_hi.astype(F32)).astype(BF16)
        h_hi, h_lo = h_ref[0, :, lo:hi_], h_ref[1, :, lo:hi_]
        return (jnp.dot(w_hi, h_hi, preferred_element_type=F32) + jnp.dot(w_hi, h_lo, preferred_element_type=F32)
                + jnp.dot(w_lo, h_hi, preferred_element_type=F32))

    pos = lax.broadcasted_iota(jnp.int32, (1, length), 1)
    tf = pos.astype(F32) / float(length - 1)
    tb = (length - pos).astype(F32) / float(length - 1)
    kf = dot3(wf_ref[...], 0, length) * jnp.exp(-tf * df_ref[...])
    kb = dot3(wb_ref[...], length, n_fft) * jnp.exp(-tb * db_ref[...])
    kb = jnp.where(pos == 0, 0.0, kb)
    inv = 1.0 / (jnp.sum(jnp.abs(kf), axis=1, keepdims=True) + jnp.sum(jnp.abs(kb), axis=1, keepdims=True))
    k_ref[:, :length] = kf * inv
    k_ref[:, length:] = kb * inv
    cb = wf_ref.shape[0]
    for n1 in range(FILTER_TILE_ROWS):
        row = k_ref[:, n1 * FFT_N2:(n1 + 1) * FFT_N2] if n1 < FFT_N1 else jnp.zeros((cb, FFT_N2), F32)
        o_ref[pl.ds(n1, cb, stride=FILTER_TILE_ROWS), :] = row


def _filter_time(length, w1, b1, w2, b2, w3, b3, freq, w_out, cb=128):
    col = lambda v: v.reshape(-1, 1)
    w1t = w1.T
    n_ch = w_out.shape[1]
    deltas = jnp.abs(jnp.linspace(math.log(HY_TARGET) / HY_SLOW_PCT, math.log(HY_TARGET) / HY_FAST_PCT,
                                  n_ch, dtype=F32)).reshape(n_ch, 1)
    wot = w_out.T
    nb = D_HY // cb
    small = lambda a: pl.BlockSpec(a.shape, lambda i: (0,) * a.ndim)
    fwd = lambda i: ((i // nb) * HY_DIRS * nb + i % nb, 0)
    bwd = lambda i: ((i // nb) * HY_DIRS * nb + nb + i % nb, 0)
    ins = [w1t[:, 0:1], w1t[:, 1:1 + HY_BANDS], w1t[:, 1 + HY_BANDS:], col(b1), w2.T, col(b2), w3.T, col(b3),
           col(freq)]
    hy_ff = w2.shape[0]
    return pl.pallas_call(
        functools.partial(_filter_time_kernel, length=length),
        grid=(HY_ORDER * nb,),
        in_specs=[small(a) for a in ins] + [pl.BlockSpec((cb, hy_ff), fwd), pl.BlockSpec((cb, hy_ff), bwd),
                                            pl.BlockSpec((cb, 1), fwd), pl.BlockSpec((cb, 1), bwd)],
        out_specs=pl.BlockSpec((cb * FILTER_TILE_ROWS, FFT_N2), lambda i: (i, 0)),
        out_shape=jax.ShapeDtypeStruct((HY_ORDER * D_HY * FILTER_TILE_ROWS, FFT_N2), F32),
        scratch_shapes=[pltpu.VMEM((2, hy_ff, 2 * length), BF16), pltpu.VMEM((cb, 2 * length), F32)],
        compiler_params=pltpu.CompilerParams(
            dimension_semantics=("arbitrary",), vmem_limit_bytes=VMEM_LIMIT_BYTES),
        name="hyena_filter_time",
    )(*ins, wot, wot, deltas, deltas)


def _filter_spec_kernel(k_ref, w1p_ref, t1_ref, t2_ref, f2_ref, o_ref):
    cb = k_ref.shape[0] // FILTER_TILE_ROWS
    kt = k_ref[...].reshape(cb, FILTER_TILE_ROWS, FFT_N2)[:, :FFT_N1, :]
    xp = jnp.swapaxes(kt.reshape(cb // 2, 2 * FFT_N1, FFT_N2), 1, 2)
    spec = _fwd_spectrum(xp, w1p_ref[...], t1_ref[...], t2_ref[...], f2_ref[...])
    for par in range(2):
        o_ref[par, :, :FFT_N1, :] = spec[par][0]
        o_ref[par, :, FFT_N1:, :] = spec[par][1]


def _filter_spectrum(kt, cb=64):
    consts = _dft_constants(real_input=True)[:4]
    nb = D_HY // cb
    full = lambda a: pl.BlockSpec(a.shape, lambda i: (0,) * a.ndim)
    return pl.pallas_call(
        _filter_spec_kernel,
        grid=(HY_ORDER * nb,),
        in_specs=[pl.BlockSpec((cb * FILTER_TILE_ROWS, FFT_N2), lambda i: (i, 0))] + [full(a) for a in consts],
        out_specs=pl.BlockSpec((None, 2, cb // 2, 2 * FFT_N1, FFT_N2), lambda i: (i // nb, 0, i % nb, 0, 0)),
        out_shape=jax.ShapeDtypeStruct((HY_ORDER, 2, D_HY // 2, 2 * FFT_N1, FFT_N2), F32),
        compiler_params=pltpu.CompilerParams(
            dimension_semantics=("parallel",), vmem_limit_bytes=VMEM_LIMIT_BYTES),
        name="hyena_filter_spectrum",
    )(kt, *consts)


def _hyena(us, w1, b1, w2, b2, w3, b3, freq, w_out, bias):
    bsz, length, _ = us.shape
    assert 2 * length == FFT_N and bsz == 2, "one complex transform carries exactly two batch rows"
    kf = _filter_spectrum(_filter_time(length, w1, b1, w2, b2, w3, b3, freq, w_out))
    biasp = jnp.repeat(bias.reshape(HY_ORDER, D_HY // 2, 1, 2), FFT_N1, axis=-1)
    return _hyena_conv(us.reshape(bsz * length, -1), kf, biasp).reshape(bsz, length, D_HY)


def kernel(x, c, ctx, c_ctx, w_ada, b_ada, norm_g, ffn_w_gate, ffn_w_up, ffn_w_down, w_in,
           s5_lam_re, s5_lam_im, s5_log_dt, s5_b_re, s5_b_im, s5_c_re, s5_c_im, s5_d,
           hy_short_w, hy_short_b, hy_w1, hy_b1, hy_w2, hy_b2, hy_w3, hy_b3, hy_freq, hy_w_out,
           hy_bias, w_pa, w_pb, w_out, final_g):
    bsz, seq, d = x.shape
    ctx_len = ctx.shape[1]
    n_rows = seq // GRID_W
    depth = w_ada.shape[0]
    assert depth == 1, "context-token outputs are only dropped by the last layer"
    l = 0

    c_rows = jnp.concatenate([c, c_ctx[None, :], jnp.zeros((8 - bsz - 1, d), F32)], axis=0)
    mod_all = _ada_mod(c_rows, w_ada[l], b_ada[l])
    mod = mod_all[:bsz].reshape(bsz, N_SUB, N_MOD, 1, d)
    mod_c = mod_all[bsz:bsz + 1].reshape(1, N_SUB, N_MOD, 1, d)

    def mods(m, sub):
        return tuple(m[:, sub, k] for k in range(N_MOD))

    wg, wu, wd = ffn_w_gate[l], ffn_w_up[l], ffn_w_down[l]
    w_in_b = w_in[l]

    xt = x.reshape(bsz * seq, d)
    ct = ctx.reshape(bsz * ctx_len, d)

    xt = _ffn_sublayer(xt, mods(mod, 0), norm_g[l, 0], wg, wu, wd, 0)
    ct = _ffn_sublayer(ct, mods(mod_c, 0), norm_g[l, 0], wg, wu, wd, 0)

    assert GRID_W * n_rows == seq
    u_s5, us_hy, sig_gates = _in_proj(xt, mod[:, 1, 0], mod[:, 1, 1], norm_g[l, 1], w_in_b,
                                      hy_short_w[l], hy_short_b[l], n_u=I_HY, n_hy=I_GA - I_HY)
    (u_ctx,) = _in_proj(ct, mod_c[:, 1, 0], mod_c[:, 1, 1], norm_g[l, 1], w_in_b[:, :D_S5])

    y_s5 = _s5_bidirectional(u_s5.reshape(bsz, seq, D_S5), u_ctx.reshape(bsz, ctx_len, D_S5),
                             s5_lam_re[l], s5_lam_im[l], s5_log_dt[l],
                             s5_b_re[l], s5_b_im[l], s5_c_re[l], s5_c_im[l], s5_d[l])
    y_hy = _hyena(us_hy.reshape(bsz, seq, I_GA - I_HY),
                  hy_w1[l], hy_b1[l], hy_w2[l], hy_b2[l], hy_w3[l], hy_b3[l], hy_freq[l],
                  hy_w_out[l], hy_bias[l])

    xt = _merge(xt, mod[:, 1, 2], y_s5.reshape(bsz * seq, D_S5), y_hy.reshape(bsz * seq, D_HY),
                sig_gates, w_pa[l].astype(BF16), w_pb[l].astype(BF16), w_out[l].astype(BF16))

    xt = _ffn_sublayer(xt, mods(mod, 2), norm_g[l, 2], wg, wu, wd, 1, final_gain=final_g)
    return xt.reshape(bsz, seq, d)
```

```python
import functools
import math

import jax
import jax.numpy as jnp
import numpy as np
from jax import lax
from jax.experimental import pallas as pl
from jax.experimental.pallas import tpu as pltpu

F32 = jnp.float32
BF16 = jnp.bfloat16

D_MODEL = 2048
GRID_W = 64
D_S5 = 1024
S5_GROUP = 16
S5_GROUPS = D_S5 // S5_GROUP
S5_STATE = 64
S5_DIRS = 2
LAMBDA_RE_MAX = -1e-4
S5_CHUNK = 16
LANE = 128
S5_GROUPS_PER_COL = LANE // S5_GROUP
D_HY = 1024
HY_ORDER = 2
HY_DIRS = 2
HY_SHORT = 3
HY_EMB = 33
HY_BANDS = (HY_EMB - 1) // 2
HY_TARGET = 1e-2
HY_FAST_PCT = 0.3
HY_SLOW_PCT = 1.5
FFT_N1 = 64
FFT_N2 = 128
FFT_N = FFT_N1 * FFT_N2
FILTER_TILE_ROWS = FFT_N1 + 8
HY_TILE_PITCH = FILTER_TILE_ROWS
HY_PAIRS_PER_COL = LANE // 2
HY_PAIRS_PER_STEP = 16
I_HY = D_S5
I_GA = D_S5 + (HY_ORDER + 1) * D_HY
I_GB = I_GA + D_MODEL
D_IN = I_GB + D_MODEL
D_FF = 5632
N_SUB = 3
N_MOD = 3
HALF_STEP = 0.5
RMS_EPS = 1e-6

VMEM_LIMIT_BYTES = 58 * 1024 * 1024


def _rms_mod(x, gain, shift, scale):
    ms = jnp.mean(x * x, axis=-1, keepdims=True)
    y = x * lax.rsqrt(ms + RMS_EPS) * gain
    return y * (1.0 + scale) + shift


def _ada_kernel(c_ref, w_ref, b_ref, o_ref):
    c = c_ref[...]
    a = c * jax.nn.sigmoid(c)
    o_ref[...] = jnp.dot(a, w_ref[...], preferred_element_type=F32,
                         precision=lax.Precision.HIGHEST) + b_ref[...]


def _ada_mod(c_rows, w, b, tn=1024):
    rows, d = c_rows.shape
    n = w.shape[1]
    return pl.pallas_call(
        _ada_kernel,
        grid=(n // tn,),
        in_specs=[pl.BlockSpec((rows, d), lambda j: (0, 0)),
                  pl.BlockSpec((d, tn), lambda j: (0, j)),
                  pl.BlockSpec((1, tn), lambda j: (0, j))],
        out_specs=pl.BlockSpec((rows, tn), lambda j: (0, j)),
        out_shape=jax.ShapeDtypeStruct((rows, n), F32),
        compiler_params=pltpu.CompilerParams(
            dimension_semantics=("arbitrary",), vmem_limit_bytes=VMEM_LIMIT_BYTES),
        name="ada_mod",
    )(c_rows, w, b.reshape(1, n))


def _ffn_kernel(x_ref, shift_ref, scale_ref, gate_ref, gain_ref, wg_ref, wu_ref, wd_ref,
                fg_ref, o_ref, h_ref, *, final_norm):
    j = pl.program_id(1)

    @pl.when(j == 0)
    def _():
        h_ref[...] = _rms_mod(x_ref[...], gain_ref[...], shift_ref[...], scale_ref[...]).astype(BF16)
        o_ref[...] = jnp.zeros_like(o_ref)

    h = h_ref[...]
    g = jnp.dot(h, wg_ref[...].astype(BF16), preferred_element_type=F32)
    u = jnp.dot(h, wu_ref[...].astype(BF16), preferred_element_type=F32)
    a = (g * jax.nn.sigmoid(g) * u).astype(BF16)
    o_ref[...] += jnp.dot(a, wd_ref[...].astype(BF16), preferred_element_type=F32)

    @pl.when(j == pl.num_programs(1) - 1)
    def _():
        y = x_ref[...] + (HALF_STEP * gate_ref[...]) * o_ref[...]
        if final_norm:
            ms = jnp.mean(y * y, axis=-1, keepdims=True)
            y = y * lax.rsqrt(ms + RMS_EPS) * fg_ref[...]
        o_ref[...] = y


def _ffn_sublayer(x, mods, gain, wg, wu, wd, which, final_gain=None, tm=1024, tf=256):
    t, d = x.shape
    bm = mods[0].shape[0]
    tm = min(tm, t // bm)
    blocks_per_batch = (t // bm) // tm
    dff = wg.shape[2]
    final_norm = final_gain is not None
    fg = final_gain if final_norm else gain
    mod_spec = pl.BlockSpec((None, 1, d), lambda i, j: (i // blocks_per_batch, 0, 0))
    vec_spec = pl.BlockSpec((1, d), lambda i, j: (0, 0))
    return pl.pallas_call(
        functools.partial(_ffn_kernel, final_norm=final_norm),
        grid=(t // tm, dff // tf),
        in_specs=[pl.BlockSpec((tm, d), lambda i, j: (i, 0)),
                  mod_spec, mod_spec, mod_spec, vec_spec,
                  pl.BlockSpec((None, d, tf), lambda i, j: (which, 0, j)),
                  pl.BlockSpec((None, d, tf), lambda i, j: (which, 0, j)),
                  pl.BlockSpec((None, tf, d), lambda i, j: (which, j, 0)),
                  vec_spec],
        out_specs=pl.BlockSpec((tm, d), lambda i, j: (i, 0)),
        out_shape=jax.ShapeDtypeStruct((t, d), F32),
        scratch_shapes=[pltpu.VMEM((tm, d), BF16)],
        compiler_params=pltpu.CompilerParams(
            dimension_semantics=("parallel", "arbitrary"), vmem_limit_bytes=VMEM_LIMIT_BYTES),
        name="ffn_final" if final_norm else "ffn",
    )(x, *mods, gain.reshape(1, d), wg, wu, wd, fg.reshape(1, d))


def _proj_kernel(x_ref, shift_ref, scale_ref, gain_ref, w_ref, sw_ref, sb_ref, *rest, n_u, n_hy, row_len, part):
    o_refs, h_ref = rest[:-1], rest[-1]
    j = pl.program_id(1)

    @pl.when(j == 0)
    def _():
        h_ref[...] = _rms_mod(x_ref[...], gain_ref[...], shift_ref[...], scale_ref[...]).astype(BF16)

    tm, tn = h_ref.shape[0], w_ref.shape[1]

    def in_parts(o_ref, epilogue):
        for c in range(0, tn, part):
            p = jnp.dot(h_ref[...], w_ref[:, c:c + part].astype(BF16), preferred_element_type=F32)
            o_ref[:, c:c + part] = epilogue(p, c).astype(o_ref.dtype)

    def short_conv(p, c):
        col = lax.broadcasted_iota(jnp.int32, p.shape, 0) % row_len
        prev = jnp.where(col == 0, 0.0, pltpu.roll(p, 1, 0))
        nxt = jnp.where(col == row_len - 1, 0.0, pltpu.roll(p, tm - 1, 0))
        sw = sw_ref[:, c:c + part]
        return sb_ref[:, c:c + part] + prev * sw[0:1] + p * sw[1:2] + nxt * sw[2:3]

    if n_hy == 0:
        in_parts(o_refs[0], lambda p, c: p)
        return

    @pl.when(j < n_u)
    def _():
        in_parts(o_refs[0], lambda p, c: p)

    @pl.when((j >= n_u) & (j < n_u + n_hy))
    def _():
        in_parts(o_refs[1], short_conv)

    @pl.when(j >= n_u + n_hy)
    def _():
        in_parts(o_refs[2], lambda p, c: jax.nn.sigmoid(p))


def _in_proj(x, shift, scale, gain, w, short_w=None, short_b=None, n_u=D_S5, n_hy=0, row_len=GRID_W,
             tm=1024, tn=512, part=256):
    t, d = x.shape
    bm = shift.shape[0]
    tm = min(tm, t // bm)
    blocks_per_batch = (t // bm) // tm
    n = w.shape[1]
    assert tm % row_len == 0 and (t // bm) % tm == 0
    bu, bh = n_u // tn, n_hy // tn
    bg = n // tn - bu - bh
    mod_spec = pl.BlockSpec((None, 1, d), lambda i, j: (i // blocks_per_batch, 0, 0))
    out_shape = [jax.ShapeDtypeStruct((t, n_u), F32)]
    out_specs = [pl.BlockSpec((tm, tn), lambda i, j: (i, jnp.minimum(j, bu - 1)))]
    if bh:
        out_shape += [jax.ShapeDtypeStruct((t, n_hy), BF16), jax.ShapeDtypeStruct((t, bg * tn), BF16)]
        out_specs += [pl.BlockSpec((tm, tn), lambda i, j: (i, jnp.clip(j - bu, 0, bh - 1))),
                      pl.BlockSpec((tm, tn), lambda i, j: (i, jnp.maximum(j - bu - bh, 0)))]
        sw, sb = short_w, short_b.reshape(1, n_hy)
        hy_blk = lambda i, j: (0, jnp.clip(j - bu, 0, bh - 1))
    else:
        sw, sb = jnp.zeros((HY_SHORT, tn), F32), jnp.zeros((1, tn), F32)
        hy_blk = lambda i, j: (0, 0)
    return pl.pallas_call(
        functools.partial(_proj_kernel, n_u=bu, n_hy=bh, row_len=row_len, part=part),
        grid=(t // tm, n // tn),
        in_specs=[pl.BlockSpec((tm, d), lambda i, j: (i, 0)),
                  mod_spec, mod_spec,
                  pl.BlockSpec((1, d), lambda i, j: (0, 0)),
                  pl.BlockSpec((d, tn), lambda i, j: (0, j)),
                  pl.BlockSpec((HY_SHORT, tn), hy_blk),
                  pl.BlockSpec((1, tn), hy_blk)],
        out_specs=out_specs,
        out_shape=out_shape,
        scratch_shapes=[pltpu.VMEM((tm, d), BF16)],
        compiler_params=pltpu.CompilerParams(
            dimension_semantics=("parallel", "arbitrary"), vmem_limit_bytes=VMEM_LIMIT_BYTES),
        name="in_proj",
    )(x, shift, scale, gain.reshape(1, d), w, sw, sb)


def _gelu_tanh(x):
    return 0.5 * x * (1.0 + jnp.tanh(math.sqrt(2.0 / math.pi) * (x + 0.044715 * (x * x * x))))


def _merge_kernel(x_ref, gate_ref, ys_ref, yh_ref, ga_ref, gb_ref, wpa_lo_ref, wpa_hi_ref, wpb_ref,
                  wout_ref, o_ref, s_ref, acc_ref):
    j = pl.program_id(1)

    @pl.when(j == 0)
    def _():
        s_ref[...] = _gelu_tanh(ys_ref[...].astype(F32)).astype(BF16)
        acc_ref[...] = jnp.zeros_like(acc_ref)

    s = s_ref[...]
    pa_lo = jnp.dot(s, wpa_lo_ref[...], preferred_element_type=F32)
    pa_hi = jnp.dot(s, wpa_hi_ref[...], preferred_element_type=F32)
    y_a = pa_lo * jax.nn.sigmoid(pa_hi)
    y_b = jnp.dot(yh_ref[...], wpb_ref[...], preferred_element_type=F32)
    m = ga_ref[...].astype(F32) * y_a + gb_ref[...].astype(F32) * y_b
    acc_ref[...] += jnp.dot(m.astype(BF16), wout_ref[...], preferred_element_type=F32)

    @pl.when(j == pl.num_programs(1) - 1)
    def _():
        o_ref[...] = x_ref[...] + gate_ref[...] * acc_ref[...]


def _merge(x, gate, y_s5, y_hy, sig_gates, w_pa, w_pb, w_out, tm=512, tn=512):
    t, d = x.shape
    bm = gate.shape[0]
    blocks_per_batch = (t // bm) // tm
    nj = d // tn
    ds5 = y_s5.shape[1]
    dhy = y_hy.shape[1]
    return pl.pallas_call(
        _merge_kernel,
        grid=(t // tm, nj),
        in_specs=[pl.BlockSpec((tm, d), lambda i, j: (i, 0)),
                  pl.BlockSpec((None, 1, d), lambda i, j: (i // blocks_per_batch, 0, 0)),
                  pl.BlockSpec((tm, ds5), lambda i, j: (i, 0)),
                  pl.BlockSpec((tm, dhy), lambda i, j: (i, 0)),
                  pl.BlockSpec((tm, tn), lambda i, j: (i, j)),
                  pl.BlockSpec((tm, tn), lambda i, j: (i, nj + j)),
                  pl.BlockSpec((ds5, tn), lambda i, j: (0, j)),
                  pl.BlockSpec((ds5, tn), lambda i, j: (0, nj + j)),
                  pl.BlockSpec((dhy, tn), lambda i, j: (0, j)),
                  pl.BlockSpec((tn, d), lambda i, j: (j, 0))],
        out_specs=pl.BlockSpec((tm, d), lambda i, j: (i, 0)),
        out_shape=jax.ShapeDtypeStruct((t, d), F32),
        scratch_shapes=[pltpu.VMEM((tm, ds5), BF16), pltpu.VMEM((tm, d), F32)],
        compiler_params=pltpu.CompilerParams(
            dimension_semantics=("parallel", "arbitrary"), vmem_limit_bytes=VMEM_LIMIT_BYTES),
        name="merge",
    )(x, gate, y_s5, y_hy, sig_gates, sig_gates, w_pa, w_pa, w_pb, w_out)


def _s5_weights_kernel(par_ref, bt_ref, c_ref, d_ref, bpow_ref, toep_ref, cpow_ref, tab_ref, ca_ref, *, n_steps):
    t, h, p = S5_CHUNK, S5_GROUP, S5_STATE
    lanes = 2 * p
    hp = lax.Precision.HIGHEST
    sgn = jnp.where(lax.broadcasted_iota(jnp.int32, (1, lanes), 1) < p, -1.0, 1.0)
    par = par_ref[...]
    gsum = None
    for d in range(S5_DIRS):
        lr = jnp.minimum(par[3 * d:3 * d + 1], LAMBDA_RE_MAX)
        li = par[3 * d + 1:3 * d + 2]
        dt = jnp.exp(par[3 * d + 2:3 * d + 3])
        zr, zi = lr * dt, li * dt

        def apow(j):
            mag = jnp.exp(j * zr)
            return mag * jnp.cos(j * zi), sgn * (mag * jnp.sin(j * zi))

        def cmul(x, a1, a2):
            return x * a1 + pltpu.roll(x, p, 1) * a2

        a1, a2 = apow(lax.broadcasted_iota(jnp.int32, (t + 1, 1), 0).astype(F32))
        nr, ni = a1[1:2] - 1.0, sgn * a2[1:2]
        den = lr * lr + li * li
        f_re = (nr * lr + ni * li) / den
        f_im = (ni * lr - nr * li) / den
        bbar = cmul(bt_ref[d], f_re, sgn * f_im)
        cc = c_ref[d]
        ca = [cmul(cc, a1[j:j + 1], a2[j:j + 1]) * (-sgn) for j in range(t + 1)]
        ca_ref[...] = jnp.zeros_like(ca_ref)
        for k in range(t):
            e_b, e_c = (t - 1 - k, k + 1) if d == 0 else (k, t - k)
            bpow_ref[k * h:(k + 1) * h, d * lanes:(d + 1) * lanes] = (
                cmul(bbar, a1[e_b:e_b + 1], a2[e_b:e_b + 1]).astype(BF16))
            cpow_ref[k * h:(k + 1) * h, d * lanes:(d + 1) * lanes] = ca[e_c].astype(BF16)
            l = t - 1 + k if d == 0 else t - 1 - k
            ca_ref[l * h:(l + 1) * h, :] = ca[k]
        g = lax.dot_general(bbar, ca_ref[...], (((1,), (1,)), ((), ())), preferred_element_type=F32, precision=hp)
        gsum = g if gsum is None else gsum + g
        for s in range(n_steps):
            s1, s2 = apow(float(t * 2 ** s))
            r = d * 2 * n_steps + 2 * s
            tab_ref[r:r + 1, :] = s1
            tab_ref[r + 1:r + 2, :] = s2
    wide = gsum.shape[1]
    col = lax.broadcasted_iota(jnp.int32, (h, wide), 1)
    row = lax.broadcasted_iota(jnp.int32, (h, wide), 0)
    gsum = gsum + jnp.where(col - (t - 1) * h == row, d_ref[...], 0.0)
    for k in range(t):
        off = (t - 1 - k) * h
        shifted = gsum if off == 0 else pltpu.roll(gsum, wide - off, 1)
        toep_ref[k * h:(k + 1) * h, :] = shifted[:, :t * h].astype(BF16)


def _s5_weights(lam_re, lam_im, log_dt, b_re, b_im, c_re, c_im, d_skip, n_steps):
    g, p, h, t = S5_GROUPS, S5_STATE, S5_GROUP, S5_CHUNK
    cat2 = lambda a: jnp.concatenate([a, a], axis=-1)
    par = jnp.stack([cat2(lam_re), cat2(lam_im), jnp.broadcast_to(log_dt[..., None], (S5_DIRS, g, 2 * p))], axis=1)
    par = par.transpose(2, 0, 1, 3).reshape(g, 3 * S5_DIRS, 2 * p)
    btc = jnp.concatenate([b_re, b_im], axis=2).transpose(1, 0, 3, 2)
    ccat = jnp.concatenate([c_re, c_im], axis=3).transpose(1, 0, 2, 3)
    wide = 2 * t * h
    drow = jnp.zeros((g, 1, wide), F32).at[:, 0, (t - 1) * h:t * h].set(d_skip.reshape(g, h))
    w = t * h
    sq = pl.BlockSpec((None, w, w), lambda i: (i, 0, 0))
    return pl.pallas_call(
        functools.partial(_s5_weights_kernel, n_steps=n_steps),
        grid=(g,),
        in_specs=[pl.BlockSpec((None, 3 * S5_DIRS, 2 * p), lambda i: (i, 0, 0)),
                  pl.BlockSpec((None, S5_DIRS, h, 2 * p), lambda i: (i, 0, 0, 0)),
                  pl.BlockSpec((None, S5_DIRS, h, 2 * p), lambda i: (i, 0, 0, 0)),
                  pl.BlockSpec((None, 1, wide), lambda i: (i, 0, 0))],
        out_specs=[sq, sq, sq, pl.BlockSpec((None, 4 * n_steps, 2 * p), lambda i: (i, 0, 0))],
        out_shape=[jax.ShapeDtypeStruct((g, w, w), BF16)] * 3 + [jax.ShapeDtypeStruct((g, 4 * n_steps, 2 * p), F32)],
        scratch_shapes=[pltpu.VMEM((wide, 2 * p), F32)],
        compiler_params=pltpu.CompilerParams(
            dimension_semantics=("parallel",), vmem_limit_bytes=VMEM_LIMIT_BYTES),
        name="s5_weights",
    )(par, btc, ccat, drow)


def _s5_kernel(u_ref, uc_ref, bpow_ref, toep_ref, cpow_ref, tab_ref, y_ref, ucat_ref, ucc_ref, yacc_ref,
               *, bsz, n_steps):
    t, gw = S5_CHUNK, S5_GROUP
    per = LANE // gw
    n_lat = u_ref.shape[0] // (bsz * t)
    n_ctx = uc_ref.shape[0] // (bsz * t)
    n_ch = n_lat + n_ctx
    rows = bsz * n_ch
    half = 2 * S5_STATE
    for k in range(t):
        ucat_ref[:, k * LANE:(k + 1) * LANE] = u_ref[pl.ds(k, bsz * n_lat, stride=t), :]
        ucc_ref[:, k * LANE:(k + 1) * LANE] = uc_ref[pl.ds(k, bsz * n_ctx, stride=t), :]
    yacc_ref[...] = jnp.zeros_like(yacc_ref)
    rib = lax.broadcasted_iota(jnp.int32, (rows, half), 0) % n_ch
    lane = lax.broadcasted_iota(jnp.int32, (1, LANE), 1)

    def cmul_add(acc, sh, a1, a2):
        return acc + a1 * sh + a2 * pltpu.roll(sh, S5_STATE, 1)

    def gather(src_ref, gl):
        cols = []
        for j in range(t // per):
            acc = None
            for tt in range(per):
                k = j * per + tt
                r = pltpu.roll(src_ref[:, k * LANE:(k + 1) * LANE], (gw * tt - gw * gl) % LANE, 1)
                acc = r if acc is None else jnp.where((lane >= gw * tt) & (lane < gw * (tt + 1)), r, acc)
            cols.append(acc)
        return jnp.concatenate(cols, axis=1).astype(BF16)

    def group(gl, carry):
        ul = gather(ucat_ref, gl)
        uc = gather(ucc_ref, gl)
        bpow = bpow_ref[gl]
        zl = jnp.dot(ul, bpow, preferred_element_type=F32)
        zc = jnp.dot(uc, bpow, preferred_element_type=F32)
        fparts, bparts = [], []
        for b in range(bsz):
            lat = slice(b * n_lat, (b + 1) * n_lat)
            ctx = slice(b * n_ctx, (b + 1) * n_ctx)
            fparts += [zc[ctx, :half], zl[lat, :half]]
            bparts += [zl[lat, half:], zc[ctx, half:]]
        fw = jnp.concatenate(fparts, axis=0)
        bw = jnp.concatenate(bparts, axis=0)
        tab = tab_ref[gl]
        for s in range(n_steps):
            d = 1 << s
            sh = jnp.where(rib >= d, pltpu.roll(fw, d, 0), 0.0)
            fw = cmul_add(fw, sh, tab[2 * s:2 * s + 1], tab[2 * s + 1:2 * s + 2])
            o = 2 * n_steps
            sh = jnp.where(rib < n_ch - d, pltpu.roll(bw, rows - d, 0), 0.0)
            bw = cmul_add(bw, sh, tab[o + 2 * s:o + 2 * s + 1], tab[o + 2 * s + 1:o + 2 * s + 2])
        fe = jnp.where(rib >= 1, pltpu.roll(fw, 1, 0), 0.0)
        be = jnp.where(rib < n_ch - 1, pltpu.roll(bw, rows - 1, 0), 0.0)
        fl = jnp.concatenate([fe[b * n_ch + n_ctx:(b + 1) * n_ch] for b in range(bsz)], axis=0)
        bl = jnp.concatenate([be[b * n_ch:b * n_ch + n_lat] for b in range(bsz)], axis=0)
        st = jnp.concatenate([fl, bl], axis=1).astype(BF16)
        y = (jnp.dot(ul, toep_ref[gl], preferred_element_type=F32)
             + lax.dot_general(st, cpow_ref[gl], (((1,), (1,)), ((), ())), preferred_element_type=F32))
        mine = (lane >= gw * gl) & (lane < gw * (gl + 1))
        for i in range(t):
            src = y[:, (i // per) * LANE:(i // per + 1) * LANE]
            r = pltpu.roll(src, (gw * gl - gw * (i % per)) % LANE, 1)
            blk = slice(i * LANE, (i + 1) * LANE)
            yacc_ref[:, blk] = jnp.where(mine, r, yacc_ref[:, blk])
        return carry

    lax.fori_loop(0, S5_GROUPS_PER_COL, group, 0)
    for i in range(t):
        y_ref[pl.ds(i, bsz * n_lat, stride=t), :] = yacc_ref[:, i * LANE:(i + 1) * LANE]


def _s5_mix(u, uc, bpw, toep, cpw, tab, bsz):
    rl, dm = u.shape
    rc = uc.shape[0]
    t = S5_CHUNK
    w = t * S5_GROUP
    n_steps = tab.shape[1] // 4
    gpc = S5_GROUPS_PER_COL
    wspec = pl.BlockSpec((gpc, w, w), lambda i: (i, 0, 0))
    return pl.pallas_call(
        functools.partial(_s5_kernel, bsz=bsz, n_steps=n_steps),
        grid=(dm // LANE,),
        in_specs=[pl.BlockSpec((rl, LANE), lambda i: (0, i)),
                  pl.BlockSpec((rc, LANE), lambda i: (0, i)),
                  wspec, wspec, wspec,
                  pl.BlockSpec((gpc, 4 * n_steps, tab.shape[2]), lambda i: (i, 0, 0))],
        out_specs=pl.BlockSpec((rl, LANE), lambda i: (0, i)),
        out_shape=jax.ShapeDtypeStruct(u.shape, F32),
        scratch_shapes=[pltpu.VMEM((rl // t, t * LANE), F32), pltpu.VMEM((rc // t, t * LANE), F32),
                        pltpu.VMEM((rl // t, t * LANE), F32)],
        compiler_params=pltpu.CompilerParams(
            dimension_semantics=("parallel",), vmem_limit_bytes=VMEM_LIMIT_BYTES),
        name="s5_mix",
    )(u, uc, bpw, toep, cpw, tab)


def _s5_bidirectional(u, u_ctx, lam_re, lam_im, log_dt, b_re, b_im, c_re, c_im, d_skip):
    bsz, length, dm = u.shape
    ctx_len = u_ctx.shape[1]
    t = S5_CHUNK
    n_lat, n_ctx = length // t, ctx_len // t
    n_steps = max(1, math.ceil(math.log2(n_lat + n_ctx)))
    bpw, toep, cpw, tab = _s5_weights(lam_re, lam_im, log_dt, b_re, b_im, c_re, c_im, d_skip, n_steps)
    y = _s5_mix(u.reshape(bsz * length, dm), u_ctx.reshape(bsz * ctx_len, dm), bpw, toep, cpw, tab, bsz)
    return y.reshape(bsz, length, dm)


def _dft_constants(real_input=False):
    n1 = np.arange(FFT_N1)
    n2 = np.arange(FFT_N2)
    half = FFT_N1 // 2
    th = 2 * np.pi * np.outer(n1, n1) / FFT_N1
    c1, s1 = np.cos(th), np.sin(th)
    if real_input:
        w1 = np.concatenate([c1, -s1], axis=1)
    else:
        w1 = np.concatenate([np.concatenate([c1[:half], -s1[:half]], axis=1),
                             np.concatenate([s1[:half], c1[:half]], axis=1)], axis=0)
    z = np.zeros_like(w1)
    w1p = np.block([[w1, z], [z, w1]])
    ph = 2 * np.pi * np.outer(n2, n1) / FFT_N
    t1 = np.concatenate([np.cos(ph), np.cos(ph)], axis=1)
    t2 = np.concatenate([np.sin(ph), -np.sin(ph)], axis=1)
    ps = 2 * np.pi * np.outer(n2, n2) / FFT_N2
    f2 = np.concatenate([np.cos(ps), -np.sin(ps)], axis=1)
    g2 = np.concatenate([np.cos(ps), np.sin(ps)], axis=1)
    c2, s2 = np.cos(ph).T, np.sin(ph).T
    wi = np.concatenate([np.concatenate([c1[:, :half], s1[:, :half]], axis=1),
                         np.concatenate([-s1[:, :half], c1[:, :half]], axis=1)], axis=0) / FFT_N
    zi = np.zeros_like(wi)
    wi2 = np.stack([np.concatenate([wi, zi], axis=1), np.concatenate([zi, wi], axis=1)])
    as_b = lambda a: jnp.asarray(a, F32).astype(BF16)
    as_f = lambda a: jnp.asarray(a, F32)
    return [as_b(w1p), as_f(t1), as_f(t2), as_b(f2), as_b(g2), as_f(c2), as_f(s2), as_b(wi2)]


def _fwd_spectrum(xp, w1p, t1, t2, f2):
    cp = xp.shape[0]
    hn = FFT_N1
    a = jnp.dot(xp.reshape(cp * FFT_N2, 128).astype(BF16), w1p, preferred_element_type=F32)
    out = []
    for par in range(2):
        ap = a[:, par * 128:(par + 1) * 128]
        ap = ap.reshape(cp, FFT_N2, 128) * t1 + pltpu.roll(ap, hn, 1).reshape(cp, FFT_N2, 128) * t2
        at = jnp.swapaxes(ap, 1, 2)
        p = jnp.dot(at.reshape(cp * 128, FFT_N2).astype(BF16), f2, preferred_element_type=F32)
        p = p.reshape(cp, 128, 2 * FFT_N2)
        out.append((p[:, :hn, :FFT_N2] - p[:, hn:, FFT_N2:], p[:, :hn, FFT_N2:] + p[:, hn:, :FFT_N2]))
    return out


def _inv_time(yre, yim, g2, c2, s2, wi_par):
    cp = yre.shape[0]
    hn = FFT_N1
    y = jnp.concatenate([yre, yim], axis=1).reshape(cp * 128, FFT_N2).astype(BF16)
    q = jnp.dot(y, g2, preferred_element_type=F32).reshape(cp, 128, 2 * FFT_N2)
    bre = q[:, :hn, :FFT_N2] - q[:, hn:, FFT_N2:]
    bim = q[:, :hn, FFT_N2:] + q[:, hn:, :FFT_N2]
    b2 = jnp.concatenate([bre * c2 - bim * s2, bre * s2 + bim * c2], axis=1)
    bt = jnp.swapaxes(b2, 1, 2)
    return jnp.dot(bt.reshape(cp * FFT_N2, 128).astype(BF16), wi_par, preferred_element_type=F32)


def _hyena_kernel(z_ref, g1_ref, g2_ref, kf_ref, bias_ref, w1p_ref, t1_ref, t2_ref, f2_ref, gi_ref, c2_ref,
                  s2_ref, wi_ref, o_ref, zt_ref, g1t_ref, g2t_ref, ot_ref, stage_ref):
    k = pl.program_id(1)
    n_s = z_ref.shape[0] // FFT_N2
    cp = HY_PAIRS_PER_STEP

    def to_tiles(x_ref, t_ref):
        for s in range(n_s):
            xs = x_ref[s * FFT_N2:(s + 1) * FFT_N2, :].astype(F32)
            stage_ref[pl.ds(s, LANE, stride=HY_TILE_PITCH), :] = xs.T
        st = stage_ref[...].reshape(HY_PAIRS_PER_COL, 2 * HY_TILE_PITCH, FFT_N2)
        both = jnp.concatenate([st[:, :n_s], st[:, HY_TILE_PITCH:HY_TILE_PITCH + n_s]], axis=1)
        t_ref[...] = jnp.swapaxes(both, 1, 2)

    @pl.when(k == 0)
    def _():
        to_tiles(z_ref, zt_ref)
        to_tiles(g1_ref, g1t_ref)
        to_tiles(g2_ref, g2t_ref)

    sl = pl.ds(pl.multiple_of(k * cp, cp), cp)
    z = zt_ref[sl]
    gates = (g1t_ref, g2t_ref)
    for o in range(HY_ORDER):
        spec = _fwd_spectrum(z, w1p_ref[...], t1_ref[...], t2_ref[...], f2_ref[...])
        conv = None
        for par in range(2):
            xre, xim = spec[par]
            kre = kf_ref[o, par, :, :FFT_N1, :]
            kim = kf_ref[o, par, :, FFT_N1:, :]
            part = _inv_time(xre * kre - xim * kim, xre * kim + xim * kre,
                             gi_ref[...], c2_ref[...], s2_ref[...], wi_ref[par])
            conv = part if conv is None else conv + part
        z = gates[o][sl] * (conv.reshape(cp, FFT_N2, LANE) + bias_ref[o] * z)
    ot_ref[sl] = z

    @pl.when(k == pl.num_programs(1) - 1)
    def _():
        back = jnp.swapaxes(ot_ref[...], 1, 2)
        for c in range(LANE):
            stage_ref[c * HY_TILE_PITCH:c * HY_TILE_PITCH + n_s, :] = back[c // 2, (c % 2) * n_s:(c % 2 + 1) * n_s, :]
        for s in range(n_s):
            rows = stage_ref[pl.ds(s, LANE, stride=HY_TILE_PITCH), :]
            o_ref[s * FFT_N2:(s + 1) * FFT_N2, :] = rows.T.astype(o_ref.dtype)


def _hyena_conv(us, kf, biasp):
    t, _ = us.shape
    consts = _dft_constants()
    ncol = D_HY // LANE
    nsub = HY_PAIRS_PER_COL // HY_PAIRS_PER_STEP
    full = lambda a: pl.BlockSpec(a.shape, lambda j, k: (0,) * a.ndim)
    nat = lambda off: pl.BlockSpec((t, LANE), lambda j, k: (0, off + j))
    tiles = pltpu.VMEM((HY_PAIRS_PER_COL, FFT_N2, LANE), F32)
    return pl.pallas_call(
        _hyena_kernel,
        grid=(ncol, nsub),
        in_specs=[nat(0), nat(ncol), nat(2 * ncol),
                  pl.BlockSpec((HY_ORDER, 2, HY_PAIRS_PER_STEP, 2 * FFT_N1, FFT_N2),
                               lambda j, k: (0, 0, j * nsub + k, 0, 0)),
                  pl.BlockSpec((HY_ORDER, HY_PAIRS_PER_STEP, 1, LANE), lambda j, k: (0, j * nsub + k, 0, 0))]
                 + [full(a) for a in consts],
        out_specs=pl.BlockSpec((t, LANE), lambda j, k: (0, j)),
        out_shape=jax.ShapeDtypeStruct((t, D_HY), us.dtype),
        scratch_shapes=[tiles, tiles, tiles, tiles, pltpu.VMEM((LANE * HY_TILE_PITCH, FFT_N2), F32)],
        compiler_params=pltpu.CompilerParams(
            dimension_semantics=("parallel", "arbitrary"), vmem_limit_bytes=VMEM_LIMIT_BYTES),
        name="hyena_conv",
    )(us, us, us, kf, biasp, *consts)


def _filter_time_kernel(w1t_ref, w1c_ref, w1s_ref, b1_ref, w2_ref, b2_ref, w3_ref, b3_ref, fr_ref,
                        wf_ref, wb_ref, df_ref, db_ref, o_ref, h_ref, k_ref, *, length):
    n_fft = 2 * length
    hp = lax.Precision.HIGHEST

    @pl.when(pl.program_id(0) == 0)
    def _():
        pos = lax.broadcasted_iota(jnp.int32, (1, n_fft), 1)
        lag = jnp.where(pos < length, pos, n_fft - pos).astype(F32)
        t = lag / float(length - 1)
        w = (2.0 * math.pi / length) * lag
        band_step = (HY_BANDS - 1 - 1e-4) / (HY_BANDS - 1)
        bands = 1e-4 + band_step * lax.broadcasted_iota(jnp.int32, (HY_BANDS, 1), 0).astype(F32)
        ang = bands * w
        fr = fr_ref[...]
        h = (w1t_ref[...] * t + jnp.dot(w1c_ref[...], jnp.cos(ang), preferred_element_type=F32, precision=hp)
             - jnp.dot(w1s_ref[...], jnp.sin(ang), preferred_element_type=F32, precision=hp))
        h = jnp.sin(fr * (h + b1_ref[...]))
        h = jnp.sin(fr * (jnp.dot(w2_ref[...], h, preferred_element_type=F32, precision=hp) + b2_ref[...]))
        h = jnp.sin(fr * (jnp.dot(w3_ref[...], h, preferred_element_type=F32, precision=hp) + b3_ref[...]))
        hi = h.astype(BF16)
        h_ref[0] = hi
        h_ref[1] = (h - hi.astype(F32)).astype(BF16)

    def dot3(w, lo, hi_):
        w_hi = w.astype(BF16)
        w_lo = (w - w_hi.astype(F32)).astype(BF16)
        h_hi, h_lo = h_ref[0, :, lo:hi_], h_ref[1, :, lo:hi_]
        return (jnp.dot(w_hi, h_hi, preferred_element_type=F32) + jnp.dot(w_hi, h_lo, preferred_element_type=F32)
                + jnp.dot(w_lo, h_hi, preferred_element_type=F32))

    pos = lax.broadcasted_iota(jnp.int32, (1, length), 1)
    tf = pos.astype(F32) / float(length - 1)
    tb = (length - pos).astype(F32) / float(length - 1)
    kf = dot3(wf_ref[...], 0, length) * jnp.exp(-tf * df_ref[...])
    kb = dot3(wb_ref[...], length, n_fft) * jnp.exp(-tb * db_ref[...])
    kb = jnp.where(pos == 0, 0.0, kb)
    inv = 1.0 / (jnp.sum(jnp.abs(kf), axis=1, keepdims=True) + jnp.sum(jnp.abs(kb), axis=1, keepdims=True))
    k_ref[:, :length] = kf * inv
    k_ref[:, length:] = kb * inv
    cb = wf_ref.shape[0]
    for n1 in range(FILTER_TILE_ROWS):
        row = k_ref[:, n1 * FFT_N2:(n1 + 1) * FFT_N2] if n1 < FFT_N1 else jnp.zeros((cb, FFT_N2), F32)
        o_ref[pl.ds(n1, cb, stride=FILTER_TILE_ROWS), :] = row


def _filter_time(length, w1, b1, w2, b2, w3, b3, freq, w_out, cb=256):
    col = lambda v: v.reshape(-1, 1)
    w1t = w1.T
    n_ch = w_out.shape[1]
    deltas = jnp.abs(jnp.linspace(math.log(HY_TARGET) / HY_SLOW_PCT, math.log(HY_TARGET) / HY_FAST_PCT,
                                  n_ch, dtype=F32)).reshape(n_ch, 1)
    wot = w_out.T
    nb = D_HY // cb
    small = lambda a: pl.BlockSpec(a.shape, lambda i: (0,) * a.ndim)
    fwd = lambda i: ((i // nb) * HY_DIRS * nb + i % nb, 0)
    bwd = lambda i: ((i // nb) * HY_DIRS * nb + nb + i % nb, 0)
    ins = [w1t[:, 0:1], w1t[:, 1:1 + HY_BANDS], w1t[:, 1 + HY_BANDS:], col(b1), w2.T, col(b2), w3.T, col(b3),
           col(freq)]
    hy_ff = w2.shape[0]
    return pl.pallas_call(
        functools.partial(_filter_time_kernel, length=length),
        grid=(HY_ORDER * nb,),
        in_specs=[small(a) for a in ins] + [pl.BlockSpec((cb, hy_ff), fwd), pl.BlockSpec((cb, hy_ff), bwd),
                                            pl.BlockSpec((cb, 1), fwd), pl.BlockSpec((cb, 1), bwd)],
        out_specs=pl.BlockSpec((cb * FILTER_TILE_ROWS, FFT_N2), lambda i: (i, 0)),
        out_shape=jax.ShapeDtypeStruct((HY_ORDER * D_HY * FILTER_TILE_ROWS, FFT_N2), F32),
        scratch_shapes=[pltpu.VMEM((2, hy_ff, 2 * length), BF16), pltpu.VMEM((cb, 2 * length), F32)],
        compiler_params=pltpu.CompilerParams(
            dimension_semantics=("arbitrary",), vmem_limit_bytes=VMEM_LIMIT_BYTES),
        name="hyena_filter_time",
    )(*ins, wot, wot, deltas, deltas)


def _filter_spec_kernel(k_ref, w1p_ref, t1_ref, t2_ref, f2_ref, o_ref):
    cb = k_ref.shape[0] // FILTER_TILE_ROWS
    kt = k_ref[...].reshape(cb, FILTER_TILE_ROWS, FFT_N2)[:, :FFT_N1, :]
    xp = jnp.swapaxes(kt.reshape(cb // 2, 2 * FFT_N1, FFT_N2), 1, 2)
    spec = _fwd_spectrum(xp, w1p_ref[...], t1_ref[...], t2_ref[...], f2_ref[...])
    for par in range(2):
        o_ref[par, :, :FFT_N1, :] = spec[par][0]
        o_ref[par, :, FFT_N1:, :] = spec[par][1]


def _filter_spectrum(kt, cb=128):
    consts = _dft_constants(real_input=True)[:4]
    nb = D_HY // cb
    full = lambda a: pl.BlockSpec(a.shape, lambda i: (0,) * a.ndim)
    return pl.pallas_call(
        _filter_spec_kernel,
        grid=(HY_ORDER * nb,),
        in_specs=[pl.BlockSpec((cb * FILTER_TILE_ROWS, FFT_N2), lambda i: (i, 0))] + [full(a) for a in consts],
        out_specs=pl.BlockSpec((None, 2, cb // 2, 2 * FFT_N1, FFT_N2), lambda i: (i // nb, 0, i % nb, 0, 0)),
        out_shape=jax.ShapeDtypeStruct((HY_ORDER, 2, D_HY // 2, 2 * FFT_N1, FFT_N2), F32),
        compiler_params=pltpu.CompilerParams(
            dimension_semantics=("parallel",), vmem_limit_bytes=VMEM_LIMIT_BYTES),
        name="hyena_filter_spectrum",
    )(kt, *consts)


def _hyena(us, w1, b1, w2, b2, w3, b3, freq, w_out, bias):
    bsz, length, _ = us.shape
    assert 2 * length == FFT_N and bsz == 2, "one complex transform carries exactly two batch rows"
    kf = _filter_spectrum(_filter_time(length, w1, b1, w2, b2, w3, b3, freq, w_out))
    biasp = jnp.repeat(bias.reshape(HY_ORDER, D_HY // 2, 1, 2), FFT_N1, axis=-1)
    return _hyena_conv(us.reshape(bsz * length, -1), kf, biasp).reshape(bsz, length, D_HY)


def kernel(x, c, ctx, c_ctx, w_ada, b_ada, norm_g, ffn_w_gate, ffn_w_up, ffn_w_down, w_in,
           s5_lam_re, s5_lam_im, s5_log_dt, s5_b_re, s5_b_im, s5_c_re, s5_c_im, s5_d,
           hy_short_w, hy_short_b, hy_w1, hy_b1, hy_w2, hy_b2, hy_w3, hy_b3, hy_freq, hy_w_out,
           hy_bias, w_pa, w_pb, w_out, final_g):
    bsz, seq, d = x.shape
    ctx_len = ctx.shape[1]
    n_rows = seq // GRID_W
    depth = w_ada.shape[0]
    assert depth == 1, "context-token outputs are only dropped by the last layer"
    l = 0

    c_rows = jnp.concatenate([c, c_ctx[None, :], jnp.zeros((8 - bsz - 1, d), F32)], axis=0)
    mod_all = _ada_mod(c_rows, w_ada[l], b_ada[l])
    mod = mod_all[:bsz].reshape(bsz, N_SUB, N_MOD, 1, d)
    mod_c = mod_all[bsz:bsz + 1].reshape(1, N_SUB, N_MOD, 1, d)

    def mods(m, sub):
        return tuple(m[:, sub, k] for k in range(N_MOD))

    wg, wu, wd = ffn_w_gate[l], ffn_w_up[l], ffn_w_down[l]
    w_in_b = w_in[l]

    xt = x.reshape(bsz * seq, d)
    ct = ctx.reshape(bsz * ctx_len, d)

    xt = _ffn_sublayer(xt, mods(mod, 0), norm_g[l, 0], wg, wu, wd, 0)
    ct = _ffn_sublayer(ct, mods(mod_c, 0), norm_g[l, 0], wg, wu, wd, 0)

    assert GRID_W * n_rows == seq
    u_s5, us_hy, sig_gates = _in_proj(xt, mod[:, 1, 0], mod[:, 1, 1], norm_g[l, 1], w_in_b,
                                      hy_short_w[l], hy_short_b[l], n_u=I_HY, n_hy=I_GA - I_HY)
    (u_ctx,) = _in_proj(ct, mod_c[:, 1, 0], mod_c[:, 1, 1], norm_g[l, 1], w_in_b[:, :D_S5])

    y_s5 = _s5_bidirectional(u_s5.reshape(bsz, seq, D_S5), u_ctx.reshape(bsz, ctx_len, D_S5),
                             s5_lam_re[l], s5_lam_im[l], s5_log_dt[l],
                             s5_b_re[l], s5_b_im[l], s5_c_re[l], s5_c_im[l], s5_d[l])
    y_hy = _hyena(us_hy.reshape(bsz, seq, I_GA - I_HY),
                  hy_w1[l], hy_b1[l], hy_w2[l], hy_b2[l], hy_w3[l], hy_b3[l], hy_freq[l],
                  hy_w_out[l], hy_bias[l])

    xt = _merge(xt, mod[:, 1, 2], y_s5.reshape(bsz * seq, D_S5), y_hy.reshape(bsz * seq, D_HY),
                sig_gates, w_pa[l].astype(BF16), w_pb[l].astype(BF16), w_out[l].astype(BF16))

    xt = _ffn_sublayer(xt, mods(mod, 2), norm_g[l, 2], wg, wu, wd, 1, final_gain=final_g)
    return xt.reshape(bsz, seq, d)
```

```python
import functools
import math

import jax
import jax.numpy as jnp
import numpy as np
from jax import lax
from jax.experimental import pallas as pl
from jax.experimental.pallas import tpu as pltpu

F32 = jnp.float32
BF16 = jnp.bfloat16

D_MODEL = 2048
GRID_W = 64
D_S5 = 1024
S5_GROUP = 16
S5_GROUPS = D_S5 // S5_GROUP
S5_STATE = 64
S5_DIRS = 2
LAMBDA_RE_MAX = -1e-4
S5_CHUNK = 16
LANE = 128
S5_GROUPS_PER_COL = LANE // S5_GROUP
D_HY = 1024
HY_ORDER = 2
HY_DIRS = 2
HY_SHORT = 3
HY_EMB = 33
HY_BANDS = (HY_EMB - 1) // 2
HY_TARGET = 1e-2
HY_FAST_PCT = 0.3
HY_SLOW_PCT = 1.5
FFT_N1 = 64
FFT_N2 = 128
FFT_N = FFT_N1 * FFT_N2
FILTER_TILE_ROWS = FFT_N1 + 8
HY_TILE_PITCH = FILTER_TILE_ROWS
HY_PAIRS_PER_COL = LANE // 2
HY_PAIRS_PER_STEP = 16
I_HY = D_S5
I_GA = D_S5 + (HY_ORDER + 1) * D_HY
I_GB = I_GA + D_MODEL
D_IN = I_GB + D_MODEL
D_FF = 5632
N_SUB = 3
N_MOD = 3
HALF_STEP = 0.5
RMS_EPS = 1e-6

VMEM_LIMIT_BYTES = 58 * 1024 * 1024


def _rms_mod(x, gain, shift, scale):
    ms = jnp.mean(x * x, axis=-1, keepdims=True)
    y = x * lax.rsqrt(ms + RMS_EPS) * gain
    return y * (1.0 + scale) + shift


def _ada_kernel(c_ref, w_ref, b_ref, o_ref):
    c = c_ref[...]
    a = c * jax.nn.sigmoid(c)
    o_ref[...] = jnp.dot(a, w_ref[...], preferred_element_type=F32,
                         precision=lax.Precision.HIGHEST) + b_ref[...]


def _ada_mod(c_rows, w, b, tn=1024):
    rows, d = c_rows.shape
    n = w.shape[1]
    return pl.pallas_call(
        _ada_kernel,
        grid=(n // tn,),
        in_specs=[pl.BlockSpec((rows, d), lambda j: (0, 0)),
                  pl.BlockSpec((d, tn), lambda j: (0, j)),
                  pl.BlockSpec((1, tn), lambda j: (0, j))],
        out_specs=pl.BlockSpec((rows, tn), lambda j: (0, j)),
        out_shape=jax.ShapeDtypeStruct((rows, n), F32),
        compiler_params=pltpu.CompilerParams(
            dimension_semantics=("arbitrary",), vmem_limit_bytes=VMEM_LIMIT_BYTES),
        name="ada_mod",
    )(c_rows, w, b.reshape(1, n))


def _ffn_kernel(x_ref, shift_ref, scale_ref, gate_ref, gain_ref, wg_ref, wu_ref, wd_ref,
                fg_ref, o_ref, h_ref, *, final_norm):
    j = pl.program_id(1)

    @pl.when(j == 0)
    def _():
        h_ref[...] = _rms_mod(x_ref[...], gain_ref[...], shift_ref[...], scale_ref[...]).astype(BF16)
        o_ref[...] = jnp.zeros_like(o_ref)

    h = h_ref[...]
    g = jnp.dot(h, wg_ref[...].astype(BF16), preferred_element_type=F32)
    u = jnp.dot(h, wu_ref[...].astype(BF16), preferred_element_type=F32)
    a = (g * jax.nn.sigmoid(g) * u).astype(BF16)
    o_ref[...] += jnp.dot(a, wd_ref[...].astype(BF16), preferred_element_type=F32)

    @pl.when(j == pl.num_programs(1) - 1)
    def _():
        y = x_ref[...] + (HALF_STEP * gate_ref[...]) * o_ref[...]
        if final_norm:
            ms = jnp.mean(y * y, axis=-1, keepdims=True)
            y = y * lax.rsqrt(ms + RMS_EPS) * fg_ref[...]
        o_ref[...] = y


def _ffn_sublayer(x, mods, gain, wg, wu, wd, which, final_gain=None, tm=1024, tf=256):
    t, d = x.shape
    bm = mods[0].shape[0]
    tm = min(tm, t // bm)
    blocks_per_batch = (t // bm) // tm
    dff = wg.shape[2]
    final_norm = final_gain is not None
    fg = final_gain if final_norm else gain
    mod_spec = pl.BlockSpec((None, 1, d), lambda i, j: (i // blocks_per_batch, 0, 0))
    vec_spec = pl.BlockSpec((1, d), lambda i, j: (0, 0))
    return pl.pallas_call(
        functools.partial(_ffn_kernel, final_norm=final_norm),
        grid=(t // tm, dff // tf),
        in_specs=[pl.BlockSpec((tm, d), lambda i, j: (i, 0)),
                  mod_spec, mod_spec, mod_spec, vec_spec,
                  pl.BlockSpec((None, d, tf), lambda i, j: (which, 0, j)),
                  pl.BlockSpec((None, d, tf), lambda i, j: (which, 0, j)),
                  pl.BlockSpec((None, tf, d), lambda i, j: (which, j, 0)),
                  vec_spec],
        out_specs=pl.BlockSpec((tm, d), lambda i, j: (i, 0)),
        out_shape=jax.ShapeDtypeStruct((t, d), F32),
        scratch_shapes=[pltpu.VMEM((tm, d), BF16)],
        compiler_params=pltpu.CompilerParams(
            dimension_semantics=("parallel", "arbitrary"), vmem_limit_bytes=VMEM_LIMIT_BYTES),
        name="ffn_final" if final_norm else "ffn",
    )(x, *mods, gain.reshape(1, d), wg, wu, wd, fg.reshape(1, d))


def _proj_kernel(x_ref, shift_ref, scale_ref, gain_ref, w_ref, sw_ref, sb_ref, *rest, n_u, n_hy, row_len, part):
    o_refs, h_ref = rest[:-1], rest[-1]
    j = pl.program_id(1)

    @pl.when(j == 0)
    def _():
        h_ref[...] = _rms_mod(x_ref[...], gain_ref[...], shift_ref[...], scale_ref[...]).astype(BF16)

    tm, tn = h_ref.shape[0], w_ref.shape[1]

    def in_parts(o_ref, epilogue):
        for c in range(0, tn, part):
            p = jnp.dot(h_ref[...], w_ref[:, c:c + part].astype(BF16), preferred_element_type=F32)
            o_ref[:, c:c + part] = epilogue(p, c).astype(o_ref.dtype)

    def short_conv(p, c):
        col = lax.broadcasted_iota(jnp.int32, p.shape, 0) % row_len
        prev = jnp.where(col == 0, 0.0, pltpu.roll(p, 1, 0))
        nxt = jnp.where(col == row_len - 1, 0.0, pltpu.roll(p, tm - 1, 0))
        sw = sw_ref[:, c:c + part]
        return sb_ref[:, c:c + part] + prev * sw[0:1] + p * sw[1:2] + nxt * sw[2:3]

    if n_hy == 0:
        in_parts(o_refs[0], lambda p, c: p)
        return

    @pl.when(j < n_u)
    def _():
        in_parts(o_refs[0], lambda p, c: p)

    @pl.when((j >= n_u) & (j < n_u + n_hy))
    def _():
        in_parts(o_refs[1], short_conv)

    @pl.when(j >= n_u + n_hy)
    def _():
        in_parts(o_refs[2], lambda p, c: jax.nn.sigmoid(p))


def _in_proj(x, shift, scale, gain, w, short_w=None, short_b=None, n_u=D_S5, n_hy=0, row_len=GRID_W,
             tm=1024, tn=512, part=256):
    t, d = x.shape
    bm = shift.shape[0]
    tm = min(tm, t // bm)
    blocks_per_batch = (t // bm) // tm
    n = w.shape[1]
    assert tm % row_len == 0 and (t // bm) % tm == 0
    bu, bh = n_u // tn, n_hy // tn
    bg = n // tn - bu - bh
    mod_spec = pl.BlockSpec((None, 1, d), lambda i, j: (i // blocks_per_batch, 0, 0))
    out_shape = [jax.ShapeDtypeStruct((t, n_u), F32)]
    out_specs = [pl.BlockSpec((tm, tn), lambda i, j: (i, jnp.minimum(j, bu - 1)))]
    if bh:
        out_shape += [jax.ShapeDtypeStruct((t, n_hy), BF16), jax.ShapeDtypeStruct((t, bg * tn), BF16)]
        out_specs += [pl.BlockSpec((tm, tn), lambda i, j: (i, jnp.clip(j - bu, 0, bh - 1))),
                      pl.BlockSpec((tm, tn), lambda i, j: (i, jnp.maximum(j - bu - bh, 0)))]
        sw, sb = short_w, short_b.reshape(1, n_hy)
        hy_blk = lambda i, j: (0, jnp.clip(j - bu, 0, bh - 1))
    else:
        sw, sb = jnp.zeros((HY_SHORT, tn), F32), jnp.zeros((1, tn), F32)
        hy_blk = lambda i, j: (0, 0)
    return pl.pallas_call(
        functools.partial(_proj_kernel, n_u=bu, n_hy=bh, row_len=row_len, part=part),
        grid=(t // tm, n // tn),
        in_specs=[pl.BlockSpec((tm, d), lambda i, j: (i, 0)),
                  mod_spec, mod_spec,
                  pl.BlockSpec((1, d), lambda i, j: (0, 0)),
                  pl.BlockSpec((d, tn), lambda i, j: (0, j)),
                  pl.BlockSpec((HY_SHORT, tn), hy_blk),
                  pl.BlockSpec((1, tn), hy_blk)],
        out_specs=out_specs,
        out_shape=out_shape,
        scratch_shapes=[pltpu.VMEM((tm, d), BF16)],
        compiler_params=pltpu.CompilerParams(
            dimension_semantics=("parallel", "arbitrary"), vmem_limit_bytes=VMEM_LIMIT_BYTES),
        name="in_proj",
    )(x, shift, scale, gain.reshape(1, d), w, sw, sb)


def _gelu_tanh(x):
    return 0.5 * x * (1.0 + jnp.tanh(math.sqrt(2.0 / math.pi) * (x + 0.044715 * (x * x * x))))


def _merge_kernel(x_ref, gate_ref, ys_ref, yh_ref, ga_ref, gb_ref, wpa_lo_ref, wpa_hi_ref, wpb_ref,
                  wout_ref, o_ref, s_ref, acc_ref):
    j = pl.program_id(1)

    @pl.when(j == 0)
    def _():
        s_ref[...] = _gelu_tanh(ys_ref[...].astype(F32)).astype(BF16)
        acc_ref[...] = jnp.zeros_like(acc_ref)

    s = s_ref[...]
    pa_lo = jnp.dot(s, wpa_lo_ref[...], preferred_element_type=F32)
    pa_hi = jnp.dot(s, wpa_hi_ref[...], preferred_element_type=F32)
    y_a = pa_lo * jax.nn.sigmoid(pa_hi)
    y_b = jnp.dot(yh_ref[...], wpb_ref[...], preferred_element_type=F32)
    m = ga_ref[...].astype(F32) * y_a + gb_ref[...].astype(F32) * y_b
    acc_ref[...] += jnp.dot(m.astype(BF16), wout_ref[...], preferred_element_type=F32)

    @pl.when(j == pl.num_programs(1) - 1)
    def _():
        o_ref[...] = x_ref[...] + gate_ref[...] * acc_ref[...]


def _merge(x, gate, y_s5, y_hy, sig_gates, w_pa, w_pb, w_out, tm=512, tn=1024):
    t, d = x.shape
    bm = gate.shape[0]
    blocks_per_batch = (t // bm) // tm
    nj = d // tn
    ds5 = y_s5.shape[1]
    dhy = y_hy.shape[1]
    return pl.pallas_call(
        _merge_kernel,
        grid=(t // tm, nj),
        in_specs=[pl.BlockSpec((tm, d), lambda i, j: (i, 0)),
                  pl.BlockSpec((None, 1, d), lambda i, j: (i // blocks_per_batch, 0, 0)),
                  pl.BlockSpec((tm, ds5), lambda i, j: (i, 0)),
                  pl.BlockSpec((tm, dhy), lambda i, j: (i, 0)),
                  pl.BlockSpec((tm, tn), lambda i, j: (i, j)),
                  pl.BlockSpec((tm, tn), lambda i, j: (i, nj + j)),
                  pl.BlockSpec((ds5, tn), lambda i, j: (0, j)),
                  pl.BlockSpec((ds5, tn), lambda i, j: (0, nj + j)),
                  pl.BlockSpec((dhy, tn), lambda i, j: (0, j)),
                  pl.BlockSpec((tn, d), lambda i, j: (j, 0))],
        out_specs=pl.BlockSpec((tm, d), lambda i, j: (i, 0)),
        out_shape=jax.ShapeDtypeStruct((t, d), F32),
        scratch_shapes=[pltpu.VMEM((tm, ds5), BF16), pltpu.VMEM((tm, d), F32)],
        compiler_params=pltpu.CompilerParams(
            dimension_semantics=("parallel", "arbitrary"), vmem_limit_bytes=VMEM_LIMIT_BYTES),
        name="merge",
    )(x, gate, y_s5, y_hy, sig_gates, sig_gates, w_pa, w_pa, w_pb, w_out)


def _s5_weights_kernel(par_ref, bt_ref, c_ref, d_ref, bpow_ref, toep_ref, cpow_ref, tab_ref, ca_ref, *, n_steps):
    t, h, p = S5_CHUNK, S5_GROUP, S5_STATE
    lanes = 2 * p
    hp = lax.Precision.HIGHEST
    sgn = jnp.where(lax.broadcasted_iota(jnp.int32, (1, lanes), 1) < p, -1.0, 1.0)
    par = par_ref[...]
    gsum = None
    for d in range(S5_DIRS):
        lr = jnp.minimum(par[3 * d:3 * d + 1], LAMBDA_RE_MAX)
        li = par[3 * d + 1:3 * d + 2]
        dt = jnp.exp(par[3 * d + 2:3 * d + 3])
        zr, zi = lr * dt, li * dt

        def apow(j):
            mag = jnp.exp(j * zr)
            return mag * jnp.cos(j * zi), sgn * (mag * jnp.sin(j * zi))

        def cmul(x, a1, a2):
            return x * a1 + pltpu.roll(x, p, 1) * a2

        a1, a2 = apow(lax.broadcasted_iota(jnp.int32, (t + 1, 1), 0).astype(F32))
        nr, ni = a1[1:2] - 1.0, sgn * a2[1:2]
        den = lr * lr + li * li
        f_re = (nr * lr + ni * li) / den
        f_im = (ni * lr - nr * li) / den
        bbar = cmul(bt_ref[d], f_re, sgn * f_im)
        cc = c_ref[d]
        ca = [cmul(cc, a1[j:j + 1], a2[j:j + 1]) * (-sgn) for j in range(t + 1)]
        ca_ref[...] = jnp.zeros_like(ca_ref)
        for k in range(t):
            e_b, e_c = (t - 1 - k, k + 1) if d == 0 else (k, t - k)
            bpow_ref[k * h:(k + 1) * h, d * lanes:(d + 1) * lanes] = (
                cmul(bbar, a1[e_b:e_b + 1], a2[e_b:e_b + 1]).astype(BF16))
            cpow_ref[k * h:(k + 1) * h, d * lanes:(d + 1) * lanes] = ca[e_c].astype(BF16)
            l = t - 1 + k if d == 0 else t - 1 - k
            ca_ref[l * h:(l + 1) * h, :] = ca[k]
        g = lax.dot_general(bbar, ca_ref[...], (((1,), (1,)), ((), ())), preferred_element_type=F32, precision=hp)
        gsum = g if gsum is None else gsum + g
        for s in range(n_steps):
            s1, s2 = apow(float(t * 2 ** s))
            r = d * 2 * n_steps + 2 * s
            tab_ref[r:r + 1, :] = s1
            tab_ref[r + 1:r + 2, :] = s2
    wide = gsum.shape[1]
    col = lax.broadcasted_iota(jnp.int32, (h, wide), 1)
    row = lax.broadcasted_iota(jnp.int32, (h, wide), 0)
    gsum = gsum + jnp.where(col - (t - 1) * h == row, d_ref[...], 0.0)
    for k in range(t):
        off = (t - 1 - k) * h
        shifted = gsum if off == 0 else pltpu.roll(gsum, wide - off, 1)
        toep_ref[k * h:(k + 1) * h, :] = shifted[:, :t * h].astype(BF16)


def _s5_weights(lam_re, lam_im, log_dt, b_re, b_im, c_re, c_im, d_skip, n_steps):
    g, p, h, t = S5_GROUPS, S5_STATE, S5_GROUP, S5_CHUNK
    cat2 = lambda a: jnp.concatenate([a, a], axis=-1)
    par = jnp.stack([cat2(lam_re), cat2(lam_im), jnp.broadcast_to(log_dt[..., None], (S5_DIRS, g, 2 * p))], axis=1)
    par = par.transpose(2, 0, 1, 3).reshape(g, 3 * S5_DIRS, 2 * p)
    btc = jnp.concatenate([b_re, b_im], axis=2).transpose(1, 0, 3, 2)
    ccat = jnp.concatenate([c_re, c_im], axis=3).transpose(1, 0, 2, 3)
    wide = 2 * t * h
    drow = jnp.zeros((g, 1, wide), F32).at[:, 0, (t - 1) * h:t * h].set(d_skip.reshape(g, h))
    w = t * h
    sq = pl.BlockSpec((None, w, w), lambda i: (i, 0, 0))
    return pl.pallas_call(
        functools.partial(_s5_weights_kernel, n_steps=n_steps),
        grid=(g,),
        in_specs=[pl.BlockSpec((None, 3 * S5_DIRS, 2 * p), lambda i: (i, 0, 0)),
                  pl.BlockSpec((None, S5_DIRS, h, 2 * p), lambda i: (i, 0, 0, 0)),
                  pl.BlockSpec((None, S5_DIRS, h, 2 * p), lambda i: (i, 0, 0, 0)),
                  pl.BlockSpec((None, 1, wide), lambda i: (i, 0, 0))],
        out_specs=[sq, sq, sq, pl.BlockSpec((None, 4 * n_steps, 2 * p), lambda i: (i, 0, 0))],
        out_shape=[jax.ShapeDtypeStruct((g, w, w), BF16)] * 3 + [jax.ShapeDtypeStruct((g, 4 * n_steps, 2 * p), F32)],
        scratch_shapes=[pltpu.VMEM((wide, 2 * p), F32)],
        compiler_params=pltpu.CompilerParams(
            dimension_semantics=("parallel",), vmem_limit_bytes=VMEM_LIMIT_BYTES),
        name="s5_weights",
    )(par, btc, ccat, drow)


def _s5_kernel(u_ref, uc_ref, bpow_ref, toep_ref, cpow_ref, tab_ref, y_ref, ucat_ref, ucc_ref, yacc_ref,
               *, bsz, n_steps):
    t, gw = S5_CHUNK, S5_GROUP
    per = LANE // gw
    n_lat = u_ref.shape[0] // (bsz * t)
    n_ctx = uc_ref.shape[0] // (bsz * t)
    n_ch = n_lat + n_ctx
    rows = bsz * n_ch
    half = 2 * S5_STATE
    for k in range(t):
        ucat_ref[:, k * LANE:(k + 1) * LANE] = u_ref[pl.ds(k, bsz * n_lat, stride=t), :]
        ucc_ref[:, k * LANE:(k + 1) * LANE] = uc_ref[pl.ds(k, bsz * n_ctx, stride=t), :]
    yacc_ref[...] = jnp.zeros_like(yacc_ref)
    rib = lax.broadcasted_iota(jnp.int32, (rows, half), 0) % n_ch
    lane = lax.broadcasted_iota(jnp.int32, (1, LANE), 1)

    def cmul_add(acc, sh, a1, a2):
        return acc + a1 * sh + a2 * pltpu.roll(sh, S5_STATE, 1)

    def gather(src_ref, gl):
        cols = []
        for j in range(t // per):
            acc = None
            for tt in range(per):
                k = j * per + tt
                r = pltpu.roll(src_ref[:, k * LANE:(k + 1) * LANE], (gw * tt - gw * gl) % LANE, 1)
                acc = r if acc is None else jnp.where((lane >= gw * tt) & (lane < gw * (tt + 1)), r, acc)
            cols.append(acc)
        return jnp.concatenate(cols, axis=1).astype(BF16)

    def group(gl, carry):
        ul = gather(ucat_ref, gl)
        uc = gather(ucc_ref, gl)
        bpow = bpow_ref[gl]
        zl = jnp.dot(ul, bpow, preferred_element_type=F32)
        zc = jnp.dot(uc, bpow, preferred_element_type=F32)
        fparts, bparts = [], []
        for b in range(bsz):
            lat = slice(b * n_lat, (b + 1) * n_lat)
            ctx = slice(b * n_ctx, (b + 1) * n_ctx)
            fparts += [zc[ctx, :half], zl[lat, :half]]
            bparts += [zl[lat, half:], zc[ctx, half:]]
        fw = jnp.concatenate(fparts, axis=0)
        bw = jnp.concatenate(bparts, axis=0)
        tab = tab_ref[gl]
        for s in range(n_steps):
            d = 1 << s
            sh = jnp.where(rib >= d, pltpu.roll(fw, d, 0), 0.0)
            fw = cmul_add(fw, sh, tab[2 * s:2 * s + 1], tab[2 * s + 1:2 * s + 2])
            o = 2 * n_steps
            sh = jnp.where(rib < n_ch - d, pltpu.roll(bw, rows - d, 0), 0.0)
            bw = cmul_add(bw, sh, tab[o + 2 * s:o + 2 * s + 1], tab[o + 2 * s + 1:o + 2 * s + 2])
        fe = jnp.where(rib >= 1, pltpu.roll(fw, 1, 0), 0.0)
        be = jnp.where(rib < n_ch - 1, pltpu.roll(bw, rows - 1, 0), 0.0)
        fl = jnp.concatenate([fe[b * n_ch + n_ctx:(b + 1) * n_ch] for b in range(bsz)], axis=0)
        bl = jnp.concatenate([be[b * n_ch:b * n_ch + n_lat] for b in range(bsz)], axis=0)
        st = jnp.concatenate([fl, bl], axis=1).astype(BF16)
        y = (jnp.dot(ul, toep_ref[gl], preferred_element_type=F32)
             + lax.dot_general(st, cpow_ref[gl], (((1,), (1,)), ((), ())), preferred_element_type=F32))
        mine = (lane >= gw * gl) & (lane < gw * (gl + 1))
        for i in range(t):
            src = y[:, (i // per) * LANE:(i // per + 1) * LANE]
            r = pltpu.roll(src, (gw * gl - gw * (i % per)) % LANE, 1)
            blk = slice(i * LANE, (i + 1) * LANE)
            yacc_ref[:, blk] = jnp.where(mine, r, yacc_ref[:, blk])
        return carry

    lax.fori_loop(0, S5_GROUPS_PER_COL, group, 0)
    for i in range(t):
        y_ref[pl.ds(i, bsz * n_lat, stride=t), :] = yacc_ref[:, i * LANE:(i + 1) * LANE]


def _s5_mix(u, uc, bpw, toep, cpw, tab, bsz):
    rl, dm = u.shape
    rc = uc.shape[0]
    t = S5_CHUNK
    w = t * S5_GROUP
    n_steps = tab.shape[1] // 4
    gpc = S5_GROUPS_PER_COL
    wspec = pl.BlockSpec((gpc, w, w), lambda i: (i, 0, 0))
    return pl.pallas_call(
        functools.partial(_s5_kernel, bsz=bsz, n_steps=n_steps),
        grid=(dm // LANE,),
        in_specs=[pl.BlockSpec((rl, LANE), lambda i: (0, i)),
                  pl.BlockSpec((rc, LANE), lambda i: (0, i)),
                  wspec, wspec, wspec,
                  pl.BlockSpec((gpc, 4 * n_steps, tab.shape[2]), lambda i: (i, 0, 0))],
        out_specs=pl.BlockSpec((rl, LANE), lambda i: (0, i)),
        out_shape=jax.ShapeDtypeStruct(u.shape, F32),
        scratch_shapes=[pltpu.VMEM((rl // t, t * LANE), F32), pltpu.VMEM((rc // t, t * LANE), F32),
                        pltpu.VMEM((rl // t, t * LANE), F32)],
        compiler_params=pltpu.CompilerParams(
            dimension_semantics=("parallel",), vmem_limit_bytes=VMEM_LIMIT_BYTES),
        name="s5_mix",
    )(u, uc, bpw, toep, cpw, tab)


def _s5_bidirectional(u, u_ctx, lam_re, lam_im, log_dt, b_re, b_im, c_re, c_im, d_skip):
    bsz, length, dm = u.shape
    ctx_len = u_ctx.shape[1]
    t = S5_CHUNK
    n_lat, n_ctx = length // t, ctx_len // t
    n_steps = max(1, math.ceil(math.log2(n_lat + n_ctx)))
    bpw, toep, cpw, tab = _s5_weights(lam_re, lam_im, log_dt, b_re, b_im, c_re, c_im, d_skip, n_steps)
    y = _s5_mix(u.reshape(bsz * length, dm), u_ctx.reshape(bsz * ctx_len, dm), bpw, toep, cpw, tab, bsz)
    return y.reshape(bsz, length, dm)


def _dft_constants(real_input=False):
    n1 = np.arange(FFT_N1)
    n2 = np.arange(FFT_N2)
    half = FFT_N1 // 2
    th = 2 * np.pi * np.outer(n1, n1) / FFT_N1
    c1, s1 = np.cos(th), np.sin(th)
    if real_input:
        w1 = np.concatenate([c1, -s1], axis=1)
    else:
        w1 = np.concatenate([np.concatenate([c1[:half], -s1[:half]], axis=1),
                             np.concatenate([s1[:half], c1[:half]], axis=1)], axis=0)
    z = np.zeros_like(w1)
    w1p = np.block([[w1, z], [z, w1]])
    ph = 2 * np.pi * np.outer(n2, n1) / FFT_N
    t1 = np.concatenate([np.cos(ph), np.cos(ph)], axis=1)
    t2 = np.concatenate([np.sin(ph), -np.sin(ph)], axis=1)
    ps = 2 * np.pi * np.outer(n2, n2) / FFT_N2
    f2 = np.concatenate([np.cos(ps), -np.sin(ps)], axis=1)
    g2 = np.concatenate([np.cos(ps), np.sin(ps)], axis=1)
    c2, s2 = np.cos(ph).T, np.sin(ph).T
    wi = np.concatenate([np.concatenate([c1[:, :half], s1[:, :half]], axis=1),
                         np.concatenate([-s1[:, :half], c1[:, :half]], axis=1)], axis=0) / FFT_N
    zi = np.zeros_like(wi)
    wi2 = np.stack([np.concatenate([wi, zi], axis=1), np.concatenate([zi, wi], axis=1)])
    as_b = lambda a: jnp.asarray(a, F32).astype(BF16)
    as_f = lambda a: jnp.asarray(a, F32)
    return [as_b(w1p), as_f(t1), as_f(t2), as_b(f2), as_b(g2), as_f(c2), as_f(s2), as_b(wi2)]


def _fwd_spectrum(xp, w1p, t1, t2, f2):
    cp = xp.shape[0]
    hn = FFT_N1
    a = jnp.dot(xp.reshape(cp * FFT_N2, LANE).astype(BF16), w1p, preferred_element_type=F32)
    out = []
    for par in range(2):
        ap = a[:, par * LANE:(par + 1) * LANE]
        ap = ap.reshape(cp, FFT_N2, LANE) * t1 + pltpu.roll(ap, hn, 1).reshape(cp, FFT_N2, LANE) * t2
        at = jnp.swapaxes(ap, 1, 2)
        p = jnp.dot(at.reshape(cp * 2 * hn, FFT_N2).astype(BF16), f2, preferred_element_type=F32)
        p = p.reshape(cp, 2 * hn, 2 * FFT_N2)
        out.append((p[:, :hn, :FFT_N2] - p[:, hn:, FFT_N2:], p[:, :hn, FFT_N2:] + p[:, hn:, :FFT_N2]))
    return out


def _inv_time(yre, yim, g2, c2, s2, wi_par):
    cp = yre.shape[0]
    hn = FFT_N1
    y = jnp.concatenate([yre, yim], axis=1).reshape(cp * 2 * hn, FFT_N2).astype(BF16)
    q = jnp.dot(y, g2, preferred_element_type=F32).reshape(cp, 2 * hn, 2 * FFT_N2)
    bre = q[:, :hn, :FFT_N2] - q[:, hn:, FFT_N2:]
    bim = q[:, :hn, FFT_N2:] + q[:, hn:, :FFT_N2]
    b2 = jnp.concatenate([bre * c2 - bim * s2, bre * s2 + bim * c2], axis=1)
    bt = jnp.swapaxes(b2, 1, 2)
    return jnp.dot(bt.reshape(cp * FFT_N2, LANE).astype(BF16), wi_par, preferred_element_type=F32)


def _hyena_kernel(z_ref, g1_ref, g2_ref, kf_ref, bias_ref, w1p_ref, t1_ref, t2_ref, f2_ref, gi_ref, c2_ref,
                  s2_ref, wi_ref, o_ref, zt_ref, g1t_ref, g2t_ref, ot_ref, stage_ref):
    k = pl.program_id(1)
    n_s = z_ref.shape[0] // FFT_N2
    cp = HY_PAIRS_PER_STEP

    def to_tiles(x_ref, t_ref):
        for s in range(n_s):
            xs = x_ref[s * FFT_N2:(s + 1) * FFT_N2, :].astype(F32)
            stage_ref[pl.ds(s, LANE, stride=HY_TILE_PITCH), :] = xs.T
        st = stage_ref[...].reshape(HY_PAIRS_PER_COL, 2 * HY_TILE_PITCH, FFT_N2)
        both = jnp.concatenate([st[:, :n_s], st[:, HY_TILE_PITCH:HY_TILE_PITCH + n_s]], axis=1)
        t_ref[...] = jnp.swapaxes(both, 1, 2)

    @pl.when(k == 0)
    def _():
        to_tiles(z_ref, zt_ref)
        to_tiles(g1_ref, g1t_ref)
        to_tiles(g2_ref, g2t_ref)

    sl = pl.ds(pl.multiple_of(k * cp, cp), cp)
    z = zt_ref[sl]
    gates = (g1t_ref, g2t_ref)
    for o in range(HY_ORDER):
        spec = _fwd_spectrum(z, w1p_ref[...], t1_ref[...], t2_ref[...], f2_ref[...])
        conv = None
        for par in range(2):
            xre, xim = spec[par]
            kre = kf_ref[o, par, :, :FFT_N1, :]
            kim = kf_ref[o, par, :, FFT_N1:, :]
            part = _inv_time(xre * kre - xim * kim, xre * kim + xim * kre,
                             gi_ref[...], c2_ref[...], s2_ref[...], wi_ref[par])
            conv = part if conv is None else conv + part
        z = gates[o][sl] * (conv.reshape(cp, FFT_N2, LANE) + bias_ref[o] * z)
    ot_ref[sl] = z

    @pl.when(k == pl.num_programs(1) - 1)
    def _():
        back = jnp.swapaxes(ot_ref[...], 1, 2)
        for c in range(LANE):
            stage_ref[c * HY_TILE_PITCH:c * HY_TILE_PITCH + n_s, :] = back[c // 2, (c % 2) * n_s:(c % 2 + 1) * n_s, :]
        for s in range(n_s):
            rows = stage_ref[pl.ds(s, LANE, stride=HY_TILE_PITCH), :]
            o_ref[s * FFT_N2:(s + 1) * FFT_N2, :] = rows.T.astype(o_ref.dtype)


def _hyena_conv(us, kf, biasp):
    t, _ = us.shape
    consts = _dft_constants()
    ncol = D_HY // LANE
    nsub = HY_PAIRS_PER_COL // HY_PAIRS_PER_STEP
    full = lambda a: pl.BlockSpec(a.shape, lambda j, k: (0,) * a.ndim)
    nat = lambda off: pl.BlockSpec((t, LANE), lambda j, k: (0, off + j))
    tiles = pltpu.VMEM((HY_PAIRS_PER_COL, FFT_N2, LANE), F32)
    return pl.pallas_call(
        _hyena_kernel,
        grid=(ncol, nsub),
        in_specs=[nat(0), nat(ncol), nat(2 * ncol),
                  pl.BlockSpec((HY_ORDER, 2, HY_PAIRS_PER_STEP, 2 * FFT_N1, FFT_N2),
                               lambda j, k: (0, 0, j * nsub + k, 0, 0)),
                  pl.BlockSpec((HY_ORDER, HY_PAIRS_PER_STEP, 1, LANE), lambda j, k: (0, j * nsub + k, 0, 0))]
                 + [full(a) for a in consts],
        out_specs=pl.BlockSpec((t, LANE), lambda j, k: (0, j)),
        out_shape=jax.ShapeDtypeStruct((t, D_HY), us.dtype),
        scratch_shapes=[tiles, tiles, tiles, tiles, pltpu.VMEM((LANE * HY_TILE_PITCH, FFT_N2), F32)],
        compiler_params=pltpu.CompilerParams(
            dimension_semantics=("parallel", "arbitrary"), vmem_limit_bytes=VMEM_LIMIT_BYTES),
        name="hyena_conv",
    )(us, us, us, kf, biasp, *consts)


def _filter_time_kernel(w1t_ref, w1c_ref, w1s_ref, b1_ref, w2_ref, b2_ref, w3_ref, b3_ref, fr_ref,
                        wf_ref, wb_ref, df_ref, db_ref, o_ref, h_ref, k_ref, *, length):
    n_fft = 2 * length
    hp = lax.Precision.HIGHEST

    @pl.when(pl.program_id(0) == 0)
    def _():
        pos = lax.broadcasted_iota(jnp.int32, (1, n_fft), 1)
        lag = jnp.where(pos < length, pos, n_fft - pos).astype(F32)
        t = lag / float(length - 1)
        w = (2.0 * math.pi / length) * lag
        band_step = (HY_BANDS - 1 - 1e-4) / (HY_BANDS - 1)
        bands = 1e-4 + band_step * lax.broadcasted_iota(jnp.int32, (HY_BANDS, 1), 0).astype(F32)
        ang = bands * w
        fr = fr_ref[...]
        h = (w1t_ref[...] * t + jnp.dot(w1c_ref[...], jnp.cos(ang), preferred_element_type=F32, precision=hp)
             - jnp.dot(w1s_ref[...], jnp.sin(ang), preferred_element_type=F32, precision=hp))
        h = jnp.sin(fr * (h + b1_ref[...]))
        h = jnp.sin(fr * (jnp.dot(w2_ref[...], h, preferred_element_type=F32, precision=hp) + b2_ref[...]))
        h = jnp.sin(fr * (jnp.dot(w3_ref[...], h, preferred_element_type=F32, precision=hp) + b3_ref[...]))
        hi = h.astype(BF16)
        h_ref[0] = hi
        h_ref[1] = (h - hi.astype(F32)).astype(BF16)

    def dot3(w, lo, hi_):
        w_hi = w.astype(BF16)
        w_lo = (w - w_hi.astype(F32)).astype(BF16)
        h_hi, h_lo = h_ref[0, :, lo:hi_], h_ref[1, :, lo:hi_]
        return (jnp.dot(w_hi, h_hi, preferred_element_type=F32) + jnp.dot(w_hi, h_lo, preferred_element_type=F32)
                + jnp.dot(w_lo, h_hi, preferred_element_type=F32))

    pos = lax.broadcasted_iota(jnp.int32, (1, length), 1)
    tf = pos.astype(F32) / float(length - 1)
    tb = (length - pos).astype(F32) / float(length - 1)
    kf = dot3(wf_ref[...], 0, length) * jnp.exp(-tf * df_ref[...])
    kb = dot3(wb_ref[...], length, n_fft) * jnp.exp(-tb * db_ref[...])
    kb = jnp.where(pos == 0, 0.0, kb)
    inv = 1.0 / (jnp.sum(jnp.abs(kf), axis=1, keepdims=True) + jnp.sum(jnp.abs(kb), axis=1, keepdims=True))
    k_ref[:, :length] = kf * inv
    k_ref[:, length:] = kb * inv
    cb = wf_ref.shape[0]
    for n1 in range(FILTER_TILE_ROWS):
        row = k_ref[:, n1 * FFT_N2:(n1 + 1) * FFT_N2] if n1 < FFT_N1 else jnp.zeros((cb, FFT_N2), F32)
        o_ref[pl.ds(n1, cb, stride=FILTER_TILE_ROWS), :] = row


def _filter_time(length, w1, b1, w2, b2, w3, b3, freq, w_out, cb=128):
    col = lambda v: v.reshape(-1, 1)
    w1t = w1.T
    n_ch = w_out.shape[1]
    deltas = jnp.abs(jnp.linspace(math.log(HY_TARGET) / HY_SLOW_PCT, math.log(HY_TARGET) / HY_FAST_PCT,
                                  n_ch, dtype=F32)).reshape(n_ch, 1)
    wot = w_out.T
    nb = D_HY // cb
    small = lambda a: pl.BlockSpec(a.shape, lambda i: (0,) * a.ndim)
    fwd = lambda i: ((i // nb) * HY_DIRS * nb + i % nb, 0)
    bwd = lambda i: ((i // nb) * HY_DIRS * nb + nb + i % nb, 0)
    ins = [w1t[:, 0:1], w1t[:, 1:1 + HY_BANDS], w1t[:, 1 + HY_BANDS:], col(b1), w2.T, col(b2), w3.T, col(b3),
           col(freq)]
    hy_ff = w2.shape[0]
    return pl.pallas_call(
        functools.partial(_filter_time_kernel, length=length),
        grid=(HY_ORDER * nb,),
        in_specs=[small(a) for a in ins] + [pl.BlockSpec((cb, hy_ff), fwd), pl.BlockSpec((cb, hy_ff), bwd),
                                            pl.BlockSpec((cb, 1), fwd), pl.BlockSpec((cb, 1), bwd)],
        out_specs=pl.BlockSpec((cb * FILTER_TILE_ROWS, FFT_N2), lambda i: (i, 0)),
        out_shape=jax.ShapeDtypeStruct((HY_ORDER * D_HY * FILTER_TILE_ROWS, FFT_N2), F32),
        scratch_shapes=[pltpu.VMEM((2, hy_ff, 2 * length), BF16), pltpu.VMEM((cb, 2 * length), F32)],
        compiler_params=pltpu.CompilerParams(
            dimension_semantics=("arbitrary",), vmem_limit_bytes=VMEM_LIMIT_BYTES),
        name="hyena_filter_time",
    )(*ins, wot, wot, deltas, deltas)


def _filter_spec_kernel(k_ref, w1p_ref, t1_ref, t2_ref, f2_ref, o_ref):
    cb = k_ref.shape[0] // FILTER_TILE_ROWS
    kt = k_ref[...].reshape(cb, FILTER_TILE_ROWS, FFT_N2)[:, :FFT_N1, :]
    xp = jnp.swapaxes(kt.reshape(cb // 2, 2 * FFT_N1, FFT_N2), 1, 2)
    spec = _fwd_spectrum(xp, w1p_ref[...], t1_ref[...], t2_ref[...], f2_ref[...])
    for par in range(2):
        o_ref[par, :, :FFT_N1, :] = spec[par][0]
        o_ref[par, :, FFT_N1:, :] = spec[par][1]


def _filter_spectrum(kt, cb=128):
    consts = _dft_constants(real_input=True)[:4]
    nb = D_HY // cb
    full = lambda a: pl.BlockSpec(a.shape, lambda i: (0,) * a.ndim)
    return pl.pallas_call(
        _filter_spec_kernel,
        grid=(HY_ORDER * nb,),
        in_specs=[pl.BlockSpec((cb * FILTER_TILE_ROWS, FFT_N2), lambda i: (i, 0))] + [full(a) for a in consts],
        out_specs=pl.BlockSpec((None, 2, cb // 2, 2 * FFT_N1, FFT_N2), lambda i: (i // nb, 0, i % nb, 0, 0)),
        out_shape=jax.ShapeDtypeStruct((HY_ORDER, 2, D_HY // 2, 2 * FFT_N1, FFT_N2), F32),
        compiler_params=pltpu.CompilerParams(
            dimension_semantics=("parallel",), vmem_limit_bytes=VMEM_LIMIT_BYTES),
        name="hyena_filter_spectrum",
    )(kt, *consts)


def _hyena(us, w1, b1, w2, b2, w3, b3, freq, w_out, bias):
    bsz, length, _ = us.shape
    assert 2 * length == FFT_N and bsz == 2, "one complex transform carries exactly two batch rows"
    kf = _filter_spectrum(_filter_time(length, w1, b1, w2, b2, w3, b3, freq, w_out))
    biasp = jnp.repeat(bias.reshape(HY_ORDER, D_HY // 2, 1, 2), FFT_N1, axis=-1)
    return _hyena_conv(us.reshape(bsz * length, -1), kf, biasp).reshape(bsz, length, D_HY)


def kernel(x, c, ctx, c_ctx, w_ada, b_ada, norm_g, ffn_w_gate, ffn_w_up, ffn_w_down, w_in,
           s5_lam_re, s5_lam_im, s5_log_dt, s5_b_re, s5_b_im, s5_c_re, s5_c_im, s5_d,
           hy_short_w, hy_short_b, hy_w1, hy_b1, hy_w2, hy_b2, hy_w3, hy_b3, hy_freq, hy_w_out,
           hy_bias, w_pa, w_pb, w_out, final_g):
    bsz, seq, d = x.shape
    ctx_len = ctx.shape[1]
    n_rows = seq // GRID_W
    depth = w_ada.shape[0]
    assert depth == 1, "context-token outputs are only dropped by the last layer"
    l = 0

    c_rows = jnp.concatenate([c, c_ctx[None, :], jnp.zeros((8 - bsz - 1, d), F32)], axis=0)
    mod_all = _ada_mod(c_rows, w_ada[l], b_ada[l])
    mod = mod_all[:bsz].reshape(bsz, N_SUB, N_MOD, 1, d)
    mod_c = mod_all[bsz:bsz + 1].reshape(1, N_SUB, N_MOD, 1, d)

    def mods(m, sub):
        return tuple(m[:, sub, k] for k in range(N_MOD))

    wg, wu, wd = ffn_w_gate[l], ffn_w_up[l], ffn_w_down[l]
    w_in_b = w_in[l]

    xt = x.reshape(bsz * seq, d)
    ct = ctx.reshape(bsz * ctx_len, d)

    xt = _ffn_sublayer(xt, mods(mod, 0), norm_g[l, 0], wg, wu, wd, 0)
    ct = _ffn_sublayer(ct, mods(mod_c, 0), norm_g[l, 0], wg, wu, wd, 0)

    assert GRID_W * n_rows == seq
    u_s5, us_hy, sig_gates = _in_proj(xt, mod[:, 1, 0], mod[:, 1, 1], norm_g[l, 1], w_in_b,
                                      hy_short_w[l], hy_short_b[l], n_u=I_HY, n_hy=I_GA - I_HY)
    (u_ctx,) = _in_proj(ct, mod_c[:, 1, 0], mod_c[:, 1, 1], norm_g[l, 1], w_in_b[:, :D_S5])

    y_s5 = _s5_bidirectional(u_s5.reshape(bsz, seq, D_S5), u_ctx.reshape(bsz, ctx_len, D_S5),
                             s5_lam_re[l], s5_lam_im[l], s5_log_dt[l],
                             s5_b_re[l], s5_b_im[l], s5_c_re[l], s5_c_im[l], s5_d[l])
    y_hy = _hyena(us_hy.reshape(bsz, seq, I_GA - I_HY),
                  hy_w1[l], hy_b1[l], hy_w2[l], hy_b2[l], hy_w3[l], hy_b3[l], hy_freq[l],
                  hy_w_out[l], hy_bias[l])

    xt = _merge(xt, mod[:, 1, 2], y_s5.reshape(bsz * seq, D_S5), y_hy.reshape(bsz * seq, D_HY),
                sig_gates, w_pa[l].astype(BF16), w_pb[l].astype(BF16), w_out[l].astype(BF16))

    xt = _ffn_sublayer(xt, mods(mod, 2), norm_g[l, 2], wg, wu, wd, 1, final_gain=final_g)
    return xt.reshape(bsz, seq, d)
```

```python
import functools
import math

import jax
import jax.numpy as jnp
import numpy as np
from jax import lax
from jax.experimental import pallas as pl
from jax.experimental.pallas import tpu as pltpu

F32 = jnp.float32
BF16 = jnp.bfloat16

D_MODEL = 2048
GRID_W = 64
D_S5 = 1024
S5_GROUP = 16
S5_GROUPS = D_S5 // S5_GROUP
S5_STATE = 64
S5_DIRS = 2
LAMBDA_RE_MAX = -1e-4
S5_CHUNK = 16
LANE = 128
S5_GROUPS_PER_COL = LANE // S5_GROUP
D_HY = 1024
HY_ORDER = 2
HY_DIRS = 2
HY_SHORT = 3
HY_EMB = 33
HY_BANDS = (HY_EMB - 1) // 2
HY_TARGET = 1e-2
HY_FAST_PCT = 0.3
HY_SLOW_PCT = 1.5
FFT_N1 = 64
FFT_N2 = 128
FFT_N = FFT_N1 * FFT_N2
FILTER_TILE_ROWS = FFT_N1 + 8
HY_TILE_PITCH = FILTER_TILE_ROWS
HY_PAIRS_PER_COL = LANE // 2
HY_PAIRS_PER_STEP = 16
I_HY = D_S5
I_GA = D_S5 + (HY_ORDER + 1) * D_HY
I_GB = I_GA + D_MODEL
D_IN = I_GB + D_MODEL
D_FF = 5632
N_SUB = 3
N_MOD = 3
HALF_STEP = 0.5
RMS_EPS = 1e-6

VMEM_LIMIT_BYTES = 58 * 1024 * 1024


def _rms_mod(x, gain, shift, scale):
    ms = jnp.mean(x * x, axis=-1, keepdims=True)
    y = x * lax.rsqrt(ms + RMS_EPS) * gain
    return y * (1.0 + scale) + shift


def _split_bf16(v):
    hi = v.astype(BF16)
    return hi, (v - hi.astype(F32)).astype(BF16)


def _ada_kernel(c_ref, w_ref, b_ref, o_ref):
    c = c_ref[...]
    a_hi, a_lo = _split_bf16(c * jax.nn.sigmoid(c))
    w_hi, w_lo = _split_bf16(w_ref[...])
    o_ref[...] = (jnp.dot(a_hi, w_hi, preferred_element_type=F32) + jnp.dot(a_lo, w_hi, preferred_element_type=F32)
                  + jnp.dot(a_hi, w_lo, preferred_element_type=F32) + b_ref[...])


def _ada_mod(c_rows, w, b, tn=1024):
    rows, d = c_rows.shape
    n = w.shape[1]
    return pl.pallas_call(
        _ada_kernel,
        grid=(n // tn,),
        in_specs=[pl.BlockSpec((rows, d), lambda j: (0, 0)),
                  pl.BlockSpec((d, tn), lambda j: (0, j)),
                  pl.BlockSpec((1, tn), lambda j: (0, j))],
        out_specs=pl.BlockSpec((rows, tn), lambda j: (0, j)),
        out_shape=jax.ShapeDtypeStruct((rows, n), F32),
        compiler_params=pltpu.CompilerParams(
            dimension_semantics=("arbitrary",), vmem_limit_bytes=VMEM_LIMIT_BYTES),
        name="ada_mod",
    )(c_rows, w, b.reshape(1, n))


def _ffn_kernel(x_ref, shift_ref, scale_ref, gate_ref, gain_ref, wg_ref, wu_ref, wd_ref,
                fg_ref, o_ref, h_ref, *, final_norm):
    j = pl.program_id(1)

    @pl.when(j == 0)
    def _():
        h_ref[...] = _rms_mod(x_ref[...], gain_ref[...], shift_ref[...], scale_ref[...]).astype(BF16)
        o_ref[...] = jnp.zeros_like(o_ref)

    h = h_ref[...]
    g = jnp.dot(h, wg_ref[...].astype(BF16), preferred_element_type=F32)
    u = jnp.dot(h, wu_ref[...].astype(BF16), preferred_element_type=F32)
    a = (g * jax.nn.sigmoid(g) * u).astype(BF16)
    o_ref[...] += jnp.dot(a, wd_ref[...].astype(BF16), preferred_element_type=F32)

    @pl.when(j == pl.num_programs(1) - 1)
    def _():
        y = x_ref[...] + (HALF_STEP * gate_ref[...]) * o_ref[...]
        if final_norm:
            ms = jnp.mean(y * y, axis=-1, keepdims=True)
            y = y * lax.rsqrt(ms + RMS_EPS) * fg_ref[...]
        o_ref[...] = y


def _ffn_sublayer(x, mods, gain, wg, wu, wd, which, final_gain=None, tm=1024, tf=256):
    t, d = x.shape
    bm = mods[0].shape[0]
    tm = min(tm, t // bm)
    blocks_per_batch = (t // bm) // tm
    dff = wg.shape[2]
    final_norm = final_gain is not None
    fg = final_gain if final_norm else gain
    mod_spec = pl.BlockSpec((None, 1, d), lambda i, j: (i // blocks_per_batch, 0, 0))
    vec_spec = pl.BlockSpec((1, d), lambda i, j: (0, 0))
    return pl.pallas_call(
        functools.partial(_ffn_kernel, final_norm=final_norm),
        grid=(t // tm, dff // tf),
        in_specs=[pl.BlockSpec((tm, d), lambda i, j: (i, 0)),
                  mod_spec, mod_spec, mod_spec, vec_spec,
                  pl.BlockSpec((None, d, tf), lambda i, j: (which, 0, j)),
                  pl.BlockSpec((None, d, tf), lambda i, j: (which, 0, j)),
                  pl.BlockSpec((None, tf, d), lambda i, j: (which, j, 0)),
                  vec_spec],
        out_specs=pl.BlockSpec((tm, d), lambda i, j: (i, 0)),
        out_shape=jax.ShapeDtypeStruct((t, d), F32),
        scratch_shapes=[pltpu.VMEM((tm, d), BF16)],
        compiler_params=pltpu.CompilerParams(
            dimension_semantics=("parallel", "arbitrary"), vmem_limit_bytes=VMEM_LIMIT_BYTES),
        name="ffn_final" if final_norm else "ffn",
    )(x, *mods, gain.reshape(1, d), wg, wu, wd, fg.reshape(1, d))


def _proj_kernel(x_ref, shift_ref, scale_ref, gain_ref, w_ref, sw_ref, sb_ref, *rest, n_u, n_hy, row_len, part):
    o_refs, h_ref = rest[:-1], rest[-1]
    j = pl.program_id(1)

    @pl.when(j == 0)
    def _():
        h_ref[...] = _rms_mod(x_ref[...], gain_ref[...], shift_ref[...], scale_ref[...]).astype(BF16)

    tm, tn = h_ref.shape[0], w_ref.shape[1]

    def in_parts(o_ref, epilogue):
        for c in range(0, tn, part):
            p = jnp.dot(h_ref[...], w_ref[:, c:c + part].astype(BF16), preferred_element_type=F32)
            o_ref[:, c:c + part] = epilogue(p, c).astype(o_ref.dtype)

    def short_conv(p, c):
        col = lax.broadcasted_iota(jnp.int32, p.shape, 0) % row_len
        prev = jnp.where(col == 0, 0.0, pltpu.roll(p, 1, 0))
        nxt = jnp.where(col == row_len - 1, 0.0, pltpu.roll(p, tm - 1, 0))
        sw = sw_ref[:, c:c + part]
        return sb_ref[:, c:c + part] + prev * sw[0:1] + p * sw[1:2] + nxt * sw[2:3]

    if n_hy == 0:
        in_parts(o_refs[0], lambda p, c: p)
        return

    @pl.when(j < n_u)
    def _():
        in_parts(o_refs[0], lambda p, c: p)

    @pl.when((j >= n_u) & (j < n_u + n_hy))
    def _():
        in_parts(o_refs[1], short_conv)

    @pl.when(j >= n_u + n_hy)
    def _():
        in_parts(o_refs[2], lambda p, c: jax.nn.sigmoid(p))


def _in_proj(x, shift, scale, gain, w, short_w=None, short_b=None, n_u=D_S5, n_hy=0, row_len=GRID_W,
             tm=1024, tn=512, part=256):
    t, d = x.shape
    bm = shift.shape[0]
    tm = min(tm, t // bm)
    blocks_per_batch = (t // bm) // tm
    n = w.shape[1]
    assert tm % row_len == 0 and (t // bm) % tm == 0
    bu, bh = n_u // tn, n_hy // tn
    bg = n // tn - bu - bh
    mod_spec = pl.BlockSpec((None, 1, d), lambda i, j: (i // blocks_per_batch, 0, 0))
    out_shape = [jax.ShapeDtypeStruct((t, n_u), F32)]
    out_specs = [pl.BlockSpec((tm, tn), lambda i, j: (i, jnp.minimum(j, bu - 1)))]
    if bh:
        out_shape += [jax.ShapeDtypeStruct((t, n_hy), BF16), jax.ShapeDtypeStruct((t, bg * tn), BF16)]
        out_specs += [pl.BlockSpec((tm, tn), lambda i, j: (i, jnp.clip(j - bu, 0, bh - 1))),
                      pl.BlockSpec((tm, tn), lambda i, j: (i, jnp.maximum(j - bu - bh, 0)))]
        sw, sb = short_w, short_b.reshape(1, n_hy)
        hy_blk = lambda i, j: (0, jnp.clip(j - bu, 0, bh - 1))
    else:
        sw, sb = jnp.zeros((HY_SHORT, tn), F32), jnp.zeros((1, tn), F32)
        hy_blk = lambda i, j: (0, 0)
    return pl.pallas_call(
        functools.partial(_proj_kernel, n_u=bu, n_hy=bh, row_len=row_len, part=part),
        grid=(t // tm, n // tn),
        in_specs=[pl.BlockSpec((tm, d), lambda i, j: (i, 0)),
                  mod_spec, mod_spec,
                  pl.BlockSpec((1, d), lambda i, j: (0, 0)),
                  pl.BlockSpec((d, tn), lambda i, j: (0, j)),
                  pl.BlockSpec((HY_SHORT, tn), hy_blk),
                  pl.BlockSpec((1, tn), hy_blk)],
        out_specs=out_specs,
        out_shape=out_shape,
        scratch_shapes=[pltpu.VMEM((tm, d), BF16)],
        compiler_params=pltpu.CompilerParams(
            dimension_semantics=("parallel", "arbitrary"), vmem_limit_bytes=VMEM_LIMIT_BYTES),
        name="in_proj",
    )(x, shift, scale, gain.reshape(1, d), w, sw, sb)


def _gelu_tanh(x):
    return 0.5 * x * (1.0 + jnp.tanh(math.sqrt(2.0 / math.pi) * (x + 0.044715 * (x * x * x))))


def _merge_kernel(x_ref, gate_ref, ys_ref, yh_ref, ga_ref, gb_ref, wpa_lo_ref, wpa_hi_ref, wpb_ref,
                  wout_ref, o_ref, s_ref, acc_ref):
    j = pl.program_id(1)

    @pl.when(j == 0)
    def _():
        s_ref[...] = _gelu_tanh(ys_ref[...].astype(F32)).astype(BF16)
        acc_ref[...] = jnp.zeros_like(acc_ref)

    s = s_ref[...]
    pa_lo = jnp.dot(s, wpa_lo_ref[...], preferred_element_type=F32)
    pa_hi = jnp.dot(s, wpa_hi_ref[...], preferred_element_type=F32)
    y_a = pa_lo * jax.nn.sigmoid(pa_hi)
    y_b = jnp.dot(yh_ref[...], wpb_ref[...], preferred_element_type=F32)
    m = ga_ref[...].astype(F32) * y_a + gb_ref[...].astype(F32) * y_b
    acc_ref[...] += jnp.dot(m.astype(BF16), wout_ref[...], preferred_element_type=F32)

    @pl.when(j == pl.num_programs(1) - 1)
    def _():
        o_ref[...] = x_ref[...] + gate_ref[...] * acc_ref[...]


def _merge(x, gate, y_s5, y_hy, sig_gates, w_pa, w_pb, w_out, tm=512, tn=1024):
    t, d = x.shape
    bm = gate.shape[0]
    blocks_per_batch = (t // bm) // tm
    nj = d // tn
    ds5 = y_s5.shape[1]
    dhy = y_hy.shape[1]
    return pl.pallas_call(
        _merge_kernel,
        grid=(t // tm, nj),
        in_specs=[pl.BlockSpec((tm, d), lambda i, j: (i, 0)),
                  pl.BlockSpec((None, 1, d), lambda i, j: (i // blocks_per_batch, 0, 0)),
                  pl.BlockSpec((tm, ds5), lambda i, j: (i, 0)),
                  pl.BlockSpec((tm, dhy), lambda i, j: (i, 0)),
                  pl.BlockSpec((tm, tn), lambda i, j: (i, j)),
                  pl.BlockSpec((tm, tn), lambda i, j: (i, nj + j)),
                  pl.BlockSpec((ds5, tn), lambda i, j: (0, j)),
                  pl.BlockSpec((ds5, tn), lambda i, j: (0, nj + j)),
                  pl.BlockSpec((dhy, tn), lambda i, j: (0, j)),
                  pl.BlockSpec((tn, d), lambda i, j: (j, 0))],
        out_specs=pl.BlockSpec((tm, d), lambda i, j: (i, 0)),
        out_shape=jax.ShapeDtypeStruct((t, d), F32),
        scratch_shapes=[pltpu.VMEM((tm, ds5), BF16), pltpu.VMEM((tm, d), F32)],
        compiler_params=pltpu.CompilerParams(
            dimension_semantics=("parallel", "arbitrary"), vmem_limit_bytes=VMEM_LIMIT_BYTES),
        name="merge",
    )(x, gate, y_s5, y_hy, sig_gates, sig_gates, w_pa, w_pa, w_pb, w_out)


def _s5_weights_kernel(*refs, n_steps):
    blocks, ca_ref = refs[:-1], refs[-1]

    def one(gi, carry):
        _s5_weights_group(*[r.at[gi] for r in blocks], ca_ref, n_steps=n_steps)
        return carry

    lax.fori_loop(0, blocks[0].shape[0], one, 0)


def _s5_weights_group(par_ref, bt_ref, c_ref, d_ref, bpow_ref, toep_ref, cpow_ref, tab_ref, ca_ref, *, n_steps):
    t, h, p = S5_CHUNK, S5_GROUP, S5_STATE
    lanes = 2 * p
    hp = lax.Precision.HIGHEST
    sgn = jnp.where(lax.broadcasted_iota(jnp.int32, (1, lanes), 1) < p, -1.0, 1.0)
    par = par_ref[...]
    gsum = None
    for d in range(S5_DIRS):
        lr = jnp.minimum(par[3 * d:3 * d + 1], LAMBDA_RE_MAX)
        li = par[3 * d + 1:3 * d + 2]
        dt = jnp.exp(par[3 * d + 2:3 * d + 3])
        zr, zi = lr * dt, li * dt

        def apow(j):
            mag = jnp.exp(j * zr)
            return mag * jnp.cos(j * zi), sgn * (mag * jnp.sin(j * zi))

        def cmul(x, a1, a2):
            return x * a1 + pltpu.roll(x, p, 1) * a2

        a1, a2 = apow(lax.broadcasted_iota(jnp.int32, (t + 1, 1), 0).astype(F32))
        nr, ni = a1[1:2] - 1.0, sgn * a2[1:2]
        den = lr * lr + li * li
        f_re = (nr * lr + ni * li) / den
        f_im = (ni * lr - nr * li) / den
        bbar = cmul(bt_ref[d], f_re, sgn * f_im)
        cc = c_ref[d]
        ca = [cmul(cc, a1[j:j + 1], a2[j:j + 1]) * (-sgn) for j in range(t + 1)]
        ca_ref[...] = jnp.zeros_like(ca_ref)
        for k in range(t):
            e_b, e_c = (t - 1 - k, k + 1) if d == 0 else (k, t - k)
            bpow_ref[k * h:(k + 1) * h, d * lanes:(d + 1) * lanes] = (
                cmul(bbar, a1[e_b:e_b + 1], a2[e_b:e_b + 1]).astype(BF16))
            cpow_ref[k * h:(k + 1) * h, d * lanes:(d + 1) * lanes] = ca[e_c].astype(BF16)
            l = t - 1 + k if d == 0 else t - 1 - k
            ca_ref[l * h:(l + 1) * h, :] = ca[k]
        g = lax.dot_general(bbar, ca_ref[...], (((1,), (1,)), ((), ())), preferred_element_type=F32, precision=hp)
        gsum = g if gsum is None else gsum + g
        for s in range(n_steps):
            s1, s2 = apow(float(t * 2 ** s))
            r = d * 2 * n_steps + 2 * s
            tab_ref[r:r + 1, :] = s1
            tab_ref[r + 1:r + 2, :] = s2
    wide = gsum.shape[1]
    col = lax.broadcasted_iota(jnp.int32, (h, wide), 1)
    row = lax.broadcasted_iota(jnp.int32, (h, wide), 0)
    gsum = gsum + jnp.where(col - (t - 1) * h == row, d_ref[...], 0.0)
    for k in range(t):
        off = (t - 1 - k) * h
        shifted = gsum if off == 0 else pltpu.roll(gsum, wide - off, 1)
        toep_ref[k * h:(k + 1) * h, :] = shifted[:, :t * h].astype(BF16)


def _s5_weights(lam_re, lam_im, log_dt, b_re, b_im, c_re, c_im, d_skip, n_steps):
    g, p, h, t = S5_GROUPS, S5_STATE, S5_GROUP, S5_CHUNK
    cat2 = lambda a: jnp.concatenate([a, a], axis=-1)
    par = jnp.stack([cat2(lam_re), cat2(lam_im), jnp.broadcast_to(log_dt[..., None], (S5_DIRS, g, 2 * p))], axis=1)
    par = par.transpose(2, 0, 1, 3).reshape(g, 3 * S5_DIRS, 2 * p)
    btc = jnp.concatenate([b_re, b_im], axis=2).transpose(1, 0, 3, 2)
    ccat = jnp.concatenate([c_re, c_im], axis=3).transpose(1, 0, 2, 3)
    wide = 2 * t * h
    drow = jnp.zeros((g, 1, wide), F32).at[:, 0, (t - 1) * h:t * h].set(d_skip.reshape(g, h))
    w = t * h
    gb = S5_GROUPS_PER_COL
    sq = pl.BlockSpec((gb, w, w), lambda i: (i, 0, 0))
    return pl.pallas_call(
        functools.partial(_s5_weights_kernel, n_steps=n_steps),
        grid=(g // gb,),
        in_specs=[pl.BlockSpec((gb, 3 * S5_DIRS, 2 * p), lambda i: (i, 0, 0)),
                  pl.BlockSpec((gb, S5_DIRS, h, 2 * p), lambda i: (i, 0, 0, 0)),
                  pl.BlockSpec((gb, S5_DIRS, h, 2 * p), lambda i: (i, 0, 0, 0)),
                  pl.BlockSpec((gb, 1, wide), lambda i: (i, 0, 0))],
        out_specs=[sq, sq, sq, pl.BlockSpec((gb, 4 * n_steps, 2 * p), lambda i: (i, 0, 0))],
        out_shape=[jax.ShapeDtypeStruct((g, w, w), BF16)] * 3 + [jax.ShapeDtypeStruct((g, 4 * n_steps, 2 * p), F32)],
        scratch_shapes=[pltpu.VMEM((wide, 2 * p), F32)],
        compiler_params=pltpu.CompilerParams(
            dimension_semantics=("parallel",), vmem_limit_bytes=VMEM_LIMIT_BYTES),
        name="s5_weights",
    )(par, btc, ccat, drow)


def _s5_kernel(u_ref, uc_ref, bpow_ref, toep_ref, cpow_ref, tab_ref, y_ref, ucat_ref, ucc_ref, yacc_ref,
               *, bsz, n_steps):
    t, gw = S5_CHUNK, S5_GROUP
    per = LANE // gw
    n_lat = u_ref.shape[0] // (bsz * t)
    n_ctx = uc_ref.shape[0] // (bsz * t)
    n_ch = n_lat + n_ctx
    rows = bsz * n_ch
    half = 2 * S5_STATE
    for k in range(t):
        ucat_ref[:, k * LANE:(k + 1) * LANE] = u_ref[pl.ds(k, bsz * n_lat, stride=t), :]
        ucc_ref[:, k * LANE:(k + 1) * LANE] = uc_ref[pl.ds(k, bsz * n_ctx, stride=t), :]
    yacc_ref[...] = jnp.zeros_like(yacc_ref)
    rib = lax.broadcasted_iota(jnp.int32, (rows, half), 0) % n_ch
    lane = lax.broadcasted_iota(jnp.int32, (1, LANE), 1)

    def cmul_add(acc, sh, a1, a2):
        return acc + a1 * sh + a2 * pltpu.roll(sh, S5_STATE, 1)

    def gather(src_ref, gl):
        cols = []
        for j in range(t // per):
            acc = None
            for tt in range(per):
                k = j * per + tt
                r = pltpu.roll(src_ref[:, k * LANE:(k + 1) * LANE], (gw * tt - gw * gl) % LANE, 1)
                acc = r if acc is None else jnp.where((lane >= gw * tt) & (lane < gw * (tt + 1)), r, acc)
            cols.append(acc)
        return jnp.concatenate(cols, axis=1).astype(BF16)

    def group(gl, carry):
        ul = gather(ucat_ref, gl)
        uc = gather(ucc_ref, gl)
        bpow = bpow_ref[gl]
        zl = jnp.dot(ul, bpow, preferred_element_type=F32)
        zc = jnp.dot(uc, bpow, preferred_element_type=F32)
        fparts, bparts = [], []
        for b in range(bsz):
            lat = slice(b * n_lat, (b + 1) * n_lat)
            ctx = slice(b * n_ctx, (b + 1) * n_ctx)
            fparts += [zc[ctx, :half], zl[lat, :half]]
            bparts += [zl[lat, half:], zc[ctx, half:]]
        fw = jnp.concatenate(fparts, axis=0)
        bw = jnp.concatenate(bparts, axis=0)
        tab = tab_ref[gl]
        for s in range(n_steps):
            d = 1 << s
            sh = jnp.where(rib >= d, pltpu.roll(fw, d, 0), 0.0)
            fw = cmul_add(fw, sh, tab[2 * s:2 * s + 1], tab[2 * s + 1:2 * s + 2])
            o = 2 * n_steps
            sh = jnp.where(rib < n_ch - d, pltpu.roll(bw, rows - d, 0), 0.0)
            bw = cmul_add(bw, sh, tab[o + 2 * s:o + 2 * s + 1], tab[o + 2 * s + 1:o + 2 * s + 2])
        fe = jnp.where(rib >= 1, pltpu.roll(fw, 1, 0), 0.0)
        be = jnp.where(rib < n_ch - 1, pltpu.roll(bw, rows - 1, 0), 0.0)
        fl = jnp.concatenate([fe[b * n_ch + n_ctx:(b + 1) * n_ch] for b in range(bsz)], axis=0)
        bl = jnp.concatenate([be[b * n_ch:b * n_ch + n_lat] for b in range(bsz)], axis=0)
        st = jnp.concatenate([fl, bl], axis=1).astype(BF16)
        y = (jnp.dot(ul, toep_ref[gl], preferred_element_type=F32)
             + lax.dot_general(st, cpow_ref[gl], (((1,), (1,)), ((), ())), preferred_element_type=F32))
        mine = (lane >= gw * gl) & (lane < gw * (gl + 1))
        for i in range(t):
            src = y[:, (i // per) * LANE:(i // per + 1) * LANE]
            r = pltpu.roll(src, (gw * gl - gw * (i % per)) % LANE, 1)
            blk = slice(i * LANE, (i + 1) * LANE)
            yacc_ref[:, blk] = jnp.where(mine, r, yacc_ref[:, blk])
        return carry

    lax.fori_loop(0, S5_GROUPS_PER_COL, group, 0)
    for i in range(t):
        y_ref[pl.ds(i, bsz * n_lat, stride=t), :] = yacc_ref[:, i * LANE:(i + 1) * LANE]


def _s5_mix(u, uc, bpw, toep, cpw, tab, bsz):
    rl, dm = u.shape
    rc = uc.shape[0]
    t = S5_CHUNK
    w = t * S5_GROUP
    n_steps = tab.shape[1] // 4
    gpc = S5_GROUPS_PER_COL
    wspec = pl.BlockSpec((gpc, w, w), lambda i: (i, 0, 0))
    return pl.pallas_call(
        functools.partial(_s5_kernel, bsz=bsz, n_steps=n_steps),
        grid=(dm // LANE,),
        in_specs=[pl.BlockSpec((rl, LANE), lambda i: (0, i)),
                  pl.BlockSpec((rc, LANE), lambda i: (0, i)),
                  wspec, wspec, wspec,
                  pl.BlockSpec((gpc, 4 * n_steps, tab.shape[2]), lambda i: (i, 0, 0))],
        out_specs=pl.BlockSpec((rl, LANE), lambda i: (0, i)),
        out_shape=jax.ShapeDtypeStruct(u.shape, F32),
        scratch_shapes=[pltpu.VMEM((rl // t, t * LANE), F32), pltpu.VMEM((rc // t, t * LANE), F32),
                        pltpu.VMEM((rl // t, t * LANE), F32)],
        compiler_params=pltpu.CompilerParams(
            dimension_semantics=("parallel",), vmem_limit_bytes=VMEM_LIMIT_BYTES),
        name="s5_mix",
    )(u, uc, bpw, toep, cpw, tab)


def _s5_bidirectional(u, u_ctx, lam_re, lam_im, log_dt, b_re, b_im, c_re, c_im, d_skip):
    bsz, length, dm = u.shape
    ctx_len = u_ctx.shape[1]
    t = S5_CHUNK
    n_lat, n_ctx = length // t, ctx_len // t
    n_steps = max(1, math.ceil(math.log2(n_lat + n_ctx)))
    bpw, toep, cpw, tab = _s5_weights(lam_re, lam_im, log_dt, b_re, b_im, c_re, c_im, d_skip, n_steps)
    y = _s5_mix(u.reshape(bsz * length, dm), u_ctx.reshape(bsz * ctx_len, dm), bpw, toep, cpw, tab, bsz)
    return y.reshape(bsz, length, dm)


def _dft_constants(real_input=False):
    n1 = np.arange(FFT_N1)
    n2 = np.arange(FFT_N2)
    half = FFT_N1 // 2
    th = 2 * np.pi * np.outer(n1, n1) / FFT_N1
    c1, s1 = np.cos(th), np.sin(th)
    if real_input:
        w1 = np.concatenate([c1, -s1], axis=1)
    else:
        w1 = np.concatenate([np.concatenate([c1[:half], -s1[:half]], axis=1),
                             np.concatenate([s1[:half], c1[:half]], axis=1)], axis=0)
    z = np.zeros_like(w1)
    w1p = np.block([[w1, z], [z, w1]])
    ph = 2 * np.pi * np.outer(n2, n1) / FFT_N
    t1 = np.concatenate([np.cos(ph), np.cos(ph)], axis=1)
    t2 = np.concatenate([np.sin(ph), -np.sin(ph)], axis=1)
    ps = 2 * np.pi * np.outer(n2, n2) / FFT_N2
    f2 = np.concatenate([np.cos(ps), -np.sin(ps)], axis=1)
    g2 = np.concatenate([np.cos(ps), np.sin(ps)], axis=1)
    c2, s2 = np.cos(ph).T, np.sin(ph).T
    wi = np.concatenate([np.concatenate([c1[:, :half], s1[:, :half]], axis=1),
                         np.concatenate([-s1[:, :half], c1[:, :half]], axis=1)], axis=0) / FFT_N
    zi = np.zeros_like(wi)
    wi2 = np.stack([np.concatenate([wi, zi], axis=1), np.concatenate([zi, wi], axis=1)])
    as_b = lambda a: jnp.asarray(a, F32).astype(BF16)
    as_f = lambda a: jnp.asarray(a, F32)
    return [as_b(w1p), as_f(t1), as_f(t2), as_b(f2), as_b(g2), as_f(c2), as_f(s2), as_b(wi2)]


def _fwd_spectrum(xp, w1p, t1, t2, f2):
    cp = xp.shape[0]
    hn = FFT_N1
    a = jnp.dot(xp.reshape(cp * FFT_N2, LANE).astype(BF16), w1p, preferred_element_type=F32)
    out = []
    for par in range(2):
        ap = a[:, par * LANE:(par + 1) * LANE]
        ap = ap.reshape(cp, FFT_N2, LANE) * t1 + pltpu.roll(ap, hn, 1).reshape(cp, FFT_N2, LANE) * t2
        at = jnp.swapaxes(ap, 1, 2)
        p = jnp.dot(at.reshape(cp * 2 * hn, FFT_N2).astype(BF16), f2, preferred_element_type=F32)
        p = p.reshape(cp, 2 * hn, 2 * FFT_N2)
        out.append((p[:, :hn, :FFT_N2] - p[:, hn:, FFT_N2:], p[:, :hn, FFT_N2:] + p[:, hn:, :FFT_N2]))
    return out


def _inv_time(yre, yim, g2, c2, s2, wi_par):
    cp = yre.shape[0]
    hn = FFT_N1
    y = jnp.concatenate([yre, yim], axis=1).reshape(cp * 2 * hn, FFT_N2).astype(BF16)
    q = jnp.dot(y, g2, preferred_element_type=F32).reshape(cp, 2 * hn, 2 * FFT_N2)
    bre = q[:, :hn, :FFT_N2] - q[:, hn:, FFT_N2:]
    bim = q[:, :hn, FFT_N2:] + q[:, hn:, :FFT_N2]
    b2 = jnp.concatenate([bre * c2 - bim * s2, bre * s2 + bim * c2], axis=1)
    bt = jnp.swapaxes(b2, 1, 2)
    return jnp.dot(bt.reshape(cp * FFT_N2, LANE).astype(BF16), wi_par, preferred_element_type=F32)


def _hyena_kernel(z_ref, g1_ref, g2_ref, kf_ref, bias_ref, w1p_ref, t1_ref, t2_ref, f2_ref, gi_ref, c2_ref,
                  s2_ref, wi_ref, o_ref, zt_ref, g1t_ref, g2t_ref, ot_ref, stage_ref):
    k = pl.program_id(1)
    n_s = z_ref.shape[0] // FFT_N2
    cp = HY_PAIRS_PER_STEP

    def to_tiles(x_ref, t_ref):
        for s in range(n_s):
            xs = x_ref[s * FFT_N2:(s + 1) * FFT_N2, :].astype(F32)
            stage_ref[pl.ds(s, LANE, stride=HY_TILE_PITCH), :] = xs.T
        st = stage_ref[...].reshape(HY_PAIRS_PER_COL, 2 * HY_TILE_PITCH, FFT_N2)
        both = jnp.concatenate([st[:, :n_s], st[:, HY_TILE_PITCH:HY_TILE_PITCH + n_s]], axis=1)
        t_ref[...] = jnp.swapaxes(both, 1, 2)

    @pl.when(k == 0)
    def _():
        to_tiles(z_ref, zt_ref)
        to_tiles(g1_ref, g1t_ref)
        to_tiles(g2_ref, g2t_ref)

    sl = pl.ds(pl.multiple_of(k * cp, cp), cp)
    z = zt_ref[sl]
    gates = (g1t_ref, g2t_ref)
    for o in range(HY_ORDER):
        spec = _fwd_spectrum(z, w1p_ref[...], t1_ref[...], t2_ref[...], f2_ref[...])
        conv = None
        for par in range(2):
            xre, xim = spec[par]
            kre = kf_ref[o, par, :, :FFT_N1, :]
            kim = kf_ref[o, par, :, FFT_N1:, :]
            part = _inv_time(xre * kre - xim * kim, xre * kim + xim * kre,
                             gi_ref[...], c2_ref[...], s2_ref[...], wi_ref[par])
            conv = part if conv is None else conv + part
        z = gates[o][sl] * (conv.reshape(cp, FFT_N2, LANE) + bias_ref[o] * z)
    ot_ref[sl] = z

    @pl.when(k == pl.num_programs(1) - 1)
    def _():
        back = jnp.swapaxes(ot_ref[...], 1, 2)
        for c in range(LANE):
            stage_ref[c * HY_TILE_PITCH:c * HY_TILE_PITCH + n_s, :] = back[c // 2, (c % 2) * n_s:(c % 2 + 1) * n_s, :]
        for s in range(n_s):
            rows = stage_ref[pl.ds(s, LANE, stride=HY_TILE_PITCH), :]
            o_ref[s * FFT_N2:(s + 1) * FFT_N2, :] = rows.T.astype(o_ref.dtype)


def _hyena_conv(us, kf, biasp):
    t, _ = us.shape
    consts = _dft_constants()
    ncol = D_HY // LANE
    nsub = HY_PAIRS_PER_COL // HY_PAIRS_PER_STEP
    full = lambda a: pl.BlockSpec(a.shape, lambda j, k: (0,) * a.ndim)
    nat = lambda off: pl.BlockSpec((t, LANE), lambda j, k: (0, off + j))
    tiles = pltpu.VMEM((HY_PAIRS_PER_COL, FFT_N2, LANE), F32)
    return pl.pallas_call(
        _hyena_kernel,
        grid=(ncol, nsub),
        in_specs=[nat(0), nat(ncol), nat(2 * ncol),
                  pl.BlockSpec((HY_ORDER, 2, HY_PAIRS_PER_STEP, 2 * FFT_N1, FFT_N2),
                               lambda j, k: (0, 0, j * nsub + k, 0, 0)),
                  pl.BlockSpec((HY_ORDER, HY_PAIRS_PER_STEP, 1, LANE), lambda j, k: (0, j * nsub + k, 0, 0))]
                 + [full(a) for a in consts],
        out_specs=pl.BlockSpec((t, LANE), lambda j, k: (0, j)),
        out_shape=jax.ShapeDtypeStruct((t, D_HY), us.dtype),
        scratch_shapes=[tiles, tiles, tiles, tiles, pltpu.VMEM((LANE * HY_TILE_PITCH, FFT_N2), F32)],
        compiler_params=pltpu.CompilerParams(
            dimension_semantics=("parallel", "arbitrary"), vmem_limit_bytes=VMEM_LIMIT_BYTES),
        name="hyena_conv",
    )(us, us, us, kf, biasp, *consts)


def _filter_time_kernel(w1t_ref, w1c_ref, w1s_ref, b1_ref, w2_ref, b2_ref, w3_ref, b3_ref, fr_ref,
                        wf_ref, wb_ref, df_ref, db_ref, o_ref, h_ref, k_ref, *, length):
    n_fft = 2 * length
    hp = lax.Precision.HIGHEST

    @pl.when(pl.program_id(0) == 0)
    def _():
        pos = lax.broadcasted_iota(jnp.int32, (1, n_fft), 1)
        lag = jnp.where(pos < length, pos, n_fft - pos).astype(F32)
        t = lag / float(length - 1)
        w = (2.0 * math.pi / length) * lag
        band_step = (HY_BANDS - 1 - 1e-4) / (HY_BANDS - 1)
        bands = 1e-4 + band_step * lax.broadcasted_iota(jnp.int32, (HY_BANDS, 1), 0).astype(F32)
        ang = bands * w
        fr = fr_ref[...]
        h = (w1t_ref[...] * t + jnp.dot(w1c_ref[...], jnp.cos(ang), preferred_element_type=F32, precision=hp)
             - jnp.dot(w1s_ref[...], jnp.sin(ang), preferred_element_type=F32, precision=hp))
        h = jnp.sin(fr * (h + b1_ref[...]))
        h = jnp.sin(fr * (jnp.dot(w2_ref[...], h, preferred_element_type=F32, precision=hp) + b2_ref[...]))
        h = jnp.sin(fr * (jnp.dot(w3_ref[...], h, preferred_element_type=F32, precision=hp) + b3_ref[...]))
        hi = h.astype(BF16)
        h_ref[0] = hi
        h_ref[1] = (h - hi.astype(F32)).astype(BF16)

    def dot3(w, lo, hi_):
        w_hi = w.astype(BF16)
        w_lo = (w - w_hi.astype(F32)).astype(BF16)
        h_hi, h_lo = h_ref[0, :, lo:hi_], h_ref[1, :, lo:hi_]
        return (jnp.dot(w_hi, h_hi, preferred_element_type=F32) + jnp.dot(w_hi, h_lo, preferred_element_type=F32)
                + jnp.dot(w_lo, h_hi, preferred_element_type=F32))

    pos = lax.broadcasted_iota(jnp.int32, (1, length), 1)
    tf = pos.astype(F32) / float(length - 1)
    tb = (length - pos).astype(F32) / float(length - 1)
    kf = dot3(wf_ref[...], 0, length) * jnp.exp(-tf * df_ref[...])
    kb = dot3(wb_ref[...], length, n_fft) * jnp.exp(-tb * db_ref[...])
    kb = jnp.where(pos == 0, 0.0, kb)
    inv = 1.0 / (jnp.sum(jnp.abs(kf), axis=1, keepdims=True) + jnp.sum(jnp.abs(kb), axis=1, keepdims=True))
    k_ref[:, :length] = kf * inv
    k_ref[:, length:] = kb * inv
    cb = wf_ref.shape[0]
    for n1 in range(FILTER_TILE_ROWS):
        row = k_ref[:, n1 * FFT_N2:(n1 + 1) * FFT_N2] if n1 < FFT_N1 else jnp.zeros((cb, FFT_N2), F32)
        o_ref[pl.ds(n1, cb, stride=FILTER_TILE_ROWS), :] = row


def _filter_time(length, w1, b1, w2, b2, w3, b3, freq, w_out, cb=128):
    col = lambda v: v.reshape(-1, 1)
    w1t = w1.T
    n_ch = w_out.shape[1]
    deltas = jnp.abs(jnp.linspace(math.log(HY_TARGET) / HY_SLOW_PCT, math.log(HY_TARGET) / HY_FAST_PCT,
                                  n_ch, dtype=F32)).reshape(n_ch, 1)
    wot = w_out.T
    nb = D_HY // cb
    small = lambda a: pl.BlockSpec(a.shape, lambda i: (0,) * a.ndim)
    fwd = lambda i: ((i // nb) * HY_DIRS * nb + i % nb, 0)
    bwd = lambda i: ((i // nb) * HY_DIRS * nb + nb + i % nb, 0)
    ins = [w1t[:, 0:1], w1t[:, 1:1 + HY_BANDS], w1t[:, 1 + HY_BANDS:], col(b1), w2.T, col(b2), w3.T, col(b3),
           col(freq)]
    hy_ff = w2.shape[0]
    return pl.pallas_call(
        functools.partial(_filter_time_kernel, length=length),
        grid=(HY_ORDER * nb,),
        in_specs=[small(a) for a in ins] + [pl.BlockSpec((cb, hy_ff), fwd), pl.BlockSpec((cb, hy_ff), bwd),
                                            pl.BlockSpec((cb, 1), fwd), pl.BlockSpec((cb, 1), bwd)],
        out_specs=pl.BlockSpec((cb * FILTER_TILE_ROWS, FFT_N2), lambda i: (i, 0)),
        out_shape=jax.ShapeDtypeStruct((HY_ORDER * D_HY * FILTER_TILE_ROWS, FFT_N2), F32),
        scratch_shapes=[pltpu.VMEM((2, hy_ff, 2 * length), BF16), pltpu.VMEM((cb, 2 * length), F32)],
        compiler_params=pltpu.CompilerParams(
            dimension_semantics=("arbitrary",), vmem_limit_bytes=VMEM_LIMIT_BYTES),
        name="hyena_filter_time",
    )(*ins, wot, wot, deltas, deltas)


def _filter_spec_kernel(k_ref, w1p_ref, t1_ref, t2_ref, f2_ref, o_ref):
    cb = k_ref.shape[0] // FILTER_TILE_ROWS
    kt = k_ref[...].reshape(cb, FILTER_TILE_ROWS, FFT_N2)[:, :FFT_N1, :]
    xp = jnp.swapaxes(kt.reshape(cb // 2, 2 * FFT_N1, FFT_N2), 1, 2)
    spec = _fwd_spectrum(xp, w1p_ref[...], t1_ref[...], t2_ref[...], f2_ref[...])
    for par in range(2):
        o_ref[par, :, :FFT_N1, :] = spec[par][0]
        o_ref[par, :, FFT_N1:, :] = spec[par][1]


def _filter_spectrum(kt, cb=128):
    consts = _dft_constants(real_input=True)[:4]
    nb = D_HY // cb
    full = lambda a: pl.BlockSpec(a.shape, lambda i: (0,) * a.ndim)
    return pl.pallas_call(
        _filter_spec_kernel,
        grid=(HY_ORDER * nb,),
        in_specs=[pl.BlockSpec((cb * FILTER_TILE_ROWS, FFT_N2), lambda i: (i, 0))] + [full(a) for a in consts],
        out_specs=pl.BlockSpec((None, 2, cb // 2, 2 * FFT_N1, FFT_N2), lambda i: (i // nb, 0, i % nb, 0, 0)),
        out_shape=jax.ShapeDtypeStruct((HY_ORDER, 2, D_HY // 2, 2 * FFT_N1, FFT_N2), F32),
        compiler_params=pltpu.CompilerParams(
            dimension_semantics=("parallel",), vmem_limit_bytes=VMEM_LIMIT_BYTES),
        name="hyena_filter_spectrum",
    )(kt, *consts)


def _hyena(us, w1, b1, w2, b2, w3, b3, freq, w_out, bias):
    bsz, length, _ = us.shape
    assert 2 * length == FFT_N and bsz == 2, "one complex transform carries exactly two batch rows"
    kf = _filter_spectrum(_filter_time(length, w1, b1, w2, b2, w3, b3, freq, w_out))
    biasp = jnp.repeat(bias.reshape(HY_ORDER, D_HY // 2, 1, 2), FFT_N1, axis=-1)
    return _hyena_conv(us.reshape(bsz * length, -1), kf, biasp).reshape(bsz, length, D_HY)


def kernel(x, c, ctx, c_ctx, w_ada, b_ada, norm_g, ffn_w_gate, ffn_w_up, ffn_w_down, w_in,
           s5_lam_re, s5_lam_im, s5_log_dt, s5_b_re, s5_b_im, s5_c_re, s5_c_im, s5_d,
           hy_short_w, hy_short_b, hy_w1, hy_b1, hy_w2, hy_b2, hy_w3, hy_b3, hy_freq, hy_w_out,
           hy_bias, w_pa, w_pb, w_out, final_g):
    bsz, seq, d = x.shape
    ctx_len = ctx.shape[1]
    n_rows = seq // GRID_W
    depth = w_ada.shape[0]
    assert depth == 1, "context-token outputs are only dropped by the last layer"
    l = 0

    c_rows = jnp.concatenate([c, c_ctx[None, :], jnp.zeros((8 - bsz - 1, d), F32)], axis=0)
    mod_all = _ada_mod(c_rows, w_ada[l], b_ada[l])
    mod = mod_all[:bsz].reshape(bsz, N_SUB, N_MOD, 1, d)
    mod_c = mod_all[bsz:bsz + 1].reshape(1, N_SUB, N_MOD, 1, d)

    def mods(m, sub):
        return tuple(m[:, sub, k] for k in range(N_MOD))

    wg, wu, wd = ffn_w_gate[l], ffn_w_up[l], ffn_w_down[l]
    w_in_b = w_in[l]

    xt = x.reshape(bsz * seq, d)
    ct = ctx.reshape(bsz * ctx_len, d)

    xt = _ffn_sublayer(xt, mods(mod, 0), norm_g[l, 0], wg, wu, wd, 0)
    ct = _ffn_sublayer(ct, mods(mod_c, 0), norm_g[l, 0], wg, wu, wd, 0)

    assert GRID_W * n_rows == seq
    u_s5, us_hy, sig_gates = _in_proj(xt, mod[:, 1, 0], mod[:, 1, 1], norm_g[l, 1], w_in_b,
                                      hy_short_w[l], hy_short_b[l], n_u=I_HY, n_hy=I_GA - I_HY)
    (u_ctx,) = _in_proj(ct, mod_c[:, 1, 0], mod_c[:, 1, 1], norm_g[l, 1], w_in_b[:, :D_S5])

    y_s5 = _s5_bidirectional(u_s5.reshape(bsz, seq, D_S5), u_ctx.reshape(bsz, ctx_len, D_S5),
                             s5_lam_re[l], s5_lam_im[l], s5_log_dt[l],
                             s5_b_re[l], s5_b_im[l], s5_c_re[l], s5_c_im[l], s5_d[l])
    y_hy = _hyena(us_hy.reshape(bsz, seq, I_GA - I_HY),
                  hy_w1[l], hy_b1[l], hy_w2[l], hy_b2[l], hy_w3[l], hy_b3[l], hy_freq[l],
                  hy_w_out[l], hy_bias[l])

    xt = _merge(xt, mod[:, 1, 2], y_s5.reshape(bsz * seq, D_S5), y_hy.reshape(bsz * seq, D_HY),
                sig_gates, w_pa[l].astype(BF16), w_pb[l].astype(BF16), w_out[l].astype(BF16))

    xt = _ffn_sublayer(xt, mods(mod, 2), norm_g[l, 2], wg, wu, wd, 1, final_gain=final_g)
    return xt.reshape(bsz, seq, d)
```

```python
import functools
import math

import jax
import jax.numpy as jnp
import numpy as np
from jax import lax
from jax.experimental import pallas as pl
from jax.experimental.pallas import tpu as pltpu

F32 = jnp.float32
BF16 = jnp.bfloat16

D_MODEL = 2048
GRID_W = 64
D_S5 = 1024
S5_GROUP = 16
S5_GROUPS = D_S5 // S5_GROUP
S5_STATE = 64
S5_DIRS = 2
LAMBDA_RE_MAX = -1e-4
S5_CHUNK = 16
LANE = 128
S5_GROUPS_PER_COL = LANE // S5_GROUP
D_HY = 1024
HY_ORDER = 2
HY_DIRS = 2
HY_SHORT = 3
HY_EMB = 33
HY_BANDS = (HY_EMB - 1) // 2
HY_TARGET = 1e-2
HY_FAST_PCT = 0.3
HY_SLOW_PCT = 1.5
FFT_N1 = 64
FFT_N2 = 128
FFT_N = FFT_N1 * FFT_N2
FILTER_TILE_ROWS = FFT_N1 + 8
HY_TILE_PITCH = FILTER_TILE_ROWS
HY_PAIRS_PER_COL = LANE // 2
HY_PAIRS_PER_STEP = 16
I_HY = D_S5
I_GA = D_S5 + (HY_ORDER + 1) * D_HY
I_GB = I_GA + D_MODEL
D_IN = I_GB + D_MODEL
D_FF = 5632
N_SUB = 3
N_MOD = 3
HALF_STEP = 0.5
RMS_EPS = 1e-6

VMEM_LIMIT_BYTES = 58 * 1024 * 1024


def _rms_mod(x, gain, shift, scale):
    ms = jnp.mean(x * x, axis=-1, keepdims=True)
    y = x * lax.rsqrt(ms + RMS_EPS) * gain
    return y * (1.0 + scale) + shift


def _split_bf16(v):
    hi = v.astype(BF16)
    return hi, (v - hi.astype(F32)).astype(BF16)


def _ada_kernel(c_ref, w_ref, b_ref, o_ref):
    c = c_ref[...]
    a_hi, a_lo = _split_bf16(c * jax.nn.sigmoid(c))
    w_hi, w_lo = _split_bf16(w_ref[...])
    o_ref[...] = (jnp.dot(a_hi, w_hi, preferred_element_type=F32) + jnp.dot(a_lo, w_hi, preferred_element_type=F32)
                  + jnp.dot(a_hi, w_lo, preferred_element_type=F32) + b_ref[...])


def _ada_mod(c_rows, w, b, tn=1024):
    rows, d = c_rows.shape
    n = w.shape[1]
    return pl.pallas_call(
        _ada_kernel,
        grid=(n // tn,),
        in_specs=[pl.BlockSpec((rows, d), lambda j: (0, 0)),
                  pl.BlockSpec((d, tn), lambda j: (0, j)),
                  pl.BlockSpec((1, tn), lambda j: (0, j))],
        out_specs=pl.BlockSpec((rows, tn), lambda j: (0, j)),
        out_shape=jax.ShapeDtypeStruct((rows, n), F32),
        compiler_params=pltpu.CompilerParams(
            dimension_semantics=("arbitrary",), vmem_limit_bytes=VMEM_LIMIT_BYTES),
        name="ada_mod",
    )(c_rows, w, b.reshape(1, n))


def _ffn_kernel(x_ref, shift_ref, scale_ref, gate_ref, gain_ref, wg_ref, wu_ref, wd_ref,
                fg_ref, o_ref, h_ref, *, final_norm):
    j = pl.program_id(1)

    @pl.when(j == 0)
    def _():
        h_ref[...] = _rms_mod(x_ref[...], gain_ref[...], shift_ref[...], scale_ref[...]).astype(BF16)
        o_ref[...] = jnp.zeros_like(o_ref)

    h = h_ref[...]
    g = jnp.dot(h, wg_ref[...].astype(BF16), preferred_element_type=F32)
    u = jnp.dot(h, wu_ref[...].astype(BF16), preferred_element_type=F32)
    a = (g * jax.nn.sigmoid(g) * u).astype(BF16)
    o_ref[...] += jnp.dot(a, wd_ref[...].astype(BF16), preferred_element_type=F32)

    @pl.when(j == pl.num_programs(1) - 1)
    def _():
        y = x_ref[...] + (HALF_STEP * gate_ref[...]) * o_ref[...]
        if final_norm:
            ms = jnp.mean(y * y, axis=-1, keepdims=True)
            y = y * lax.rsqrt(ms + RMS_EPS) * fg_ref[...]
        o_ref[...] = y


def _ffn_sublayer(x, mods, gain, wg, wu, wd, which, final_gain=None, tm=1024, tf=256):
    t, d = x.shape
    bm = mods[0].shape[0]
    tm = min(tm, t // bm)
    blocks_per_batch = (t // bm) // tm
    dff = wg.shape[2]
    final_norm = final_gain is not None
    fg = final_gain if final_norm else gain
    mod_spec = pl.BlockSpec((None, 1, d), lambda i, j: (i // blocks_per_batch, 0, 0))
    vec_spec = pl.BlockSpec((1, d), lambda i, j: (0, 0))
    return pl.pallas_call(
        functools.partial(_ffn_kernel, final_norm=final_norm),
        grid=(t // tm, dff // tf),
        in_specs=[pl.BlockSpec((tm, d), lambda i, j: (i, 0)),
                  mod_spec, mod_spec, mod_spec, vec_spec,
                  pl.BlockSpec((None, d, tf), lambda i, j: (which, 0, j)),
                  pl.BlockSpec((None, d, tf), lambda i, j: (which, 0, j)),
                  pl.BlockSpec((None, tf, d), lambda i, j: (which, j, 0)),
                  vec_spec],
        out_specs=pl.BlockSpec((tm, d), lambda i, j: (i, 0)),
        out_shape=jax.ShapeDtypeStruct((t, d), F32),
        scratch_shapes=[pltpu.VMEM((tm, d), BF16)],
        compiler_params=pltpu.CompilerParams(
            dimension_semantics=("parallel", "arbitrary"), vmem_limit_bytes=VMEM_LIMIT_BYTES),
        name="ffn_final" if final_norm else "ffn",
    )(x, *mods, gain.reshape(1, d), wg, wu, wd, fg.reshape(1, d))


def _proj_kernel(x_ref, shift_ref, scale_ref, gain_ref, w_ref, sw_ref, sb_ref, *rest, n_u, n_hy, row_len, part):
    o_refs, h_ref = rest[:-1], rest[-1]
    j = pl.program_id(1)

    @pl.when(j == 0)
    def _():
        h_ref[...] = _rms_mod(x_ref[...], gain_ref[...], shift_ref[...], scale_ref[...]).astype(BF16)

    tm, tn = h_ref.shape[0], w_ref.shape[1]

    def in_parts(o_ref, epilogue):
        for c in range(0, tn, part):
            p = jnp.dot(h_ref[...], w_ref[:, c:c + part].astype(BF16), preferred_element_type=F32)
            o_ref[:, c:c + part] = epilogue(p, c).astype(o_ref.dtype)

    def short_conv(p, c):
        col = lax.broadcasted_iota(jnp.int32, p.shape, 0) % row_len
        prev = jnp.where(col == 0, 0.0, pltpu.roll(p, 1, 0))
        nxt = jnp.where(col == row_len - 1, 0.0, pltpu.roll(p, tm - 1, 0))
        sw = sw_ref[:, c:c + part]
        return sb_ref[:, c:c + part] + prev * sw[0:1] + p * sw[1:2] + nxt * sw[2:3]

    if n_hy == 0:
        in_parts(o_refs[0], lambda p, c: p)
        return

    @pl.when(j < n_u)
    def _():
        in_parts(o_refs[0], lambda p, c: p)

    @pl.when((j >= n_u) & (j < n_u + n_hy))
    def _():
        in_parts(o_refs[1], short_conv)

    @pl.when(j >= n_u + n_hy)
    def _():
        in_parts(o_refs[2], lambda p, c: jax.nn.sigmoid(p))


def _in_proj(x, shift, scale, gain, w, short_w=None, short_b=None, n_u=D_S5, n_hy=0, row_len=GRID_W,
             tm=1024, tn=512, part=256):
    t, d = x.shape
    bm = shift.shape[0]
    tm = min(tm, t // bm)
    blocks_per_batch = (t // bm) // tm
    n = w.shape[1]
    assert tm % row_len == 0 and (t // bm) % tm == 0
    bu, bh = n_u // tn, n_hy // tn
    bg = n // tn - bu - bh
    mod_spec = pl.BlockSpec((None, 1, d), lambda i, j: (i // blocks_per_batch, 0, 0))
    out_shape = [jax.ShapeDtypeStruct((t, n_u), F32)]
    out_specs = [pl.BlockSpec((tm, tn), lambda i, j: (i, jnp.minimum(j, bu - 1)))]
    if bh:
        out_shape += [jax.ShapeDtypeStruct((t, n_hy), BF16), jax.ShapeDtypeStruct((t, bg * tn), BF16)]
        out_specs += [pl.BlockSpec((tm, tn), lambda i, j: (i, jnp.clip(j - bu, 0, bh - 1))),
                      pl.BlockSpec((tm, tn), lambda i, j: (i, jnp.maximum(j - bu - bh, 0)))]
        sw, sb = short_w, short_b.reshape(1, n_hy)
        hy_blk = lambda i, j: (0, jnp.clip(j - bu, 0, bh - 1))
    else:
        sw, sb = jnp.zeros((HY_SHORT, tn), F32), jnp.zeros((1, tn), F32)
        hy_blk = lambda i, j: (0, 0)
    return pl.pallas_call(
        functools.partial(_proj_kernel, n_u=bu, n_hy=bh, row_len=row_len, part=part),
        grid=(t // tm, n // tn),
        in_specs=[pl.BlockSpec((tm, d), lambda i, j: (i, 0)),
                  mod_spec, mod_spec,
                  pl.BlockSpec((1, d), lambda i, j: (0, 0)),
                  pl.BlockSpec((d, tn), lambda i, j: (0, j)),
                  pl.BlockSpec((HY_SHORT, tn), hy_blk),
                  pl.BlockSpec((1, tn), hy_blk)],
        out_specs=out_specs,
        out_shape=out_shape,
        scratch_shapes=[pltpu.VMEM((tm, d), BF16)],
        compiler_params=pltpu.CompilerParams(
            dimension_semantics=("parallel", "arbitrary"), vmem_limit_bytes=VMEM_LIMIT_BYTES),
        name="in_proj",
    )(x, shift, scale, gain.reshape(1, d), w, sw, sb)


def _gelu_tanh(x):
    return 0.5 * x * (1.0 + jnp.tanh(math.sqrt(2.0 / math.pi) * (x + 0.044715 * (x * x * x))))


def _merge_kernel(x_ref, gate_ref, ys_ref, yh_ref, ga_ref, gb_ref, wpa_lo_ref, wpa_hi_ref, wpb_ref,
                  wout_ref, o_ref, s_ref, acc_ref):
    j = pl.program_id(1)

    @pl.when(j == 0)
    def _():
        s_ref[...] = _gelu_tanh(ys_ref[...].astype(F32)).astype(BF16)
        acc_ref[...] = jnp.zeros_like(acc_ref)

    s = s_ref[...]
    pa_lo = jnp.dot(s, wpa_lo_ref[...], preferred_element_type=F32)
    pa_hi = jnp.dot(s, wpa_hi_ref[...], preferred_element_type=F32)
    y_a = pa_lo * jax.nn.sigmoid(pa_hi)
    y_b = jnp.dot(yh_ref[...], wpb_ref[...], preferred_element_type=F32)
    m = ga_ref[...].astype(F32) * y_a + gb_ref[...].astype(F32) * y_b
    acc_ref[...] += jnp.dot(m.astype(BF16), wout_ref[...], preferred_element_type=F32)

    @pl.when(j == pl.num_programs(1) - 1)
    def _():
        o_ref[...] = x_ref[...] + gate_ref[...] * acc_ref[...]


def _merge(x, gate, y_s5, y_hy, sig_gates, w_pa, w_pb, w_out, tm=512, tn=1024):
    t, d = x.shape
    bm = gate.shape[0]
    blocks_per_batch = (t // bm) // tm
    nj = d // tn
    ds5 = y_s5.shape[1]
    dhy = y_hy.shape[1]
    return pl.pallas_call(
        _merge_kernel,
        grid=(t // tm, nj),
        in_specs=[pl.BlockSpec((tm, d), lambda i, j: (i, 0)),
                  pl.BlockSpec((None, 1, d), lambda i, j: (i // blocks_per_batch, 0, 0)),
                  pl.BlockSpec((tm, ds5), lambda i, j: (i, 0)),
                  pl.BlockSpec((tm, dhy), lambda i, j: (i, 0)),
                  pl.BlockSpec((tm, tn), lambda i, j: (i, j)),
                  pl.BlockSpec((tm, tn), lambda i, j: (i, nj + j)),
                  pl.BlockSpec((ds5, tn), lambda i, j: (0, j)),
                  pl.BlockSpec((ds5, tn), lambda i, j: (0, nj + j)),
                  pl.BlockSpec((dhy, tn), lambda i, j: (0, j)),
                  pl.BlockSpec((tn, d), lambda i, j: (j, 0))],
        out_specs=pl.BlockSpec((tm, d), lambda i, j: (i, 0)),
        out_shape=jax.ShapeDtypeStruct((t, d), F32),
        scratch_shapes=[pltpu.VMEM((tm, ds5), BF16), pltpu.VMEM((tm, d), F32)],
        compiler_params=pltpu.CompilerParams(
            dimension_semantics=("parallel", "arbitrary"), vmem_limit_bytes=VMEM_LIMIT_BYTES),
        name="merge",
    )(x, gate, y_s5, y_hy, sig_gates, sig_gates, w_pa, w_pa, w_pb, w_out)


def _s5_weights_kernel(*refs, n_steps):
    blocks, ca_ref = refs[:-1], refs[-1]

    def one(gi, carry):
        _s5_weights_group(*[r.at[gi] for r in blocks], ca_ref, n_steps=n_steps)
        return carry

    lax.fori_loop(0, blocks[0].shape[0], one, 0)


def _s5_weights_group(par_ref, bt_ref, c_ref, d_ref, bpow_ref, toep_ref, cpow_ref, tab_ref, ca_ref, *, n_steps):
    t, h, p = S5_CHUNK, S5_GROUP, S5_STATE
    lanes = 2 * p
    hp = lax.Precision.HIGHEST
    sgn = jnp.where(lax.broadcasted_iota(jnp.int32, (1, lanes), 1) < p, -1.0, 1.0)
    par = par_ref[...]
    gsum = None
    for d in range(S5_DIRS):
        lr = jnp.minimum(par[3 * d:3 * d + 1], LAMBDA_RE_MAX)
        li = par[3 * d + 1:3 * d + 2]
        dt = jnp.exp(par[3 * d + 2:3 * d + 3])
        zr, zi = lr * dt, li * dt

        def apow(j):
            mag = jnp.exp(j * zr)
            return mag * jnp.cos(j * zi), sgn * (mag * jnp.sin(j * zi))

        def cmul(x, a1, a2):
            return x * a1 + pltpu.roll(x, p, 1) * a2

        a1, a2 = apow(lax.broadcasted_iota(jnp.int32, (t + 1, 1), 0).astype(F32))
        nr, ni = a1[1:2] - 1.0, sgn * a2[1:2]
        den = lr * lr + li * li
        f_re = (nr * lr + ni * li) / den
        f_im = (ni * lr - nr * li) / den
        bbar = cmul(bt_ref[d], f_re, sgn * f_im)
        cc = c_ref[d]
        ca = [cmul(cc, a1[j:j + 1], a2[j:j + 1]) * (-sgn) for j in range(t + 1)]
        ca_ref[...] = jnp.zeros_like(ca_ref)
        for k in range(t):
            e_b, e_c = (t - 1 - k, k + 1) if d == 0 else (k, t - k)
            bpow_ref[k * h:(k + 1) * h, d * lanes:(d + 1) * lanes] = (
                cmul(bbar, a1[e_b:e_b + 1], a2[e_b:e_b + 1]).astype(BF16))
            cpow_ref[k * h:(k + 1) * h, d * lanes:(d + 1) * lanes] = ca[e_c].astype(BF16)
            l = t - 1 + k if d == 0 else t - 1 - k
            ca_ref[l * h:(l + 1) * h, :] = ca[k]
        g = lax.dot_general(bbar, ca_ref[...], (((1,), (1,)), ((), ())), preferred_element_type=F32, precision=hp)
        gsum = g if gsum is None else gsum + g
        for s in range(n_steps):
            s1, s2 = apow(float(t * 2 ** s))
            r = d * 2 * n_steps + 2 * s
            tab_ref[r:r + 1, :] = s1
            tab_ref[r + 1:r + 2, :] = s2
    wide = gsum.shape[1]
    col = lax.broadcasted_iota(jnp.int32, (h, wide), 1)
    row = lax.broadcasted_iota(jnp.int32, (h, wide), 0)
    gsum = gsum + jnp.where(col - (t - 1) * h == row, d_ref[...], 0.0)
    for k in range(t):
        off = (t - 1 - k) * h
        shifted = gsum if off == 0 else pltpu.roll(gsum, wide - off, 1)
        toep_ref[k * h:(k + 1) * h, :] = shifted[:, :t * h].astype(BF16)


def _s5_weights(lam_re, lam_im, log_dt, b_re, b_im, c_re, c_im, d_skip, n_steps):
    g, p, h, t = S5_GROUPS, S5_STATE, S5_GROUP, S5_CHUNK
    cat2 = lambda a: jnp.concatenate([a, a], axis=-1)
    par = jnp.stack([cat2(lam_re), cat2(lam_im), jnp.broadcast_to(log_dt[..., None], (S5_DIRS, g, 2 * p))], axis=1)
    par = par.transpose(2, 0, 1, 3).reshape(g, 3 * S5_DIRS, 2 * p)
    btc = jnp.concatenate([b_re, b_im], axis=2).transpose(1, 0, 3, 2)
    ccat = jnp.concatenate([c_re, c_im], axis=3).transpose(1, 0, 2, 3)
    wide = 2 * t * h
    drow = jnp.zeros((g, 1, wide), F32).at[:, 0, (t - 1) * h:t * h].set(d_skip.reshape(g, h))
    w = t * h
    gb = S5_GROUPS_PER_COL
    sq = pl.BlockSpec((gb, w, w), lambda i: (i, 0, 0))
    return pl.pallas_call(
        functools.partial(_s5_weights_kernel, n_steps=n_steps),
        grid=(g // gb,),
        in_specs=[pl.BlockSpec((gb, 3 * S5_DIRS, 2 * p), lambda i: (i, 0, 0)),
                  pl.BlockSpec((gb, S5_DIRS, h, 2 * p), lambda i: (i, 0, 0, 0)),
                  pl.BlockSpec((gb, S5_DIRS, h, 2 * p), lambda i: (i, 0, 0, 0)),
                  pl.BlockSpec((gb, 1, wide), lambda i: (i, 0, 0))],
        out_specs=[sq, sq, sq, pl.BlockSpec((gb, 4 * n_steps, 2 * p), lambda i: (i, 0, 0))],
        out_shape=[jax.ShapeDtypeStruct((g, w, w), BF16)] * 3 + [jax.ShapeDtypeStruct((g, 4 * n_steps, 2 * p), F32)],
        scratch_shapes=[pltpu.VMEM((wide, 2 * p), F32)],
        compiler_params=pltpu.CompilerParams(
            dimension_semantics=("parallel",), vmem_limit_bytes=VMEM_LIMIT_BYTES),
        name="s5_weights",
    )(par, btc, ccat, drow)


def _s5_kernel(u_ref, uc_ref, bpow_ref, toep_ref, cpow_ref, tab_ref, y_ref, ucat_ref, ucc_ref, yacc_ref,
               *, bsz, n_steps):
    t, gw = S5_CHUNK, S5_GROUP
    per = LANE // gw
    n_lat = u_ref.shape[0] // (bsz * t)
    n_ctx = uc_ref.shape[0] // (bsz * t)
    n_ch = n_lat + n_ctx
    rows = bsz * n_ch
    half = 2 * S5_STATE
    for k in range(t):
        ucat_ref[:, k * LANE:(k + 1) * LANE] = u_ref[pl.ds(k, bsz * n_lat, stride=t), :]
        ucc_ref[:, k * LANE:(k + 1) * LANE] = uc_ref[pl.ds(k, bsz * n_ctx, stride=t), :]
    yacc_ref[...] = jnp.zeros_like(yacc_ref)
    rib = lax.broadcasted_iota(jnp.int32, (rows, half), 0) % n_ch
    lane = lax.broadcasted_iota(jnp.int32, (1, LANE), 1)

    def cmul_add(acc, sh, a1, a2):
        return acc + a1 * sh + a2 * pltpu.roll(sh, S5_STATE, 1)

    def gather(src_ref, gl):
        cols = []
        for j in range(t // per):
            acc = None
            for tt in range(per):
                k = j * per + tt
                r = pltpu.roll(src_ref[:, k * LANE:(k + 1) * LANE], (gw * tt - gw * gl) % LANE, 1)
                acc = r if acc is None else jnp.where((lane >= gw * tt) & (lane < gw * (tt + 1)), r, acc)
            cols.append(acc)
        return jnp.concatenate(cols, axis=1).astype(BF16)

    def group(gl, carry):
        ul = gather(ucat_ref, gl)
        uc = gather(ucc_ref, gl)
        bpow = bpow_ref[gl]
        zl = jnp.dot(ul, bpow, preferred_element_type=F32)
        zc = jnp.dot(uc, bpow, preferred_element_type=F32)
        fparts, bparts = [], []
        for b in range(bsz):
            lat = slice(b * n_lat, (b + 1) * n_lat)
            ctx = slice(b * n_ctx, (b + 1) * n_ctx)
            fparts += [zc[ctx, :half], zl[lat, :half]]
            bparts += [zl[lat, half:], zc[ctx, half:]]
        fw = jnp.concatenate(fparts, axis=0)
        bw = jnp.concatenate(bparts, axis=0)
        tab = tab_ref[gl]
        for s in range(n_steps):
            d = 1 << s
            sh = jnp.where(rib >= d, pltpu.roll(fw, d, 0), 0.0)
            fw = cmul_add(fw, sh, tab[2 * s:2 * s + 1], tab[2 * s + 1:2 * s + 2])
            o = 2 * n_steps
            sh = jnp.where(rib < n_ch - d, pltpu.roll(bw, rows - d, 0), 0.0)
            bw = cmul_add(bw, sh, tab[o + 2 * s:o + 2 * s + 1], tab[o + 2 * s + 1:o + 2 * s + 2])
        fe = jnp.where(rib >= 1, pltpu.roll(fw, 1, 0), 0.0)
        be = jnp.where(rib < n_ch - 1, pltpu.roll(bw, rows - 1, 0), 0.0)
        fl = jnp.concatenate([fe[b * n_ch + n_ctx:(b + 1) * n_ch] for b in range(bsz)], axis=0)
        bl = jnp.concatenate([be[b * n_ch:b * n_ch + n_lat] for b in range(bsz)], axis=0)
        st = jnp.concatenate([fl, bl], axis=1).astype(BF16)
        y = (jnp.dot(ul, toep_ref[gl], preferred_element_type=F32)
             + lax.dot_general(st, cpow_ref[gl], (((1,), (1,)), ((), ())), preferred_element_type=F32))
        mine = (lane >= gw * gl) & (lane < gw * (gl + 1))
        for i in range(t):
            src = y[:, (i // per) * LANE:(i // per + 1) * LANE]
            r = pltpu.roll(src, (gw * gl - gw * (i % per)) % LANE, 1)
            blk = slice(i * LANE, (i + 1) * LANE)
            yacc_ref[:, blk] = jnp.where(mine, r, yacc_ref[:, blk])
        return carry

    lax.fori_loop(0, S5_GROUPS_PER_COL, group, 0)
    for i in range(t):
        y_ref[pl.ds(i, bsz * n_lat, stride=t), :] = yacc_ref[:, i * LANE:(i + 1) * LANE]


def _s5_mix(u, uc, bpw, toep, cpw, tab, bsz):
    rl, dm = u.shape
    rc = uc.shape[0]
    t = S5_CHUNK
    w = t * S5_GROUP
    n_steps = tab.shape[1] // 4
    gpc = S5_GROUPS_PER_COL
    wspec = pl.BlockSpec((gpc, w, w), lambda i: (i, 0, 0))
    return pl.pallas_call(
        functools.partial(_s5_kernel, bsz=bsz, n_steps=n_steps),
        grid=(dm // LANE,),
        in_specs=[pl.BlockSpec((rl, LANE), lambda i: (0, i)),
                  pl.BlockSpec((rc, LANE), lambda i: (0, i)),
                  wspec, wspec, wspec,
                  pl.BlockSpec((gpc, 4 * n_steps, tab.shape[2]), lambda i: (i, 0, 0))],
        out_specs=pl.BlockSpec((rl, LANE), lambda i: (0, i)),
        out_shape=jax.ShapeDtypeStruct(u.shape, F32),
        scratch_shapes=[pltpu.VMEM((rl // t, t * LANE), F32), pltpu.VMEM((rc // t, t * LANE), F32),
                        pltpu.VMEM((rl // t, t * LANE), F32)],
        compiler_params=pltpu.CompilerParams(
            dimension_semantics=("parallel",), vmem_limit_bytes=VMEM_LIMIT_BYTES),
        name="s5_mix",
    )(u, uc, bpw, toep, cpw, tab)


def _s5_bidirectional(u, u_ctx, lam_re, lam_im, log_dt, b_re, b_im, c_re, c_im, d_skip):
    bsz, length, dm = u.shape
    ctx_len = u_ctx.shape[1]
    t = S5_CHUNK
    n_lat, n_ctx = length // t, ctx_len // t
    n_steps = max(1, math.ceil(math.log2(n_lat + n_ctx)))
    bpw, toep, cpw, tab = _s5_weights(lam_re, lam_im, log_dt, b_re, b_im, c_re, c_im, d_skip, n_steps)
    y = _s5_mix(u.reshape(bsz * length, dm), u_ctx.reshape(bsz * ctx_len, dm), bpw, toep, cpw, tab, bsz)
    return y.reshape(bsz, length, dm)


def _dft_constants(real_input=False):
    n1 = np.arange(FFT_N1)
    n2 = np.arange(FFT_N2)
    half = FFT_N1 // 2
    th = 2 * np.pi * np.outer(n1, n1) / FFT_N1
    c1, s1 = np.cos(th), np.sin(th)
    if real_input:
        w1 = np.concatenate([c1, -s1], axis=1)
    else:
        w1 = np.concatenate([np.concatenate([c1[:half], -s1[:half]], axis=1),
                             np.concatenate([s1[:half], c1[:half]], axis=1)], axis=0)
    z = np.zeros_like(w1)
    w1p = np.block([[w1, z], [z, w1]])
    ph = 2 * np.pi * np.outer(n2, n1) / FFT_N
    ps = 2 * np.pi * np.outer(n2, n2) / FFT_N2
    f2 = np.concatenate([np.cos(ps), -np.sin(ps)], axis=1)
    g2 = np.concatenate([np.cos(ps), np.sin(ps)], axis=1)
    c2, s2 = np.cos(ph).T, np.sin(ph).T
    wi = np.concatenate([np.concatenate([c1[:, :half], s1[:, :half]], axis=1),
                         np.concatenate([-s1[:, :half], c1[:, :half]], axis=1)], axis=0) / FFT_N
    zi = np.zeros_like(wi)
    wi2 = np.stack([np.concatenate([wi, zi], axis=1), np.concatenate([zi, wi], axis=1)])
    as_b = lambda a: jnp.asarray(a, F32).astype(BF16)
    as_f = lambda a: jnp.asarray(a, F32)
    return [as_b(w1p), as_f(c2), as_f(s2), as_b(f2), as_b(g2), as_b(wi2)]


def _fwd_spectrum(xp, w1p, c2, s2, f2):
    cp = xp.shape[0]
    hn = FFT_N1
    a = jnp.dot(xp.reshape(cp * FFT_N2, LANE).astype(BF16), w1p, preferred_element_type=F32)
    out = []
    for par in range(2):
        ap = a[:, par * LANE:(par + 1) * LANE].reshape(cp, FFT_N2, LANE)
        at = jnp.swapaxes(ap, 1, 2)
        are, aim = at[:, :hn], at[:, hn:]
        at = jnp.concatenate([are * c2 + aim * s2, aim * c2 - are * s2], axis=1)
        p = jnp.dot(at.reshape(cp * 2 * hn, FFT_N2).astype(BF16), f2, preferred_element_type=F32)
        p = p.reshape(cp, 2 * hn, 2 * FFT_N2)
        out.append((p[:, :hn, :FFT_N2] - p[:, hn:, FFT_N2:], p[:, :hn, FFT_N2:] + p[:, hn:, :FFT_N2]))
    return out


def _inv_time(yre, yim, g2, c2, s2, wi_par):
    cp = yre.shape[0]
    hn = FFT_N1
    y = jnp.concatenate([yre, yim], axis=1).reshape(cp * 2 * hn, FFT_N2).astype(BF16)
    q = jnp.dot(y, g2, preferred_element_type=F32).reshape(cp, 2 * hn, 2 * FFT_N2)
    bre = q[:, :hn, :FFT_N2] - q[:, hn:, FFT_N2:]
    bim = q[:, :hn, FFT_N2:] + q[:, hn:, :FFT_N2]
    b2 = jnp.concatenate([bre * c2 - bim * s2, bre * s2 + bim * c2], axis=1)
    bt = jnp.swapaxes(b2, 1, 2)
    return jnp.dot(bt.reshape(cp * FFT_N2, LANE).astype(BF16), wi_par, preferred_element_type=F32)


def _hyena_kernel(z_ref, g1_ref, g2_ref, kf_ref, bias_ref, w1p_ref, c2_ref, s2_ref, f2_ref, gi_ref, wi_ref,
                  o_ref, zt_ref, g1t_ref, g2t_ref, ot_ref, stage_ref):
    k = pl.program_id(1)
    n_s = z_ref.shape[0] // FFT_N2
    cp = HY_PAIRS_PER_STEP

    def to_tiles(x_ref, t_ref):
        for s in range(n_s):
            xs = x_ref[s * FFT_N2:(s + 1) * FFT_N2, :].astype(F32)
            stage_ref[pl.ds(s, LANE, stride=HY_TILE_PITCH), :] = xs.T
        st = stage_ref[...].reshape(HY_PAIRS_PER_COL, 2 * HY_TILE_PITCH, FFT_N2)
        both = jnp.concatenate([st[:, :n_s], st[:, HY_TILE_PITCH:HY_TILE_PITCH + n_s]], axis=1)
        t_ref[...] = jnp.swapaxes(both, 1, 2)

    @pl.when(k == 0)
    def _():
        to_tiles(z_ref, zt_ref)
        to_tiles(g1_ref, g1t_ref)
        to_tiles(g2_ref, g2t_ref)

    sl = pl.ds(pl.multiple_of(k * cp, cp), cp)
    z = zt_ref[sl]
    gates = (g1t_ref, g2t_ref)
    for o in range(HY_ORDER):
        spec = _fwd_spectrum(z, w1p_ref[...], c2_ref[...], s2_ref[...], f2_ref[...])
        conv = None
        for par in range(2):
            xre, xim = spec[par]
            kre = kf_ref[o, par, :, :FFT_N1, :]
            kim = kf_ref[o, par, :, FFT_N1:, :]
            part = _inv_time(xre * kre - xim * kim, xre * kim + xim * kre,
                             gi_ref[...], c2_ref[...], s2_ref[...], wi_ref[par])
            conv = part if conv is None else conv + part
        z = gates[o][sl] * (conv.reshape(cp, FFT_N2, LANE) + bias_ref[o] * z)
    ot_ref[sl] = z

    @pl.when(k == pl.num_programs(1) - 1)
    def _():
        back = jnp.swapaxes(ot_ref[...], 1, 2)
        for c in range(LANE):
            stage_ref[c * HY_TILE_PITCH:c * HY_TILE_PITCH + n_s, :] = back[c // 2, (c % 2) * n_s:(c % 2 + 1) * n_s, :]
        for s in range(n_s):
            rows = stage_ref[pl.ds(s, LANE, stride=HY_TILE_PITCH), :]
            o_ref[s * FFT_N2:(s + 1) * FFT_N2, :] = rows.T.astype(o_ref.dtype)


def _hyena_conv(us, kf, biasp):
    t, _ = us.shape
    consts = _dft_constants()
    ncol = D_HY // LANE
    nsub = HY_PAIRS_PER_COL // HY_PAIRS_PER_STEP
    full = lambda a: pl.BlockSpec(a.shape, lambda j, k: (0,) * a.ndim)
    nat = lambda off: pl.BlockSpec((t, LANE), lambda j, k: (0, off + j))
    tiles = pltpu.VMEM((HY_PAIRS_PER_COL, FFT_N2, LANE), F32)
    return pl.pallas_call(
        _hyena_kernel,
        grid=(ncol, nsub),
        in_specs=[nat(0), nat(ncol), nat(2 * ncol),
                  pl.BlockSpec((HY_ORDER, 2, HY_PAIRS_PER_STEP, 2 * FFT_N1, FFT_N2),
                               lambda j, k: (0, 0, j * nsub + k, 0, 0)),
                  pl.BlockSpec((HY_ORDER, HY_PAIRS_PER_STEP, 1, LANE), lambda j, k: (0, j * nsub + k, 0, 0))]
                 + [full(a) for a in consts],
        out_specs=pl.BlockSpec((t, LANE), lambda j, k: (0, j)),
        out_shape=jax.ShapeDtypeStruct((t, D_HY), us.dtype),
        scratch_shapes=[tiles, tiles, tiles, tiles, pltpu.VMEM((LANE * HY_TILE_PITCH, FFT_N2), F32)],
        compiler_params=pltpu.CompilerParams(
            dimension_semantics=("parallel", "arbitrary"), vmem_limit_bytes=VMEM_LIMIT_BYTES),
        name="hyena_conv",
    )(us, us, us, kf, biasp, *consts)


def _filter_time_kernel(w1t_ref, w1c_ref, w1s_ref, b1_ref, w2_ref, b2_ref, w3_ref, b3_ref, fr_ref,
                        wf_ref, wb_ref, df_ref, db_ref, o_ref, h_ref, k_ref, *, length):
    n_fft = 2 * length
    hp = lax.Precision.HIGHEST

    @pl.when(pl.program_id(0) == 0)
    def _():
        pos = lax.broadcasted_iota(jnp.int32, (1, n_fft), 1)
        lag = jnp.where(pos < length, pos, n_fft - pos).astype(F32)
        t = lag / float(length - 1)
        w = (2.0 * math.pi / length) * lag
        band_step = (HY_BANDS - 1 - 1e-4) / (HY_BANDS - 1)
        bands = 1e-4 + band_step * lax.broadcasted_iota(jnp.int32, (HY_BANDS, 1), 0).astype(F32)
        ang = bands * w
        fr = fr_ref[...]
        h = (w1t_ref[...] * t + jnp.dot(w1c_ref[...], jnp.cos(ang), preferred_element_type=F32, precision=hp)
             - jnp.dot(w1s_ref[...], jnp.sin(ang), preferred_element_type=F32, precision=hp))
        h = jnp.sin(fr * (h + b1_ref[...]))
        h = jnp.sin(fr * (jnp.dot(w2_ref[...], h, preferred_element_type=F32, precision=hp) + b2_ref[...]))
        h = jnp.sin(fr * (jnp.dot(w3_ref[...], h, preferred_element_type=F32, precision=hp) + b3_ref[...]))
        hi = h.astype(BF16)
        h_ref[0] = hi
        h_ref[1] = (h - hi.astype(F32)).astype(BF16)

    def dot3(w, lo, hi_):
        w_hi = w.astype(BF16)
        w_lo = (w - w_hi.astype(F32)).astype(BF16)
        h_hi, h_lo = h_ref[0, :, lo:hi_], h_ref[1, :, lo:hi_]
        return (jnp.dot(w_hi, h_hi, preferred_element_type=F32) + jnp.dot(w_hi, h_lo, preferred_element_type=F32)
                + jnp.dot(w_lo, h_hi, preferred_element_type=F32))

    pos = lax.broadcasted_iota(jnp.int32, (1, length), 1)
    tf = pos.astype(F32) / float(length - 1)
    tb = (length - pos).astype(F32) / float(length - 1)
    kf = dot3(wf_ref[...], 0, length) * jnp.exp(-tf * df_ref[...])
    kb = dot3(wb_ref[...], length, n_fft) * jnp.exp(-tb * db_ref[...])
    kb = jnp.where(pos == 0, 0.0, kb)
    inv = 1.0 / (jnp.sum(jnp.abs(kf), axis=1, keepdims=True) + jnp.sum(jnp.abs(kb), axis=1, keepdims=True))
    k_ref[:, :length] = kf * inv
    k_ref[:, length:] = kb * inv
    cb = wf_ref.shape[0]
    for n1 in range(FILTER_TILE_ROWS):
        row = k_ref[:, n1 * FFT_N2:(n1 + 1) * FFT_N2] if n1 < FFT_N1 else jnp.zeros((cb, FFT_N2), F32)
        o_ref[pl.ds(n1, cb, stride=FILTER_TILE_ROWS), :] = row


def _filter_time(length, w1, b1, w2, b2, w3, b3, freq, w_out, cb=128):
    col = lambda v: v.reshape(-1, 1)
    w1t = w1.T
    n_ch = w_out.shape[1]
    deltas = jnp.abs(jnp.linspace(math.log(HY_TARGET) / HY_SLOW_PCT, math.log(HY_TARGET) / HY_FAST_PCT,
                                  n_ch, dtype=F32)).reshape(n_ch, 1)
    wot = w_out.T
    nb = D_HY // cb
    small = lambda a: pl.BlockSpec(a.shape, lambda i: (0,) * a.ndim)
    fwd = lambda i: ((i // nb) * HY_DIRS * nb + i % nb, 0)
    bwd = lambda i: ((i // nb) * HY_DIRS * nb + nb + i % nb, 0)
    ins = [w1t[:, 0:1], w1t[:, 1:1 + HY_BANDS], w1t[:, 1 + HY_BANDS:], col(b1), w2.T, col(b2), w3.T, col(b3),
           col(freq)]
    hy_ff = w2.shape[0]
    return pl.pallas_call(
        functools.partial(_filter_time_kernel, length=length),
        grid=(HY_ORDER * nb,),
        in_specs=[small(a) for a in ins] + [pl.BlockSpec((cb, hy_ff), fwd), pl.BlockSpec((cb, hy_ff), bwd),
                                            pl.BlockSpec((cb, 1), fwd), pl.BlockSpec((cb, 1), bwd)],
        out_specs=pl.BlockSpec((cb * FILTER_TILE_ROWS, FFT_N2), lambda i: (i, 0)),
        out_shape=jax.ShapeDtypeStruct((HY_ORDER * D_HY * FILTER_TILE_ROWS, FFT_N2), F32),
        scratch_shapes=[pltpu.VMEM((2, hy_ff, 2 * length), BF16), pltpu.VMEM((cb, 2 * length), F32)],
        compiler_params=pltpu.CompilerParams(
            dimension_semantics=("arbitrary",), vmem_limit_bytes=VMEM_LIMIT_BYTES),
        name="hyena_filter_time",
    )(*ins, wot, wot, deltas, deltas)


def _filter_spec_kernel(k_ref, w1p_ref, c2_ref, s2_ref, f2_ref, o_ref):
    cb = k_ref.shape[0] // FILTER_TILE_ROWS
    kt = k_ref[...].reshape(cb, FILTER_TILE_ROWS, FFT_N2)[:, :FFT_N1, :]
    xp = jnp.swapaxes(kt.reshape(cb // 2, 2 * FFT_N1, FFT_N2), 1, 2)
    spec = _fwd_spectrum(xp, w1p_ref[...], c2_ref[...], s2_ref[...], f2_ref[...])
    for par in range(2):
        o_ref[par, :, :FFT_N1, :] = spec[par][0]
        o_ref[par, :, FFT_N1:, :] = spec[par][1]


def _filter_spectrum(kt, cb=128):
    consts = _dft_constants(real_input=True)[:4]
    nb = D_HY // cb
    full = lambda a: pl.BlockSpec(a.shape, lambda i: (0,) * a.ndim)
    return pl.pallas_call(
        _filter_spec_kernel,
        grid=(HY_ORDER * nb,),
        in_specs=[pl.BlockSpec((cb * FILTER_TILE_ROWS, FFT_N2), lambda i: (i, 0))] + [full(a) for a in consts],
        out_specs=pl.BlockSpec((None, 2, cb // 2, 2 * FFT_N1, FFT_N2), lambda i: (i // nb, 0, i % nb, 0, 0)),
        out_shape=jax.ShapeDtypeStruct((HY_ORDER, 2, D_HY // 2, 2 * FFT_N1, FFT_N2), F32),
        compiler_params=pltpu.CompilerParams(
            dimension_semantics=("parallel",), vmem_limit_bytes=VMEM_LIMIT_BYTES),
        name="hyena_filter_spectrum",
    )(kt, *consts)


def _hyena(us, w1, b1, w2, b2, w3, b3, freq, w_out, bias):
    bsz, length, _ = us.shape
    assert 2 * length == FFT_N and bsz == 2, "one complex transform carries exactly two batch rows"
    kf = _filter_spectrum(_filter_time(length, w1, b1, w2, b2, w3, b3, freq, w_out))
    biasp = jnp.repeat(bias.reshape(HY_ORDER, D_HY // 2, 1, 2), FFT_N1, axis=-1)
    return _hyena_conv(us.reshape(bsz * length, -1), kf, biasp).reshape(bsz, length, D_HY)


def kernel(x, c, ctx, c_ctx, w_ada, b_ada, norm_g, ffn_w_gate, ffn_w_up, ffn_w_down, w_in,
           s5_lam_re, s5_lam_im, s5_log_dt, s5_b_re, s5_b_im, s5_c_re, s5_c_im, s5_d,
           hy_short_w, hy_short_b, hy_w1, hy_b1, hy_w2, hy_b2, hy_w3, hy_b3, hy_freq, hy_w_out,
           hy_bias, w_pa, w_pb, w_out, final_g):
    bsz, seq, d = x.shape
    ctx_len = ctx.shape[1]
    n_rows = seq // GRID_W
    depth = w_ada.shape[0]
    assert depth == 1, "context-token outputs are only dropped by the last layer"
    l = 0

    c_rows = jnp.concatenate([c, c_ctx[None, :], jnp.zeros((8 - bsz - 1, d), F32)], axis=0)
    mod_all = _ada_mod(c_rows, w_ada[l], b_ada[l])
    mod = mod_all[:bsz].reshape(bsz, N_SUB, N_MOD, 1, d)
    mod_c = mod_all[bsz:bsz + 1].reshape(1, N_SUB, N_MOD, 1, d)

    def mods(m, sub):
        return tuple(m[:, sub, k] for k in range(N_MOD))

    wg, wu, wd = ffn_w_gate[l], ffn_w_up[l], ffn_w_down[l]
    w_in_b = w_in[l]

    xt = x.reshape(bsz * seq, d)
    ct = ctx.reshape(bsz * ctx_len, d)

    xt = _ffn_sublayer(xt, mods(mod, 0), norm_g[l, 0], wg, wu, wd, 0)
    ct = _ffn_sublayer(ct, mods(mod_c, 0), norm_g[l, 0], wg, wu, wd, 0)

    assert GRID_W * n_rows == seq
    u_s5, us_hy, sig_gates = _in_proj(xt, mod[:, 1, 0], mod[:, 1, 1], norm_g[l, 1], w_in_b,
                                      hy_short_w[l], hy_short_b[l], n_u=I_HY, n_hy=I_GA - I_HY)
    (u_ctx,) = _in_proj(ct, mod_c[:, 1, 0], mod_c[:, 1, 1], norm_g[l, 1], w_in_b[:, :D_S5])

    y_s5 = _s5_bidirectional(u_s5.reshape(bsz, seq, D_S5), u_ctx.reshape(bsz, ctx_len, D_S5),
                             s5_lam_re[l], s5_lam_im[l], s5_log_dt[l],
                             s5_b_re[l], s5_b_im[l], s5_c_re[l], s5_c_im[l], s5_d[l])
    y_hy = _hyena(us_hy.reshape(bsz, seq, I_GA - I_HY),
                  hy_w1[l], hy_b1[l], hy_w2[l], hy_b2[l], hy_w3[l], hy_b3[l], hy_freq[l],
                  hy_w_out[l], hy_bias[l])

    xt = _merge(xt, mod[:, 1, 2], y_s5.reshape(bsz * seq, D_S5), y_hy.reshape(bsz * seq, D_HY),
                sig_gates, w_pa[l].astype(BF16), w_pb[l].astype(BF16), w_out[l].astype(BF16))

    xt = _ffn_sublayer(xt, mods(mod, 2), norm_g[l, 2], wg, wu, wd, 1, final_gain=final_g)
    return xt.reshape(bsz, seq, d)
```

```python
import functools
import math

import jax
import jax.numpy as jnp
import numpy as np
from jax import lax
from jax.experimental import pallas as pl
from jax.experimental.pallas import tpu as pltpu

F32 = jnp.float32
BF16 = jnp.bfloat16

D_MODEL = 2048
GRID_W = 64
D_S5 = 1024
S5_GROUP = 16
S5_GROUPS = D_S5 // S5_GROUP
S5_STATE = 64
S5_DIRS = 2
LAMBDA_RE_MAX = -1e-4
S5_CHUNK = 16
LANE = 128
S5_GROUPS_PER_COL = LANE // S5_GROUP
D_HY = 1024
HY_ORDER = 2
HY_DIRS = 2
HY_SHORT = 3
HY_EMB = 33
HY_BANDS = (HY_EMB - 1) // 2
HY_TARGET = 1e-2
HY_FAST_PCT = 0.3
HY_SLOW_PCT = 1.5
FFT_N1 = 64
FFT_N2 = 128
FFT_N = FFT_N1 * FFT_N2
FILTER_TILE_ROWS = FFT_N1 + 8
HY_TILE_PITCH = FILTER_TILE_ROWS
HY_PAIRS_PER_COL = LANE // 2
HY_PAIRS_PER_STEP = 16
I_HY = D_S5
I_GA = D_S5 + (HY_ORDER + 1) * D_HY
I_GB = I_GA + D_MODEL
D_IN = I_GB + D_MODEL
D_FF = 5632
N_SUB = 3
N_MOD = 3
HALF_STEP = 0.5
RMS_EPS = 1e-6

VMEM_LIMIT_BYTES = 58 * 1024 * 1024


def _rms_mod(x, gain, shift, scale):
    ms = jnp.mean(x * x, axis=-1, keepdims=True)
    y = x * lax.rsqrt(ms + RMS_EPS) * gain
    return y * (1.0 + scale) + shift


def _split_bf16(v):
    hi = v.astype(BF16)
    return hi, (v - hi.astype(F32)).astype(BF16)


def _ada_kernel(c_ref, w_ref, b_ref, o_ref):
    c = c_ref[...]
    a_hi, a_lo = _split_bf16(c * jax.nn.sigmoid(c))
    w_hi, w_lo = _split_bf16(w_ref[...])
    o_ref[...] = (jnp.dot(a_hi, w_hi, preferred_element_type=F32) + jnp.dot(a_lo, w_hi, preferred_element_type=F32)
                  + jnp.dot(a_hi, w_lo, preferred_element_type=F32) + b_ref[...])


def _ada_mod(c_rows, w, b, tn=1024):
    rows, d = c_rows.shape
    n = w.shape[1]
    return pl.pallas_call(
        _ada_kernel,
        grid=(n // tn,),
        in_specs=[pl.BlockSpec((rows, d), lambda j: (0, 0)),
                  pl.BlockSpec((d, tn), lambda j: (0, j)),
                  pl.BlockSpec((1, tn), lambda j: (0, j))],
        out_specs=pl.BlockSpec((rows, tn), lambda j: (0, j)),
        out_shape=jax.ShapeDtypeStruct((rows, n), F32),
        compiler_params=pltpu.CompilerParams(
            dimension_semantics=("arbitrary",), vmem_limit_bytes=VMEM_LIMIT_BYTES),
        name="ada_mod",
    )(c_rows, w, b.reshape(1, n))


def _ffn_kernel(x_ref, shift_ref, scale_ref, gate_ref, gain_ref, wg_ref, wu_ref, wd_ref,
                fg_ref, o_ref, h_ref, *, final_norm):
    j = pl.program_id(1)

    @pl.when(j == 0)
    def _():
        h_ref[...] = _rms_mod(x_ref[...], gain_ref[...], shift_ref[...], scale_ref[...]).astype(BF16)
        o_ref[...] = jnp.zeros_like(o_ref)

    h = h_ref[...]
    g = jnp.dot(h, wg_ref[...].astype(BF16), preferred_element_type=F32)
    u = jnp.dot(h, wu_ref[...].astype(BF16), preferred_element_type=F32)
    a = (g * jax.nn.sigmoid(g) * u).astype(BF16)
    o_ref[...] += jnp.dot(a, wd_ref[...].astype(BF16), preferred_element_type=F32)

    @pl.when(j == pl.num_programs(1) - 1)
    def _():
        y = x_ref[...] + (HALF_STEP * gate_ref[...]) * o_ref[...]
        if final_norm:
            ms = jnp.mean(y * y, axis=-1, keepdims=True)
            y = y * lax.rsqrt(ms + RMS_EPS) * fg_ref[...]
        o_ref[...] = y


def _ffn_sublayer(x, mods, gain, wg, wu, wd, which, final_gain=None, tm=1024, tf=256):
    t, d = x.shape
    bm = mods[0].shape[0]
    tm = min(tm, t // bm)
    blocks_per_batch = (t // bm) // tm
    dff = wg.shape[2]
    final_norm = final_gain is not None
    fg = final_gain if final_norm else gain
    mod_spec = pl.BlockSpec((None, 1, d), lambda i, j: (i // blocks_per_batch, 0, 0))
    vec_spec = pl.BlockSpec((1, d), lambda i, j: (0, 0))
    return pl.pallas_call(
        functools.partial(_ffn_kernel, final_norm=final_norm),
        grid=(t // tm, dff // tf),
        in_specs=[pl.BlockSpec((tm, d), lambda i, j: (i, 0)),
                  mod_spec, mod_spec, mod_spec, vec_spec,
                  pl.BlockSpec((None, d, tf), lambda i, j: (which, 0, j)),
                  pl.BlockSpec((None, d, tf), lambda i, j: (which, 0, j)),
                  pl.BlockSpec((None, tf, d), lambda i, j: (which, j, 0)),
                  vec_spec],
        out_specs=pl.BlockSpec((tm, d), lambda i, j: (i, 0)),
        out_shape=jax.ShapeDtypeStruct((t, d), F32),
        scratch_shapes=[pltpu.VMEM((tm, d), BF16)],
        compiler_params=pltpu.CompilerParams(
            dimension_semantics=("parallel", "arbitrary"), vmem_limit_bytes=VMEM_LIMIT_BYTES),
        name="ffn_final" if final_norm else "ffn",
    )(x, *mods, gain.reshape(1, d), wg, wu, wd, fg.reshape(1, d))


def _proj_kernel(x_ref, shift_ref, scale_ref, gain_ref, w_ref, sw_ref, sb_ref, *rest, n_u, n_hy, row_len, part):
    o_refs, h_ref = rest[:-1], rest[-1]
    j = pl.program_id(1)

    @pl.when(j == 0)
    def _():
        h_ref[...] = _rms_mod(x_ref[...], gain_ref[...], shift_ref[...], scale_ref[...]).astype(BF16)

    tm, tn = h_ref.shape[0], w_ref.shape[1]

    def in_parts(o_ref, epilogue):
        for c in range(0, tn, part):
            p = jnp.dot(h_ref[...], w_ref[:, c:c + part].astype(BF16), preferred_element_type=F32)
            o_ref[:, c:c + part] = epilogue(p, c).astype(o_ref.dtype)

    def short_conv(p, c):
        col = lax.broadcasted_iota(jnp.int32, p.shape, 0) % row_len
        prev = jnp.where(col == 0, 0.0, pltpu.roll(p, 1, 0))
        nxt = jnp.where(col == row_len - 1, 0.0, pltpu.roll(p, tm - 1, 0))
        sw = sw_ref[:, c:c + part]
        return sb_ref[:, c:c + part] + prev * sw[0:1] + p * sw[1:2] + nxt * sw[2:3]

    if n_hy == 0:
        in_parts(o_refs[0], lambda p, c: p)
        return

    @pl.when(j < n_u)
    def _():
        in_parts(o_refs[0], lambda p, c: p)

    @pl.when((j >= n_u) & (j < n_u + n_hy))
    def _():
        in_parts(o_refs[1], short_conv)

    @pl.when(j >= n_u + n_hy)
    def _():
        in_parts(o_refs[2], lambda p, c: jax.nn.sigmoid(p))


def _in_proj(x, shift, scale, gain, w, short_w=None, short_b=None, n_u=D_S5, n_hy=0, row_len=GRID_W,
             tm=1024, tn=512, part=256):
    t, d = x.shape
    bm = shift.shape[0]
    tm = min(tm, t // bm)
    blocks_per_batch = (t // bm) // tm
    n = w.shape[1]
    assert tm % row_len == 0 and (t // bm) % tm == 0
    bu, bh = n_u // tn, n_hy // tn
    bg = n // tn - bu - bh
    mod_spec = pl.BlockSpec((None, 1, d), lambda i, j: (i // blocks_per_batch, 0, 0))
    out_shape = [jax.ShapeDtypeStruct((t, n_u), F32)]
    out_specs = [pl.BlockSpec((tm, tn), lambda i, j: (i, jnp.minimum(j, bu - 1)))]
    if bh:
        out_shape += [jax.ShapeDtypeStruct((t, n_hy), BF16), jax.ShapeDtypeStruct((t, bg * tn), BF16)]
        out_specs += [pl.BlockSpec((tm, tn), lambda i, j: (i, jnp.clip(j - bu, 0, bh - 1))),
                      pl.BlockSpec((tm, tn), lambda i, j: (i, jnp.maximum(j - bu - bh, 0)))]
        sw, sb = short_w, short_b.reshape(1, n_hy)
        hy_blk = lambda i, j: (0, jnp.clip(j - bu, 0, bh - 1))
    else:
        sw, sb = jnp.zeros((HY_SHORT, tn), F32), jnp.zeros((1, tn), F32)
        hy_blk = lambda i, j: (0, 0)
    return pl.pallas_call(
        functools.partial(_proj_kernel, n_u=bu, n_hy=bh, row_len=row_len, part=part),
        grid=(t // tm, n // tn),
        in_specs=[pl.BlockSpec((tm, d), lambda i, j: (i, 0)),
                  mod_spec, mod_spec,
                  pl.BlockSpec((1, d), lambda i, j: (0, 0)),
                  pl.BlockSpec((d, tn), lambda i, j: (0, j)),
                  pl.BlockSpec((HY_SHORT, tn), hy_blk),
                  pl.BlockSpec((1, tn), hy_blk)],
        out_specs=out_specs,
        out_shape=out_shape,
        scratch_shapes=[pltpu.VMEM((tm, d), BF16)],
        compiler_params=pltpu.CompilerParams(
            dimension_semantics=("parallel", "arbitrary"), vmem_limit_bytes=VMEM_LIMIT_BYTES),
        name="in_proj",
    )(x, shift, scale, gain.reshape(1, d), w, sw, sb)


def _gelu_tanh(x):
    return 0.5 * x * (1.0 + jnp.tanh(math.sqrt(2.0 / math.pi) * (x + 0.044715 * (x * x * x))))


def _merge_kernel(x_ref, gate_ref, ys_ref, yh_ref, ga_ref, gb_ref, wpa_lo_ref, wpa_hi_ref, wpb_ref,
                  wout_ref, o_ref, s_ref, acc_ref):
    j = pl.program_id(1)

    @pl.when(j == 0)
    def _():
        s_ref[...] = _gelu_tanh(ys_ref[...].astype(F32)).astype(BF16)
        acc_ref[...] = jnp.zeros_like(acc_ref)

    s = s_ref[...]
    pa_lo = jnp.dot(s, wpa_lo_ref[...], preferred_element_type=F32)
    pa_hi = jnp.dot(s, wpa_hi_ref[...], preferred_element_type=F32)
    y_a = pa_lo * jax.nn.sigmoid(pa_hi)
    y_b = jnp.dot(yh_ref[...], wpb_ref[...], preferred_element_type=F32)
    m = ga_ref[...].astype(F32) * y_a + gb_ref[...].astype(F32) * y_b
    acc_ref[...] += jnp.dot(m.astype(BF16), wout_ref[...], preferred_element_type=F32)

    @pl.when(j == pl.num_programs(1) - 1)
    def _():
        o_ref[...] = x_ref[...] + gate_ref[...] * acc_ref[...]


def _merge(x, gate, y_s5, y_hy, sig_gates, w_pa, w_pb, w_out, tm=512, tn=1024):
    t, d = x.shape
    bm = gate.shape[0]
    blocks_per_batch = (t // bm) // tm
    nj = d // tn
    ds5 = y_s5.shape[1]
    dhy = y_hy.shape[1]
    return pl.pallas_call(
        _merge_kernel,
        grid=(t // tm, nj),
        in_specs=[pl.BlockSpec((tm, d), lambda i, j: (i, 0)),
                  pl.BlockSpec((None, 1, d), lambda i, j: (i // blocks_per_batch, 0, 0)),
                  pl.BlockSpec((tm, ds5), lambda i, j: (i, 0)),
                  pl.BlockSpec((tm, dhy), lambda i, j: (i, 0)),
                  pl.BlockSpec((tm, tn), lambda i, j: (i, j)),
                  pl.BlockSpec((tm, tn), lambda i, j: (i, nj + j)),
                  pl.BlockSpec((ds5, tn), lambda i, j: (0, j)),
                  pl.BlockSpec((ds5, tn), lambda i, j: (0, nj + j)),
                  pl.BlockSpec((dhy, tn), lambda i, j: (0, j)),
                  pl.BlockSpec((tn, d), lambda i, j: (j, 0))],
        out_specs=pl.BlockSpec((tm, d), lambda i, j: (i, 0)),
        out_shape=jax.ShapeDtypeStruct((t, d), F32),
        scratch_shapes=[pltpu.VMEM((tm, ds5), BF16), pltpu.VMEM((tm, d), F32)],
        compiler_params=pltpu.CompilerParams(
            dimension_semantics=("parallel", "arbitrary"), vmem_limit_bytes=VMEM_LIMIT_BYTES),
        name="merge",
    )(x, gate, y_s5, y_hy, sig_gates, sig_gates, w_pa, w_pa, w_pb, w_out)


def _s5_weights_kernel(*refs, n_steps):
    blocks, ca_ref = refs[:-1], refs[-1]

    def one(gi, carry):
        _s5_weights_group(*[r.at[gi] for r in blocks], ca_ref, n_steps=n_steps)
        return carry

    lax.fori_loop(0, blocks[0].shape[0], one, 0)


def _s5_weights_group(par_ref, bt_ref, c_ref, d_ref, bpow_ref, toep_ref, cpow_ref, tab_ref, ca_ref, *, n_steps):
    t, h, p = S5_CHUNK, S5_GROUP, S5_STATE
    lanes = 2 * p
    hp = lax.Precision.HIGHEST
    sgn = jnp.where(lax.broadcasted_iota(jnp.int32, (1, lanes), 1) < p, -1.0, 1.0)
    par = par_ref[...]
    gsum = None
    for d in range(S5_DIRS):
        lr = jnp.minimum(par[3 * d:3 * d + 1], LAMBDA_RE_MAX)
        li = par[3 * d + 1:3 * d + 2]
        dt = jnp.exp(par[3 * d + 2:3 * d + 3])
        zr, zi = lr * dt, li * dt

        def apow(j):
            mag = jnp.exp(j * zr)
            return mag * jnp.cos(j * zi), sgn * (mag * jnp.sin(j * zi))

        def cmul(x, a1, a2):
            return x * a1 + pltpu.roll(x, p, 1) * a2

        a1, a2 = apow(lax.broadcasted_iota(jnp.int32, (t + 1, 1), 0).astype(F32))
        nr, ni = a1[1:2] - 1.0, sgn * a2[1:2]
        den = lr * lr + li * li
        f_re = (nr * lr + ni * li) / den
        f_im = (ni * lr - nr * li) / den
        bbar = cmul(bt_ref[d], f_re, sgn * f_im)
        cc = c_ref[d]
        ca = [cmul(cc, a1[j:j + 1], a2[j:j + 1]) * (-sgn) for j in range(t + 1)]
        ca_ref[...] = jnp.zeros_like(ca_ref)
        for k in range(t):
            e_b, e_c = (t - 1 - k, k + 1) if d == 0 else (k, t - k)
            bpow_ref[k * h:(k + 1) * h, d * lanes:(d + 1) * lanes] = (
                cmul(bbar, a1[e_b:e_b + 1], a2[e_b:e_b + 1]).astype(BF16))
            cpow_ref[k * h:(k + 1) * h, d * lanes:(d + 1) * lanes] = ca[e_c].astype(BF16)
            l = t - 1 + k if d == 0 else t - 1 - k
            ca_ref[l * h:(l + 1) * h, :] = ca[k]
        g = lax.dot_general(bbar, ca_ref[...], (((1,), (1,)), ((), ())), preferred_element_type=F32, precision=hp)
        gsum = g if gsum is None else gsum + g
        for s in range(n_steps):
            s1, s2 = apow(float(t * 2 ** s))
            r = d * 2 * n_steps + 2 * s
            tab_ref[r:r + 1, :] = s1
            tab_ref[r + 1:r + 2, :] = s2
    wide = gsum.shape[1]
    col = lax.broadcasted_iota(jnp.int32, (h, wide), 1)
    row = lax.broadcasted_iota(jnp.int32, (h, wide), 0)
    gsum = gsum + jnp.where(col - (t - 1) * h == row, d_ref[...], 0.0)
    for k in range(t):
        off = (t - 1 - k) * h
        shifted = gsum if off == 0 else pltpu.roll(gsum, wide - off, 1)
        toep_ref[k * h:(k + 1) * h, :] = shifted[:, :t * h].astype(BF16)


def _s5_weights(lam_re, lam_im, log_dt, b_re, b_im, c_re, c_im, d_skip, n_steps):
    g, p, h, t = S5_GROUPS, S5_STATE, S5_GROUP, S5_CHUNK
    cat2 = lambda a: jnp.concatenate([a, a], axis=-1)
    par = jnp.stack([cat2(lam_re), cat2(lam_im), jnp.broadcast_to(log_dt[..., None], (S5_DIRS, g, 2 * p))], axis=1)
    par = par.transpose(2, 0, 1, 3).reshape(g, 3 * S5_DIRS, 2 * p)
    btc = jnp.concatenate([b_re, b_im], axis=2).transpose(1, 0, 3, 2)
    ccat = jnp.concatenate([c_re, c_im], axis=3).transpose(1, 0, 2, 3)
    wide = 2 * t * h
    drow = jnp.zeros((g, 1, wide), F32).at[:, 0, (t - 1) * h:t * h].set(d_skip.reshape(g, h))
    w = t * h
    gb = S5_GROUPS_PER_COL
    sq = pl.BlockSpec((gb, w, w), lambda i: (i, 0, 0))
    return pl.pallas_call(
        functools.partial(_s5_weights_kernel, n_steps=n_steps),
        grid=(g // gb,),
        in_specs=[pl.BlockSpec((gb, 3 * S5_DIRS, 2 * p), lambda i: (i, 0, 0)),
                  pl.BlockSpec((gb, S5_DIRS, h, 2 * p), lambda i: (i, 0, 0, 0)),
                  pl.BlockSpec((gb, S5_DIRS, h, 2 * p), lambda i: (i, 0, 0, 0)),
                  pl.BlockSpec((gb, 1, wide), lambda i: (i, 0, 0))],
        out_specs=[sq, sq, sq, pl.BlockSpec((gb, 4 * n_steps, 2 * p), lambda i: (i, 0, 0))],
        out_shape=[jax.ShapeDtypeStruct((g, w, w), BF16)] * 3 + [jax.ShapeDtypeStruct((g, 4 * n_steps, 2 * p), F32)],
        scratch_shapes=[pltpu.VMEM((wide, 2 * p), F32)],
        compiler_params=pltpu.CompilerParams(
            dimension_semantics=("parallel",), vmem_limit_bytes=VMEM_LIMIT_BYTES),
        name="s5_weights",
    )(par, btc, ccat, drow)


def _s5_select():
    t, h = S5_CHUNK, S5_GROUP
    sel = np.zeros((S5_GROUPS_PER_COL, t, LANE, t, h), np.float32)
    for gl in range(S5_GROUPS_PER_COL):
        for k in range(t):
            sel[gl, k, gl * h:(gl + 1) * h, k, :] = np.eye(h)
    return jnp.asarray(sel.reshape(S5_GROUPS_PER_COL, t * LANE, t * h)).astype(BF16)


def _s5_kernel(u_ref, uc_ref, bpow_ref, toep_ref, cpow_ref, tab_ref, sel_ref, y_ref, ucat_ref, ucc_ref, yacc_ref,
               *, bsz, n_steps):
    t, gw = S5_CHUNK, S5_GROUP
    per = LANE // gw
    n_lat = u_ref.shape[0] // (bsz * t)
    n_ctx = uc_ref.shape[0] // (bsz * t)
    n_ch = n_lat + n_ctx
    rows = bsz * n_ch
    half = 2 * S5_STATE
    for k in range(t):
        yacc_ref[:, k * LANE:(k + 1) * LANE] = u_ref[pl.ds(k, bsz * n_lat, stride=t), :]
        ucc_ref[:, k * LANE:(k + 1) * LANE] = uc_ref[pl.ds(k, bsz * n_ctx, stride=t), :].astype(BF16)
    ucat_ref[...] = yacc_ref[...].astype(BF16)
    yacc_ref[...] = jnp.zeros_like(yacc_ref)
    rib = lax.broadcasted_iota(jnp.int32, (rows, half), 0) % n_ch
    lane = lax.broadcasted_iota(jnp.int32, (1, LANE), 1)

    def cmul_add(acc, sh, a1, a2):
        return acc + a1 * sh + a2 * pltpu.roll(sh, S5_STATE, 1)

    def group(gl, carry):
        sel = sel_ref[gl]
        ul = jnp.dot(ucat_ref[...], sel, preferred_element_type=F32).astype(BF16)
        uc = jnp.dot(ucc_ref[...], sel, preferred_element_type=F32).astype(BF16)
        bpow = bpow_ref[gl]
        zl = jnp.dot(ul, bpow, preferred_element_type=F32)
        zc = jnp.dot(uc, bpow, preferred_element_type=F32)
        fparts, bparts = [], []
        for b in range(bsz):
            lat = slice(b * n_lat, (b + 1) * n_lat)
            ctx = slice(b * n_ctx, (b + 1) * n_ctx)
            fparts += [zc[ctx, :half], zl[lat, :half]]
            bparts += [zl[lat, half:], zc[ctx, half:]]
        fw = jnp.concatenate(fparts, axis=0)
        bw = jnp.concatenate(bparts, axis=0)
        tab = tab_ref[gl]
        for s in range(n_steps):
            d = 1 << s
            sh = jnp.where(rib >= d, pltpu.roll(fw, d, 0), 0.0)
            fw = cmul_add(fw, sh, tab[2 * s:2 * s + 1], tab[2 * s + 1:2 * s + 2])
            o = 2 * n_steps
            sh = jnp.where(rib < n_ch - d, pltpu.roll(bw, rows - d, 0), 0.0)
            bw = cmul_add(bw, sh, tab[o + 2 * s:o + 2 * s + 1], tab[o + 2 * s + 1:o + 2 * s + 2])
        fe = jnp.where(rib >= 1, pltpu.roll(fw, 1, 0), 0.0)
        be = jnp.where(rib < n_ch - 1, pltpu.roll(bw, rows - 1, 0), 0.0)
        fl = jnp.concatenate([fe[b * n_ch + n_ctx:(b + 1) * n_ch] for b in range(bsz)], axis=0)
        bl = jnp.concatenate([be[b * n_ch:b * n_ch + n_lat] for b in range(bsz)], axis=0)
        st = jnp.concatenate([fl, bl], axis=1).astype(BF16)
        y = (jnp.dot(ul, toep_ref[gl], preferred_element_type=F32)
             + lax.dot_general(st, cpow_ref[gl], (((1,), (1,)), ((), ())), preferred_element_type=F32))
        mine = (lane >= gw * gl) & (lane < gw * (gl + 1))
        for i in range(t):
            src = y[:, (i // per) * LANE:(i // per + 1) * LANE]
            r = pltpu.roll(src, (gw * gl - gw * (i % per)) % LANE, 1)
            blk = slice(i * LANE, (i + 1) * LANE)
            yacc_ref[:, blk] = jnp.where(mine, r, yacc_ref[:, blk])
        return carry

    lax.fori_loop(0, S5_GROUPS_PER_COL, group, 0)
    for i in range(t):
        y_ref[pl.ds(i, bsz * n_lat, stride=t), :] = yacc_ref[:, i * LANE:(i + 1) * LANE]


def _s5_mix(u, uc, bpw, toep, cpw, tab, bsz):
    rl, dm = u.shape
    rc = uc.shape[0]
    t = S5_CHUNK
    w = t * S5_GROUP
    n_steps = tab.shape[1] // 4
    gpc = S5_GROUPS_PER_COL
    sel = _s5_select()
    wspec = pl.BlockSpec((gpc, w, w), lambda i: (i, 0, 0))
    return pl.pallas_call(
        functools.partial(_s5_kernel, bsz=bsz, n_steps=n_steps),
        grid=(dm // LANE,),
        in_specs=[pl.BlockSpec((rl, LANE), lambda i: (0, i)),
                  pl.BlockSpec((rc, LANE), lambda i: (0, i)),
                  wspec, wspec, wspec,
                  pl.BlockSpec((gpc, 4 * n_steps, tab.shape[2]), lambda i: (i, 0, 0)),
                  pl.BlockSpec(sel.shape, lambda i: (0, 0, 0))],
        out_specs=pl.BlockSpec((rl, LANE), lambda i: (0, i)),
        out_shape=jax.ShapeDtypeStruct(u.shape, F32),
        scratch_shapes=[pltpu.VMEM((rl // t, t * LANE), BF16), pltpu.VMEM((rc // t, t * LANE), BF16),
                        pltpu.VMEM((rl // t, t * LANE), F32)],
        compiler_params=pltpu.CompilerParams(
            dimension_semantics=("parallel",), vmem_limit_bytes=VMEM_LIMIT_BYTES),
        name="s5_mix",
    )(u, uc, bpw, toep, cpw, tab, sel)


def _s5_bidirectional(u, u_ctx, lam_re, lam_im, log_dt, b_re, b_im, c_re, c_im, d_skip):
    bsz, length, dm = u.shape
    ctx_len = u_ctx.shape[1]
    t = S5_CHUNK
    n_lat, n_ctx = length // t, ctx_len // t
    n_steps = max(1, math.ceil(math.log2(n_lat + n_ctx)))
    bpw, toep, cpw, tab = _s5_weights(lam_re, lam_im, log_dt, b_re, b_im, c_re, c_im, d_skip, n_steps)
    y = _s5_mix(u.reshape(bsz * length, dm), u_ctx.reshape(bsz * ctx_len, dm), bpw, toep, cpw, tab, bsz)
    return y.reshape(bsz, length, dm)


def _dft_constants(real_input=False):
    n1 = np.arange(FFT_N1)
    n2 = np.arange(FFT_N2)
    half = FFT_N1 // 2
    th = 2 * np.pi * np.outer(n1, n1) / FFT_N1
    c1, s1 = np.cos(th), np.sin(th)
    if real_input:
        w1 = np.concatenate([c1, -s1], axis=1)
    else:
        w1 = np.concatenate([np.concatenate([c1[:half], -s1[:half]], axis=1),
                             np.concatenate([s1[:half], c1[:half]], axis=1)], axis=0)
    z = np.zeros_like(w1)
    w1p = np.block([[w1, z], [z, w1]])
    ph = 2 * np.pi * np.outer(n2, n1) / FFT_N
    ps = 2 * np.pi * np.outer(n2, n2) / FFT_N2
    f2 = np.concatenate([np.cos(ps), -np.sin(ps)], axis=1)
    g2 = np.concatenate([np.cos(ps), np.sin(ps)], axis=1)
    c2, s2 = np.cos(ph).T, np.sin(ph).T
    wi = np.concatenate([np.concatenate([c1[:, :half], s1[:, :half]], axis=1),
                         np.concatenate([-s1[:, :half], c1[:, :half]], axis=1)], axis=0) / FFT_N
    zi = np.zeros_like(wi)
    wi2 = np.stack([np.concatenate([wi, zi], axis=1), np.concatenate([zi, wi], axis=1)])
    as_b = lambda a: jnp.asarray(a, F32).astype(BF16)
    as_f = lambda a: jnp.asarray(a, F32)
    return [as_b(w1p), as_f(c2), as_f(s2), as_b(f2), as_b(g2), as_b(wi2)]


def _fwd_spectrum(xp, w1p, c2, s2, f2):
    cp = xp.shape[0]
    hn = FFT_N1
    a = jnp.dot(xp.reshape(cp * FFT_N2, LANE).astype(BF16), w1p, preferred_element_type=F32)
    out = []
    for par in range(2):
        ap = a[:, par * LANE:(par + 1) * LANE].reshape(cp, FFT_N2, LANE)
        at = jnp.swapaxes(ap, 1, 2)
        are, aim = at[:, :hn], at[:, hn:]
        at = jnp.concatenate([are * c2 + aim * s2, aim * c2 - are * s2], axis=1)
        p = jnp.dot(at.reshape(cp * 2 * hn, FFT_N2).astype(BF16), f2, preferred_element_type=F32)
        p = p.reshape(cp, 2 * hn, 2 * FFT_N2)
        out.append((p[:, :hn, :FFT_N2] - p[:, hn:, FFT_N2:], p[:, :hn, FFT_N2:] + p[:, hn:, :FFT_N2]))
    return out


def _inv_time(yre, yim, g2, c2, s2, wi_par):
    cp = yre.shape[0]
    hn = FFT_N1
    y = jnp.concatenate([yre, yim], axis=1).reshape(cp * 2 * hn, FFT_N2).astype(BF16)
    q = jnp.dot(y, g2, preferred_element_type=F32).reshape(cp, 2 * hn, 2 * FFT_N2)
    bre = q[:, :hn, :FFT_N2] - q[:, hn:, FFT_N2:]
    bim = q[:, :hn, FFT_N2:] + q[:, hn:, :FFT_N2]
    b2 = jnp.concatenate([bre * c2 - bim * s2, bre * s2 + bim * c2], axis=1)
    bt = jnp.swapaxes(b2, 1, 2)
    return jnp.dot(bt.reshape(cp * FFT_N2, LANE).astype(BF16), wi_par, preferred_element_type=F32)


def _hyena_kernel(z_ref, g1_ref, g2_ref, kf_ref, bias_ref, w1p_ref, c2_ref, s2_ref, f2_ref, gi_ref, wi_ref,
                  o_ref, zt_ref, g1t_ref, g2t_ref, ot_ref, stage_ref):
    k = pl.program_id(1)
    n_s = z_ref.shape[0] // FFT_N2
    cp = HY_PAIRS_PER_STEP

    def to_tiles(x_ref, t_ref):
        for s in range(n_s):
            xs = x_ref[s * FFT_N2:(s + 1) * FFT_N2, :].astype(F32)
            stage_ref[pl.ds(s, LANE, stride=HY_TILE_PITCH), :] = xs.T
        st = stage_ref[...].reshape(HY_PAIRS_PER_COL, 2 * HY_TILE_PITCH, FFT_N2)
        both = jnp.concatenate([st[:, :n_s], st[:, HY_TILE_PITCH:HY_TILE_PITCH + n_s]], axis=1)
        t_ref[...] = jnp.swapaxes(both, 1, 2)

    @pl.when(k == 0)
    def _():
        to_tiles(z_ref, zt_ref)
        to_tiles(g1_ref, g1t_ref)
        to_tiles(g2_ref, g2t_ref)

    sl = pl.ds(pl.multiple_of(k * cp, cp), cp)
    z = zt_ref[sl]
    gates = (g1t_ref, g2t_ref)
    for o in range(HY_ORDER):
        spec = _fwd_spectrum(z, w1p_ref[...], c2_ref[...], s2_ref[...], f2_ref[...])
        conv = None
        for par in range(2):
            xre, xim = spec[par]
            kre = kf_ref[o, par, :, :FFT_N1, :]
            kim = kf_ref[o, par, :, FFT_N1:, :]
            part = _inv_time(xre * kre - xim * kim, xre * kim + xim * kre,
                             gi_ref[...], c2_ref[...], s2_ref[...], wi_ref[par])
            conv = part if conv is None else conv + part
        z = gates[o][sl] * (conv.reshape(cp, FFT_N2, LANE) + bias_ref[o] * z)
    ot_ref[sl] = z

    @pl.when(k == pl.num_programs(1) - 1)
    def _():
        back = jnp.swapaxes(ot_ref[...], 1, 2)
        for c in range(LANE):
            stage_ref[c * HY_TILE_PITCH:c * HY_TILE_PITCH + n_s, :] = back[c // 2, (c % 2) * n_s:(c % 2 + 1) * n_s, :]
        for s in range(n_s):
            rows = stage_ref[pl.ds(s, LANE, stride=HY_TILE_PITCH), :]
            o_ref[s * FFT_N2:(s + 1) * FFT_N2, :] = rows.T.astype(o_ref.dtype)


def _hyena_conv(us, kf, biasp):
    t, _ = us.shape
    consts = _dft_constants()
    ncol = D_HY // LANE
    nsub = HY_PAIRS_PER_COL // HY_PAIRS_PER_STEP
    full = lambda a: pl.BlockSpec(a.shape, lambda j, k: (0,) * a.ndim)
    nat = lambda off: pl.BlockSpec((t, LANE), lambda j, k: (0, off + j))
    tiles = pltpu.VMEM((HY_PAIRS_PER_COL, FFT_N2, LANE), F32)
    return pl.pallas_call(
        _hyena_kernel,
        grid=(ncol, nsub),
        in_specs=[nat(0), nat(ncol), nat(2 * ncol),
                  pl.BlockSpec((HY_ORDER, 2, HY_PAIRS_PER_STEP, 2 * FFT_N1, FFT_N2),
                               lambda j, k: (0, 0, j * nsub + k, 0, 0)),
                  pl.BlockSpec((HY_ORDER, HY_PAIRS_PER_STEP, 1, LANE), lambda j, k: (0, j * nsub + k, 0, 0))]
                 + [full(a) for a in consts],
        out_specs=pl.BlockSpec((t, LANE), lambda j, k: (0, j)),
        out_shape=jax.ShapeDtypeStruct((t, D_HY), us.dtype),
        scratch_shapes=[tiles, tiles, tiles, tiles, pltpu.VMEM((LANE * HY_TILE_PITCH, FFT_N2), F32)],
        compiler_params=pltpu.CompilerParams(
            dimension_semantics=("parallel", "arbitrary"), vmem_limit_bytes=VMEM_LIMIT_BYTES),
        name="hyena_conv",
    )(us, us, us, kf, biasp, *consts)


def _filter_time_kernel(w1t_ref, w1c_ref, w1s_ref, b1_ref, w2_ref, b2_ref, w3_ref, b3_ref, fr_ref,
                        wf_ref, wb_ref, df_ref, db_ref, o_ref, h_ref, k_ref, *, length):
    n_fft = 2 * length
    hp = lax.Precision.HIGHEST

    @pl.when(pl.program_id(0) == 0)
    def _():
        pos = lax.broadcasted_iota(jnp.int32, (1, n_fft), 1)
        lag = jnp.where(pos < length, pos, n_fft - pos).astype(F32)
        t = lag / float(length - 1)
        w = (2.0 * math.pi / length) * lag
        band_step = (HY_BANDS - 1 - 1e-4) / (HY_BANDS - 1)
        bands = 1e-4 + band_step * lax.broadcasted_iota(jnp.int32, (HY_BANDS, 1), 0).astype(F32)
        ang = bands * w
        fr = fr_ref[...]
        h = (w1t_ref[...] * t + jnp.dot(w1c_ref[...], jnp.cos(ang), preferred_element_type=F32, precision=hp)
             - jnp.dot(w1s_ref[...], jnp.sin(ang), preferred_element_type=F32, precision=hp))
        h = jnp.sin(fr * (h + b1_ref[...]))
        h = jnp.sin(fr * (jnp.dot(w2_ref[...], h, preferred_element_type=F32, precision=hp) + b2_ref[...]))
        h = jnp.sin(fr * (jnp.dot(w3_ref[...], h, preferred_element_type=F32, precision=hp) + b3_ref[...]))
        hi = h.astype(BF16)
        h_ref[0] = hi
        h_ref[1] = (h - hi.astype(F32)).astype(BF16)

    def dot3(w, lo, hi_):
        w_hi = w.astype(BF16)
        w_lo = (w - w_hi.astype(F32)).astype(BF16)
        h_hi, h_lo = h_ref[0, :, lo:hi_], h_ref[1, :, lo:hi_]
        return (jnp.dot(w_hi, h_hi, preferred_element_type=F32) + jnp.dot(w_hi, h_lo, preferred_element_type=F32)
                + jnp.dot(w_lo, h_hi, preferred_element_type=F32))

    pos = lax.broadcasted_iota(jnp.int32, (1, length), 1)
    tf = pos.astype(F32) / float(length - 1)
    tb = (length - pos).astype(F32) / float(length - 1)
    kf = dot3(wf_ref[...], 0, length) * jnp.exp(-tf * df_ref[...])
    kb = dot3(wb_ref[...], length, n_fft) * jnp.exp(-tb * db_ref[...])
    kb = jnp.where(pos == 0, 0.0, kb)
    inv = 1.0 / (jnp.sum(jnp.abs(kf), axis=1, keepdims=True) + jnp.sum(jnp.abs(kb), axis=1, keepdims=True))
    k_ref[:, :length] = kf * inv
    k_ref[:, length:] = kb * inv
    cb = wf_ref.shape[0]
    for n1 in range(FILTER_TILE_ROWS):
        row = k_ref[:, n1 * FFT_N2:(n1 + 1) * FFT_N2] if n1 < FFT_N1 else jnp.zeros((cb, FFT_N2), F32)
        o_ref[pl.ds(n1, cb, stride=FILTER_TILE_ROWS), :] = row


def _filter_time(length, w1, b1, w2, b2, w3, b3, freq, w_out, cb=128):
    col = lambda v: v.reshape(-1, 1)
    w1t = w1.T
    n_ch = w_out.shape[1]
    deltas = jnp.abs(jnp.linspace(math.log(HY_TARGET) / HY_SLOW_PCT, math.log(HY_TARGET) / HY_FAST_PCT,
                                  n_ch, dtype=F32)).reshape(n_ch, 1)
    wot = w_out.T
    nb = D_HY // cb
    small = lambda a: pl.BlockSpec(a.shape, lambda i: (0,) * a.ndim)
    fwd = lambda i: ((i // nb) * HY_DIRS * nb + i % nb, 0)
    bwd = lambda i: ((i // nb) * HY_DIRS * nb + nb + i % nb, 0)
    ins = [w1t[:, 0:1], w1t[:, 1:1 + HY_BANDS], w1t[:, 1 + HY_BANDS:], col(b1), w2.T, col(b2), w3.T, col(b3),
           col(freq)]
    hy_ff = w2.shape[0]
    return pl.pallas_call(
        functools.partial(_filter_time_kernel, length=length),
        grid=(HY_ORDER * nb,),
        in_specs=[small(a) for a in ins] + [pl.BlockSpec((cb, hy_ff), fwd), pl.BlockSpec((cb, hy_ff), bwd),
                                            pl.BlockSpec((cb, 1), fwd), pl.BlockSpec((cb, 1), bwd)],
        out_specs=pl.BlockSpec((cb * FILTER_TILE_ROWS, FFT_N2), lambda i: (i, 0)),
        out_shape=jax.ShapeDtypeStruct((HY_ORDER * D_HY * FILTER_TILE_ROWS, FFT_N2), F32),
        scratch_shapes=[pltpu.VMEM((2, hy_ff, 2 * length), BF16), pltpu.VMEM((cb, 2 * length), F32)],
        compiler_params=pltpu.CompilerParams(
            dimension_semantics=("arbitrary",), vmem_limit_bytes=VMEM_LIMIT_BYTES),
        name="hyena_filter_time",
    )(*ins, wot, wot, deltas, deltas)


def _filter_spec_kernel(k_ref, w1p_ref, c2_ref, s2_ref, f2_ref, o_ref):
    cb = k_ref.shape[0] // FILTER_TILE_ROWS
    kt = k_ref[...].reshape(cb, FILTER_TILE_ROWS, FFT_N2)[:, :FFT_N1, :]
    xp = jnp.swapaxes(kt.reshape(cb // 2, 2 * FFT_N1, FFT_N2), 1, 2)
    spec = _fwd_spectrum(xp, w1p_ref[...], c2_ref[...], s2_ref[...], f2_ref[...])
    for par in range(2):
        o_ref[par, :, :FFT_N1, :] = spec[par][0]
        o_ref[par, :, FFT_N1:, :] = spec[par][1]


def _filter_spectrum(kt, cb=128):
    consts = _dft_constants(real_input=True)[:4]
    nb = D_HY // cb
    full = lambda a: pl.BlockSpec(a.shape, lambda i: (0,) * a.ndim)
    return pl.pallas_call(
        _filter_spec_kernel,
        grid=(HY_ORDER * nb,),
        in_specs=[pl.BlockSpec((cb * FILTER_TILE_ROWS, FFT_N2), lambda i: (i, 0))] + [full(a) for a in consts],
        out_specs=pl.BlockSpec((None, 2, cb // 2, 2 * FFT_N1, FFT_N2), lambda i: (i // nb, 0, i % nb, 0, 0)),
        out_shape=jax.ShapeDtypeStruct((HY_ORDER, 2, D_HY // 2, 2 * FFT_N1, FFT_N2), F32),
        compiler_params=pltpu.CompilerParams(
            dimension_semantics=("parallel",), vmem_limit_bytes=VMEM_LIMIT_BYTES),
        name="hyena_filter_spectrum",
    )(kt, *consts)


def _hyena(us, w1, b1, w2, b2, w3, b3, freq, w_out, bias):
    bsz, length, _ = us.shape
    assert 2 * length == FFT_N and bsz == 2, "one complex transform carries exactly two batch rows"
    kf = _filter_spectrum(_filter_time(length, w1, b1, w2, b2, w3, b3, freq, w_out))
    biasp = jnp.repeat(bias.reshape(HY_ORDER, D_HY // 2, 1, 2), FFT_N1, axis=-1)
    return _hyena_conv(us.reshape(bsz * length, -1), kf, biasp).reshape(bsz, length, D_HY)


def kernel(x, c, ctx, c_ctx, w_ada, b_ada, norm_g, ffn_w_gate, ffn_w_up, ffn_w_down, w_in,
           s5_lam_re, s5_lam_im, s5_log_dt, s5_b_re, s5_b_im, s5_c_re, s5_c_im, s5_d,
           hy_short_w, hy_short_b, hy_w1, hy_b1, hy_w2, hy_b2, hy_w3, hy_b3, hy_freq, hy_w_out,
           hy_bias, w_pa, w_pb, w_out, final_g):
    bsz, seq, d = x.shape
    ctx_len = ctx.shape[1]
    n_rows = seq // GRID_W
    depth = w_ada.shape[0]
    assert depth == 1, "context-token outputs are only dropped by the last layer"
    l = 0

    c_rows = jnp.concatenate([c, c_ctx[None, :], jnp.zeros((8 - bsz - 1, d), F32)], axis=0)
    mod_all = _ada_mod(c_rows, w_ada[l], b_ada[l])
    mod = mod_all[:bsz].reshape(bsz, N_SUB, N_MOD, 1, d)
    mod_c = mod_all[bsz:bsz + 1].reshape(1, N_SUB, N_MOD, 1, d)

    def mods(m, sub):
        return tuple(m[:, sub, k] for k in range(N_MOD))

    wg, wu, wd = ffn_w_gate[l], ffn_w_up[l], ffn_w_down[l]
    w_in_b = w_in[l]

    xt = x.reshape(bsz * seq, d)
    ct = ctx.reshape(bsz * ctx_len, d)

    xt = _ffn_sublayer(xt, mods(mod, 0), norm_g[l, 0], wg, wu, wd, 0)
    ct = _ffn_sublayer(ct, mods(mod_c, 0), norm_g[l, 0], wg, wu, wd, 0)

    assert GRID_W * n_rows == seq
    u_s5, us_hy, sig_gates = _in_proj(xt, mod[:, 1, 0], mod[:, 1, 1], norm_g[l, 1], w_in_b,
                                      hy_short_w[l], hy_short_b[l], n_u=I_HY, n_hy=I_GA - I_HY)
    (u_ctx,) = _in_proj(ct, mod_c[:, 1, 0], mod_c[:, 1, 1], norm_g[l, 1], w_in_b[:, :D_S5])

    y_s5 = _s5_bidirectional(u_s5.reshape(bsz, seq, D_S5), u_ctx.reshape(bsz, ctx_len, D_S5),
                             s5_lam_re[l], s5_lam_im[l], s5_log_dt[l],
                             s5_b_re[l], s5_b_im[l], s5_c_re[l], s5_c_im[l], s5_d[l])
    y_hy = _hyena(us_hy.reshape(bsz, seq, I_GA - I_HY),
                  hy_w1[l], hy_b1[l], hy_w2[l], hy_b2[l], hy_w3[l], hy_b3[l], hy_freq[l],
                  hy_w_out[l], hy_bias[l])

    xt = _merge(xt, mod[:, 1, 2], y_s5.reshape(bsz * seq, D_S5), y_hy.reshape(bsz * seq, D_HY),
                sig_gates, w_pa[l].astype(BF16), w_pb[l].astype(BF16), w_out[l].astype(BF16))

    xt = _ffn_sublayer(xt, mods(mod, 2), norm_g[l, 2], wg, wu, wd, 1, final_gain=final_g)
    return xt.reshape(bsz, seq, d)
```

```python
import functools
import math

import jax
import jax.numpy as jnp
import numpy as np
from jax import lax
from jax.experimental import pallas as pl
from jax.experimental.pallas import tpu as pltpu

F32 = jnp.float32
BF16 = jnp.bfloat16

D_MODEL = 2048
GRID_W = 64
D_S5 = 1024
S5_GROUP = 16
S5_GROUPS = D_S5 // S5_GROUP
S5_STATE = 64
S5_DIRS = 2
LAMBDA_RE_MAX = -1e-4
S5_CHUNK = 16
LANE = 128
S5_GROUPS_PER_COL = LANE // S5_GROUP
D_HY = 1024
HY_ORDER = 2
HY_DIRS = 2
HY_SHORT = 3
HY_EMB = 33
HY_BANDS = (HY_EMB - 1) // 2
HY_TARGET = 1e-2
HY_FAST_PCT = 0.3
HY_SLOW_PCT = 1.5
FFT_N1 = 64
FFT_N2 = 128
FFT_N = FFT_N1 * FFT_N2
FILTER_TILE_ROWS = FFT_N1 + 8
HY_TILE_PITCH = FILTER_TILE_ROWS
HY_PAIRS_PER_COL = LANE // 2
HY_PAIRS_PER_STEP = 16
I_HY = D_S5
I_GA = D_S5 + (HY_ORDER + 1) * D_HY
I_GB = I_GA + D_MODEL
D_IN = I_GB + D_MODEL
D_FF = 5632
N_SUB = 3
N_MOD = 3
HALF_STEP = 0.5
RMS_EPS = 1e-6

VMEM_LIMIT_BYTES = 58 * 1024 * 1024


def _rms_mod(x, gain, shift, scale):
    ms = jnp.mean(x * x, axis=-1, keepdims=True)
    y = x * lax.rsqrt(ms + RMS_EPS) * gain
    return y * (1.0 + scale) + shift


def _split_bf16(v):
    hi = v.astype(BF16)
    return hi, (v - hi.astype(F32)).astype(BF16)


def _ada_kernel(c_ref, w_ref, b_ref, o_ref):
    c = c_ref[...]
    a_hi, a_lo = _split_bf16(c * jax.nn.sigmoid(c))
    w_hi, w_lo = _split_bf16(w_ref[...])
    o_ref[...] = (jnp.dot(a_hi, w_hi, preferred_element_type=F32) + jnp.dot(a_lo, w_hi, preferred_element_type=F32)
                  + jnp.dot(a_hi, w_lo, preferred_element_type=F32) + b_ref[...])


def _ada_mod(c_rows, w, b, tn=1024):
    rows, d = c_rows.shape
    n = w.shape[1]
    return pl.pallas_call(
        _ada_kernel,
        grid=(n // tn,),
        in_specs=[pl.BlockSpec((rows, d), lambda j: (0, 0)),
                  pl.BlockSpec((d, tn), lambda j: (0, j)),
                  pl.BlockSpec((1, tn), lambda j: (0, j))],
        out_specs=pl.BlockSpec((rows, tn), lambda j: (0, j)),
        out_shape=jax.ShapeDtypeStruct((rows, n), F32),
        compiler_params=pltpu.CompilerParams(
            dimension_semantics=("arbitrary",), vmem_limit_bytes=VMEM_LIMIT_BYTES),
        name="ada_mod",
    )(c_rows, w, b.reshape(1, n))


def _ffn_kernel(x_ref, shift_ref, scale_ref, gate_ref, gain_ref, wg_ref, wu_ref, wd_ref,
                fg_ref, o_ref, h_ref, *, final_norm):
    j = pl.program_id(1)
    tm = x_ref.shape[0]

    def swiglu_down(h):
        g = jnp.dot(h, wg_ref[...].astype(BF16), preferred_element_type=F32)
        u = jnp.dot(h, wu_ref[...].astype(BF16), preferred_element_type=F32)
        a = (g * jax.nn.sigmoid(g) * u).astype(BF16)
        return jnp.dot(a, wd_ref[...].astype(BF16), preferred_element_type=F32)

    @pl.when(j == 0)
    def _():
        for r in range(0, tm, tm // 2):
            rows = slice(r, r + tm // 2)
            h = _rms_mod(x_ref[rows, :], gain_ref[...], shift_ref[...], scale_ref[...]).astype(BF16)
            h_ref[rows, :] = h
            o_ref[rows, :] = swiglu_down(h)

    @pl.when(j > 0)
    def _():
        o_ref[...] += swiglu_down(h_ref[...])

    @pl.when(j == pl.num_programs(1) - 1)
    def _():
        y = x_ref[...] + (HALF_STEP * gate_ref[...]) * o_ref[...]
        if final_norm:
            ms = jnp.mean(y * y, axis=-1, keepdims=True)
            y = y * lax.rsqrt(ms + RMS_EPS) * fg_ref[...]
        o_ref[...] = y


def _ffn_sublayer(x, mods, gain, wg, wu, wd, which, final_gain=None, tm=1024, tf=256):
    t, d = x.shape
    bm = mods[0].shape[0]
    tm = min(tm, t // bm)
    blocks_per_batch = (t // bm) // tm
    dff = wg.shape[2]
    final_norm = final_gain is not None
    fg = final_gain if final_norm else gain
    mod_spec = pl.BlockSpec((None, 1, d), lambda i, j: (i // blocks_per_batch, 0, 0))
    vec_spec = pl.BlockSpec((1, d), lambda i, j: (0, 0))
    return pl.pallas_call(
        functools.partial(_ffn_kernel, final_norm=final_norm),
        grid=(t // tm, dff // tf),
        in_specs=[pl.BlockSpec((tm, d), lambda i, j: (i, 0)),
                  mod_spec, mod_spec, mod_spec, vec_spec,
                  pl.BlockSpec((None, d, tf), lambda i, j: (which, 0, j)),
                  pl.BlockSpec((None, d, tf), lambda i, j: (which, 0, j)),
                  pl.BlockSpec((None, tf, d), lambda i, j: (which, j, 0)),
                  vec_spec],
        out_specs=pl.BlockSpec((tm, d), lambda i, j: (i, 0)),
        out_shape=jax.ShapeDtypeStruct((t, d), F32),
        scratch_shapes=[pltpu.VMEM((tm, d), BF16)],
        compiler_params=pltpu.CompilerParams(
            dimension_semantics=("parallel", "arbitrary"), vmem_limit_bytes=VMEM_LIMIT_BYTES),
        name="ffn_final" if final_norm else "ffn",
    )(x, *mods, gain.reshape(1, d), wg, wu, wd, fg.reshape(1, d))


def _proj_kernel(x_ref, shift_ref, scale_ref, gain_ref, w_ref, sw_ref, sb_ref, *rest, n_u, n_hy, row_len, part):
    o_refs, h_ref = rest[:-1], rest[-1]
    j = pl.program_id(1)

    @pl.when(j == 0)
    def _():
        h_ref[...] = _rms_mod(x_ref[...], gain_ref[...], shift_ref[...], scale_ref[...]).astype(BF16)

    tm, tn = h_ref.shape[0], w_ref.shape[1]

    def in_parts(o_ref, epilogue):
        for c in range(0, tn, part):
            p = jnp.dot(h_ref[...], w_ref[:, c:c + part].astype(BF16), preferred_element_type=F32)
            o_ref[:, c:c + part] = epilogue(p, c).astype(o_ref.dtype)

    def short_conv(p, c):
        col = lax.broadcasted_iota(jnp.int32, p.shape, 0) % row_len
        prev = jnp.where(col == 0, 0.0, pltpu.roll(p, 1, 0))
        nxt = jnp.where(col == row_len - 1, 0.0, pltpu.roll(p, tm - 1, 0))
        sw = sw_ref[:, c:c + part]
        return sb_ref[:, c:c + part] + prev * sw[0:1] + p * sw[1:2] + nxt * sw[2:3]

    if n_hy == 0:
        in_parts(o_refs[0], lambda p, c: p)
        return

    @pl.when(j < n_u)
    def _():
        in_parts(o_refs[0], lambda p, c: p)

    @pl.when((j >= n_u) & (j < n_u + n_hy))
    def _():
        in_parts(o_refs[1], short_conv)

    @pl.when(j >= n_u + n_hy)
    def _():
        in_parts(o_refs[2], lambda p, c: jax.nn.sigmoid(p))


def _in_proj(x, shift, scale, gain, w, short_w=None, short_b=None, n_u=D_S5, n_hy=0, row_len=GRID_W,
             tm=1024, tn=512, part=256):
    t, d = x.shape
    bm = shift.shape[0]
    tm = min(tm, t // bm)
    blocks_per_batch = (t // bm) // tm
    n = w.shape[1]
    assert tm % row_len == 0 and (t // bm) % tm == 0
    bu, bh = n_u // tn, n_hy // tn
    bg = n // tn - bu - bh
    mod_spec = pl.BlockSpec((None, 1, d), lambda i, j: (i // blocks_per_batch, 0, 0))
    out_shape = [jax.ShapeDtypeStruct((t, n_u), F32)]
    out_specs = [pl.BlockSpec((tm, tn), lambda i, j: (i, jnp.minimum(j, bu - 1)))]
    if bh:
        out_shape += [jax.ShapeDtypeStruct((t, n_hy), BF16), jax.ShapeDtypeStruct((t, bg * tn), BF16)]
        out_specs += [pl.BlockSpec((tm, tn), lambda i, j: (i, jnp.clip(j - bu, 0, bh - 1))),
                      pl.BlockSpec((tm, tn), lambda i, j: (i, jnp.maximum(j - bu - bh, 0)))]
        sw, sb = short_w, short_b.reshape(1, n_hy)
        hy_blk = lambda i, j: (0, jnp.clip(j - bu, 0, bh - 1))
    else:
        sw, sb = jnp.zeros((HY_SHORT, tn), F32), jnp.zeros((1, tn), F32)
        hy_blk = lambda i, j: (0, 0)
    return pl.pallas_call(
        functools.partial(_proj_kernel, n_u=bu, n_hy=bh, row_len=row_len, part=part),
        grid=(t // tm, n // tn),
        in_specs=[pl.BlockSpec((tm, d), lambda i, j: (i, 0)),
                  mod_spec, mod_spec,
                  pl.BlockSpec((1, d), lambda i, j: (0, 0)),
                  pl.BlockSpec((d, tn), lambda i, j: (0, j)),
                  pl.BlockSpec((HY_SHORT, tn), hy_blk),
                  pl.BlockSpec((1, tn), hy_blk)],
        out_specs=out_specs,
        out_shape=out_shape,
        scratch_shapes=[pltpu.VMEM((tm, d), BF16)],
        compiler_params=pltpu.CompilerParams(
            dimension_semantics=("parallel", "arbitrary"), vmem_limit_bytes=VMEM_LIMIT_BYTES),
        name="in_proj",
    )(x, shift, scale, gain.reshape(1, d), w, sw, sb)


def _gelu_tanh(x):
    return 0.5 * x * (1.0 + jnp.tanh(math.sqrt(2.0 / math.pi) * (x + 0.044715 * (x * x * x))))


def _merge_kernel(x_ref, gate_ref, ys_ref, yh_ref, ga_ref, gb_ref, wpa_lo_ref, wpa_hi_ref, wpb_ref,
                  wout_ref, o_ref, s_ref, acc_ref):
    j = pl.program_id(1)

    @pl.when(j == 0)
    def _():
        s_ref[...] = _gelu_tanh(ys_ref[...].astype(F32)).astype(BF16)
        acc_ref[...] = jnp.zeros_like(acc_ref)

    s = s_ref[...]
    pa_lo = jnp.dot(s, wpa_lo_ref[...], preferred_element_type=F32)
    pa_hi = jnp.dot(s, wpa_hi_ref[...], preferred_element_type=F32)
    y_a = pa_lo * jax.nn.sigmoid(pa_hi)
    y_b = jnp.dot(yh_ref[...], wpb_ref[...], preferred_element_type=F32)
    m = ga_ref[...].astype(F32) * y_a + gb_ref[...].astype(F32) * y_b
    acc_ref[...] += jnp.dot(m.astype(BF16), wout_ref[...], preferred_element_type=F32)

    @pl.when(j == pl.num_programs(1) - 1)
    def _():
        o_ref[...] = x_ref[...] + gate_ref[...] * acc_ref[...]


def _merge(x, gate, y_s5, y_hy, sig_gates, w_pa, w_pb, w_out, tm=512, tn=1024):
    t, d = x.shape
    bm = gate.shape[0]
    blocks_per_batch = (t // bm) // tm
    nj = d // tn
    ds5 = y_s5.shape[1]
    dhy = y_hy.shape[1]
    return pl.pallas_call(
        _merge_kernel,
        grid=(t // tm, nj),
        in_specs=[pl.BlockSpec((tm, d), lambda i, j: (i, 0)),
                  pl.BlockSpec((None, 1, d), lambda i, j: (i // blocks_per_batch, 0, 0)),
                  pl.BlockSpec((tm, ds5), lambda i, j: (i, 0)),
                  pl.BlockSpec((tm, dhy), lambda i, j: (i, 0)),
                  pl.BlockSpec((tm, tn), lambda i, j: (i, j)),
                  pl.BlockSpec((tm, tn), lambda i, j: (i, nj + j)),
                  pl.BlockSpec((ds5, tn), lambda i, j: (0, j)),
                  pl.BlockSpec((ds5, tn), lambda i, j: (0, nj + j)),
                  pl.BlockSpec((dhy, tn), lambda i, j: (0, j)),
                  pl.BlockSpec((tn, d), lambda i, j: (j, 0))],
        out_specs=pl.BlockSpec((tm, d), lambda i, j: (i, 0)),
        out_shape=jax.ShapeDtypeStruct((t, d), F32),
        scratch_shapes=[pltpu.VMEM((tm, ds5), BF16), pltpu.VMEM((tm, d), F32)],
        compiler_params=pltpu.CompilerParams(
            dimension_semantics=("parallel", "arbitrary"), vmem_limit_bytes=VMEM_LIMIT_BYTES),
        name="merge",
    )(x, gate, y_s5, y_hy, sig_gates, sig_gates, w_pa, w_pa, w_pb, w_out)


def _s5_weights_kernel(*refs, n_steps):
    blocks, ca_ref = refs[:-1], refs[-1]

    def one(gi, carry):
        _s5_weights_group(*[r.at[gi] for r in blocks], ca_ref, n_steps=n_steps)
        return carry

    lax.fori_loop(0, blocks[0].shape[0], one, 0)


def _s5_weights_group(par_ref, bt_ref, c_ref, d_ref, bpow_ref, toep_ref, cpow_ref, tab_ref, ca_ref, *, n_steps):
    t, h, p = S5_CHUNK, S5_GROUP, S5_STATE
    lanes = 2 * p
    hp = lax.Precision.HIGHEST
    sgn = jnp.where(lax.broadcasted_iota(jnp.int32, (1, lanes), 1) < p, -1.0, 1.0)
    par = par_ref[...]
    gsum = None
    for d in range(S5_DIRS):
        lr = jnp.minimum(par[3 * d:3 * d + 1], LAMBDA_RE_MAX)
        li = par[3 * d + 1:3 * d + 2]
        dt = jnp.exp(par[3 * d + 2:3 * d + 3])
        zr, zi = lr * dt, li * dt

        def apow(j):
            mag = jnp.exp(j * zr)
            return mag * jnp.cos(j * zi), sgn * (mag * jnp.sin(j * zi))

        def cmul(x, a1, a2):
            return x * a1 + pltpu.roll(x, p, 1) * a2

        a1, a2 = apow(lax.broadcasted_iota(jnp.int32, (t + 1, 1), 0).astype(F32))
        nr, ni = a1[1:2] - 1.0, sgn * a2[1:2]
        den = lr * lr + li * li
        f_re = (nr * lr + ni * li) / den
        f_im = (ni * lr - nr * li) / den
        bbar = cmul(bt_ref[d], f_re, sgn * f_im)
        cc = c_ref[d]
        ca = [cmul(cc, a1[j:j + 1], a2[j:j + 1]) * (-sgn) for j in range(t + 1)]
        ca_ref[...] = jnp.zeros_like(ca_ref)
        for k in range(t):
            e_b, e_c = (t - 1 - k, k + 1) if d == 0 else (k, t - k)
            bpow_ref[k * h:(k + 1) * h, d * lanes:(d + 1) * lanes] = (
                cmul(bbar, a1[e_b:e_b + 1], a2[e_b:e_b + 1]).astype(BF16))
            cpow_ref[k * h:(k + 1) * h, d * lanes:(d + 1) * lanes] = ca[e_c].astype(BF16)
            l = t - 1 + k if d == 0 else t - 1 - k
            ca_ref[l * h:(l + 1) * h, :] = ca[k]
        g = lax.dot_general(bbar, ca_ref[...], (((1,), (1,)), ((), ())), preferred_element_type=F32, precision=hp)
        gsum = g if gsum is None else gsum + g
        for s in range(n_steps):
            s1, s2 = apow(float(t * 2 ** s))
            r = d * 2 * n_steps + 2 * s
            tab_ref[r:r + 1, :] = s1
            tab_ref[r + 1:r + 2, :] = s2
    wide = gsum.shape[1]
    col = lax.broadcasted_iota(jnp.int32, (h, wide), 1)
    row = lax.broadcasted_iota(jnp.int32, (h, wide), 0)
    gsum = gsum + jnp.where(col - (t - 1) * h == row, d_ref[...], 0.0)
    for k in range(t):
        off = (t - 1 - k) * h
        shifted = gsum if off == 0 else pltpu.roll(gsum, wide - off, 1)
        toep_ref[k * h:(k + 1) * h, :] = shifted[:, :t * h].astype(BF16)


def _s5_weights(lam_re, lam_im, log_dt, b_re, b_im, c_re, c_im, d_skip, n_steps):
    g, p, h, t = S5_GROUPS, S5_STATE, S5_GROUP, S5_CHUNK
    cat2 = lambda a: jnp.concatenate([a, a], axis=-1)
    par = jnp.stack([cat2(lam_re), cat2(lam_im), jnp.broadcast_to(log_dt[..., None], (S5_DIRS, g, 2 * p))], axis=1)
    par = par.transpose(2, 0, 1, 3).reshape(g, 3 * S5_DIRS, 2 * p)
    btc = jnp.concatenate([b_re, b_im], axis=2).transpose(1, 0, 3, 2)
    ccat = jnp.concatenate([c_re, c_im], axis=3).transpose(1, 0, 2, 3)
    wide = 2 * t * h
    drow = jnp.zeros((g, 1, wide), F32).at[:, 0, (t - 1) * h:t * h].set(d_skip.reshape(g, h))
    w = t * h
    gb = S5_GROUPS_PER_COL
    sq = pl.BlockSpec((gb, w, w), lambda i: (i, 0, 0))
    return pl.pallas_call(
        functools.partial(_s5_weights_kernel, n_steps=n_steps),
        grid=(g // gb,),
        in_specs=[pl.BlockSpec((gb, 3 * S5_DIRS, 2 * p), lambda i: (i, 0, 0)),
                  pl.BlockSpec((gb, S5_DIRS, h, 2 * p), lambda i: (i, 0, 0, 0)),
                  pl.BlockSpec((gb, S5_DIRS, h, 2 * p), lambda i: (i, 0, 0, 0)),
                  pl.BlockSpec((gb, 1, wide), lambda i: (i, 0, 0))],
        out_specs=[sq, sq, sq, pl.BlockSpec((gb, 4 * n_steps, 2 * p), lambda i: (i, 0, 0))],
        out_shape=[jax.ShapeDtypeStruct((g, w, w), BF16)] * 3 + [jax.ShapeDtypeStruct((g, 4 * n_steps, 2 * p), F32)],
        scratch_shapes=[pltpu.VMEM((wide, 2 * p), F32)],
        compiler_params=pltpu.CompilerParams(
            dimension_semantics=("parallel",), vmem_limit_bytes=VMEM_LIMIT_BYTES),
        name="s5_weights",
    )(par, btc, ccat, drow)


def _s5_kernel(u_ref, uc_ref, bpow_ref, toep_ref, cpow_ref, tab_ref, y_ref, ucat_ref, ucc_ref, yacc_ref,
               *, bsz, n_steps):
    t, gw = S5_CHUNK, S5_GROUP
    per = LANE // gw
    n_lat = u_ref.shape[0] // (bsz * t)
    n_ctx = uc_ref.shape[0] // (bsz * t)
    n_ch = n_lat + n_ctx
    rows = bsz * n_ch
    half = 2 * S5_STATE
    for k in range(t):
        ucat_ref[:, k * LANE:(k + 1) * LANE] = u_ref[pl.ds(k, bsz * n_lat, stride=t), :]
        ucc_ref[:, k * LANE:(k + 1) * LANE] = uc_ref[pl.ds(k, bsz * n_ctx, stride=t), :]
    yacc_ref[...] = jnp.zeros_like(yacc_ref)
    rib = lax.broadcasted_iota(jnp.int32, (rows, half), 0) % n_ch
    lane = lax.broadcasted_iota(jnp.int32, (1, LANE), 1)

    def cmul_add(acc, sh, a1, a2):
        return acc + a1 * sh + a2 * pltpu.roll(sh, S5_STATE, 1)

    def gather(src_ref, gl):
        cols = []
        for j in range(t // per):
            acc = None
            for tt in range(per):
                k = j * per + tt
                r = pltpu.roll(src_ref[:, k * LANE:(k + 1) * LANE], (gw * tt - gw * gl) % LANE, 1)
                acc = r if acc is None else jnp.where((lane >= gw * tt) & (lane < gw * (tt + 1)), r, acc)
            cols.append(acc)
        return jnp.concatenate(cols, axis=1).astype(BF16)

    def group(gl, carry):
        ul = gather(ucat_ref, gl)
        uc = gather(ucc_ref, gl)
        bpow = bpow_ref[gl]
        zl = jnp.dot(ul, bpow, preferred_element_type=F32)
        zc = jnp.dot(uc, bpow, preferred_element_type=F32)
        fparts, bparts = [], []
        for b in range(bsz):
            lat = slice(b * n_lat, (b + 1) * n_lat)
            ctx = slice(b * n_ctx, (b + 1) * n_ctx)
            fparts += [zc[ctx, :half], zl[lat, :half]]
            bparts += [zl[lat, half:], zc[ctx, half:]]
        fw = jnp.concatenate(fparts, axis=0)
        bw = jnp.concatenate(bparts, axis=0)
        tab = tab_ref[gl]
        for s in range(n_steps):
            d = 1 << s
            sh = jnp.where(rib >= d, pltpu.roll(fw, d, 0), 0.0)
            fw = cmul_add(fw, sh, tab[2 * s:2 * s + 1], tab[2 * s + 1:2 * s + 2])
            o = 2 * n_steps
            sh = jnp.where(rib < n_ch - d, pltpu.roll(bw, rows - d, 0), 0.0)
            bw = cmul_add(bw, sh, tab[o + 2 * s:o + 2 * s + 1], tab[o + 2 * s + 1:o + 2 * s + 2])
        fe = jnp.where(rib >= 1, pltpu.roll(fw, 1, 0), 0.0)
        be = jnp.where(rib < n_ch - 1, pltpu.roll(bw, rows - 1, 0), 0.0)
        fl = jnp.concatenate([fe[b * n_ch + n_ctx:(b + 1) * n_ch] for b in range(bsz)], axis=0)
        bl = jnp.concatenate([be[b * n_ch:b * n_ch + n_lat] for b in range(bsz)], axis=0)
        st = jnp.concatenate([fl, bl], axis=1).astype(BF16)
        y = (jnp.dot(ul, toep_ref[gl], preferred_element_type=F32)
             + lax.dot_general(st, cpow_ref[gl], (((1,), (1,)), ((), ())), preferred_element_type=F32))
        mine = (lane >= gw * gl) & (lane < gw * (gl + 1))
        for i in range(t):
            src = y[:, (i // per) * LANE:(i // per + 1) * LANE]
            r = pltpu.roll(src, (gw * gl - gw * (i % per)) % LANE, 1)
            blk = slice(i * LANE, (i + 1) * LANE)
            yacc_ref[:, blk] = jnp.where(mine, r, yacc_ref[:, blk])
        return carry

    lax.fori_loop(0, S5_GROUPS_PER_COL, group, 0)
    for i in range(t):
        y_ref[pl.ds(i, bsz * n_lat, stride=t), :] = yacc_ref[:, i * LANE:(i + 1) * LANE]


def _s5_mix(u, uc, bpw, toep, cpw, tab, bsz):
    rl, dm = u.shape
    rc = uc.shape[0]
    t = S5_CHUNK
    w = t * S5_GROUP
    n_steps = tab.shape[1] // 4
    gpc = S5_GROUPS_PER_COL
    wspec = pl.BlockSpec((gpc, w, w), lambda i: (i, 0, 0))
    return pl.pallas_call(
        functools.partial(_s5_kernel, bsz=bsz, n_steps=n_steps),
        grid=(dm // LANE,),
        in_specs=[pl.BlockSpec((rl, LANE), lambda i: (0, i)),
                  pl.BlockSpec((rc, LANE), lambda i: (0, i)),
                  wspec, wspec, wspec,
                  pl.BlockSpec((gpc, 4 * n_steps, tab.shape[2]), lambda i: (i, 0, 0))],
        out_specs=pl.BlockSpec((rl, LANE), lambda i: (0, i)),
        out_shape=jax.ShapeDtypeStruct(u.shape, F32),
        scratch_shapes=[pltpu.VMEM((rl // t, t * LANE), F32), pltpu.VMEM((rc // t, t * LANE), F32),
                        pltpu.VMEM((rl // t, t * LANE), F32)],
        compiler_params=pltpu.CompilerParams(
            dimension_semantics=("parallel",), vmem_limit_bytes=VMEM_LIMIT_BYTES),
        name="s5_mix",
    )(u, uc, bpw, toep, cpw, tab)


def _s5_bidirectional(u, u_ctx, lam_re, lam_im, log_dt, b_re, b_im, c_re, c_im, d_skip):
    bsz, length, dm = u.shape
    ctx_len = u_ctx.shape[1]
    t = S5_CHUNK
    n_lat, n_ctx = length // t, ctx_len // t
    n_steps = max(1, math.ceil(math.log2(n_lat + n_ctx)))
    bpw, toep, cpw, tab = _s5_weights(lam_re, lam_im, log_dt, b_re, b_im, c_re, c_im, d_skip, n_steps)
    y = _s5_mix(u.reshape(bsz * length, dm), u_ctx.reshape(bsz * ctx_len, dm), bpw, toep, cpw, tab, bsz)
    return y.reshape(bsz, length, dm)


def _dft_constants(real_input=False):
    n1 = np.arange(FFT_N1)
    n2 = np.arange(FFT_N2)
    half = FFT_N1 // 2
    th = 2 * np.pi * np.outer(n1, n1) / FFT_N1
    c1, s1 = np.cos(th), np.sin(th)
    if real_input:
        w1 = np.concatenate([c1, -s1], axis=1)
    else:
        w1 = np.concatenate([np.concatenate([c1[:half], -s1[:half]], axis=1),
                             np.concatenate([s1[:half], c1[:half]], axis=1)], axis=0)
    z = np.zeros_like(w1)
    w1p = np.block([[w1, z], [z, w1]])
    ph = 2 * np.pi * np.outer(n2, n1) / FFT_N
    ps = 2 * np.pi * np.outer(n2, n2) / FFT_N2
    f2 = np.concatenate([np.cos(ps), -np.sin(ps)], axis=1)
    g2 = np.concatenate([np.cos(ps), np.sin(ps)], axis=1)
    c2, s2 = np.cos(ph).T, np.sin(ph).T
    wi = np.concatenate([np.concatenate([c1[:, :half], s1[:, :half]], axis=1),
                         np.concatenate([-s1[:, :half], c1[:, :half]], axis=1)], axis=0) / FFT_N
    zi = np.zeros_like(wi)
    wi2 = np.stack([np.concatenate([wi, zi], axis=1), np.concatenate([zi, wi], axis=1)])
    as_b = lambda a: jnp.asarray(a, F32).astype(BF16)
    as_f = lambda a: jnp.asarray(a, F32)
    return [as_b(w1p), as_f(c2), as_f(s2), as_b(f2), as_b(g2), as_b(wi2)]


def _fwd_spectrum(xp, w1p, c2, s2, f2):
    cp = xp.shape[0]
    hn = FFT_N1
    a = jnp.dot(xp.reshape(cp * FFT_N2, LANE).astype(BF16), w1p, preferred_element_type=F32)
    out = []
    for par in range(2):
        ap = a[:, par * LANE:(par + 1) * LANE].reshape(cp, FFT_N2, LANE)
        at = jnp.swapaxes(ap, 1, 2)
        are, aim = at[:, :hn], at[:, hn:]
        at = jnp.concatenate([are * c2 + aim * s2, aim * c2 - are * s2], axis=1)
        p = jnp.dot(at.reshape(cp * 2 * hn, FFT_N2).astype(BF16), f2, preferred_element_type=F32)
        p = p.reshape(cp, 2 * hn, 2 * FFT_N2)
        out.append((p[:, :hn, :FFT_N2] - p[:, hn:, FFT_N2:], p[:, :hn, FFT_N2:] + p[:, hn:, :FFT_N2]))
    return out


def _inv_time(yre, yim, g2, c2, s2, wi_par):
    cp = yre.shape[0]
    hn = FFT_N1
    y = jnp.concatenate([yre, yim], axis=1).reshape(cp * 2 * hn, FFT_N2).astype(BF16)
    q = jnp.dot(y, g2, preferred_element_type=F32).reshape(cp, 2 * hn, 2 * FFT_N2)
    bre = q[:, :hn, :FFT_N2] - q[:, hn:, FFT_N2:]
    bim = q[:, :hn, FFT_N2:] + q[:, hn:, :FFT_N2]
    b2 = jnp.concatenate([bre * c2 - bim * s2, bre * s2 + bim * c2], axis=1)
    bt = jnp.swapaxes(b2, 1, 2)
    return jnp.dot(bt.reshape(cp * FFT_N2, LANE).astype(BF16), wi_par, preferred_element_type=F32)


def _hyena_kernel(z_ref, g1_ref, g2_ref, kf_ref, bias_ref, w1p_ref, c2_ref, s2_ref, f2_ref, gi_ref, wi_ref,
                  o_ref, zt_ref, g1t_ref, g2t_ref, ot_ref, stage_ref):
    k = pl.program_id(1)
    n_s = z_ref.shape[0] // FFT_N2
    cp = HY_PAIRS_PER_STEP

    def to_tiles(x_ref, t_ref):
        for s in range(n_s):
            xs = x_ref[s * FFT_N2:(s + 1) * FFT_N2, :].astype(F32)
            stage_ref[pl.ds(s, LANE, stride=HY_TILE_PITCH), :] = xs.T
        st = stage_ref[...].reshape(HY_PAIRS_PER_COL, 2 * HY_TILE_PITCH, FFT_N2)
        both = jnp.concatenate([st[:, :n_s], st[:, HY_TILE_PITCH:HY_TILE_PITCH + n_s]], axis=1)
        t_ref[...] = jnp.swapaxes(both, 1, 2)

    @pl.when(k == 0)
    def _():
        to_tiles(z_ref, zt_ref)
        to_tiles(g1_ref, g1t_ref)
        to_tiles(g2_ref, g2t_ref)

    sl = pl.ds(pl.multiple_of(k * cp, cp), cp)
    z = zt_ref[sl]
    gates = (g1t_ref, g2t_ref)
    for o in range(HY_ORDER):
        spec = _fwd_spectrum(z, w1p_ref[...], c2_ref[...], s2_ref[...], f2_ref[...])
        conv = None
        for par in range(2):
            xre, xim = spec[par]
            kre = kf_ref[o, par, :, :FFT_N1, :]
            kim = kf_ref[o, par, :, FFT_N1:, :]
            part = _inv_time(xre * kre - xim * kim, xre * kim + xim * kre,
                             gi_ref[...], c2_ref[...], s2_ref[...], wi_ref[par])
            conv = part if conv is None else conv + part
        z = gates[o][sl] * (conv.reshape(cp, FFT_N2, LANE) + bias_ref[o] * z)
    ot_ref[sl] = z

    @pl.when(k == pl.num_programs(1) - 1)
    def _():
        back = jnp.swapaxes(ot_ref[...], 1, 2)
        for c in range(LANE):
            stage_ref[c * HY_TILE_PITCH:c * HY_TILE_PITCH + n_s, :] = back[c // 2, (c % 2) * n_s:(c % 2 + 1) * n_s, :]
        for s in range(n_s):
            rows = stage_ref[pl.ds(s, LANE, stride=HY_TILE_PITCH), :]
            o_ref[s * FFT_N2:(s + 1) * FFT_N2, :] = rows.T.astype(o_ref.dtype)


def _hyena_conv(us, kf, biasp):
    t, _ = us.shape
    consts = _dft_constants()
    ncol = D_HY // LANE
    nsub = HY_PAIRS_PER_COL // HY_PAIRS_PER_STEP
    full = lambda a: pl.BlockSpec(a.shape, lambda j, k: (0,) * a.ndim)
    nat = lambda off: pl.BlockSpec((t, LANE), lambda j, k: (0, off + j))
    tiles = pltpu.VMEM((HY_PAIRS_PER_COL, FFT_N2, LANE), F32)
    return pl.pallas_call(
        _hyena_kernel,
        grid=(ncol, nsub),
        in_specs=[nat(0), nat(ncol), nat(2 * ncol),
                  pl.BlockSpec((HY_ORDER, 2, HY_PAIRS_PER_STEP, 2 * FFT_N1, FFT_N2),
                               lambda j, k: (0, 0, j * nsub + k, 0, 0)),
                  pl.BlockSpec((HY_ORDER, HY_PAIRS_PER_STEP, 1, LANE), lambda j, k: (0, j * nsub + k, 0, 0))]
                 + [full(a) for a in consts],
        out_specs=pl.BlockSpec((t, LANE), lambda j, k: (0, j)),
        out_shape=jax.ShapeDtypeStruct((t, D_HY), us.dtype),
        scratch_shapes=[tiles, tiles, tiles, tiles, pltpu.VMEM((LANE * HY_TILE_PITCH, FFT_N2), F32)],
        compiler_params=pltpu.CompilerParams(
            dimension_semantics=("parallel", "arbitrary"), vmem_limit_bytes=VMEM_LIMIT_BYTES),
        name="hyena_conv",
    )(us, us, us, kf, biasp, *consts)


def _filter_time_kernel(w1t_ref, w1c_ref, w1s_ref, b1_ref, w2_ref, b2_ref, w3_ref, b3_ref, fr_ref,
                        wf_ref, wb_ref, df_ref, db_ref, o_ref, h_ref, k_ref, *, length):
    n_fft = 2 * length
    hp = lax.Precision.HIGHEST

    @pl.when(pl.program_id(0) == 0)
    def _():
        pos = lax.broadcasted_iota(jnp.int32, (1, n_fft), 1)
        lag = jnp.where(pos < length, pos, n_fft - pos).astype(F32)
        t = lag / float(length - 1)
        w = (2.0 * math.pi / length) * lag
        band_step = (HY_BANDS - 1 - 1e-4) / (HY_BANDS - 1)
        bands = 1e-4 + band_step * lax.broadcasted_iota(jnp.int32, (HY_BANDS, 1), 0).astype(F32)
        ang = bands * w
        fr = fr_ref[...]
        h = (w1t_ref[...] * t + jnp.dot(w1c_ref[...], jnp.cos(ang), preferred_element_type=F32, precision=hp)
             - jnp.dot(w1s_ref[...], jnp.sin(ang), preferred_element_type=F32, precision=hp))
        h = jnp.sin(fr * (h + b1_ref[...]))
        h = jnp.sin(fr * (jnp.dot(w2_ref[...], h, preferred_element_type=F32, precision=hp) + b2_ref[...]))
        h = jnp.sin(fr * (jnp.dot(w3_ref[...], h, preferred_element_type=F32, precision=hp) + b3_ref[...]))
        hi = h.astype(BF16)
        h_ref[0] = hi
        h_ref[1] = (h - hi.astype(F32)).astype(BF16)

    def dot3(w, lo, hi_):
        w_hi = w.astype(BF16)
        w_lo = (w - w_hi.astype(F32)).astype(BF16)
        h_hi, h_lo = h_ref[0, :, lo:hi_], h_ref[1, :, lo:hi_]
        return (jnp.dot(w_hi, h_hi, preferred_element_type=F32) + jnp.dot(w_hi, h_lo, preferred_element_type=F32)
                + jnp.dot(w_lo, h_hi, preferred_element_type=F32))

    pos = lax.broadcasted_iota(jnp.int32, (1, length), 1)
    tf = pos.astype(F32) / float(length - 1)
    tb = (length - pos).astype(F32) / float(length - 1)
    kf = dot3(wf_ref[...], 0, length) * jnp.exp(-tf * df_ref[...])
    kb = dot3(wb_ref[...], length, n_fft) * jnp.exp(-tb * db_ref[...])
    kb = jnp.where(pos == 0, 0.0, kb)
    inv = 1.0 / (jnp.sum(jnp.abs(kf), axis=1, keepdims=True) + jnp.sum(jnp.abs(kb), axis=1, keepdims=True))
    k_ref[:, :length] = kf * inv
    k_ref[:, length:] = kb * inv
    cb = wf_ref.shape[0]
    for n1 in range(FILTER_TILE_ROWS):
        row = k_ref[:, n1 * FFT_N2:(n1 + 1) * FFT_N2] if n1 < FFT_N1 else jnp.zeros((cb, FFT_N2), F32)
        o_ref[pl.ds(n1, cb, stride=FILTER_TILE_ROWS), :] = row


def _filter_time(length, w1, b1, w2, b2, w3, b3, freq, w_out, cb=128):
    col = lambda v: v.reshape(-1, 1)
    w1t = w1.T
    n_ch = w_out.shape[1]
    deltas = jnp.abs(jnp.linspace(math.log(HY_TARGET) / HY_SLOW_PCT, math.log(HY_TARGET) / HY_FAST_PCT,
                                  n_ch, dtype=F32)).reshape(n_ch, 1)
    wot = w_out.T
    nb = D_HY // cb
    small = lambda a: pl.BlockSpec(a.shape, lambda i: (0,) * a.ndim)
    fwd = lambda i: ((i // nb) * HY_DIRS * nb + i % nb, 0)
    bwd = lambda i: ((i // nb) * HY_DIRS * nb + nb + i % nb, 0)
    ins = [w1t[:, 0:1], w1t[:, 1:1 + HY_BANDS], w1t[:, 1 + HY_BANDS:], col(b1), w2.T, col(b2), w3.T, col(b3),
           col(freq)]
    hy_ff = w2.shape[0]
    return pl.pallas_call(
        functools.partial(_filter_time_kernel, length=length),
        grid=(HY_ORDER * nb,),
        in_specs=[small(a) for a in ins] + [pl.BlockSpec((cb, hy_ff), fwd), pl.BlockSpec((cb, hy_ff), bwd),
                                            pl.BlockSpec((cb, 1), fwd), pl.BlockSpec((cb, 1), bwd)],
        out_specs=pl.BlockSpec((cb * FILTER_TILE_ROWS, FFT_N2), lambda i: (i, 0)),
        out_shape=jax.ShapeDtypeStruct((HY_ORDER * D_HY * FILTER_TILE_ROWS, FFT_N2), F32),
        scratch_shapes=[pltpu.VMEM((2, hy_ff, 2 * length), BF16), pltpu.VMEM((cb, 2 * length), F32)],
        compiler_params=pltpu.CompilerParams(
            dimension_semantics=("arbitrary",), vmem_limit_bytes=VMEM_LIMIT_BYTES),
        name="hyena_filter_time",
    )(*ins, wot, wot, deltas, deltas)


def _filter_spec_kernel(k_ref, w1p_ref, c2_ref, s2_ref, f2_ref, o_ref):
    cb = k_ref.shape[0] // FILTER_TILE_ROWS
    kt = k_ref[...].reshape(cb, FILTER_TILE_ROWS, FFT_N2)[:, :FFT_N1, :]
    xp = jnp.swapaxes(kt.reshape(cb // 2, 2 * FFT_N1, FFT_N2), 1, 2)
    spec = _fwd_spectrum(xp, w1p_ref[...], c2_ref[...], s2_ref[...], f2_ref[...])
    for par in range(2):
        o_ref[par, :, :FFT_N1, :] = spec[par][0]
        o_ref[par, :, FFT_N1:, :] = spec[par][1]


def _filter_spectrum(kt, cb=128):
    consts = _dft_constants(real_input=True)[:4]
    nb = D_HY // cb
    full = lambda a: pl.BlockSpec(a.shape, lambda i: (0,) * a.ndim)
    return pl.pallas_call(
        _filter_spec_kernel,
        grid=(HY_ORDER * nb,),
        in_specs=[pl.BlockSpec((cb * FILTER_TILE_ROWS, FFT_N2), lambda i: (i, 0))] + [full(a) for a in consts],
        out_specs=pl.BlockSpec((None, 2, cb // 2, 2 * FFT_N1, FFT_N2), lambda i: (i // nb, 0, i % nb, 0, 0)),
        out_shape=jax.ShapeDtypeStruct((HY_ORDER, 2, D_HY // 2, 2 * FFT_N1, FFT_N2), F32),
        compiler_params=pltpu.CompilerParams(
            dimension_semantics=("parallel",), vmem_limit_bytes=VMEM_LIMIT_BYTES),
        name="hyena_filter_spectrum",
    )(kt, *consts)


def _hyena(us, w1, b1, w2, b2, w3, b3, freq, w_out, bias):
    bsz, length, _ = us.shape
    assert 2 * length == FFT_N and bsz == 2, "one complex transform carries exactly two batch rows"
    kf = _filter_spectrum(_filter_time(length, w1, b1, w2, b2, w3, b3, freq, w_out))
    biasp = jnp.repeat(bias.reshape(HY_ORDER, D_HY // 2, 1, 2), FFT_N1, axis=-1)
    return _hyena_conv(us.reshape(bsz * length, -1), kf, biasp).reshape(bsz, length, D_HY)


def kernel(x, c, ctx, c_ctx, w_ada, b_ada, norm_g, ffn_w_gate, ffn_w_up, ffn_w_down, w_in,
           s5_lam_re, s5_lam_im, s5_log_dt, s5_b_re, s5_b_im, s5_c_re, s5_c_im, s5_d,
           hy_short_w, hy_short_b, hy_w1, hy_b1, hy_w2, hy_b2, hy_w3, hy_b3, hy_freq, hy_w_out,
           hy_bias, w_pa, w_pb, w_out, final_g):
    bsz, seq, d = x.shape
    ctx_len = ctx.shape[1]
    n_rows = seq // GRID_W
    depth = w_ada.shape[0]
    assert depth == 1, "context-token outputs are only dropped by the last layer"
    l = 0

    c_rows = jnp.concatenate([c, c_ctx[None, :], jnp.zeros((8 - bsz - 1, d), F32)], axis=0)
    mod_all = _ada_mod(c_rows, w_ada[l], b_ada[l])
    mod = mod_all[:bsz].reshape(bsz, N_SUB, N_MOD, 1, d)
    mod_c = mod_all[bsz:bsz + 1].reshape(1, N_SUB, N_MOD, 1, d)

    def mods(m, sub):
        return tuple(m[:, sub, k] for k in range(N_MOD))

    wg, wu, wd = ffn_w_gate[l], ffn_w_up[l], ffn_w_down[l]
    w_in_b = w_in[l]

    xt = x.reshape(bsz * seq, d)
    ct = ctx.reshape(bsz * ctx_len, d)

    xt = _ffn_sublayer(xt, mods(mod, 0), norm_g[l, 0], wg, wu, wd, 0)
    ct = _ffn_sublayer(ct, mods(mod_c, 0), norm_g[l, 0], wg, wu, wd, 0)

    assert GRID_W * n_rows == seq
    u_s5, us_hy, sig_gates = _in_proj(xt, mod[:, 1, 0], mod[:, 1, 1], norm_g[l, 1], w_in_b,
                                      hy_short_w[l], hy_short_b[l], n_u=I_HY, n_hy=I_GA - I_HY)
    (u_ctx,) = _in_proj(ct, mod_c[:, 1, 0], mod_c[:, 1, 1], norm_g[l, 1], w_in_b[:, :D_S5])

    y_s5 = _s5_bidirectional(u_s5.reshape(bsz, seq, D_S5), u_ctx.reshape(bsz, ctx_len, D_S5),
                             s5_lam_re[l], s5_lam_im[l], s5_log_dt[l],
                             s5_b_re[l], s5_b_im[l], s5_c_re[l], s5_c_im[l], s5_d[l])
    y_hy = _hyena(us_hy.reshape(bsz, seq, I_GA - I_HY),
                  hy_w1[l], hy_b1[l], hy_w2[l], hy_b2[l], hy_w3[l], hy_b3[l], hy_freq[l],
                  hy_w_out[l], hy_bias[l])

    xt = _merge(xt, mod[:, 1, 2], y_s5.reshape(bsz * seq, D_S5), y_hy.reshape(bsz * seq, D_HY),
                sig_gates, w_pa[l].astype(BF16), w_pb[l].astype(BF16), w_out[l].astype(BF16))

    xt = _ffn_sublayer(xt, mods(mod, 2), norm_g[l, 2], wg, wu, wd, 1, final_gain=final_g)
    return xt.reshape(bsz, seq, d)
```

```python
import functools
import math

import jax
import jax.numpy as jnp
import numpy as np
from jax import lax
from jax.experimental import pallas as pl
from jax.experimental.pallas import tpu as pltpu

F32 = jnp.float32
BF16 = jnp.bfloat16

D_MODEL = 2048
GRID_W = 64
D_S5 = 1024
S5_GROUP = 16
S5_GROUPS = D_S5 // S5_GROUP
S5_STATE = 64
S5_DIRS = 2
LAMBDA_RE_MAX = -1e-4
S5_CHUNK = 16
LANE = 128
S5_GROUPS_PER_COL = LANE // S5_GROUP
D_HY = 1024
HY_ORDER = 2
HY_DIRS = 2
HY_SHORT = 3
HY_EMB = 33
HY_BANDS = (HY_EMB - 1) // 2
HY_TARGET = 1e-2
HY_FAST_PCT = 0.3
HY_SLOW_PCT = 1.5
FFT_N1 = 64
FFT_N2 = 128
FFT_N = FFT_N1 * FFT_N2
FILTER_TILE_ROWS = FFT_N1 + 8
HY_TILE_PITCH = FILTER_TILE_ROWS
HY_PAIRS_PER_COL = LANE // 2
HY_PAIRS_PER_STEP = 16
I_HY = D_S5
I_GA = D_S5 + (HY_ORDER + 1) * D_HY
I_GB = I_GA + D_MODEL
D_IN = I_GB + D_MODEL
D_FF = 5632
N_SUB = 3
N_MOD = 3
HALF_STEP = 0.5
FFN_ROW_PARTS = 4
RMS_EPS = 1e-6

VMEM_LIMIT_BYTES = 58 * 1024 * 1024


def _rms_mod(x, gain, shift, scale):
    ms = jnp.mean(x * x, axis=-1, keepdims=True)
    y = x * lax.rsqrt(ms + RMS_EPS) * gain
    return y * (1.0 + scale) + shift


def _split_bf16(v):
    hi = v.astype(BF16)
    return hi, (v - hi.astype(F32)).astype(BF16)


def _ada_kernel(c_ref, w_ref, b_ref, o_ref):
    c = c_ref[...]
    a_hi, a_lo = _split_bf16(c * jax.nn.sigmoid(c))
    w_hi, w_lo = _split_bf16(w_ref[...])
    o_ref[...] = (jnp.dot(a_hi, w_hi, preferred_element_type=F32) + jnp.dot(a_lo, w_hi, preferred_element_type=F32)
                  + jnp.dot(a_hi, w_lo, preferred_element_type=F32) + b_ref[...])


def _ada_mod(c_rows, w, b, tn=1024):
    rows, d = c_rows.shape
    n = w.shape[1]
    return pl.pallas_call(
        _ada_kernel,
        grid=(n // tn,),
        in_specs=[pl.BlockSpec((rows, d), lambda j: (0, 0)),
                  pl.BlockSpec((d, tn), lambda j: (0, j)),
                  pl.BlockSpec((1, tn), lambda j: (0, j))],
        out_specs=pl.BlockSpec((rows, tn), lambda j: (0, j)),
        out_shape=jax.ShapeDtypeStruct((rows, n), F32),
        compiler_params=pltpu.CompilerParams(
            dimension_semantics=("arbitrary",), vmem_limit_bytes=VMEM_LIMIT_BYTES),
        name="ada_mod",
    )(c_rows, w, b.reshape(1, n))


def _ffn_kernel(x_ref, shift_ref, scale_ref, gate_ref, gain_ref, wg_ref, wu_ref, wd_ref,
                fg_ref, o_ref, h_ref, *, final_norm):
    j = pl.program_id(1)
    tm = x_ref.shape[0]

    def swiglu_down(h):
        g = jnp.dot(h, wg_ref[...].astype(BF16), preferred_element_type=F32)
        u = jnp.dot(h, wu_ref[...].astype(BF16), preferred_element_type=F32)
        a = (g * jax.nn.sigmoid(g) * u).astype(BF16)
        return jnp.dot(a, wd_ref[...].astype(BF16), preferred_element_type=F32)

    @pl.when(j == 0)
    def _():
        for r in range(0, tm, tm // FFN_ROW_PARTS):
            rows = slice(r, r + tm // FFN_ROW_PARTS)
            h = _rms_mod(x_ref[rows, :], gain_ref[...], shift_ref[...], scale_ref[...]).astype(BF16)
            h_ref[rows, :] = h
            o_ref[rows, :] = swiglu_down(h)

    @pl.when(j > 0)
    def _():
        o_ref[...] += swiglu_down(h_ref[...])

    @pl.when(j == pl.num_programs(1) - 1)
    def _():
        y = x_ref[...] + (HALF_STEP * gate_ref[...]) * o_ref[...]
        if final_norm:
            ms = jnp.mean(y * y, axis=-1, keepdims=True)
            y = y * lax.rsqrt(ms + RMS_EPS) * fg_ref[...]
        o_ref[...] = y


def _ffn_sublayer(x, mods, gain, wg, wu, wd, which, final_gain=None, tm=1024, tf=256):
    t, d = x.shape
    bm = mods[0].shape[0]
    tm = min(tm, t // bm)
    blocks_per_batch = (t // bm) // tm
    dff = wg.shape[2]
    final_norm = final_gain is not None
    fg = final_gain if final_norm else gain
    mod_spec = pl.BlockSpec((None, 1, d), lambda i, j: (i // blocks_per_batch, 0, 0))
    vec_spec = pl.BlockSpec((1, d), lambda i, j: (0, 0))
    return pl.pallas_call(
        functools.partial(_ffn_kernel, final_norm=final_norm),
        grid=(t // tm, dff // tf),
        in_specs=[pl.BlockSpec((tm, d), lambda i, j: (i, 0)),
                  mod_spec, mod_spec, mod_spec, vec_spec,
                  pl.BlockSpec((None, d, tf), lambda i, j: (which, 0, j)),
                  pl.BlockSpec((None, d, tf), lambda i, j: (which, 0, j)),
                  pl.BlockSpec((None, tf, d), lambda i, j: (which, j, 0)),
                  vec_spec],
        out_specs=pl.BlockSpec((tm, d), lambda i, j: (i, 0)),
        out_shape=jax.ShapeDtypeStruct((t, d), F32),
        scratch_shapes=[pltpu.VMEM((tm, d), BF16)],
        compiler_params=pltpu.CompilerParams(
            dimension_semantics=("parallel", "arbitrary"), vmem_limit_bytes=VMEM_LIMIT_BYTES),
        name="ffn_final" if final_norm else "ffn",
    )(x, *mods, gain.reshape(1, d), wg, wu, wd, fg.reshape(1, d))


def _proj_kernel(x_ref, shift_ref, scale_ref, gain_ref, w_ref, sw_ref, sb_ref, *rest, n_u, n_hy, row_len, part):
    o_refs, h_ref = rest[:-1], rest[-1]
    j = pl.program_id(1)

    tm, tn = h_ref.shape[0], w_ref.shape[1]

    def first_block(o_ref):
        for r in range(0, tm, tm // 2):
            rows = slice(r, r + tm // 2)
            h = _rms_mod(x_ref[rows, :], gain_ref[...], shift_ref[...], scale_ref[...]).astype(BF16)
            h_ref[rows, :] = h
            for c in range(0, tn, part):
                p = jnp.dot(h, w_ref[:, c:c + part].astype(BF16), preferred_element_type=F32)
                o_ref[rows, c:c + part] = p.astype(o_ref.dtype)

    def in_parts(o_ref, epilogue):
        for c in range(0, tn, part):
            p = jnp.dot(h_ref[...], w_ref[:, c:c + part].astype(BF16), preferred_element_type=F32)
            o_ref[:, c:c + part] = epilogue(p, c).astype(o_ref.dtype)

    def short_conv(p, c):
        col = lax.broadcasted_iota(jnp.int32, p.shape, 0) % row_len
        prev = jnp.where(col == 0, 0.0, pltpu.roll(p, 1, 0))
        nxt = jnp.where(col == row_len - 1, 0.0, pltpu.roll(p, tm - 1, 0))
        sw = sw_ref[:, c:c + part]
        return sb_ref[:, c:c + part] + prev * sw[0:1] + p * sw[1:2] + nxt * sw[2:3]

    @pl.when(j == 0)
    def _():
        first_block(o_refs[0])

    if n_hy == 0:
        @pl.when(j > 0)
        def _():
            in_parts(o_refs[0], lambda p, c: p)
        return

    @pl.when((j > 0) & (j < n_u))
    def _():
        in_parts(o_refs[0], lambda p, c: p)

    @pl.when((j >= n_u) & (j < n_u + n_hy))
    def _():
        in_parts(o_refs[1], short_conv)

    @pl.when(j >= n_u + n_hy)
    def _():
        in_parts(o_refs[2], lambda p, c: jax.nn.sigmoid(p))


def _in_proj(x, shift, scale, gain, w, short_w=None, short_b=None, n_u=D_S5, n_hy=0, row_len=GRID_W,
             tm=1024, tn=512, part=256):
    t, d = x.shape
    bm = shift.shape[0]
    tm = min(tm, t // bm)
    blocks_per_batch = (t // bm) // tm
    n = w.shape[1]
    assert tm % row_len == 0 and (t // bm) % tm == 0
    bu, bh = n_u // tn, n_hy // tn
    bg = n // tn - bu - bh
    mod_spec = pl.BlockSpec((None, 1, d), lambda i, j: (i // blocks_per_batch, 0, 0))
    out_shape = [jax.ShapeDtypeStruct((t, n_u), F32)]
    out_specs = [pl.BlockSpec((tm, tn), lambda i, j: (i, jnp.minimum(j, bu - 1)))]
    if bh:
        out_shape += [jax.ShapeDtypeStruct((t, n_hy), BF16), jax.ShapeDtypeStruct((t, bg * tn), BF16)]
        out_specs += [pl.BlockSpec((tm, tn), lambda i, j: (i, jnp.clip(j - bu, 0, bh - 1))),
                      pl.BlockSpec((tm, tn), lambda i, j: (i, jnp.maximum(j - bu - bh, 0)))]
        sw, sb = short_w, short_b.reshape(1, n_hy)
        hy_blk = lambda i, j: (0, jnp.clip(j - bu, 0, bh - 1))
    else:
        sw, sb = jnp.zeros((HY_SHORT, tn), F32), jnp.zeros((1, tn), F32)
        hy_blk = lambda i, j: (0, 0)
    return pl.pallas_call(
        functools.partial(_proj_kernel, n_u=bu, n_hy=bh, row_len=row_len, part=part),
        grid=(t // tm, n // tn),
        in_specs=[pl.BlockSpec((tm, d), lambda i, j: (i, 0)),
                  mod_spec, mod_spec,
                  pl.BlockSpec((1, d), lambda i, j: (0, 0)),
                  pl.BlockSpec((d, tn), lambda i, j: (0, j)),
                  pl.BlockSpec((HY_SHORT, tn), hy_blk),
                  pl.BlockSpec((1, tn), hy_blk)],
        out_specs=out_specs,
        out_shape=out_shape,
        scratch_shapes=[pltpu.VMEM((tm, d), BF16)],
        compiler_params=pltpu.CompilerParams(
            dimension_semantics=("parallel", "arbitrary"), vmem_limit_bytes=VMEM_LIMIT_BYTES),
        name="in_proj",
    )(x, shift, scale, gain.reshape(1, d), w, sw, sb)


def _gelu_tanh(x):
    return 0.5 * x * (1.0 + jnp.tanh(math.sqrt(2.0 / math.pi) * (x + 0.044715 * (x * x * x))))


def _merge_kernel(x_ref, gate_ref, ys_ref, yh_ref, ga_ref, gb_ref, wpa_lo_ref, wpa_hi_ref, wpb_ref,
                  wout_ref, o_ref, s_ref, acc_ref):
    j = pl.program_id(1)

    @pl.when(j == 0)
    def _():
        s_ref[...] = _gelu_tanh(ys_ref[...].astype(F32)).astype(BF16)
        acc_ref[...] = jnp.zeros_like(acc_ref)

    s = s_ref[...]
    pa_lo = jnp.dot(s, wpa_lo_ref[...], preferred_element_type=F32)
    pa_hi = jnp.dot(s, wpa_hi_ref[...], preferred_element_type=F32)
    y_a = pa_lo * jax.nn.sigmoid(pa_hi)
    y_b = jnp.dot(yh_ref[...], wpb_ref[...], preferred_element_type=F32)
    m = ga_ref[...].astype(F32) * y_a + gb_ref[...].astype(F32) * y_b
    acc_ref[...] += jnp.dot(m.astype(BF16), wout_ref[...], preferred_element_type=F32)

    @pl.when(j == pl.num_programs(1) - 1)
    def _():
        o_ref[...] = x_ref[...] + gate_ref[...] * acc_ref[...]


def _merge(x, gate, y_s5, y_hy, sig_gates, w_pa, w_pb, w_out, tm=512, tn=1024):
    t, d = x.shape
    bm = gate.shape[0]
    blocks_per_batch = (t // bm) // tm
    nj = d // tn
    ds5 = y_s5.shape[1]
    dhy = y_hy.shape[1]
    return pl.pallas_call(
        _merge_kernel,
        grid=(t // tm, nj),
        in_specs=[pl.BlockSpec((tm, d), lambda i, j: (i, 0)),
                  pl.BlockSpec((None, 1, d), lambda i, j: (i // blocks_per_batch, 0, 0)),
                  pl.BlockSpec((tm, ds5), lambda i, j: (i, 0)),
                  pl.BlockSpec((tm, dhy), lambda i, j: (i, 0)),
                  pl.BlockSpec((tm, tn), lambda i, j: (i, j)),
                  pl.BlockSpec((tm, tn), lambda i, j: (i, nj + j)),
                  pl.BlockSpec((ds5, tn), lambda i, j: (0, j)),
                  pl.BlockSpec((ds5, tn), lambda i, j: (0, nj + j)),
                  pl.BlockSpec((dhy, tn), lambda i, j: (0, j)),
                  pl.BlockSpec((tn, d), lambda i, j: (j, 0))],
        out_specs=pl.BlockSpec((tm, d), lambda i, j: (i, 0)),
        out_shape=jax.ShapeDtypeStruct((t, d), F32),
        scratch_shapes=[pltpu.VMEM((tm, ds5), BF16), pltpu.VMEM((tm, d), F32)],
        compiler_params=pltpu.CompilerParams(
            dimension_semantics=("parallel", "arbitrary"), vmem_limit_bytes=VMEM_LIMIT_BYTES),
        name="merge",
    )(x, gate, y_s5, y_hy, sig_gates, sig_gates, w_pa, w_pa, w_pb, w_out)


def _s5_weights_kernel(*refs, n_steps):
    blocks, ca_ref = refs[:-1], refs[-1]

    def one(gi, carry):
        _s5_weights_group(*[r.at[gi] for r in blocks], ca_ref, n_steps=n_steps)
        return carry

    lax.fori_loop(0, blocks[0].shape[0], one, 0)


def _s5_weights_group(par_ref, bt_ref, c_ref, d_ref, bpow_ref, toep_ref, cpow_ref, tab_ref, ca_ref, *, n_steps):
    t, h, p = S5_CHUNK, S5_GROUP, S5_STATE
    lanes = 2 * p
    hp = lax.Precision.HIGHEST
    sgn = jnp.where(lax.broadcasted_iota(jnp.int32, (1, lanes), 1) < p, -1.0, 1.0)
    par = par_ref[...]
    gsum = None
    for d in range(S5_DIRS):
        lr = jnp.minimum(par[3 * d:3 * d + 1], LAMBDA_RE_MAX)
        li = par[3 * d + 1:3 * d + 2]
        dt = jnp.exp(par[3 * d + 2:3 * d + 3])
        zr, zi = lr * dt, li * dt

        def apow(j):
            mag = jnp.exp(j * zr)
            return mag * jnp.cos(j * zi), sgn * (mag * jnp.sin(j * zi))

        def cmul(x, a1, a2):
            return x * a1 + pltpu.roll(x, p, 1) * a2

        a1, a2 = apow(lax.broadcasted_iota(jnp.int32, (t + 1, 1), 0).astype(F32))
        nr, ni = a1[1:2] - 1.0, sgn * a2[1:2]
        den = lr * lr + li * li
        f_re = (nr * lr + ni * li) / den
        f_im = (ni * lr - nr * li) / den
        bbar = cmul(bt_ref[d], f_re, sgn * f_im)
        cc = c_ref[d]
        ca = [cmul(cc, a1[j:j + 1], a2[j:j + 1]) * (-sgn) for j in range(t + 1)]
        ca_ref[...] = jnp.zeros_like(ca_ref)
        for k in range(t):
            e_b, e_c = (t - 1 - k, k + 1) if d == 0 else (k, t - k)
            bpow_ref[k * h:(k + 1) * h, d * lanes:(d + 1) * lanes] = (
                cmul(bbar, a1[e_b:e_b + 1], a2[e_b:e_b + 1]).astype(BF16))
            cpow_ref[k * h:(k + 1) * h, d * lanes:(d + 1) * lanes] = ca[e_c].astype(BF16)
            l = t - 1 + k if d == 0 else t - 1 - k
            ca_ref[l * h:(l + 1) * h, :] = ca[k]
        g = lax.dot_general(bbar, ca_ref[...], (((1,), (1,)), ((), ())), preferred_element_type=F32, precision=hp)
        gsum = g if gsum is None else gsum + g
        for s in range(n_steps):
            s1, s2 = apow(float(t * 2 ** s))
            r = d * 2 * n_steps + 2 * s
            tab_ref[r:r + 1, :] = s1
            tab_ref[r + 1:r + 2, :] = s2
    wide = gsum.shape[1]
    col = lax.broadcasted_iota(jnp.int32, (h, wide), 1)
    row = lax.broadcasted_iota(jnp.int32, (h, wide), 0)
    gsum = gsum + jnp.where(col - (t - 1) * h == row, d_ref[...], 0.0)
    for k in range(t):
        off = (t - 1 - k) * h
        shifted = gsum if off == 0 else pltpu.roll(gsum, wide - off, 1)
        toep_ref[k * h:(k + 1) * h, :] = shifted[:, :t * h].astype(BF16)


def _s5_weights(lam_re, lam_im, log_dt, b_re, b_im, c_re, c_im, d_skip, n_steps):
    g, p, h, t = S5_GROUPS, S5_STATE, S5_GROUP, S5_CHUNK
    cat2 = lambda a: jnp.concatenate([a, a], axis=-1)
    par = jnp.stack([cat2(lam_re), cat2(lam_im), jnp.broadcast_to(log_dt[..., None], (S5_DIRS, g, 2 * p))], axis=1)
    par = par.transpose(2, 0, 1, 3).reshape(g, 3 * S5_DIRS, 2 * p)
    btc = jnp.concatenate([b_re, b_im], axis=2).transpose(1, 0, 3, 2)
    ccat = jnp.concatenate([c_re, c_im], axis=3).transpose(1, 0, 2, 3)
    wide = 2 * t * h
    drow = jnp.zeros((g, 1, wide), F32).at[:, 0, (t - 1) * h:t * h].set(d_skip.reshape(g, h))
    w = t * h
    gb = S5_GROUPS_PER_COL
    sq = pl.BlockSpec((gb, w, w), lambda i: (i, 0, 0))
    return pl.pallas_call(
        functools.partial(_s5_weights_kernel, n_steps=n_steps),
        grid=(g // gb,),
        in_specs=[pl.BlockSpec((gb, 3 * S5_DIRS, 2 * p), lambda i: (i, 0, 0)),
                  pl.BlockSpec((gb, S5_DIRS, h, 2 * p), lambda i: (i, 0, 0, 0)),
                  pl.BlockSpec((gb, S5_DIRS, h, 2 * p), lambda i: (i, 0, 0, 0)),
                  pl.BlockSpec((gb, 1, wide), lambda i: (i, 0, 0))],
        out_specs=[sq, sq, sq, pl.BlockSpec((gb, 4 * n_steps, 2 * p), lambda i: (i, 0, 0))],
        out_shape=[jax.ShapeDtypeStruct((g, w, w), BF16)] * 3 + [jax.ShapeDtypeStruct((g, 4 * n_steps, 2 * p), F32)],
        scratch_shapes=[pltpu.VMEM((wide, 2 * p), F32)],
        compiler_params=pltpu.CompilerParams(
            dimension_semantics=("parallel",), vmem_limit_bytes=VMEM_LIMIT_BYTES),
        name="s5_weights",
    )(par, btc, ccat, drow)


def _s5_kernel(u_ref, uc_ref, bpow_ref, toep_ref, cpow_ref, tab_ref, y_ref, ucat_ref, ucc_ref, yacc_ref,
               *, bsz, n_steps):
    t, gw = S5_CHUNK, S5_GROUP
    per = LANE // gw
    n_lat = u_ref.shape[0] // (bsz * t)
    n_ctx = uc_ref.shape[0] // (bsz * t)
    n_ch = n_lat + n_ctx
    rows = bsz * n_ch
    half = 2 * S5_STATE
    for k in range(t):
        ucat_ref[:, k * LANE:(k + 1) * LANE] = u_ref[pl.ds(k, bsz * n_lat, stride=t), :]
        ucc_ref[:, k * LANE:(k + 1) * LANE] = uc_ref[pl.ds(k, bsz * n_ctx, stride=t), :]
    yacc_ref[...] = jnp.zeros_like(yacc_ref)
    rib = lax.broadcasted_iota(jnp.int32, (rows, half), 0) % n_ch
    lane = lax.broadcasted_iota(jnp.int32, (1, LANE), 1)

    def cmul_add(acc, sh, a1, a2):
        return acc + a1 * sh + a2 * pltpu.roll(sh, S5_STATE, 1)

    def gather(src_ref, gl):
        cols = []
        for j in range(t // per):
            acc = None
            for tt in range(per):
                k = j * per + tt
                r = pltpu.roll(src_ref[:, k * LANE:(k + 1) * LANE], (gw * tt - gw * gl) % LANE, 1)
                acc = r if acc is None else jnp.where((lane >= gw * tt) & (lane < gw * (tt + 1)), r, acc)
            cols.append(acc)
        return jnp.concatenate(cols, axis=1).astype(BF16)

    def group(gl, carry):
        ul = gather(ucat_ref, gl)
        uc = gather(ucc_ref, gl)
        bpow = bpow_ref[gl]
        zl = jnp.dot(ul, bpow, preferred_element_type=F32)
        zc = jnp.dot(uc, bpow, preferred_element_type=F32)
        fparts, bparts = [], []
        for b in range(bsz):
            lat = slice(b * n_lat, (b + 1) * n_lat)
            ctx = slice(b * n_ctx, (b + 1) * n_ctx)
            fparts += [zc[ctx, :half], zl[lat, :half]]
            bparts += [zl[lat, half:], zc[ctx, half:]]
        fw = jnp.concatenate(fparts, axis=0)
        bw = jnp.concatenate(bparts, axis=0)
        tab = tab_ref[gl]
        for s in range(n_steps):
            d = 1 << s
            sh = jnp.where(rib >= d, pltpu.roll(fw, d, 0), 0.0)
            fw = cmul_add(fw, sh, tab[2 * s:2 * s + 1], tab[2 * s + 1:2 * s + 2])
            o = 2 * n_steps
            sh = jnp.where(rib < n_ch - d, pltpu.roll(bw, rows - d, 0), 0.0)
            bw = cmul_add(bw, sh, tab[o + 2 * s:o + 2 * s + 1], tab[o + 2 * s + 1:o + 2 * s + 2])
        fe = jnp.where(rib >= 1, pltpu.roll(fw, 1, 0), 0.0)
        be = jnp.where(rib < n_ch - 1, pltpu.roll(bw, rows - 1, 0), 0.0)
        fl = jnp.concatenate([fe[b * n_ch + n_ctx:(b + 1) * n_ch] for b in range(bsz)], axis=0)
        bl = jnp.concatenate([be[b * n_ch:b * n_ch + n_lat] for b in range(bsz)], axis=0)
        st = jnp.concatenate([fl, bl], axis=1).astype(BF16)
        y = (jnp.dot(ul, toep_ref[gl], preferred_element_type=F32)
             + lax.dot_general(st, cpow_ref[gl], (((1,), (1,)), ((), ())), preferred_element_type=F32))
        mine = (lane >= gw * gl) & (lane < gw * (gl + 1))
        for i in range(t):
            src = y[:, (i // per) * LANE:(i // per + 1) * LANE]
            r = pltpu.roll(src, (gw * gl - gw * (i % per)) % LANE, 1)
            blk = slice(i * LANE, (i + 1) * LANE)
            yacc_ref[:, blk] = jnp.where(mine, r, yacc_ref[:, blk])
        return carry

    lax.fori_loop(0, S5_GROUPS_PER_COL, group, 0)
    for i in range(t):
        y_ref[pl.ds(i, bsz * n_lat, stride=t), :] = yacc_ref[:, i * LANE:(i + 1) * LANE]


def _s5_mix(u, uc, bpw, toep, cpw, tab, bsz):
    rl, dm = u.shape
    rc = uc.shape[0]
    t = S5_CHUNK
    w = t * S5_GROUP
    n_steps = tab.shape[1] // 4
    gpc = S5_GROUPS_PER_COL
    wspec = pl.BlockSpec((gpc, w, w), lambda i: (i, 0, 0))
    return pl.pallas_call(
        functools.partial(_s5_kernel, bsz=bsz, n_steps=n_steps),
        grid=(dm // LANE,),
        in_specs=[pl.BlockSpec((rl, LANE), lambda i: (0, i)),
                  pl.BlockSpec((rc, LANE), lambda i: (0, i)),
                  wspec, wspec, wspec,
                  pl.BlockSpec((gpc, 4 * n_steps, tab.shape[2]), lambda i: (i, 0, 0))],
        out_specs=pl.BlockSpec((rl, LANE), lambda i: (0, i)),
        out_shape=jax.ShapeDtypeStruct(u.shape, F32),
        scratch_shapes=[pltpu.VMEM((rl // t, t * LANE), F32), pltpu.VMEM((rc // t, t * LANE), F32),
                        pltpu.VMEM((rl // t, t * LANE), F32)],
        compiler_params=pltpu.CompilerParams(
            dimension_semantics=("parallel",), vmem_limit_bytes=VMEM_LIMIT_BYTES),
        name="s5_mix",
    )(u, uc, bpw, toep, cpw, tab)


def _s5_bidirectional(u, u_ctx, lam_re, lam_im, log_dt, b_re, b_im, c_re, c_im, d_skip):
    bsz, length, dm = u.shape
    ctx_len = u_ctx.shape[1]
    t = S5_CHUNK
    n_lat, n_ctx = length // t, ctx_len // t
    n_steps = max(1, math.ceil(math.log2(n_lat + n_ctx)))
    bpw, toep, cpw, tab = _s5_weights(lam_re, lam_im, log_dt, b_re, b_im, c_re, c_im, d_skip, n_steps)
    y = _s5_mix(u.reshape(bsz * length, dm), u_ctx.reshape(bsz * ctx_len, dm), bpw, toep, cpw, tab, bsz)
    return y.reshape(bsz, length, dm)


def _dft_constants(real_input=False):
    n1 = np.arange(FFT_N1)
    n2 = np.arange(FFT_N2)
    half = FFT_N1 // 2
    th = 2 * np.pi * np.outer(n1, n1) / FFT_N1
    c1, s1 = np.cos(th), np.sin(th)
    if real_input:
        w1 = np.concatenate([c1, -s1], axis=1)
    else:
        w1 = np.concatenate([np.concatenate([c1[:half], -s1[:half]], axis=1),
                             np.concatenate([s1[:half], c1[:half]], axis=1)], axis=0)
    z = np.zeros_like(w1)
    w1p = np.block([[w1, z], [z, w1]])
    ph = 2 * np.pi * np.outer(n2, n1) / FFT_N
    ps = 2 * np.pi * np.outer(n2, n2) / FFT_N2
    f2 = np.concatenate([np.cos(ps), -np.sin(ps)], axis=1)
    g2 = np.concatenate([np.cos(ps), np.sin(ps)], axis=1)
    c2, s2 = np.cos(ph).T, np.sin(ph).T
    wi = np.concatenate([np.concatenate([c1[:, :half], s1[:, :half]], axis=1),
                         np.concatenate([-s1[:, :half], c1[:, :half]], axis=1)], axis=0) / FFT_N
    zi = np.zeros_like(wi)
    wi2 = np.stack([np.concatenate([wi, zi], axis=1), np.concatenate([zi, wi], axis=1)])
    as_b = lambda a: jnp.asarray(a, F32).astype(BF16)
    as_f = lambda a: jnp.asarray(a, F32)
    return [as_b(w1p), as_f(c2), as_f(s2), as_b(f2), as_b(g2), as_b(wi2)]


def _fwd_spectrum(xp, w1p, c2, s2, f2):
    cp = xp.shape[0]
    hn = FFT_N1
    a = jnp.dot(xp.reshape(cp * FFT_N2, LANE).astype(BF16), w1p, preferred_element_type=F32)
    out = []
    for par in range(2):
        ap = a[:, par * LANE:(par + 1) * LANE].reshape(cp, FFT_N2, LANE)
        at = jnp.swapaxes(ap, 1, 2)
        are, aim = at[:, :hn], at[:, hn:]
        at = jnp.concatenate([are * c2 + aim * s2, aim * c2 - are * s2], axis=1)
        p = jnp.dot(at.reshape(cp * 2 * hn, FFT_N2).astype(BF16), f2, preferred_element_type=F32)
        p = p.reshape(cp, 2 * hn, 2 * FFT_N2)
        out.append((p[:, :hn, :FFT_N2] - p[:, hn:, FFT_N2:], p[:, :hn, FFT_N2:] + p[:, hn:, :FFT_N2]))
    return out


def _inv_time(yre, yim, g2, c2, s2, wi_par):
    cp = yre.shape[0]
    hn = FFT_N1
    y = jnp.concatenate([yre, yim], axis=1).reshape(cp * 2 * hn, FFT_N2).astype(BF16)
    q = jnp.dot(y, g2, preferred_element_type=F32).reshape(cp, 2 * hn, 2 * FFT_N2)
    bre = q[:, :hn, :FFT_N2] - q[:, hn:, FFT_N2:]
    bim = q[:, :hn, FFT_N2:] + q[:, hn:, :FFT_N2]
    b2 = jnp.concatenate([bre * c2 - bim * s2, bre * s2 + bim * c2], axis=1)
    bt = jnp.swapaxes(b2, 1, 2)
    return jnp.dot(bt.reshape(cp * FFT_N2, LANE).astype(BF16), wi_par, preferred_element_type=F32)


def _hyena_kernel(z_ref, g1_ref, g2_ref, kf_ref, bias_ref, w1p_ref, c2_ref, s2_ref, f2_ref, gi_ref, wi_ref,
                  o_ref, zt_ref, g1t_ref, g2t_ref, ot_ref, stage_ref):
    k = pl.program_id(1)
    n_s = z_ref.shape[0] // FFT_N2
    cp = HY_PAIRS_PER_STEP

    def to_tiles(x_ref, t_ref):
        for s in range(n_s):
            xs = x_ref[s * FFT_N2:(s + 1) * FFT_N2, :].astype(F32)
            stage_ref[pl.ds(s, LANE, stride=HY_TILE_PITCH), :] = xs.T
        st = stage_ref[...].reshape(HY_PAIRS_PER_COL, 2 * HY_TILE_PITCH, FFT_N2)
        both = jnp.concatenate([st[:, :n_s], st[:, HY_TILE_PITCH:HY_TILE_PITCH + n_s]], axis=1)
        t_ref[...] = jnp.swapaxes(both, 1, 2)

    @pl.when(k == 0)
    def _():
        to_tiles(z_ref, zt_ref)
        to_tiles(g1_ref, g1t_ref)
        to_tiles(g2_ref, g2t_ref)

    sl = pl.ds(pl.multiple_of(k * cp, cp), cp)
    z = zt_ref[sl]
    gates = (g1t_ref, g2t_ref)
    for o in range(HY_ORDER):
        spec = _fwd_spectrum(z, w1p_ref[...], c2_ref[...], s2_ref[...], f2_ref[...])
        conv = None
        for par in range(2):
            xre, xim = spec[par]
            kre = kf_ref[o, par, :, :FFT_N1, :]
            kim = kf_ref[o, par, :, FFT_N1:, :]
            part = _inv_time(xre * kre - xim * kim, xre * kim + xim * kre,
                             gi_ref[...], c2_ref[...], s2_ref[...], wi_ref[par])
            conv = part if conv is None else conv + part
        z = gates[o][sl] * (conv.reshape(cp, FFT_N2, LANE) + bias_ref[o] * z)
    ot_ref[sl] = z

    @pl.when(k == pl.num_programs(1) - 1)
    def _():
        back = jnp.swapaxes(ot_ref[...], 1, 2)
        for c in range(LANE):
            stage_ref[c * HY_TILE_PITCH:c * HY_TILE_PITCH + n_s, :] = back[c // 2, (c % 2) * n_s:(c % 2 + 1) * n_s, :]
        for s in range(n_s):
            rows = stage_ref[pl.ds(s, LANE, stride=HY_TILE_PITCH), :]
            o_ref[s * FFT_N2:(s + 1) * FFT_N2, :] = rows.T.astype(o_ref.dtype)


def _hyena_conv(us, kf, biasp):
    t, _ = us.shape
    consts = _dft_constants()
    ncol = D_HY // LANE
    nsub = HY_PAIRS_PER_COL // HY_PAIRS_PER_STEP
    full = lambda a: pl.BlockSpec(a.shape, lambda j, k: (0,) * a.ndim)
    nat = lambda off: pl.BlockSpec((t, LANE), lambda j, k: (0, off + j))
    tiles = pltpu.VMEM((HY_PAIRS_PER_COL, FFT_N2, LANE), F32)
    return pl.pallas_call(
        _hyena_kernel,
        grid=(ncol, nsub),
        in_specs=[nat(0), nat(ncol), nat(2 * ncol),
                  pl.BlockSpec((HY_ORDER, 2, HY_PAIRS_PER_STEP, 2 * FFT_N1, FFT_N2),
                               lambda j, k: (0, 0, j * nsub + k, 0, 0)),
                  pl.BlockSpec((HY_ORDER, HY_PAIRS_PER_STEP, 1, LANE), lambda j, k: (0, j * nsub + k, 0, 0))]
                 + [full(a) for a in consts],
        out_specs=pl.BlockSpec((t, LANE), lambda j, k: (0, j)),
        out_shape=jax.ShapeDtypeStruct((t, D_HY), us.dtype),
        scratch_shapes=[tiles, tiles, tiles, tiles, pltpu.VMEM((LANE * HY_TILE_PITCH, FFT_N2), F32)],
        compiler_params=pltpu.CompilerParams(
            dimension_semantics=("parallel", "arbitrary"), vmem_limit_bytes=VMEM_LIMIT_BYTES),
        name="hyena_conv",
    )(us, us, us, kf, biasp, *consts)


def _filter_time_kernel(w1t_ref, w1c_ref, w1s_ref, b1_ref, w2_ref, b2_ref, w3_ref, b3_ref, fr_ref,
                        wf_ref, wb_ref, df_ref, db_ref, o_ref, h_ref, k_ref, *, length):
    n_fft = 2 * length
    hp = lax.Precision.HIGHEST

    @pl.when(pl.program_id(0) == 0)
    def _():
        pos = lax.broadcasted_iota(jnp.int32, (1, n_fft), 1)
        lag = jnp.where(pos < length, pos, n_fft - pos).astype(F32)
        t = lag / float(length - 1)
        w = (2.0 * math.pi / length) * lag
        band_step = (HY_BANDS - 1 - 1e-4) / (HY_BANDS - 1)
        bands = 1e-4 + band_step * lax.broadcasted_iota(jnp.int32, (HY_BANDS, 1), 0).astype(F32)
        ang = bands * w
        fr = fr_ref[...]
        h = (w1t_ref[...] * t + jnp.dot(w1c_ref[...], jnp.cos(ang), preferred_element_type=F32, precision=hp)
             - jnp.dot(w1s_ref[...], jnp.sin(ang), preferred_element_type=F32, precision=hp))
        h = jnp.sin(fr * (h + b1_ref[...]))
        h = jnp.sin(fr * (jnp.dot(w2_ref[...], h, preferred_element_type=F32, precision=hp) + b2_ref[...]))
        h = jnp.sin(fr * (jnp.dot(w3_ref[...], h, preferred_element_type=F32, precision=hp) + b3_ref[...]))
        hi = h.astype(BF16)
        h_ref[0] = hi
        h_ref[1] = (h - hi.astype(F32)).astype(BF16)

    def dot3(w, lo, hi_):
        w_hi = w.astype(BF16)
        w_lo = (w - w_hi.astype(F32)).astype(BF16)
        h_hi, h_lo = h_ref[0, :, lo:hi_], h_ref[1, :, lo:hi_]
        return (jnp.dot(w_hi, h_hi, preferred_element_type=F32) + jnp.dot(w_hi, h_lo, preferred_element_type=F32)
                + jnp.dot(w_lo, h_hi, preferred_element_type=F32))

    pos = lax.broadcasted_iota(jnp.int32, (1, length), 1)
    tf = pos.astype(F32) / float(length - 1)
    tb = (length - pos).astype(F32) / float(length - 1)
    kf = dot3(wf_ref[...], 0, length) * jnp.exp(-tf * df_ref[...])
    kb = dot3(wb_ref[...], length, n_fft) * jnp.exp(-tb * db_ref[...])
    kb = jnp.where(pos == 0, 0.0, kb)
    inv = 1.0 / (jnp.sum(jnp.abs(kf), axis=1, keepdims=True) + jnp.sum(jnp.abs(kb), axis=1, keepdims=True))
    k_ref[:, :length] = kf * inv
    k_ref[:, length:] = kb * inv
    cb = wf_ref.shape[0]
    for n1 in range(FILTER_TILE_ROWS):
        row = k_ref[:, n1 * FFT_N2:(n1 + 1) * FFT_N2] if n1 < FFT_N1 else jnp.zeros((cb, FFT_N2), F32)
        o_ref[pl.ds(n1, cb, stride=FILTER_TILE_ROWS), :] = row


def _filter_time(length, w1, b1, w2, b2, w3, b3, freq, w_out, cb=128):
    col = lambda v: v.reshape(-1, 1)
    w1t = w1.T
    n_ch = w_out.shape[1]
    deltas = jnp.abs(jnp.linspace(math.log(HY_TARGET) / HY_SLOW_PCT, math.log(HY_TARGET) / HY_FAST_PCT,
                                  n_ch, dtype=F32)).reshape(n_ch, 1)
    wot = w_out.T
    nb = D_HY // cb
    small = lambda a: pl.BlockSpec(a.shape, lambda i: (0,) * a.ndim)
    fwd = lambda i: ((i // nb) * HY_DIRS * nb + i % nb, 0)
    bwd = lambda i: ((i // nb) * HY_DIRS * nb + nb + i % nb, 0)
    ins = [w1t[:, 0:1], w1t[:, 1:1 + HY_BANDS], w1t[:, 1 + HY_BANDS:], col(b1), w2.T, col(b2), w3.T, col(b3),
           col(freq)]
    hy_ff = w2.shape[0]
    return pl.pallas_call(
        functools.partial(_filter_time_kernel, length=length),
        grid=(HY_ORDER * nb,),
        in_specs=[small(a) for a in ins] + [pl.BlockSpec((cb, hy_ff), fwd), pl.BlockSpec((cb, hy_ff), bwd),
                                            pl.BlockSpec((cb, 1), fwd), pl.BlockSpec((cb, 1), bwd)],
        out_specs=pl.BlockSpec((cb * FILTER_TILE_ROWS, FFT_N2), lambda i: (i, 0)),
        out_shape=jax.ShapeDtypeStruct((HY_ORDER * D_HY * FILTER_TILE_ROWS, FFT_N2), F32),
        scratch_shapes=[pltpu.VMEM((2, hy_ff, 2 * length), BF16), pltpu.VMEM((cb, 2 * length), F32)],
        compiler_params=pltpu.CompilerParams(
            dimension_semantics=("arbitrary",), vmem_limit_bytes=VMEM_LIMIT_BYTES),
        name="hyena_filter_time",
    )(*ins, wot, wot, deltas, deltas)


def _filter_spec_kernel(k_ref, w1p_ref, c2_ref, s2_ref, f2_ref, o_ref):
    cb = k_ref.shape[0] // FILTER_TILE_ROWS
    kt = k_ref[...].reshape(cb, FILTER_TILE_ROWS, FFT_N2)[:, :FFT_N1, :]
    xp = jnp.swapaxes(kt.reshape(cb // 2, 2 * FFT_N1, FFT_N2), 1, 2)
    spec = _fwd_spectrum(xp, w1p_ref[...], c2_ref[...], s2_ref[...], f2_ref[...])
    for par in range(2):
        o_ref[par, :, :FFT_N1, :] = spec[par][0]
        o_ref[par, :, FFT_N1:, :] = spec[par][1]


def _filter_spectrum(kt, cb=128):
    consts = _dft_constants(real_input=True)[:4]
    nb = D_HY // cb
    full = lambda a: pl.BlockSpec(a.shape, lambda i: (0,) * a.ndim)
    return pl.pallas_call(
        _filter_spec_kernel,
        grid=(HY_ORDER * nb,),
        in_specs=[pl.BlockSpec((cb * FILTER_TILE_ROWS, FFT_N2), lambda i: (i, 0))] + [full(a) for a in consts],
        out_specs=pl.BlockSpec((None, 2, cb // 2, 2 * FFT_N1, FFT_N2), lambda i: (i // nb, 0, i % nb, 0, 0)),
        out_shape=jax.ShapeDtypeStruct((HY_ORDER, 2, D_HY // 2, 2 * FFT_N1, FFT_N2), F32),
        compiler_params=pltpu.CompilerParams(
            dimension_semantics=("parallel",), vmem_limit_bytes=VMEM_LIMIT_BYTES),
        name="hyena_filter_spectrum",
    )(kt, *consts)


def _hyena(us, w1, b1, w2, b2, w3, b3, freq, w_out, bias):
    bsz, length, _ = us.shape
    assert 2 * length == FFT_N and bsz == 2, "one complex transform carries exactly two batch rows"
    kf = _filter_spectrum(_filter_time(length, w1, b1, w2, b2, w3, b3, freq, w_out))
    biasp = jnp.repeat(bias.reshape(HY_ORDER, D_HY // 2, 1, 2), FFT_N1, axis=-1)
    return _hyena_conv(us.reshape(bsz * length, -1), kf, biasp).reshape(bsz, length, D_HY)


def kernel(x, c, ctx, c_ctx, w_ada, b_ada, norm_g, ffn_w_gate, ffn_w_up, ffn_w_down, w_in,
           s5_lam_re, s5_lam_im, s5_log_dt, s5_b_re, s5_b_im, s5_c_re, s5_c_im, s5_d,
           hy_short_w, hy_short_b, hy_w1, hy_b1, hy_w2, hy_b2, hy_w3, hy_b3, hy_freq, hy_w_out,
           hy_bias, w_pa, w_pb, w_out, final_g):
    bsz, seq, d = x.shape
    ctx_len = ctx.shape[1]
    n_rows = seq // GRID_W
    depth = w_ada.shape[0]
    assert depth == 1, "context-token outputs are only dropped by the last layer"
    l = 0

    c_rows = jnp.concatenate([c, c_ctx[None, :], jnp.zeros((8 - bsz - 1, d), F32)], axis=0)
    mod_all = _ada_mod(c_rows, w_ada[l], b_ada[l])
    mod = mod_all[:bsz].reshape(bsz, N_SUB, N_MOD, 1, d)
    mod_c = mod_all[bsz:bsz + 1].reshape(1, N_SUB, N_MOD, 1, d)

    def mods(m, sub):
        return tuple(m[:, sub, k] for k in range(N_MOD))

    wg, wu, wd = ffn_w_gate[l], ffn_w_up[l], ffn_w_down[l]
    w_in_b = w_in[l]

    xt = x.reshape(bsz * seq, d)
    ct = ctx.reshape(bsz * ctx_len, d)

    xt = _ffn_sublayer(xt, mods(mod, 0), norm_g[l, 0], wg, wu, wd, 0)
    ct = _ffn_sublayer(ct, mods(mod_c, 0), norm_g[l, 0], wg, wu, wd, 0)

    assert GRID_W * n_rows == seq
    u_s5, us_hy, sig_gates = _in_proj(xt, mod[:, 1, 0], mod[:, 1, 1], norm_g[l, 1], w_in_b,
                                      hy_short_w[l], hy_short_b[l], n_u=I_HY, n_hy=I_GA - I_HY)
    (u_ctx,) = _in_proj(ct, mod_c[:, 1, 0], mod_c[:, 1, 1], norm_g[l, 1], w_in_b[:, :D_S5])

    y_s5 = _s5_bidirectional(u_s5.reshape(bsz, seq, D_S5), u_ctx.reshape(bsz, ctx_len, D_S5),
                             s5_lam_re[l], s5_lam_im[l], s5_log_dt[l],
                             s5_b_re[l], s5_b_im[l], s5_c_re[l], s5_c_im[l], s5_d[l])
    y_hy = _hyena(us_hy.reshape(bsz, seq, I_GA - I_HY),
                  hy_w1[l], hy_b1[l], hy_w2[l], hy_b2[l], hy_w3[l], hy_b3[l], hy_freq[l],
                  hy_w_out[l], hy_bias[l])

    xt = _merge(xt, mod[:, 1, 2], y_s5.reshape(bsz * seq, D_S5), y_hy.reshape(bsz * seq, D_HY),
                sig_gates, w_pa[l].astype(BF16), w_pb[l].astype(BF16), w_out[l].astype(BF16))

    xt = _ffn_sublayer(xt, mods(mod, 2), norm_g[l, 2], wg, wu, wd, 1, final_gain=final_g)
    return xt.reshape(bsz, seq, d)
```

```python
import functools
import math

import jax
import jax.numpy as jnp
import numpy as np
from jax import lax
from jax.experimental import pallas as pl
from jax.experimental.pallas import tpu as pltpu

F32 = jnp.float32
BF16 = jnp.bfloat16

D_MODEL = 2048
GRID_W = 64
D_S5 = 1024
S5_GROUP = 16
S5_GROUPS = D_S5 // S5_GROUP
S5_STATE = 64
S5_DIRS = 2
LAMBDA_RE_MAX = -1e-4
S5_CHUNK = 16
LANE = 128
S5_GROUPS_PER_COL = LANE // S5_GROUP
D_HY = 1024
HY_ORDER = 2
HY_DIRS = 2
HY_SHORT = 3
HY_EMB = 33
HY_BANDS = (HY_EMB - 1) // 2
HY_TARGET = 1e-2
HY_FAST_PCT = 0.3
HY_SLOW_PCT = 1.5
FFT_N1 = 64
FFT_N2 = 128
FFT_N = FFT_N1 * FFT_N2
FILTER_TILE_ROWS = FFT_N1 + 8
HY_TILE_PITCH = FILTER_TILE_ROWS
HY_PAIRS_PER_COL = LANE // 2
HY_PAIRS_PER_STEP = 16
I_HY = D_S5
I_GA = D_S5 + (HY_ORDER + 1) * D_HY
I_GB = I_GA + D_MODEL
D_IN = I_GB + D_MODEL
D_FF = 5632
N_SUB = 3
N_MOD = 3
HALF_STEP = 0.5
FFN_ROW_PARTS = 4
RMS_EPS = 1e-6

VMEM_LIMIT_BYTES = 58 * 1024 * 1024


def _rms_mod(x, gain, shift, scale):
    ms = jnp.mean(x * x, axis=-1, keepdims=True)
    y = x * lax.rsqrt(ms + RMS_EPS) * gain
    return y * (1.0 + scale) + shift


def _split_bf16(v):
    hi = v.astype(BF16)
    return hi, (v - hi.astype(F32)).astype(BF16)


def _ada_kernel(c_ref, w_ref, b_ref, o_ref):
    c = c_ref[...]
    a_hi, a_lo = _split_bf16(c * jax.nn.sigmoid(c))
    w_hi, w_lo = _split_bf16(w_ref[...])
    o_ref[...] = (jnp.dot(a_hi, w_hi, preferred_element_type=F32) + jnp.dot(a_lo, w_hi, preferred_element_type=F32)
                  + jnp.dot(a_hi, w_lo, preferred_element_type=F32) + b_ref[...])


def _ada_mod(c_rows, w, b, tn=1024):
    rows, d = c_rows.shape
    n = w.shape[1]
    return pl.pallas_call(
        _ada_kernel,
        grid=(n // tn,),
        in_specs=[pl.BlockSpec((rows, d), lambda j: (0, 0)),
                  pl.BlockSpec((d, tn), lambda j: (0, j)),
                  pl.BlockSpec((1, tn), lambda j: (0, j))],
        out_specs=pl.BlockSpec((rows, tn), lambda j: (0, j)),
        out_shape=jax.ShapeDtypeStruct((rows, n), F32),
        compiler_params=pltpu.CompilerParams(
            dimension_semantics=("arbitrary",), vmem_limit_bytes=VMEM_LIMIT_BYTES),
        name="ada_mod",
    )(c_rows, w, b.reshape(1, n))


def _ffn_kernel(x_ref, shift_ref, scale_ref, gate_ref, gain_ref, wg_ref, wu_ref, wd_ref,
                fg_ref, o_ref, h_ref, *, final_norm):
    j = pl.program_id(1)
    tm = x_ref.shape[0]

    def swiglu_down(h):
        g = jnp.dot(h, wg_ref[...].astype(BF16), preferred_element_type=F32)
        u = jnp.dot(h, wu_ref[...].astype(BF16), preferred_element_type=F32)
        a = (g * jax.nn.sigmoid(g) * u).astype(BF16)
        return jnp.dot(a, wd_ref[...].astype(BF16), preferred_element_type=F32)

    @pl.when(j == 0)
    def _():
        for r in range(0, tm, tm // FFN_ROW_PARTS):
            rows = slice(r, r + tm // FFN_ROW_PARTS)
            h = _rms_mod(x_ref[rows, :], gain_ref[...], shift_ref[...], scale_ref[...]).astype(BF16)
            h_ref[rows, :] = h
            o_ref[rows, :] = swiglu_down(h)

    @pl.when(j > 0)
    def _():
        o_ref[...] += swiglu_down(h_ref[...])

    @pl.when(j == pl.num_programs(1) - 1)
    def _():
        y = x_ref[...] + (HALF_STEP * gate_ref[...]) * o_ref[...]
        if final_norm:
            ms = jnp.mean(y * y, axis=-1, keepdims=True)
            y = y * lax.rsqrt(ms + RMS_EPS) * fg_ref[...]
        o_ref[...] = y


def _ffn_sublayer(x, mods, gain, wg, wu, wd, which, final_gain=None, tm=1024, tf=256):
    t, d = x.shape
    bm = mods[0].shape[0]
    tm = min(tm, t // bm)
    blocks_per_batch = (t // bm) // tm
    dff = wg.shape[2]
    final_norm = final_gain is not None
    fg = final_gain if final_norm else gain
    mod_spec = pl.BlockSpec((None, 1, d), lambda i, j: (i // blocks_per_batch, 0, 0))
    vec_spec = pl.BlockSpec((1, d), lambda i, j: (0, 0))
    return pl.pallas_call(
        functools.partial(_ffn_kernel, final_norm=final_norm),
        grid=(t // tm, dff // tf),
        in_specs=[pl.BlockSpec((tm, d), lambda i, j: (i, 0)),
                  mod_spec, mod_spec, mod_spec, vec_spec,
                  pl.BlockSpec((None, d, tf), lambda i, j: (which, 0, j)),
                  pl.BlockSpec((None, d, tf), lambda i, j: (which, 0, j)),
                  pl.BlockSpec((None, tf, d), lambda i, j: (which, j, 0)),
                  vec_spec],
        out_specs=pl.BlockSpec((tm, d), lambda i, j: (i, 0)),
        out_shape=jax.ShapeDtypeStruct((t, d), F32),
        scratch_shapes=[pltpu.VMEM((tm, d), BF16)],
        compiler_params=pltpu.CompilerParams(
            dimension_semantics=("parallel", "arbitrary"), vmem_limit_bytes=VMEM_LIMIT_BYTES),
        name="ffn_final" if final_norm else "ffn",
    )(x, *mods, gain.reshape(1, d), wg, wu, wd, fg.reshape(1, d))


def _proj_kernel(x_ref, shift_ref, scale_ref, gain_ref, w_ref, sw_ref, sb_ref, *rest, n_u, n_hy, row_len, part):
    o_refs, h_ref = rest[:-1], rest[-1]
    j = pl.program_id(1)

    tm, tn = h_ref.shape[0], w_ref.shape[1]

    def first_block(o_ref):
        for r in range(0, tm, tm // 2):
            rows = slice(r, r + tm // 2)
            h = _rms_mod(x_ref[rows, :], gain_ref[...], shift_ref[...], scale_ref[...]).astype(BF16)
            h_ref[rows, :] = h
            for c in range(0, tn, part):
                p = jnp.dot(h, w_ref[:, c:c + part].astype(BF16), preferred_element_type=F32)
                o_ref[rows, c:c + part] = p.astype(o_ref.dtype)

    def in_parts(o_ref, epilogue):
        for r in range(0, tm, tm // 2):
            rows = slice(r, r + tm // 2)
            h = h_ref[rows, :]
            for c in range(0, tn, part):
                p = jnp.dot(h, w_ref[:, c:c + part].astype(BF16), preferred_element_type=F32)
                o_ref[rows, c:c + part] = epilogue(p, c).astype(o_ref.dtype)

    def short_conv(p, c):
        col = lax.broadcasted_iota(jnp.int32, p.shape, 0) % row_len
        prev = jnp.where(col == 0, 0.0, pltpu.roll(p, 1, 0))
        nxt = jnp.where(col == row_len - 1, 0.0, pltpu.roll(p, p.shape[0] - 1, 0))
        sw = sw_ref[:, c:c + part]
        return sb_ref[:, c:c + part] + prev * sw[0:1] + p * sw[1:2] + nxt * sw[2:3]

    @pl.when(j == 0)
    def _():
        first_block(o_refs[0])

    if n_hy == 0:
        @pl.when(j > 0)
        def _():
            in_parts(o_refs[0], lambda p, c: p)
        return

    @pl.when((j > 0) & (j < n_u))
    def _():
        in_parts(o_refs[0], lambda p, c: p)

    @pl.when((j >= n_u) & (j < n_u + n_hy))
    def _():
        in_parts(o_refs[1], short_conv)

    @pl.when(j >= n_u + n_hy)
    def _():
        in_parts(o_refs[2], lambda p, c: jax.nn.sigmoid(p))


def _in_proj(x, shift, scale, gain, w, short_w=None, short_b=None, n_u=D_S5, n_hy=0, row_len=GRID_W,
             tm=1024, tn=512, part=256):
    t, d = x.shape
    bm = shift.shape[0]
    tm = min(tm, t // bm)
    blocks_per_batch = (t // bm) // tm
    n = w.shape[1]
    assert (tm // 2) % row_len == 0 and (t // bm) % tm == 0
    bu, bh = n_u // tn, n_hy // tn
    bg = n // tn - bu - bh
    mod_spec = pl.BlockSpec((None, 1, d), lambda i, j: (i // blocks_per_batch, 0, 0))
    out_shape = [jax.ShapeDtypeStruct((t, n_u), F32)]
    out_specs = [pl.BlockSpec((tm, tn), lambda i, j: (i, jnp.minimum(j, bu - 1)))]
    if bh:
        out_shape += [jax.ShapeDtypeStruct((t, n_hy), BF16), jax.ShapeDtypeStruct((t, bg * tn), BF16)]
        out_specs += [pl.BlockSpec((tm, tn), lambda i, j: (i, jnp.clip(j - bu, 0, bh - 1))),
                      pl.BlockSpec((tm, tn), lambda i, j: (i, jnp.maximum(j - bu - bh, 0)))]
        sw, sb = short_w, short_b.reshape(1, n_hy)
        hy_blk = lambda i, j: (0, jnp.clip(j - bu, 0, bh - 1))
    else:
        sw, sb = jnp.zeros((HY_SHORT, tn), F32), jnp.zeros((1, tn), F32)
        hy_blk = lambda i, j: (0, 0)
    return pl.pallas_call(
        functools.partial(_proj_kernel, n_u=bu, n_hy=bh, row_len=row_len, part=part),
        grid=(t // tm, n // tn),
        in_specs=[pl.BlockSpec((tm, d), lambda i, j: (i, 0)),
                  mod_spec, mod_spec,
                  pl.BlockSpec((1, d), lambda i, j: (0, 0)),
                  pl.BlockSpec((d, tn), lambda i, j: (0, j)),
                  pl.BlockSpec((HY_SHORT, tn), hy_blk),
                  pl.BlockSpec((1, tn), hy_blk)],
        out_specs=out_specs,
        out_shape=out_shape,
        scratch_shapes=[pltpu.VMEM((tm, d), BF16)],
        compiler_params=pltpu.CompilerParams(
            dimension_semantics=("parallel", "arbitrary"), vmem_limit_bytes=VMEM_LIMIT_BYTES),
        name="in_proj",
    )(x, shift, scale, gain.reshape(1, d), w, sw, sb)


def _gelu_tanh(x):
    return 0.5 * x * (1.0 + jnp.tanh(math.sqrt(2.0 / math.pi) * (x + 0.044715 * (x * x * x))))


def _merge_kernel(x_ref, gate_ref, ys_ref, yh_ref, ga_ref, gb_ref, wpa_lo_ref, wpa_hi_ref, wpb_ref,
                  wout_ref, o_ref, s_ref, acc_ref):
    j = pl.program_id(1)

    @pl.when(j == 0)
    def _():
        s_ref[...] = _gelu_tanh(ys_ref[...].astype(F32)).astype(BF16)
        acc_ref[...] = jnp.zeros_like(acc_ref)

    s = s_ref[...]
    pa_lo = jnp.dot(s, wpa_lo_ref[...], preferred_element_type=F32)
    pa_hi = jnp.dot(s, wpa_hi_ref[...], preferred_element_type=F32)
    y_a = pa_lo * jax.nn.sigmoid(pa_hi)
    y_b = jnp.dot(yh_ref[...], wpb_ref[...], preferred_element_type=F32)
    m = ga_ref[...].astype(F32) * y_a + gb_ref[...].astype(F32) * y_b
    acc_ref[...] += jnp.dot(m.astype(BF16), wout_ref[...], preferred_element_type=F32)

    @pl.when(j == pl.num_programs(1) - 1)
    def _():
        o_ref[...] = x_ref[...] + gate_ref[...] * acc_ref[...]


def _merge(x, gate, y_s5, y_hy, sig_gates, w_pa, w_pb, w_out, tm=512, tn=1024):
    t, d = x.shape
    bm = gate.shape[0]
    blocks_per_batch = (t // bm) // tm
    nj = d // tn
    ds5 = y_s5.shape[1]
    dhy = y_hy.shape[1]
    return pl.pallas_call(
        _merge_kernel,
        grid=(t // tm, nj),
        in_specs=[pl.BlockSpec((tm, d), lambda i, j: (i, 0)),
                  pl.BlockSpec((None, 1, d), lambda i, j: (i // blocks_per_batch, 0, 0)),
                  pl.BlockSpec((tm, ds5), lambda i, j: (i, 0)),
                  pl.BlockSpec((tm, dhy), lambda i, j: (i, 0)),
                  pl.BlockSpec((tm, tn), lambda i, j: (i, j)),
                  pl.BlockSpec((tm, tn), lambda i, j: (i, nj + j)),
                  pl.BlockSpec((ds5, tn), lambda i, j: (0, j)),
                  pl.BlockSpec((ds5, tn), lambda i, j: (0, nj + j)),
                  pl.BlockSpec((dhy, tn), lambda i, j: (0, j)),
                  pl.BlockSpec((tn, d), lambda i, j: (j, 0))],
        out_specs=pl.BlockSpec((tm, d), lambda i, j: (i, 0)),
        out_shape=jax.ShapeDtypeStruct((t, d), F32),
        scratch_shapes=[pltpu.VMEM((tm, ds5), BF16), pltpu.VMEM((tm, d), F32)],
        compiler_params=pltpu.CompilerParams(
            dimension_semantics=("parallel", "arbitrary"), vmem_limit_bytes=VMEM_LIMIT_BYTES),
        name="merge",
    )(x, gate, y_s5, y_hy, sig_gates, sig_gates, w_pa, w_pa, w_pb, w_out)


def _s5_weights_kernel(*refs, n_steps):
    blocks, ca_ref = refs[:-1], refs[-1]

    def one(gi, carry):
        _s5_weights_group(*[r.at[gi] for r in blocks], ca_ref, n_steps=n_steps)
        return carry

    lax.fori_loop(0, blocks[0].shape[0], one, 0)


def _s5_weights_group(par_ref, bt_ref, c_ref, d_ref, bpow_ref, toep_ref, cpow_ref, tab_ref, ca_ref, *, n_steps):
    t, h, p = S5_CHUNK, S5_GROUP, S5_STATE
    lanes = 2 * p
    hp = lax.Precision.HIGHEST
    sgn = jnp.where(lax.broadcasted_iota(jnp.int32, (1, lanes), 1) < p, -1.0, 1.0)
    par = par_ref[...]
    gsum = None
    for d in range(S5_DIRS):
        lr = jnp.minimum(par[3 * d:3 * d + 1], LAMBDA_RE_MAX)
        li = par[3 * d + 1:3 * d + 2]
        dt = jnp.exp(par[3 * d + 2:3 * d + 3])
        zr, zi = lr * dt, li * dt

        def apow(j):
            mag = jnp.exp(j * zr)
            return mag * jnp.cos(j * zi), sgn * (mag * jnp.sin(j * zi))

        def cmul(x, a1, a2):
            return x * a1 + pltpu.roll(x, p, 1) * a2

        a1, a2 = apow(lax.broadcasted_iota(jnp.int32, (t + 1, 1), 0).astype(F32))
        nr, ni = a1[1:2] - 1.0, sgn * a2[1:2]
        den = lr * lr + li * li
        f_re = (nr * lr + ni * li) / den
        f_im = (ni * lr - nr * li) / den
        bbar = cmul(bt_ref[d], f_re, sgn * f_im)
        cc = c_ref[d]
        ca = [cmul(cc, a1[j:j + 1], a2[j:j + 1]) * (-sgn) for j in range(t + 1)]
        ca_ref[...] = jnp.zeros_like(ca_ref)
        for k in range(t):
            e_b, e_c = (t - 1 - k, k + 1) if d == 0 else (k, t - k)
            bpow_ref[k * h:(k + 1) * h, d * lanes:(d + 1) * lanes] = (
                cmul(bbar, a1[e_b:e_b + 1], a2[e_b:e_b + 1]).astype(BF16))
            cpow_ref[k * h:(k + 1) * h, d * lanes:(d + 1) * lanes] = ca[e_c].astype(BF16)
            l = t - 1 + k if d == 0 else t - 1 - k
            ca_ref[l * h:(l + 1) * h, :] = ca[k]
        g = lax.dot_general(bbar, ca_ref[...], (((1,), (1,)), ((), ())), preferred_element_type=F32, precision=hp)
        gsum = g if gsum is None else gsum + g
        for s in range(n_steps):
            s1, s2 = apow(float(t * 2 ** s))
            r = d * 2 * n_steps + 2 * s
            tab_ref[r:r + 1, :] = s1
            tab_ref[r + 1:r + 2, :] = s2
    wide = gsum.shape[1]
    col = lax.broadcasted_iota(jnp.int32, (h, wide), 1)
    row = lax.broadcasted_iota(jnp.int32, (h, wide), 0)
    gsum = gsum + jnp.where(col - (t - 1) * h == row, d_ref[...], 0.0)
    for k in range(t):
        off = (t - 1 - k) * h
        shifted = gsum if off == 0 else pltpu.roll(gsum, wide - off, 1)
        toep_ref[k * h:(k + 1) * h, :] = shifted[:, :t * h].astype(BF16)


def _s5_weights(lam_re, lam_im, log_dt, b_re, b_im, c_re, c_im, d_skip, n_steps):
    g, p, h, t = S5_GROUPS, S5_STATE, S5_GROUP, S5_CHUNK
    cat2 = lambda a: jnp.concatenate([a, a], axis=-1)
    par = jnp.stack([cat2(lam_re), cat2(lam_im), jnp.broadcast_to(log_dt[..., None], (S5_DIRS, g, 2 * p))], axis=1)
    par = par.transpose(2, 0, 1, 3).reshape(g, 3 * S5_DIRS, 2 * p)
    btc = jnp.concatenate([b_re, b_im], axis=2).transpose(1, 0, 3, 2)
    ccat = jnp.concatenate([c_re, c_im], axis=3).transpose(1, 0, 2, 3)
    wide = 2 * t * h
    drow = jnp.zeros((g, 1, wide), F32).at[:, 0, (t - 1) * h:t * h].set(d_skip.reshape(g, h))
    w = t * h
    gb = S5_GROUPS_PER_COL
    sq = pl.BlockSpec((gb, w, w), lambda i: (i, 0, 0))
    return pl.pallas_call(
        functools.partial(_s5_weights_kernel, n_steps=n_steps),
        grid=(g // gb,),
        in_specs=[pl.BlockSpec((gb, 3 * S5_DIRS, 2 * p), lambda i: (i, 0, 0)),
                  pl.BlockSpec((gb, S5_DIRS, h, 2 * p), lambda i: (i, 0, 0, 0)),
                  pl.BlockSpec((gb, S5_DIRS, h, 2 * p), lambda i: (i, 0, 0, 0)),
                  pl.BlockSpec((gb, 1, wide), lambda i: (i, 0, 0))],
        out_specs=[sq, sq, sq, pl.BlockSpec((gb, 4 * n_steps, 2 * p), lambda i: (i, 0, 0))],
        out_shape=[jax.ShapeDtypeStruct((g, w, w), BF16)] * 3 + [jax.ShapeDtypeStruct((g, 4 * n_steps, 2 * p), F32)],
        scratch_shapes=[pltpu.VMEM((wide, 2 * p), F32)],
        compiler_params=pltpu.CompilerParams(
            dimension_semantics=("parallel",), vmem_limit_bytes=VMEM_LIMIT_BYTES),
        name="s5_weights",
    )(par, btc, ccat, drow)


def _s5_kernel(u_ref, uc_ref, bpow_ref, toep_ref, cpow_ref, tab_ref, y_ref, ucat_ref, ucc_ref, yacc_ref,
               *, bsz, n_steps):
    t, gw = S5_CHUNK, S5_GROUP
    per = LANE // gw
    n_lat = u_ref.shape[0] // (bsz * t)
    n_ctx = uc_ref.shape[0] // (bsz * t)
    n_ch = n_lat + n_ctx
    rows = bsz * n_ch
    half = 2 * S5_STATE
    for k in range(t):
        ucat_ref[:, k * LANE:(k + 1) * LANE] = u_ref[pl.ds(k, bsz * n_lat, stride=t), :]
        ucc_ref[:, k * LANE:(k + 1) * LANE] = uc_ref[pl.ds(k, bsz * n_ctx, stride=t), :]
    yacc_ref[...] = jnp.zeros_like(yacc_ref)
    rib = lax.broadcasted_iota(jnp.int32, (rows, half), 0) % n_ch
    lane = lax.broadcasted_iota(jnp.int32, (1, LANE), 1)

    def cmul_add(acc, sh, a1, a2):
        return acc + a1 * sh + a2 * pltpu.roll(sh, S5_STATE, 1)

    def gather(src_ref, gl):
        cols = []
        for j in range(t // per):
            acc = None
            for tt in range(per):
                k = j * per + tt
                r = pltpu.roll(src_ref[:, k * LANE:(k + 1) * LANE], (gw * tt - gw * gl) % LANE, 1)
                acc = r if acc is None else jnp.where((lane >= gw * tt) & (lane < gw * (tt + 1)), r, acc)
            cols.append(acc)
        return jnp.concatenate(cols, axis=1).astype(BF16)

    def group(gl, carry):
        ul = gather(ucat_ref, gl)
        uc = gather(ucc_ref, gl)
        bpow = bpow_ref[gl]
        zl = jnp.dot(ul, bpow, preferred_element_type=F32)
        zc = jnp.dot(uc, bpow, preferred_element_type=F32)
        fparts, bparts = [], []
        for b in range(bsz):
            lat = slice(b * n_lat, (b + 1) * n_lat)
            ctx = slice(b * n_ctx, (b + 1) * n_ctx)
            fparts += [zc[ctx, :half], zl[lat, :half]]
            bparts += [zl[lat, half:], zc[ctx, half:]]
        fw = jnp.concatenate(fparts, axis=0)
        bw = jnp.concatenate(bparts, axis=0)
        tab = tab_ref[gl]
        for s in range(n_steps):
            d = 1 << s
            sh = jnp.where(rib >= d, pltpu.roll(fw, d, 0), 0.0)
            fw = cmul_add(fw, sh, tab[2 * s:2 * s + 1], tab[2 * s + 1:2 * s + 2])
            o = 2 * n_steps
            sh = jnp.where(rib < n_ch - d, pltpu.roll(bw, rows - d, 0), 0.0)
            bw = cmul_add(bw, sh, tab[o + 2 * s:o + 2 * s + 1], tab[o + 2 * s + 1:o + 2 * s + 2])
        fe = jnp.where(rib >= 1, pltpu.roll(fw, 1, 0), 0.0)
        be = jnp.where(rib < n_ch - 1, pltpu.roll(bw, rows - 1, 0), 0.0)
        fl = jnp.concatenate([fe[b * n_ch + n_ctx:(b + 1) * n_ch] for b in range(bsz)], axis=0)
        bl = jnp.concatenate([be[b * n_ch:b * n_ch + n_lat] for b in range(bsz)], axis=0)
        st = jnp.concatenate([fl, bl], axis=1).astype(BF16)
        y = (jnp.dot(ul, toep_ref[gl], preferred_element_type=F32)
             + lax.dot_general(st, cpow_ref[gl], (((1,), (1,)), ((), ())), preferred_element_type=F32))
        mine = (lane >= gw * gl) & (lane < gw * (gl + 1))
        for i in range(t):
            src = y[:, (i // per) * LANE:(i // per + 1) * LANE]
            r = pltpu.roll(src, (gw * gl - gw * (i % per)) % LANE, 1)
            blk = slice(i * LANE, (i + 1) * LANE)
            yacc_ref[:, blk] = jnp.where(mine, r, yacc_ref[:, blk])
        return carry

    lax.fori_loop(0, S5_GROUPS_PER_COL, group, 0)
    for i in range(t):
        y_ref[pl.ds(i, bsz * n_lat, stride=t), :] = yacc_ref[:, i * LANE:(i + 1) * LANE]


def _s5_mix(u, uc, bpw, toep, cpw, tab, bsz):
    rl, dm = u.shape
    rc = uc.shape[0]
    t = S5_CHUNK
    w = t * S5_GROUP
    n_steps = tab.shape[1] // 4
    gpc = S5_GROUPS_PER_COL
    wspec = pl.BlockSpec((gpc, w, w), lambda i: (i, 0, 0))
    return pl.pallas_call(
        functools.partial(_s5_kernel, bsz=bsz, n_steps=n_steps),
        grid=(dm // LANE,),
        in_specs=[pl.BlockSpec((rl, LANE), lambda i: (0, i)),
                  pl.BlockSpec((rc, LANE), lambda i: (0, i)),
                  wspec, wspec, wspec,
                  pl.BlockSpec((gpc, 4 * n_steps, tab.shape[2]), lambda i: (i, 0, 0))],
        out_specs=pl.BlockSpec((rl, LANE), lambda i: (0, i)),
        out_shape=jax.ShapeDtypeStruct(u.shape, F32),
        scratch_shapes=[pltpu.VMEM((rl // t, t * LANE), F32), pltpu.VMEM((rc // t, t * LANE), F32),
                        pltpu.VMEM((rl // t, t * LANE), F32)],
        compiler_params=pltpu.CompilerParams(
            dimension_semantics=("parallel",), vmem_limit_bytes=VMEM_LIMIT_BYTES),
        name="s5_mix",
    )(u, uc, bpw, toep, cpw, tab)


def _s5_bidirectional(u, u_ctx, lam_re, lam_im, log_dt, b_re, b_im, c_re, c_im, d_skip):
    bsz, length, dm = u.shape
    ctx_len = u_ctx.shape[1]
    t = S5_CHUNK
    n_lat, n_ctx = length // t, ctx_len // t
    n_steps = max(1, math.ceil(math.log2(n_lat + n_ctx)))
    bpw, toep, cpw, tab = _s5_weights(lam_re, lam_im, log_dt, b_re, b_im, c_re, c_im, d_skip, n_steps)
    y = _s5_mix(u.reshape(bsz * length, dm), u_ctx.reshape(bsz * ctx_len, dm), bpw, toep, cpw, tab, bsz)
    return y.reshape(bsz, length, dm)


def _dft_constants(real_input=False):
    n1 = np.arange(FFT_N1)
    n2 = np.arange(FFT_N2)
    half = FFT_N1 // 2
    th = 2 * np.pi * np.outer(n1, n1) / FFT_N1
    c1, s1 = np.cos(th), np.sin(th)
    if real_input:
        w1 = np.concatenate([c1, -s1], axis=1)
    else:
        w1 = np.concatenate([np.concatenate([c1[:half], -s1[:half]], axis=1),
                             np.concatenate([s1[:half], c1[:half]], axis=1)], axis=0)
    z = np.zeros_like(w1)
    w1p = np.block([[w1, z], [z, w1]])
    ph = 2 * np.pi * np.outer(n2, n1) / FFT_N
    ps = 2 * np.pi * np.outer(n2, n2) / FFT_N2
    f2 = np.concatenate([np.cos(ps), -np.sin(ps)], axis=1)
    g2 = np.concatenate([np.cos(ps), np.sin(ps)], axis=1)
    c2, s2 = np.cos(ph).T, np.sin(ph).T
    wi = np.concatenate([np.concatenate([c1[:, :half], s1[:, :half]], axis=1),
                         np.concatenate([-s1[:, :half], c1[:, :half]], axis=1)], axis=0) / FFT_N
    zi = np.zeros_like(wi)
    wi2 = np.stack([np.concatenate([wi, zi], axis=1), np.concatenate([zi, wi], axis=1)])
    as_b = lambda a: jnp.asarray(a, F32).astype(BF16)
    as_f = lambda a: jnp.asarray(a, F32)
    return [as_b(w1p), as_f(c2), as_f(s2), as_b(f2), as_b(g2), as_b(wi2)]


def _fwd_spectrum(xp, w1p, c2, s2, f2):
    cp = xp.shape[0]
    hn = FFT_N1
    a = jnp.dot(xp.reshape(cp * FFT_N2, LANE).astype(BF16), w1p, preferred_element_type=F32)
    out = []
    for par in range(2):
        ap = a[:, par * LANE:(par + 1) * LANE].reshape(cp, FFT_N2, LANE)
        at = jnp.swapaxes(ap, 1, 2)
        are, aim = at[:, :hn], at[:, hn:]
        at = jnp.concatenate([are * c2 + aim * s2, aim * c2 - are * s2], axis=1)
        p = jnp.dot(at.reshape(cp * 2 * hn, FFT_N2).astype(BF16), f2, preferred_element_type=F32)
        p = p.reshape(cp, 2 * hn, 2 * FFT_N2)
        out.append((p[:, :hn, :FFT_N2] - p[:, hn:, FFT_N2:], p[:, :hn, FFT_N2:] + p[:, hn:, :FFT_N2]))
    return out


def _inv_time(yre, yim, g2, c2, s2, wi_par):
    cp = yre.shape[0]
    hn = FFT_N1
    y = jnp.concatenate([yre, yim], axis=1).reshape(cp * 2 * hn, FFT_N2).astype(BF16)
    q = jnp.dot(y, g2, preferred_element_type=F32).reshape(cp, 2 * hn, 2 * FFT_N2)
    bre = q[:, :hn, :FFT_N2] - q[:, hn:, FFT_N2:]
    bim = q[:, :hn, FFT_N2:] + q[:, hn:, :FFT_N2]
    b2 = jnp.concatenate([bre * c2 - bim * s2, bre * s2 + bim * c2], axis=1)
    bt = jnp.swapaxes(b2, 1, 2)
    return jnp.dot(bt.reshape(cp * FFT_N2, LANE).astype(BF16), wi_par, preferred_element_type=F32)


def _hyena_kernel(z_ref, g1_ref, g2_ref, kf_ref, bias_ref, w1p_ref, c2_ref, s2_ref, f2_ref, gi_ref, wi_ref,
                  o_ref, zt_ref, g1t_ref, g2t_ref, ot_ref, stage_ref):
    k = pl.program_id(1)
    n_s = z_ref.shape[0] // FFT_N2
    cp = HY_PAIRS_PER_STEP

    def to_tiles(x_ref, t_ref):
        for s in range(n_s):
            xs = x_ref[s * FFT_N2:(s + 1) * FFT_N2, :].astype(F32)
            stage_ref[pl.ds(s, LANE, stride=HY_TILE_PITCH), :] = xs.T
        st = stage_ref[...].reshape(HY_PAIRS_PER_COL, 2 * HY_TILE_PITCH, FFT_N2)
        both = jnp.concatenate([st[:, :n_s], st[:, HY_TILE_PITCH:HY_TILE_PITCH + n_s]], axis=1)
        t_ref[...] = jnp.swapaxes(both, 1, 2)

    @pl.when(k == 0)
    def _():
        to_tiles(z_ref, zt_ref)
        to_tiles(g1_ref, g1t_ref)
        to_tiles(g2_ref, g2t_ref)

    sl = pl.ds(pl.multiple_of(k * cp, cp), cp)
    z = zt_ref[sl]
    gates = (g1t_ref, g2t_ref)
    for o in range(HY_ORDER):
        spec = _fwd_spectrum(z, w1p_ref[...], c2_ref[...], s2_ref[...], f2_ref[...])
        conv = None
        for par in range(2):
            xre, xim = spec[par]
            kre = kf_ref[o, par, :, :FFT_N1, :]
            kim = kf_ref[o, par, :, FFT_N1:, :]
            part = _inv_time(xre * kre - xim * kim, xre * kim + xim * kre,
                             gi_ref[...], c2_ref[...], s2_ref[...], wi_ref[par])
            conv = part if conv is None else conv + part
        z = gates[o][sl] * (conv.reshape(cp, FFT_N2, LANE) + bias_ref[o] * z)
    ot_ref[sl] = z

    @pl.when(k == pl.num_programs(1) - 1)
    def _():
        back = jnp.swapaxes(ot_ref[...], 1, 2)
        for c in range(LANE):
            stage_ref[c * HY_TILE_PITCH:c * HY_TILE_PITCH + n_s, :] = back[c // 2, (c % 2) * n_s:(c % 2 + 1) * n_s, :]
        for s in range(n_s):
            rows = stage_ref[pl.ds(s, LANE, stride=HY_TILE_PITCH), :]
            o_ref[s * FFT_N2:(s + 1) * FFT_N2, :] = rows.T.astype(o_ref.dtype)


def _hyena_conv(us, kf, biasp):
    t, _ = us.shape
    consts = _dft_constants()
    ncol = D_HY // LANE
    nsub = HY_PAIRS_PER_COL // HY_PAIRS_PER_STEP
    full = lambda a: pl.BlockSpec(a.shape, lambda j, k: (0,) * a.ndim)
    nat = lambda off: pl.BlockSpec((t, LANE), lambda j, k: (0, off + j))
    tiles = pltpu.VMEM((HY_PAIRS_PER_COL, FFT_N2, LANE), F32)
    return pl.pallas_call(
        _hyena_kernel,
        grid=(ncol, nsub),
        in_specs=[nat(0), nat(ncol), nat(2 * ncol),
                  pl.BlockSpec((HY_ORDER, 2, HY_PAIRS_PER_STEP, 2 * FFT_N1, FFT_N2),
                               lambda j, k: (0, 0, j * nsub + k, 0, 0)),
                  pl.BlockSpec((HY_ORDER, HY_PAIRS_PER_STEP, 1, LANE), lambda j, k: (0, j * nsub + k, 0, 0))]
                 + [full(a) for a in consts],
        out_specs=pl.BlockSpec((t, LANE), lambda j, k: (0, j)),
        out_shape=jax.ShapeDtypeStruct((t, D_HY), us.dtype),
        scratch_shapes=[tiles, tiles, tiles, tiles, pltpu.VMEM((LANE * HY_TILE_PITCH, FFT_N2), F32)],
        compiler_params=pltpu.CompilerParams(
            dimension_semantics=("parallel", "arbitrary"), vmem_limit_bytes=VMEM_LIMIT_BYTES),
        name="hyena_conv",
    )(us, us, us, kf, biasp, *consts)


def _filter_time_kernel(w1t_ref, w1c_ref, w1s_ref, b1_ref, w2_ref, b2_ref, w3_ref, b3_ref, fr_ref,
                        wf_ref, wb_ref, df_ref, db_ref, o_ref, h_ref, k_ref, *, length):
    n_fft = 2 * length
    hp = lax.Precision.HIGHEST

    @pl.when(pl.program_id(0) == 0)
    def _():
        pos = lax.broadcasted_iota(jnp.int32, (1, n_fft), 1)
        lag = jnp.where(pos < length, pos, n_fft - pos).astype(F32)
        t = lag / float(length - 1)
        w = (2.0 * math.pi / length) * lag
        band_step = (HY_BANDS - 1 - 1e-4) / (HY_BANDS - 1)
        bands = 1e-4 + band_step * lax.broadcasted_iota(jnp.int32, (HY_BANDS, 1), 0).astype(F32)
        ang = bands * w
        fr = fr_ref[...]
        h = (w1t_ref[...] * t + jnp.dot(w1c_ref[...], jnp.cos(ang), preferred_element_type=F32, precision=hp)
             - jnp.dot(w1s_ref[...], jnp.sin(ang), preferred_element_type=F32, precision=hp))
        h = jnp.sin(fr * (h + b1_ref[...]))
        h = jnp.sin(fr * (jnp.dot(w2_ref[...], h, preferred_element_type=F32, precision=hp) + b2_ref[...]))
        h = jnp.sin(fr * (jnp.dot(w3_ref[...], h, preferred_element_type=F32, precision=hp) + b3_ref[...]))
        hi = h.astype(BF16)
        h_ref[0] = hi
        h_ref[1] = (h - hi.astype(F32)).astype(BF16)

    def dot3(w, lo, hi_):
        w_hi = w.astype(BF16)
        w_lo = (w - w_hi.astype(F32)).astype(BF16)
        h_hi, h_lo = h_ref[0, :, lo:hi_], h_ref[1, :, lo:hi_]
        return (jnp.dot(w_hi, h_hi, preferred_element_type=F32) + jnp.dot(w_hi, h_lo, preferred_element_type=F32)
                + jnp.dot(w_lo, h_hi, preferred_element_type=F32))

    pos = lax.broadcasted_iota(jnp.int32, (1, length), 1)
    tf = pos.astype(F32) / float(length - 1)
    tb = (length - pos).astype(F32) / float(length - 1)
    kf = dot3(wf_ref[...], 0, length) * jnp.exp(-tf * df_ref[...])
    kb = dot3(wb_ref[...], length, n_fft) * jnp.exp(-tb * db_ref[...])
    kb = jnp.where(pos == 0, 0.0, kb)
    inv = 1.0 / (jnp.sum(jnp.abs(kf), axis=1, keepdims=True) + jnp.sum(jnp.abs(kb), axis=1, keepdims=True))
    k_ref[:, :length] = kf * inv
    k_ref[:, length:] = kb * inv
    cb = wf_ref.shape[0]
    for n1 in range(FILTER_TILE_ROWS):
        row = k_ref[:, n1 * FFT_N2:(n1 + 1) * FFT_N2] if n1 < FFT_N1 else jnp.zeros((cb, FFT_N2), F32)
        o_ref[pl.ds(n1, cb, stride=FILTER_TILE_ROWS), :] = row


def _filter_time(length, w1, b1, w2, b2, w3, b3, freq, w_out, cb=128):
    col = lambda v: v.reshape(-1, 1)
    w1t = w1.T
    n_ch = w_out.shape[1]
    deltas = jnp.abs(jnp.linspace(math.log(HY_TARGET) / HY_SLOW_PCT, math.log(HY_TARGET) / HY_FAST_PCT,
                                  n_ch, dtype=F32)).reshape(n_ch, 1)
    wot = w_out.T
    nb = D_HY // cb
    small = lambda a: pl.BlockSpec(a.shape, lambda i: (0,) * a.ndim)
    fwd = lambda i: ((i // nb) * HY_DIRS * nb + i % nb, 0)
    bwd = lambda i: ((i // nb) * HY_DIRS * nb + nb + i % nb, 0)
    ins = [w1t[:, 0:1], w1t[:, 1:1 + HY_BANDS], w1t[:, 1 + HY_BANDS:], col(b1), w2.T, col(b2), w3.T, col(b3),
           col(freq)]
    hy_ff = w2.shape[0]
    return pl.pallas_call(
        functools.partial(_filter_time_kernel, length=length),
        grid=(HY_ORDER * nb,),
        in_specs=[small(a) for a in ins] + [pl.BlockSpec((cb, hy_ff), fwd), pl.BlockSpec((cb, hy_ff), bwd),
                                            pl.BlockSpec((cb, 1), fwd), pl.BlockSpec((cb, 1), bwd)],
        out_specs=pl.BlockSpec((cb * FILTER_TILE_ROWS, FFT_N2), lambda i: (i, 0)),
        out_shape=jax.ShapeDtypeStruct((HY_ORDER * D_HY * FILTER_TILE_ROWS, FFT_N2), F32),
        scratch_shapes=[pltpu.VMEM((2, hy_ff, 2 * length), BF16), pltpu.VMEM((cb, 2 * length), F32)],
        compiler_params=pltpu.CompilerParams(
            dimension_semantics=("arbitrary",), vmem_limit_bytes=VMEM_LIMIT_BYTES),
        name="hyena_filter_time",
    )(*ins, wot, wot, deltas, deltas)


def _filter_spec_kernel(k_ref, w1p_ref, c2_ref, s2_ref, f2_ref, o_ref):
    cb = k_ref.shape[0] // FILTER_TILE_ROWS
    kt = k_ref[...].reshape(cb, FILTER_TILE_ROWS, FFT_N2)[:, :FFT_N1, :]
    xp = jnp.swapaxes(kt.reshape(cb // 2, 2 * FFT_N1, FFT_N2), 1, 2)
    spec = _fwd_spectrum(xp, w1p_ref[...], c2_ref[...], s2_ref[...], f2_ref[...])
    for par in range(2):
        o_ref[par, :, :FFT_N1, :] = spec[par][0]
        o_ref[par, :, FFT_N1:, :] = spec[par][1]


def _filter_spectrum(kt, cb=128):
    consts = _dft_constants(real_input=True)[:4]
    nb = D_HY // cb
    full = lambda a: pl.BlockSpec(a.shape, lambda i: (0,) * a.ndim)
    return pl.pallas_call(
        _filter_spec_kernel,
        grid=(HY_ORDER * nb,),
        in_specs=[pl.BlockSpec((cb * FILTER_TILE_ROWS, FFT_N2), lambda i: (i, 0))] + [full(a) for a in consts],
        out_specs=pl.BlockSpec((None, 2, cb // 2, 2 * FFT_N1, FFT_N2), lambda i: (i // nb, 0, i % nb, 0, 0)),
        out_shape=jax.ShapeDtypeStruct((HY_ORDER, 2, D_HY // 2, 2 * FFT_N1, FFT_N2), F32),
        compiler_params=pltpu.CompilerParams(
            dimension_semantics=("parallel",), vmem_limit_bytes=VMEM_LIMIT_BYTES),
        name="hyena_filter_spectrum",
    )(kt, *consts)


def _hyena(us, w1, b1, w2, b2, w3, b3, freq, w_out, bias):
    bsz, length, _ = us.shape
    assert 2 * length == FFT_N and bsz == 2, "one complex transform carries exactly two batch rows"
    kf = _filter_spectrum(_filter_time(length, w1, b1, w2, b2, w3, b3, freq, w_out))
    biasp = jnp.repeat(bias.reshape(HY_ORDER, D_HY // 2, 1, 2), FFT_N1, axis=-1)
    return _hyena_conv(us.reshape(bsz * length, -1), kf, biasp).reshape(bsz, length, D_HY)


def kernel(x, c, ctx, c_ctx, w_ada, b_ada, norm_g, ffn_w_gate, ffn_w_up, ffn_w_down, w_in,
           s5_lam_re, s5_lam_im, s5_log_dt, s5_b_re, s5_b_im, s5_c_re, s5_c_im, s5_d,
           hy_short_w, hy_short_b, hy_w1, hy_b1, hy_w2, hy_b2, hy_w3, hy_b3, hy_freq, hy_w_out,
           hy_bias, w_pa, w_pb, w_out, final_g):
    bsz, seq, d = x.shape
    ctx_len = ctx.shape[1]
    n_rows = seq // GRID_W
    depth = w_ada.shape[0]
    assert depth == 1, "context-token outputs are only dropped by the last layer"
    l = 0

    c_rows = jnp.concatenate([c, c_ctx[None, :], jnp.zeros((8 - bsz - 1, d), F32)], axis=0)
    mod_all = _ada_mod(c_rows, w_ada[l], b_ada[l])
    mod = mod_all[:bsz].reshape(bsz, N_SUB, N_MOD, 1, d)
    mod_c = mod_all[bsz:bsz + 1].reshape(1, N_SUB, N_MOD, 1, d)

    def mods(m, sub):
        return tuple(m[:, sub, k] for k in range(N_MOD))

    wg, wu, wd = ffn_w_gate[l], ffn_w_up[l], ffn_w_down[l]
    w_in_b = w_in[l]

    xt = x.reshape(bsz * seq, d)
    ct = ctx.reshape(bsz * ctx_len, d)

    xt = _ffn_sublayer(xt, mods(mod, 0), norm_g[l, 0], wg, wu, wd, 0)
    ct = _ffn_sublayer(ct, mods(mod_c, 0), norm_g[l, 0], wg, wu, wd, 0)

    assert GRID_W * n_rows == seq
    u_s5, us_hy, sig_gates = _in_proj(xt, mod[:, 1, 0], mod[:, 1, 1], norm_g[l, 1], w_in_b,
                                      hy_short_w[l], hy_short_b[l], n_u=I_HY, n_hy=I_GA - I_HY)
    (u_ctx,) = _in_proj(ct, mod_c[:, 1, 0], mod_c[:, 1, 1], norm_g[l, 1], w_in_b[:, :D_S5])

    y_s5 = _s5_bidirectional(u_s5.reshape(bsz, seq, D_S5), u_ctx.reshape(bsz, ctx_len, D_S5),
                             s5_lam_re[l], s5_lam_im[l], s5_log_dt[l],
                             s5_b_re[l], s5_b_im[l], s5_c_re[l], s5_c_im[l], s5_d[l])
    y_hy = _hyena(us_hy.reshape(bsz, seq, I_GA - I_HY),
                  hy_w1[l], hy_b1[l], hy_w2[l], hy_b2[l], hy_w3[l], hy_b3[l], hy_freq[l],
                  hy_w_out[l], hy_bias[l])

    xt = _merge(xt, mod[:, 1, 2], y_s5.reshape(bsz * seq, D_S5), y_hy.reshape(bsz * seq, D_HY),
                sig_gates, w_pa[l].astype(BF16), w_pb[l].astype(BF16), w_out[l].astype(BF16))

    xt = _ffn_sublayer(xt, mods(mod, 2), norm_g[l, 2], wg, wu, wd, 1, final_gain=final_g)
    return xt.reshape(bsz, seq, d)
```

```python
import functools
import math

import jax
import jax.numpy as jnp
import numpy as np
from jax import lax
from jax.experimental import pallas as pl
from jax.experimental.pallas import tpu as pltpu

F32 = jnp.float32
BF16 = jnp.bfloat16

D_MODEL = 2048
GRID_W = 64
D_S5 = 1024
S5_GROUP = 16
S5_GROUPS = D_S5 // S5_GROUP
S5_STATE = 64
S5_DIRS = 2
LAMBDA_RE_MAX = -1e-4
S5_CHUNK = 16
LANE = 128
S5_GROUPS_PER_COL = LANE // S5_GROUP
D_HY = 1024
HY_ORDER = 2
HY_DIRS = 2
HY_SHORT = 3
HY_EMB = 33
HY_BANDS = (HY_EMB - 1) // 2
HY_TARGET = 1e-2
HY_FAST_PCT = 0.3
HY_SLOW_PCT = 1.5
FFT_N1 = 64
FFT_N2 = 128
FFT_N = FFT_N1 * FFT_N2
FILTER_TILE_ROWS = FFT_N1 + 8
HY_TILE_PITCH = FILTER_TILE_ROWS
HY_PAIRS_PER_COL = LANE // 2
HY_PAIRS_PER_STEP = 16
I_HY = D_S5
I_GA = D_S5 + (HY_ORDER + 1) * D_HY
I_GB = I_GA + D_MODEL
D_IN = I_GB + D_MODEL
D_FF = 5632
N_SUB = 3
N_MOD = 3
HALF_STEP = 0.5
FFN_ROW_PARTS = 4
RMS_EPS = 1e-6

VMEM_LIMIT_BYTES = 58 * 1024 * 1024


def _rms_mod(x, gain, shift, scale):
    ms = jnp.mean(x * x, axis=-1, keepdims=True)
    y = x * lax.rsqrt(ms + RMS_EPS) * gain
    return y * (1.0 + scale) + shift


def _split_bf16(v):
    hi = v.astype(BF16)
    return hi, (v - hi.astype(F32)).astype(BF16)


def _ada_kernel(c_ref, w_ref, b_ref, o_ref):
    c = c_ref[...]
    a_hi, a_lo = _split_bf16(c * jax.nn.sigmoid(c))
    w_hi, w_lo = _split_bf16(w_ref[...])
    o_ref[...] = (jnp.dot(a_hi, w_hi, preferred_element_type=F32) + jnp.dot(a_lo, w_hi, preferred_element_type=F32)
                  + jnp.dot(a_hi, w_lo, preferred_element_type=F32) + b_ref[...])


def _ada_mod(c_rows, w, b, tn=1024):
    rows, d = c_rows.shape
    n = w.shape[1]
    return pl.pallas_call(
        _ada_kernel,
        grid=(n // tn,),
        in_specs=[pl.BlockSpec((rows, d), lambda j: (0, 0)),
                  pl.BlockSpec((d, tn), lambda j: (0, j)),
                  pl.BlockSpec((1, tn), lambda j: (0, j))],
        out_specs=pl.BlockSpec((rows, tn), lambda j: (0, j)),
        out_shape=jax.ShapeDtypeStruct((rows, n), F32),
        compiler_params=pltpu.CompilerParams(
            dimension_semantics=("arbitrary",), vmem_limit_bytes=VMEM_LIMIT_BYTES),
        name="ada_mod",
    )(c_rows, w, b.reshape(1, n))


def _ffn_kernel(x_ref, shift_ref, scale_ref, gate_ref, gain_ref, wg_ref, wu_ref, wd_ref,
                fg_ref, o_ref, h_ref, *, final_norm):
    j = pl.program_id(1)
    tm = x_ref.shape[0]

    def swiglu_down(h):
        g = jnp.dot(h, wg_ref[...].astype(BF16), preferred_element_type=F32)
        u = jnp.dot(h, wu_ref[...].astype(BF16), preferred_element_type=F32)
        a = (g * jax.nn.sigmoid(g) * u).astype(BF16)
        return jnp.dot(a, wd_ref[...].astype(BF16), preferred_element_type=F32)

    @pl.when(j == 0)
    def _():
        for r in range(0, tm, tm // FFN_ROW_PARTS):
            rows = slice(r, r + tm // FFN_ROW_PARTS)
            h = _rms_mod(x_ref[rows, :], gain_ref[...], shift_ref[...], scale_ref[...]).astype(BF16)
            h_ref[rows, :] = h
            o_ref[rows, :] = swiglu_down(h)

    @pl.when(j > 0)
    def _():
        o_ref[...] += swiglu_down(h_ref[...])

    @pl.when(j == pl.num_programs(1) - 1)
    def _():
        y = x_ref[...] + (HALF_STEP * gate_ref[...]) * o_ref[...]
        if final_norm:
            ms = jnp.mean(y * y, axis=-1, keepdims=True)
            y = y * lax.rsqrt(ms + RMS_EPS) * fg_ref[...]
        o_ref[...] = y


def _ffn_sublayer(x, mods, gain, wg, wu, wd, which, final_gain=None, tm=1024, tf=256):
    t, d = x.shape
    bm = mods[0].shape[0]
    tm = min(tm, t // bm)
    blocks_per_batch = (t // bm) // tm
    dff = wg.shape[2]
    final_norm = final_gain is not None
    fg = final_gain if final_norm else gain
    mod_spec = pl.BlockSpec((None, 1, d), lambda i, j: (i // blocks_per_batch, 0, 0))
    vec_spec = pl.BlockSpec((1, d), lambda i, j: (0, 0))
    return pl.pallas_call(
        functools.partial(_ffn_kernel, final_norm=final_norm),
        grid=(t // tm, dff // tf),
        in_specs=[pl.BlockSpec((tm, d), lambda i, j: (i, 0)),
                  mod_spec, mod_spec, mod_spec, vec_spec,
                  pl.BlockSpec((None, d, tf), lambda i, j: (which, 0, j)),
                  pl.BlockSpec((None, d, tf), lambda i, j: (which, 0, j)),
                  pl.BlockSpec((None, tf, d), lambda i, j: (which, j, 0)),
                  vec_spec],
        out_specs=pl.BlockSpec((tm, d), lambda i, j: (i, 0)),
        out_shape=jax.ShapeDtypeStruct((t, d), F32),
        scratch_shapes=[pltpu.VMEM((tm, d), BF16)],
        compiler_params=pltpu.CompilerParams(
            dimension_semantics=("parallel", "arbitrary"), vmem_limit_bytes=VMEM_LIMIT_BYTES),
        name="ffn_final" if final_norm else "ffn",
    )(x, *mods, gain.reshape(1, d), wg, wu, wd, fg.reshape(1, d))


def _proj_kernel(x_ref, shift_ref, scale_ref, gain_ref, w_ref, sw_ref, sb_ref, *rest, n_u, n_hy, row_len, part):
    o_refs, h_ref = rest[:-1], rest[-1]
    j = pl.program_id(1)

    tm, tn = h_ref.shape[0], w_ref.shape[1]

    def first_block(o_ref):
        for r in range(0, tm, tm // 2):
            rows = slice(r, r + tm // 2)
            h = _rms_mod(x_ref[rows, :], gain_ref[...], shift_ref[...], scale_ref[...]).astype(BF16)
            h_ref[rows, :] = h
            for c in range(0, tn, part):
                p = jnp.dot(h, w_ref[:, c:c + part].astype(BF16), preferred_element_type=F32)
                o_ref[rows, c:c + part] = p.astype(o_ref.dtype)

    def in_parts(o_ref, epilogue):
        for r in range(0, tm, tm // 2):
            rows = slice(r, r + tm // 2)
            h = h_ref[rows, :]
            for c in range(0, tn, part):
                p = jnp.dot(h, w_ref[:, c:c + part].astype(BF16), preferred_element_type=F32)
                o_ref[rows, c:c + part] = epilogue(p, c).astype(o_ref.dtype)

    def short_conv(p, c):
        col = lax.broadcasted_iota(jnp.int32, p.shape, 0) % row_len
        prev = jnp.where(col == 0, 0.0, pltpu.roll(p, 1, 0))
        nxt = jnp.where(col == row_len - 1, 0.0, pltpu.roll(p, p.shape[0] - 1, 0))
        sw = sw_ref[:, c:c + part]
        return sb_ref[:, c:c + part] + prev * sw[0:1] + p * sw[1:2] + nxt * sw[2:3]

    @pl.when(j == 0)
    def _():
        first_block(o_refs[0])

    if n_hy == 0:
        @pl.when(j > 0)
        def _():
            in_parts(o_refs[0], lambda p, c: p)
        return

    @pl.when((j > 0) & (j < n_u))
    def _():
        in_parts(o_refs[0], lambda p, c: p)

    @pl.when((j >= n_u) & (j < n_u + n_hy))
    def _():
        in_parts(o_refs[1], short_conv)

    @pl.when(j >= n_u + n_hy)
    def _():
        in_parts(o_refs[2], lambda p, c: jax.nn.sigmoid(p))


def _in_proj(x, shift, scale, gain, w, short_w=None, short_b=None, n_u=D_S5, n_hy=0, row_len=GRID_W,
             tm=1024, tn=512, part=256):
    t, d = x.shape
    bm = shift.shape[0]
    tm = min(tm, t // bm)
    blocks_per_batch = (t // bm) // tm
    n = w.shape[1]
    assert (tm // 2) % row_len == 0 and (t // bm) % tm == 0
    bu, bh = n_u // tn, n_hy // tn
    bg = n // tn - bu - bh
    mod_spec = pl.BlockSpec((None, 1, d), lambda i, j: (i // blocks_per_batch, 0, 0))
    out_shape = [jax.ShapeDtypeStruct((t, n_u), F32)]
    out_specs = [pl.BlockSpec((tm, tn), lambda i, j: (i, jnp.minimum(j, bu - 1)))]
    if bh:
        out_shape += [jax.ShapeDtypeStruct((t, n_hy), BF16), jax.ShapeDtypeStruct((t, bg * tn), BF16)]
        out_specs += [pl.BlockSpec((tm, tn), lambda i, j: (i, jnp.clip(j - bu, 0, bh - 1))),
                      pl.BlockSpec((tm, tn), lambda i, j: (i, jnp.maximum(j - bu - bh, 0)))]
        sw, sb = short_w, short_b.reshape(1, n_hy)
        hy_blk = lambda i, j: (0, jnp.clip(j - bu, 0, bh - 1))
    else:
        sw, sb = jnp.zeros((HY_SHORT, tn), F32), jnp.zeros((1, tn), F32)
        hy_blk = lambda i, j: (0, 0)
    return pl.pallas_call(
        functools.partial(_proj_kernel, n_u=bu, n_hy=bh, row_len=row_len, part=part),
        grid=(t // tm, n // tn),
        in_specs=[pl.BlockSpec((tm, d), lambda i, j: (i, 0)),
                  mod_spec, mod_spec,
                  pl.BlockSpec((1, d), lambda i, j: (0, 0)),
                  pl.BlockSpec((d, tn), lambda i, j: (0, j)),
                  pl.BlockSpec((HY_SHORT, tn), hy_blk),
                  pl.BlockSpec((1, tn), hy_blk)],
        out_specs=out_specs,
        out_shape=out_shape,
        scratch_shapes=[pltpu.VMEM((tm, d), BF16)],
        compiler_params=pltpu.CompilerParams(
            dimension_semantics=("parallel", "arbitrary"), vmem_limit_bytes=VMEM_LIMIT_BYTES),
        name="in_proj",
    )(x, shift, scale, gain.reshape(1, d), w, sw, sb)


def _gelu_tanh(x):
    return 0.5 * x * (1.0 + jnp.tanh(math.sqrt(2.0 / math.pi) * (x + 0.044715 * (x * x * x))))


def _merge_kernel(x_ref, gate_ref, ys_ref, yh_ref, ga_ref, gb_ref, wpa_lo_ref, wpa_hi_ref, wpb_ref,
                  wout_ref, o_ref, s_ref, acc_ref):
    j = pl.program_id(1)

    @pl.when(j == 0)
    def _():
        s_ref[...] = _gelu_tanh(ys_ref[...].astype(F32)).astype(BF16)
        acc_ref[...] = jnp.zeros_like(acc_ref)

    s = s_ref[...]
    pa_lo = jnp.dot(s, wpa_lo_ref[...], preferred_element_type=F32)
    pa_hi = jnp.dot(s, wpa_hi_ref[...], preferred_element_type=F32)
    y_a = pa_lo * jax.nn.sigmoid(pa_hi)
    y_b = jnp.dot(yh_ref[...], wpb_ref[...], preferred_element_type=F32)
    m = ga_ref[...].astype(F32) * y_a + gb_ref[...].astype(F32) * y_b
    acc_ref[...] += jnp.dot(m.astype(BF16), wout_ref[...], preferred_element_type=F32)

    @pl.when(j == pl.num_programs(1) - 1)
    def _():
        o_ref[...] = x_ref[...] + gate_ref[...] * acc_ref[...]


def _merge(x, gate, y_s5, y_hy, sig_gates, w_pa, w_pb, w_out, tm=512, tn=1024):
    t, d = x.shape
    bm = gate.shape[0]
    blocks_per_batch = (t // bm) // tm
    nj = d // tn
    ds5 = y_s5.shape[1]
    dhy = y_hy.shape[1]
    return pl.pallas_call(
        _merge_kernel,
        grid=(t // tm, nj),
        in_specs=[pl.BlockSpec((tm, d), lambda i, j: (i, 0)),
                  pl.BlockSpec((None, 1, d), lambda i, j: (i // blocks_per_batch, 0, 0)),
                  pl.BlockSpec((tm, ds5), lambda i, j: (i, 0)),
                  pl.BlockSpec((tm, dhy), lambda i, j: (i, 0)),
                  pl.BlockSpec((tm, tn), lambda i, j: (i, j)),
                  pl.BlockSpec((tm, tn), lambda i, j: (i, nj + j)),
                  pl.BlockSpec((ds5, tn), lambda i, j: (0, j)),
                  pl.BlockSpec((ds5, tn), lambda i, j: (0, nj + j)),
                  pl.BlockSpec((dhy, tn), lambda i, j: (0, j)),
                  pl.BlockSpec((tn, d), lambda i, j: (j, 0))],
        out_specs=pl.BlockSpec((tm, d), lambda i, j: (i, 0)),
        out_shape=jax.ShapeDtypeStruct((t, d), F32),
        scratch_shapes=[pltpu.VMEM((tm, ds5), BF16), pltpu.VMEM((tm, d), F32)],
        compiler_params=pltpu.CompilerParams(
            dimension_semantics=("parallel", "arbitrary"), vmem_limit_bytes=VMEM_LIMIT_BYTES),
        name="merge",
    )(x, gate, y_s5, y_hy, sig_gates, sig_gates, w_pa, w_pa, w_pb, w_out)


def _s5_weights_kernel(*refs, n_steps):
    blocks, ca_ref = refs[:-1], refs[-1]

    def one(gi, carry):
        _s5_weights_group(*[r.at[gi] for r in blocks], ca_ref, n_steps=n_steps)
        return carry

    lax.fori_loop(0, blocks[0].shape[0], one, 0)


def _s5_weights_group(par_ref, bt_ref, c_ref, d_ref, bpow_ref, toep_ref, cpow_ref, tab_ref, ca_ref, *, n_steps):
    t, h, p = S5_CHUNK, S5_GROUP, S5_STATE
    lanes = 2 * p
    hp = lax.Precision.HIGHEST
    sgn = jnp.where(lax.broadcasted_iota(jnp.int32, (1, lanes), 1) < p, -1.0, 1.0)
    par = par_ref[...]
    gsum = None
    for d in range(S5_DIRS):
        lr = jnp.minimum(par[3 * d:3 * d + 1], LAMBDA_RE_MAX)
        li = par[3 * d + 1:3 * d + 2]
        dt = jnp.exp(par[3 * d + 2:3 * d + 3])
        zr, zi = lr * dt, li * dt

        def apow(j):
            mag = jnp.exp(j * zr)
            return mag * jnp.cos(j * zi), sgn * (mag * jnp.sin(j * zi))

        def cmul(x, a1, a2):
            return x * a1 + pltpu.roll(x, p, 1) * a2

        a1, a2 = apow(lax.broadcasted_iota(jnp.int32, (t + 1, 1), 0).astype(F32))
        nr, ni = a1[1:2] - 1.0, sgn * a2[1:2]
        den = lr * lr + li * li
        f_re = (nr * lr + ni * li) / den
        f_im = (ni * lr - nr * li) / den
        bbar = cmul(bt_ref[d], f_re, sgn * f_im)
        cc = c_ref[d]
        ca = [cmul(cc, a1[j:j + 1], a2[j:j + 1]) * (-sgn) for j in range(t + 1)]
        ca_ref[...] = jnp.zeros_like(ca_ref)
        for k in range(t):
            e_b, e_c = (t - 1 - k, k + 1) if d == 0 else (k, t - k)
            bpow_ref[k * h:(k + 1) * h, d * lanes:(d + 1) * lanes] = (
                cmul(bbar, a1[e_b:e_b + 1], a2[e_b:e_b + 1]).astype(BF16))
            cpow_ref[k * h:(k + 1) * h, d * lanes:(d + 1) * lanes] = ca[e_c].astype(BF16)
            l = t - 1 + k if d == 0 else t - 1 - k
            ca_ref[l * h:(l + 1) * h, :] = ca[k]
        g = lax.dot_general(bbar, ca_ref[...], (((1,), (1,)), ((), ())), preferred_element_type=F32, precision=hp)
        gsum = g if gsum is None else gsum + g
        for s in range(n_steps):
            s1, s2 = apow(float(t * 2 ** s))
            r = d * 2 * n_steps + 2 * s
            tab_ref[r:r + 1, :] = s1
            tab_ref[r + 1:r + 2, :] = s2
    wide = gsum.shape[1]
    col = lax.broadcasted_iota(jnp.int32, (h, wide), 1)
    row = lax.broadcasted_iota(jnp.int32, (h, wide), 0)
    gsum = gsum + jnp.where(col - (t - 1) * h == row, d_ref[...], 0.0)
    for k in range(t):
        off = (t - 1 - k) * h
        shifted = gsum if off == 0 else pltpu.roll(gsum, wide - off, 1)
        toep_ref[k * h:(k + 1) * h, :] = shifted[:, :t * h].astype(BF16)


def _s5_weights(lam_re, lam_im, log_dt, b_re, b_im, c_re, c_im, d_skip, n_steps):
    g, p, h, t = S5_GROUPS, S5_STATE, S5_GROUP, S5_CHUNK
    cat2 = lambda a: jnp.concatenate([a, a], axis=-1)
    par = jnp.stack([cat2(lam_re), cat2(lam_im), jnp.broadcast_to(log_dt[..., None], (S5_DIRS, g, 2 * p))], axis=1)
    par = par.transpose(2, 0, 1, 3).reshape(g, 3 * S5_DIRS, 2 * p)
    btc = jnp.concatenate([b_re, b_im], axis=2).transpose(1, 0, 3, 2)
    ccat = jnp.concatenate([c_re, c_im], axis=3).transpose(1, 0, 2, 3)
    wide = 2 * t * h
    drow = jnp.zeros((g, 1, wide), F32).at[:, 0, (t - 1) * h:t * h].set(d_skip.reshape(g, h))
    w = t * h
    gb = S5_GROUPS_PER_COL
    sq = pl.BlockSpec((gb, w, w), lambda i: (i, 0, 0))
    return pl.pallas_call(
        functools.partial(_s5_weights_kernel, n_steps=n_steps),
        grid=(g // gb,),
        in_specs=[pl.BlockSpec((gb, 3 * S5_DIRS, 2 * p), lambda i: (i, 0, 0)),
                  pl.BlockSpec((gb, S5_DIRS, h, 2 * p), lambda i: (i, 0, 0, 0)),
                  pl.BlockSpec((gb, S5_DIRS, h, 2 * p), lambda i: (i, 0, 0, 0)),
                  pl.BlockSpec((gb, 1, wide), lambda i: (i, 0, 0))],
        out_specs=[sq, sq, sq, pl.BlockSpec((gb, 4 * n_steps, 2 * p), lambda i: (i, 0, 0))],
        out_shape=[jax.ShapeDtypeStruct((g, w, w), BF16)] * 3 + [jax.ShapeDtypeStruct((g, 4 * n_steps, 2 * p), F32)],
        scratch_shapes=[pltpu.VMEM((wide, 2 * p), F32)],
        compiler_params=pltpu.CompilerParams(
            dimension_semantics=("parallel",), vmem_limit_bytes=VMEM_LIMIT_BYTES),
        name="s5_weights",
    )(par, btc, ccat, drow)


def _s5_kernel(u_ref, uc_ref, bpow_ref, toep_ref, cpow_ref, tab_ref, y_ref, ucat_ref, ucc_ref, yacc_ref,
               *, bsz, n_steps):
    t, gw = S5_CHUNK, S5_GROUP
    per = LANE // gw
    n_lat = u_ref.shape[0] // (bsz * t)
    n_ctx = uc_ref.shape[0] // (bsz * t)
    n_ch = n_lat + n_ctx
    rows = bsz * n_ch
    half = 2 * S5_STATE
    for k in range(t):
        ucat_ref[:, k * LANE:(k + 1) * LANE] = u_ref[pl.ds(k, bsz * n_lat, stride=t), :]
        ucc_ref[:, k * LANE:(k + 1) * LANE] = uc_ref[pl.ds(k, bsz * n_ctx, stride=t), :]
    yacc_ref[...] = jnp.zeros_like(yacc_ref)
    rib = lax.broadcasted_iota(jnp.int32, (rows, half), 0) % n_ch
    lane = lax.broadcasted_iota(jnp.int32, (1, LANE), 1)

    def cmul_add(acc, sh, a1, a2):
        return acc + a1 * sh + a2 * pltpu.roll(sh, S5_STATE, 1)

    def gather(src_ref, gl):
        cols = []
        for j in range(t // per):
            acc = None
            for tt in range(per):
                k = j * per + tt
                r = pltpu.roll(src_ref[:, k * LANE:(k + 1) * LANE], (gw * tt - gw * gl) % LANE, 1)
                acc = r if acc is None else jnp.where((lane >= gw * tt) & (lane < gw * (tt + 1)), r, acc)
            cols.append(acc)
        return jnp.concatenate(cols, axis=1).astype(BF16)

    def group(gl, carry):
        ul = gather(ucat_ref, gl)
        uc = gather(ucc_ref, gl)
        bpow = bpow_ref[gl]
        zl = jnp.dot(ul, bpow, preferred_element_type=F32)
        zc = jnp.dot(uc, bpow, preferred_element_type=F32)
        fparts, bparts = [], []
        for b in range(bsz):
            lat = slice(b * n_lat, (b + 1) * n_lat)
            ctx = slice(b * n_ctx, (b + 1) * n_ctx)
            fparts += [zc[ctx, :half], zl[lat, :half]]
            bparts += [zl[lat, half:], zc[ctx, half:]]
        fw = jnp.concatenate(fparts, axis=0)
        bw = jnp.concatenate(bparts, axis=0)
        tab = tab_ref[gl]
        for s in range(n_steps):
            d = 1 << s
            sh = jnp.where(rib >= d, pltpu.roll(fw, d, 0), 0.0)
            fw = cmul_add(fw, sh, tab[2 * s:2 * s + 1], tab[2 * s + 1:2 * s + 2])
            o = 2 * n_steps
            sh = jnp.where(rib < n_ch - d, pltpu.roll(bw, rows - d, 0), 0.0)
            bw = cmul_add(bw, sh, tab[o + 2 * s:o + 2 * s + 1], tab[o + 2 * s + 1:o + 2 * s + 2])
        fe = jnp.where(rib >= 1, pltpu.roll(fw, 1, 0), 0.0)
        be = jnp.where(rib < n_ch - 1, pltpu.roll(bw, rows - 1, 0), 0.0)
        fl = jnp.concatenate([fe[b * n_ch + n_ctx:(b + 1) * n_ch] for b in range(bsz)], axis=0)
        bl = jnp.concatenate([be[b * n_ch:b * n_ch + n_lat] for b in range(bsz)], axis=0)
        st = jnp.concatenate([fl, bl], axis=1).astype(BF16)
        y = (jnp.dot(ul, toep_ref[gl], preferred_element_type=F32)
             + lax.dot_general(st, cpow_ref[gl], (((1,), (1,)), ((), ())), preferred_element_type=F32))
        mine = (lane >= gw * gl) & (lane < gw * (gl + 1))
        for i in range(t):
            src = y[:, (i // per) * LANE:(i // per + 1) * LANE]
            r = pltpu.roll(src, (gw * gl - gw * (i % per)) % LANE, 1)
            blk = slice(i * LANE, (i + 1) * LANE)
            yacc_ref[:, blk] = jnp.where(mine, r, yacc_ref[:, blk])
        return carry

    lax.fori_loop(0, S5_GROUPS_PER_COL, group, 0)
    for i in range(t):
        y_ref[pl.ds(i, bsz * n_lat, stride=t), :] = yacc_ref[:, i * LANE:(i + 1) * LANE]


def _s5_mix(u, uc, bpw, toep, cpw, tab, bsz):
    rl, dm = u.shape
    rc = uc.shape[0]
    t = S5_CHUNK
    w = t * S5_GROUP
    n_steps = tab.shape[1] // 4
    gpc = S5_GROUPS_PER_COL
    wspec = pl.BlockSpec((gpc, w, w), lambda i: (i, 0, 0))
    return pl.pallas_call(
        functools.partial(_s5_kernel, bsz=bsz, n_steps=n_steps),
        grid=(dm // LANE,),
        in_specs=[pl.BlockSpec((rl, LANE), lambda i: (0, i)),
                  pl.BlockSpec((rc, LANE), lambda i: (0, i)),
                  wspec, wspec, wspec,
                  pl.BlockSpec((gpc, 4 * n_steps, tab.shape[2]), lambda i: (i, 0, 0))],
        out_specs=pl.BlockSpec((rl, LANE), lambda i: (0, i)),
        out_shape=jax.ShapeDtypeStruct(u.shape, F32),
        scratch_shapes=[pltpu.VMEM((rl // t, t * LANE), F32), pltpu.VMEM((rc // t, t * LANE), F32),
                        pltpu.VMEM((rl // t, t * LANE), F32)],
        compiler_params=pltpu.CompilerParams(
            dimension_semantics=("parallel",), vmem_limit_bytes=VMEM_LIMIT_BYTES),
        name="s5_mix",
    )(u, uc, bpw, toep, cpw, tab)


def _s5_bidirectional(u, u_ctx, lam_re, lam_im, log_dt, b_re, b_im, c_re, c_im, d_skip):
    bsz, length, dm = u.shape
    ctx_len = u_ctx.shape[1]
    t = S5_CHUNK
    n_lat, n_ctx = length // t, ctx_len // t
    n_steps = max(1, math.ceil(math.log2(n_lat + n_ctx)))
    bpw, toep, cpw, tab = _s5_weights(lam_re, lam_im, log_dt, b_re, b_im, c_re, c_im, d_skip, n_steps)
    y = _s5_mix(u.reshape(bsz * length, dm), u_ctx.reshape(bsz * ctx_len, dm), bpw, toep, cpw, tab, bsz)
    return y.reshape(bsz, length, dm)


def _dft_constants(real_input=False):
    n1 = np.arange(FFT_N1)
    n2 = np.arange(FFT_N2)
    half = FFT_N1 // 2
    th = 2 * np.pi * np.outer(n1, n1) / FFT_N1
    c1, s1 = np.cos(th), np.sin(th)
    if real_input:
        w1 = np.concatenate([c1, -s1], axis=1)
    else:
        w1 = np.concatenate([np.concatenate([c1[:half], -s1[:half]], axis=1),
                             np.concatenate([s1[:half], c1[:half]], axis=1)], axis=0)
    z = np.zeros_like(w1)
    w1p = np.block([[w1, z], [z, w1]])
    ph = 2 * np.pi * np.outer(n2, n1) / FFT_N
    ps = 2 * np.pi * np.outer(n2, n2) / FFT_N2
    f2 = np.concatenate([np.cos(ps), -np.sin(ps)], axis=1)
    g2 = np.concatenate([np.cos(ps), np.sin(ps)], axis=1)
    c2, s2 = np.cos(ph).T, np.sin(ph).T
    wi = np.concatenate([np.concatenate([c1[:, :half], s1[:, :half]], axis=1),
                         np.concatenate([-s1[:, :half], c1[:, :half]], axis=1)], axis=0) / FFT_N
    zi = np.zeros_like(wi)
    wi2 = np.stack([np.concatenate([wi, zi], axis=1), np.concatenate([zi, wi], axis=1)])
    as_b = lambda a: jnp.asarray(a, F32).astype(BF16)
    as_f = lambda a: jnp.asarray(a, F32)
    return [as_b(w1p), as_f(c2), as_f(s2), as_b(f2), as_b(g2), as_b(wi2)]


def _fwd_spectrum(xp, w1p, c2, s2, f2):
    cp = xp.shape[0]
    hn = FFT_N1
    a = jnp.dot(xp.reshape(cp * FFT_N2, LANE).astype(BF16), w1p, preferred_element_type=F32)
    out = []
    for par in range(2):
        ap = a[:, par * LANE:(par + 1) * LANE].reshape(cp, FFT_N2, LANE)
        at = jnp.swapaxes(ap, 1, 2)
        are, aim = at[:, :hn], at[:, hn:]
        at = jnp.concatenate([are * c2 + aim * s2, aim * c2 - are * s2], axis=1)
        p = jnp.dot(at.reshape(cp * 2 * hn, FFT_N2).astype(BF16), f2, preferred_element_type=F32)
        p = p.reshape(cp, 2 * hn, 2 * FFT_N2)
        out.append((p[:, :hn, :FFT_N2] - p[:, hn:, FFT_N2:], p[:, :hn, FFT_N2:] + p[:, hn:, :FFT_N2]))
    return out


def _inv_time(yre, yim, g2, c2, s2, wi_par):
    cp = yre.shape[0]
    hn = FFT_N1
    y = jnp.concatenate([yre, yim], axis=1).reshape(cp * 2 * hn, FFT_N2).astype(BF16)
    q = jnp.dot(y, g2, preferred_element_type=F32).reshape(cp, 2 * hn, 2 * FFT_N2)
    bre = q[:, :hn, :FFT_N2] - q[:, hn:, FFT_N2:]
    bim = q[:, :hn, FFT_N2:] + q[:, hn:, :FFT_N2]
    b2 = jnp.concatenate([bre * c2 - bim * s2, bre * s2 + bim * c2], axis=1)
    bt = jnp.swapaxes(b2, 1, 2)
    return jnp.dot(bt.reshape(cp * FFT_N2, LANE).astype(BF16), wi_par, preferred_element_type=F32)


def _hyena_kernel(z_ref, g1_ref, g2_ref, kf_ref, bias_ref, w1p_ref, c2_ref, s2_ref, f2_ref, gi_ref, wi_ref,
                  o_ref, zt_ref, g1t_ref, g2t_ref, ot_ref, stage_ref):
    k = pl.program_id(1)
    n_s = z_ref.shape[0] // FFT_N2
    cp = HY_PAIRS_PER_STEP

    def to_tiles(x_ref, t_ref):
        for s in range(n_s):
            xs = x_ref[s * FFT_N2:(s + 1) * FFT_N2, :].astype(F32)
            stage_ref[pl.ds(s, LANE, stride=HY_TILE_PITCH), :] = xs.T
        st = stage_ref[...].reshape(HY_PAIRS_PER_COL, 2 * HY_TILE_PITCH, FFT_N2)
        both = jnp.concatenate([st[:, :n_s], st[:, HY_TILE_PITCH:HY_TILE_PITCH + n_s]], axis=1)
        t_ref[...] = jnp.swapaxes(both, 1, 2)

    @pl.when(k == 0)
    def _():
        to_tiles(z_ref, zt_ref)
        to_tiles(g1_ref, g1t_ref)
        to_tiles(g2_ref, g2t_ref)

    sl = pl.ds(pl.multiple_of(k * cp, cp), cp)
    z = zt_ref[sl]
    gates = (g1t_ref, g2t_ref)
    for o in range(HY_ORDER):
        spec = _fwd_spectrum(z, w1p_ref[...], c2_ref[...], s2_ref[...], f2_ref[...])
        conv = None
        for par in range(2):
            xre, xim = spec[par]
            kre = kf_ref[o, par, :, :FFT_N1, :]
            kim = kf_ref[o, par, :, FFT_N1:, :]
            part = _inv_time(xre * kre - xim * kim, xre * kim + xim * kre,
                             gi_ref[...], c2_ref[...], s2_ref[...], wi_ref[par])
            conv = part if conv is None else conv + part
        z = gates[o][sl] * (conv.reshape(cp, FFT_N2, LANE) + bias_ref[o] * z)
    ot_ref[sl] = z

    @pl.when(k == pl.num_programs(1) - 1)
    def _():
        back = jnp.swapaxes(ot_ref[...], 1, 2)
        for c in range(LANE):
            stage_ref[c * HY_TILE_PITCH:c * HY_TILE_PITCH + n_s, :] = back[c // 2, (c % 2) * n_s:(c % 2 + 1) * n_s, :]
        for s in range(n_s):
            rows = stage_ref[pl.ds(s, LANE, stride=HY_TILE_PITCH), :]
            o_ref[s * FFT_N2:(s + 1) * FFT_N2, :] = rows.T.astype(o_ref.dtype)


def _hyena_conv(us, kf, biasp):
    t, _ = us.shape
    consts = _dft_constants()
    ncol = D_HY // LANE
    nsub = HY_PAIRS_PER_COL // HY_PAIRS_PER_STEP
    full = lambda a: pl.BlockSpec(a.shape, lambda j, k: (0,) * a.ndim)
    nat = lambda off: pl.BlockSpec((t, LANE), lambda j, k: (0, off + j))
    tiles = pltpu.VMEM((HY_PAIRS_PER_COL, FFT_N2, LANE), F32)
    return pl.pallas_call(
        _hyena_kernel,
        grid=(ncol, nsub),
        in_specs=[nat(0), nat(ncol), nat(2 * ncol),
                  pl.BlockSpec((HY_ORDER, 2, HY_PAIRS_PER_STEP, 2 * FFT_N1, FFT_N2),
                               lambda j, k: (0, 0, j * nsub + k, 0, 0)),
                  pl.BlockSpec((HY_ORDER, HY_PAIRS_PER_STEP, 1, LANE), lambda j, k: (0, j * nsub + k, 0, 0))]
                 + [full(a) for a in consts],
        out_specs=pl.BlockSpec((t, LANE), lambda j, k: (0, j)),
        out_shape=jax.ShapeDtypeStruct((t, D_HY), us.dtype),
        scratch_shapes=[tiles, tiles, tiles, tiles, pltpu.VMEM((LANE * HY_TILE_PITCH, FFT_N2), F32)],
        compiler_params=pltpu.CompilerParams(
            dimension_semantics=("parallel", "arbitrary"), vmem_limit_bytes=VMEM_LIMIT_BYTES),
        name="hyena_conv",
    )(us, us, us, kf, biasp, *consts)


def _filter_time_kernel(w1t_ref, w1c_ref, w1s_ref, b1_ref, w2_ref, b2_ref, w3_ref, b3_ref, fr_ref,
                        wf_ref, wb_ref, df_ref, db_ref, o_ref, h_ref, k_ref, *, length):
    n_fft = 2 * length
    hp = lax.Precision.HIGHEST

    @pl.when(pl.program_id(0) == 0)
    def _():
        pos = lax.broadcasted_iota(jnp.int32, (1, n_fft), 1)
        lag = jnp.where(pos < length, pos, n_fft - pos).astype(F32)
        t = lag / float(length - 1)
        w = (2.0 * math.pi / length) * lag
        band_step = (HY_BANDS - 1 - 1e-4) / (HY_BANDS - 1)
        bands = 1e-4 + band_step * lax.broadcasted_iota(jnp.int32, (HY_BANDS, 1), 0).astype(F32)
        ang = bands * w
        fr = fr_ref[...]
        h = (w1t_ref[...] * t + jnp.dot(w1c_ref[...], jnp.cos(ang), preferred_element_type=F32, precision=hp)
             - jnp.dot(w1s_ref[...], jnp.sin(ang), preferred_element_type=F32, precision=hp))
        h = jnp.sin(fr * (h + b1_ref[...]))
        h = jnp.sin(fr * (jnp.dot(w2_ref[...], h, preferred_element_type=F32, precision=hp) + b2_ref[...]))
        h = jnp.sin(fr * (jnp.dot(w3_ref[...], h, preferred_element_type=F32, precision=hp) + b3_ref[...]))
        hi = h.astype(BF16)
        h_ref[0] = hi
        h_ref[1] = (h - hi.astype(F32)).astype(BF16)

    def dot3(w, lo, hi_):
        w_hi = w.astype(BF16)
        w_lo = (w - w_hi.astype(F32)).astype(BF16)
        h_hi, h_lo = h_ref[0, :, lo:hi_], h_ref[1, :, lo:hi_]
        return (jnp.dot(w_hi, h_hi, preferred_element_type=F32) + jnp.dot(w_hi, h_lo, preferred_element_type=F32)
                + jnp.dot(w_lo, h_hi, preferred_element_type=F32))

    pos = lax.broadcasted_iota(jnp.int32, (1, length), 1)
    tf = pos.astype(F32) / float(length - 1)
    tb = (length - pos).astype(F32) / float(length - 1)
    kf = dot3(wf_ref[...], 0, length) * jnp.exp(-tf * df_ref[...])
    kb = dot3(wb_ref[...], length, n_fft) * jnp.exp(-tb * db_ref[...])
    kb = jnp.where(pos == 0, 0.0, kb)
    inv = 1.0 / (jnp.sum(jnp.abs(kf), axis=1, keepdims=True) + jnp.sum(jnp.abs(kb), axis=1, keepdims=True))
    k_ref[:, :length] = kf * inv
    k_ref[:, length:] = kb * inv
    cb = wf_ref.shape[0]
    for n1 in range(FILTER_TILE_ROWS):
        row = k_ref[:, n1 * FFT_N2:(n1 + 1) * FFT_N2] if n1 < FFT_N1 else jnp.zeros((cb, FFT_N2), F32)
        o_ref[pl.ds(n1, cb, stride=FILTER_TILE_ROWS), :] = row


def _filter_time(length, w1, b1, w2, b2, w3, b3, freq, w_out, cb=128):
    col = lambda v: v.reshape(-1, 1)
    w1t = w1.T
    n_ch = w_out.shape[1]
    deltas = jnp.abs(jnp.linspace(math.log(HY_TARGET) / HY_SLOW_PCT, math.log(HY_TARGET) / HY_FAST_PCT,
                                  n_ch, dtype=F32)).reshape(n_ch, 1)
    wot = w_out.T
    nb = D_HY // cb
    small = lambda a: pl.BlockSpec(a.shape, lambda i: (0,) * a.ndim)
    fwd = lambda i: ((i // nb) * HY_DIRS * nb + i % nb, 0)
    bwd = lambda i: ((i // nb) * HY_DIRS * nb + nb + i % nb, 0)
    ins = [w1t[:, 0:1], w1t[:, 1:1 + HY_BANDS], w1t[:, 1 + HY_BANDS:], col(b1), w2.T, col(b2), w3.T, col(b3),
           col(freq)]
    hy_ff = w2.shape[0]
    return pl.pallas_call(
        functools.partial(_filter_time_kernel, length=length),
        grid=(HY_ORDER * nb,),
        in_specs=[small(a) for a in ins] + [pl.BlockSpec((cb, hy_ff), fwd), pl.BlockSpec((cb, hy_ff), bwd),
                                            pl.BlockSpec((cb, 1), fwd), pl.BlockSpec((cb, 1), bwd)],
        out_specs=pl.BlockSpec((cb * FILTER_TILE_ROWS, FFT_N2), lambda i: (i, 0)),
        out_shape=jax.ShapeDtypeStruct((HY_ORDER * D_HY * FILTER_TILE_ROWS, FFT_N2), F32),
        scratch_shapes=[pltpu.VMEM((2, hy_ff, 2 * length), BF16), pltpu.VMEM((cb, 2 * length), F32)],
        compiler_params=pltpu.CompilerParams(
            dimension_semantics=("arbitrary",), vmem_limit_bytes=VMEM_LIMIT_BYTES),
        name="hyena_filter_time",
    )(*ins, wot, wot, deltas, deltas)


def _filter_spec_kernel(k_ref, w1p_ref, c2_ref, s2_ref, f2_ref, o_ref):
    cb = k_ref.shape[0] // FILTER_TILE_ROWS
    kt = k_ref[...].reshape(cb, FILTER_TILE_ROWS, FFT_N2)[:, :FFT_N1, :]
    xp = jnp.swapaxes(kt.reshape(cb // 2, 2 * FFT_N1, FFT_N2), 1, 2)
    spec = _fwd_spectrum(xp, w1p_ref[...], c2_ref[...], s2_ref[...], f2_ref[...])
    for par in range(2):
        o_ref[par, :, :FFT_N1, :] = spec[par][0]
        o_ref[par, :, FFT_N1:, :] = spec[par][1]


def _filter_spectrum(kt, cb=128):
    consts = _dft_constants(real_input=True)[:4]
    nb = D_HY // cb
    full = lambda a: pl.BlockSpec(a.shape, lambda i: (0,) * a.ndim)
    return pl.pallas_call(
        _filter_spec_kernel,
        grid=(HY_ORDER * nb,),
        in_specs=[pl.BlockSpec((cb * FILTER_TILE_ROWS, FFT_N2), lambda i: (i, 0))] + [full(a) for a in consts],
        out_specs=pl.BlockSpec((None, 2, cb // 2, 2 * FFT_N1, FFT_N2), lambda i: (i // nb, 0, i % nb, 0, 0)),
        out_shape=jax.ShapeDtypeStruct((HY_ORDER, 2, D_HY // 2, 2 * FFT_N1, FFT_N2), F32),
        compiler_params=pltpu.CompilerParams(
            dimension_semantics=("parallel",), vmem_limit_bytes=VMEM_LIMIT_BYTES),
        name="hyena_filter_spectrum",
    )(kt, *consts)


def _hyena(us, w1, b1, w2, b2, w3, b3, freq, w_out, bias):
    bsz, length, _ = us.shape
    assert 2 * length == FFT_N and bsz == 2, "one complex transform carries exactly two batch rows"
    kf = _filter_spectrum(_filter_time(length, w1, b1, w2, b2, w3, b3, freq, w_out))
    biasp = jnp.repeat(bias.reshape(HY_ORDER, D_HY // 2, 1, 2), FFT_N1, axis=-1)
    return _hyena_conv(us.reshape(bsz * length, -1), kf, biasp).reshape(bsz, length, D_HY)


def kernel(x, c, ctx, c_ctx, w_ada, b_ada, norm_g, ffn_w_gate, ffn_w_up, ffn_w_down, w_in,
           s5_lam_re, s5_lam_im, s5_log_dt, s5_b_re, s5_b_im, s5_c_re, s5_c_im, s5_d,
           hy_short_w, hy_short_b, hy_w1, hy_b1, hy_w2, hy_b2, hy_w3, hy_b3, hy_freq, hy_w_out,
           hy_bias, w_pa, w_pb, w_out, final_g):
    bsz, seq, d = x.shape
    ctx_len = ctx.shape[1]
    n_rows = seq // GRID_W
    depth = w_ada.shape[0]
    assert depth == 1, "context-token outputs are only dropped by the last layer"
    l = 0

    c_rows = jnp.concatenate([c, c_ctx[None, :], jnp.zeros((8 - bsz - 1, d), F32)], axis=0)
    mod_all = _ada_mod(c_rows, w_ada[l], b_ada[l])
    mod = mod_all[:bsz].reshape(bsz, N_SUB, N_MOD, 1, d)
    mod_c = mod_all[bsz:bsz + 1].reshape(1, N_SUB, N_MOD, 1, d)

    def mods(m, sub):
        return tuple(m[:, sub, k] for k in range(N_MOD))

    wg, wu, wd = ffn_w_gate[l], ffn_w_up[l], ffn_w_down[l]
    w_in_b = w_in[l].astype(BF16)

    xt = x.reshape(bsz * seq, d)
    ct = ctx.reshape(bsz * ctx_len, d)

    xt = _ffn_sublayer(xt, mods(mod, 0), norm_g[l, 0], wg, wu, wd, 0)
    ct = _ffn_sublayer(ct, mods(mod_c, 0), norm_g[l, 0], wg, wu, wd, 0)

    assert GRID_W * n_rows == seq
    u_s5, us_hy, sig_gates = _in_proj(xt, mod[:, 1, 0], mod[:, 1, 1], norm_g[l, 1], w_in_b,
                                      hy_short_w[l], hy_short_b[l], n_u=I_HY, n_hy=I_GA - I_HY)
    (u_ctx,) = _in_proj(ct, mod_c[:, 1, 0], mod_c[:, 1, 1], norm_g[l, 1], w_in_b[:, :D_S5])

    y_s5 = _s5_bidirectional(u_s5.reshape(bsz, seq, D_S5), u_ctx.reshape(bsz, ctx_len, D_S5),
                             s5_lam_re[l], s5_lam_im[l], s5_log_dt[l],
                             s5_b_re[l], s5_b_im[l], s5_c_re[l], s5_c_im[l], s5_d[l])
    y_hy = _hyena(us_hy.reshape(bsz, seq, I_GA - I_HY),
                  hy_w1[l], hy_b1[l], hy_w2[l], hy_b2[l], hy_w3[l], hy_b3[l], hy_freq[l],
                  hy_w_out[l], hy_bias[l])

    xt = _merge(xt, mod[:, 1, 2], y_s5.reshape(bsz * seq, D_S5), y_hy.reshape(bsz * seq, D_HY),
                sig_gates, w_pa[l].astype(BF16), w_pb[l].astype(BF16), w_out[l].astype(BF16))

    xt = _ffn_sublayer(xt, mods(mod, 2), norm_g[l, 2], wg, wu, wd, 1, final_gain=final_g)
    return xt.reshape(bsz, seq, d)
```

```python
import functools
import math

import jax
import jax.numpy as jnp
import numpy as np
from jax import lax
from jax.experimental import pallas as pl
from jax.experimental.pallas import tpu as pltpu

F32 = jnp.float32
BF16 = jnp.bfloat16

D_MODEL = 2048
GRID_W = 64
D_S5 = 1024
S5_GROUP = 16
S5_GROUPS = D_S5 // S5_GROUP
S5_STATE = 64
S5_DIRS = 2
LAMBDA_RE_MAX = -1e-4
S5_CHUNK = 16
LANE = 128
S5_GROUPS_PER_COL = LANE // S5_GROUP
D_HY = 1024
HY_ORDER = 2
HY_DIRS = 2
HY_SHORT = 3
HY_EMB = 33
HY_BANDS = (HY_EMB - 1) // 2
HY_TARGET = 1e-2
HY_FAST_PCT = 0.3
HY_SLOW_PCT = 1.5
FFT_N1 = 64
FFT_N2 = 128
FFT_N = FFT_N1 * FFT_N2
FILTER_TILE_ROWS = FFT_N1 + 8
HY_TILE_PITCH = FILTER_TILE_ROWS
HY_PAIRS_PER_COL = LANE // 2
HY_PAIRS_PER_STEP = 16
I_HY = D_S5
I_GA = D_S5 + (HY_ORDER + 1) * D_HY
I_GB = I_GA + D_MODEL
D_IN = I_GB + D_MODEL
D_FF = 5632
N_SUB = 3
N_MOD = 3
HALF_STEP = 0.5
FFN_ROW_PARTS = 4
RMS_EPS = 1e-6

VMEM_LIMIT_BYTES = 58 * 1024 * 1024


def _rms_mod(x, gain, shift, scale):
    ms = jnp.mean(x * x, axis=-1, keepdims=True)
    y = x * lax.rsqrt(ms + RMS_EPS) * gain
    return y * (1.0 + scale) + shift


def _split_bf16(v):
    hi = v.astype(BF16)
    return hi, (v - hi.astype(F32)).astype(BF16)


def _ada_kernel(c_ref, w_ref, b_ref, o_ref):
    c = c_ref[...]
    a_hi, a_lo = _split_bf16(c * jax.nn.sigmoid(c))
    w_hi, w_lo = _split_bf16(w_ref[...])
    o_ref[...] = (jnp.dot(a_hi, w_hi, preferred_element_type=F32) + jnp.dot(a_lo, w_hi, preferred_element_type=F32)
                  + jnp.dot(a_hi, w_lo, preferred_element_type=F32) + b_ref[...])


def _ada_mod(c_rows, w, b, tn=1024):
    rows, d = c_rows.shape
    n = w.shape[1]
    return pl.pallas_call(
        _ada_kernel,
        grid=(n // tn,),
        in_specs=[pl.BlockSpec((rows, d), lambda j: (0, 0)),
                  pl.BlockSpec((d, tn), lambda j: (0, j)),
                  pl.BlockSpec((1, tn), lambda j: (0, j))],
        out_specs=pl.BlockSpec((rows, tn), lambda j: (0, j)),
        out_shape=jax.ShapeDtypeStruct((rows, n), F32),
        compiler_params=pltpu.CompilerParams(
            dimension_semantics=("arbitrary",), vmem_limit_bytes=VMEM_LIMIT_BYTES),
        name="ada_mod",
    )(c_rows, w, b.reshape(1, n))


def _ffn_kernel(x_ref, shift_ref, scale_ref, gate_ref, gain_ref, wg_ref, wu_ref, wd_ref,
                fg_ref, o_ref, h_ref, *, final_norm):
    j = pl.program_id(1)
    tm = x_ref.shape[0]

    def swiglu_down(h):
        g = jnp.dot(h, wg_ref[...].astype(BF16), preferred_element_type=F32)
        u = jnp.dot(h, wu_ref[...].astype(BF16), preferred_element_type=F32)
        a = (g * jax.nn.sigmoid(g) * u).astype(BF16)
        return jnp.dot(a, wd_ref[...].astype(BF16), preferred_element_type=F32)

    @pl.when(j == 0)
    def _():
        for r in range(0, tm, tm // FFN_ROW_PARTS):
            rows = slice(r, r + tm // FFN_ROW_PARTS)
            h = _rms_mod(x_ref[rows, :], gain_ref[...], shift_ref[...], scale_ref[...]).astype(BF16)
            h_ref[rows, :] = h
            o_ref[rows, :] = swiglu_down(h)

    last = pl.num_programs(1) - 1

    @pl.when((j > 0) & (j < last))
    def _():
        o_ref[...] += swiglu_down(h_ref[...])

    @pl.when(j == last)
    def _():
        for r in range(0, tm, tm // FFN_ROW_PARTS):
            rows = slice(r, r + tm // FFN_ROW_PARTS)
            acc = o_ref[rows, :] + swiglu_down(h_ref[rows, :])
            y = x_ref[rows, :] + (HALF_STEP * gate_ref[...]) * acc
            if final_norm:
                ms = jnp.mean(y * y, axis=-1, keepdims=True)
                y = y * lax.rsqrt(ms + RMS_EPS) * fg_ref[...]
            o_ref[rows, :] = y


def _ffn_sublayer(x, mods, gain, wg, wu, wd, which, final_gain=None, tm=1024, tf=256):
    t, d = x.shape
    bm = mods[0].shape[0]
    tm = min(tm, t // bm)
    blocks_per_batch = (t // bm) // tm
    dff = wg.shape[2]
    final_norm = final_gain is not None
    fg = final_gain if final_norm else gain
    mod_spec = pl.BlockSpec((None, 1, d), lambda i, j: (i // blocks_per_batch, 0, 0))
    vec_spec = pl.BlockSpec((1, d), lambda i, j: (0, 0))
    return pl.pallas_call(
        functools.partial(_ffn_kernel, final_norm=final_norm),
        grid=(t // tm, dff // tf),
        in_specs=[pl.BlockSpec((tm, d), lambda i, j: (i, 0)),
                  mod_spec, mod_spec, mod_spec, vec_spec,
                  pl.BlockSpec((None, d, tf), lambda i, j: (which, 0, j)),
                  pl.BlockSpec((None, d, tf), lambda i, j: (which, 0, j)),
                  pl.BlockSpec((None, tf, d), lambda i, j: (which, j, 0)),
                  vec_spec],
        out_specs=pl.BlockSpec((tm, d), lambda i, j: (i, 0)),
        out_shape=jax.ShapeDtypeStruct((t, d), F32),
        scratch_shapes=[pltpu.VMEM((tm, d), BF16)],
        compiler_params=pltpu.CompilerParams(
            dimension_semantics=("parallel", "arbitrary"), vmem_limit_bytes=VMEM_LIMIT_BYTES),
        name="ffn_final" if final_norm else "ffn",
    )(x, *mods, gain.reshape(1, d), wg, wu, wd, fg.reshape(1, d))


def _proj_kernel(x_ref, shift_ref, scale_ref, gain_ref, w_ref, sw_ref, sb_ref, *rest, n_u, n_hy, row_len, part):
    o_refs, h_ref = rest[:-1], rest[-1]
    j = pl.program_id(1)

    tm, tn = h_ref.shape[0], w_ref.shape[1]

    def first_block(o_ref):
        for r in range(0, tm, tm // 2):
            rows = slice(r, r + tm // 2)
            h = _rms_mod(x_ref[rows, :], gain_ref[...], shift_ref[...], scale_ref[...]).astype(BF16)
            h_ref[rows, :] = h
            for c in range(0, tn, part):
                p = jnp.dot(h, w_ref[:, c:c + part].astype(BF16), preferred_element_type=F32)
                o_ref[rows, c:c + part] = p.astype(o_ref.dtype)

    def in_parts(o_ref, epilogue):
        for c in range(0, tn, part):
            p = jnp.dot(h_ref[...], w_ref[:, c:c + part].astype(BF16), preferred_element_type=F32)
            o_ref[:, c:c + part] = epilogue(p, c).astype(o_ref.dtype)

    def short_conv(p, c):
        col = lax.broadcasted_iota(jnp.int32, p.shape, 0) % row_len
        prev = jnp.where(col == 0, 0.0, pltpu.roll(p, 1, 0))
        nxt = jnp.where(col == row_len - 1, 0.0, pltpu.roll(p, tm - 1, 0))
        sw = sw_ref[:, c:c + part]
        return sb_ref[:, c:c + part] + prev * sw[0:1] + p * sw[1:2] + nxt * sw[2:3]

    @pl.when(j == 0)
    def _():
        first_block(o_refs[0])

    if n_hy == 0:
        @pl.when(j > 0)
        def _():
            in_parts(o_refs[0], lambda p, c: p)
        return

    @pl.when((j > 0) & (j < n_u))
    def _():
        in_parts(o_refs[0], lambda p, c: p)

    @pl.when((j >= n_u) & (j < n_u + n_hy))
    def _():
        in_parts(o_refs[1], short_conv)

    @pl.when(j >= n_u + n_hy)
    def _():
        in_parts(o_refs[2], lambda p, c: jax.nn.sigmoid(p))


def _in_proj(x, shift, scale, gain, w, short_w=None, short_b=None, n_u=D_S5, n_hy=0, row_len=GRID_W,
             tm=1024, tn=512, part=256):
    t, d = x.shape
    bm = shift.shape[0]
    tm = min(tm, t // bm)
    blocks_per_batch = (t // bm) // tm
    n = w.shape[1]
    assert tm % row_len == 0 and (t // bm) % tm == 0
    bu, bh = n_u // tn, n_hy // tn
    bg = n // tn - bu - bh
    mod_spec = pl.BlockSpec((None, 1, d), lambda i, j: (i // blocks_per_batch, 0, 0))
    out_shape = [jax.ShapeDtypeStruct((t, n_u), F32)]
    out_specs = [pl.BlockSpec((tm, tn), lambda i, j: (i, jnp.minimum(j, bu - 1)))]
    if bh:
        out_shape += [jax.ShapeDtypeStruct((t, n_hy), BF16), jax.ShapeDtypeStruct((t, bg * tn), BF16)]
        out_specs += [pl.BlockSpec((tm, tn), lambda i, j: (i, jnp.clip(j - bu, 0, bh - 1))),
                      pl.BlockSpec((tm, tn), lambda i, j: (i, jnp.maximum(j - bu - bh, 0)))]
        sw, sb = short_w, short_b.reshape(1, n_hy)
        hy_blk = lambda i, j: (0, jnp.clip(j - bu, 0, bh - 1))
    else:
        sw, sb = jnp.zeros((HY_SHORT, tn), F32), jnp.zeros((1, tn), F32)
        hy_blk = lambda i, j: (0, 0)
    return pl.pallas_call(
        functools.partial(_proj_kernel, n_u=bu, n_hy=bh, row_len=row_len, part=part),
        grid=(t // tm, n // tn),
        in_specs=[pl.BlockSpec((tm, d), lambda i, j: (i, 0)),
                  mod_spec, mod_spec,
                  pl.BlockSpec((1, d), lambda i, j: (0, 0)),
                  pl.BlockSpec((d, tn), lambda i, j: (0, j)),
                  pl.BlockSpec((HY_SHORT, tn), hy_blk),
                  pl.BlockSpec((1, tn), hy_blk)],
        out_specs=out_specs,
        out_shape=out_shape,
        scratch_shapes=[pltpu.VMEM((tm, d), BF16)],
        compiler_params=pltpu.CompilerParams(
            dimension_semantics=("parallel", "arbitrary"), vmem_limit_bytes=VMEM_LIMIT_BYTES),
        name="in_proj",
    )(x, shift, scale, gain.reshape(1, d), w, sw, sb)


def _gelu_tanh(x):
    return 0.5 * x * (1.0 + jnp.tanh(math.sqrt(2.0 / math.pi) * (x + 0.044715 * (x * x * x))))


def _merge_kernel(x_ref, gate_ref, ys_ref, yh_ref, ga_ref, gb_ref, wpa_lo_ref, wpa_hi_ref, wpb_ref,
                  wout_ref, o_ref, s_ref, acc_ref):
    j = pl.program_id(1)

    @pl.when(j == 0)
    def _():
        s_ref[...] = _gelu_tanh(ys_ref[...].astype(F32)).astype(BF16)
        acc_ref[...] = jnp.zeros_like(acc_ref)

    s = s_ref[...]
    pa_lo = jnp.dot(s, wpa_lo_ref[...], preferred_element_type=F32)
    pa_hi = jnp.dot(s, wpa_hi_ref[...], preferred_element_type=F32)
    y_a = pa_lo * jax.nn.sigmoid(pa_hi)
    y_b = jnp.dot(yh_ref[...], wpb_ref[...], preferred_element_type=F32)
    m = ga_ref[...].astype(F32) * y_a + gb_ref[...].astype(F32) * y_b
    acc_ref[...] += jnp.dot(m.astype(BF16), wout_ref[...], preferred_element_type=F32)

    @pl.when(j == pl.num_programs(1) - 1)
    def _():
        o_ref[...] = x_ref[...] + gate_ref[...] * acc_ref[...]


def _merge(x, gate, y_s5, y_hy, sig_gates, w_pa, w_pb, w_out, tm=512, tn=1024):
    t, d = x.shape
    bm = gate.shape[0]
    blocks_per_batch = (t // bm) // tm
    nj = d // tn
    ds5 = y_s5.shape[1]
    dhy = y_hy.shape[1]
    return pl.pallas_call(
        _merge_kernel,
        grid=(t // tm, nj),
        in_specs=[pl.BlockSpec((tm, d), lambda i, j: (i, 0)),
                  pl.BlockSpec((None, 1, d), lambda i, j: (i // blocks_per_batch, 0, 0)),
                  pl.BlockSpec((tm, ds5), lambda i, j: (i, 0)),
                  pl.BlockSpec((tm, dhy), lambda i, j: (i, 0)),
                  pl.BlockSpec((tm, tn), lambda i, j: (i, j)),
                  pl.BlockSpec((tm, tn), lambda i, j: (i, nj + j)),
                  pl.BlockSpec((ds5, tn), lambda i, j: (0, j)),
                  pl.BlockSpec((ds5, tn), lambda i, j: (0, nj + j)),
                  pl.BlockSpec((dhy, tn), lambda i, j: (0, j)),
                  pl.BlockSpec((tn, d), lambda i, j: (j, 0))],
        out_specs=pl.BlockSpec((tm, d), lambda i, j: (i, 0)),
        out_shape=jax.ShapeDtypeStruct((t, d), F32),
        scratch_shapes=[pltpu.VMEM((tm, ds5), BF16), pltpu.VMEM((tm, d), F32)],
        compiler_params=pltpu.CompilerParams(
            dimension_semantics=("parallel", "arbitrary"), vmem_limit_bytes=VMEM_LIMIT_BYTES),
        name="merge",
    )(x, gate, y_s5, y_hy, sig_gates, sig_gates, w_pa, w_pa, w_pb, w_out)


def _s5_weights_kernel(*refs, n_steps):
    blocks, ca_ref = refs[:-1], refs[-1]

    def one(gi, carry):
        _s5_weights_group(*[r.at[gi] for r in blocks], ca_ref, n_steps=n_steps)
        return carry

    lax.fori_loop(0, blocks[0].shape[0], one, 0)


def _s5_weights_group(par_ref, bt_ref, c_ref, d_ref, bpow_ref, toep_ref, cpow_ref, tab_ref, ca_ref, *, n_steps):
    t, h, p = S5_CHUNK, S5_GROUP, S5_STATE
    lanes = 2 * p
    hp = lax.Precision.HIGHEST
    sgn = jnp.where(lax.broadcasted_iota(jnp.int32, (1, lanes), 1) < p, -1.0, 1.0)
    par = par_ref[...]
    gsum = None
    for d in range(S5_DIRS):
        lr = jnp.minimum(par[3 * d:3 * d + 1], LAMBDA_RE_MAX)
        li = par[3 * d + 1:3 * d + 2]
        dt = jnp.exp(par[3 * d + 2:3 * d + 3])
        zr, zi = lr * dt, li * dt

        def apow(j):
            mag = jnp.exp(j * zr)
            return mag * jnp.cos(j * zi), sgn * (mag * jnp.sin(j * zi))

        def cmul(x, a1, a2):
            return x * a1 + pltpu.roll(x, p, 1) * a2

        a1, a2 = apow(lax.broadcasted_iota(jnp.int32, (t + 1, 1), 0).astype(F32))
        nr, ni = a1[1:2] - 1.0, sgn * a2[1:2]
        den = lr * lr + li * li
        f_re = (nr * lr + ni * li) / den
        f_im = (ni * lr - nr * li) / den
        bbar = cmul(bt_ref[d], f_re, sgn * f_im)
        cc = c_ref[d]
        ca = [cmul(cc, a1[j:j + 1], a2[j:j + 1]) * (-sgn) for j in range(t + 1)]
        ca_ref[...] = jnp.zeros_like(ca_ref)
        for k in range(t):
            e_b, e_c = (t - 1 - k, k + 1) if d == 0 else (k, t - k)
            bpow_ref[k * h:(k + 1) * h, d * lanes:(d + 1) * lanes] = (
                cmul(bbar, a1[e_b:e_b + 1], a2[e_b:e_b + 1]).astype(BF16))
            cpow_ref[k * h:(k + 1) * h, d * lanes:(d + 1) * lanes] = ca[e_c].astype(BF16)
            l = t - 1 + k if d == 0 else t - 1 - k
            ca_ref[l * h:(l + 1) * h, :] = ca[k]
        g = lax.dot_general(bbar, ca_ref[...], (((1,), (1,)), ((), ())), preferred_element_type=F32, precision=hp)
        gsum = g if gsum is None else gsum + g
        for s in range(n_steps):
            s1, s2 = apow(float(t * 2 ** s))
            r = d * 2 * n_steps + 2 * s
            tab_ref[r:r + 1, :] = s1
            tab_ref[r + 1:r + 2, :] = s2
    wide = gsum.shape[1]
    col = lax.broadcasted_iota(jnp.int32, (h, wide), 1)
    row = lax.broadcasted_iota(jnp.int32, (h, wide), 0)
    gsum = gsum + jnp.where(col - (t - 1) * h == row, d_ref[...], 0.0)
    for k in range(t):
        off = (t - 1 - k) * h
        shifted = gsum if off == 0 else pltpu.roll(gsum, wide - off, 1)
        toep_ref[k * h:(k + 1) * h, :] = shifted[:, :t * h].astype(BF16)


def _s5_weights(lam_re, lam_im, log_dt, b_re, b_im, c_re, c_im, d_skip, n_steps):
    g, p, h, t = S5_GROUPS, S5_STATE, S5_GROUP, S5_CHUNK
    cat2 = lambda a: jnp.concatenate([a, a], axis=-1)
    par = jnp.stack([cat2(lam_re), cat2(lam_im), jnp.broadcast_to(log_dt[..., None], (S5_DIRS, g, 2 * p))], axis=1)
    par = par.transpose(2, 0, 1, 3).reshape(g, 3 * S5_DIRS, 2 * p)
    btc = jnp.concatenate([b_re, b_im], axis=2).transpose(1, 0, 3, 2)
    ccat = jnp.concatenate([c_re, c_im], axis=3).transpose(1, 0, 2, 3)
    wide = 2 * t * h
    drow = jnp.zeros((g, 1, wide), F32).at[:, 0, (t - 1) * h:t * h].set(d_skip.reshape(g, h))
    w = t * h
    gb = S5_GROUPS_PER_COL
    sq = pl.BlockSpec((gb, w, w), lambda i: (i, 0, 0))
    return pl.pallas_call(
        functools.partial(_s5_weights_kernel, n_steps=n_steps),
        grid=(g // gb,),
        in_specs=[pl.BlockSpec((gb, 3 * S5_DIRS, 2 * p), lambda i: (i, 0, 0)),
                  pl.BlockSpec((gb, S5_DIRS, h, 2 * p), lambda i: (i, 0, 0, 0)),
                  pl.BlockSpec((gb, S5_DIRS, h, 2 * p), lambda i: (i, 0, 0, 0)),
                  pl.BlockSpec((gb, 1, wide), lambda i: (i, 0, 0))],
        out_specs=[sq, sq, sq, pl.BlockSpec((gb, 4 * n_steps, 2 * p), lambda i: (i, 0, 0))],
        out_shape=[jax.ShapeDtypeStruct((g, w, w), BF16)] * 3 + [jax.ShapeDtypeStruct((g, 4 * n_steps, 2 * p), F32)],
        scratch_shapes=[pltpu.VMEM((wide, 2 * p), F32)],
        compiler_params=pltpu.CompilerParams(
            dimension_semantics=("parallel",), vmem_limit_bytes=VMEM_LIMIT_BYTES),
        name="s5_weights",
    )(par, btc, ccat, drow)


def _s5_kernel(u_ref, uc_ref, bpow_ref, toep_ref, cpow_ref, tab_ref, y_ref, ucat_ref, ucc_ref, yacc_ref,
               *, bsz, n_steps):
    t, gw = S5_CHUNK, S5_GROUP
    per = LANE // gw
    n_lat = u_ref.shape[0] // (bsz * t)
    n_ctx = uc_ref.shape[0] // (bsz * t)
    n_ch = n_lat + n_ctx
    rows = bsz * n_ch
    half = 2 * S5_STATE
    for k in range(t):
        ucat_ref[:, k * LANE:(k + 1) * LANE] = u_ref[pl.ds(k, bsz * n_lat, stride=t), :]
        ucc_ref[:, k * LANE:(k + 1) * LANE] = uc_ref[pl.ds(k, bsz * n_ctx, stride=t), :]
    yacc_ref[...] = jnp.zeros_like(yacc_ref)
    rib = lax.broadcasted_iota(jnp.int32, (rows, half), 0) % n_ch
    lane = lax.broadcasted_iota(jnp.int32, (1, LANE), 1)

    def cmul_add(acc, sh, a1, a2):
        return acc + a1 * sh + a2 * pltpu.roll(sh, S5_STATE, 1)

    def gather(src_ref, gl):
        cols = []
        for j in range(t // per):
            acc = None
            for tt in range(per):
                k = j * per + tt
                r = pltpu.roll(src_ref[:, k * LANE:(k + 1) * LANE], (gw * tt - gw * gl) % LANE, 1)
                acc = r if acc is None else jnp.where((lane >= gw * tt) & (lane < gw * (tt + 1)), r, acc)
            cols.append(acc)
        return jnp.concatenate(cols, axis=1).astype(BF16)

    def group(gl, carry):
        ul = gather(ucat_ref, gl)
        uc = gather(ucc_ref, gl)
        bpow = bpow_ref[gl]
        zl = jnp.dot(ul, bpow, preferred_element_type=F32)
        zc = jnp.dot(uc, bpow, preferred_element_type=F32)
        fparts, bparts = [], []
        for b in range(bsz):
            lat = slice(b * n_lat, (b + 1) * n_lat)
            ctx = slice(b * n_ctx, (b + 1) * n_ctx)
            fparts += [zc[ctx, :half], zl[lat, :half]]
            bparts += [zl[lat, half:], zc[ctx, half:]]
        fw = jnp.concatenate(fparts, axis=0)
        bw = jnp.concatenate(bparts, axis=0)
        tab = tab_ref[gl]
        for s in range(n_steps):
            d = 1 << s
            sh = jnp.where(rib >= d, pltpu.roll(fw, d, 0), 0.0)
            fw = cmul_add(fw, sh, tab[2 * s:2 * s + 1], tab[2 * s + 1:2 * s + 2])
            o = 2 * n_steps
            sh = jnp.where(rib < n_ch - d, pltpu.roll(bw, rows - d, 0), 0.0)
            bw = cmul_add(bw, sh, tab[o + 2 * s:o + 2 * s + 1], tab[o + 2 * s + 1:o + 2 * s + 2])
        fe = jnp.where(rib >= 1, pltpu.roll(fw, 1, 0), 0.0)
        be = jnp.where(rib < n_ch - 1, pltpu.roll(bw, rows - 1, 0), 0.0)
        fl = jnp.concatenate([fe[b * n_ch + n_ctx:(b + 1) * n_ch] for b in range(bsz)], axis=0)
        bl = jnp.concatenate([be[b * n_ch:b * n_ch + n_lat] for b in range(bsz)], axis=0)
        st = jnp.concatenate([fl, bl], axis=1).astype(BF16)
        y = (jnp.dot(ul, toep_ref[gl], preferred_element_type=F32)
             + lax.dot_general(st, cpow_ref[gl], (((1,), (1,)), ((), ())), preferred_element_type=F32))
        mine = (lane >= gw * gl) & (lane < gw * (gl + 1))
        for i in range(t):
            src = y[:, (i // per) * LANE:(i // per + 1) * LANE]
            r = pltpu.roll(src, (gw * gl - gw * (i % per)) % LANE, 1)
            blk = slice(i * LANE, (i + 1) * LANE)
            yacc_ref[:, blk] = jnp.where(mine, r, yacc_ref[:, blk])
        return carry

    lax.fori_loop(0, S5_GROUPS_PER_COL, group, 0)
    for i in range(t):
        y_ref[pl.ds(i, bsz * n_lat, stride=t), :] = yacc_ref[:, i * LANE:(i + 1) * LANE]


def _s5_mix(u, uc, bpw, toep, cpw, tab, bsz):
    rl, dm = u.shape
    rc = uc.shape[0]
    t = S5_CHUNK
    w = t * S5_GROUP
    n_steps = tab.shape[1] // 4
    gpc = S5_GROUPS_PER_COL
    wspec = pl.BlockSpec((gpc, w, w), lambda i: (i, 0, 0))
    return pl.pallas_call(
        functools.partial(_s5_kernel, bsz=bsz, n_steps=n_steps),
        grid=(dm // LANE,),
        in_specs=[pl.BlockSpec((rl, LANE), lambda i: (0, i)),
                  pl.BlockSpec((rc, LANE), lambda i: (0, i)),
                  wspec, wspec, wspec,
                  pl.BlockSpec((gpc, 4 * n_steps, tab.shape[2]), lambda i: (i, 0, 0))],
        out_specs=pl.BlockSpec((rl, LANE), lambda i: (0, i)),
        out_shape=jax.ShapeDtypeStruct(u.shape, F32),
        scratch_shapes=[pltpu.VMEM((rl // t, t * LANE), F32), pltpu.VMEM((rc // t, t * LANE), F32),
                        pltpu.VMEM((rl // t, t * LANE), F32)],
        compiler_params=pltpu.CompilerParams(
            dimension_semantics=("parallel",), vmem_limit_bytes=VMEM_LIMIT_BYTES),
        name="s5_mix",
    )(u, uc, bpw, toep, cpw, tab)


def _s5_bidirectional(u, u_ctx, lam_re, lam_im, log_dt, b_re, b_im, c_re, c_im, d_skip):
    bsz, length, dm = u.shape
    ctx_len = u_ctx.shape[1]
    t = S5_CHUNK
    n_lat, n_ctx = length // t, ctx_len // t
    n_steps = max(1, math.ceil(math.log2(n_lat + n_ctx)))
    bpw, toep, cpw, tab = _s5_weights(lam_re, lam_im, log_dt, b_re, b_im, c_re, c_im, d_skip, n_steps)
    y = _s5_mix(u.reshape(bsz * length, dm), u_ctx.reshape(bsz * ctx_len, dm), bpw, toep, cpw, tab, bsz)
    return y.reshape(bsz, length, dm)


def _dft_constants(real_input=False):
    n1 = np.arange(FFT_N1)
    n2 = np.arange(FFT_N2)
    half = FFT_N1 // 2
    th = 2 * np.pi * np.outer(n1, n1) / FFT_N1
    c1, s1 = np.cos(th), np.sin(th)
    if real_input:
        w1 = np.concatenate([c1, -s1], axis=1)
    else:
        w1 = np.concatenate([np.concatenate([c1[:half], -s1[:half]], axis=1),
                             np.concatenate([s1[:half], c1[:half]], axis=1)], axis=0)
    z = np.zeros_like(w1)
    w1p = np.block([[w1, z], [z, w1]])
    ph = 2 * np.pi * np.outer(n2, n1) / FFT_N
    ps = 2 * np.pi * np.outer(n2, n2) / FFT_N2
    f2 = np.concatenate([np.cos(ps), -np.sin(ps)], axis=1)
    g2 = np.concatenate([np.cos(ps), np.sin(ps)], axis=1)
    c2, s2 = np.cos(ph).T, np.sin(ph).T
    wi = np.concatenate([np.concatenate([c1[:, :half], s1[:, :half]], axis=1),
                         np.concatenate([-s1[:, :half], c1[:, :half]], axis=1)], axis=0) / FFT_N
    zi = np.zeros_like(wi)
    wi2 = np.stack([np.concatenate([wi, zi], axis=1), np.concatenate([zi, wi], axis=1)])
    as_b = lambda a: jnp.asarray(a, F32).astype(BF16)
    as_f = lambda a: jnp.asarray(a, F32)
    return [as_b(w1p), as_f(c2), as_f(s2), as_b(f2), as_b(g2), as_b(wi2)]


def _fwd_spectrum(xp, w1p, c2, s2, f2):
    cp = xp.shape[0]
    hn = FFT_N1
    a = jnp.dot(xp.reshape(cp * FFT_N2, LANE).astype(BF16), w1p, preferred_element_type=F32)
    out = []
    for par in range(2):
        ap = a[:, par * LANE:(par + 1) * LANE].reshape(cp, FFT_N2, LANE)
        at = jnp.swapaxes(ap, 1, 2)
        are, aim = at[:, :hn], at[:, hn:]
        at = jnp.concatenate([are * c2 + aim * s2, aim * c2 - are * s2], axis=1)
        p = jnp.dot(at.reshape(cp * 2 * hn, FFT_N2).astype(BF16), f2, preferred_element_type=F32)
        p = p.reshape(cp, 2 * hn, 2 * FFT_N2)
        out.append((p[:, :hn, :FFT_N2] - p[:, hn:, FFT_N2:], p[:, :hn, FFT_N2:] + p[:, hn:, :FFT_N2]))
    return out


def _inv_time(yre, yim, g2, c2, s2, wi_par):
    cp = yre.shape[0]
    hn = FFT_N1
    y = jnp.concatenate([yre, yim], axis=1).reshape(cp * 2 * hn, FFT_N2).astype(BF16)
    q = jnp.dot(y, g2, preferred_element_type=F32).reshape(cp, 2 * hn, 2 * FFT_N2)
    bre = q[:, :hn, :FFT_N2] - q[:, hn:, FFT_N2:]
    bim = q[:, :hn, FFT_N2:] + q[:, hn:, :FFT_N2]
    b2 = jnp.concatenate([bre * c2 - bim * s2, bre * s2 + bim * c2], axis=1)
    bt = jnp.swapaxes(b2, 1, 2)
    return jnp.dot(bt.reshape(cp * FFT_N2, LANE).astype(BF16), wi_par, preferred_element_type=F32)


def _hyena_kernel(z_ref, g1_ref, g2_ref, kf_ref, bias_ref, w1p_ref, c2_ref, s2_ref, f2_ref, gi_ref, wi_ref,
                  o_ref, zt_ref, g1t_ref, g2t_ref, ot_ref, stage_ref):
    k = pl.program_id(1)
    n_s = z_ref.shape[0] // FFT_N2
    cp = HY_PAIRS_PER_STEP

    def to_tiles(x_ref, t_ref):
        for s in range(n_s):
            xs = x_ref[s * FFT_N2:(s + 1) * FFT_N2, :].astype(F32)
            stage_ref[pl.ds(s, LANE, stride=HY_TILE_PITCH), :] = xs.T
        st = stage_ref[...].reshape(HY_PAIRS_PER_COL, 2 * HY_TILE_PITCH, FFT_N2)
        both = jnp.concatenate([st[:, :n_s], st[:, HY_TILE_PITCH:HY_TILE_PITCH + n_s]], axis=1)
        t_ref[...] = jnp.swapaxes(both, 1, 2)

    @pl.when(k == 0)
    def _():
        to_tiles(z_ref, zt_ref)
        to_tiles(g1_ref, g1t_ref)
        to_tiles(g2_ref, g2t_ref)

    sl = pl.ds(pl.multiple_of(k * cp, cp), cp)
    z = zt_ref[sl]
    gates = (g1t_ref, g2t_ref)
    for o in range(HY_ORDER):
        spec = _fwd_spectrum(z, w1p_ref[...], c2_ref[...], s2_ref[...], f2_ref[...])
        conv = None
        for par in range(2):
            xre, xim = spec[par]
            kre = kf_ref[o, par, :, :FFT_N1, :]
            kim = kf_ref[o, par, :, FFT_N1:, :]
            part = _inv_time(xre * kre - xim * kim, xre * kim + xim * kre,
                             gi_ref[...], c2_ref[...], s2_ref[...], wi_ref[par])
            conv = part if conv is None else conv + part
        z = gates[o][sl] * (conv.reshape(cp, FFT_N2, LANE) + bias_ref[o] * z)
    ot_ref[sl] = z

    @pl.when(k == pl.num_programs(1) - 1)
    def _():
        back = jnp.swapaxes(ot_ref[...], 1, 2)
        for c in range(LANE):
            stage_ref[c * HY_TILE_PITCH:c * HY_TILE_PITCH + n_s, :] = back[c // 2, (c % 2) * n_s:(c % 2 + 1) * n_s, :]
        for s in range(n_s):
            rows = stage_ref[pl.ds(s, LANE, stride=HY_TILE_PITCH), :]
            o_ref[s * FFT_N2:(s + 1) * FFT_N2, :] = rows.T.astype(o_ref.dtype)


def _hyena_conv(us, kf, biasp):
    t, _ = us.shape
    consts = _dft_constants()
    ncol = D_HY // LANE
    nsub = HY_PAIRS_PER_COL // HY_PAIRS_PER_STEP
    full = lambda a: pl.BlockSpec(a.shape, lambda j, k: (0,) * a.ndim)
    nat = lambda off: pl.BlockSpec((t, LANE), lambda j, k: (0, off + j))
    tiles = pltpu.VMEM((HY_PAIRS_PER_COL, FFT_N2, LANE), F32)
    return pl.pallas_call(
        _hyena_kernel,
        grid=(ncol, nsub),
        in_specs=[nat(0), nat(ncol), nat(2 * ncol),
                  pl.BlockSpec((HY_ORDER, 2, HY_PAIRS_PER_STEP, 2 * FFT_N1, FFT_N2),
                               lambda j, k: (0, 0, j * nsub + k, 0, 0)),
                  pl.BlockSpec((HY_ORDER, HY_PAIRS_PER_STEP, 1, LANE), lambda j, k: (0, j * nsub + k, 0, 0))]
                 + [full(a) for a in consts],
        out_specs=pl.BlockSpec((t, LANE), lambda j, k: (0, j)),
        out_shape=jax.ShapeDtypeStruct((t, D_HY), us.dtype),
        scratch_shapes=[tiles, tiles, tiles, tiles, pltpu.VMEM((LANE * HY_TILE_PITCH, FFT_N2), F32)],
        compiler_params=pltpu.CompilerParams(
            dimension_semantics=("parallel", "arbitrary"), vmem_limit_bytes=VMEM_LIMIT_BYTES),
        name="hyena_conv",
    )(us, us, us, kf, biasp, *consts)


def _filter_time_kernel(w1t_ref, w1c_ref, w1s_ref, b1_ref, w2_ref, b2_ref, w3_ref, b3_ref, fr_ref,
                        wf_ref, wb_ref, df_ref, db_ref, o_ref, h_ref, k_ref, *, length):
    n_fft = 2 * length
    hp = lax.Precision.HIGHEST

    @pl.when(pl.program_id(0) == 0)
    def _():
        pos = lax.broadcasted_iota(jnp.int32, (1, n_fft), 1)
        lag = jnp.where(pos < length, pos, n_fft - pos).astype(F32)
        t = lag / float(length - 1)
        w = (2.0 * math.pi / length) * lag
        band_step = (HY_BANDS - 1 - 1e-4) / (HY_BANDS - 1)
        bands = 1e-4 + band_step * lax.broadcasted_iota(jnp.int32, (HY_BANDS, 1), 0).astype(F32)
        ang = bands * w
        fr = fr_ref[...]
        h = (w1t_ref[...] * t + jnp.dot(w1c_ref[...], jnp.cos(ang), preferred_element_type=F32, precision=hp)
             - jnp.dot(w1s_ref[...], jnp.sin(ang), preferred_element_type=F32, precision=hp))
        h = jnp.sin(fr * (h + b1_ref[...]))
        h = jnp.sin(fr * (jnp.dot(w2_ref[...], h, preferred_element_type=F32, precision=hp) + b2_ref[...]))
        h = jnp.sin(fr * (jnp.dot(w3_ref[...], h, preferred_element_type=F32, precision=hp) + b3_ref[...]))
        hi = h.astype(BF16)
        h_ref[0] = hi
        h_ref[1] = (h - hi.astype(F32)).astype(BF16)

    def dot3(w, lo, hi_):
        w_hi = w.astype(BF16)
        w_lo = (w - w_hi.astype(F32)).astype(BF16)
        h_hi, h_lo = h_ref[0, :, lo:hi_], h_ref[1, :, lo:hi_]
        return (jnp.dot(w_hi, h_hi, preferred_element_type=F32) + jnp.dot(w_hi, h_lo, preferred_element_type=F32)
                + jnp.dot(w_lo, h_hi, preferred_element_type=F32))

    pos = lax.broadcasted_iota(jnp.int32, (1, length), 1)
    tf = pos.astype(F32) / float(length - 1)
    tb = (length - pos).astype(F32) / float(length - 1)
    kf = dot3(wf_ref[...], 0, length) * jnp.exp(-tf * df_ref[...])
    kb = dot3(wb_ref[...], length, n_fft) * jnp.exp(-tb * db_ref[...])
    kb = jnp.where(pos == 0, 0.0, kb)
    inv = 1.0 / (jnp.sum(jnp.abs(kf), axis=1, keepdims=True) + jnp.sum(jnp.abs(kb), axis=1, keepdims=True))
    k_ref[:, :length] = kf * inv
    k_ref[:, length:] = kb * inv
    cb = wf_ref.shape[0]
    for n1 in range(FILTER_TILE_ROWS):
        row = k_ref[:, n1 * FFT_N2:(n1 + 1) * FFT_N2] if n1 < FFT_N1 else jnp.zeros((cb, FFT_N2), F32)
        o_ref[pl.ds(n1, cb, stride=FILTER_TILE_ROWS), :] = row


def _filter_time(length, w1, b1, w2, b2, w3, b3, freq, w_out, cb=128):
    col = lambda v: v.reshape(-1, 1)
    w1t = w1.T
    n_ch = w_out.shape[1]
    deltas = jnp.abs(jnp.linspace(math.log(HY_TARGET) / HY_SLOW_PCT, math.log(HY_TARGET) / HY_FAST_PCT,
                                  n_ch, dtype=F32)).reshape(n_ch, 1)
    wot = w_out.T
    nb = D_HY // cb
    small = lambda a: pl.BlockSpec(a.shape, lambda i: (0,) * a.ndim)
    fwd = lambda i: ((i // nb) * HY_DIRS * nb + i % nb, 0)
    bwd = lambda i: ((i // nb) * HY_DIRS * nb + nb + i % nb, 0)
    ins = [w1t[:, 0:1], w1t[:, 1:1 + HY_BANDS], w1t[:, 1 + HY_BANDS:], col(b1), w2.T, col(b2), w3.T, col(b3),
           col(freq)]
    hy_ff = w2.shape[0]
    return pl.pallas_call(
        functools.partial(_filter_time_kernel, length=length),
        grid=(HY_ORDER * nb,),
        in_specs=[small(a) for a in ins] + [pl.BlockSpec((cb, hy_ff), fwd), pl.BlockSpec((cb, hy_ff), bwd),
                                            pl.BlockSpec((cb, 1), fwd), pl.BlockSpec((cb, 1), bwd)],
        out_specs=pl.BlockSpec((cb * FILTER_TILE_ROWS, FFT_N2), lambda i: (i, 0)),
        out_shape=jax.ShapeDtypeStruct((HY_ORDER * D_HY * FILTER_TILE_ROWS, FFT_N2), F32),
        scratch_shapes=[pltpu.VMEM((2, hy_ff, 2 * length), BF16), pltpu.VMEM((cb, 2 * length), F32)],
        compiler_params=pltpu.CompilerParams(
            dimension_semantics=("arbitrary",), vmem_limit_bytes=VMEM_LIMIT_BYTES),
        name="hyena_filter_time",
    )(*ins, wot, wot, deltas, deltas)


def _filter_spec_kernel(k_ref, w1p_ref, c2_ref, s2_ref, f2_ref, o_ref):
    cb = k_ref.shape[0] // FILTER_TILE_ROWS
    kt = k_ref[...].reshape(cb, FILTER_TILE_ROWS, FFT_N2)[:, :FFT_N1, :]
    xp = jnp.swapaxes(kt.reshape(cb // 2, 2 * FFT_N1, FFT_N2), 1, 2)
    spec = _fwd_spectrum(xp, w1p_ref[...], c2_ref[...], s2_ref[...], f2_ref[...])
    for par in range(2):
        o_ref[par, :, :FFT_N1, :] = spec[par][0]
        o_ref[par, :, FFT_N1:, :] = spec[par][1]


def _filter_spectrum(kt, cb=128):
    consts = _dft_constants(real_input=True)[:4]
    nb = D_HY // cb
    full = lambda a: pl.BlockSpec(a.shape, lambda i: (0,) * a.ndim)
    return pl.pallas_call(
        _filter_spec_kernel,
        grid=(HY_ORDER * nb,),
        in_specs=[pl.BlockSpec((cb * FILTER_TILE_ROWS, FFT_N2), lambda i: (i, 0))] + [full(a) for a in consts],
        out_specs=pl.BlockSpec((None, 2, cb // 2, 2 * FFT_N1, FFT_N2), lambda i: (i // nb, 0, i % nb, 0, 0)),
        out_shape=jax.ShapeDtypeStruct((HY_ORDER, 2, D_HY // 2, 2 * FFT_N1, FFT_N2), F32),
        compiler_params=pltpu.CompilerParams(
            dimension_semantics=("parallel",), vmem_limit_bytes=VMEM_LIMIT_BYTES),
        name="hyena_filter_spectrum",
    )(kt, *consts)


def _hyena(us, w1, b1, w2, b2, w3, b3, freq, w_out, bias):
    bsz, length, _ = us.shape
    assert 2 * length == FFT_N and bsz == 2, "one complex transform carries exactly two batch rows"
    kf = _filter_spectrum(_filter_time(length, w1, b1, w2, b2, w3, b3, freq, w_out))
    biasp = jnp.repeat(bias.reshape(HY_ORDER, D_HY // 2, 1, 2), FFT_N1, axis=-1)
    return _hyena_conv(us.reshape(bsz * length, -1), kf, biasp).reshape(bsz, length, D_HY)


def kernel(x, c, ctx, c_ctx, w_ada, b_ada, norm_g, ffn_w_gate, ffn_w_up, ffn_w_down, w_in,
           s5_lam_re, s5_lam_im, s5_log_dt, s5_b_re, s5_b_im, s5_c_re, s5_c_im, s5_d,
           hy_short_w, hy_short_b, hy_w1, hy_b1, hy_w2, hy_b2, hy_w3, hy_b3, hy_freq, hy_w_out,
           hy_bias, w_pa, w_pb, w_out, final_g):
    bsz, seq, d = x.shape
    ctx_len = ctx.shape[1]
    n_rows = seq // GRID_W
    depth = w_ada.shape[0]
    assert depth == 1, "context-token outputs are only dropped by the last layer"
    l = 0

    c_rows = jnp.concatenate([c, c_ctx[None, :], jnp.zeros((8 - bsz - 1, d), F32)], axis=0)
    mod_all = _ada_mod(c_rows, w_ada[l], b_ada[l])
    mod = mod_all[:bsz].reshape(bsz, N_SUB, N_MOD, 1, d)
    mod_c = mod_all[bsz:bsz + 1].reshape(1, N_SUB, N_MOD, 1, d)

    def mods(m, sub):
        return tuple(m[:, sub, k] for k in range(N_MOD))

    wg, wu, wd = ffn_w_gate[l], ffn_w_up[l], ffn_w_down[l]
    w_in_b = w_in[l]

    xt = x.reshape(bsz * seq, d)
    ct = ctx.reshape(bsz * ctx_len, d)

    xt = _ffn_sublayer(xt, mods(mod, 0), norm_g[l, 0], wg, wu, wd, 0)
    ct = _ffn_sublayer(ct, mods(mod_c, 0), norm_g[l, 0], wg, wu, wd, 0)

    assert GRID_W * n_rows == seq
    u_s5, us_hy, sig_gates = _in_proj(xt, mod[:, 1, 0], mod[:, 1, 1], norm_g[l, 1], w_in_b,
                                      hy_short_w[l], hy_short_b[l], n_u=I_HY, n_hy=I_GA - I_HY)
    (u_ctx,) = _in_proj(ct, mod_c[:, 1, 0], mod_c[:, 1, 1], norm_g[l, 1], w_in_b[:, :D_S5])

    y_s5 = _s5_bidirectional(u_s5.reshape(bsz, seq, D_S5), u_ctx.reshape(bsz, ctx_len, D_S5),
                             s5_lam_re[l], s5_lam_im[l], s5_log_dt[l],
                             s5_b_re[l], s5_b_im[l], s5_c_re[l], s5_c_im[l], s5_d[l])
    y_hy = _hyena(us_hy.reshape(bsz, seq, I_GA - I_HY),
                  hy_w1[l], hy_b1[l], hy_w2[l], hy_b2[l], hy_w3[l], hy_b3[l], hy_freq[l],
                  hy_w_out[l], hy_bias[l])

    xt = _merge(xt, mod[:, 1, 2], y_s5.reshape(bsz * seq, D_S5), y_hy.reshape(bsz * seq, D_HY),
                sig_gates, w_pa[l].astype(BF16), w_pb[l].astype(BF16), w_out[l].astype(BF16))

    xt = _ffn_sublayer(xt, mods(mod, 2), norm_g[l, 2], wg, wu, wd, 1, final_gain=final_g)
    return xt.reshape(bsz, seq, d)
```
